```python
import math
import jax, jax.numpy as jnp
from jax import lax
import numpy as np

D_MODEL = 2048
BATCH = 8
SEQ = 4096
DEPTH = 1

MEM_LEN = 256

POOL_WIDTH = D_MODEL // 2
POOL_WINDOWS = (2, 4, 8, 16)
POOL_GROUPS = len(POOL_WINDOWS)
POOL_GROUP_DIM = POOL_WIDTH // POOL_GROUPS

SGU_WIDTH = D_MODEL // 2
SGU_CHUNK = 128
SGU_HEADS = 8
SGU_HEAD_DIM = SGU_WIDTH // SGU_HEADS

XATTN_HEADS = 4
XATTN_HEAD_DIM = D_MODEL // 8
XATTN_WIDTH = XATTN_HEADS * XATTN_HEAD_DIM

BRANCH_WIDTHS = (POOL_WIDTH, SGU_WIDTH, XATTN_WIDTH)
MIX_WIDTH = POOL_WIDTH + SGU_WIDTH + XATTN_WIDTH
IN_SPLITS = (POOL_WIDTH, POOL_WIDTH, SGU_WIDTH, SGU_WIDTH, SGU_WIDTH, XATTN_WIDTH, XATTN_WIDTH)
IN_WIDTH = sum(IN_SPLITS)
EPS = 1e-6

kernel_name = "hybrid_pool_sgu_memxattn_layer"


def rmsnorm(x, g):
    xf = x.astype(jnp.float32)
    y = xf * lax.rsqrt(jnp.mean(xf * xf, axis=-1, keepdims=True) + EPS)
    return (y * g.astype(jnp.float32)).astype(x.dtype)


def layernorm(x, g, b):
    xf = x.astype(jnp.float32)
    mu = jnp.mean(xf, axis=-1, keepdims=True)
    xc = xf - mu
    y = xc * lax.rsqrt(jnp.mean(xc * xc, axis=-1, keepdims=True) + EPS)
    return (y * g.astype(jnp.float32) + b.astype(jnp.float32)).astype(x.dtype)


def split_cols(a, sizes):
    idx = list(np.cumsum(sizes)[:-1])
    return jnp.split(a, idx, axis=-1)


def pool_mixer(xa, w_pool, scale):
    B, S, _ = xa.shape
    xg = xa.reshape(B, S, POOL_GROUPS, POOL_GROUP_DIM).astype(jnp.float32)
    csum = jnp.cumsum(xg, axis=1)
    t = jnp.arange(1, S + 1, dtype=jnp.float32)
    outs = []
    for g, w in enumerate(POOL_WINDOWS):
        cg = csum[:, :, g]
        lower = jnp.pad(cg[:, :S - w], ((0, 0), (w, 0), (0, 0)))
        count = jnp.minimum(t, float(w))[None, :, None]
        outs.append((cg - lower) / count - xg[:, :, g])
    d = jnp.stack(outs, axis=2).astype(xa.dtype)
    y = jnp.einsum('bsgc,gcd->bsgd', d, w_pool)
    return y.reshape(B, S, POOL_WIDTH) * scale


def spatial_gating(u, v, ln_g, ln_b, w_s, b_s):
    B, S, _ = v.shape
    n_chunks = S // SGU_CHUNK
    vn = layernorm(v, ln_g, ln_b)
    vc = vn.reshape(B, n_chunks, SGU_CHUNK, SGU_HEADS, SGU_HEAD_DIM)
    causal = jnp.tril(jnp.ones((SGU_CHUNK, SGU_CHUNK), dtype=bool))
    w = jnp.where(causal[None], w_s, jnp.zeros_like(w_s))
    z = jnp.einsum('hts,bnshd->bnthd', w, vc) + jnp.transpose(b_s)[None, None, :, :, None]
    return u * z.reshape(B, S, SGU_WIDTH)


def memory_cross_attention(q, k, v):
    B, S, _ = q.shape
    M = k.shape[1]
    qh = q.reshape(B, S, XATTN_HEADS, XATTN_HEAD_DIM)
    kh = k.reshape(B, M, XATTN_HEADS, XATTN_HEAD_DIM)
    vh = v.reshape(B, M, XATTN_HEADS, XATTN_HEAD_DIM)
    s = jnp.einsum('bshd,bmhd->bhsm', qh, kh).astype(jnp.float32) * (1.0 / math.sqrt(XATTN_HEAD_DIM))
    p = jax.nn.softmax(s, axis=-1).astype(vh.dtype)
    o = jnp.einsum('bhsm,bmhd->bshd', p, vh)
    return o.reshape(B, S, XATTN_WIDTH)


def _fwd_setup_inputs(seed: int = 0) -> dict:
    key = jax.random.key(seed)
    ks = jax.random.split(key, 16)
    f32 = jnp.float32
    nrm = lambda k, shape, s: jax.random.normal(k, shape, f32) * s
    return {
        "x": nrm(ks[0], (BATCH, SEQ, D_MODEL), 1.0),
        "mem": nrm(ks[1], (BATCH, MEM_LEN, D_MODEL), 1.0),
        "norm_pre": 1.0 + nrm(ks[2], (DEPTH, D_MODEL), 0.05),
        "w_in": nrm(ks[3], (DEPTH, D_MODEL, IN_WIDTH), D_MODEL ** -0.5),
        "pool_w": nrm(ks[4], (DEPTH, POOL_GROUPS, POOL_GROUP_DIM, POOL_GROUP_DIM), POOL_GROUP_DIM ** -0.5),
        "pool_scale": 1.0 + nrm(ks[5], (DEPTH, POOL_WIDTH), 0.1),
        "sgu_ln_g": 1.0 + nrm(ks[6], (DEPTH, SGU_WIDTH), 0.05),
        "sgu_ln_b": nrm(ks[7], (DEPTH, SGU_WIDTH), 0.02),
        "sgu_w": nrm(ks[8], (DEPTH, SGU_HEADS, SGU_CHUNK, SGU_CHUNK), SGU_CHUNK ** -0.5),
        "sgu_b": 1.0 + nrm(ks[9], (DEPTH, SGU_HEADS, SGU_CHUNK), 0.1),
        "mem_norm": 1.0 + nrm(ks[10], (D_MODEL,), 0.05),
        "w_kv": nrm(ks[11], (DEPTH, D_MODEL, 2 * XATTN_WIDTH), D_MODEL ** -0.5),
        "branch_norm": 1.0 + nrm(ks[12], (DEPTH, MIX_WIDTH), 0.05),
        "w_out": nrm(ks[13], (DEPTH, MIX_WIDTH, D_MODEL), MIX_WIDTH ** -0.5),
        "norm_post": 1.0 + nrm(ks[14], (DEPTH, D_MODEL), 0.05),
    }


def _fwd_reference(x, mem, norm_pre, w_in, pool_w, pool_scale, sgu_ln_g, sgu_ln_b, sgu_w, sgu_b,
              mem_norm, w_kv, branch_norm, w_out, norm_post):
    mem_n = rmsnorm(mem, mem_norm)
    for l in range(DEPTH):
        h = rmsnorm(x, norm_pre[l])
        proj = jnp.einsum('bsd,de->bse', h, w_in[l])
        xa, ga, u, vb, gb, q, gc = split_cols(proj, IN_SPLITS)
        k_m, v_m = split_cols(jnp.einsum('bmd,de->bme', mem_n, w_kv[l]), (XATTN_WIDTH, XATTN_WIDTH))

        ya = pool_mixer(xa, pool_w[l], pool_scale[l]) * jax.nn.silu(ga)
        yb = spatial_gating(u, vb, sgu_ln_g[l], sgu_ln_b[l], sgu_w[l], sgu_b[l]) * jax.nn.silu(gb)
        yc = memory_cross_attention(q, k_m, v_m) * jax.nn.silu(gc)

        g_a, g_b, g_c = split_cols(branch_norm[l], BRANCH_WIDTHS)
        y = jnp.concatenate([rmsnorm(ya, g_a), rmsnorm(yb, g_b), rmsnorm(yc, g_c)], axis=-1)
        out = jnp.einsum('bse,ed->bsd', y, w_out[l])
        x = x + rmsnorm(out, norm_post[l])
    return x


import jax as _jax
import jax.numpy as _jnp

TWIN_FORMAT = 'train_step'
FWD_PARAMS = ['x', 'mem', 'norm_pre', 'w_in', 'pool_w', 'pool_scale', 'sgu_ln_g', 'sgu_ln_b', 'sgu_w', 'sgu_b', 'mem_norm', 'w_kv', 'branch_norm', 'w_out', 'norm_post']
TWIN_WEIGHTS = ['norm_pre', 'w_in', 'pool_w', 'pool_scale', 'sgu_ln_g', 'sgu_ln_b', 'sgu_w', 'sgu_b', 'mem_norm', 'w_kv', 'branch_norm', 'w_out', 'norm_post']
TWIN_DIFF_INPUT = 'x'
TWIN_INPUTS = ['x', 'mem', 'norm_pre', 'w_in', 'pool_w', 'pool_scale', 'sgu_ln_g', 'sgu_ln_b', 'sgu_w', 'sgu_b', 'mem_norm', 'w_kv', 'branch_norm', 'w_out', 'norm_post', 'loss_target', 'm_norm_pre', 'm_w_in', 'm_pool_w', 'm_pool_scale', 'm_sgu_ln_g', 'm_sgu_ln_b', 'm_sgu_w', 'm_sgu_b', 'm_mem_norm', 'm_w_kv', 'm_branch_norm', 'm_w_out', 'm_norm_post', 'v_norm_pre', 'v_w_in', 'v_pool_w', 'v_pool_scale', 'v_sgu_ln_g', 'v_sgu_ln_b', 'v_sgu_w', 'v_sgu_b', 'v_mem_norm', 'v_w_kv', 'v_branch_norm', 'v_w_out', 'v_norm_post']
TWIN_OUTPUTS = ['loss', 'grad_x', 'grad_norm_pre', 'grad_w_in', 'grad_pool_w', 'grad_pool_scale', 'grad_sgu_ln_g', 'grad_sgu_ln_b', 'grad_sgu_w', 'grad_sgu_b', 'grad_mem_norm', 'grad_w_kv', 'grad_branch_norm', 'grad_w_out', 'grad_norm_post', 'delta_norm_pre', 'delta_w_in', 'delta_pool_w', 'delta_pool_scale', 'delta_sgu_ln_g', 'delta_sgu_ln_b', 'delta_sgu_w', 'delta_sgu_b', 'delta_mem_norm', 'delta_w_kv', 'delta_branch_norm', 'delta_w_out', 'delta_norm_post', 'new_m_norm_pre', 'new_m_w_in', 'new_m_pool_w', 'new_m_pool_scale', 'new_m_sgu_ln_g', 'new_m_sgu_ln_b', 'new_m_sgu_w', 'new_m_sgu_b', 'new_m_mem_norm', 'new_m_w_kv', 'new_m_branch_norm', 'new_m_w_out', 'new_m_norm_post', 'new_v_norm_pre', 'new_v_w_in', 'new_v_pool_w', 'new_v_pool_scale', 'new_v_sgu_ln_g', 'new_v_sgu_ln_b', 'new_v_sgu_w', 'new_v_sgu_b', 'new_v_mem_norm', 'new_v_w_kv', 'new_v_branch_norm', 'new_v_w_out', 'new_v_norm_post']
TWIN_LEAF_KINDS = {'loss': 'loss', 'grad_x': 'grad_x', 'grad_norm_pre': 'grad_w', 'grad_w_in': 'grad_w', 'grad_pool_w': 'grad_w', 'grad_pool_scale': 'grad_w', 'grad_sgu_ln_g': 'grad_w', 'grad_sgu_ln_b': 'grad_w', 'grad_sgu_w': 'grad_w', 'grad_sgu_b': 'grad_w', 'grad_mem_norm': 'grad_w', 'grad_w_kv': 'grad_w', 'grad_branch_norm': 'grad_w', 'grad_w_out': 'grad_w', 'grad_norm_post': 'grad_w', 'delta_norm_pre': 'delta_w', 'delta_w_in': 'delta_w', 'delta_pool_w': 'delta_w', 'delta_pool_scale': 'delta_w', 'delta_sgu_ln_g': 'delta_w', 'delta_sgu_ln_b': 'delta_w', 'delta_sgu_w': 'delta_w', 'delta_sgu_b': 'delta_w', 'delta_mem_norm': 'delta_w', 'delta_w_kv': 'delta_w', 'delta_branch_norm': 'delta_w', 'delta_w_out': 'delta_w', 'delta_norm_post': 'delta_w', 'new_m_norm_pre': 'new_m', 'new_m_w_in': 'new_m', 'new_m_pool_w': 'new_m', 'new_m_pool_scale': 'new_m', 'new_m_sgu_ln_g': 'new_m', 'new_m_sgu_ln_b': 'new_m', 'new_m_sgu_w': 'new_m', 'new_m_sgu_b': 'new_m', 'new_m_mem_norm': 'new_m', 'new_m_w_kv': 'new_m', 'new_m_branch_norm': 'new_m', 'new_m_w_out': 'new_m', 'new_m_norm_post': 'new_m', 'new_v_norm_pre': 'new_v', 'new_v_w_in': 'new_v', 'new_v_pool_w': 'new_v', 'new_v_pool_scale': 'new_v', 'new_v_sgu_ln_g': 'new_v', 'new_v_sgu_ln_b': 'new_v', 'new_v_sgu_w': 'new_v', 'new_v_sgu_b': 'new_v', 'new_v_mem_norm': 'new_v', 'new_v_w_kv': 'new_v', 'new_v_branch_norm': 'new_v', 'new_v_w_out': 'new_v', 'new_v_norm_post': 'new_v'}


def _forward(args):
    return _fwd_reference(*[args[k] for k in FWD_PARAMS])


def _output_shape():
    def fwd():
        inp = _fwd_setup_inputs(0)
        return _fwd_reference(*[inp[k] for k in FWD_PARAMS])
    out = _jax.eval_shape(fwd)
    return out.shape, out.dtype

N_MICROBATCH = 1
ADAM_LR = 0.001
ADAM_B1 = 0.9
ADAM_B2 = 0.999
ADAM_EPS = 1e-08
ADAM_WD = 0.01
ADAM_STEP = 10
PER_EXAMPLE_BATCH_AXIS = {'x': 0, 'mem': 0, 'loss_target': 0}
SHARED_INPUTS = []
_WEIGHT_DTYPES = {'norm_pre': _jnp.float32, 'w_in': _jnp.float32, 'pool_w': _jnp.float32, 'pool_scale': _jnp.float32, 'sgu_ln_g': _jnp.float32, 'sgu_ln_b': _jnp.float32, 'sgu_w': _jnp.float32, 'sgu_b': _jnp.float32, 'mem_norm': _jnp.float32, 'w_kv': _jnp.float32, 'branch_norm': _jnp.float32, 'w_out': _jnp.float32, 'norm_post': _jnp.float32}
MOMENT_SCALE = {'norm_pre': 1.906205e-01, 'w_in': 1.041570e-01, 'pool_w': 1.201959e-01, 'pool_scale': 1.217058e-01, 'sgu_ln_g': 5.772113e-02, 'sgu_ln_b': 5.645055e-02, 'sgu_w': 5.695016e-02, 'sgu_b': 8.217681e-02, 'mem_norm': 1.049375e-01, 'w_kv': 1.050712e-01, 'branch_norm': 1.188553e-01, 'w_out': 1.423897e-01, 'norm_post': 1.598154e+01}


def _to_microbatches(a, axis):
    t = _jnp.moveaxis(a, axis, 0)
    t = t.reshape((N_MICROBATCH, t.shape[0] // N_MICROBATCH) + t.shape[1:])
    return _jnp.moveaxis(t, 1, axis + 1)


def setup_inputs(seed: int = 0) -> dict:
    inp = _fwd_setup_inputs(seed)
    key = _jax.random.fold_in(_jax.random.key(seed), 7919)
    shape, _ = _output_shape()
    out = dict(inp)
    out["loss_target"] = _jax.random.normal(_jax.random.fold_in(key, 0), shape, _jnp.float32)
    for i, name in enumerate(TWIN_WEIGHTS):
        w = inp[name].astype(_jnp.float32)
        if MOMENT_SCALE is None:
            s = _jnp.sqrt(_jnp.mean(_jnp.square(w)) + 1e-30)
        else:
            s = MOMENT_SCALE[name]
        km, kv = _jax.random.split(_jax.random.fold_in(key, i + 1))
        out[name] = w
        out["m_" + name] = s * _jax.random.normal(km, w.shape, _jnp.float32)
        out["v_" + name] = (s * s) * _jax.random.uniform(kv, w.shape, _jnp.float32, 0.5, 1.5)
    if N_MICROBATCH > 1:
        for name, axis in PER_EXAMPLE_BATCH_AXIS.items():
            out[name] = _to_microbatches(out[name], axis)
    return {'x': out['x'], 'mem': out['mem'], 'norm_pre': out['norm_pre'], 'w_in': out['w_in'], 'pool_w': out['pool_w'], 'pool_scale': out['pool_scale'], 'sgu_ln_g': out['sgu_ln_g'], 'sgu_ln_b': out['sgu_ln_b'], 'sgu_w': out['sgu_w'], 'sgu_b': out['sgu_b'], 'mem_norm': out['mem_norm'], 'w_kv': out['w_kv'], 'branch_norm': out['branch_norm'], 'w_out': out['w_out'], 'norm_post': out['norm_post'], 'loss_target': out['loss_target'], 'm_norm_pre': out['m_norm_pre'], 'm_w_in': out['m_w_in'], 'm_pool_w': out['m_pool_w'], 'm_pool_scale': out['m_pool_scale'], 'm_sgu_ln_g': out['m_sgu_ln_g'], 'm_sgu_ln_b': out['m_sgu_ln_b'], 'm_sgu_w': out['m_sgu_w'], 'm_sgu_b': out['m_sgu_b'], 'm_mem_norm': out['m_mem_norm'], 'm_w_kv': out['m_w_kv'], 'm_branch_norm': out['m_branch_norm'], 'm_w_out': out['m_w_out'], 'm_norm_post': out['m_norm_post'], 'v_norm_pre': out['v_norm_pre'], 'v_w_in': out['v_w_in'], 'v_pool_w': out['v_pool_w'], 'v_pool_scale': out['v_pool_scale'], 'v_sgu_ln_g': out['v_sgu_ln_g'], 'v_sgu_ln_b': out['v_sgu_ln_b'], 'v_sgu_w': out['v_sgu_w'], 'v_sgu_b': out['v_sgu_b'], 'v_mem_norm': out['v_mem_norm'], 'v_w_kv': out['v_w_kv'], 'v_branch_norm': out['v_branch_norm'], 'v_w_out': out['v_w_out'], 'v_norm_post': out['v_norm_post']}


def _loss(weights, diff, rest, loss_target):
    with _jax.named_scope("forward"):
        args = {**rest, TWIN_DIFF_INPUT: diff, **{k: w.astype(_WEIGHT_DTYPES[k]) for k, w in weights.items()}}
        y = _forward(args)
    with _jax.named_scope("loss_head"):
        err = _jnp.square(y.astype(_jnp.float32) - loss_target)
        return 0.5 * _jnp.sum(_jnp.mean(err, axis=-1)) if err.ndim else 0.5 * err


def _adamw(w, g, m, v):
    m = ADAM_B1 * m + (1.0 - ADAM_B1) * g
    v = ADAM_B2 * v + (1.0 - ADAM_B2) * _jnp.square(g)
    m_hat = m / (1.0 - ADAM_B1 ** ADAM_STEP)
    v_hat = v / (1.0 - ADAM_B2 ** ADAM_STEP)
    delta = -ADAM_LR * (m_hat / (_jnp.sqrt(v_hat) + ADAM_EPS) + ADAM_WD * w)
    return delta, m, v


def reference(x, mem, norm_pre, w_in, pool_w, pool_scale, sgu_ln_g, sgu_ln_b, sgu_w, sgu_b, mem_norm, w_kv, branch_norm, w_out, norm_post, loss_target, m_norm_pre, m_w_in, m_pool_w, m_pool_scale, m_sgu_ln_g, m_sgu_ln_b, m_sgu_w, m_sgu_b, m_mem_norm, m_w_kv, m_branch_norm, m_w_out, m_norm_post, v_norm_pre, v_w_in, v_pool_w, v_pool_scale, v_sgu_ln_g, v_sgu_ln_b, v_sgu_w, v_sgu_b, v_mem_norm, v_w_kv, v_branch_norm, v_w_out, v_norm_post):
    given = dict(x=x, mem=mem, norm_pre=norm_pre, w_in=w_in, pool_w=pool_w, pool_scale=pool_scale, sgu_ln_g=sgu_ln_g, sgu_ln_b=sgu_ln_b, sgu_w=sgu_w, sgu_b=sgu_b, mem_norm=mem_norm, w_kv=w_kv, branch_norm=branch_norm, w_out=w_out, norm_post=norm_post, loss_target=loss_target, m_norm_pre=m_norm_pre, m_w_in=m_w_in, m_pool_w=m_pool_w, m_pool_scale=m_pool_scale, m_sgu_ln_g=m_sgu_ln_g, m_sgu_ln_b=m_sgu_ln_b, m_sgu_w=m_sgu_w, m_sgu_b=m_sgu_b, m_mem_norm=m_mem_norm, m_w_kv=m_w_kv, m_branch_norm=m_branch_norm, m_w_out=m_w_out, m_norm_post=m_norm_post, v_norm_pre=v_norm_pre, v_w_in=v_w_in, v_pool_w=v_pool_w, v_pool_scale=v_pool_scale, v_sgu_ln_g=v_sgu_ln_g, v_sgu_ln_b=v_sgu_ln_b, v_sgu_w=v_sgu_w, v_sgu_b=v_sgu_b, v_mem_norm=v_mem_norm, v_w_kv=v_w_kv, v_branch_norm=v_branch_norm, v_w_out=v_w_out, v_norm_post=v_norm_post)
    weights = {n: given[n] for n in TWIN_WEIGHTS}
    shared = {n: given[n] for n in SHARED_INPUTS}
    per_example = {n: given[n] for n in ['x', 'mem']}
    grad_fn = _jax.value_and_grad(_loss, argnums=(0, 1))

    def one_microbatch(ex, loss_target):
        ex = dict(ex)
        diff = ex.pop(TWIN_DIFF_INPUT)
        return grad_fn(weights, diff, {**shared, **ex}, loss_target)

    if N_MICROBATCH == 1:
        loss, (grad_w, grad_x) = one_microbatch(per_example, given["loss_target"])
    else:
        def body(carry, xs):
            loss_sum, grad_sum = carry
            l_k, (gw_k, gx_k) = one_microbatch(xs[0], xs[1])
            with _jax.named_scope("update"):
                return (loss_sum + l_k, _jax.tree.map(_jnp.add, grad_sum, gw_k)), gx_k

        init = (_jnp.zeros((), _jnp.float32), _jax.tree.map(_jnp.zeros_like, weights))
        (loss, grad_w), grad_x = _jax.lax.scan(body, init, (per_example, given["loss_target"]))
    with _jax.named_scope("update"):
        delta_w, new_m, new_v = {}, {}, {}
        for n in TWIN_WEIGHTS:
            delta_w[n], new_m[n], new_v[n] = _adamw(weights[n], grad_w[n], given["m_" + n], given["v_" + n])
    return (loss, grad_x, *[grad_w[n] for n in TWIN_WEIGHTS], *[delta_w[n] for n in TWIN_WEIGHTS],
            *[new_m[n] for n in TWIN_WEIGHTS], *[new_v[n] for n in TWIN_WEIGHTS])
```

```python
import functools

import jax
import jax.numpy as jnp
from jax import lax
from jax.experimental import pallas as pl
from jax.experimental.pallas import tpu as pltpu

F32 = jnp.float32
BF16 = jnp.bfloat16
EPS = 1e-6

D_MODEL = 2048
POOL_WINDOWS = (2, 4, 8, 16)
POOL_GROUP_DIM = 256
BRANCH = 1024
SGU_CHUNK = 128
SGU_HEADS = 8
SGU_HEAD_DIM = 128
XATTN_HEADS = 4
XATTN_HEAD_DIM = 256
MIX_WIDTH = 3 * BRANCH
IN_WIDTH = 7 * BRANCH
N_DEV = 8
WIN_BLK = IN_WIDTH // N_DEV
WOUT_BLK = MIX_WIDTH // N_DEV
WKV_BLK = D_MODEL // N_DEV
POOL_BLK = POOL_GROUP_DIM // N_DEV
HALO = 16

ADAM_LR = 0.001
ADAM_B1 = 0.9
ADAM_B2 = 0.999
ADAM_EPS = 1e-08
ADAM_WD = 0.01
ADAM_STEP = 10

VMEM_LIMIT_BYTES = 56 * 1024 * 1024

TILE_PREP = 512
TILE_PROJ = 512
TILE_MIX = 128
TILE_GRAD = 512
TILE_ADAM_BYTES = 1 << 20

NN = (((1,), (0,)), ((), ()))
NT = (((1,), (1,)), ((), ()))
TN = (((0,), (0,)), ((), ()))
MESH = pl.DeviceIdType.MESH


def _dot(a, b, dims=NN):
    return lax.dot_general(a, b, dims, preferred_element_type=F32)


def _params(*semantics):
    return pltpu.CompilerParams(dimension_semantics=semantics, vmem_limit_bytes=VMEM_LIMIT_BYTES)


def _rowmean(a):
    return jnp.mean(a, axis=-1, keepdims=True)


def _colsum(a):
    return jnp.sum(a, axis=0, keepdims=True)


def _full(shape):
    zeros = (0,) * len(shape)
    return pl.BlockSpec(shape, lambda *_: zeros)


def _resident(shape):
    zeros = (0,) * len(shape)
    return pl.BlockSpec(shape, lambda *_: zeros, pipeline_mode=pl.Buffered(1))


def _prep(x, norm_pre):
    s, d = x.shape
    t = min(TILE_PREP, s)

    def body(x_ref, g_ref, h_ref):
        xv = x_ref[...]
        r = lax.rsqrt(_rowmean(xv * xv) + EPS)
        h_ref[...] = (xv * r * g_ref[...]).astype(BF16)

    return pl.pallas_call(
        body,
        name="prep",
        grid=(s // t,),
        in_specs=[pl.BlockSpec((t, d), lambda i: (i, 0)), _full((1, d))],
        out_specs=pl.BlockSpec((t, d), lambda i: (i, 0)),
        out_shape=jax.ShapeDtypeStruct((s, d), BF16),
        compiler_params=_params("parallel"),
    )(x, norm_pre)


def _kv_forward(mem, mem_norm, wkv):
    m, d = mem.shape

    def body(mem_ref, g_ref, w_ref, memn_ref, kv_ref):
        mv = mem_ref[...]
        r = lax.rsqrt(_rowmean(mv * mv) + EPS)
        memn = (mv * r * g_ref[...]).astype(BF16)
        memn_ref[...] = memn
        kv_ref[...] = _dot(memn, w_ref[...]).astype(BF16)

    return pl.pallas_call(
        body,
        name="kv_forward",
        grid=(1,),
        in_specs=[_full((m, d)), _full((1, d)), _full(wkv.shape)],
        out_specs=[_full((m, d)), _full((m, wkv.shape[1]))],
        out_shape=[jax.ShapeDtypeStruct((m, d), BF16), jax.ShapeDtypeStruct((m, wkv.shape[1]), BF16)],
        compiler_params=_params("arbitrary"),
    )(mem, mem_norm, wkv)


def _proj(h, win):
    s, d = h.shape
    t = min(TILE_PROJ, s)

    def body(h_ref, w_ref, o_ref):
        o_ref[...] = _dot(h_ref[...], w_ref[...])

    return pl.pallas_call(
        body,
        name="proj",
        grid=(N_DEV, s // t),
        in_specs=[
            pl.BlockSpec((t, d), lambda j, i: (i, 0)),
            pl.BlockSpec((None, d, WIN_BLK), lambda j, i: (j, 0, 0)),
        ],
        out_specs=pl.BlockSpec((t, WIN_BLK), lambda j, i: (i, j)),
        out_shape=jax.ShapeDtypeStruct((s, IN_WIDTH), F32),
        compiler_params=_params("parallel", "parallel"),
    )(h, win)


def _sigmoid(a):
    return jax.nn.sigmoid(a)


def _dsilu(a, sg):
    return sg * (1.0 + a * (1.0 - sg))


def _rms_fwd(u, gain):
    r = lax.rsqrt(_rowmean(u * u) + EPS)
    n = u * r
    return r, n, n * gain


def _rms_bwd(dy, gain, r, n):
    dn = dy * gain
    return _colsum(dy * n), r * (dn - n * _rowmean(dn * n))


def _mix(proj, x, target, kv, wout, pool_w, pool_scale, ln_g, ln_b, sgu_bias, sgu_wm, branch_norm, norm_post):
    s, d = x.shape
    t = min(TILE_MIX, s)
    n_tiles = s // t
    n_chunks = t // SGU_CHUNK
    halo_blocks_per_tile = t // HALO
    inv_d = 1.0 / d
    scale = 1.0 / (XATTN_HEAD_DIM**0.5)

    def body(
        proj_ref, halo_ref, x_ref, tgt_ref, kv_ref, wout_ref, pw_ref, pscale_ref, lng_ref, lnb_ref, bias_ref, wm_ref,
        bnorm_ref, gpost_ref,
        y_ref, dout_ref, dxo_ref, dproj_ref, loss_ref, dgpost_ref, dbnorm_ref, dpscale_ref, dlng_ref, dlnb_ref,
        dpw_ref, dwm_ref, dbias_ref, dkv_ref,
        carry_ref, dzsum_ref,
    ):
        i = pl.program_id(0)
        tile = n_tiles - 1 - i

        @pl.when(i == 0)
        def _():
            carry_ref[...] = jnp.zeros_like(carry_ref)
            dzsum_ref[...] = jnp.zeros_like(dzsum_ref)
            for ref in (loss_ref, dgpost_ref, dbnorm_ref, dpscale_ref, dlng_ref, dlnb_ref, dpw_ref, dwm_ref, dkv_ref):
                ref[...] = jnp.zeros_like(ref)

        t_glob = tile * t + lax.broadcasted_iota(jnp.int32, (t, 1), 0)
        inv_cnt = [1.0 / jnp.minimum(t_glob + 1, w).astype(F32) for w in POOL_WINDOWS]

        xa = proj_ref[:, 0:BRANCH]
        ga = proj_ref[:, BRANCH : 2 * BRANCH]
        halo = jnp.where(tile == 0, 0.0, halo_ref[...])
        d_bf, pm_parts = [], []
        for g, w in enumerate(POOL_WINDOWS):
            cols = slice(g * POOL_GROUP_DIM, (g + 1) * POOL_GROUP_DIM)
            acc = jnp.concatenate([halo[:, cols], xa[:, cols]], axis=0)
            k = 1
            while k < w:
                acc = acc + pltpu.roll(acc, k, axis=0)
                k *= 2
            dg = (acc[HALO:, :] * inv_cnt[g] - xa[:, cols]).astype(BF16)
            d_bf.append(dg)
            pm_parts.append(_dot(dg, pw_ref[g]))
        pm = jnp.concatenate(pm_parts, axis=1)
        pscale = pscale_ref[...]
        pa = pm * pscale
        sga = _sigmoid(ga)
        sila = ga * sga
        ua = pa * sila
        g_a = bnorm_ref[:, 0:BRANCH]
        ra, na, ya = _rms_fwd(ua, g_a)

        u = proj_ref[:, 2 * BRANCH : 3 * BRANCH]
        v = proj_ref[:, 3 * BRANCH : 4 * BRANCH]
        gb = proj_ref[:, 4 * BRANCH : 5 * BRANCH]
        lng = lng_ref[...]
        vc = v - _rowmean(v)
        rstd = lax.rsqrt(_rowmean(vc * vc) + EPS)
        vhat = vc * rstd
        vn_bf = (vhat * lng + lnb_ref[...]).astype(BF16)
        z_rows = []
        for c in range(n_chunks):
            rows = slice(c * SGU_CHUNK, (c + 1) * SGU_CHUNK)
            z_rows.append(
                jnp.concatenate(
                    [
                        _dot(wm_ref[hd], vn_bf[rows, hd * SGU_HEAD_DIM : (hd + 1) * SGU_HEAD_DIM])
                        for hd in range(SGU_HEADS)
                    ],
                    axis=1,
                )
                + bias_ref[...]
            )
        z = z_rows[0] if n_chunks == 1 else jnp.concatenate(z_rows, axis=0)
        sb = u * z
        sgb = _sigmoid(gb)
        silb = gb * sgb
        ub = sb * silb
        g_b = bnorm_ref[:, BRANCH : 2 * BRANCH]
        rb, nb, yb = _rms_fwd(ub, g_b)

        q = proj_ref[:, 5 * BRANCH : 6 * BRANCH]
        gc = proj_ref[:, 6 * BRANCH : 7 * BRANCH]
        q_bf, p_bf, o_parts = [], [], []
        for hd in range(XATTN_HEADS):
            cols = slice(hd * XATTN_HEAD_DIM, (hd + 1) * XATTN_HEAD_DIM)
            qh = q[:, cols].astype(BF16)
            sc = _dot(qh, kv_ref[:, cols], NT) * scale
            e = jnp.exp(sc - jnp.max(sc, axis=-1, keepdims=True))
            p = e / jnp.sum(e, axis=-1, keepdims=True)
            q_bf.append(qh)
            p_bf.append(p.astype(BF16))
            o_parts.append(_dot(p_bf[hd], kv_ref[:, BRANCH + hd * XATTN_HEAD_DIM : BRANCH + (hd + 1) * XATTN_HEAD_DIM]))
        o = jnp.concatenate(o_parts, axis=1)
        sgc = _sigmoid(gc)
        silc = gc * sgc
        uc = o * silc
        g_c = bnorm_ref[:, 2 * BRANCH : 3 * BRANCH]
        rc, nc, yc = _rms_fwd(uc, g_c)

        y_bf = jnp.concatenate([ya, yb, yc], axis=1).astype(BF16)
        y_ref[...] = y_bf
        out = _dot(y_bf, wout_ref[...])
        gpost = gpost_ref[...]
        r_out = lax.rsqrt(_rowmean(out * out) + EPS)
        on = out * r_out
        err = x_ref[...] + on * gpost - tgt_ref[...]
        loss_ref[...] += 0.5 * jnp.sum(_rowmean(err * err), axis=0, keepdims=True)

        dxo = err * inv_d
        dxo_ref[...] = dxo
        dgp, dout = _rms_bwd(dxo, gpost, r_out, on)
        dgpost_ref[...] += dgp
        dout_bf = dout.astype(BF16)
        dout_ref[...] = dout_bf
        dy = _dot(dout_bf, wout_ref[...], NT)

        dg_a, dua = _rms_bwd(dy[:, 0:BRANCH], g_a, ra, na)
        dg_b, dub = _rms_bwd(dy[:, BRANCH : 2 * BRANCH], g_b, rb, nb)
        dg_c, duc = _rms_bwd(dy[:, 2 * BRANCH : 3 * BRANCH], g_c, rc, nc)
        dbnorm_ref[...] += jnp.concatenate([dg_a, dg_b, dg_c], axis=1)

        dpa = dua * sila
        dga = dua * pa * _dsilu(ga, sga)
        dpscale_ref[...] += _colsum(dpa * pm)
        dpm = dpa * pscale
        dxa_parts, carry_parts = [], []
        for g, w in enumerate(POOL_WINDOWS):
            cols = slice(g * POOL_GROUP_DIM, (g + 1) * POOL_GROUP_DIM)
            dpm_g = dpm[:, cols].astype(BF16)
            dd = _dot(dpm_g, pw_ref[g], NT)
            dpw_ref[g] += _dot(d_bf[g], dpm_g, TN)
            cg = dd * inv_cnt[g]
            carry_parts.append(cg[0:HALO, :])
            acc = jnp.concatenate([cg, carry_ref[:, cols]], axis=0)
            k = 1
            while k < w:
                acc = acc + pltpu.roll(acc, t + HALO - k, axis=0)
                k *= 2
            dxa_parts.append(acc[0:t, :] - dd)
        carry_ref[...] = jnp.concatenate(carry_parts, axis=1)
        dxa = jnp.concatenate(dxa_parts, axis=1)

        dsb = dub * silb
        dgb = dub * sb * _dsilu(gb, sgb)
        du = dsb * z
        dz = dsb * u
        dz_bf = dz.astype(BF16)
        dvn_rows = []
        dz_sum = None
        for c in range(n_chunks):
            rows = slice(c * SGU_CHUNK, (c + 1) * SGU_CHUNK)
            dz_sum = dz[rows, :] if dz_sum is None else dz_sum + dz[rows, :]
            parts = []
            for hd in range(SGU_HEADS):
                cols = slice(hd * SGU_HEAD_DIM, (hd + 1) * SGU_HEAD_DIM)
                parts.append(_dot(wm_ref[hd], dz_bf[rows, cols], TN))
                dwm_ref[hd] += _dot(dz_bf[rows, cols], vn_bf[rows, cols], NT)
            dvn_rows.append(jnp.concatenate(parts, axis=1))
        dzsum_ref[...] += dz_sum
        dvn = dvn_rows[0] if n_chunks == 1 else jnp.concatenate(dvn_rows, axis=0)
        dlng_ref[...] += _colsum(dvn * vhat)
        dlnb_ref[...] += _colsum(dvn)
        dvh = dvn * lng
        dv = rstd * (dvh - _rowmean(dvh) - vhat * _rowmean(dvh * vhat))

        do = duc * silc
        dgc = duc * o * _dsilu(gc, sgc)
        dq_parts = []
        for hd in range(XATTN_HEADS):
            cols = slice(hd * XATTN_HEAD_DIM, (hd + 1) * XATTN_HEAD_DIM)
            vcols = slice(BRANCH + hd * XATTN_HEAD_DIM, BRANCH + (hd + 1) * XATTN_HEAD_DIM)
            do_h = do[:, cols].astype(BF16)
            p = p_bf[hd].astype(F32)
            dp = _dot(do_h, kv_ref[:, vcols], NT)
            dkv_ref[:, vcols] += _dot(p_bf[hd], do_h, TN)
            ds_bf = (p * (dp - jnp.sum(dp * p, axis=-1, keepdims=True)) * scale).astype(BF16)
            dq_parts.append(_dot(ds_bf, kv_ref[:, cols]))
            dkv_ref[:, cols] += _dot(ds_bf, q_bf[hd], TN)
        dq = jnp.concatenate(dq_parts, axis=1)

        dproj_ref[...] = jnp.concatenate([dxa, dga, du, dv, dgb, dq, dgc], axis=1).astype(BF16)

        @pl.when(i == n_tiles - 1)
        def _():
            keep = lax.broadcasted_iota(jnp.int32, (SGU_CHUNK, SGU_CHUNK), 0) >= lax.broadcasted_iota(
                jnp.int32, (SGU_CHUNK, SGU_CHUNK), 1
            )
            for hd in range(SGU_HEADS):
                dwm_ref[hd] = jnp.where(keep, dwm_ref[hd], 0.0)
                per_pos = dzsum_ref[:, hd * SGU_HEAD_DIM : (hd + 1) * SGU_HEAD_DIM]
                dbias_ref[hd : hd + 1, :] = _colsum(per_pos.T)

    row_tile = lambda width: pl.BlockSpec((t, width), lambda i: (n_tiles - 1 - i, 0))
    halo_spec = pl.BlockSpec(
        (HALO, BRANCH), lambda i: (jnp.maximum((n_tiles - 1 - i) * halo_blocks_per_tile - 1, 0), 0)
    )
    acc_shapes = [
        (1, 128),
        (1, d),
        (1, MIX_WIDTH),
        (1, BRANCH),
        (1, BRANCH),
        (1, BRANCH),
        pool_w.shape,
        sgu_wm.shape,
        (SGU_HEADS, SGU_CHUNK),
        kv.shape,
    ]
    return pl.pallas_call(
        body,
        name="mix",
        grid=(n_tiles,),
        in_specs=[
            row_tile(IN_WIDTH), halo_spec, row_tile(d), row_tile(d), _resident(kv.shape), _resident(wout.shape),
            _resident(pool_w.shape), _full((1, BRANCH)), _full((1, BRANCH)), _full((1, BRANCH)),
            _resident((SGU_CHUNK, BRANCH)), _resident(sgu_wm.shape), _full((1, MIX_WIDTH)), _full((1, d)),
        ],
        out_specs=[row_tile(MIX_WIDTH), row_tile(d), row_tile(d), row_tile(IN_WIDTH)] + [_full(a) for a in acc_shapes],
        out_shape=[
            jax.ShapeDtypeStruct((s, MIX_WIDTH), BF16),
            jax.ShapeDtypeStruct((s, d), BF16),
            jax.ShapeDtypeStruct((s, d), F32),
            jax.ShapeDtypeStruct((s, IN_WIDTH), BF16),
        ]
        + [jax.ShapeDtypeStruct(a, F32) for a in acc_shapes],
        scratch_shapes=[pltpu.VMEM((HALO, BRANCH), F32), pltpu.VMEM((SGU_CHUNK, BRANCH), F32)],
        compiler_params=_params("arbitrary"),
    )(proj, proj, x, target, kv, wout, pool_w, pool_scale, ln_g, ln_b, sgu_bias, sgu_wm, branch_norm, norm_post)


def _grad_cols(a, b, n_blk, name):
    s, k = a.shape
    c = b.shape[1] // n_blk
    t = min(TILE_GRAD, s)
    n_t = s // t

    def body(a_ref, b_ref, o_ref, acc_ref):
        i = pl.program_id(1)

        @pl.when(i == 0)
        def _():
            acc_ref[...] = jnp.zeros_like(acc_ref)

        acc_ref[...] += _dot(a_ref[...], b_ref[...], TN)

        @pl.when(i == n_t - 1)
        def _():
            o_ref[...] = acc_ref[...].astype(BF16)

    return pl.pallas_call(
        body,
        name=name,
        grid=(n_blk, n_t),
        in_specs=[pl.BlockSpec((t, k), lambda j, i: (i, 0)), pl.BlockSpec((t, c), lambda j, i: (i, j))],
        out_specs=pl.BlockSpec((None, k, c), lambda j, i: (j, 0, 0)),
        out_shape=jax.ShapeDtypeStruct((n_blk, k, c), BF16),
        scratch_shapes=[pltpu.VMEM((k, c), F32)],
        compiler_params=_params("parallel", "arbitrary"),
    )(a, b)


def _grad_rows(a, b, n_blk, name):
    s, n = b.shape
    r = a.shape[1] // n_blk
    t = min(TILE_GRAD, s)
    n_t = s // t

    def body(a_ref, b_ref, o_ref, acc_ref):
        i = pl.program_id(1)

        @pl.when(i == 0)
        def _():
            acc_ref[...] = jnp.zeros_like(acc_ref)

        acc_ref[...] += _dot(a_ref[...], b_ref[...], TN)

        @pl.when(i == n_t - 1)
        def _():
            o_ref[...] = acc_ref[...].astype(BF16)

    return pl.pallas_call(
        body,
        name=name,
        grid=(n_blk, n_t),
        in_specs=[pl.BlockSpec((t, r), lambda j, i: (i, j)), pl.BlockSpec((t, n), lambda j, i: (i, 0))],
        out_specs=pl.BlockSpec((None, r, n), lambda j, i: (j, 0, 0)),
        out_shape=jax.ShapeDtypeStruct((n_blk, r, n), BF16),
        scratch_shapes=[pltpu.VMEM((r, n), F32)],
        compiler_params=_params("parallel", "arbitrary"),
    )(a, b)


def _input_grad(dproj, win, x, dxo, norm_pre):
    s, d = x.shape
    t = min(TILE_GRAD, s)
    n_t = s // t

    def body(dp_ref, w_ref, x_ref, dxo_ref, g_ref, gx_ref, dg_ref, acc_ref):
        i = pl.program_id(0)
        j = pl.program_id(1)

        @pl.when(jnp.logical_and(i == 0, j == 0))
        def _():
            dg_ref[...] = jnp.zeros_like(dg_ref)

        @pl.when(j == 0)
        def _():
            acc_ref[...] = jnp.zeros_like(acc_ref)

        acc_ref[...] += _dot(dp_ref[...], w_ref[...], NT)

        @pl.when(j == N_DEV - 1)
        def _():
            xv = x_ref[...]
            gain = g_ref[...]
            r = lax.rsqrt(_rowmean(xv * xv) + EPS)
            dgain, dx = _rms_bwd(acc_ref[...], gain, r, xv * r)
            dg_ref[...] += dgain
            gx_ref[...] = dxo_ref[...] + dx

    return pl.pallas_call(
        body,
        name="input_grad",
        grid=(n_t, N_DEV),
        in_specs=[
            pl.BlockSpec((t, WIN_BLK), lambda i, j: (i, j)),
            pl.BlockSpec((None, d, WIN_BLK), lambda i, j: (j, 0, 0)),
            pl.BlockSpec((t, d), lambda i, j: (i, 0)),
            pl.BlockSpec((t, d), lambda i, j: (i, 0)),
            _full((1, d)),
        ],
        out_specs=[pl.BlockSpec((t, d), lambda i, j: (i, 0)), _full((1, d))],
        out_shape=[jax.ShapeDtypeStruct((s, d), F32), jax.ShapeDtypeStruct((1, d), F32)],
        scratch_shapes=[pltpu.VMEM((t, d), F32)],
        compiler_params=_params("arbitrary", "arbitrary"),
    )(dproj, win, x, dxo, norm_pre)


def _kv_backward(dkv, memn, wkv, mem, mem_norm):
    m, d = mem.shape
    n = wkv.shape[1]

    def body(dkv_ref, memn_ref, w_ref, mem_ref, g_ref, gw_ref, dg_ref):
        dkv_bf = dkv_ref[...].astype(BF16)
        gw_ref[...] = _dot(memn_ref[...], dkv_bf, TN).astype(BF16).reshape(N_DEV, WKV_BLK, n)
        dmemn = _dot(dkv_bf, w_ref[...], NT)
        mv = mem_ref[...]
        r = lax.rsqrt(_rowmean(mv * mv) + EPS)
        dg_ref[...] = _colsum(dmemn * (mv * r))

    return pl.pallas_call(
        body,
        name="kv_backward",
        grid=(1,),
        in_specs=[_full((m, n)), _full((m, d)), _full(wkv.shape), _full((m, d)), _full((1, d))],
        out_specs=[_full((N_DEV, WKV_BLK, n)), _full((1, d))],
        out_shape=[jax.ShapeDtypeStruct((N_DEV, WKV_BLK, n), BF16), jax.ShapeDtypeStruct((1, d), F32)],
        compiler_params=_params("arbitrary"),
    )(dkv, memn, wkv, mem, mem_norm)


def _adamw_math(w, g, m, v):
    m = ADAM_B1 * m + (1.0 - ADAM_B1) * g
    v = ADAM_B2 * v + (1.0 - ADAM_B2) * (g * g)
    m_hat = m / (1.0 - ADAM_B1**ADAM_STEP)
    v_hat = v / (1.0 - ADAM_B2**ADAM_STEP)
    delta = -ADAM_LR * (m_hat / (jnp.sqrt(v_hat) + ADAM_EPS) + ADAM_WD * w)
    return delta, m, v


def _adamw(parts, w, m, v, name):
    r, c = w.shape
    t = r
    while t * c * 4 > TILE_ADAM_BYTES and t % 16 == 0:
        t //= 2

    def body(p_ref, w_ref, m_ref, v_ref, g_ref, d_ref, nm_ref, nv_ref):
        g = p_ref[0].astype(F32)
        for k in range(1, N_DEV):
            g = g + p_ref[k].astype(F32)
        delta, nm, nv = _adamw_math(w_ref[...], g, m_ref[...], v_ref[...])
        g_ref[...] = g
        d_ref[...] = delta
        nm_ref[...] = nm
        nv_ref[...] = nv

    tile = pl.BlockSpec((t, c), lambda i: (i, 0))
    return pl.pallas_call(
        body,
        name=name,
        grid=(r // t,),
        in_specs=[pl.BlockSpec((N_DEV, t, c), lambda i: (0, i, 0)), tile, tile, tile],
        out_specs=[tile] * 4,
        out_shape=[jax.ShapeDtypeStruct((r, c), F32)] * 4,
        compiler_params=_params("parallel"),
    )(parts, w, m, v)


ANY = pl.BlockSpec(memory_space=pl.ANY)


def _position():
    return lax.axis_index("x"), lax.axis_index("y"), lax.axis_index("c")


def _gather_weights(shards):
    n_arr = len(shards)

    def body(*refs):
        src = refs[:n_arr]
        out = refs[n_arr : 2 * n_arr]
        send_sems, recv_sems, local_sems = refs[2 * n_arr :]
        x, y, c = _position()
        me, sibling = (x, y, c), (x, y, 1 - c)
        chips = [(1 - x, y), (x, 1 - y), (1 - x, 1 - y)]

        def block(a, px, py, pc):
            return out[a].at[4 * px + 2 * py + pc]

        def copy(a, k, owner, to, from_input=False):
            return pltpu.make_async_remote_copy(
                src_ref=src[a] if from_input else block(a, *owner),
                dst_ref=block(a, *owner),
                send_sem=send_sems.at[a, k],
                recv_sem=recv_sems.at[a, k],
                device_id=to,
                device_id_type=MESH,
            )

        mine = [pltpu.make_async_copy(src[a], block(a, *me), local_sems.at[a]) for a in range(n_arr)]
        started = []
        for a in range(n_arr):
            mine[a].start()
            first = [copy(a, 0, me, sibling, from_input=True)]
            first += [copy(a, 1 + j, me, (*chip, c), from_input=True) for j, chip in enumerate(chips)]
            for cp in first:
                cp.start()
            started += first
        for a in range(n_arr):
            for j, chip in enumerate(chips):
                copy(a, 1 + j, (*chip, c), me).wait_recv()
                passed = copy(a, 4 + j, (*chip, c), sibling)
                passed.start()
                started.append(passed)
        for a in range(n_arr):
            copy(a, 0, sibling, me).wait_recv()
            for j, chip in enumerate(chips):
                copy(a, 4 + j, (*chip, 1 - c), me).wait_recv()
        for cp in started:
            cp.wait_send()
        for a in range(n_arr):
            mine[a].wait()

    return pl.pallas_call(
        body,
        name="gather_weights",
        in_specs=[ANY] * n_arr,
        out_specs=[ANY] * n_arr,
        out_shape=[jax.ShapeDtypeStruct((N_DEV,) + a.shape, a.dtype) for a in shards],
        scratch_shapes=[
            pltpu.SemaphoreType.DMA((n_arr, 7)),
            pltpu.SemaphoreType.DMA((n_arr, 7)),
            pltpu.SemaphoreType.DMA((n_arr,)),
        ],
        compiler_params=pltpu.CompilerParams(has_side_effects=True),
    )(*shards)


def _scatter_partials(partials):
    n_arr = len(partials)

    def body(*refs):
        src = refs[:n_arr]
        out = refs[n_arr : 2 * n_arr]
        send_sems, recv_sems, local_sems = refs[2 * n_arr :]
        x, y, c = _position()
        me = 4 * x + 2 * y + c

        def peer(k):
            return (x ^ ((k >> 2) & 1), y ^ ((k >> 1) & 1), c ^ (k & 1))

        def copy(a, k):
            px, py, pc = peer(k)
            return pltpu.make_async_remote_copy(
                src_ref=src[a].at[4 * px + 2 * py + pc],
                dst_ref=out[a].at[me],
                send_sem=send_sems.at[a, k - 1],
                recv_sem=recv_sems.at[a, k - 1],
                device_id=(px, py, pc),
                device_id_type=MESH,
            )

        mine = [pltpu.make_async_copy(src[a].at[me], out[a].at[me], local_sems.at[a]) for a in range(n_arr)]
        copies = [copy(a, k) for a in range(n_arr) for k in range(1, N_DEV)]
        for cp in mine + copies:
            cp.start()
        for cp in copies:
            cp.wait_recv()
        for cp in copies:
            cp.wait_send()
        for cp in mine:
            cp.wait()

    return pl.pallas_call(
        body,
        name="scatter_partials",
        in_specs=[ANY] * n_arr,
        out_specs=[ANY] * n_arr,
        out_shape=[jax.ShapeDtypeStruct(a.shape, a.dtype) for a in partials],
        scratch_shapes=[
            pltpu.SemaphoreType.DMA((n_arr, 7)),
            pltpu.SemaphoreType.DMA((n_arr, 7)),
            pltpu.SemaphoreType.DMA((n_arr,)),
        ],
        compiler_params=pltpu.CompilerParams(has_side_effects=True),
    )(*partials)


SMALL = ("norm_pre", "pool_scale", "sgu_ln_g", "sgu_ln_b", "sgu_w", "sgu_b", "mem_norm", "branch_norm", "norm_post")


def _local_view(name, w):
    if name == "sgu_w":
        return w.reshape(SGU_HEADS, SGU_CHUNK, SGU_CHUNK)
    if name == "sgu_b":
        return w.reshape(SGU_HEADS, SGU_CHUNK)
    return w.reshape(1, -1)


def _local_step(x, mem, target, win, wout, wkv, pool_w_bf, small):
    s, d = x.shape
    causal = jnp.tril(jnp.ones((SGU_CHUNK, SGU_CHUNK), dtype=bool))
    sgu_wm = jnp.where(causal[None], small["sgu_w"], 0.0).astype(BF16)
    sgu_bias = jnp.repeat(jnp.transpose(small["sgu_b"]), SGU_HEAD_DIM, axis=1)

    h = _prep(x, small["norm_pre"])
    memn, kv = _kv_forward(mem, small["mem_norm"], wkv)
    proj = _proj(h, win)
    (y, dout, dxo, dproj, loss, d_norm_post, d_branch_norm, d_pool_scale, d_ln_g, d_ln_b, d_pool_w, d_sgu_w, d_sgu_b,
     dkv) = _mix(
        proj, x, target, kv, wout, pool_w_bf, small["pool_scale"], small["sgu_ln_g"], small["sgu_ln_b"], sgu_bias,
        sgu_wm, small["branch_norm"], small["norm_post"],
    )
    g_win = _grad_cols(h, dproj, N_DEV, "grad_w_in")
    g_wout = _grad_rows(y, dout, N_DEV, "grad_w_out")
    grad_x, d_norm_pre = _input_grad(dproj, win, x, dxo, small["norm_pre"])
    g_wkv, d_mem_norm = _kv_backward(dkv, memn, wkv, mem, small["mem_norm"])
    g_pool = (
        d_pool_w.reshape(len(POOL_WINDOWS), N_DEV, POOL_BLK, POOL_GROUP_DIM)
        .transpose(1, 0, 2, 3)
        .reshape(N_DEV, len(POOL_WINDOWS) * POOL_BLK, POOL_GROUP_DIM)
        .astype(BF16)
    )
    small_grads = dict(
        norm_pre=d_norm_pre, pool_scale=d_pool_scale, sgu_ln_g=d_ln_g, sgu_ln_b=d_ln_b, sgu_w=d_sgu_w, sgu_b=d_sgu_b,
        mem_norm=d_mem_norm, branch_norm=d_branch_norm, norm_post=d_norm_post,
    )
    return loss, grad_x, (g_win, g_wout, g_wkv, g_pool), small_grads


def kernel(x, mem, norm_pre, w_in, pool_w, pool_scale, sgu_ln_g, sgu_ln_b, sgu_w, sgu_b, mem_norm, w_kv, branch_norm, w_out, norm_post, loss_target, m_norm_pre, m_w_in, m_pool_w, m_pool_scale, m_sgu_ln_g, m_sgu_ln_b, m_sgu_w, m_sgu_b, m_mem_norm, m_w_kv, m_branch_norm, m_w_out, m_norm_post, v_norm_pre, v_w_in, v_pool_w, v_pool_scale, v_sgu_ln_g, v_sgu_ln_b, v_sgu_w, v_sgu_b, v_mem_norm, v_w_kv, v_branch_norm, v_w_out, v_norm_post):
    weights = dict(norm_pre=norm_pre, w_in=w_in, pool_w=pool_w, pool_scale=pool_scale, sgu_ln_g=sgu_ln_g, sgu_ln_b=sgu_ln_b, sgu_w=sgu_w, sgu_b=sgu_b, mem_norm=mem_norm, w_kv=w_kv, branch_norm=branch_norm, w_out=w_out, norm_post=norm_post)
    first = dict(norm_pre=m_norm_pre, w_in=m_w_in, pool_w=m_pool_w, pool_scale=m_pool_scale, sgu_ln_g=m_sgu_ln_g, sgu_ln_b=m_sgu_ln_b, sgu_w=m_sgu_w, sgu_b=m_sgu_b, mem_norm=m_mem_norm, w_kv=m_w_kv, branch_norm=m_branch_norm, w_out=m_w_out, norm_post=m_norm_post)
    second = dict(norm_pre=v_norm_pre, w_in=v_w_in, pool_w=v_pool_w, pool_scale=v_pool_scale, sgu_ln_g=v_sgu_ln_g, sgu_ln_b=v_sgu_ln_b, sgu_w=v_sgu_w, sgu_b=v_sgu_b, mem_norm=v_mem_norm, w_kv=v_w_kv, branch_norm=v_branch_norm, w_out=v_w_out, norm_post=v_norm_post)
    order = ("norm_pre", "w_in", "pool_w", "pool_scale", "sgu_ln_g", "sgu_ln_b", "sgu_w", "sgu_b", "mem_norm", "w_kv", "branch_norm", "w_out", "norm_post")

    owned_shape = dict(
        w_in=(D_MODEL, WIN_BLK), w_out=(WOUT_BLK, D_MODEL), w_kv=(WKV_BLK, 2 * BRANCH),
        pool_w=(len(POOL_WINDOWS) * POOL_BLK, POOL_GROUP_DIM),
    )
    owned = {n: weights[n].reshape(owned_shape[n]) for n in owned_shape}
    win, wout, wkv, pool_all = _gather_weights([owned[n].astype(BF16) for n in ("w_in", "w_out", "w_kv", "pool_w")])
    wout = wout.reshape(MIX_WIDTH, D_MODEL)
    wkv = wkv.reshape(D_MODEL, 2 * BRANCH)
    pool_full = (
        pool_all.reshape(N_DEV, len(POOL_WINDOWS), POOL_BLK, POOL_GROUP_DIM)
        .transpose(1, 0, 2, 3)
        .reshape(len(POOL_WINDOWS), POOL_GROUP_DIM, POOL_GROUP_DIM)
    )

    small = {n: _local_view(n, weights[n]) for n in SMALL}
    loss, grad_x, big_parts, small_grads = _local_step(
        x[0], mem[0], loss_target[0], win, wout, wkv, pool_full, small
    )

    packed = jnp.concatenate([small_grads[n].reshape(-1, 128) for n in SMALL], axis=0)
    rows = packed.shape[0]
    landed = _scatter_partials(
        list(big_parts) + [jnp.broadcast_to(packed[None], (N_DEV, rows, 128))]
    )

    grads, deltas, new_m, new_v = {}, {}, {}, {}
    for n, parts in zip(("w_in", "w_out", "w_kv", "pool_w"), landed[:4]):
        shape = weights[n].shape
        res = _adamw(
            parts, owned[n], first[n].reshape(owned_shape[n]), second[n].reshape(owned_shape[n]), "adamw_" + n
        )
        grads[n], deltas[n], new_m[n], new_v[n] = (a.reshape(shape) for a in res)
    flat = lambda tree: jnp.concatenate([tree[n].reshape(-1, 128) for n in SMALL], axis=0)
    res = _adamw(landed[4], flat(weights), flat(first), flat(second), "adamw_replicated")
    at = 0
    for n in SMALL:
        size = weights[n].size // 128
        for tree, a in zip((grads, deltas, new_m, new_v), res):
            tree[n] = a[at : at + size].reshape(weights[n].shape)
        at += size

    total = lax.psum(loss[0, 0], ("x", "y", "c"))
    return (
        total,
        grad_x[None],
        *[grads[n] for n in order],
        *[deltas[n] for n in order],
        *[new_m[n] for n in order],
        *[new_v[n] for n in order],
    )
```

```python
import functools

import jax
import jax.numpy as jnp
from jax import lax
from jax.experimental import pallas as pl
from jax.experimental.pallas import tpu as pltpu

F32 = jnp.float32
BF16 = jnp.bfloat16
EPS = 1e-6

D_MODEL = 2048
POOL_WINDOWS = (2, 4, 8, 16)
POOL_GROUP_DIM = 256
BRANCH = 1024
SGU_CHUNK = 128
SGU_HEADS = 8
SGU_HEAD_DIM = 128
XATTN_HEADS = 4
XATTN_HEAD_DIM = 256
MIX_WIDTH = 3 * BRANCH
IN_WIDTH = 7 * BRANCH
N_DEV = 8
WIN_BLK = IN_WIDTH // N_DEV
WOUT_BLK = MIX_WIDTH // N_DEV
WKV_BLK = D_MODEL // N_DEV
POOL_BLK = POOL_GROUP_DIM // N_DEV
HALO = 16

ADAM_LR = 0.001
ADAM_B1 = 0.9
ADAM_B2 = 0.999
ADAM_EPS = 1e-08
ADAM_WD = 0.01
ADAM_STEP = 10

VMEM_LIMIT_BYTES = 56 * 1024 * 1024

TILE_PREP = 512
TILE_PROJ = 512
TILE_MIX = 128
TILE_GRAD = 512
TILE_ADAM_BYTES = 1 << 20

NN = (((1,), (0,)), ((), ()))
NT = (((1,), (1,)), ((), ()))
TN = (((0,), (0,)), ((), ()))
MESH = pl.DeviceIdType.MESH


def _dot(a, b, dims=NN):
    return lax.dot_general(a, b, dims, preferred_element_type=F32)


def _params(*semantics):
    return pltpu.CompilerParams(dimension_semantics=semantics, vmem_limit_bytes=VMEM_LIMIT_BYTES)


def _rowmean(a):
    return jnp.mean(a, axis=-1, keepdims=True)


def _colsum(a):
    return jnp.sum(a, axis=0, keepdims=True)


def _full(shape):
    zeros = (0,) * len(shape)
    return pl.BlockSpec(shape, lambda *_: zeros)


def _resident(shape):
    zeros = (0,) * len(shape)
    return pl.BlockSpec(shape, lambda *_: zeros, pipeline_mode=pl.Buffered(1))


def _prep(x, norm_pre):
    s, d = x.shape
    t = min(TILE_PREP, s)

    def body(x_ref, g_ref, h_ref):
        xv = x_ref[...]
        r = lax.rsqrt(_rowmean(xv * xv) + EPS)
        h_ref[...] = (xv * r * g_ref[...]).astype(BF16)

    return pl.pallas_call(
        body,
        name="prep",
        grid=(s // t,),
        in_specs=[pl.BlockSpec((t, d), lambda i: (i, 0)), _full((1, d))],
        out_specs=pl.BlockSpec((t, d), lambda i: (i, 0)),
        out_shape=jax.ShapeDtypeStruct((s, d), BF16),
        compiler_params=_params("parallel"),
    )(x, norm_pre)


def _kv_forward(mem, mem_norm, wkv):
    m, d = mem.shape

    def body(mem_ref, g_ref, w_ref, memn_ref, kv_ref):
        mv = mem_ref[...]
        r = lax.rsqrt(_rowmean(mv * mv) + EPS)
        memn = (mv * r * g_ref[...]).astype(BF16)
        memn_ref[...] = memn
        kv_ref[...] = _dot(memn, w_ref[...]).astype(BF16)

    return pl.pallas_call(
        body,
        name="kv_forward",
        grid=(1,),
        in_specs=[_full((m, d)), _full((1, d)), _full(wkv.shape)],
        out_specs=[_full((m, d)), _full((m, wkv.shape[1]))],
        out_shape=[jax.ShapeDtypeStruct((m, d), BF16), jax.ShapeDtypeStruct((m, wkv.shape[1]), BF16)],
        compiler_params=_params("arbitrary"),
    )(mem, mem_norm, wkv)


def _proj_gather(h, shards):
    s, d = h.shape
    t = min(TILE_PROJ, s)
    n_t = s // t
    n_arr = len(shards)
    x, y, c = _position()
    me = 4 * x + 2 * y + c
    order = jnp.stack([me, me ^ 1, me ^ 4, me ^ 2, me ^ 6, me ^ 5, me ^ 3, me ^ 7]).astype(jnp.int32)

    def body(order_ref, h_ref, *refs):
        del order_ref
        src = refs[:n_arr]
        proj_ref = refs[n_arr]
        out = refs[n_arr + 1 : 2 * n_arr + 1]
        wbuf, send_sems, recv_sems, local_sems, load_sem = refs[2 * n_arr + 1 :]
        j = pl.program_id(0)
        i = pl.program_id(1)
        x, y, c = _position()
        me, sibling = (x, y, c), (x, y, 1 - c)
        chips = [(1 - x, y), (x, 1 - y), (1 - x, 1 - y)]

        def block(a, px, py, pc):
            return out[a].at[4 * px + 2 * py + pc]

        def copy(a, k, owner, to, from_input=False):
            return pltpu.make_async_remote_copy(
                src_ref=src[a] if from_input else block(a, *owner),
                dst_ref=block(a, *owner),
                send_sem=send_sems.at[a, k],
                recv_sem=recv_sems.at[a, k],
                device_id=to,
                device_id_type=MESH,
            )

        def own(a):
            return pltpu.make_async_copy(src[a], block(a, *me), local_sems.at[a])

        def first_sends(a):
            return [copy(a, 0, me, sibling, from_input=True)] + [
                copy(a, 1 + n, me, (*chip, c), from_input=True) for n, chip in enumerate(chips)
            ]

        def passed_on(a, n):
            return copy(a, 4 + n, (*chips[n], c), sibling)

        def load(ref):
            cp = pltpu.make_async_copy(ref, wbuf, load_sem)
            cp.start()
            cp.wait()

        def at_block_start(step):
            return jnp.logical_and(j == step, i == 0)

        @pl.when(at_block_start(0))
        def _():
            for a in range(n_arr):
                own(a).start()
                for cp in first_sends(a):
                    cp.start()
            load(src[0])

        @pl.when(at_block_start(1))
        def _():
            copy(0, 0, sibling, me).wait_recv()
            load(block(0, *sibling))

        for n, chip in enumerate(chips):

            @pl.when(at_block_start(2 + n))
            def _():
                copy(0, 1 + n, (*chip, c), me).wait_recv()
                passed_on(0, n).start()
                load(block(0, *chip, c))

            @pl.when(at_block_start(5 + n))
            def _():
                copy(0, 4 + n, (*chip, 1 - c), me).wait_recv()
                load(block(0, *chip, 1 - c))

        proj_ref[...] = _dot(h_ref[...], wbuf[...])

        @pl.when(jnp.logical_and(j == N_DEV - 1, i == n_t - 1))
        def _():
            for a in range(1, n_arr):
                for n, chip in enumerate(chips):
                    copy(a, 1 + n, (*chip, c), me).wait_recv()
                    passed_on(a, n).start()
            for a in range(1, n_arr):
                copy(a, 0, sibling, me).wait_recv()
                for n, chip in enumerate(chips):
                    copy(a, 4 + n, (*chip, 1 - c), me).wait_recv()
            for a in range(n_arr):
                for cp in first_sends(a) + [passed_on(a, n) for n in range(3)]:
                    cp.wait_send()
                own(a).wait()

    return pl.pallas_call(
        body,
        name="proj_gather",
        grid_spec=pltpu.PrefetchScalarGridSpec(
            num_scalar_prefetch=1,
            grid=(N_DEV, n_t),
            in_specs=[pl.BlockSpec((t, d), lambda j, i, order_ref: (i, 0))] + [ANY] * n_arr,
            out_specs=[pl.BlockSpec((t, WIN_BLK), lambda j, i, order_ref: (i, order_ref[j]))] + [ANY] * n_arr,
            scratch_shapes=[
                pltpu.VMEM(shards[0].shape, BF16),
                pltpu.SemaphoreType.DMA((n_arr, 7)),
                pltpu.SemaphoreType.DMA((n_arr, 7)),
                pltpu.SemaphoreType.DMA((n_arr,)),
                pltpu.SemaphoreType.DMA,
            ],
        ),
        out_shape=[jax.ShapeDtypeStruct((s, IN_WIDTH), F32)]
        + [jax.ShapeDtypeStruct((N_DEV,) + a.shape, a.dtype) for a in shards],
        compiler_params=_params("arbitrary", "arbitrary"),
    )(order, h, *shards)


def _sigmoid(a):
    return jax.nn.sigmoid(a)


def _dsilu(a, sg):
    return sg * (1.0 + a * (1.0 - sg))


def _rms_fwd(u, gain):
    r = lax.rsqrt(_rowmean(u * u) + EPS)
    n = u * r
    return r, n, n * gain


def _rms_bwd(dy, gain, r, n):
    dn = dy * gain
    return _colsum(dy * n), r * (dn - n * _rowmean(dn * n))


def _mix(proj, x, target, kv, wout, pool_w, pool_scale, ln_g, ln_b, sgu_bias, sgu_wm, branch_norm, norm_post):
    s, d = x.shape
    t = min(TILE_MIX, s)
    n_tiles = s // t
    n_chunks = t // SGU_CHUNK
    halo_blocks_per_tile = t // HALO
    inv_d = 1.0 / d
    scale = 1.0 / (XATTN_HEAD_DIM**0.5)

    def body(
        proj_ref, halo_ref, x_ref, tgt_ref, kv_ref, wout_ref, pw_ref, pscale_ref, lng_ref, lnb_ref, bias_ref, wm_ref,
        bnorm_ref, gpost_ref,
        y_ref, dout_ref, dxo_ref, dproj_ref, loss_ref, dgpost_ref, dbnorm_ref, dpscale_ref, dlng_ref, dlnb_ref,
        dpw_ref, dwm_ref, dbias_ref, dkv_ref,
        carry_ref, dzsum_ref,
    ):
        i = pl.program_id(0)
        tile = n_tiles - 1 - i

        @pl.when(i == 0)
        def _():
            carry_ref[...] = jnp.zeros_like(carry_ref)
            dzsum_ref[...] = jnp.zeros_like(dzsum_ref)
            for ref in (loss_ref, dgpost_ref, dbnorm_ref, dpscale_ref, dlng_ref, dlnb_ref, dpw_ref, dwm_ref, dkv_ref):
                ref[...] = jnp.zeros_like(ref)

        t_glob = tile * t + lax.broadcasted_iota(jnp.int32, (t, 1), 0)
        inv_cnt = [1.0 / jnp.minimum(t_glob + 1, w).astype(F32) for w in POOL_WINDOWS]

        xa = proj_ref[:, 0:BRANCH]
        ga = proj_ref[:, BRANCH : 2 * BRANCH]
        halo = jnp.where(tile == 0, 0.0, halo_ref[...])
        d_bf, pm_parts = [], []
        for g, w in enumerate(POOL_WINDOWS):
            cols = slice(g * POOL_GROUP_DIM, (g + 1) * POOL_GROUP_DIM)
            acc = jnp.concatenate([halo[:, cols], xa[:, cols]], axis=0)
            k = 1
            while k < w:
                acc = acc + pltpu.roll(acc, k, axis=0)
                k *= 2
            dg = (acc[HALO:, :] * inv_cnt[g] - xa[:, cols]).astype(BF16)
            d_bf.append(dg)
            pm_parts.append(_dot(dg, pw_ref[g]))
        pm = jnp.concatenate(pm_parts, axis=1)
        pscale = pscale_ref[...]
        pa = pm * pscale
        sga = _sigmoid(ga)
        sila = ga * sga
        ua = pa * sila
        g_a = bnorm_ref[:, 0:BRANCH]
        ra, na, ya = _rms_fwd(ua, g_a)

        u = proj_ref[:, 2 * BRANCH : 3 * BRANCH]
        v = proj_ref[:, 3 * BRANCH : 4 * BRANCH]
        gb = proj_ref[:, 4 * BRANCH : 5 * BRANCH]
        lng = lng_ref[...]
        vc = v - _rowmean(v)
        rstd = lax.rsqrt(_rowmean(vc * vc) + EPS)
        vhat = vc * rstd
        vn_bf = (vhat * lng + lnb_ref[...]).astype(BF16)
        z_rows = []
        for c in range(n_chunks):
            rows = slice(c * SGU_CHUNK, (c + 1) * SGU_CHUNK)
            z_rows.append(
                jnp.concatenate(
                    [
                        _dot(wm_ref[hd], vn_bf[rows, hd * SGU_HEAD_DIM : (hd + 1) * SGU_HEAD_DIM])
                        for hd in range(SGU_HEADS)
                    ],
                    axis=1,
                )
                + bias_ref[...]
            )
        z = z_rows[0] if n_chunks == 1 else jnp.concatenate(z_rows, axis=0)
        sb = u * z
        sgb = _sigmoid(gb)
        silb = gb * sgb
        ub = sb * silb
        g_b = bnorm_ref[:, BRANCH : 2 * BRANCH]
        rb, nb, yb = _rms_fwd(ub, g_b)

        q = proj_ref[:, 5 * BRANCH : 6 * BRANCH]
        gc = proj_ref[:, 6 * BRANCH : 7 * BRANCH]
        q_bf, p_bf, o_parts = [], [], []
        for hd in range(XATTN_HEADS):
            cols = slice(hd * XATTN_HEAD_DIM, (hd + 1) * XATTN_HEAD_DIM)
            qh = q[:, cols].astype(BF16)
            sc = _dot(qh, kv_ref[:, cols], NT) * scale
            e = jnp.exp(sc - jnp.max(sc, axis=-1, keepdims=True))
            p = e / jnp.sum(e, axis=-1, keepdims=True)
            q_bf.append(qh)
            p_bf.append(p.astype(BF16))
            o_parts.append(_dot(p_bf[hd], kv_ref[:, BRANCH + hd * XATTN_HEAD_DIM : BRANCH + (hd + 1) * XATTN_HEAD_DIM]))
        o = jnp.concatenate(o_parts, axis=1)
        sgc = _sigmoid(gc)
        silc = gc * sgc
        uc = o * silc
        g_c = bnorm_ref[:, 2 * BRANCH : 3 * BRANCH]
        rc, nc, yc = _rms_fwd(uc, g_c)

        y_bf = jnp.concatenate([ya, yb, yc], axis=1).astype(BF16)
        y_ref[...] = y_bf
        out = _dot(y_bf, wout_ref[...])
        gpost = gpost_ref[...]
        r_out = lax.rsqrt(_rowmean(out * out) + EPS)
        on = out * r_out
        err = x_ref[...] + on * gpost - tgt_ref[...]
        loss_ref[...] += 0.5 * jnp.sum(_rowmean(err * err), axis=0, keepdims=True)

        dxo = err * inv_d
        dxo_ref[...] = dxo
        dgp, dout = _rms_bwd(dxo, gpost, r_out, on)
        dgpost_ref[...] += dgp
        dout_bf = dout.astype(BF16)
        dout_ref[...] = dout_bf
        dy = _dot(dout_bf, wout_ref[...], NT)

        dg_a, dua = _rms_bwd(dy[:, 0:BRANCH], g_a, ra, na)
        dg_b, dub = _rms_bwd(dy[:, BRANCH : 2 * BRANCH], g_b, rb, nb)
        dg_c, duc = _rms_bwd(dy[:, 2 * BRANCH : 3 * BRANCH], g_c, rc, nc)
        dbnorm_ref[...] += jnp.concatenate([dg_a, dg_b, dg_c], axis=1)

        dpa = dua * sila
        dga = dua * pa * _dsilu(ga, sga)
        dpscale_ref[...] += _colsum(dpa * pm)
        dpm = dpa * pscale
        dxa_parts, carry_parts = [], []
        for g, w in enumerate(POOL_WINDOWS):
            cols = slice(g * POOL_GROUP_DIM, (g + 1) * POOL_GROUP_DIM)
            dpm_g = dpm[:, cols].astype(BF16)
            dd = _dot(dpm_g, pw_ref[g], NT)
            dpw_ref[g] += _dot(d_bf[g], dpm_g, TN)
            cg = dd * inv_cnt[g]
            carry_parts.append(cg[0:HALO, :])
            acc = jnp.concatenate([cg, carry_ref[:, cols]], axis=0)
            k = 1
            while k < w:
                acc = acc + pltpu.roll(acc, t + HALO - k, axis=0)
                k *= 2
            dxa_parts.append(acc[0:t, :] - dd)
        carry_ref[...] = jnp.concatenate(carry_parts, axis=1)
        dxa = jnp.concatenate(dxa_parts, axis=1)

        dsb = dub * silb
        dgb = dub * sb * _dsilu(gb, sgb)
        du = dsb * z
        dz = dsb * u
        dz_bf = dz.astype(BF16)
        dvn_rows = []
        dz_sum = None
        for c in range(n_chunks):
            rows = slice(c * SGU_CHUNK, (c + 1) * SGU_CHUNK)
            dz_sum = dz[rows, :] if dz_sum is None else dz_sum + dz[rows, :]
            parts = []
            for hd in range(SGU_HEADS):
                cols = slice(hd * SGU_HEAD_DIM, (hd + 1) * SGU_HEAD_DIM)
                parts.append(_dot(wm_ref[hd], dz_bf[rows, cols], TN))
                dwm_ref[hd] += _dot(dz_bf[rows, cols], vn_bf[rows, cols], NT)
            dvn_rows.append(jnp.concatenate(parts, axis=1))
        dzsum_ref[...] += dz_sum
        dvn = dvn_rows[0] if n_chunks == 1 else jnp.concatenate(dvn_rows, axis=0)
        dlng_ref[...] += _colsum(dvn * vhat)
        dlnb_ref[...] += _colsum(dvn)
        dvh = dvn * lng
        dv = rstd * (dvh - _rowmean(dvh) - vhat * _rowmean(dvh * vhat))

        do = duc * silc
        dgc = duc * o * _dsilu(gc, sgc)
        dq_parts = []
        for hd in range(XATTN_HEADS):
            cols = slice(hd * XATTN_HEAD_DIM, (hd + 1) * XATTN_HEAD_DIM)
            vcols = slice(BRANCH + hd * XATTN_HEAD_DIM, BRANCH + (hd + 1) * XATTN_HEAD_DIM)
            do_h = do[:, cols].astype(BF16)
            p = p_bf[hd].astype(F32)
            dp = _dot(do_h, kv_ref[:, vcols], NT)
            dkv_ref[:, vcols] += _dot(p_bf[hd], do_h, TN)
            ds_bf = (p * (dp - jnp.sum(dp * p, axis=-1, keepdims=True)) * scale).astype(BF16)
            dq_parts.append(_dot(ds_bf, kv_ref[:, cols]))
            dkv_ref[:, cols] += _dot(ds_bf, q_bf[hd], TN)
        dq = jnp.concatenate(dq_parts, axis=1)

        dproj_ref[...] = jnp.concatenate([dxa, dga, du, dv, dgb, dq, dgc], axis=1).astype(BF16)

        @pl.when(i == n_tiles - 1)
        def _():
            keep = lax.broadcasted_iota(jnp.int32, (SGU_CHUNK, SGU_CHUNK), 0) >= lax.broadcasted_iota(
                jnp.int32, (SGU_CHUNK, SGU_CHUNK), 1
            )
            for hd in range(SGU_HEADS):
                dwm_ref[hd] = jnp.where(keep, dwm_ref[hd], 0.0)
                per_pos = dzsum_ref[:, hd * SGU_HEAD_DIM : (hd + 1) * SGU_HEAD_DIM]
                dbias_ref[hd : hd + 1, :] = _colsum(per_pos.T)

    row_tile = lambda width: pl.BlockSpec((t, width), lambda i: (n_tiles - 1 - i, 0))
    halo_spec = pl.BlockSpec(
        (HALO, BRANCH), lambda i: (jnp.maximum((n_tiles - 1 - i) * halo_blocks_per_tile - 1, 0), 0)
    )
    acc_shapes = [
        (1, 128),
        (1, d),
        (1, MIX_WIDTH),
        (1, BRANCH),
        (1, BRANCH),
        (1, BRANCH),
        pool_w.shape,
        sgu_wm.shape,
        (SGU_HEADS, SGU_CHUNK),
        kv.shape,
    ]
    return pl.pallas_call(
        body,
        name="mix",
        grid=(n_tiles,),
        in_specs=[
            row_tile(IN_WIDTH), halo_spec, row_tile(d), row_tile(d), _resident(kv.shape), _resident(wout.shape),
            _resident(pool_w.shape), _full((1, BRANCH)), _full((1, BRANCH)), _full((1, BRANCH)),
            _resident((SGU_CHUNK, BRANCH)), _resident(sgu_wm.shape), _full((1, MIX_WIDTH)), _full((1, d)),
        ],
        out_specs=[row_tile(MIX_WIDTH), row_tile(d), row_tile(d), row_tile(IN_WIDTH)] + [_full(a) for a in acc_shapes],
        out_shape=[
            jax.ShapeDtypeStruct((s, MIX_WIDTH), BF16),
            jax.ShapeDtypeStruct((s, d), BF16),
            jax.ShapeDtypeStruct((s, d), F32),
            jax.ShapeDtypeStruct((s, IN_WIDTH), BF16),
        ]
        + [jax.ShapeDtypeStruct(a, F32) for a in acc_shapes],
        scratch_shapes=[pltpu.VMEM((HALO, BRANCH), F32), pltpu.VMEM((SGU_CHUNK, BRANCH), F32)],
        compiler_params=_params("arbitrary"),
    )(proj, proj, x, target, kv, wout, pool_w, pool_scale, ln_g, ln_b, sgu_bias, sgu_wm, branch_norm, norm_post)


ANY = pl.BlockSpec(memory_space=pl.ANY)


def _position():
    return lax.axis_index("x"), lax.axis_index("y"), lax.axis_index("c")


def _scatter_specs(partials):
    n = len(partials)
    return dict(
        in_specs=[ANY] * n,
        out_specs=[ANY] * n,
        out_shape=[jax.ShapeDtypeStruct(p.shape, p.dtype) for p in partials],
        scratch_shapes=[pltpu.SemaphoreType.DMA((n, N_DEV)), pltpu.SemaphoreType.DMA((n, N_DEV)), pltpu.SemaphoreType.DMA((n,))]
        if n
        else [],
    )


def _scatter(src, out, send_sems, recv_sems, local_sems):
    x, y, c = _position()
    me = 4 * x + 2 * y + c

    def copies():
        res = []
        for a in range(len(src)):
            for k in range(1, N_DEV):
                px, py, pc = x ^ ((k >> 2) & 1), y ^ ((k >> 1) & 1), c ^ (k & 1)
                res.append(
                    pltpu.make_async_remote_copy(
                        src_ref=src[a].at[4 * px + 2 * py + pc],
                        dst_ref=out[a].at[me],
                        send_sem=send_sems.at[a, k],
                        recv_sem=recv_sems.at[a, k],
                        device_id=(px, py, pc),
                        device_id_type=MESH,
                    )
                )
        return res

    def own():
        return [pltpu.make_async_copy(src[a].at[me], out[a].at[me], local_sems.at[a]) for a in range(len(src))]

    def start():
        for cp in own() + copies():
            cp.start()

    def finish():
        for cp in copies():
            cp.wait_recv()
        for cp in copies():
            cp.wait_send()
        for cp in own():
            cp.wait()

    return start, finish


def _exchange(partials, name):
    n_p = len(partials)
    carried = _scatter_specs(partials)

    def body(*refs):
        start, finish = _scatter(refs[:n_p], refs[n_p : 2 * n_p], *refs[2 * n_p :])
        start()
        finish()

    return pl.pallas_call(
        body,
        name=name,
        in_specs=carried["in_specs"],
        out_specs=carried["out_specs"],
        out_shape=carried["out_shape"],
        scratch_shapes=carried["scratch_shapes"],
    )(*partials)


def _weight_grad(a, b, n_blk, blocked, name, partials):
    s = a.shape[0]
    t = min(TILE_GRAD, s)
    n_t = s // t
    if blocked == "cols":
        k, c = a.shape[1], b.shape[1] // n_blk
        a_spec = pl.BlockSpec((t, k), lambda j, i: (i, 0))
        b_spec = pl.BlockSpec((t, c), lambda j, i: (i, j))
    else:
        k, c = a.shape[1] // n_blk, b.shape[1]
        a_spec = pl.BlockSpec((t, k), lambda j, i: (i, j))
        b_spec = pl.BlockSpec((t, c), lambda j, i: (i, 0))
    n_p = len(partials)
    carried = _scatter_specs(partials)

    def body(a_ref, b_ref, *refs):
        src = refs[:n_p]
        o_ref = refs[n_p]
        landed = refs[n_p + 1 : 2 * n_p + 1]
        acc_ref = refs[2 * n_p + 1]
        start, finish = _scatter(src, landed, *refs[2 * n_p + 2 :])
        j = pl.program_id(0)
        i = pl.program_id(1)

        @pl.when(jnp.logical_and(j == 0, i == 0))
        def _():
            start()

        @pl.when(i == 0)
        def _():
            acc_ref[...] = jnp.zeros_like(acc_ref)

        acc_ref[...] += _dot(a_ref[...], b_ref[...], TN)

        @pl.when(i == n_t - 1)
        def _():
            o_ref[...] = acc_ref[...].astype(BF16)

        @pl.when(jnp.logical_and(j == n_blk - 1, i == n_t - 1))
        def _():
            finish()

    res = pl.pallas_call(
        body,
        name=name,
        grid=(n_blk, n_t),
        in_specs=[a_spec, b_spec] + carried["in_specs"],
        out_specs=[pl.BlockSpec((None, k, c), lambda j, i: (j, 0, 0))] + carried["out_specs"],
        out_shape=[jax.ShapeDtypeStruct((n_blk, k, c), BF16)] + carried["out_shape"],
        scratch_shapes=[pltpu.VMEM((k, c), F32)] + carried["scratch_shapes"],
        compiler_params=_params("arbitrary", "arbitrary"),
    )(a, b, *partials)
    return res[0], res[1:]


def _input_grad(dproj, win, x, dxo, norm_pre, partials):
    s, d = x.shape
    t = min(TILE_GRAD, s)
    n_t = s // t
    n_p = len(partials)
    carried = _scatter_specs(partials)

    def body(dp_ref, w_ref, x_ref, dxo_ref, g_ref, *refs):
        src = refs[:n_p]
        gx_ref, dg_ref = refs[n_p : n_p + 2]
        landed = refs[n_p + 2 : 2 * n_p + 2]
        acc_ref = refs[2 * n_p + 2]
        start, finish = _scatter(src, landed, *refs[2 * n_p + 3 :])
        i = pl.program_id(0)
        j = pl.program_id(1)

        @pl.when(jnp.logical_and(i == 0, j == 0))
        def _():
            start()
            dg_ref[...] = jnp.zeros_like(dg_ref)

        @pl.when(j == 0)
        def _():
            acc_ref[...] = jnp.zeros_like(acc_ref)

        acc_ref[...] += _dot(dp_ref[...], w_ref[...], NT)

        @pl.when(j == N_DEV - 1)
        def _():
            xv = x_ref[...]
            gain = g_ref[...]
            r = lax.rsqrt(_rowmean(xv * xv) + EPS)
            dgain, dx = _rms_bwd(acc_ref[...], gain, r, xv * r)
            dg_ref[...] += dgain
            gx_ref[...] = dxo_ref[...] + dx

        @pl.when(jnp.logical_and(i == n_t - 1, j == N_DEV - 1))
        def _():
            finish()

    res = pl.pallas_call(
        body,
        name="input_grad",
        grid=(n_t, N_DEV),
        in_specs=[
            pl.BlockSpec((t, WIN_BLK), lambda i, j: (i, j)),
            pl.BlockSpec((None, d, WIN_BLK), lambda i, j: (j, 0, 0)),
            pl.BlockSpec((t, d), lambda i, j: (i, 0)),
            pl.BlockSpec((t, d), lambda i, j: (i, 0)),
            _full((1, d)),
        ]
        + carried["in_specs"],
        out_specs=[pl.BlockSpec((t, d), lambda i, j: (i, 0)), _full((1, d))] + carried["out_specs"],
        out_shape=[jax.ShapeDtypeStruct((s, d), F32), jax.ShapeDtypeStruct((1, d), F32)] + carried["out_shape"],
        scratch_shapes=[pltpu.VMEM((t, d), F32)] + carried["scratch_shapes"],
        compiler_params=_params("arbitrary", "arbitrary"),
    )(dproj, win, x, dxo, norm_pre, *partials)
    return res[0], res[1], res[2:]


def _kv_backward(dkv, memn, wkv, mem, mem_norm):
    m, d = mem.shape
    n = wkv.shape[1]

    def body(dkv_ref, memn_ref, w_ref, mem_ref, g_ref, gw_ref, dg_ref):
        dkv_bf = dkv_ref[...].astype(BF16)
        gw_ref[...] = _dot(memn_ref[...], dkv_bf, TN).astype(BF16).reshape(N_DEV, WKV_BLK, n)
        dmemn = _dot(dkv_bf, w_ref[...], NT)
        mv = mem_ref[...]
        r = lax.rsqrt(_rowmean(mv * mv) + EPS)
        dg_ref[...] = _colsum(dmemn * (mv * r))

    return pl.pallas_call(
        body,
        name="kv_backward",
        grid=(1,),
        in_specs=[_full((m, n)), _full((m, d)), _full(wkv.shape), _full((m, d)), _full((1, d))],
        out_specs=[_full((N_DEV, WKV_BLK, n)), _full((1, d))],
        out_shape=[jax.ShapeDtypeStruct((N_DEV, WKV_BLK, n), BF16), jax.ShapeDtypeStruct((1, d), F32)],
        compiler_params=_params("arbitrary"),
    )(dkv, memn, wkv, mem, mem_norm)


def _adamw_math(w, g, m, v):
    m = ADAM_B1 * m + (1.0 - ADAM_B1) * g
    v = ADAM_B2 * v + (1.0 - ADAM_B2) * (g * g)
    m_hat = m / (1.0 - ADAM_B1**ADAM_STEP)
    v_hat = v / (1.0 - ADAM_B2**ADAM_STEP)
    delta = -ADAM_LR * (m_hat / (jnp.sqrt(v_hat) + ADAM_EPS) + ADAM_WD * w)
    return delta, m, v


def _adamw(parts, w, m, v, name):
    r, c = w.shape
    t = r
    while t * c * 4 > TILE_ADAM_BYTES and t % 16 == 0:
        t //= 2

    def body(p_ref, w_ref, m_ref, v_ref, g_ref, d_ref, nm_ref, nv_ref):
        g = p_ref[0].astype(F32)
        for k in range(1, N_DEV):
            g = g + p_ref[k].astype(F32)
        delta, nm, nv = _adamw_math(w_ref[...], g, m_ref[...], v_ref[...])
        g_ref[...] = g
        d_ref[...] = delta
        nm_ref[...] = nm
        nv_ref[...] = nv

    tile = pl.BlockSpec((t, c), lambda i: (i, 0))
    return pl.pallas_call(
        body,
        name=name,
        grid=(r // t,),
        in_specs=[pl.BlockSpec((N_DEV, t, c), lambda i: (0, i, 0)), tile, tile, tile],
        out_specs=[tile] * 4,
        out_shape=[jax.ShapeDtypeStruct((r, c), F32)] * 4,
        compiler_params=_params("parallel"),
    )(parts, w, m, v)


SMALL = ("norm_pre", "pool_scale", "sgu_ln_g", "sgu_ln_b", "sgu_w", "sgu_b", "mem_norm", "branch_norm", "norm_post")


def _local_view(name, w):
    if name == "sgu_w":
        return w.reshape(SGU_HEADS, SGU_CHUNK, SGU_CHUNK)
    if name == "sgu_b":
        return w.reshape(SGU_HEADS, SGU_CHUNK)
    return w.reshape(1, -1)


def _forward_backward(x, mem, target, shards, small):
    causal = jnp.tril(jnp.ones((SGU_CHUNK, SGU_CHUNK), dtype=bool))
    sgu_wm = jnp.where(causal[None], small["sgu_w"], 0.0).astype(BF16)
    sgu_bias = jnp.repeat(jnp.transpose(small["sgu_b"]), SGU_HEAD_DIM, axis=1)

    h = _prep(x, small["norm_pre"])
    proj, win, wkv, pool_all, wout = _proj_gather(h, shards)
    wout = wout.reshape(MIX_WIDTH, D_MODEL)
    wkv = wkv.reshape(D_MODEL, 2 * BRANCH)
    pool_full = (
        pool_all.reshape(N_DEV, len(POOL_WINDOWS), POOL_BLK, POOL_GROUP_DIM)
        .transpose(1, 0, 2, 3)
        .reshape(len(POOL_WINDOWS), POOL_GROUP_DIM, POOL_GROUP_DIM)
    )
    memn, kv = _kv_forward(mem, small["mem_norm"], wkv)
    (y, dout, dxo, dproj, loss, d_norm_post, d_branch_norm, d_pool_scale, d_ln_g, d_ln_b, d_pool_w, d_sgu_w, d_sgu_b,
     dkv) = _mix(
        proj, x, target, kv, wout, pool_full, small["pool_scale"], small["sgu_ln_g"], small["sgu_ln_b"], sgu_bias,
        sgu_wm, small["branch_norm"], small["norm_post"],
    )
    g_wkv, d_mem_norm = _kv_backward(dkv, memn, wkv, mem, small["mem_norm"])
    g_pool = (
        d_pool_w.reshape(len(POOL_WINDOWS), N_DEV, POOL_BLK, POOL_GROUP_DIM)
        .transpose(1, 0, 2, 3)
        .reshape(N_DEV, len(POOL_WINDOWS) * POOL_BLK, POOL_GROUP_DIM)
        .astype(BF16)
    )
    small_grads = dict(
        pool_scale=d_pool_scale, sgu_ln_g=d_ln_g, sgu_ln_b=d_ln_b, sgu_w=d_sgu_w, sgu_b=d_sgu_b,
        mem_norm=d_mem_norm, branch_norm=d_branch_norm, norm_post=d_norm_post,
    )
    packed = jnp.concatenate([small_grads[n].reshape(-1, 128) for n in SMALL if n != "norm_pre"], axis=0)
    packed = jnp.broadcast_to(packed[None], (N_DEV,) + packed.shape)

    g_wout, (l_wkv, l_pool, l_packed) = _weight_grad(y, dout, N_DEV, "rows", "grad_w_out", [g_wkv, g_pool, packed])
    g_win, (l_wout,) = _weight_grad(h, dproj, N_DEV, "cols", "grad_w_in", [g_wout])
    grad_x, d_norm_pre, (l_win,) = _input_grad(dproj, win, x, dxo, small["norm_pre"], [g_win])
    return loss, grad_x, dict(w_in=l_win, w_out=l_wout, w_kv=l_wkv, pool_w=l_pool), l_packed, d_norm_pre


def kernel(x, mem, norm_pre, w_in, pool_w, pool_scale, sgu_ln_g, sgu_ln_b, sgu_w, sgu_b, mem_norm, w_kv, branch_norm, w_out, norm_post, loss_target, m_norm_pre, m_w_in, m_pool_w, m_pool_scale, m_sgu_ln_g, m_sgu_ln_b, m_sgu_w, m_sgu_b, m_mem_norm, m_w_kv, m_branch_norm, m_w_out, m_norm_post, v_norm_pre, v_w_in, v_pool_w, v_pool_scale, v_sgu_ln_g, v_sgu_ln_b, v_sgu_w, v_sgu_b, v_mem_norm, v_w_kv, v_branch_norm, v_w_out, v_norm_post):
    weights = dict(norm_pre=norm_pre, w_in=w_in, pool_w=pool_w, pool_scale=pool_scale, sgu_ln_g=sgu_ln_g, sgu_ln_b=sgu_ln_b, sgu_w=sgu_w, sgu_b=sgu_b, mem_norm=mem_norm, w_kv=w_kv, branch_norm=branch_norm, w_out=w_out, norm_post=norm_post)
    first = dict(norm_pre=m_norm_pre, w_in=m_w_in, pool_w=m_pool_w, pool_scale=m_pool_scale, sgu_ln_g=m_sgu_ln_g, sgu_ln_b=m_sgu_ln_b, sgu_w=m_sgu_w, sgu_b=m_sgu_b, mem_norm=m_mem_norm, w_kv=m_w_kv, branch_norm=m_branch_norm, w_out=m_w_out, norm_post=m_norm_post)
    second = dict(norm_pre=v_norm_pre, w_in=v_w_in, pool_w=v_pool_w, pool_scale=v_pool_scale, sgu_ln_g=v_sgu_ln_g, sgu_ln_b=v_sgu_ln_b, sgu_w=v_sgu_w, sgu_b=v_sgu_b, mem_norm=v_mem_norm, w_kv=v_w_kv, branch_norm=v_branch_norm, w_out=v_w_out, norm_post=v_norm_post)
    order = ("norm_pre", "w_in", "pool_w", "pool_scale", "sgu_ln_g", "sgu_ln_b", "sgu_w", "sgu_b", "mem_norm", "w_kv", "branch_norm", "w_out", "norm_post")

    owned_shape = dict(
        w_in=(D_MODEL, WIN_BLK), w_out=(WOUT_BLK, D_MODEL), w_kv=(WKV_BLK, 2 * BRANCH),
        pool_w=(len(POOL_WINDOWS) * POOL_BLK, POOL_GROUP_DIM),
    )
    owned = {n: weights[n].reshape(owned_shape[n]) for n in owned_shape}
    small = {n: _local_view(n, weights[n]) for n in SMALL}
    loss, grad_x, landed, landed_packed, d_norm_pre = _forward_backward(
        x[0], mem[0], loss_target[0], [owned[n].astype(BF16) for n in ("w_in", "w_kv", "pool_w", "w_out")], small
    )
    d_norm_pre = d_norm_pre.reshape(-1, 128)
    (landed_norm_pre,) = _exchange([jnp.broadcast_to(d_norm_pre[None], (N_DEV,) + d_norm_pre.shape)], "exchange_norm_pre")

    grads, deltas, new_m, new_v = {}, {}, {}, {}
    for n in owned_shape:
        shape = weights[n].shape
        res = _adamw(
            landed[n], owned[n], first[n].reshape(owned_shape[n]), second[n].reshape(owned_shape[n]), "adamw_" + n
        )
        grads[n], deltas[n], new_m[n], new_v[n] = (a.reshape(shape) for a in res)
    rest = [n for n in SMALL if n != "norm_pre"]
    flat = lambda tree, names: jnp.concatenate([tree[n].reshape(-1, 128) for n in names], axis=0)
    res_rest = _adamw(landed_packed, flat(weights, rest), flat(first, rest), flat(second, rest), "adamw_replicated")
    res_pre = _adamw(
        landed_norm_pre, flat(weights, ["norm_pre"]), flat(first, ["norm_pre"]), flat(second, ["norm_pre"]),
        "adamw_norm_pre",
    )
    at = 0
    for n in rest:
        size = weights[n].size // 128
        for tree, a in zip((grads, deltas, new_m, new_v), res_rest):
            tree[n] = a[at : at + size].reshape(weights[n].shape)
        at += size
    for tree, a in zip((grads, deltas, new_m, new_v), res_pre):
        tree["norm_pre"] = a.reshape(weights["norm_pre"].shape)

    total = lax.psum(loss[0, 0], ("x", "y", "c"))
    return (
        total,
        grad_x[None],
        *[grads[n] for n in order],
        *[deltas[n] for n in order],
        *[new_m[n] for n in order],
        *[new_v[n] for n in order],
    )
```

```python
import functools

import jax
import jax.numpy as jnp
from jax import lax
from jax.experimental import pallas as pl
from jax.experimental.pallas import tpu as pltpu

F32 = jnp.float32
BF16 = jnp.bfloat16
EPS = 1e-6

D_MODEL = 2048
POOL_WINDOWS = (2, 4, 8, 16)
POOL_GROUP_DIM = 256
BRANCH = 1024
SGU_CHUNK = 128
SGU_HEADS = 8
SGU_HEAD_DIM = 128
XATTN_HEADS = 4
XATTN_HEAD_DIM = 256
MIX_WIDTH = 3 * BRANCH
IN_WIDTH = 7 * BRANCH
N_DEV = 8
WIN_BLK = IN_WIDTH // N_DEV
WOUT_BLK = MIX_WIDTH // N_DEV
WKV_BLK = D_MODEL // N_DEV
POOL_BLK = POOL_GROUP_DIM // N_DEV
HALO = 16

ADAM_LR = 0.001
ADAM_B1 = 0.9
ADAM_B2 = 0.999
ADAM_EPS = 1e-08
ADAM_WD = 0.01
ADAM_STEP = 10

VMEM_LIMIT_BYTES = 56 * 1024 * 1024

TILE_PREP = 512
TILE_PROJ = 512
TILE_MIX = 128
TILE_GRAD = 512
TILE_ADAM_BYTES = 1 << 20

NN = (((1,), (0,)), ((), ()))
NT = (((1,), (1,)), ((), ()))
TN = (((0,), (0,)), ((), ()))
MESH = pl.DeviceIdType.MESH


def _dot(a, b, dims=NN):
    return lax.dot_general(a, b, dims, preferred_element_type=F32)


def _params(*semantics):
    return pltpu.CompilerParams(dimension_semantics=semantics, vmem_limit_bytes=VMEM_LIMIT_BYTES)


def _rowmean(a):
    return jnp.mean(a, axis=-1, keepdims=True)


def _colsum(a):
    return jnp.sum(a, axis=0, keepdims=True)


def _full(shape):
    zeros = (0,) * len(shape)
    return pl.BlockSpec(shape, lambda *_: zeros)


def _resident(shape):
    zeros = (0,) * len(shape)
    return pl.BlockSpec(shape, lambda *_: zeros, pipeline_mode=pl.Buffered(1))


def _prep(x, norm_pre):
    s, d = x.shape
    t = min(TILE_PREP, s)

    def body(x_ref, g_ref, h_ref):
        xv = x_ref[...]
        r = lax.rsqrt(_rowmean(xv * xv) + EPS)
        h_ref[...] = (xv * r * g_ref[...]).astype(BF16)

    return pl.pallas_call(
        body,
        name="prep",
        grid=(s // t,),
        in_specs=[pl.BlockSpec((t, d), lambda i: (i, 0)), _full((1, d))],
        out_specs=pl.BlockSpec((t, d), lambda i: (i, 0)),
        out_shape=jax.ShapeDtypeStruct((s, d), BF16),
        compiler_params=_params("parallel"),
    )(x, norm_pre)


def _kv_forward(mem, mem_norm, wkv):
    m, d = mem.shape

    def body(mem_ref, g_ref, w_ref, memn_ref, kv_ref):
        mv = mem_ref[...]
        r = lax.rsqrt(_rowmean(mv * mv) + EPS)
        memn = (mv * r * g_ref[...]).astype(BF16)
        memn_ref[...] = memn
        kv_ref[...] = _dot(memn, w_ref[...]).astype(BF16)

    return pl.pallas_call(
        body,
        name="kv_forward",
        grid=(1,),
        in_specs=[_full((m, d)), _full((1, d)), _full(wkv.shape)],
        out_specs=[_full((m, d)), _full((m, wkv.shape[1]))],
        out_shape=[jax.ShapeDtypeStruct((m, d), BF16), jax.ShapeDtypeStruct((m, wkv.shape[1]), BF16)],
        compiler_params=_params("arbitrary"),
    )(mem, mem_norm, wkv)


def _proj_gather(h, shards):
    s, d = h.shape
    t = min(TILE_PROJ, s)
    n_t = s // t
    n_arr = len(shards)
    x, y, c = _position()
    me = 4 * x + 2 * y + c
    order = jnp.stack([me, me ^ 1, me ^ 4, me ^ 2, me ^ 6, me ^ 5, me ^ 3, me ^ 7]).astype(jnp.int32)

    def body(order_ref, h_ref, *refs):
        del order_ref
        src = refs[:n_arr]
        proj_ref = refs[n_arr]
        out = refs[n_arr + 1 : 2 * n_arr + 1]
        wbuf, send_sems, recv_sems, local_sems, load_sem = refs[2 * n_arr + 1 :]
        j = pl.program_id(0)
        i = pl.program_id(1)
        x, y, c = _position()
        me, sibling = (x, y, c), (x, y, 1 - c)
        chips = [(1 - x, y), (x, 1 - y), (1 - x, 1 - y)]

        def block(a, px, py, pc):
            return out[a].at[4 * px + 2 * py + pc]

        def copy(a, k, owner, to, from_input=False):
            return pltpu.make_async_remote_copy(
                src_ref=src[a] if from_input else block(a, *owner),
                dst_ref=block(a, *owner),
                send_sem=send_sems.at[a, k],
                recv_sem=recv_sems.at[a, k],
                device_id=to,
                device_id_type=MESH,
            )

        def own(a):
            return pltpu.make_async_copy(src[a], block(a, *me), local_sems.at[a])

        def first_sends(a):
            return [copy(a, 0, me, sibling, from_input=True)] + [
                copy(a, 1 + n, me, (*chip, c), from_input=True) for n, chip in enumerate(chips)
            ]

        def passed_on(a, n):
            return copy(a, 4 + n, (*chips[n], c), sibling)

        def load(ref):
            cp = pltpu.make_async_copy(ref, wbuf, load_sem)
            cp.start()
            cp.wait()

        def at_block_start(step):
            return jnp.logical_and(j == step, i == 0)

        @pl.when(at_block_start(0))
        def _():
            for a in range(n_arr):
                own(a).start()
                for cp in first_sends(a):
                    cp.start()
            load(src[0])

        @pl.when(at_block_start(1))
        def _():
            copy(0, 0, sibling, me).wait_recv()
            load(block(0, *sibling))

        for n, chip in enumerate(chips):

            @pl.when(at_block_start(2 + n))
            def _():
                copy(0, 1 + n, (*chip, c), me).wait_recv()
                passed_on(0, n).start()
                load(block(0, *chip, c))

            @pl.when(at_block_start(5 + n))
            def _():
                copy(0, 4 + n, (*chip, 1 - c), me).wait_recv()
                load(block(0, *chip, 1 - c))

        proj_ref[...] = _dot(h_ref[...], wbuf[...])

        @pl.when(jnp.logical_and(j == N_DEV - 1, i == n_t - 1))
        def _():
            for a in range(1, n_arr):
                for n, chip in enumerate(chips):
                    copy(a, 1 + n, (*chip, c), me).wait_recv()
                    passed_on(a, n).start()
            for a in range(1, n_arr):
                copy(a, 0, sibling, me).wait_recv()
                for n, chip in enumerate(chips):
                    copy(a, 4 + n, (*chip, 1 - c), me).wait_recv()
            for a in range(n_arr):
                for cp in first_sends(a) + [passed_on(a, n) for n in range(3)]:
                    cp.wait_send()
                own(a).wait()

    return pl.pallas_call(
        body,
        name="proj_gather",
        grid_spec=pltpu.PrefetchScalarGridSpec(
            num_scalar_prefetch=1,
            grid=(N_DEV, n_t),
            in_specs=[pl.BlockSpec((t, d), lambda j, i, order_ref: (i, 0))] + [ANY] * n_arr,
            out_specs=[pl.BlockSpec((t, WIN_BLK), lambda j, i, order_ref: (i, order_ref[j]))] + [ANY] * n_arr,
            scratch_shapes=[
                pltpu.VMEM(shards[0].shape, BF16),
                pltpu.SemaphoreType.DMA((n_arr, 7)),
                pltpu.SemaphoreType.DMA((n_arr, 7)),
                pltpu.SemaphoreType.DMA((n_arr,)),
                pltpu.SemaphoreType.DMA,
            ],
        ),
        out_shape=[jax.ShapeDtypeStruct((s, IN_WIDTH), F32)]
        + [jax.ShapeDtypeStruct((N_DEV,) + a.shape, a.dtype) for a in shards],
        compiler_params=_params("arbitrary", "arbitrary"),
    )(order, h, *shards)


def _sigmoid(a):
    return jax.nn.sigmoid(a)


def _dsilu(a, sg):
    return sg * (1.0 + a * (1.0 - sg))


def _rms_fwd(u, gain):
    r = lax.rsqrt(_rowmean(u * u) + EPS)
    n = u * r
    return r, n, n * gain


def _rms_bwd(dy, gain, r, n):
    dn = dy * gain
    return _colsum(dy * n), r * (dn - n * _rowmean(dn * n))


def _mix(proj, x, target, kv, wout, pool_w, pool_scale, ln_g, ln_b, sgu_bias, sgu_wm, branch_norm, norm_post):
    s, d = x.shape
    t = min(TILE_MIX, s)
    n_tiles = s // t
    n_chunks = t // SGU_CHUNK
    halo_blocks_per_tile = t // HALO
    inv_d = 1.0 / d
    scale = 1.0 / (XATTN_HEAD_DIM**0.5)

    def body(
        proj_ref, halo_ref, x_ref, tgt_ref, kv_ref, wout_ref, pw_ref, pscale_ref, lng_ref, lnb_ref, bias_ref, wm_ref,
        bnorm_ref, gpost_ref,
        y_ref, dout_ref, dxo_ref, dproj_ref, loss_ref, dgpost_ref, dbnorm_ref, dpscale_ref, dlng_ref, dlnb_ref,
        dpw_ref, dwm_ref, dbias_ref, dkv_ref,
        carry_ref, dzsum_ref,
    ):
        i = pl.program_id(0)
        tile = n_tiles - 1 - i

        @pl.when(i == 0)
        def _():
            carry_ref[...] = jnp.zeros_like(carry_ref)
            dzsum_ref[...] = jnp.zeros_like(dzsum_ref)
            for ref in (loss_ref, dgpost_ref, dbnorm_ref, dpscale_ref, dlng_ref, dlnb_ref, dpw_ref, dwm_ref, dkv_ref):
                ref[...] = jnp.zeros_like(ref)

        t_glob = tile * t + lax.broadcasted_iota(jnp.int32, (t, 1), 0)
        inv_cnt = [1.0 / jnp.minimum(t_glob + 1, w).astype(F32) for w in POOL_WINDOWS]

        xa = proj_ref[:, 0:BRANCH]
        ga = proj_ref[:, BRANCH : 2 * BRANCH]
        halo = jnp.where(tile == 0, 0.0, halo_ref[...])
        d_bf, pm_parts = [], []
        for g, w in enumerate(POOL_WINDOWS):
            cols = slice(g * POOL_GROUP_DIM, (g + 1) * POOL_GROUP_DIM)
            acc = jnp.concatenate([halo[:, cols], xa[:, cols]], axis=0)
            k = 1
            while k < w:
                acc = acc + pltpu.roll(acc, k, axis=0)
                k *= 2
            dg = (acc[HALO:, :] * inv_cnt[g] - xa[:, cols]).astype(BF16)
            d_bf.append(dg)
            pm_parts.append(_dot(dg, pw_ref[g]))
        pm = jnp.concatenate(pm_parts, axis=1)
        pscale = pscale_ref[...]
        pa = pm * pscale
        sga = _sigmoid(ga)
        sila = ga * sga
        ua = pa * sila
        g_a = bnorm_ref[:, 0:BRANCH]
        ra, na, ya = _rms_fwd(ua, g_a)

        u = proj_ref[:, 2 * BRANCH : 3 * BRANCH]
        v = proj_ref[:, 3 * BRANCH : 4 * BRANCH]
        gb = proj_ref[:, 4 * BRANCH : 5 * BRANCH]
        lng = lng_ref[...]
        vc = v - _rowmean(v)
        rstd = lax.rsqrt(_rowmean(vc * vc) + EPS)
        vhat = vc * rstd
        vn_bf = (vhat * lng + lnb_ref[...]).astype(BF16)
        z_rows = []
        for c in range(n_chunks):
            rows = slice(c * SGU_CHUNK, (c + 1) * SGU_CHUNK)
            z_rows.append(
                jnp.concatenate(
                    [
                        _dot(wm_ref[hd], vn_bf[rows, hd * SGU_HEAD_DIM : (hd + 1) * SGU_HEAD_DIM])
                        for hd in range(SGU_HEADS)
                    ],
                    axis=1,
                )
                + bias_ref[...]
            )
        z = z_rows[0] if n_chunks == 1 else jnp.concatenate(z_rows, axis=0)
        sb = u * z
        sgb = _sigmoid(gb)
        silb = gb * sgb
        ub = sb * silb
        g_b = bnorm_ref[:, BRANCH : 2 * BRANCH]
        rb, nb, yb = _rms_fwd(ub, g_b)

        q = proj_ref[:, 5 * BRANCH : 6 * BRANCH]
        gc = proj_ref[:, 6 * BRANCH : 7 * BRANCH]
        q_bf, p_bf, o_parts = [], [], []
        for hd in range(XATTN_HEADS):
            cols = slice(hd * XATTN_HEAD_DIM, (hd + 1) * XATTN_HEAD_DIM)
            qh = q[:, cols].astype(BF16)
            sc = _dot(qh, kv_ref[:, cols], NT) * scale
            e = jnp.exp(sc - jnp.max(sc, axis=-1, keepdims=True))
            p = e / jnp.sum(e, axis=-1, keepdims=True)
            q_bf.append(qh)
            p_bf.append(p.astype(BF16))
            o_parts.append(_dot(p_bf[hd], kv_ref[:, BRANCH + hd * XATTN_HEAD_DIM : BRANCH + (hd + 1) * XATTN_HEAD_DIM]))
        o = jnp.concatenate(o_parts, axis=1)
        sgc = _sigmoid(gc)
        silc = gc * sgc
        uc = o * silc
        g_c = bnorm_ref[:, 2 * BRANCH : 3 * BRANCH]
        rc, nc, yc = _rms_fwd(uc, g_c)

        y_bf = jnp.concatenate([ya, yb, yc], axis=1).astype(BF16)
        y_ref[...] = y_bf
        out = _dot(y_bf, wout_ref[...])
        gpost = gpost_ref[...]
        r_out = lax.rsqrt(_rowmean(out * out) + EPS)
        on = out * r_out
        err = x_ref[...] + on * gpost - tgt_ref[...]
        loss_ref[...] += 0.5 * jnp.sum(_rowmean(err * err), axis=0, keepdims=True)

        dxo = err * inv_d
        dxo_ref[...] = dxo
        dgp, dout = _rms_bwd(dxo, gpost, r_out, on)
        dgpost_ref[...] += dgp
        dout_bf = dout.astype(BF16)
        dout_ref[...] = dout_bf
        dy = _dot(dout_bf, wout_ref[...], NT)

        dg_a, dua = _rms_bwd(dy[:, 0:BRANCH], g_a, ra, na)
        dg_b, dub = _rms_bwd(dy[:, BRANCH : 2 * BRANCH], g_b, rb, nb)
        dg_c, duc = _rms_bwd(dy[:, 2 * BRANCH : 3 * BRANCH], g_c, rc, nc)
        dbnorm_ref[...] += jnp.concatenate([dg_a, dg_b, dg_c], axis=1)

        dpa = dua * sila
        dga = dua * pa * _dsilu(ga, sga)
        dpscale_ref[...] += _colsum(dpa * pm)
        dpm = dpa * pscale
        dxa_parts, carry_parts = [], []
        for g, w in enumerate(POOL_WINDOWS):
            cols = slice(g * POOL_GROUP_DIM, (g + 1) * POOL_GROUP_DIM)
            dpm_g = dpm[:, cols].astype(BF16)
            dd = _dot(dpm_g, pw_ref[g], NT)
            dpw_ref[g] += _dot(d_bf[g], dpm_g, TN)
            cg = dd * inv_cnt[g]
            carry_parts.append(cg[0:HALO, :])
            acc = jnp.concatenate([cg, carry_ref[:, cols]], axis=0)
            k = 1
            while k < w:
                acc = acc + pltpu.roll(acc, t + HALO - k, axis=0)
                k *= 2
            dxa_parts.append(acc[0:t, :] - dd)
        carry_ref[...] = jnp.concatenate(carry_parts, axis=1)
        dxa = jnp.concatenate(dxa_parts, axis=1)

        dsb = dub * silb
        dgb = dub * sb * _dsilu(gb, sgb)
        du = dsb * z
        dz = dsb * u
        dz_bf = dz.astype(BF16)
        dvn_rows = []
        dz_sum = None
        for c in range(n_chunks):
            rows = slice(c * SGU_CHUNK, (c + 1) * SGU_CHUNK)
            dz_sum = dz[rows, :] if dz_sum is None else dz_sum + dz[rows, :]
            parts = []
            for hd in range(SGU_HEADS):
                cols = slice(hd * SGU_HEAD_DIM, (hd + 1) * SGU_HEAD_DIM)
                parts.append(_dot(wm_ref[hd], dz_bf[rows, cols], TN))
                dwm_ref[hd] += _dot(dz_bf[rows, cols], vn_bf[rows, cols], NT)
            dvn_rows.append(jnp.concatenate(parts, axis=1))
        dzsum_ref[...] += dz_sum
        dvn = dvn_rows[0] if n_chunks == 1 else jnp.concatenate(dvn_rows, axis=0)
        dlng_ref[...] += _colsum(dvn * vhat)
        dlnb_ref[...] += _colsum(dvn)
        dvh = dvn * lng
        dv = rstd * (dvh - _rowmean(dvh) - vhat * _rowmean(dvh * vhat))

        do = duc * silc
        dgc = duc * o * _dsilu(gc, sgc)
        dq_parts = []
        for hd in range(XATTN_HEADS):
            cols = slice(hd * XATTN_HEAD_DIM, (hd + 1) * XATTN_HEAD_DIM)
            vcols = slice(BRANCH + hd * XATTN_HEAD_DIM, BRANCH + (hd + 1) * XATTN_HEAD_DIM)
            do_h = do[:, cols].astype(BF16)
            p = p_bf[hd].astype(F32)
            dp = _dot(do_h, kv_ref[:, vcols], NT)
            dkv_ref[:, vcols] += _dot(p_bf[hd], do_h, TN)
            ds_bf = (p * (dp - jnp.sum(dp * p, axis=-1, keepdims=True)) * scale).astype(BF16)
            dq_parts.append(_dot(ds_bf, kv_ref[:, cols]))
            dkv_ref[:, cols] += _dot(ds_bf, q_bf[hd], TN)
        dq = jnp.concatenate(dq_parts, axis=1)

        dproj_ref[...] = jnp.concatenate([dxa, dga, du, dv, dgb, dq, dgc], axis=1).astype(BF16)

        @pl.when(i == n_tiles - 1)
        def _():
            keep = lax.broadcasted_iota(jnp.int32, (SGU_CHUNK, SGU_CHUNK), 0) >= lax.broadcasted_iota(
                jnp.int32, (SGU_CHUNK, SGU_CHUNK), 1
            )
            for hd in range(SGU_HEADS):
                dwm_ref[hd] = jnp.where(keep, dwm_ref[hd], 0.0)
                per_pos = dzsum_ref[:, hd * SGU_HEAD_DIM : (hd + 1) * SGU_HEAD_DIM]
                dbias_ref[hd : hd + 1, :] = _colsum(per_pos.T)

    row_tile = lambda width: pl.BlockSpec((t, width), lambda i: (n_tiles - 1 - i, 0))
    halo_spec = pl.BlockSpec(
        (HALO, BRANCH), lambda i: (jnp.maximum((n_tiles - 1 - i) * halo_blocks_per_tile - 1, 0), 0)
    )
    acc_shapes = [
        (1, 128),
        (1, d),
        (1, MIX_WIDTH),
        (1, BRANCH),
        (1, BRANCH),
        (1, BRANCH),
        pool_w.shape,
        sgu_wm.shape,
        (SGU_HEADS, SGU_CHUNK),
        kv.shape,
    ]
    return pl.pallas_call(
        body,
        name="mix",
        grid=(n_tiles,),
        in_specs=[
            row_tile(IN_WIDTH), halo_spec, row_tile(d), row_tile(d), _resident(kv.shape), _resident(wout.shape),
            _resident(pool_w.shape), _full((1, BRANCH)), _full((1, BRANCH)), _full((1, BRANCH)),
            _resident((SGU_CHUNK, BRANCH)), _resident(sgu_wm.shape), _full((1, MIX_WIDTH)), _full((1, d)),
        ],
        out_specs=[row_tile(MIX_WIDTH), row_tile(d), row_tile(d), row_tile(IN_WIDTH)] + [_full(a) for a in acc_shapes],
        out_shape=[
            jax.ShapeDtypeStruct((s, MIX_WIDTH), BF16),
            jax.ShapeDtypeStruct((s, d), BF16),
            jax.ShapeDtypeStruct((s, d), F32),
            jax.ShapeDtypeStruct((s, IN_WIDTH), BF16),
        ]
        + [jax.ShapeDtypeStruct(a, F32) for a in acc_shapes],
        scratch_shapes=[pltpu.VMEM((HALO, BRANCH), F32), pltpu.VMEM((SGU_CHUNK, BRANCH), F32)],
        compiler_params=_params("arbitrary"),
    )(proj, proj, x, target, kv, wout, pool_w, pool_scale, ln_g, ln_b, sgu_bias, sgu_wm, branch_norm, norm_post)


ANY = pl.BlockSpec(memory_space=pl.ANY)


def _position():
    return lax.axis_index("x"), lax.axis_index("y"), lax.axis_index("c")


N_CHIPS = 4


def _landing_shape(kind, a):
    return (N_CHIPS,) + a.shape[2:] if kind == "pair" else a.shape


def _carry_specs(groups):
    arrays = [(kind, a) for kind, arrs in groups for a in arrs]
    scratch = []
    for _, arrs in groups:
        n = len(arrs)
        scratch += [pltpu.SemaphoreType.DMA((n, N_DEV)), pltpu.SemaphoreType.DMA((n, N_DEV)), pltpu.SemaphoreType.DMA((n,))]
    return dict(
        n=len(arrays),
        operands=[a for _, a in arrays],
        in_specs=[ANY] * len(arrays),
        out_specs=[ANY] * len(arrays),
        out_shape=[jax.ShapeDtypeStruct(_landing_shape(kind, a), a.dtype) for kind, a in arrays],
        scratch_shapes=scratch,
    )


def _carry(groups, src, out, sems):
    x, y, c = _position()
    chip = 2 * x + y
    me = 2 * chip + c

    def remote(s, d, send_sems, recv_sems, a, m, to):
        return pltpu.make_async_remote_copy(
            src_ref=s, dst_ref=d, send_sem=send_sems.at[a, m], recv_sem=recv_sems.at[a, m], device_id=to,
            device_id_type=MESH,
        )

    def copies():
        far, near = [], []
        at = 0
        for g, (kind, arrs) in enumerate(groups):
            send_sems, recv_sems, local_sems = sems[3 * g : 3 * g + 3]
            for a in range(len(arrs)):
                s, d = src[at + a], out[at + a]
                if kind == "pair":
                    far.append(remote(s.at[:, 1 - c], d, send_sems, recv_sems, a, 1, (x, y, 1 - c)))
                elif kind == "chips":
                    for m in range(1, N_CHIPS):
                        px, py = x ^ (m >> 1), y ^ (m & 1)
                        far.append(remote(s.at[2 * px + py], d.at[chip], send_sems, recv_sems, a, m, (px, py, c)))
                    near.append(pltpu.make_async_copy(s.at[chip], d.at[chip], local_sems.at[a]))
                else:
                    for m in range(1, N_DEV):
                        px, py, pc = x ^ ((m >> 2) & 1), y ^ ((m >> 1) & 1), c ^ (m & 1)
                        far.append(
                            remote(s.at[4 * px + 2 * py + pc], d.at[me], send_sems, recv_sems, a, m, (px, py, pc))
                        )
                    near.append(pltpu.make_async_copy(s.at[me], d.at[me], local_sems.at[a]))
            at += len(arrs)
        return far, near

    def start():
        far, near = copies()
        for cp in near + far:
            cp.start()

    def finish():
        far, near = copies()
        for cp in far:
            cp.wait_recv()
        for cp in far:
            cp.wait_send()
        for cp in near:
            cp.wait()

    return start, finish


def _exchange(groups, name):
    carried = _carry_specs(groups)
    n_c = carried["n"]

    def body(*refs):
        start, finish = _carry(groups, refs[:n_c], refs[n_c : 2 * n_c], refs[2 * n_c :])
        start()
        finish()

    return pl.pallas_call(
        body,
        name=name,
        in_specs=carried["in_specs"],
        out_specs=carried["out_specs"],
        out_shape=carried["out_shape"],
        scratch_shapes=carried["scratch_shapes"],
    )(*carried["operands"])


def _pair_sum(mine, theirs, name, groups=()):
    n = len(mine)
    carried = _carry_specs(groups)
    n_c = carried["n"]
    core = lax.axis_index("c").astype(jnp.int32).reshape(1)

    def body(core_ref, *refs):
        del core_ref
        own = refs[:n]
        sib = refs[n : 2 * n]
        src = refs[2 * n : 2 * n + n_c]
        out = refs[2 * n + n_c : 3 * n + n_c]
        landed = refs[3 * n + n_c : 3 * n + 2 * n_c]
        start, finish = _carry(groups, src, landed, refs[3 * n + 2 * n_c :])
        b = pl.program_id(0)

        @pl.when(b == 0)
        def _():
            start()

        for a in range(n):
            out[a][...] = (own[a][...].astype(F32) + sib[a][...].astype(F32)).astype(out[a].dtype)

        @pl.when(b == N_CHIPS - 1)
        def _():
            finish()

    block = lambda a: pl.BlockSpec((None,) + a.shape[1:], lambda b, core_ref: (b, 0, 0))
    res = pl.pallas_call(
        body,
        name=name,
        grid_spec=pltpu.PrefetchScalarGridSpec(
            num_scalar_prefetch=1,
            grid=(N_CHIPS,),
            in_specs=[pl.BlockSpec((None, None) + a.shape[2:], lambda b, core_ref: (b, core_ref[0], 0, 0)) for a in mine]
            + [block(a) for a in theirs]
            + carried["in_specs"],
            out_specs=[block(a) for a in theirs] + carried["out_specs"],
            scratch_shapes=carried["scratch_shapes"],
        ),
        out_shape=[jax.ShapeDtypeStruct(a.shape, a.dtype) for a in theirs] + carried["out_shape"],
        compiler_params=_params("arbitrary"),
    )(core, *mine, *theirs, *carried["operands"])
    return res[:n], res[n:]


def _weight_grad(a, b, n_blk, blocked, name, groups):
    s = a.shape[0]
    t = min(TILE_GRAD, s)
    n_t = s // t
    if blocked == "cols":
        k, c = a.shape[1], b.shape[1] // n_blk
        a_spec = pl.BlockSpec((t, k), lambda j, i: (i, 0))
        b_spec = pl.BlockSpec((t, c), lambda j, i: (i, j))
    else:
        k, c = a.shape[1] // n_blk, b.shape[1]
        a_spec = pl.BlockSpec((t, k), lambda j, i: (i, j))
        b_spec = pl.BlockSpec((t, c), lambda j, i: (i, 0))
    carried = _carry_specs(groups)
    n_p = carried["n"]

    def body(a_ref, b_ref, *refs):
        src = refs[:n_p]
        o_ref = refs[n_p]
        landed = refs[n_p + 1 : 2 * n_p + 1]
        acc_ref = refs[2 * n_p + 1]
        start, finish = _carry(groups, src, landed, refs[2 * n_p + 2 :])
        j = pl.program_id(0)
        i = pl.program_id(1)

        @pl.when(jnp.logical_and(j == 0, i == 0))
        def _():
            start()

        @pl.when(i == 0)
        def _():
            acc_ref[...] = jnp.zeros_like(acc_ref)

        acc_ref[...] += _dot(a_ref[...], b_ref[...], TN)

        @pl.when(i == n_t - 1)
        def _():
            o_ref[...] = acc_ref[...].astype(BF16)

        @pl.when(jnp.logical_and(j == n_blk - 1, i == n_t - 1))
        def _():
            finish()

    res = pl.pallas_call(
        body,
        name=name,
        grid=(n_blk, n_t),
        in_specs=[a_spec, b_spec] + carried["in_specs"],
        out_specs=[pl.BlockSpec((None, k, c), lambda j, i: (j, 0, 0))] + carried["out_specs"],
        out_shape=[jax.ShapeDtypeStruct((n_blk, k, c), BF16)] + carried["out_shape"],
        scratch_shapes=[pltpu.VMEM((k, c), F32)] + carried["scratch_shapes"],
        compiler_params=_params("arbitrary", "arbitrary"),
    )(a, b, *carried["operands"])
    return res[0], res[1:]


def _input_grad(dproj, win, x, dxo, norm_pre, groups):
    s, d = x.shape
    t = min(TILE_GRAD, s)
    n_t = s // t
    carried = _carry_specs(groups)
    n_p = carried["n"]

    def body(dp_ref, w_ref, x_ref, dxo_ref, g_ref, *refs):
        src = refs[:n_p]
        gx_ref, dg_ref = refs[n_p : n_p + 2]
        landed = refs[n_p + 2 : 2 * n_p + 2]
        acc_ref = refs[2 * n_p + 2]
        start, finish = _carry(groups, src, landed, refs[2 * n_p + 3 :])
        i = pl.program_id(0)
        j = pl.program_id(1)

        @pl.when(jnp.logical_and(i == 0, j == 0))
        def _():
            start()
            dg_ref[...] = jnp.zeros_like(dg_ref)

        @pl.when(j == 0)
        def _():
            acc_ref[...] = jnp.zeros_like(acc_ref)

        acc_ref[...] += _dot(dp_ref[...], w_ref[...], NT)

        @pl.when(j == N_DEV - 1)
        def _():
            xv = x_ref[...]
            gain = g_ref[...]
            r = lax.rsqrt(_rowmean(xv * xv) + EPS)
            dgain, dx = _rms_bwd(acc_ref[...], gain, r, xv * r)
            dg_ref[...] += dgain
            gx_ref[...] = dxo_ref[...] + dx

        @pl.when(jnp.logical_and(i == n_t - 1, j == N_DEV - 1))
        def _():
            finish()

    res = pl.pallas_call(
        body,
        name="input_grad",
        grid=(n_t, N_DEV),
        in_specs=[
            pl.BlockSpec((t, WIN_BLK), lambda i, j: (i, j)),
            pl.BlockSpec((None, d, WIN_BLK), lambda i, j: (j, 0, 0)),
            pl.BlockSpec((t, d), lambda i, j: (i, 0)),
            pl.BlockSpec((t, d), lambda i, j: (i, 0)),
            _full((1, d)),
        ]
        + carried["in_specs"],
        out_specs=[pl.BlockSpec((t, d), lambda i, j: (i, 0)), _full((1, d))] + carried["out_specs"],
        out_shape=[jax.ShapeDtypeStruct((s, d), F32), jax.ShapeDtypeStruct((1, d), F32)] + carried["out_shape"],
        scratch_shapes=[pltpu.VMEM((t, d), F32)] + carried["scratch_shapes"],
        compiler_params=_params("arbitrary", "arbitrary"),
    )(dproj, win, x, dxo, norm_pre, *carried["operands"])
    return res[0], res[1], res[2:]


def _kv_backward(dkv, memn, wkv, mem, mem_norm):
    m, d = mem.shape
    n = wkv.shape[1]

    def body(dkv_ref, memn_ref, w_ref, mem_ref, g_ref, gw_ref, dg_ref):
        dkv_bf = dkv_ref[...].astype(BF16)
        gw_ref[...] = _dot(memn_ref[...], dkv_bf, TN).astype(BF16).reshape(N_DEV, WKV_BLK, n)
        dmemn = _dot(dkv_bf, w_ref[...], NT)
        mv = mem_ref[...]
        r = lax.rsqrt(_rowmean(mv * mv) + EPS)
        dg_ref[...] = _colsum(dmemn * (mv * r))

    return pl.pallas_call(
        body,
        name="kv_backward",
        grid=(1,),
        in_specs=[_full((m, n)), _full((m, d)), _full(wkv.shape), _full((m, d)), _full((1, d))],
        out_specs=[_full((N_DEV, WKV_BLK, n)), _full((1, d))],
        out_shape=[jax.ShapeDtypeStruct((N_DEV, WKV_BLK, n), BF16), jax.ShapeDtypeStruct((1, d), F32)],
        compiler_params=_params("arbitrary"),
    )(dkv, memn, wkv, mem, mem_norm)


def _adamw_math(w, g, m, v):
    m = ADAM_B1 * m + (1.0 - ADAM_B1) * g
    v = ADAM_B2 * v + (1.0 - ADAM_B2) * (g * g)
    m_hat = m / (1.0 - ADAM_B1**ADAM_STEP)
    v_hat = v / (1.0 - ADAM_B2**ADAM_STEP)
    delta = -ADAM_LR * (m_hat / (jnp.sqrt(v_hat) + ADAM_EPS) + ADAM_WD * w)
    return delta, m, v


def _adamw(parts, w, m, v, name):
    r, c = w.shape
    slots = parts.shape[0]
    t = r
    while t * c * 4 > TILE_ADAM_BYTES and t % 16 == 0:
        t //= 2

    def body(p_ref, w_ref, m_ref, v_ref, g_ref, d_ref, nm_ref, nv_ref):
        g = p_ref[0].astype(F32)
        for k in range(1, slots):
            g = g + p_ref[k].astype(F32)
        delta, nm, nv = _adamw_math(w_ref[...], g, m_ref[...], v_ref[...])
        g_ref[...] = g
        d_ref[...] = delta
        nm_ref[...] = nm
        nv_ref[...] = nv

    tile = pl.BlockSpec((t, c), lambda i: (i, 0))
    return pl.pallas_call(
        body,
        name=name,
        grid=(r // t,),
        in_specs=[pl.BlockSpec((slots, t, c), lambda i: (0, i, 0)), tile, tile, tile],
        out_specs=[tile] * 4,
        out_shape=[jax.ShapeDtypeStruct((r, c), F32)] * 4,
        compiler_params=_params("parallel"),
    )(parts, w, m, v)


SMALL = ("norm_pre", "pool_scale", "sgu_ln_g", "sgu_ln_b", "sgu_w", "sgu_b", "mem_norm", "branch_norm", "norm_post")


def _local_view(name, w):
    if name == "sgu_w":
        return w.reshape(SGU_HEADS, SGU_CHUNK, SGU_CHUNK)
    if name == "sgu_b":
        return w.reshape(SGU_HEADS, SGU_CHUNK)
    return w.reshape(1, -1)


def _forward_backward(x, mem, target, shards, small):
    causal = jnp.tril(jnp.ones((SGU_CHUNK, SGU_CHUNK), dtype=bool))
    sgu_wm = jnp.where(causal[None], small["sgu_w"], 0.0).astype(BF16)
    sgu_bias = jnp.repeat(jnp.transpose(small["sgu_b"]), SGU_HEAD_DIM, axis=1)

    h = _prep(x, small["norm_pre"])
    proj, win, wkv, pool_all, wout = _proj_gather(h, shards)
    wout = wout.reshape(MIX_WIDTH, D_MODEL)
    wkv = wkv.reshape(D_MODEL, 2 * BRANCH)
    pool_full = (
        pool_all.reshape(N_DEV, len(POOL_WINDOWS), POOL_BLK, POOL_GROUP_DIM)
        .transpose(1, 0, 2, 3)
        .reshape(len(POOL_WINDOWS), POOL_GROUP_DIM, POOL_GROUP_DIM)
    )
    memn, kv = _kv_forward(mem, small["mem_norm"], wkv)
    (y, dout, dxo, dproj, loss, d_norm_post, d_branch_norm, d_pool_scale, d_ln_g, d_ln_b, d_pool_w, d_sgu_w, d_sgu_b,
     dkv) = _mix(
        proj, x, target, kv, wout, pool_full, small["pool_scale"], small["sgu_ln_g"], small["sgu_ln_b"], sgu_bias,
        sgu_wm, small["branch_norm"], small["norm_post"],
    )
    g_wkv, d_mem_norm = _kv_backward(dkv, memn, wkv, mem, small["mem_norm"])
    g_pool = (
        d_pool_w.reshape(len(POOL_WINDOWS), N_DEV, POOL_BLK, POOL_GROUP_DIM)
        .transpose(1, 0, 2, 3)
        .reshape(N_DEV, len(POOL_WINDOWS) * POOL_BLK, POOL_GROUP_DIM)
        .astype(BF16)
    )
    small_grads = dict(
        pool_scale=d_pool_scale, sgu_ln_g=d_ln_g, sgu_ln_b=d_ln_b, sgu_w=d_sgu_w, sgu_b=d_sgu_b,
        mem_norm=d_mem_norm, branch_norm=d_branch_norm, norm_post=d_norm_post,
    )
    packed = jnp.concatenate([small_grads[n].reshape(-1, 128) for n in SMALL if n != "norm_pre"], axis=0)
    packed = jnp.broadcast_to(packed[None, None], (N_CHIPS, 2) + packed.shape)

    by_chip = lambda g: g.reshape((N_CHIPS, 2) + g.shape[1:])
    small_mine = [by_chip(g_wkv), by_chip(g_pool), packed]
    g_win, small_theirs = _weight_grad(h, dproj, N_DEV, "cols", "grad_w_in", [("pair", small_mine)])
    small_sums, _ = _pair_sum(small_mine, small_theirs, "pair_sum_small")
    g_wout, (l_wkv, l_pool, l_packed, win_theirs) = _weight_grad(
        y, dout, N_DEV, "rows", "grad_w_out", [("chips", list(small_sums)), ("pair", [by_chip(g_win)])]
    )
    (win_sums,), (wout_theirs,) = _pair_sum(
        [by_chip(g_win)], [win_theirs], "pair_sum_w_in", [("pair", [by_chip(g_wout)])]
    )
    (wout_sums,), _ = _pair_sum([by_chip(g_wout)], [wout_theirs], "pair_sum_w_out")
    grad_x, d_norm_pre, (l_win, l_wout) = _input_grad(
        dproj, win, x, dxo, small["norm_pre"], [("chips", [win_sums, wout_sums])]
    )
    return loss, grad_x, dict(w_in=l_win, w_out=l_wout, w_kv=l_wkv, pool_w=l_pool), l_packed, d_norm_pre


def kernel(x, mem, norm_pre, w_in, pool_w, pool_scale, sgu_ln_g, sgu_ln_b, sgu_w, sgu_b, mem_norm, w_kv, branch_norm, w_out, norm_post, loss_target, m_norm_pre, m_w_in, m_pool_w, m_pool_scale, m_sgu_ln_g, m_sgu_ln_b, m_sgu_w, m_sgu_b, m_mem_norm, m_w_kv, m_branch_norm, m_w_out, m_norm_post, v_norm_pre, v_w_in, v_pool_w, v_pool_scale, v_sgu_ln_g, v_sgu_ln_b, v_sgu_w, v_sgu_b, v_mem_norm, v_w_kv, v_branch_norm, v_w_out, v_norm_post):
    weights = dict(norm_pre=norm_pre, w_in=w_in, pool_w=pool_w, pool_scale=pool_scale, sgu_ln_g=sgu_ln_g, sgu_ln_b=sgu_ln_b, sgu_w=sgu_w, sgu_b=sgu_b, mem_norm=mem_norm, w_kv=w_kv, branch_norm=branch_norm, w_out=w_out, norm_post=norm_post)
    first = dict(norm_pre=m_norm_pre, w_in=m_w_in, pool_w=m_pool_w, pool_scale=m_pool_scale, sgu_ln_g=m_sgu_ln_g, sgu_ln_b=m_sgu_ln_b, sgu_w=m_sgu_w, sgu_b=m_sgu_b, mem_norm=m_mem_norm, w_kv=m_w_kv, branch_norm=m_branch_norm, w_out=m_w_out, norm_post=m_norm_post)
    second = dict(norm_pre=v_norm_pre, w_in=v_w_in, pool_w=v_pool_w, pool_scale=v_pool_scale, sgu_ln_g=v_sgu_ln_g, sgu_ln_b=v_sgu_ln_b, sgu_w=v_sgu_w, sgu_b=v_sgu_b, mem_norm=v_mem_norm, w_kv=v_w_kv, branch_norm=v_branch_norm, w_out=v_w_out, norm_post=v_norm_post)
    order = ("norm_pre", "w_in", "pool_w", "pool_scale", "sgu_ln_g", "sgu_ln_b", "sgu_w", "sgu_b", "mem_norm", "w_kv", "branch_norm", "w_out", "norm_post")

    owned_shape = dict(
        w_in=(D_MODEL, WIN_BLK), w_out=(WOUT_BLK, D_MODEL), w_kv=(WKV_BLK, 2 * BRANCH),
        pool_w=(len(POOL_WINDOWS) * POOL_BLK, POOL_GROUP_DIM),
    )
    owned = {n: weights[n].reshape(owned_shape[n]) for n in owned_shape}
    small = {n: _local_view(n, weights[n]) for n in SMALL}
    loss, grad_x, landed, landed_packed, d_norm_pre = _forward_backward(
        x[0], mem[0], loss_target[0], [owned[n].astype(BF16) for n in ("w_in", "w_kv", "pool_w", "w_out")], small
    )
    d_norm_pre = d_norm_pre.reshape(-1, 128)
    (landed_norm_pre,) = _exchange(
        [("all", [jnp.broadcast_to(d_norm_pre[None], (N_DEV,) + d_norm_pre.shape)])], "exchange_norm_pre"
    )

    grads, deltas, new_m, new_v = {}, {}, {}, {}
    for n in owned_shape:
        shape = weights[n].shape
        res = _adamw(
            landed[n], owned[n], first[n].reshape(owned_shape[n]), second[n].reshape(owned_shape[n]), "adamw_" + n
        )
        grads[n], deltas[n], new_m[n], new_v[n] = (a.reshape(shape) for a in res)
    rest = [n for n in SMALL if n != "norm_pre"]
    flat = lambda tree, names: jnp.concatenate([tree[n].reshape(-1, 128) for n in names], axis=0)
    res_rest = _adamw(landed_packed, flat(weights, rest), flat(first, rest), flat(second, rest), "adamw_replicated")
    res_pre = _adamw(
        landed_norm_pre, flat(weights, ["norm_pre"]), flat(first, ["norm_pre"]), flat(second, ["norm_pre"]),
        "adamw_norm_pre",
    )
    at = 0
    for n in rest:
        size = weights[n].size // 128
        for tree, a in zip((grads, deltas, new_m, new_v), res_rest):
            tree[n] = a[at : at + size].reshape(weights[n].shape)
        at += size
    for tree, a in zip((grads, deltas, new_m, new_v), res_pre):
        tree["norm_pre"] = a.reshape(weights["norm_pre"].shape)

    total = lax.psum(loss[0, 0], ("x", "y", "c"))
    return (
        total,
        grad_x[None],
        *[grads[n] for n in order],
        *[deltas[n] for n in order],
        *[new_m[n] for n in order],
        *[new_v[n] for n in order],
    )
```

```python
import functools

import jax
import jax.numpy as jnp
from jax import lax
from jax.experimental import pallas as pl
from jax.experimental.pallas import tpu as pltpu

F32 = jnp.float32
BF16 = jnp.bfloat16
EPS = 1e-6

D_MODEL = 2048
POOL_WINDOWS = (2, 4, 8, 16)
POOL_GROUP_DIM = 256
BRANCH = 1024
SGU_CHUNK = 128
SGU_HEADS = 8
SGU_HEAD_DIM = 128
XATTN_HEADS = 4
XATTN_HEAD_DIM = 256
MIX_WIDTH = 3 * BRANCH
IN_WIDTH = 7 * BRANCH
N_DEV = 8
WIN_BLK = IN_WIDTH // N_DEV
WOUT_BLK = MIX_WIDTH // N_DEV
WKV_BLK = D_MODEL // N_DEV
POOL_BLK = POOL_GROUP_DIM // N_DEV
HALO = 16

ADAM_LR = 0.001
ADAM_B1 = 0.9
ADAM_B2 = 0.999
ADAM_EPS = 1e-08
ADAM_WD = 0.01
ADAM_STEP = 10

VMEM_LIMIT_BYTES = 56 * 1024 * 1024

TILE_PROJ = 512
TILE_MIX = 128
TILE_GRAD = 512
TILE_ADAM_BYTES = 1 << 20

NN = (((1,), (0,)), ((), ()))
NT = (((1,), (1,)), ((), ()))
TN = (((0,), (0,)), ((), ()))
MESH = pl.DeviceIdType.MESH


def _dot(a, b, dims=NN):
    return lax.dot_general(a, b, dims, preferred_element_type=F32)


def _params(*semantics):
    return pltpu.CompilerParams(dimension_semantics=semantics, vmem_limit_bytes=VMEM_LIMIT_BYTES)


def _rowmean(a):
    return jnp.mean(a, axis=-1, keepdims=True)


def _colsum(a):
    return jnp.sum(a, axis=0, keepdims=True)


def _full(shape):
    zeros = (0,) * len(shape)
    return pl.BlockSpec(shape, lambda *_: zeros)


def _resident(shape):
    zeros = (0,) * len(shape)
    return pl.BlockSpec(shape, lambda *_: zeros, pipeline_mode=pl.Buffered(1))


def _kv_forward(mem, mem_norm, wkv):
    m, d = mem.shape

    def body(mem_ref, g_ref, w_ref, memn_ref, kv_ref):
        mv = mem_ref[...]
        r = lax.rsqrt(_rowmean(mv * mv) + EPS)
        memn = (mv * r * g_ref[...]).astype(BF16)
        memn_ref[...] = memn
        kv_ref[...] = _dot(memn, w_ref[...]).astype(BF16)

    return pl.pallas_call(
        body,
        name="kv_forward",
        grid=(1,),
        in_specs=[_full((m, d)), _full((1, d)), _full(wkv.shape)],
        out_specs=[_full((m, d)), _full((m, wkv.shape[1]))],
        out_shape=[jax.ShapeDtypeStruct((m, d), BF16), jax.ShapeDtypeStruct((m, wkv.shape[1]), BF16)],
        compiler_params=_params("arbitrary"),
    )(mem, mem_norm, wkv)


def _proj_gather(x_in, norm_pre, shards):
    s, d = x_in.shape
    t = min(TILE_PROJ, s)
    n_t = s // t
    n_arr = len(shards)

    def places(x, y, c):
        return (x, y, c), (x, y, 1 - c), (x ^ c, y ^ (1 - c)), (x ^ (1 - c), y ^ c), (1 - x, 1 - y)

    def index(chip, core):
        return 4 * chip[0] + 2 * chip[1] + core

    _, _, chip_a, chip_b, chip_d = places(*_position())
    c_out = lax.axis_index("c")
    me_out = index((lax.axis_index("x"), lax.axis_index("y")), c_out)
    order = jnp.stack(
        [
            me_out, me_out ^ 1, index(chip_a, c_out), index(chip_b, c_out), index(chip_b, 1 - c_out),
            index(chip_a, 1 - c_out), index(chip_d, c_out), index(chip_d, 1 - c_out),
        ]
    ).astype(jnp.int32)

    def body(order_ref, x_ref, g_ref, *refs):
        del order_ref
        src = refs[:n_arr]
        proj_ref, h_ref = refs[n_arr : n_arr + 2]
        out = refs[n_arr + 2 : 2 * n_arr + 2]
        wbuf, send_sems, recv_sems, local_sems, load_sem = refs[2 * n_arr + 2 :]
        j = pl.program_id(0)
        i = pl.program_id(1)
        me, sibling, chip_a, chip_b, chip_d = places(*_position())
        c = me[2]

        def block(a, chip, core):
            return out[a].at[index(chip, core)]

        def copy(a, k, owner, to, from_input=False):
            return pltpu.make_async_remote_copy(
                src_ref=src[a] if from_input else block(a, *owner),
                dst_ref=block(a, *owner),
                send_sem=send_sems.at[a, k],
                recv_sem=recv_sems.at[a, k],
                device_id=to,
                device_id_type=MESH,
            )

        mine = (me[:2], c)

        def own(a):
            return pltpu.make_async_copy(src[a], block(a, *mine), local_sems.at[a])

        def first_sends(a):
            return [
                copy(a, 0, mine, sibling, from_input=True),
                copy(a, 1, mine, (*chip_a, c), from_input=True),
                copy(a, 2, mine, (*chip_b, c), from_input=True),
            ]

        def onward(a, k):
            owner = {3: chip_a, 4: chip_a, 5: chip_b, 6: chip_d}[k]
            return copy(a, k, (owner, c), (*chip_b, c) if k == 3 else sibling)

        def landed(a, k):
            owner = {0: mine[0], 1: chip_a, 2: chip_b, 3: chip_d, 4: chip_b, 5: chip_a, 6: chip_d}[k]
            core = c if k in (1, 2, 3) else 1 - c
            copy(a, k, (owner, core), me).wait_recv()
            return owner, core

        def load(ref):
            cp = pltpu.make_async_copy(ref, wbuf, load_sem)
            cp.start()
            cp.wait()

        def at_block_start(step):
            return jnp.logical_and(j == step, i == 0)

        @pl.when(at_block_start(0))
        def _():
            own(0).start()
            for cp in first_sends(0):
                cp.start()
            load(src[0])

        steps = {1: (0, ()), 2: (1, (3, 4)), 3: (2, (5,)), 4: (4, ()), 5: (5, ()), 6: (3, (6,)), 7: (6, ())}
        for step, (k, then) in steps.items():

            @pl.when(at_block_start(step))
            def _():
                owner = landed(0, k)
                for k2 in then:
                    onward(0, k2).start()
                if step == 2:
                    for a in range(1, n_arr):
                        own(a).start()
                        for cp in first_sends(a):
                            cp.start()
                load(block(0, *owner))

        xv = x_ref[...]
        h = (xv * lax.rsqrt(_rowmean(xv * xv) + EPS) * g_ref[...]).astype(BF16)
        proj_ref[...] = _dot(h, wbuf[...])

        @pl.when(j == 0)
        def _():
            h_ref[...] = h

        @pl.when(jnp.logical_and(j == N_DEV - 1, i == n_t - 1))
        def _():
            for a in range(1, n_arr):
                for k, then in ((1, (3, 4)), (2, (5,)), (3, (6,))):
                    landed(a, k)
                    for k2 in then:
                        onward(a, k2).start()
            for a in range(1, n_arr):
                for k in (0, 4, 5, 6):
                    landed(a, k)
            for a in range(n_arr):
                for cp in first_sends(a) + [onward(a, k) for k in (3, 4, 5, 6)]:
                    cp.wait_send()
                own(a).wait()

    res = pl.pallas_call(
        body,
        name="proj_gather",
        grid_spec=pltpu.PrefetchScalarGridSpec(
            num_scalar_prefetch=1,
            grid=(N_DEV, n_t),
            in_specs=[
                pl.BlockSpec((t, d), lambda j, i, order_ref: (i, 0)),
                pl.BlockSpec((1, d), lambda j, i, order_ref: (0, 0)),
            ]
            + [ANY] * n_arr,
            out_specs=[
                pl.BlockSpec((t, WIN_BLK), lambda j, i, order_ref: (i, order_ref[j])),
                pl.BlockSpec((t, d), lambda j, i, order_ref: (jnp.where(j == 0, i, n_t - 1), 0)),
            ]
            + [ANY] * n_arr,
            scratch_shapes=[
                pltpu.VMEM(shards[0].shape, BF16),
                pltpu.SemaphoreType.DMA((n_arr, 7)),
                pltpu.SemaphoreType.DMA((n_arr, 7)),
                pltpu.SemaphoreType.DMA((n_arr,)),
                pltpu.SemaphoreType.DMA,
            ],
        ),
        out_shape=[jax.ShapeDtypeStruct((s, IN_WIDTH), F32), jax.ShapeDtypeStruct((s, d), BF16)]
        + [jax.ShapeDtypeStruct((N_DEV,) + a.shape, a.dtype) for a in shards],
        compiler_params=_params("arbitrary", "arbitrary"),
    )(order, x_in, norm_pre, *shards)
    return res[0], res[1], res[2:]


def _sigmoid(a):
    return jax.nn.sigmoid(a)


def _dsilu(a, sg):
    return sg * (1.0 + a * (1.0 - sg))


def _rms_fwd(u, gain):
    r = lax.rsqrt(_rowmean(u * u) + EPS)
    n = u * r
    return r, n, n * gain


def _rms_bwd(dy, gain, r, n):
    dn = dy * gain
    return _colsum(dy * n), r * (dn - n * _rowmean(dn * n))


def _mix(proj, x, target, kv, wout, pool_w, pool_scale, ln_g, ln_b, sgu_bias, sgu_wm, branch_norm, norm_post):
    s, d = x.shape
    t = min(TILE_MIX, s)
    n_tiles = s // t
    n_chunks = t // SGU_CHUNK
    halo_blocks_per_tile = t // HALO
    inv_d = 1.0 / d
    scale = 1.0 / (XATTN_HEAD_DIM**0.5)

    def body(
        proj_ref, halo_ref, x_ref, tgt_ref, kv_ref, wout_ref, pw_ref, pscale_ref, lng_ref, lnb_ref, bias_ref, wm_ref,
        bnorm_ref, gpost_ref,
        y_ref, dout_ref, dxo_ref, dproj_ref, loss_ref, dgpost_ref, dbnorm_ref, dpscale_ref, dlng_ref, dlnb_ref,
        dpw_ref, dwm_ref, dbias_ref, dkv_ref,
        carry_ref, dzsum_ref,
    ):
        i = pl.program_id(0)
        tile = n_tiles - 1 - i

        @pl.when(i == 0)
        def _():
            carry_ref[...] = jnp.zeros_like(carry_ref)
            dzsum_ref[...] = jnp.zeros_like(dzsum_ref)
            for ref in (loss_ref, dgpost_ref, dbnorm_ref, dpscale_ref, dlng_ref, dlnb_ref, dpw_ref, dwm_ref, dkv_ref):
                ref[...] = jnp.zeros_like(ref)

        t_glob = tile * t + lax.broadcasted_iota(jnp.int32, (t, 1), 0)
        inv_cnt = [1.0 / jnp.minimum(t_glob + 1, w).astype(F32) for w in POOL_WINDOWS]

        xa = proj_ref[:, 0:BRANCH]
        ga = proj_ref[:, BRANCH : 2 * BRANCH]
        halo = jnp.where(tile == 0, 0.0, halo_ref[...])
        d_bf, pm_parts = [], []
        for g, w in enumerate(POOL_WINDOWS):
            cols = slice(g * POOL_GROUP_DIM, (g + 1) * POOL_GROUP_DIM)
            acc = jnp.concatenate([halo[:, cols], xa[:, cols]], axis=0)
            k = 1
            while k < w:
                acc = acc + pltpu.roll(acc, k, axis=0)
                k *= 2
            dg = (acc[HALO:, :] * inv_cnt[g] - xa[:, cols]).astype(BF16)
            d_bf.append(dg)
            pm_parts.append(_dot(dg, pw_ref[g]))
        pm = jnp.concatenate(pm_parts, axis=1)
        pscale = pscale_ref[...]
        pa = pm * pscale
        sga = _sigmoid(ga)
        sila = ga * sga
        ua = pa * sila
        g_a = bnorm_ref[:, 0:BRANCH]
        ra, na, ya = _rms_fwd(ua, g_a)

        u = proj_ref[:, 2 * BRANCH : 3 * BRANCH]
        v = proj_ref[:, 3 * BRANCH : 4 * BRANCH]
        gb = proj_ref[:, 4 * BRANCH : 5 * BRANCH]
        lng = lng_ref[...]
        vc = v - _rowmean(v)
        rstd = lax.rsqrt(_rowmean(vc * vc) + EPS)
        vhat = vc * rstd
        vn_bf = (vhat * lng + lnb_ref[...]).astype(BF16)
        z_rows = []
        for c in range(n_chunks):
            rows = slice(c * SGU_CHUNK, (c + 1) * SGU_CHUNK)
            z_rows.append(
                jnp.concatenate(
                    [
                        _dot(wm_ref[hd], vn_bf[rows, hd * SGU_HEAD_DIM : (hd + 1) * SGU_HEAD_DIM])
                        for hd in range(SGU_HEADS)
                    ],
                    axis=1,
                )
                + bias_ref[...]
            )
        z = z_rows[0] if n_chunks == 1 else jnp.concatenate(z_rows, axis=0)
        sb = u * z
        sgb = _sigmoid(gb)
        silb = gb * sgb
        ub = sb * silb
        g_b = bnorm_ref[:, BRANCH : 2 * BRANCH]
        rb, nb, yb = _rms_fwd(ub, g_b)

        q = proj_ref[:, 5 * BRANCH : 6 * BRANCH]
        gc = proj_ref[:, 6 * BRANCH : 7 * BRANCH]
        q_bf, p_bf, o_parts = [], [], []
        for hd in range(XATTN_HEADS):
            cols = slice(hd * XATTN_HEAD_DIM, (hd + 1) * XATTN_HEAD_DIM)
            qh = q[:, cols].astype(BF16)
            sc = _dot(qh, kv_ref[:, cols], NT) * scale
            e = jnp.exp(sc - jnp.max(sc, axis=-1, keepdims=True))
            p = e / jnp.sum(e, axis=-1, keepdims=True)
            q_bf.append(qh)
            p_bf.append(p.astype(BF16))
            o_parts.append(_dot(p_bf[hd], kv_ref[:, BRANCH + hd * XATTN_HEAD_DIM : BRANCH + (hd + 1) * XATTN_HEAD_DIM]))
        o = jnp.concatenate(o_parts, axis=1)
        sgc = _sigmoid(gc)
        silc = gc * sgc
        uc = o * silc
        g_c = bnorm_ref[:, 2 * BRANCH : 3 * BRANCH]
        rc, nc, yc = _rms_fwd(uc, g_c)

        y_bf = jnp.concatenate([ya, yb, yc], axis=1).astype(BF16)
        y_ref[...] = y_bf
        out = _dot(y_bf, wout_ref[...])
        gpost = gpost_ref[...]
        r_out = lax.rsqrt(_rowmean(out * out) + EPS)
        on = out * r_out
        err = x_ref[...] + on * gpost - tgt_ref[...]
        loss_ref[...] += 0.5 * jnp.sum(_rowmean(err * err), axis=0, keepdims=True)

        dxo = err * inv_d
        dxo_ref[...] = dxo
        dgp, dout = _rms_bwd(dxo, gpost, r_out, on)
        dgpost_ref[...] += dgp
        dout_bf = dout.astype(BF16)
        dout_ref[...] = dout_bf
        dy = _dot(dout_bf, wout_ref[...], NT)

        dg_a, dua = _rms_bwd(dy[:, 0:BRANCH], g_a, ra, na)
        dg_b, dub = _rms_bwd(dy[:, BRANCH : 2 * BRANCH], g_b, rb, nb)
        dg_c, duc = _rms_bwd(dy[:, 2 * BRANCH : 3 * BRANCH], g_c, rc, nc)
        dbnorm_ref[...] += jnp.concatenate([dg_a, dg_b, dg_c], axis=1)

        dpa = dua * sila
        dga = dua * pa * _dsilu(ga, sga)
        dpscale_ref[...] += _colsum(dpa * pm)
        dpm = dpa * pscale
        dxa_parts, carry_parts = [], []
        for g, w in enumerate(POOL_WINDOWS):
            cols = slice(g * POOL_GROUP_DIM, (g + 1) * POOL_GROUP_DIM)
            dpm_g = dpm[:, cols].astype(BF16)
            dd = _dot(dpm_g, pw_ref[g], NT)
            dpw_ref[g] += _dot(d_bf[g], dpm_g, TN)
            cg = dd * inv_cnt[g]
            carry_parts.append(cg[0:HALO, :])
            acc = jnp.concatenate([cg, carry_ref[:, cols]], axis=0)
            k = 1
            while k < w:
                acc = acc + pltpu.roll(acc, t + HALO - k, axis=0)
                k *= 2
            dxa_parts.append(acc[0:t, :] - dd)
        carry_ref[...] = jnp.concatenate(carry_parts, axis=1)
        dxa = jnp.concatenate(dxa_parts, axis=1)

        dsb = dub * silb
        dgb = dub * sb * _dsilu(gb, sgb)
        du = dsb * z
        dz = dsb * u
        dz_bf = dz.astype(BF16)
        dvn_rows = []
        dz_sum = None
        for c in range(n_chunks):
            rows = slice(c * SGU_CHUNK, (c + 1) * SGU_CHUNK)
            dz_sum = dz[rows, :] if dz_sum is None else dz_sum + dz[rows, :]
            parts = []
            for hd in range(SGU_HEADS):
                cols = slice(hd * SGU_HEAD_DIM, (hd + 1) * SGU_HEAD_DIM)
                parts.append(_dot(wm_ref[hd], dz_bf[rows, cols], TN))
                dwm_ref[hd] += _dot(dz_bf[rows, cols], vn_bf[rows, cols], NT)
            dvn_rows.append(jnp.concatenate(parts, axis=1))
        dzsum_ref[...] += dz_sum
        dvn = dvn_rows[0] if n_chunks == 1 else jnp.concatenate(dvn_rows, axis=0)
        dlng_ref[...] += _colsum(dvn * vhat)
        dlnb_ref[...] += _colsum(dvn)
        dvh = dvn * lng
        dv = rstd * (dvh - _rowmean(dvh) - vhat * _rowmean(dvh * vhat))

        do = duc * silc
        dgc = duc * o * _dsilu(gc, sgc)
        dq_parts = []
        for hd in range(XATTN_HEADS):
            cols = slice(hd * XATTN_HEAD_DIM, (hd + 1) * XATTN_HEAD_DIM)
            vcols = slice(BRANCH + hd * XATTN_HEAD_DIM, BRANCH + (hd + 1) * XATTN_HEAD_DIM)
            do_h = do[:, cols].astype(BF16)
            p = p_bf[hd].astype(F32)
            dp = _dot(do_h, kv_ref[:, vcols], NT)
            dkv_ref[:, vcols] += _dot(p_bf[hd], do_h, TN)
            ds_bf = (p * (dp - jnp.sum(dp * p, axis=-1, keepdims=True)) * scale).astype(BF16)
            dq_parts.append(_dot(ds_bf, kv_ref[:, cols]))
            dkv_ref[:, cols] += _dot(ds_bf, q_bf[hd], TN)
        dq = jnp.concatenate(dq_parts, axis=1)

        dproj_ref[...] = jnp.concatenate([dxa, dga, du, dv, dgb, dq, dgc], axis=1).astype(BF16)

        @pl.when(i == n_tiles - 1)
        def _():
            keep = lax.broadcasted_iota(jnp.int32, (SGU_CHUNK, SGU_CHUNK), 0) >= lax.broadcasted_iota(
                jnp.int32, (SGU_CHUNK, SGU_CHUNK), 1
            )
            for hd in range(SGU_HEADS):
                dwm_ref[hd] = jnp.where(keep, dwm_ref[hd], 0.0)
                per_pos = dzsum_ref[:, hd * SGU_HEAD_DIM : (hd + 1) * SGU_HEAD_DIM]
                dbias_ref[hd : hd + 1, :] = _colsum(per_pos.T)

    row_tile = lambda width: pl.BlockSpec((t, width), lambda i: (n_tiles - 1 - i, 0))
    halo_spec = pl.BlockSpec(
        (HALO, BRANCH), lambda i: (jnp.maximum((n_tiles - 1 - i) * halo_blocks_per_tile - 1, 0), 0)
    )
    acc_shapes = [
        (1, 128),
        (1, d),
        (1, MIX_WIDTH),
        (1, BRANCH),
        (1, BRANCH),
        (1, BRANCH),
        pool_w.shape,
        sgu_wm.shape,
        (SGU_HEADS, SGU_CHUNK),
        kv.shape,
    ]
    return pl.pallas_call(
        body,
        name="mix",
        grid=(n_tiles,),
        in_specs=[
            row_tile(IN_WIDTH), halo_spec, row_tile(d), row_tile(d), _resident(kv.shape), _resident(wout.shape),
            _resident(pool_w.shape), _full((1, BRANCH)), _full((1, BRANCH)), _full((1, BRANCH)),
            _resident((SGU_CHUNK, BRANCH)), _resident(sgu_wm.shape), _full((1, MIX_WIDTH)), _full((1, d)),
        ],
        out_specs=[row_tile(MIX_WIDTH), row_tile(d), row_tile(d), row_tile(IN_WIDTH)] + [_full(a) for a in acc_shapes],
        out_shape=[
            jax.ShapeDtypeStruct((s, MIX_WIDTH), BF16),
            jax.ShapeDtypeStruct((s, d), BF16),
            jax.ShapeDtypeStruct((s, d), F32),
            jax.ShapeDtypeStruct((s, IN_WIDTH), BF16),
        ]
        + [jax.ShapeDtypeStruct(a, F32) for a in acc_shapes],
        scratch_shapes=[pltpu.VMEM((HALO, BRANCH), F32), pltpu.VMEM((SGU_CHUNK, BRANCH), F32)],
        compiler_params=_params("arbitrary"),
    )(proj, proj, x, target, kv, wout, pool_w, pool_scale, ln_g, ln_b, sgu_bias, sgu_wm, branch_norm, norm_post)


ANY = pl.BlockSpec(memory_space=pl.ANY)


def _position():
    return lax.axis_index("x"), lax.axis_index("y"), lax.axis_index("c")


N_CHIPS = 4


def _landing_shape(kind, a):
    return (N_CHIPS,) + a.shape[2:] if kind == "pair" else a.shape


def _carry_specs(groups):
    arrays = [(kind, a) for kind, arrs in groups for a in arrs]
    scratch = []
    for _, arrs in groups:
        n = len(arrs)
        scratch += [pltpu.SemaphoreType.DMA((n, N_DEV)), pltpu.SemaphoreType.DMA((n, N_DEV)), pltpu.SemaphoreType.DMA((n,))]
    return dict(
        n=len(arrays),
        operands=[a for _, a in arrays],
        in_specs=[ANY] * len(arrays),
        out_specs=[ANY] * len(arrays),
        out_shape=[jax.ShapeDtypeStruct(_landing_shape(kind, a), a.dtype) for kind, a in arrays],
        scratch_shapes=scratch,
    )


def _carry(groups, src, out, sems):
    x, y, c = _position()
    chip = 2 * x + y
    me = 2 * chip + c

    def remote(s, d, send_sems, recv_sems, a, m, to):
        return pltpu.make_async_remote_copy(
            src_ref=s, dst_ref=d, send_sem=send_sems.at[a, m], recv_sem=recv_sems.at[a, m], device_id=to,
            device_id_type=MESH,
        )

    def copies():
        far, near = [], []
        at = 0
        for g, (kind, arrs) in enumerate(groups):
            send_sems, recv_sems, local_sems = sems[3 * g : 3 * g + 3]
            for a in range(len(arrs)):
                s, d = src[at + a], out[at + a]
                if kind == "pair":
                    far.append(remote(s.at[:, 1 - c], d, send_sems, recv_sems, a, 1, (x, y, 1 - c)))
                elif kind == "chips":
                    for m in range(1, N_CHIPS):
                        px, py = x ^ (m >> 1), y ^ (m & 1)
                        far.append(remote(s.at[2 * px + py], d.at[chip], send_sems, recv_sems, a, m, (px, py, c)))
                    near.append(pltpu.make_async_copy(s.at[chip], d.at[chip], local_sems.at[a]))
                else:
                    for m in range(1, N_DEV):
                        px, py, pc = x ^ ((m >> 2) & 1), y ^ ((m >> 1) & 1), c ^ (m & 1)
                        far.append(
                            remote(s.at[4 * px + 2 * py + pc], d.at[me], send_sems, recv_sems, a, m, (px, py, pc))
                        )
                    near.append(pltpu.make_async_copy(s.at[me], d.at[me], local_sems.at[a]))
            at += len(arrs)
        return far, near

    def start():
        far, near = copies()
        for cp in near + far:
            cp.start()

    def finish():
        far, near = copies()
        for cp in far:
            cp.wait_recv()
        for cp in far:
            cp.wait_send()
        for cp in near:
            cp.wait()

    return start, finish


def _exchange(groups, name):
    carried = _carry_specs(groups)
    n_c = carried["n"]

    def body(*refs):
        start, finish = _carry(groups, refs[:n_c], refs[n_c : 2 * n_c], refs[2 * n_c :])
        start()
        finish()

    return pl.pallas_call(
        body,
        name=name,
        in_specs=carried["in_specs"],
        out_specs=carried["out_specs"],
        out_shape=carried["out_shape"],
        scratch_shapes=carried["scratch_shapes"],
    )(*carried["operands"])


def _pair_sum(mine, theirs, name, groups=()):
    n = len(mine)
    carried = _carry_specs(groups)
    n_c = carried["n"]
    core = lax.axis_index("c").astype(jnp.int32).reshape(1)

    def body(core_ref, *refs):
        del core_ref
        own = refs[:n]
        sib = refs[n : 2 * n]
        src = refs[2 * n : 2 * n + n_c]
        out = refs[2 * n + n_c : 3 * n + n_c]
        landed = refs[3 * n + n_c : 3 * n + 2 * n_c]
        start, finish = _carry(groups, src, landed, refs[3 * n + 2 * n_c :])
        b = pl.program_id(0)

        @pl.when(b == 0)
        def _():
            start()

        for a in range(n):
            out[a][...] = (own[a][...].astype(F32) + sib[a][...].astype(F32)).astype(out[a].dtype)

        @pl.when(b == N_CHIPS - 1)
        def _():
            finish()

    block = lambda a: pl.BlockSpec((None,) + a.shape[1:], lambda b, core_ref: (b, 0, 0))
    res = pl.pallas_call(
        body,
        name=name,
        grid_spec=pltpu.PrefetchScalarGridSpec(
            num_scalar_prefetch=1,
            grid=(N_CHIPS,),
            in_specs=[pl.BlockSpec((None, None) + a.shape[2:], lambda b, core_ref: (b, core_ref[0], 0, 0)) for a in mine]
            + [block(a) for a in theirs]
            + carried["in_specs"],
            out_specs=[block(a) for a in theirs] + carried["out_specs"],
            scratch_shapes=carried["scratch_shapes"],
        ),
        out_shape=[jax.ShapeDtypeStruct(a.shape, a.dtype) for a in theirs] + carried["out_shape"],
        compiler_params=_params("arbitrary"),
    )(core, *mine, *theirs, *carried["operands"])
    return res[:n], res[n:]


def _weight_grad(a, b, n_blk, blocked, name, groups):
    s = a.shape[0]
    t = min(TILE_GRAD, s)
    n_t = s // t
    if blocked == "cols":
        k, c = a.shape[1], b.shape[1] // n_blk
        a_spec = pl.BlockSpec((t, k), lambda j, i: (i, 0))
        b_spec = pl.BlockSpec((t, c), lambda j, i: (i, j))
    else:
        k, c = a.shape[1] // n_blk, b.shape[1]
        a_spec = pl.BlockSpec((t, k), lambda j, i: (i, j))
        b_spec = pl.BlockSpec((t, c), lambda j, i: (i, 0))
    carried = _carry_specs(groups)
    n_p = carried["n"]

    def body(a_ref, b_ref, *refs):
        src = refs[:n_p]
        o_ref = refs[n_p]
        landed = refs[n_p + 1 : 2 * n_p + 1]
        acc_ref = refs[2 * n_p + 1]
        start, finish = _carry(groups, src, landed, refs[2 * n_p + 2 :])
        j = pl.program_id(0)
        i = pl.program_id(1)

        @pl.when(jnp.logical_and(j == 0, i == 0))
        def _():
            start()

        @pl.when(i == 0)
        def _():
            acc_ref[...] = jnp.zeros_like(acc_ref)

        acc_ref[...] += _dot(a_ref[...], b_ref[...], TN)

        @pl.when(i == n_t - 1)
        def _():
            o_ref[...] = acc_ref[...].astype(BF16)

        @pl.when(jnp.logical_and(j == n_blk - 1, i == n_t - 1))
        def _():
            finish()

    res = pl.pallas_call(
        body,
        name=name,
        grid=(n_blk, n_t),
        in_specs=[a_spec, b_spec] + carried["in_specs"],
        out_specs=[pl.BlockSpec((None, k, c), lambda j, i: (j, 0, 0))] + carried["out_specs"],
        out_shape=[jax.ShapeDtypeStruct((n_blk, k, c), BF16)] + carried["out_shape"],
        scratch_shapes=[pltpu.VMEM((k, c), F32)] + carried["scratch_shapes"],
        compiler_params=_params("arbitrary", "arbitrary"),
    )(a, b, *carried["operands"])
    return res[0], res[1:]


def _input_grad(dproj, win, x, dxo, norm_pre, groups):
    s, d = x.shape
    t = min(TILE_GRAD, s)
    n_t = s // t
    carried = _carry_specs(groups)
    n_p = carried["n"]

    def body(dp_ref, w_ref, x_ref, dxo_ref, g_ref, *refs):
        src = refs[:n_p]
        gx_ref, dg_ref = refs[n_p : n_p + 2]
        landed = refs[n_p + 2 : 2 * n_p + 2]
        acc_ref = refs[2 * n_p + 2]
        start, finish = _carry(groups, src, landed, refs[2 * n_p + 3 :])
        i = pl.program_id(0)
        j = pl.program_id(1)

        @pl.when(jnp.logical_and(i == 0, j == 0))
        def _():
            start()
            dg_ref[...] = jnp.zeros_like(dg_ref)

        @pl.when(j == 0)
        def _():
            acc_ref[...] = jnp.zeros_like(acc_ref)

        acc_ref[...] += _dot(dp_ref[...], w_ref[...], NT)

        @pl.when(j == N_DEV - 1)
        def _():
            xv = x_ref[...]
            gain = g_ref[...]
            r = lax.rsqrt(_rowmean(xv * xv) + EPS)
            dgain, dx = _rms_bwd(acc_ref[...], gain, r, xv * r)
            dg_ref[...] += dgain
            gx_ref[...] = dxo_ref[...] + dx

        @pl.when(jnp.logical_and(i == n_t - 1, j == N_DEV - 1))
        def _():
            finish()

    res = pl.pallas_call(
        body,
        name="input_grad",
        grid=(n_t, N_DEV),
        in_specs=[
            pl.BlockSpec((t, WIN_BLK), lambda i, j: (i, j)),
            pl.BlockSpec((None, d, WIN_BLK), lambda i, j: (j, 0, 0)),
            pl.BlockSpec((t, d), lambda i, j: (i, 0)),
            pl.BlockSpec((t, d), lambda i, j: (i, 0)),
            _full((1, d)),
        ]
        + carried["in_specs"],
        out_specs=[pl.BlockSpec((t, d), lambda i, j: (i, 0)), _full((1, d))] + carried["out_specs"],
        out_shape=[jax.ShapeDtypeStruct((s, d), F32), jax.ShapeDtypeStruct((1, d), F32)] + carried["out_shape"],
        scratch_shapes=[pltpu.VMEM((t, d), F32)] + carried["scratch_shapes"],
        compiler_params=_params("arbitrary", "arbitrary"),
    )(dproj, win, x, dxo, norm_pre, *carried["operands"])
    return res[0], res[1], res[2:]


def _kv_backward(dkv, memn, wkv, mem, mem_norm):
    m, d = mem.shape
    n = wkv.shape[1]

    def body(dkv_ref, memn_ref, w_ref, mem_ref, g_ref, gw_ref, dg_ref):
        dkv_bf = dkv_ref[...].astype(BF16)
        gw_ref[...] = _dot(memn_ref[...], dkv_bf, TN).astype(BF16).reshape(N_DEV, WKV_BLK, n)
        dmemn = _dot(dkv_bf, w_ref[...], NT)
        mv = mem_ref[...]
        r = lax.rsqrt(_rowmean(mv * mv) + EPS)
        dg_ref[...] = _colsum(dmemn * (mv * r))

    return pl.pallas_call(
        body,
        name="kv_backward",
        grid=(1,),
        in_specs=[_full((m, n)), _full((m, d)), _full(wkv.shape), _full((m, d)), _full((1, d))],
        out_specs=[_full((N_DEV, WKV_BLK, n)), _full((1, d))],
        out_shape=[jax.ShapeDtypeStruct((N_DEV, WKV_BLK, n), BF16), jax.ShapeDtypeStruct((1, d), F32)],
        compiler_params=_params("arbitrary"),
    )(dkv, memn, wkv, mem, mem_norm)


def _adamw_math(w, g, m, v):
    m = ADAM_B1 * m + (1.0 - ADAM_B1) * g
    v = ADAM_B2 * v + (1.0 - ADAM_B2) * (g * g)
    m_hat = m / (1.0 - ADAM_B1**ADAM_STEP)
    v_hat = v / (1.0 - ADAM_B2**ADAM_STEP)
    delta = -ADAM_LR * (m_hat / (jnp.sqrt(v_hat) + ADAM_EPS) + ADAM_WD * w)
    return delta, m, v


def _adamw(parts, w, m, v, name):
    r, c = w.shape
    slots = parts.shape[0]
    t = r
    while t * c * 4 > TILE_ADAM_BYTES and t % 16 == 0:
        t //= 2

    def body(p_ref, w_ref, m_ref, v_ref, g_ref, d_ref, nm_ref, nv_ref):
        g = p_ref[0].astype(F32)
        for k in range(1, slots):
            g = g + p_ref[k].astype(F32)
        delta, nm, nv = _adamw_math(w_ref[...], g, m_ref[...], v_ref[...])
        g_ref[...] = g
        d_ref[...] = delta
        nm_ref[...] = nm
        nv_ref[...] = nv

    tile = pl.BlockSpec((t, c), lambda i: (i, 0))
    return pl.pallas_call(
        body,
        name=name,
        grid=(r // t,),
        in_specs=[pl.BlockSpec((slots, t, c), lambda i: (0, i, 0)), tile, tile, tile],
        out_specs=[tile] * 4,
        out_shape=[jax.ShapeDtypeStruct((r, c), F32)] * 4,
        compiler_params=_params("parallel"),
    )(parts, w, m, v)


SMALL = ("norm_pre", "pool_scale", "sgu_ln_g", "sgu_ln_b", "sgu_w", "sgu_b", "mem_norm", "branch_norm", "norm_post")


def _local_view(name, w):
    if name == "sgu_w":
        return w.reshape(SGU_HEADS, SGU_CHUNK, SGU_CHUNK)
    if name == "sgu_b":
        return w.reshape(SGU_HEADS, SGU_CHUNK)
    return w.reshape(1, -1)


def _forward_backward(x, mem, target, shards, small):
    causal = jnp.tril(jnp.ones((SGU_CHUNK, SGU_CHUNK), dtype=bool))
    sgu_wm = jnp.where(causal[None], small["sgu_w"], 0.0).astype(BF16)
    sgu_bias = jnp.repeat(jnp.transpose(small["sgu_b"]), SGU_HEAD_DIM, axis=1)

    proj, h, (win, wkv, pool_all, wout) = _proj_gather(x, small["norm_pre"], shards)
    wout = wout.reshape(MIX_WIDTH, D_MODEL)
    wkv = wkv.reshape(D_MODEL, 2 * BRANCH)
    pool_full = (
        pool_all.reshape(N_DEV, len(POOL_WINDOWS), POOL_BLK, POOL_GROUP_DIM)
        .transpose(1, 0, 2, 3)
        .reshape(len(POOL_WINDOWS), POOL_GROUP_DIM, POOL_GROUP_DIM)
    )
    memn, kv = _kv_forward(mem, small["mem_norm"], wkv)
    (y, dout, dxo, dproj, loss, d_norm_post, d_branch_norm, d_pool_scale, d_ln_g, d_ln_b, d_pool_w, d_sgu_w, d_sgu_b,
     dkv) = _mix(
        proj, x, target, kv, wout, pool_full, small["pool_scale"], small["sgu_ln_g"], small["sgu_ln_b"], sgu_bias,
        sgu_wm, small["branch_norm"], small["norm_post"],
    )
    g_wkv, d_mem_norm = _kv_backward(dkv, memn, wkv, mem, small["mem_norm"])
    g_pool = (
        d_pool_w.reshape(len(POOL_WINDOWS), N_DEV, POOL_BLK, POOL_GROUP_DIM)
        .transpose(1, 0, 2, 3)
        .reshape(N_DEV, len(POOL_WINDOWS) * POOL_BLK, POOL_GROUP_DIM)
        .astype(BF16)
    )
    small_grads = dict(
        pool_scale=d_pool_scale, sgu_ln_g=d_ln_g, sgu_ln_b=d_ln_b, sgu_w=d_sgu_w, sgu_b=d_sgu_b,
        mem_norm=d_mem_norm, branch_norm=d_branch_norm, norm_post=d_norm_post,
    )
    packed = jnp.concatenate([small_grads[n].reshape(-1, 128) for n in SMALL if n != "norm_pre"], axis=0)
    packed = jnp.broadcast_to(packed[None, None], (N_CHIPS, 2) + packed.shape)

    by_chip = lambda g: g.reshape((N_CHIPS, 2) + g.shape[1:])
    small_mine = [by_chip(g_wkv), by_chip(g_pool), packed]
    g_win, small_theirs = _weight_grad(h, dproj, N_DEV, "cols", "grad_w_in", [("pair", small_mine)])
    small_sums, _ = _pair_sum(small_mine, small_theirs, "pair_sum_small")
    g_wout, (l_wkv, l_pool, l_packed, win_theirs) = _weight_grad(
        y, dout, N_DEV, "rows", "grad_w_out", [("chips", list(small_sums)), ("pair", [by_chip(g_win)])]
    )
    (win_sums,), (wout_theirs,) = _pair_sum(
        [by_chip(g_win)], [win_theirs], "pair_sum_w_in", [("pair", [by_chip(g_wout)])]
    )
    (wout_sums,), _ = _pair_sum([by_chip(g_wout)], [wout_theirs], "pair_sum_w_out")
    grad_x, d_norm_pre, (l_win, l_wout) = _input_grad(
        dproj, win, x, dxo, small["norm_pre"], [("chips", [win_sums, wout_sums])]
    )
    return loss, grad_x, dict(w_in=l_win, w_out=l_wout, w_kv=l_wkv, pool_w=l_pool), l_packed, d_norm_pre


def kernel(x, mem, norm_pre, w_in, pool_w, pool_scale, sgu_ln_g, sgu_ln_b, sgu_w, sgu_b, mem_norm, w_kv, branch_norm, w_out, norm_post, loss_target, m_norm_pre, m_w_in, m_pool_w, m_pool_scale, m_sgu_ln_g, m_sgu_ln_b, m_sgu_w, m_sgu_b, m_mem_norm, m_w_kv, m_branch_norm, m_w_out, m_norm_post, v_norm_pre, v_w_in, v_pool_w, v_pool_scale, v_sgu_ln_g, v_sgu_ln_b, v_sgu_w, v_sgu_b, v_mem_norm, v_w_kv, v_branch_norm, v_w_out, v_norm_post):
    weights = dict(norm_pre=norm_pre, w_in=w_in, pool_w=pool_w, pool_scale=pool_scale, sgu_ln_g=sgu_ln_g, sgu_ln_b=sgu_ln_b, sgu_w=sgu_w, sgu_b=sgu_b, mem_norm=mem_norm, w_kv=w_kv, branch_norm=branch_norm, w_out=w_out, norm_post=norm_post)
    first = dict(norm_pre=m_norm_pre, w_in=m_w_in, pool_w=m_pool_w, pool_scale=m_pool_scale, sgu_ln_g=m_sgu_ln_g, sgu_ln_b=m_sgu_ln_b, sgu_w=m_sgu_w, sgu_b=m_sgu_b, mem_norm=m_mem_norm, w_kv=m_w_kv, branch_norm=m_branch_norm, w_out=m_w_out, norm_post=m_norm_post)
    second = dict(norm_pre=v_norm_pre, w_in=v_w_in, pool_w=v_pool_w, pool_scale=v_pool_scale, sgu_ln_g=v_sgu_ln_g, sgu_ln_b=v_sgu_ln_b, sgu_w=v_sgu_w, sgu_b=v_sgu_b, mem_norm=v_mem_norm, w_kv=v_w_kv, branch_norm=v_branch_norm, w_out=v_w_out, norm_post=v_norm_post)
    order = ("norm_pre", "w_in", "pool_w", "pool_scale", "sgu_ln_g", "sgu_ln_b", "sgu_w", "sgu_b", "mem_norm", "w_kv", "branch_norm", "w_out", "norm_post")

    owned_shape = dict(
        w_in=(D_MODEL, WIN_BLK), w_out=(WOUT_BLK, D_MODEL), w_kv=(WKV_BLK, 2 * BRANCH),
        pool_w=(len(POOL_WINDOWS) * POOL_BLK, POOL_GROUP_DIM),
    )
    owned = {n: weights[n].reshape(owned_shape[n]) for n in owned_shape}
    small = {n: _local_view(n, weights[n]) for n in SMALL}
    loss, grad_x, landed, landed_packed, d_norm_pre = _forward_backward(
        x[0], mem[0], loss_target[0], [owned[n].astype(BF16) for n in ("w_in", "w_kv", "pool_w", "w_out")], small
    )
    d_norm_pre = d_norm_pre.reshape(-1, 128)
    (landed_norm_pre,) = _exchange(
        [("all", [jnp.broadcast_to(d_norm_pre[None], (N_DEV,) + d_norm_pre.shape)])], "exchange_norm_pre"
    )

    grads, deltas, new_m, new_v = {}, {}, {}, {}
    for n in owned_shape:
        shape = weights[n].shape
        res = _adamw(
            landed[n], owned[n], first[n].reshape(owned_shape[n]), second[n].reshape(owned_shape[n]), "adamw_" + n
        )
        grads[n], deltas[n], new_m[n], new_v[n] = (a.reshape(shape) for a in res)
    rest = [n for n in SMALL if n != "norm_pre"]
    flat = lambda tree, names: jnp.concatenate([tree[n].reshape(-1, 128) for n in names], axis=0)
    res_rest = _adamw(landed_packed, flat(weights, rest), flat(first, rest), flat(second, rest), "adamw_replicated")
    res_pre = _adamw(
        landed_norm_pre, flat(weights, ["norm_pre"]), flat(first, ["norm_pre"]), flat(second, ["norm_pre"]),
        "adamw_norm_pre",
    )
    at = 0
    for n in rest:
        size = weights[n].size // 128
        for tree, a in zip((grads, deltas, new_m, new_v), res_rest):
            tree[n] = a[at : at + size].reshape(weights[n].shape)
        at += size
    for tree, a in zip((grads, deltas, new_m, new_v), res_pre):
        tree["norm_pre"] = a.reshape(weights["norm_pre"].shape)

    total = lax.psum(loss[0, 0], ("x", "y", "c"))
    return (
        total,
        grad_x[None],
        *[grads[n] for n in order],
        *[deltas[n] for n in order],
        *[new_m[n] for n in order],
        *[new_v[n] for n in order],
    )
```

```python
import functools

import jax
import jax.numpy as jnp
from jax import lax
from jax.experimental import pallas as pl
from jax.experimental.pallas import tpu as pltpu

F32 = jnp.float32
BF16 = jnp.bfloat16
EPS = 1e-6

D_MODEL = 2048
POOL_WINDOWS = (2, 4, 8, 16)
POOL_GROUP_DIM = 256
BRANCH = 1024
SGU_CHUNK = 128
SGU_HEADS = 8
SGU_HEAD_DIM = 128
XATTN_HEADS = 4
XATTN_HEAD_DIM = 256
MIX_WIDTH = 3 * BRANCH
IN_WIDTH = 7 * BRANCH
N_DEV = 8
WIN_BLK = IN_WIDTH // N_DEV
WOUT_BLK = MIX_WIDTH // N_DEV
WKV_BLK = D_MODEL // N_DEV
POOL_BLK = POOL_GROUP_DIM // N_DEV
HALO = 16

ADAM_LR = 0.001
ADAM_B1 = 0.9
ADAM_B2 = 0.999
ADAM_EPS = 1e-08
ADAM_WD = 0.01
ADAM_STEP = 10

VMEM_LIMIT_BYTES = 56 * 1024 * 1024
VMEM_LIMIT_MIX_BYTES = 63 * 1024 * 1024

TILE_PROJ = 512
TILE_MIX = 128
TILE_GRAD = 512
TILE_ADAM_BYTES = 1 << 20

ANY = pl.BlockSpec(memory_space=pl.ANY)
NN = (((1,), (0,)), ((), ()))
NT = (((1,), (1,)), ((), ()))
TN = (((0,), (0,)), ((), ()))
MESH = pl.DeviceIdType.MESH


def _dot(a, b, dims=NN):
    return lax.dot_general(a, b, dims, preferred_element_type=F32)


def _params(*semantics, vmem_limit_bytes=VMEM_LIMIT_BYTES):
    return pltpu.CompilerParams(dimension_semantics=semantics, vmem_limit_bytes=vmem_limit_bytes)


def _rowmean(a):
    return jnp.mean(a, axis=-1, keepdims=True)


def _colsum(a):
    return jnp.sum(a, axis=0, keepdims=True)


def _full(shape):
    zeros = (0,) * len(shape)
    return pl.BlockSpec(shape, lambda *_: zeros)


def _resident(shape):
    zeros = (0,) * len(shape)
    return pl.BlockSpec(shape, lambda *_: zeros, pipeline_mode=pl.Buffered(1))


def _kv_forward(mem, mem_norm, wkv):
    m, d = mem.shape

    def body(mem_ref, g_ref, w_ref, memn_ref, kv_ref):
        mv = mem_ref[...]
        r = lax.rsqrt(_rowmean(mv * mv) + EPS)
        memn = (mv * r * g_ref[...]).astype(BF16)
        memn_ref[...] = memn
        kv_ref[...] = _dot(memn, w_ref[...]).astype(BF16)

    return pl.pallas_call(
        body,
        name="kv_forward",
        grid=(1,),
        in_specs=[_full((m, d)), _full((1, d)), _full(wkv.shape)],
        out_specs=[_full((m, d)), _full((m, wkv.shape[1]))],
        out_shape=[jax.ShapeDtypeStruct((m, d), BF16), jax.ShapeDtypeStruct((m, wkv.shape[1]), BF16)],
        compiler_params=_params("arbitrary"),
    )(mem, mem_norm, wkv)


def _proj_gather(x_in, norm_pre, shards):
    s, d = x_in.shape
    t = min(TILE_PROJ, s)
    n_t = s // t
    n_arr = len(shards)

    def places(x, y, c):
        return (x, y, c), (x, y, 1 - c), (x ^ c, y ^ (1 - c)), (x ^ (1 - c), y ^ c), (1 - x, 1 - y)

    def index(chip, core):
        return 4 * chip[0] + 2 * chip[1] + core

    _, _, chip_a, chip_b, chip_d = places(*_position())
    c_out = lax.axis_index("c")
    me_out = index((lax.axis_index("x"), lax.axis_index("y")), c_out)
    order = jnp.stack(
        [
            me_out, me_out ^ 1, index(chip_a, c_out), index(chip_b, c_out), index(chip_b, 1 - c_out),
            index(chip_a, 1 - c_out), index(chip_d, c_out), index(chip_d, 1 - c_out),
        ]
    ).astype(jnp.int32)

    def body(order_ref, x_ref, g_ref, *refs):
        del order_ref
        src = refs[:n_arr]
        proj_ref, h_ref = refs[n_arr : n_arr + 2]
        out = refs[n_arr + 2 : 2 * n_arr + 2]
        wbuf, send_sems, recv_sems, local_sems, load_sem = refs[2 * n_arr + 2 :]
        j = pl.program_id(0)
        i = pl.program_id(1)
        me, sibling, chip_a, chip_b, chip_d = places(*_position())
        c = me[2]

        def block(a, chip, core):
            return out[a].at[index(chip, core)]

        def copy(a, k, owner, to, from_input=False):
            return pltpu.make_async_remote_copy(
                src_ref=src[a] if from_input else block(a, *owner),
                dst_ref=block(a, *owner),
                send_sem=send_sems.at[a, k],
                recv_sem=recv_sems.at[a, k],
                device_id=to,
                device_id_type=MESH,
            )

        mine = (me[:2], c)

        def own(a):
            return pltpu.make_async_copy(src[a], block(a, *mine), local_sems.at[a])

        def first_sends(a):
            return [
                copy(a, 0, mine, sibling, from_input=True),
                copy(a, 1, mine, (*chip_a, c), from_input=True),
                copy(a, 2, mine, (*chip_b, c), from_input=True),
            ]

        def onward(a, k):
            owner = {3: chip_a, 4: chip_a, 5: chip_b, 6: chip_d}[k]
            return copy(a, k, (owner, c), (*chip_b, c) if k == 3 else sibling)

        def landed(a, k):
            owner = {0: mine[0], 1: chip_a, 2: chip_b, 3: chip_d, 4: chip_b, 5: chip_a, 6: chip_d}[k]
            core = c if k in (1, 2, 3) else 1 - c
            copy(a, k, (owner, core), me).wait_recv()
            return owner, core

        def load(ref):
            cp = pltpu.make_async_copy(ref, wbuf, load_sem)
            cp.start()
            cp.wait()

        def at_block_start(step):
            return jnp.logical_and(j == step, i == 0)

        @pl.when(at_block_start(0))
        def _():
            own(0).start()
            for cp in first_sends(0):
                cp.start()
            load(src[0])

        steps = {1: (0, ()), 2: (1, (3, 4)), 3: (2, (5,)), 4: (4, ()), 5: (5, ()), 6: (3, (6,)), 7: (6, ())}
        for step, (k, then) in steps.items():

            @pl.when(at_block_start(step))
            def _():
                owner = landed(0, k)
                for k2 in then:
                    onward(0, k2).start()
                if step == 2:
                    for a in range(1, n_arr):
                        own(a).start()
                        for cp in first_sends(a):
                            cp.start()
                if step == 6:
                    for a in range(1, n_arr):
                        for k1, then1 in ((1, (3, 4)), (2, (5,))):
                            landed(a, k1)
                            for k2 in then1:
                                onward(a, k2).start()
                load(block(0, *owner))

        xv = x_ref[...]
        h = (xv * lax.rsqrt(_rowmean(xv * xv) + EPS) * g_ref[...]).astype(BF16)
        proj_ref[...] = _dot(h, wbuf[...]).astype(BF16)

        @pl.when(j == 0)
        def _():
            h_ref[...] = h

        @pl.when(jnp.logical_and(j == N_DEV - 1, i == n_t - 1))
        def _():
            for a in range(1, n_arr):
                landed(a, 3)
                onward(a, 6).start()
            for a in range(1, n_arr):
                for k in (0, 4, 5, 6):
                    landed(a, k)
            for a in range(n_arr):
                for cp in first_sends(a) + [onward(a, k) for k in (3, 4, 5, 6)]:
                    cp.wait_send()
                own(a).wait()

    res = pl.pallas_call(
        body,
        name="proj_gather",
        grid_spec=pltpu.PrefetchScalarGridSpec(
            num_scalar_prefetch=1,
            grid=(N_DEV, n_t),
            in_specs=[
                pl.BlockSpec((t, d), lambda j, i, order_ref: (i, 0)),
                pl.BlockSpec((1, d), lambda j, i, order_ref: (0, 0)),
            ]
            + [ANY] * n_arr,
            out_specs=[
                pl.BlockSpec((t, WIN_BLK), lambda j, i, order_ref: (i, order_ref[j])),
                pl.BlockSpec((t, d), lambda j, i, order_ref: (jnp.where(j == 0, i, n_t - 1), 0)),
            ]
            + [ANY] * n_arr,
            scratch_shapes=[
                pltpu.VMEM(shards[0].shape, BF16),
                pltpu.SemaphoreType.DMA((n_arr, 7)),
                pltpu.SemaphoreType.DMA((n_arr, 7)),
                pltpu.SemaphoreType.DMA((n_arr,)),
                pltpu.SemaphoreType.DMA,
            ],
        ),
        out_shape=[jax.ShapeDtypeStruct((s, IN_WIDTH), BF16), jax.ShapeDtypeStruct((s, d), BF16)]
        + [jax.ShapeDtypeStruct((N_DEV,) + a.shape, a.dtype) for a in shards],
        compiler_params=_params("arbitrary", "arbitrary"),
    )(order, x_in, norm_pre, *shards)
    return res[0], res[1], res[2:]


def _sigmoid(a):
    return jax.nn.sigmoid(a)


def _dsilu(a, sg):
    return sg * (1.0 + a * (1.0 - sg))


def _rms_fwd(u, gain):
    r = lax.rsqrt(_rowmean(u * u) + EPS)
    n = u * r
    return r, n, n * gain


def _rms_bwd(dy, gain, r, n):
    dn = dy * gain
    return _colsum(dy * n), r * (dn - n * _rowmean(dn * n))


def _mix(proj, x, target, kv, kv_t, wout, wout_t, pool_w, pool_w_t, pool_scale, ln_g, ln_b, sgu_bias, sgu_wm, sgu_wm_t, branch_norm, norm_post):
    s, d = x.shape
    t = min(TILE_MIX, s)
    n_tiles = s // t
    n_chunks = t // SGU_CHUNK
    halo_blocks_per_tile = t // HALO
    inv_d = 1.0 / d
    scale = 1.0 / (XATTN_HEAD_DIM**0.5)

    def body(
        proj_ref, halo_ref, x_ref, tgt_ref, kv_ref, kvt_ref, wout_ref, wout_t_ref, pw_ref, pwt_ref, pscale_ref, lng_ref,
        lnb_ref, bias_ref, wm_ref, wmt_ref, bnorm_ref, gpost_ref,
        y_ref, dout_ref, dxo_ref, dproj_ref, loss_ref, dgpost_ref, dbnorm_ref, dpscale_ref, dlng_ref, dlnb_ref,
        dpw_out, dwm_out, dbias_ref, dkv_out,
        carry_ref, dzsum_ref, dpw_ref, dwm_ref, dkv_ref,
    ):
        i = pl.program_id(0)
        tile = n_tiles - 1 - i

        @pl.when(i == 0)
        def _():
            carry_ref[...] = jnp.zeros_like(carry_ref)
            dzsum_ref[...] = jnp.zeros_like(dzsum_ref)
            for ref in (loss_ref, dgpost_ref, dbnorm_ref, dpscale_ref, dlng_ref, dlnb_ref, dpw_ref, dwm_ref, dkv_ref):
                ref[...] = jnp.zeros_like(ref)

        t_glob = tile * t + lax.broadcasted_iota(jnp.int32, (t, 1), 0)
        inv_cnt = [1.0 / jnp.minimum(t_glob + 1, w).astype(F32) for w in POOL_WINDOWS]

        xa = proj_ref[:, 0:BRANCH].astype(F32)
        ga = proj_ref[:, BRANCH : 2 * BRANCH].astype(F32)
        halo = jnp.where(tile == 0, 0.0, halo_ref[...].astype(F32))
        d_bf, pm_parts = [], []
        for g, w in enumerate(POOL_WINDOWS):
            cols = slice(g * POOL_GROUP_DIM, (g + 1) * POOL_GROUP_DIM)
            acc = jnp.concatenate([halo[:, cols], xa[:, cols]], axis=0)
            k = 1
            while k < w:
                acc = acc + pltpu.roll(acc, k, axis=0)
                k *= 2
            dg = (acc[HALO:, :] * inv_cnt[g] - xa[:, cols]).astype(BF16)
            d_bf.append(dg)
            pm_parts.append(_dot(dg, pw_ref[g]))
        pm = jnp.concatenate(pm_parts, axis=1)
        pscale = pscale_ref[...]
        pa = pm * pscale
        sga = _sigmoid(ga)
        sila = ga * sga
        ua = pa * sila
        g_a = bnorm_ref[:, 0:BRANCH]
        ra, na, ya = _rms_fwd(ua, g_a)

        u = proj_ref[:, 2 * BRANCH : 3 * BRANCH].astype(F32)
        v = proj_ref[:, 3 * BRANCH : 4 * BRANCH].astype(F32)
        gb = proj_ref[:, 4 * BRANCH : 5 * BRANCH].astype(F32)
        lng = lng_ref[...]
        vc = v - _rowmean(v)
        rstd = lax.rsqrt(_rowmean(vc * vc) + EPS)
        vhat = vc * rstd
        vn_bf = (vhat * lng + lnb_ref[...]).astype(BF16)
        z_rows = []
        for c in range(n_chunks):
            rows = slice(c * SGU_CHUNK, (c + 1) * SGU_CHUNK)
            z_rows.append(
                jnp.concatenate(
                    [
                        _dot(wm_ref[hd], vn_bf[rows, hd * SGU_HEAD_DIM : (hd + 1) * SGU_HEAD_DIM])
                        for hd in range(SGU_HEADS)
                    ],
                    axis=1,
                )
                + bias_ref[...]
            )
        z = z_rows[0] if n_chunks == 1 else jnp.concatenate(z_rows, axis=0)
        sb = u * z
        sgb = _sigmoid(gb)
        silb = gb * sgb
        ub = sb * silb
        g_b = bnorm_ref[:, BRANCH : 2 * BRANCH]
        rb, nb, yb = _rms_fwd(ub, g_b)

        q = proj_ref[:, 5 * BRANCH : 6 * BRANCH]
        gc = proj_ref[:, 6 * BRANCH : 7 * BRANCH].astype(F32)
        q_bf, p_bf, o_parts = [], [], []
        for hd in range(XATTN_HEADS):
            cols = slice(hd * XATTN_HEAD_DIM, (hd + 1) * XATTN_HEAD_DIM)
            qh = q[:, cols]
            sc = _dot(qh, kvt_ref[cols, :]) * scale
            e = jnp.exp(sc - jnp.max(sc, axis=-1, keepdims=True))
            p = e / jnp.sum(e, axis=-1, keepdims=True)
            q_bf.append(qh)
            p_bf.append(p.astype(BF16))
            o_parts.append(_dot(p_bf[hd], kv_ref[:, BRANCH + hd * XATTN_HEAD_DIM : BRANCH + (hd + 1) * XATTN_HEAD_DIM]))
        o = jnp.concatenate(o_parts, axis=1)
        sgc = _sigmoid(gc)
        silc = gc * sgc
        uc = o * silc
        g_c = bnorm_ref[:, 2 * BRANCH : 3 * BRANCH]
        rc, nc, yc = _rms_fwd(uc, g_c)

        out = None
        for b, y_branch in enumerate((ya, yb, yc)):
            rows = slice(b * BRANCH, (b + 1) * BRANCH)
            y_bf = y_branch.astype(BF16)
            y_ref[:, rows] = y_bf
            part = _dot(y_bf, wout_ref[rows, :])
            out = part if out is None else out + part
        gpost = gpost_ref[...]
        r_out = lax.rsqrt(_rowmean(out * out) + EPS)
        on = out * r_out
        err = x_ref[...] + on * gpost - tgt_ref[...]
        loss_ref[...] += 0.5 * jnp.sum(_rowmean(err * err), axis=0, keepdims=True)

        dxo = err * inv_d
        dxo_ref[...] = dxo
        dgp, dout = _rms_bwd(dxo, gpost, r_out, on)
        dgpost_ref[...] += dgp
        dout_bf = dout.astype(BF16)
        dout_ref[...] = dout_bf
        dy = [_dot(dout_bf, wout_t_ref[:, b * BRANCH : (b + 1) * BRANCH]) for b in range(3)]

        dg_a, dua = _rms_bwd(dy[0], g_a, ra, na)
        dg_b, dub = _rms_bwd(dy[1], g_b, rb, nb)
        dg_c, duc = _rms_bwd(dy[2], g_c, rc, nc)
        dbnorm_ref[...] += jnp.concatenate([dg_a, dg_b, dg_c], axis=1)

        dpa = dua * sila
        dga = dua * pa * _dsilu(ga, sga)
        dpscale_ref[...] += _colsum(dpa * pm)
        dpm = dpa * pscale
        dxa_parts, carry_parts = [], []
        for g, w in enumerate(POOL_WINDOWS):
            cols = slice(g * POOL_GROUP_DIM, (g + 1) * POOL_GROUP_DIM)
            dpm_g = dpm[:, cols].astype(BF16)
            dd = _dot(dpm_g, pwt_ref[g])
            dpw_ref[g] += _dot(d_bf[g], dpm_g, TN)
            cg = dd * inv_cnt[g]
            carry_parts.append(cg[0:HALO, :])
            acc = jnp.concatenate([cg, carry_ref[:, cols]], axis=0)
            k = 1
            while k < w:
                acc = acc + pltpu.roll(acc, t + HALO - k, axis=0)
                k *= 2
            dxa_parts.append(acc[0:t, :] - dd)
        carry_ref[...] = jnp.concatenate(carry_parts, axis=1)
        dxa = jnp.concatenate(dxa_parts, axis=1)

        dsb = dub * silb
        dgb = dub * sb * _dsilu(gb, sgb)
        du = dsb * z
        dz = dsb * u
        dz_bf = dz.astype(BF16)
        dvn_rows = []
        dz_sum = None
        for c in range(n_chunks):
            rows = slice(c * SGU_CHUNK, (c + 1) * SGU_CHUNK)
            dz_sum = dz[rows, :] if dz_sum is None else dz_sum + dz[rows, :]
            parts = []
            for hd in range(SGU_HEADS):
                cols = slice(hd * SGU_HEAD_DIM, (hd + 1) * SGU_HEAD_DIM)
                parts.append(_dot(wmt_ref[hd], dz_bf[rows, cols]))
                dwm_ref[hd] += _dot(dz_bf[rows, cols], vn_bf[rows, cols], NT)
            dvn_rows.append(jnp.concatenate(parts, axis=1))
        dzsum_ref[...] += dz_sum
        dvn = dvn_rows[0] if n_chunks == 1 else jnp.concatenate(dvn_rows, axis=0)
        dlng_ref[...] += _colsum(dvn * vhat)
        dlnb_ref[...] += _colsum(dvn)
        dvh = dvn * lng
        dv = rstd * (dvh - _rowmean(dvh) - vhat * _rowmean(dvh * vhat))

        do = duc * silc
        dgc = duc * o * _dsilu(gc, sgc)
        dq_parts = []
        for hd in range(XATTN_HEADS):
            cols = slice(hd * XATTN_HEAD_DIM, (hd + 1) * XATTN_HEAD_DIM)
            vcols = slice(BRANCH + hd * XATTN_HEAD_DIM, BRANCH + (hd + 1) * XATTN_HEAD_DIM)
            do_h = do[:, cols].astype(BF16)
            p = p_bf[hd].astype(F32)
            dp = _dot(do_h, kvt_ref[vcols, :])
            dkv_ref[:, vcols] += _dot(p_bf[hd], do_h, TN)
            ds_bf = (p * (dp - jnp.sum(dp * p, axis=-1, keepdims=True)) * scale).astype(BF16)
            dq_parts.append(_dot(ds_bf, kv_ref[:, cols]))
            dkv_ref[:, cols] += _dot(ds_bf, q_bf[hd], TN)
        dq = jnp.concatenate(dq_parts, axis=1)

        dproj_ref[...] = jnp.concatenate([dxa, dga, du, dv, dgb, dq, dgc], axis=1).astype(BF16)

        @pl.when(i == n_tiles - 1)
        def _():
            keep = lax.broadcasted_iota(jnp.int32, (SGU_CHUNK, SGU_CHUNK), 0) >= lax.broadcasted_iota(
                jnp.int32, (SGU_CHUNK, SGU_CHUNK), 1
            )
            for hd in range(SGU_HEADS):
                dwm_ref[hd] = jnp.where(keep, dwm_ref[hd], 0.0)
                per_pos = dzsum_ref[:, hd * SGU_HEAD_DIM : (hd + 1) * SGU_HEAD_DIM]
                dbias_ref[hd : hd + 1, :] = _colsum(per_pos.T)
            for acc, res in ((dpw_ref, dpw_out), (dwm_ref, dwm_out), (dkv_ref, dkv_out)):
                pltpu.sync_copy(acc, res)

    row_tile = lambda width: pl.BlockSpec((t, width), lambda i: (n_tiles - 1 - i, 0))
    halo_spec = pl.BlockSpec(
        (HALO, BRANCH), lambda i: (jnp.maximum((n_tiles - 1 - i) * halo_blocks_per_tile - 1, 0), 0)
    )
    acc_shapes = [
        (1, 128),
        (1, d),
        (1, MIX_WIDTH),
        (1, BRANCH),
        (1, BRANCH),
        (1, BRANCH),
        pool_w.shape,
        sgu_wm.shape,
        (SGU_HEADS, SGU_CHUNK),
        kv.shape,
    ]
    return pl.pallas_call(
        body,
        name="mix",
        grid=(n_tiles,),
        in_specs=[
            row_tile(IN_WIDTH), halo_spec, row_tile(d), row_tile(d), _resident(kv.shape), _resident(kv_t.shape),
            _resident(wout.shape), _resident(wout_t.shape), _resident(pool_w.shape), _resident(pool_w_t.shape),
            _full((1, BRANCH)), _full((1, BRANCH)), _full((1, BRANCH)), _resident((SGU_CHUNK, BRANCH)),
            _resident(sgu_wm.shape), _resident(sgu_wm_t.shape), _full((1, MIX_WIDTH)), _full((1, d)),
        ],
        out_specs=[row_tile(MIX_WIDTH), row_tile(d), row_tile(d), row_tile(IN_WIDTH)]
        + [ANY if len(a) == 3 or a == kv.shape else _full(a) for a in acc_shapes],
        out_shape=[
            jax.ShapeDtypeStruct((s, MIX_WIDTH), BF16),
            jax.ShapeDtypeStruct((s, d), BF16),
            jax.ShapeDtypeStruct((s, d), F32),
            jax.ShapeDtypeStruct((s, IN_WIDTH), BF16),
        ]
        + [jax.ShapeDtypeStruct(a, F32) for a in acc_shapes],
        scratch_shapes=[
            pltpu.VMEM((HALO, BRANCH), F32), pltpu.VMEM((SGU_CHUNK, BRANCH), F32), pltpu.VMEM(pool_w.shape, F32),
            pltpu.VMEM(sgu_wm.shape, F32), pltpu.VMEM(kv.shape, F32),
        ],
        compiler_params=_params("arbitrary", vmem_limit_bytes=VMEM_LIMIT_MIX_BYTES),
    )(
        proj, proj, x, target, kv, kv_t, wout, wout_t, pool_w, pool_w_t, pool_scale, ln_g, ln_b, sgu_bias, sgu_wm,
        sgu_wm_t, branch_norm, norm_post,
    )


def _position():
    return lax.axis_index("x"), lax.axis_index("y"), lax.axis_index("c")


N_CHIPS = 4


def _landing_shape(kind, a):
    return (N_CHIPS,) + a.shape[2:] if kind == "pair" else a.shape


def _carry_specs(groups):
    arrays = [(kind, a) for kind, arrs in groups for a in arrs]
    scratch = []
    for _, arrs in groups:
        n = len(arrs)
        scratch += [pltpu.SemaphoreType.DMA((n, N_DEV)), pltpu.SemaphoreType.DMA((n, N_DEV)), pltpu.SemaphoreType.DMA((n,))]
    return dict(
        n=len(arrays),
        operands=[a for _, a in arrays],
        in_specs=[ANY] * len(arrays),
        out_specs=[ANY] * len(arrays),
        out_shape=[jax.ShapeDtypeStruct(_landing_shape(kind, a), a.dtype) for kind, a in arrays],
        scratch_shapes=scratch,
    )


def _carry(groups, src, out, sems):
    x, y, c = _position()
    chip = 2 * x + y
    me = 2 * chip + c

    def remote(s, d, send_sems, recv_sems, a, m, to):
        return pltpu.make_async_remote_copy(
            src_ref=s, dst_ref=d, send_sem=send_sems.at[a, m], recv_sem=recv_sems.at[a, m], device_id=to,
            device_id_type=MESH,
        )

    def copies():
        far, near = [], []
        at = 0
        for g, (kind, arrs) in enumerate(groups):
            send_sems, recv_sems, local_sems = sems[3 * g : 3 * g + 3]
            for a in range(len(arrs)):
                s, d = src[at + a], out[at + a]
                if kind == "pair":
                    far.append(remote(s.at[:, 1 - c], d, send_sems, recv_sems, a, 1, (x, y, 1 - c)))
                elif kind == "chips":
                    for m in range(1, N_CHIPS):
                        px, py = x ^ (m >> 1), y ^ (m & 1)
                        far.append(remote(s.at[2 * px + py], d.at[chip], send_sems, recv_sems, a, m, (px, py, c)))
                    near.append(pltpu.make_async_copy(s.at[chip], d.at[chip], local_sems.at[a]))
                else:
                    for m in range(1, N_DEV):
                        px, py, pc = x ^ ((m >> 2) & 1), y ^ ((m >> 1) & 1), c ^ (m & 1)
                        far.append(
                            remote(s.at[4 * px + 2 * py + pc], d.at[me], send_sems, recv_sems, a, m, (px, py, pc))
                        )
                    near.append(pltpu.make_async_copy(s.at[me], d.at[me], local_sems.at[a]))
            at += len(arrs)
        return far, near

    def start():
        far, near = copies()
        for cp in near + far:
            cp.start()

    def finish():
        far, near = copies()
        for cp in far:
            cp.wait_recv()
        for cp in far:
            cp.wait_send()
        for cp in near:
            cp.wait()

    return start, finish


def _exchange(groups, name):
    carried = _carry_specs(groups)
    n_c = carried["n"]

    def body(*refs):
        start, finish = _carry(groups, refs[:n_c], refs[n_c : 2 * n_c], refs[2 * n_c :])
        start()
        finish()

    return pl.pallas_call(
        body,
        name=name,
        in_specs=carried["in_specs"],
        out_specs=carried["out_specs"],
        out_shape=carried["out_shape"],
        scratch_shapes=carried["scratch_shapes"],
    )(*carried["operands"])


def _pair_sum(mine, theirs, name, groups=()):
    n = len(mine)
    carried = _carry_specs(groups)
    n_c = carried["n"]
    core = lax.axis_index("c").astype(jnp.int32).reshape(1)

    def body(core_ref, *refs):
        del core_ref
        own = refs[:n]
        sib = refs[n : 2 * n]
        src = refs[2 * n : 2 * n + n_c]
        out = refs[2 * n + n_c : 3 * n + n_c]
        landed = refs[3 * n + n_c : 3 * n + 2 * n_c]
        start, finish = _carry(groups, src, landed, refs[3 * n + 2 * n_c :])
        b = pl.program_id(0)

        @pl.when(b == 0)
        def _():
            start()

        for a in range(n):
            out[a][...] = (own[a][...].astype(F32) + sib[a][...].astype(F32)).astype(out[a].dtype)

        @pl.when(b == N_CHIPS - 1)
        def _():
            finish()

    block = lambda a: pl.BlockSpec((None,) + a.shape[1:], lambda b, core_ref: (b, 0, 0))
    res = pl.pallas_call(
        body,
        name=name,
        grid_spec=pltpu.PrefetchScalarGridSpec(
            num_scalar_prefetch=1,
            grid=(N_CHIPS,),
            in_specs=[pl.BlockSpec((None, None) + a.shape[2:], lambda b, core_ref: (b, core_ref[0], 0, 0)) for a in mine]
            + [block(a) for a in theirs]
            + carried["in_specs"],
            out_specs=[block(a) for a in theirs] + carried["out_specs"],
            scratch_shapes=carried["scratch_shapes"],
        ),
        out_shape=[jax.ShapeDtypeStruct(a.shape, a.dtype) for a in theirs] + carried["out_shape"],
        compiler_params=_params("arbitrary"),
    )(core, *mine, *theirs, *carried["operands"])
    return res[:n], res[n:]


def _weight_grad(a, b, n_blk, blocked, name, groups):
    s = a.shape[0]
    t = min(TILE_GRAD, s)
    n_t = s // t
    if blocked == "cols":
        k, c = a.shape[1], b.shape[1] // n_blk
        a_spec = pl.BlockSpec((t, k), lambda j, i: (i, 0))
        b_spec = pl.BlockSpec((t, c), lambda j, i: (i, j))
    else:
        k, c = a.shape[1] // n_blk, b.shape[1]
        a_spec = pl.BlockSpec((t, k), lambda j, i: (i, j))
        b_spec = pl.BlockSpec((t, c), lambda j, i: (i, 0))
    carried = _carry_specs(groups)
    n_p = carried["n"]

    def body(a_ref, b_ref, *refs):
        src = refs[:n_p]
        o_ref = refs[n_p]
        landed = refs[n_p + 1 : 2 * n_p + 1]
        acc_ref = refs[2 * n_p + 1]
        start, finish = _carry(groups, src, landed, refs[2 * n_p + 2 :])
        j = pl.program_id(0)
        i = pl.program_id(1)

        @pl.when(jnp.logical_and(j == 0, i == 0))
        def _():
            start()

        @pl.when(i == 0)
        def _():
            acc_ref[...] = jnp.zeros_like(acc_ref)

        acc_ref[...] += _dot(a_ref[...], b_ref[...], TN)

        @pl.when(i == n_t - 1)
        def _():
            o_ref[...] = acc_ref[...].astype(BF16)

        @pl.when(jnp.logical_and(j == n_blk - 1, i == n_t - 1))
        def _():
            finish()

    res = pl.pallas_call(
        body,
        name=name,
        grid=(n_blk, n_t),
        in_specs=[a_spec, b_spec] + carried["in_specs"],
        out_specs=[pl.BlockSpec((None, k, c), lambda j, i: (j, 0, 0))] + carried["out_specs"],
        out_shape=[jax.ShapeDtypeStruct((n_blk, k, c), BF16)] + carried["out_shape"],
        scratch_shapes=[pltpu.VMEM((k, c), F32)] + carried["scratch_shapes"],
        compiler_params=_params("arbitrary", "arbitrary"),
    )(a, b, *carried["operands"])
    return res[0], res[1:]


def _input_grad(dproj, win, x, dxo, norm_pre, groups):
    s, d = x.shape
    t = min(TILE_GRAD, s)
    n_t = s // t
    carried = _carry_specs(groups)
    n_p = carried["n"]

    def body(dp_ref, w_ref, x_ref, dxo_ref, g_ref, *refs):
        src = refs[:n_p]
        gx_ref, dg_ref = refs[n_p : n_p + 2]
        landed = refs[n_p + 2 : 2 * n_p + 2]
        acc_ref = refs[2 * n_p + 2]
        start, finish = _carry(groups, src, landed, refs[2 * n_p + 3 :])
        i = pl.program_id(0)
        j = pl.program_id(1)

        @pl.when(jnp.logical_and(i == 0, j == 0))
        def _():
            start()
            dg_ref[...] = jnp.zeros_like(dg_ref)

        @pl.when(j == 0)
        def _():
            acc_ref[...] = jnp.zeros_like(acc_ref)

        acc_ref[...] += _dot(dp_ref[...], w_ref[...], NT)

        @pl.when(j == N_DEV - 1)
        def _():
            xv = x_ref[...]
            gain = g_ref[...]
            r = lax.rsqrt(_rowmean(xv * xv) + EPS)
            dgain, dx = _rms_bwd(acc_ref[...], gain, r, xv * r)
            dg_ref[...] += dgain
            gx_ref[...] = dxo_ref[...] + dx

        @pl.when(jnp.logical_and(i == n_t - 1, j == N_DEV - 1))
        def _():
            finish()

    res = pl.pallas_call(
        body,
        name="input_grad",
        grid=(n_t, N_DEV),
        in_specs=[
            pl.BlockSpec((t, WIN_BLK), lambda i, j: (i, j)),
            pl.BlockSpec((None, d, WIN_BLK), lambda i, j: (j, 0, 0)),
            pl.BlockSpec((t, d), lambda i, j: (i, 0)),
            pl.BlockSpec((t, d), lambda i, j: (i, 0)),
            _full((1, d)),
        ]
        + carried["in_specs"],
        out_specs=[pl.BlockSpec((t, d), lambda i, j: (i, 0)), _full((1, d))] + carried["out_specs"],
        out_shape=[jax.ShapeDtypeStruct((s, d), F32), jax.ShapeDtypeStruct((1, d), F32)] + carried["out_shape"],
        scratch_shapes=[pltpu.VMEM((t, d), F32)] + carried["scratch_shapes"],
        compiler_params=_params("arbitrary", "arbitrary"),
    )(dproj, win, x, dxo, norm_pre, *carried["operands"])
    return res[0], res[1], res[2:]


def _kv_backward(dkv, memn, wkv, mem, mem_norm):
    m, d = mem.shape
    n = wkv.shape[1]

    def body(dkv_ref, memn_ref, w_ref, mem_ref, g_ref, gw_ref, dg_ref):
        dkv_bf = dkv_ref[...].astype(BF16)
        gw_ref[...] = _dot(memn_ref[...], dkv_bf, TN).astype(BF16).reshape(N_DEV, WKV_BLK, n)
        dmemn = _dot(dkv_bf, w_ref[...], NT)
        mv = mem_ref[...]
        r = lax.rsqrt(_rowmean(mv * mv) + EPS)
        dg_ref[...] = _colsum(dmemn * (mv * r))

    return pl.pallas_call(
        body,
        name="kv_backward",
        grid=(1,),
        in_specs=[_full((m, n)), _full((m, d)), _full(wkv.shape), _full((m, d)), _full((1, d))],
        out_specs=[_full((N_DEV, WKV_BLK, n)), _full((1, d))],
        out_shape=[jax.ShapeDtypeStruct((N_DEV, WKV_BLK, n), BF16), jax.ShapeDtypeStruct((1, d), F32)],
        compiler_params=_params("arbitrary"),
    )(dkv, memn, wkv, mem, mem_norm)


def _adamw_math(w, g, m, v):
    m = ADAM_B1 * m + (1.0 - ADAM_B1) * g
    v = ADAM_B2 * v + (1.0 - ADAM_B2) * (g * g)
    m_hat = m / (1.0 - ADAM_B1**ADAM_STEP)
    v_hat = v / (1.0 - ADAM_B2**ADAM_STEP)
    delta = -ADAM_LR * (m_hat / (jnp.sqrt(v_hat) + ADAM_EPS) + ADAM_WD * w)
    return delta, m, v


def _adamw(parts, w, m, v, name):
    r, c = w.shape
    slots = parts.shape[0]
    t = r
    while t * c * 4 > TILE_ADAM_BYTES and t % 16 == 0:
        t //= 2

    def body(p_ref, w_ref, m_ref, v_ref, g_ref, d_ref, nm_ref, nv_ref):
        g = p_ref[0].astype(F32)
        for k in range(1, slots):
            g = g + p_ref[k].astype(F32)
        delta, nm, nv = _adamw_math(w_ref[...], g, m_ref[...], v_ref[...])
        g_ref[...] = g
        d_ref[...] = delta
        nm_ref[...] = nm
        nv_ref[...] = nv

    tile = pl.BlockSpec((t, c), lambda i: (i, 0))
    return pl.pallas_call(
        body,
        name=name,
        grid=(r // t,),
        in_specs=[pl.BlockSpec((slots, t, c), lambda i: (0, i, 0)), tile, tile, tile],
        out_specs=[tile] * 4,
        out_shape=[jax.ShapeDtypeStruct((r, c), F32)] * 4,
        compiler_params=_params("parallel"),
    )(parts, w, m, v)


SMALL = ("norm_pre", "pool_scale", "sgu_ln_g", "sgu_ln_b", "sgu_w", "sgu_b", "mem_norm", "branch_norm", "norm_post")


def _local_view(name, w):
    if name == "sgu_w":
        return w.reshape(SGU_HEADS, SGU_CHUNK, SGU_CHUNK)
    if name == "sgu_b":
        return w.reshape(SGU_HEADS, SGU_CHUNK)
    return w.reshape(1, -1)


def _forward_backward(x, mem, target, shards, small):
    causal = jnp.tril(jnp.ones((SGU_CHUNK, SGU_CHUNK), dtype=bool))
    sgu_wm = jnp.where(causal[None], small["sgu_w"], 0.0).astype(BF16)
    sgu_bias = jnp.repeat(jnp.transpose(small["sgu_b"]), SGU_HEAD_DIM, axis=1)

    proj, h, (win, wkv, pool_all, wout) = _proj_gather(x, small["norm_pre"], shards)
    wout = wout.reshape(MIX_WIDTH, D_MODEL)
    wkv = wkv.reshape(D_MODEL, 2 * BRANCH)
    pool_full = (
        pool_all.reshape(N_DEV, len(POOL_WINDOWS), POOL_BLK, POOL_GROUP_DIM)
        .transpose(1, 0, 2, 3)
        .reshape(len(POOL_WINDOWS), POOL_GROUP_DIM, POOL_GROUP_DIM)
    )
    memn, kv = _kv_forward(mem, small["mem_norm"], wkv)
    (y, dout, dxo, dproj, loss, d_norm_post, d_branch_norm, d_pool_scale, d_ln_g, d_ln_b, d_pool_w, d_sgu_w, d_sgu_b,
     dkv) = _mix(
        proj, x, target, kv, kv.T, wout, wout.T, pool_full, jnp.swapaxes(pool_full, 1, 2), small["pool_scale"],
        small["sgu_ln_g"], small["sgu_ln_b"], sgu_bias, sgu_wm, jnp.swapaxes(sgu_wm, 1, 2), small["branch_norm"],
        small["norm_post"],
    )
    g_wkv, d_mem_norm = _kv_backward(dkv, memn, wkv, mem, small["mem_norm"])
    g_pool = (
        d_pool_w.reshape(len(POOL_WINDOWS), N_DEV, POOL_BLK, POOL_GROUP_DIM)
        .transpose(1, 0, 2, 3)
        .reshape(N_DEV, len(POOL_WINDOWS) * POOL_BLK, POOL_GROUP_DIM)
        .astype(BF16)
    )
    small_grads = dict(
        pool_scale=d_pool_scale, sgu_ln_g=d_ln_g, sgu_ln_b=d_ln_b, sgu_w=d_sgu_w, sgu_b=d_sgu_b,
        mem_norm=d_mem_norm, branch_norm=d_branch_norm, norm_post=d_norm_post,
    )
    packed = jnp.concatenate([small_grads[n].reshape(-1, 128) for n in SMALL if n != "norm_pre"], axis=0)
    packed = jnp.broadcast_to(packed[None, None], (N_CHIPS, 2) + packed.shape)

    by_chip = lambda g: g.reshape((N_CHIPS, 2) + g.shape[1:])
    small_mine = [by_chip(g_wkv), by_chip(g_pool), packed]
    g_win, small_theirs = _weight_grad(h, dproj, N_DEV, "cols", "grad_w_in", [("pair", small_mine)])
    small_sums, _ = _pair_sum(small_mine, small_theirs, "pair_sum_small")
    g_wout, (l_wkv, l_pool, l_packed, win_theirs) = _weight_grad(
        y, dout, N_DEV, "rows", "grad_w_out", [("chips", list(small_sums)), ("pair", [by_chip(g_win)])]
    )
    (win_sums,), (wout_theirs,) = _pair_sum(
        [by_chip(g_win)], [win_theirs], "pair_sum_w_in", [("pair", [by_chip(g_wout)])]
    )
    (wout_sums,), _ = _pair_sum([by_chip(g_wout)], [wout_theirs], "pair_sum_w_out")
    grad_x, d_norm_pre, (l_win, l_wout) = _input_grad(
        dproj, win, x, dxo, small["norm_pre"], [("chips", [win_sums, wout_sums])]
    )
    return loss, grad_x, dict(w_in=l_win, w_out=l_wout, w_kv=l_wkv, pool_w=l_pool), l_packed, d_norm_pre


def kernel(x, mem, norm_pre, w_in, pool_w, pool_scale, sgu_ln_g, sgu_ln_b, sgu_w, sgu_b, mem_norm, w_kv, branch_norm, w_out, norm_post, loss_target, m_norm_pre, m_w_in, m_pool_w, m_pool_scale, m_sgu_ln_g, m_sgu_ln_b, m_sgu_w, m_sgu_b, m_mem_norm, m_w_kv, m_branch_norm, m_w_out, m_norm_post, v_norm_pre, v_w_in, v_pool_w, v_pool_scale, v_sgu_ln_g, v_sgu_ln_b, v_sgu_w, v_sgu_b, v_mem_norm, v_w_kv, v_branch_norm, v_w_out, v_norm_post):
    weights = dict(norm_pre=norm_pre, w_in=w_in, pool_w=pool_w, pool_scale=pool_scale, sgu_ln_g=sgu_ln_g, sgu_ln_b=sgu_ln_b, sgu_w=sgu_w, sgu_b=sgu_b, mem_norm=mem_norm, w_kv=w_kv, branch_norm=branch_norm, w_out=w_out, norm_post=norm_post)
    first = dict(norm_pre=m_norm_pre, w_in=m_w_in, pool_w=m_pool_w, pool_scale=m_pool_scale, sgu_ln_g=m_sgu_ln_g, sgu_ln_b=m_sgu_ln_b, sgu_w=m_sgu_w, sgu_b=m_sgu_b, mem_norm=m_mem_norm, w_kv=m_w_kv, branch_norm=m_branch_norm, w_out=m_w_out, norm_post=m_norm_post)
    second = dict(norm_pre=v_norm_pre, w_in=v_w_in, pool_w=v_pool_w, pool_scale=v_pool_scale, sgu_ln_g=v_sgu_ln_g, sgu_ln_b=v_sgu_ln_b, sgu_w=v_sgu_w, sgu_b=v_sgu_b, mem_norm=v_mem_norm, w_kv=v_w_kv, branch_norm=v_branch_norm, w_out=v_w_out, norm_post=v_norm_post)
    order = ("norm_pre", "w_in", "pool_w", "pool_scale", "sgu_ln_g", "sgu_ln_b", "sgu_w", "sgu_b", "mem_norm", "w_kv", "branch_norm", "w_out", "norm_post")

    owned_shape = dict(
        w_in=(D_MODEL, WIN_BLK), w_out=(WOUT_BLK, D_MODEL), w_kv=(WKV_BLK, 2 * BRANCH),
        pool_w=(len(POOL_WINDOWS) * POOL_BLK, POOL_GROUP_DIM),
    )
    owned = {n: weights[n].reshape(owned_shape[n]) for n in owned_shape}
    small = {n: _local_view(n, weights[n]) for n in SMALL}
    loss, grad_x, landed, landed_packed, d_norm_pre = _forward_backward(
        x[0], mem[0], loss_target[0], [owned[n].astype(BF16) for n in ("w_in", "w_kv", "pool_w", "w_out")], small
    )
    d_norm_pre = d_norm_pre.reshape(-1, 128)
    (landed_norm_pre,) = _exchange(
        [("all", [jnp.broadcast_to(d_norm_pre[None], (N_DEV,) + d_norm_pre.shape)])], "exchange_norm_pre"
    )

    grads, deltas, new_m, new_v = {}, {}, {}, {}
    for n in owned_shape:
        shape = weights[n].shape
        res = _adamw(
            landed[n], owned[n], first[n].reshape(owned_shape[n]), second[n].reshape(owned_shape[n]), "adamw_" + n
        )
        grads[n], deltas[n], new_m[n], new_v[n] = (a.reshape(shape) for a in res)
    rest = [n for n in SMALL if n != "norm_pre"]
    flat = lambda tree, names: jnp.concatenate([tree[n].reshape(-1, 128) for n in names], axis=0)
    res_rest = _adamw(landed_packed, flat(weights, rest), flat(first, rest), flat(second, rest), "adamw_replicated")
    res_pre = _adamw(
        landed_norm_pre, flat(weights, ["norm_pre"]), flat(first, ["norm_pre"]), flat(second, ["norm_pre"]),
        "adamw_norm_pre",
    )
    at = 0
    for n in rest:
        size = weights[n].size // 128
        for tree, a in zip((grads, deltas, new_m, new_v), res_rest):
            tree[n] = a[at : at + size].reshape(weights[n].shape)
        at += size
    for tree, a in zip((grads, deltas, new_m, new_v), res_pre):
        tree["norm_pre"] = a.reshape(weights["norm_pre"].shape)

    total = lax.psum(loss[0, 0], ("x", "y", "c"))
    return (
        total,
        grad_x[None],
        *[grads[n] for n in order],
        *[deltas[n] for n in order],
        *[new_m[n] for n in order],
        *[new_v[n] for n in order],
    )
```

```python
import functools

import jax
import jax.numpy as jnp
from jax import lax
from jax.experimental import pallas as pl
from jax.experimental.pallas import tpu as pltpu

F32 = jnp.float32
BF16 = jnp.bfloat16
EPS = 1e-6

D_MODEL = 2048
POOL_WINDOWS = (2, 4, 8, 16)
POOL_GROUP_DIM = 256
BRANCH = 1024
SGU_CHUNK = 128
SGU_HEADS = 8
SGU_HEAD_DIM = 128
XATTN_HEADS = 4
XATTN_HEAD_DIM = 256
MIX_WIDTH = 3 * BRANCH
IN_WIDTH = 7 * BRANCH
N_DEV = 8
WIN_BLK = IN_WIDTH // N_DEV
WOUT_BLK = MIX_WIDTH // N_DEV
WKV_BLK = D_MODEL // N_DEV
POOL_BLK = POOL_GROUP_DIM // N_DEV
HALO = 16

ADAM_LR = 0.001
ADAM_B1 = 0.9
ADAM_B2 = 0.999
ADAM_EPS = 1e-08
ADAM_WD = 0.01
ADAM_STEP = 10

VMEM_LIMIT_BYTES = 56 * 1024 * 1024
VMEM_LIMIT_MIX_BYTES = 63 * 1024 * 1024

TILE_PROJ = 512
TILE_MIX = 128
TILE_GRAD = 512
TILE_ADAM_BYTES = 1 << 20

ANY = pl.BlockSpec(memory_space=pl.ANY)
NN = (((1,), (0,)), ((), ()))
NT = (((1,), (1,)), ((), ()))
TN = (((0,), (0,)), ((), ()))
MESH = pl.DeviceIdType.MESH


def _dot(a, b, dims=NN):
    return lax.dot_general(a, b, dims, preferred_element_type=F32)


def _params(*semantics, vmem_limit_bytes=VMEM_LIMIT_BYTES):
    return pltpu.CompilerParams(dimension_semantics=semantics, vmem_limit_bytes=vmem_limit_bytes)


def _rowmean(a):
    return jnp.mean(a, axis=-1, keepdims=True)


def _colsum(a):
    return jnp.sum(a, axis=0, keepdims=True)


def _full(shape):
    zeros = (0,) * len(shape)
    return pl.BlockSpec(shape, lambda *_: zeros)


def _resident(shape):
    zeros = (0,) * len(shape)
    return pl.BlockSpec(shape, lambda *_: zeros, pipeline_mode=pl.Buffered(1))


def _kv_forward(mem, mem_norm, wkv):
    m, d = mem.shape

    def body(mem_ref, g_ref, w_ref, memn_ref, kv_ref):
        mv = mem_ref[...]
        r = lax.rsqrt(_rowmean(mv * mv) + EPS)
        memn = (mv * r * g_ref[...]).astype(BF16)
        memn_ref[...] = memn
        kv_ref[...] = _dot(memn, w_ref[...]).astype(BF16)

    return pl.pallas_call(
        body,
        name="kv_forward",
        grid=(1,),
        in_specs=[_full((m, d)), _full((1, d)), _full(wkv.shape)],
        out_specs=[_full((m, d)), _full((m, wkv.shape[1]))],
        out_shape=[jax.ShapeDtypeStruct((m, d), BF16), jax.ShapeDtypeStruct((m, wkv.shape[1]), BF16)],
        compiler_params=_params("arbitrary"),
    )(mem, mem_norm, wkv)


def _proj_gather(x_in, norm_pre, shards):
    s, d = x_in.shape
    t = min(TILE_PROJ, s)
    n_t = s // t
    n_arr = len(shards)

    def places(x, y, c):
        return (x, y, c), (x, y, 1 - c), (x ^ c, y ^ (1 - c)), (x ^ (1 - c), y ^ c), (1 - x, 1 - y)

    def index(chip, core):
        return 4 * chip[0] + 2 * chip[1] + core

    _, _, chip_a, chip_b, chip_d = places(*_position())
    c_out = lax.axis_index("c")
    me_out = index((lax.axis_index("x"), lax.axis_index("y")), c_out)
    order = jnp.stack(
        [
            me_out, me_out ^ 1, index(chip_a, c_out), index(chip_b, c_out), index(chip_b, 1 - c_out),
            index(chip_a, 1 - c_out), index(chip_d, c_out), index(chip_d, 1 - c_out),
        ]
    ).astype(jnp.int32)

    def body(order_ref, x_ref, g_ref, *refs):
        del order_ref
        src = refs[:n_arr]
        proj_ref, h_ref = refs[n_arr : n_arr + 2]
        out = refs[n_arr + 2 : 2 * n_arr + 2]
        wbuf, hs, send_sems, recv_sems, local_sems, load_sems = refs[2 * n_arr + 2 :]
        j = pl.program_id(0)
        i = pl.program_id(1)
        me, sibling, chip_a, chip_b, chip_d = places(*_position())
        c = me[2]

        def block(a, chip, core):
            return out[a].at[index(chip, core)]

        def copy(a, k, owner, to, from_input=False):
            return pltpu.make_async_remote_copy(
                src_ref=src[a] if from_input else block(a, *owner),
                dst_ref=block(a, *owner),
                send_sem=send_sems.at[a, k],
                recv_sem=recv_sems.at[a, k],
                device_id=to,
                device_id_type=MESH,
            )

        mine = (me[:2], c)

        def own(a):
            return pltpu.make_async_copy(src[a], block(a, *mine), local_sems.at[a])

        def first_sends(a):
            return [
                copy(a, 0, mine, sibling, from_input=True),
                copy(a, 1, mine, (*chip_a, c), from_input=True),
                copy(a, 2, mine, (*chip_b, c), from_input=True),
            ]

        def onward(a, k):
            owner = {3: chip_a, 4: chip_a, 5: chip_b, 6: chip_d}[k]
            return copy(a, k, (owner, c), (*chip_b, c) if k == 3 else sibling)

        def landed(a, k):
            owner = {0: mine[0], 1: chip_a, 2: chip_b, 3: chip_d, 4: chip_b, 5: chip_a, 6: chip_d}[k]
            core = c if k in (1, 2, 3) else 1 - c
            copy(a, k, (owner, core), me).wait_recv()
            return owner, core

        def load(ref, step):
            return pltpu.make_async_copy(ref, wbuf.at[step % 2], load_sems.at[step % 2])

        @pl.when(jnp.logical_and(j == 0, i == 0))
        def _():
            own(0).start()
            for cp in first_sends(0):
                cp.start()
            load(src[0], 0).start()
            load(src[0], 0).wait()

        steps = {1: (0, ()), 2: (1, (3, 4)), 3: (2, (5,)), 4: (4, ()), 5: (5, ()), 6: (3, (6,)), 7: (6, ())}
        for step, (k, then) in steps.items():

            @pl.when(jnp.logical_and(j == step, i == 0))
            def _():
                load(src[0], step).wait()

            @pl.when(jnp.logical_and(j == step - 1, i == n_t - 1))
            def _():
                owner = landed(0, k)
                for k2 in then:
                    onward(0, k2).start()
                if step == 2:
                    for a in range(1, n_arr):
                        own(a).start()
                        for cp in first_sends(a):
                            cp.start()
                if step == 6:
                    for a in range(1, n_arr):
                        for k1, then1 in ((1, (3, 4)), (2, (5,))):
                            landed(a, k1)
                            for k2 in then1:
                                onward(a, k2).start()
                load(block(0, *owner), step).start()

        @pl.when(j == 0)
        def _():
            xv = x_ref[...]
            h = (xv * lax.rsqrt(_rowmean(xv * xv) + EPS) * g_ref[...]).astype(BF16)
            hs[i] = h
            h_ref[...] = h

        proj_ref[...] = _dot(hs[i], wbuf[j % 2]).astype(BF16)

        @pl.when(jnp.logical_and(j == N_DEV - 1, i == n_t - 1))
        def _():
            for a in range(1, n_arr):
                landed(a, 3)
                onward(a, 6).start()
            for a in range(1, n_arr):
                for k in (0, 4, 5, 6):
                    landed(a, k)
            for a in range(n_arr):
                for cp in first_sends(a) + [onward(a, k) for k in (3, 4, 5, 6)]:
                    cp.wait_send()
                own(a).wait()

    res = pl.pallas_call(
        body,
        name="proj_gather",
        grid_spec=pltpu.PrefetchScalarGridSpec(
            num_scalar_prefetch=1,
            grid=(N_DEV, n_t),
            in_specs=[
                pl.BlockSpec((t, d), lambda j, i, order_ref: (jnp.where(j == 0, i, n_t - 1), 0)),
                pl.BlockSpec((1, d), lambda j, i, order_ref: (0, 0)),
            ]
            + [ANY] * n_arr,
            out_specs=[
                pl.BlockSpec((t, WIN_BLK), lambda j, i, order_ref: (i, order_ref[j])),
                pl.BlockSpec((t, d), lambda j, i, order_ref: (jnp.where(j == 0, i, n_t - 1), 0)),
            ]
            + [ANY] * n_arr,
            scratch_shapes=[
                pltpu.VMEM((2,) + shards[0].shape, BF16),
                pltpu.VMEM((n_t, t, d), BF16),
                pltpu.SemaphoreType.DMA((n_arr, 7)),
                pltpu.SemaphoreType.DMA((n_arr, 7)),
                pltpu.SemaphoreType.DMA((n_arr,)),
                pltpu.SemaphoreType.DMA((2,)),
            ],
        ),
        out_shape=[jax.ShapeDtypeStruct((s, IN_WIDTH), BF16), jax.ShapeDtypeStruct((s, d), BF16)]
        + [jax.ShapeDtypeStruct((N_DEV,) + a.shape, a.dtype) for a in shards],
        compiler_params=_params("arbitrary", "arbitrary"),
    )(order, x_in, norm_pre, *shards)
    return res[0], res[1], res[2:]


def _sigmoid(a):
    return jax.nn.sigmoid(a)


def _dsilu(a, sg):
    return sg * (1.0 + a * (1.0 - sg))


def _rms_fwd(u, gain):
    r = lax.rsqrt(_rowmean(u * u) + EPS)
    n = u * r
    return r, n, n * gain


def _rms_bwd(dy, gain, r, n):
    dn = dy * gain
    return _colsum(dy * n), r * (dn - n * _rowmean(dn * n))


def _mix(proj, x, target, kv, kv_t, wout, wout_t, pool_w, pool_w_t, pool_scale, ln_g, ln_b, sgu_bias, sgu_wm, sgu_wm_t, branch_norm, norm_post):
    s, d = x.shape
    t = min(TILE_MIX, s)
    n_tiles = s // t
    n_chunks = t // SGU_CHUNK
    halo_blocks_per_tile = t // HALO
    inv_d = 1.0 / d
    scale = 1.0 / (XATTN_HEAD_DIM**0.5)

    def body(
        proj_ref, halo_ref, x_ref, tgt_ref, kv_ref, kvt_ref, wout_ref, wout_t_ref, pw_ref, pwt_ref, pscale_ref, lng_ref,
        lnb_ref, bias_ref, wm_ref, wmt_ref, bnorm_ref, gpost_ref,
        y_ref, dout_ref, dxo_ref, dproj_ref, loss_ref, dgpost_ref, dbnorm_ref, dpscale_ref, dlng_ref, dlnb_ref,
        dpw_out, dwm_out, dbias_ref, dkv_out,
        carry_ref, dzsum_ref, dpw_ref, dwm_ref, dkv_ref,
    ):
        i = pl.program_id(0)
        tile = n_tiles - 1 - i

        @pl.when(i == 0)
        def _():
            carry_ref[...] = jnp.zeros_like(carry_ref)
            dzsum_ref[...] = jnp.zeros_like(dzsum_ref)
            for ref in (loss_ref, dgpost_ref, dbnorm_ref, dpscale_ref, dlng_ref, dlnb_ref, dpw_ref, dwm_ref, dkv_ref):
                ref[...] = jnp.zeros_like(ref)

        t_glob = tile * t + lax.broadcasted_iota(jnp.int32, (t, 1), 0)
        inv_cnt = [1.0 / jnp.minimum(t_glob + 1, w).astype(F32) for w in POOL_WINDOWS]

        xa = proj_ref[:, 0:BRANCH].astype(F32)
        ga = proj_ref[:, BRANCH : 2 * BRANCH].astype(F32)
        halo = jnp.where(tile == 0, 0.0, halo_ref[...].astype(F32))
        d_bf, pm_parts = [], []
        for g, w in enumerate(POOL_WINDOWS):
            cols = slice(g * POOL_GROUP_DIM, (g + 1) * POOL_GROUP_DIM)
            acc = jnp.concatenate([halo[:, cols], xa[:, cols]], axis=0)
            k = 1
            while k < w:
                acc = acc + pltpu.roll(acc, k, axis=0)
                k *= 2
            dg = (acc[HALO:, :] * inv_cnt[g] - xa[:, cols]).astype(BF16)
            d_bf.append(dg)
            pm_parts.append(_dot(dg, pw_ref[g]))
        pm = jnp.concatenate(pm_parts, axis=1)
        pscale = pscale_ref[...]
        pa = pm * pscale
        sga = _sigmoid(ga)
        sila = ga * sga
        ua = pa * sila
        g_a = bnorm_ref[:, 0:BRANCH]
        ra, na, ya = _rms_fwd(ua, g_a)

        u = proj_ref[:, 2 * BRANCH : 3 * BRANCH].astype(F32)
        v = proj_ref[:, 3 * BRANCH : 4 * BRANCH].astype(F32)
        gb = proj_ref[:, 4 * BRANCH : 5 * BRANCH].astype(F32)
        lng = lng_ref[...]
        vc = v - _rowmean(v)
        rstd = lax.rsqrt(_rowmean(vc * vc) + EPS)
        vhat = vc * rstd
        vn_bf = (vhat * lng + lnb_ref[...]).astype(BF16)
        z_rows = []
        for c in range(n_chunks):
            rows = slice(c * SGU_CHUNK, (c + 1) * SGU_CHUNK)
            z_rows.append(
                jnp.concatenate(
                    [
                        _dot(wm_ref[hd], vn_bf[rows, hd * SGU_HEAD_DIM : (hd + 1) * SGU_HEAD_DIM])
                        for hd in range(SGU_HEADS)
                    ],
                    axis=1,
                )
                + bias_ref[...]
            )
        z = z_rows[0] if n_chunks == 1 else jnp.concatenate(z_rows, axis=0)
        sb = u * z
        sgb = _sigmoid(gb)
        silb = gb * sgb
        ub = sb * silb
        g_b = bnorm_ref[:, BRANCH : 2 * BRANCH]
        rb, nb, yb = _rms_fwd(ub, g_b)

        q = proj_ref[:, 5 * BRANCH : 6 * BRANCH]
        gc = proj_ref[:, 6 * BRANCH : 7 * BRANCH].astype(F32)
        q_bf, p_bf, o_parts = [], [], []
        for hd in range(XATTN_HEADS):
            cols = slice(hd * XATTN_HEAD_DIM, (hd + 1) * XATTN_HEAD_DIM)
            qh = q[:, cols]
            sc = _dot(qh, kvt_ref[cols, :]) * scale
            e = jnp.exp(sc - jnp.max(sc, axis=-1, keepdims=True))
            p = e / jnp.sum(e, axis=-1, keepdims=True)
            q_bf.append(qh)
            p_bf.append(p.astype(BF16))
            o_parts.append(_dot(p_bf[hd], kv_ref[:, BRANCH + hd * XATTN_HEAD_DIM : BRANCH + (hd + 1) * XATTN_HEAD_DIM]))
        o = jnp.concatenate(o_parts, axis=1)
        sgc = _sigmoid(gc)
        silc = gc * sgc
        uc = o * silc
        g_c = bnorm_ref[:, 2 * BRANCH : 3 * BRANCH]
        rc, nc, yc = _rms_fwd(uc, g_c)

        out = None
        for b, y_branch in enumerate((ya, yb, yc)):
            rows = slice(b * BRANCH, (b + 1) * BRANCH)
            y_bf = y_branch.astype(BF16)
            y_ref[:, rows] = y_bf
            part = _dot(y_bf, wout_ref[rows, :])
            out = part if out is None else out + part
        gpost = gpost_ref[...]
        r_out = lax.rsqrt(_rowmean(out * out) + EPS)
        on = out * r_out
        err = x_ref[...] + on * gpost - tgt_ref[...]
        loss_ref[...] += 0.5 * jnp.sum(_rowmean(err * err), axis=0, keepdims=True)

        dxo = err * inv_d
        dxo_ref[...] = dxo
        dgp, dout = _rms_bwd(dxo, gpost, r_out, on)
        dgpost_ref[...] += dgp
        dout_bf = dout.astype(BF16)
        dout_ref[...] = dout_bf
        dy = [_dot(dout_bf, wout_t_ref[:, b * BRANCH : (b + 1) * BRANCH]) for b in range(3)]

        dg_a, dua = _rms_bwd(dy[0], g_a, ra, na)
        dg_b, dub = _rms_bwd(dy[1], g_b, rb, nb)
        dg_c, duc = _rms_bwd(dy[2], g_c, rc, nc)
        dbnorm_ref[...] += jnp.concatenate([dg_a, dg_b, dg_c], axis=1)

        dpa = dua * sila
        dga = dua * pa * _dsilu(ga, sga)
        dpscale_ref[...] += _colsum(dpa * pm)
        dpm = dpa * pscale
        dxa_parts, carry_parts = [], []
        for g, w in enumerate(POOL_WINDOWS):
            cols = slice(g * POOL_GROUP_DIM, (g + 1) * POOL_GROUP_DIM)
            dpm_g = dpm[:, cols].astype(BF16)
            dd = _dot(dpm_g, pwt_ref[g])
            dpw_ref[g] += _dot(d_bf[g], dpm_g, TN)
            cg = dd * inv_cnt[g]
            carry_parts.append(cg[0:HALO, :])
            acc = jnp.concatenate([cg, carry_ref[:, cols]], axis=0)
            k = 1
            while k < w:
                acc = acc + pltpu.roll(acc, t + HALO - k, axis=0)
                k *= 2
            dxa_parts.append(acc[0:t, :] - dd)
        carry_ref[...] = jnp.concatenate(carry_parts, axis=1)
        dxa = jnp.concatenate(dxa_parts, axis=1)

        dsb = dub * silb
        dgb = dub * sb * _dsilu(gb, sgb)
        du = dsb * z
        dz = dsb * u
        dz_bf = dz.astype(BF16)
        dvn_rows = []
        dz_sum = None
        for c in range(n_chunks):
            rows = slice(c * SGU_CHUNK, (c + 1) * SGU_CHUNK)
            dz_sum = dz[rows, :] if dz_sum is None else dz_sum + dz[rows, :]
            parts = []
            for hd in range(SGU_HEADS):
                cols = slice(hd * SGU_HEAD_DIM, (hd + 1) * SGU_HEAD_DIM)
                parts.append(_dot(wmt_ref[hd], dz_bf[rows, cols]))
                dwm_ref[hd] += _dot(dz_bf[rows, cols], vn_bf[rows, cols], NT)
            dvn_rows.append(jnp.concatenate(parts, axis=1))
        dzsum_ref[...] += dz_sum
        dvn = dvn_rows[0] if n_chunks == 1 else jnp.concatenate(dvn_rows, axis=0)
        dlng_ref[...] += _colsum(dvn * vhat)
        dlnb_ref[...] += _colsum(dvn)
        dvh = dvn * lng
        dv = rstd * (dvh - _rowmean(dvh) - vhat * _rowmean(dvh * vhat))

        do = duc * silc
        dgc = duc * o * _dsilu(gc, sgc)
        dq_parts = []
        for hd in range(XATTN_HEADS):
            cols = slice(hd * XATTN_HEAD_DIM, (hd + 1) * XATTN_HEAD_DIM)
            vcols = slice(BRANCH + hd * XATTN_HEAD_DIM, BRANCH + (hd + 1) * XATTN_HEAD_DIM)
            do_h = do[:, cols].astype(BF16)
            p = p_bf[hd].astype(F32)
            dp = _dot(do_h, kvt_ref[vcols, :])
            dkv_ref[:, vcols] += _dot(p_bf[hd], do_h, TN)
            ds_bf = (p * (dp - jnp.sum(dp * p, axis=-1, keepdims=True)) * scale).astype(BF16)
            dq_parts.append(_dot(ds_bf, kv_ref[:, cols]))
            dkv_ref[:, cols] += _dot(ds_bf, q_bf[hd], TN)
        dq = jnp.concatenate(dq_parts, axis=1)

        dproj_ref[...] = jnp.concatenate([dxa, dga, du, dv, dgb, dq, dgc], axis=1).astype(BF16)

        @pl.when(i == n_tiles - 1)
        def _():
            keep = lax.broadcasted_iota(jnp.int32, (SGU_CHUNK, SGU_CHUNK), 0) >= lax.broadcasted_iota(
                jnp.int32, (SGU_CHUNK, SGU_CHUNK), 1
            )
            for hd in range(SGU_HEADS):
                dwm_ref[hd] = jnp.where(keep, dwm_ref[hd], 0.0)
                per_pos = dzsum_ref[:, hd * SGU_HEAD_DIM : (hd + 1) * SGU_HEAD_DIM]
                dbias_ref[hd : hd + 1, :] = _colsum(per_pos.T)
            for acc, res in ((dpw_ref, dpw_out), (dwm_ref, dwm_out), (dkv_ref, dkv_out)):
                pltpu.sync_copy(acc, res)

    row_tile = lambda width: pl.BlockSpec((t, width), lambda i: (n_tiles - 1 - i, 0))
    halo_spec = pl.BlockSpec(
        (HALO, BRANCH), lambda i: (jnp.maximum((n_tiles - 1 - i) * halo_blocks_per_tile - 1, 0), 0)
    )
    acc_shapes = [
        (1, 128),
        (1, d),
        (1, MIX_WIDTH),
        (1, BRANCH),
        (1, BRANCH),
        (1, BRANCH),
        pool_w.shape,
        sgu_wm.shape,
        (SGU_HEADS, SGU_CHUNK),
        kv.shape,
    ]
    return pl.pallas_call(
        body,
        name="mix",
        grid=(n_tiles,),
        in_specs=[
            row_tile(IN_WIDTH), halo_spec, row_tile(d), row_tile(d), _resident(kv.shape), _resident(kv_t.shape),
            _resident(wout.shape), _resident(wout_t.shape), _resident(pool_w.shape), _resident(pool_w_t.shape),
            _full((1, BRANCH)), _full((1, BRANCH)), _full((1, BRANCH)), _resident((SGU_CHUNK, BRANCH)),
            _resident(sgu_wm.shape), _resident(sgu_wm_t.shape), _full((1, MIX_WIDTH)), _full((1, d)),
        ],
        out_specs=[row_tile(MIX_WIDTH), row_tile(d), row_tile(d), row_tile(IN_WIDTH)]
        + [ANY if len(a) == 3 or a == kv.shape else _full(a) for a in acc_shapes],
        out_shape=[
            jax.ShapeDtypeStruct((s, MIX_WIDTH), BF16),
            jax.ShapeDtypeStruct((s, d), BF16),
            jax.ShapeDtypeStruct((s, d), F32),
            jax.ShapeDtypeStruct((s, IN_WIDTH), BF16),
        ]
        + [jax.ShapeDtypeStruct(a, F32) for a in acc_shapes],
        scratch_shapes=[
            pltpu.VMEM((HALO, BRANCH), F32), pltpu.VMEM((SGU_CHUNK, BRANCH), F32), pltpu.VMEM(pool_w.shape, F32),
            pltpu.VMEM(sgu_wm.shape, F32), pltpu.VMEM(kv.shape, F32),
        ],
        compiler_params=_params("arbitrary", vmem_limit_bytes=VMEM_LIMIT_MIX_BYTES),
    )(
        proj, proj, x, target, kv, kv_t, wout, wout_t, pool_w, pool_w_t, pool_scale, ln_g, ln_b, sgu_bias, sgu_wm,
        sgu_wm_t, branch_norm, norm_post,
    )


def _position():
    return lax.axis_index("x"), lax.axis_index("y"), lax.axis_index("c")


N_CHIPS = 4


def _landing_shape(kind, a):
    return (N_CHIPS,) + a.shape[2:] if kind == "pair" else a.shape


def _carry_specs(groups):
    arrays = [(kind, a) for kind, arrs in groups for a in arrs]
    scratch = []
    for _, arrs in groups:
        n = len(arrs)
        scratch += [pltpu.SemaphoreType.DMA((n, N_DEV)), pltpu.SemaphoreType.DMA((n, N_DEV)), pltpu.SemaphoreType.DMA((n,))]
    return dict(
        n=len(arrays),
        operands=[a for _, a in arrays],
        in_specs=[ANY] * len(arrays),
        out_specs=[ANY] * len(arrays),
        out_shape=[jax.ShapeDtypeStruct(_landing_shape(kind, a), a.dtype) for kind, a in arrays],
        scratch_shapes=scratch,
    )


def _carry(groups, src, out, sems):
    x, y, c = _position()
    chip = 2 * x + y
    me = 2 * chip + c

    def remote(s, d, send_sems, recv_sems, a, m, to):
        return pltpu.make_async_remote_copy(
            src_ref=s, dst_ref=d, send_sem=send_sems.at[a, m], recv_sem=recv_sems.at[a, m], device_id=to,
            device_id_type=MESH,
        )

    def copies():
        far, near = [], []
        at = 0
        for g, (kind, arrs) in enumerate(groups):
            send_sems, recv_sems, local_sems = sems[3 * g : 3 * g + 3]
            for a in range(len(arrs)):
                s, d = src[at + a], out[at + a]
                if kind == "pair":
                    far.append(remote(s.at[:, 1 - c], d, send_sems, recv_sems, a, 1, (x, y, 1 - c)))
                elif kind == "chips":
                    for m in range(1, N_CHIPS):
                        px, py = x ^ (m >> 1), y ^ (m & 1)
                        far.append(remote(s.at[2 * px + py], d.at[chip], send_sems, recv_sems, a, m, (px, py, c)))
                    near.append(pltpu.make_async_copy(s.at[chip], d.at[chip], local_sems.at[a]))
                else:
                    for m in range(1, N_DEV):
                        px, py, pc = x ^ ((m >> 2) & 1), y ^ ((m >> 1) & 1), c ^ (m & 1)
                        far.append(
                            remote(s.at[4 * px + 2 * py + pc], d.at[me], send_sems, recv_sems, a, m, (px, py, pc))
                        )
                    near.append(pltpu.make_async_copy(s.at[me], d.at[me], local_sems.at[a]))
            at += len(arrs)
        return far, near

    def start():
        far, near = copies()
        for cp in near + far:
            cp.start()

    def finish():
        far, near = copies()
        for cp in far:
            cp.wait_recv()
        for cp in far:
            cp.wait_send()
        for cp in near:
            cp.wait()

    return start, finish


def _exchange(groups, name):
    carried = _carry_specs(groups)
    n_c = carried["n"]

    def body(*refs):
        start, finish = _carry(groups, refs[:n_c], refs[n_c : 2 * n_c], refs[2 * n_c :])
        start()
        finish()

    return pl.pallas_call(
        body,
        name=name,
        in_specs=carried["in_specs"],
        out_specs=carried["out_specs"],
        out_shape=carried["out_shape"],
        scratch_shapes=carried["scratch_shapes"],
    )(*carried["operands"])


def _pair_sum(mine, theirs, name, groups=()):
    n = len(mine)
    carried = _carry_specs(groups)
    n_c = carried["n"]
    core = lax.axis_index("c").astype(jnp.int32).reshape(1)

    def body(core_ref, *refs):
        del core_ref
        own = refs[:n]
        sib = refs[n : 2 * n]
        src = refs[2 * n : 2 * n + n_c]
        out = refs[2 * n + n_c : 3 * n + n_c]
        landed = refs[3 * n + n_c : 3 * n + 2 * n_c]
        start, finish = _carry(groups, src, landed, refs[3 * n + 2 * n_c :])
        b = pl.program_id(0)

        @pl.when(b == 0)
        def _():
            start()

        for a in range(n):
            out[a][...] = (own[a][...].astype(F32) + sib[a][...].astype(F32)).astype(out[a].dtype)

        @pl.when(b == N_CHIPS - 1)
        def _():
            finish()

    block = lambda a: pl.BlockSpec((None,) + a.shape[1:], lambda b, core_ref: (b, 0, 0))
    res = pl.pallas_call(
        body,
        name=name,
        grid_spec=pltpu.PrefetchScalarGridSpec(
            num_scalar_prefetch=1,
            grid=(N_CHIPS,),
            in_specs=[pl.BlockSpec((None, None) + a.shape[2:], lambda b, core_ref: (b, core_ref[0], 0, 0)) for a in mine]
            + [block(a) for a in theirs]
            + carried["in_specs"],
            out_specs=[block(a) for a in theirs] + carried["out_specs"],
            scratch_shapes=carried["scratch_shapes"],
        ),
        out_shape=[jax.ShapeDtypeStruct(a.shape, a.dtype) for a in theirs] + carried["out_shape"],
        compiler_params=_params("arbitrary"),
    )(core, *mine, *theirs, *carried["operands"])
    return res[:n], res[n:]


def _weight_grad(a, b, n_blk, blocked, name, groups):
    s = a.shape[0]
    t = min(TILE_GRAD, s)
    n_t = s // t
    if blocked == "cols":
        k, c = a.shape[1], b.shape[1] // n_blk
        a_spec = pl.BlockSpec((t, k), lambda j, i: (i, 0))
        b_spec = pl.BlockSpec((t, c), lambda j, i: (i, j))
    else:
        k, c = a.shape[1] // n_blk, b.shape[1]
        a_spec = pl.BlockSpec((t, k), lambda j, i: (i, j))
        b_spec = pl.BlockSpec((t, c), lambda j, i: (i, 0))
    carried = _carry_specs(groups)
    n_p = carried["n"]

    def body(a_ref, b_ref, *refs):
        src = refs[:n_p]
        o_ref = refs[n_p]
        landed = refs[n_p + 1 : 2 * n_p + 1]
        acc_ref = refs[2 * n_p + 1]
        start, finish = _carry(groups, src, landed, refs[2 * n_p + 2 :])
        j = pl.program_id(0)
        i = pl.program_id(1)

        @pl.when(jnp.logical_and(j == 0, i == 0))
        def _():
            start()

        @pl.when(i == 0)
        def _():
            acc_ref[...] = jnp.zeros_like(acc_ref)

        acc_ref[...] += _dot(a_ref[...], b_ref[...], TN)

        @pl.when(i == n_t - 1)
        def _():
            o_ref[...] = acc_ref[...].astype(BF16)

        @pl.when(jnp.logical_and(j == n_blk - 1, i == n_t - 1))
        def _():
            finish()

    res = pl.pallas_call(
        body,
        name=name,
        grid=(n_blk, n_t),
        in_specs=[a_spec, b_spec] + carried["in_specs"],
        out_specs=[pl.BlockSpec((None, k, c), lambda j, i: (j, 0, 0))] + carried["out_specs"],
        out_shape=[jax.ShapeDtypeStruct((n_blk, k, c), BF16)] + carried["out_shape"],
        scratch_shapes=[pltpu.VMEM((k, c), F32)] + carried["scratch_shapes"],
        compiler_params=_params("arbitrary", "arbitrary"),
    )(a, b, *carried["operands"])
    return res[0], res[1:]


def _input_grad(dproj, win, x, dxo, norm_pre, groups):
    s, d = x.shape
    t = min(TILE_GRAD, s)
    n_t = s // t
    carried = _carry_specs(groups)
    n_p = carried["n"]

    def body(dp_ref, w_ref, x_ref, dxo_ref, g_ref, *refs):
        src = refs[:n_p]
        gx_ref, dg_ref = refs[n_p : n_p + 2]
        landed = refs[n_p + 2 : 2 * n_p + 2]
        acc_ref = refs[2 * n_p + 2]
        start, finish = _carry(groups, src, landed, refs[2 * n_p + 3 :])
        i = pl.program_id(0)
        j = pl.program_id(1)

        @pl.when(jnp.logical_and(i == 0, j == 0))
        def _():
            start()
            dg_ref[...] = jnp.zeros_like(dg_ref)

        @pl.when(j == 0)
        def _():
            acc_ref[...] = jnp.zeros_like(acc_ref)

        acc_ref[...] += _dot(dp_ref[...], w_ref[...], NT)

        @pl.when(j == N_DEV - 1)
        def _():
            xv = x_ref[...]
            gain = g_ref[...]
            r = lax.rsqrt(_rowmean(xv * xv) + EPS)
            dgain, dx = _rms_bwd(acc_ref[...], gain, r, xv * r)
            dg_ref[...] += dgain
            gx_ref[...] = dxo_ref[...] + dx

        @pl.when(jnp.logical_and(i == n_t - 1, j == N_DEV - 1))
        def _():
            finish()

    res = pl.pallas_call(
        body,
        name="input_grad",
        grid=(n_t, N_DEV),
        in_specs=[
            pl.BlockSpec((t, WIN_BLK), lambda i, j: (i, j)),
            pl.BlockSpec((None, d, WIN_BLK), lambda i, j: (j, 0, 0)),
            pl.BlockSpec((t, d), lambda i, j: (i, 0)),
            pl.BlockSpec((t, d), lambda i, j: (i, 0)),
            _full((1, d)),
        ]
        + carried["in_specs"],
        out_specs=[pl.BlockSpec((t, d), lambda i, j: (i, 0)), _full((1, d))] + carried["out_specs"],
        out_shape=[jax.ShapeDtypeStruct((s, d), F32), jax.ShapeDtypeStruct((1, d), F32)] + carried["out_shape"],
        scratch_shapes=[pltpu.VMEM((t, d), F32)] + carried["scratch_shapes"],
        compiler_params=_params("arbitrary", "arbitrary"),
    )(dproj, win, x, dxo, norm_pre, *carried["operands"])
    return res[0], res[1], res[2:]


def _kv_backward(dkv, memn, wkv, mem, mem_norm):
    m, d = mem.shape
    n = wkv.shape[1]

    def body(dkv_ref, memn_ref, w_ref, mem_ref, g_ref, gw_ref, dg_ref):
        dkv_bf = dkv_ref[...].astype(BF16)
        gw_ref[...] = _dot(memn_ref[...], dkv_bf, TN).astype(BF16).reshape(N_DEV, WKV_BLK, n)
        dmemn = _dot(dkv_bf, w_ref[...], NT)
        mv = mem_ref[...]
        r = lax.rsqrt(_rowmean(mv * mv) + EPS)
        dg_ref[...] = _colsum(dmemn * (mv * r))

    return pl.pallas_call(
        body,
        name="kv_backward",
        grid=(1,),
        in_specs=[_full((m, n)), _full((m, d)), _full(wkv.shape), _full((m, d)), _full((1, d))],
        out_specs=[_full((N_DEV, WKV_BLK, n)), _full((1, d))],
        out_shape=[jax.ShapeDtypeStruct((N_DEV, WKV_BLK, n), BF16), jax.ShapeDtypeStruct((1, d), F32)],
        compiler_params=_params("arbitrary"),
    )(dkv, memn, wkv, mem, mem_norm)


def _adamw_math(w, g, m, v):
    m = ADAM_B1 * m + (1.0 - ADAM_B1) * g
    v = ADAM_B2 * v + (1.0 - ADAM_B2) * (g * g)
    m_hat = m / (1.0 - ADAM_B1**ADAM_STEP)
    v_hat = v / (1.0 - ADAM_B2**ADAM_STEP)
    delta = -ADAM_LR * (m_hat / (jnp.sqrt(v_hat) + ADAM_EPS) + ADAM_WD * w)
    return delta, m, v


def _adamw(parts, w, m, v, name):
    r, c = w.shape
    slots = parts.shape[0]
    t = r
    while t * c * 4 > TILE_ADAM_BYTES and t % 16 == 0:
        t //= 2

    def body(p_ref, w_ref, m_ref, v_ref, g_ref, d_ref, nm_ref, nv_ref):
        g = p_ref[0].astype(F32)
        for k in range(1, slots):
            g = g + p_ref[k].astype(F32)
        delta, nm, nv = _adamw_math(w_ref[...], g, m_ref[...], v_ref[...])
        g_ref[...] = g
        d_ref[...] = delta
        nm_ref[...] = nm
        nv_ref[...] = nv

    tile = pl.BlockSpec((t, c), lambda i: (i, 0))
    return pl.pallas_call(
        body,
        name=name,
        grid=(r // t,),
        in_specs=[pl.BlockSpec((slots, t, c), lambda i: (0, i, 0)), tile, tile, tile],
        out_specs=[tile] * 4,
        out_shape=[jax.ShapeDtypeStruct((r, c), F32)] * 4,
        compiler_params=_params("parallel"),
    )(parts, w, m, v)


def _adamw_packed(parts, triples, name):
    slots = parts.shape[0]
    sizes = [w.shape[0] for w, _, _ in triples]

    def body(p_ref, *refs):
        ins = refs[: 3 * len(triples)]
        outs = refs[3 * len(triples) :]
        at = 0
        for n, rows in enumerate(sizes):
            g = p_ref[0, at : at + rows, :]
            for k in range(1, slots):
                g = g + p_ref[k, at : at + rows, :]
            w_ref, m_ref, v_ref = ins[3 * n : 3 * n + 3]
            delta, nm, nv = _adamw_math(w_ref[...], g, m_ref[...], v_ref[...])
            for ref, val in zip(outs[4 * n : 4 * n + 4], (g, delta, nm, nv)):
                ref[...] = val
            at += rows

    flat = [a for t in triples for a in t]
    res = pl.pallas_call(
        body,
        name=name,
        out_shape=[jax.ShapeDtypeStruct(w.shape, F32) for w, _, _ in triples for _ in range(4)],
        compiler_params=pltpu.CompilerParams(vmem_limit_bytes=VMEM_LIMIT_BYTES),
    )(parts, *flat)
    return [res[4 * n : 4 * n + 4] for n in range(len(triples))]


SMALL = ("norm_pre", "pool_scale", "sgu_ln_g", "sgu_ln_b", "sgu_w", "sgu_b", "mem_norm", "branch_norm", "norm_post")


def _local_view(name, w):
    if name == "sgu_w":
        return w.reshape(SGU_HEADS, SGU_CHUNK, SGU_CHUNK)
    if name == "sgu_b":
        return w.reshape(SGU_HEADS, SGU_CHUNK)
    return w.reshape(1, -1)


def _forward_backward(x, mem, target, shards, small):
    causal = jnp.tril(jnp.ones((SGU_CHUNK, SGU_CHUNK), dtype=bool))
    sgu_wm = jnp.where(causal[None], small["sgu_w"], 0.0).astype(BF16)
    sgu_bias = jnp.repeat(jnp.transpose(small["sgu_b"]), SGU_HEAD_DIM, axis=1)

    proj, h, (win, wkv, pool_all, wout) = _proj_gather(x, small["norm_pre"], shards)
    wout = wout.reshape(MIX_WIDTH, D_MODEL)
    wkv = wkv.reshape(D_MODEL, 2 * BRANCH)
    pool_full = (
        pool_all.reshape(N_DEV, len(POOL_WINDOWS), POOL_BLK, POOL_GROUP_DIM)
        .transpose(1, 0, 2, 3)
        .reshape(len(POOL_WINDOWS), POOL_GROUP_DIM, POOL_GROUP_DIM)
    )
    memn, kv = _kv_forward(mem, small["mem_norm"], wkv)
    (y, dout, dxo, dproj, loss, d_norm_post, d_branch_norm, d_pool_scale, d_ln_g, d_ln_b, d_pool_w, d_sgu_w, d_sgu_b,
     dkv) = _mix(
        proj, x, target, kv, kv.T, wout, wout.T, pool_full, jnp.swapaxes(pool_full, 1, 2), small["pool_scale"],
        small["sgu_ln_g"], small["sgu_ln_b"], sgu_bias, sgu_wm, jnp.swapaxes(sgu_wm, 1, 2), small["branch_norm"],
        small["norm_post"],
    )
    g_wkv, d_mem_norm = _kv_backward(dkv, memn, wkv, mem, small["mem_norm"])
    g_pool = (
        d_pool_w.reshape(len(POOL_WINDOWS), N_DEV, POOL_BLK, POOL_GROUP_DIM)
        .transpose(1, 0, 2, 3)
        .reshape(N_DEV, len(POOL_WINDOWS) * POOL_BLK, POOL_GROUP_DIM)
        .astype(BF16)
    )
    small_grads = dict(
        pool_scale=d_pool_scale, sgu_ln_g=d_ln_g, sgu_ln_b=d_ln_b, sgu_w=d_sgu_w, sgu_b=d_sgu_b,
        mem_norm=d_mem_norm, branch_norm=d_branch_norm, norm_post=d_norm_post,
    )
    packed = jnp.concatenate([small_grads[n].reshape(-1, 128) for n in SMALL if n != "norm_pre"], axis=0)
    packed = jnp.broadcast_to(packed[None, None], (N_CHIPS, 2) + packed.shape)

    by_chip = lambda g: g.reshape((N_CHIPS, 2) + g.shape[1:])
    small_mine = [by_chip(g_wkv), by_chip(g_pool), packed]
    g_win, small_theirs = _weight_grad(h, dproj, N_DEV, "cols", "grad_w_in", [("pair", small_mine)])
    small_sums, _ = _pair_sum(small_mine, small_theirs, "pair_sum_small")
    g_wout, (l_wkv, l_pool, l_packed, win_theirs) = _weight_grad(
        y, dout, N_DEV, "rows", "grad_w_out", [("chips", list(small_sums)), ("pair", [by_chip(g_win)])]
    )
    (win_sums,), (wout_theirs,) = _pair_sum(
        [by_chip(g_win)], [win_theirs], "pair_sum_w_in", [("pair", [by_chip(g_wout)])]
    )
    (wout_sums,), _ = _pair_sum([by_chip(g_wout)], [wout_theirs], "pair_sum_w_out")
    grad_x, d_norm_pre, (l_win, l_wout) = _input_grad(
        dproj, win, x, dxo, small["norm_pre"], [("chips", [win_sums, wout_sums])]
    )
    return loss, grad_x, dict(w_in=l_win, w_out=l_wout, w_kv=l_wkv, pool_w=l_pool), l_packed, d_norm_pre


def kernel(x, mem, norm_pre, w_in, pool_w, pool_scale, sgu_ln_g, sgu_ln_b, sgu_w, sgu_b, mem_norm, w_kv, branch_norm, w_out, norm_post, loss_target, m_norm_pre, m_w_in, m_pool_w, m_pool_scale, m_sgu_ln_g, m_sgu_ln_b, m_sgu_w, m_sgu_b, m_mem_norm, m_w_kv, m_branch_norm, m_w_out, m_norm_post, v_norm_pre, v_w_in, v_pool_w, v_pool_scale, v_sgu_ln_g, v_sgu_ln_b, v_sgu_w, v_sgu_b, v_mem_norm, v_w_kv, v_branch_norm, v_w_out, v_norm_post):
    weights = dict(norm_pre=norm_pre, w_in=w_in, pool_w=pool_w, pool_scale=pool_scale, sgu_ln_g=sgu_ln_g, sgu_ln_b=sgu_ln_b, sgu_w=sgu_w, sgu_b=sgu_b, mem_norm=mem_norm, w_kv=w_kv, branch_norm=branch_norm, w_out=w_out, norm_post=norm_post)
    first = dict(norm_pre=m_norm_pre, w_in=m_w_in, pool_w=m_pool_w, pool_scale=m_pool_scale, sgu_ln_g=m_sgu_ln_g, sgu_ln_b=m_sgu_ln_b, sgu_w=m_sgu_w, sgu_b=m_sgu_b, mem_norm=m_mem_norm, w_kv=m_w_kv, branch_norm=m_branch_norm, w_out=m_w_out, norm_post=m_norm_post)
    second = dict(norm_pre=v_norm_pre, w_in=v_w_in, pool_w=v_pool_w, pool_scale=v_pool_scale, sgu_ln_g=v_sgu_ln_g, sgu_ln_b=v_sgu_ln_b, sgu_w=v_sgu_w, sgu_b=v_sgu_b, mem_norm=v_mem_norm, w_kv=v_w_kv, branch_norm=v_branch_norm, w_out=v_w_out, norm_post=v_norm_post)
    order = ("norm_pre", "w_in", "pool_w", "pool_scale", "sgu_ln_g", "sgu_ln_b", "sgu_w", "sgu_b", "mem_norm", "w_kv", "branch_norm", "w_out", "norm_post")

    owned_shape = dict(
        w_in=(D_MODEL, WIN_BLK), w_out=(WOUT_BLK, D_MODEL), w_kv=(WKV_BLK, 2 * BRANCH),
        pool_w=(len(POOL_WINDOWS) * POOL_BLK, POOL_GROUP_DIM),
    )
    owned = {n: weights[n].reshape(owned_shape[n]) for n in owned_shape}
    small = {n: _local_view(n, weights[n]) for n in SMALL}
    loss, grad_x, landed, landed_packed, d_norm_pre = _forward_backward(
        x[0], mem[0], loss_target[0], [owned[n].astype(BF16) for n in ("w_in", "w_kv", "pool_w", "w_out")], small
    )
    d_norm_pre = d_norm_pre.reshape(-1, 128)
    (landed_norm_pre,) = _exchange(
        [("all", [jnp.broadcast_to(d_norm_pre[None], (N_DEV,) + d_norm_pre.shape)])], "exchange_norm_pre"
    )

    grads, deltas, new_m, new_v = {}, {}, {}, {}
    for n in owned_shape:
        shape = weights[n].shape
        res = _adamw(
            landed[n], owned[n], first[n].reshape(owned_shape[n]), second[n].reshape(owned_shape[n]), "adamw_" + n
        )
        grads[n], deltas[n], new_m[n], new_v[n] = (a.reshape(shape) for a in res)
    rows_of = lambda tree, n: tree[n].reshape(-1, 128)
    for names, parts, name in (
        ([n for n in SMALL if n != "norm_pre"], landed_packed, "adamw_replicated"),
        (["norm_pre"], landed_norm_pre, "adamw_norm_pre"),
    ):
        res = _adamw_packed(parts, [(rows_of(weights, n), rows_of(first, n), rows_of(second, n)) for n in names], name)
        for n, four in zip(names, res):
            for tree, a in zip((grads, deltas, new_m, new_v), four):
                tree[n] = a.reshape(weights[n].shape)

    total = lax.psum(loss[0, 0], ("x", "y", "c"))
    return (
        total,
        grad_x[None],
        *[grads[n] for n in order],
        *[deltas[n] for n in order],
        *[new_m[n] for n in order],
        *[new_v[n] for n in order],
    )
```

```python
import functools

import jax
import jax.numpy as jnp
from jax import lax
from jax.experimental import pallas as pl
from jax.experimental.pallas import tpu as pltpu

F32 = jnp.float32
BF16 = jnp.bfloat16
EPS = 1e-6

D_MODEL = 2048
POOL_WINDOWS = (2, 4, 8, 16)
POOL_GROUP_DIM = 256
BRANCH = 1024
SGU_CHUNK = 128
SGU_HEADS = 8
SGU_HEAD_DIM = 128
XATTN_HEADS = 4
XATTN_HEAD_DIM = 256
MIX_WIDTH = 3 * BRANCH
IN_WIDTH = 7 * BRANCH
N_DEV = 8
WIN_BLK = IN_WIDTH // N_DEV
WOUT_BLK = MIX_WIDTH // N_DEV
WKV_BLK = D_MODEL // N_DEV
POOL_BLK = POOL_GROUP_DIM // N_DEV
HALO = 16

ADAM_LR = 0.001
ADAM_B1 = 0.9
ADAM_B2 = 0.999
ADAM_EPS = 1e-08
ADAM_WD = 0.01
ADAM_STEP = 10

VMEM_LIMIT_BYTES = 56 * 1024 * 1024
VMEM_LIMIT_MIX_BYTES = 63 * 1024 * 1024

TILE_PROJ = 512
TILE_MIX = 128
TILE_GRAD = 512
TILE_ADAM_BYTES = 1 << 20

ANY = pl.BlockSpec(memory_space=pl.ANY)
NN = (((1,), (0,)), ((), ()))
NT = (((1,), (1,)), ((), ()))
TN = (((0,), (0,)), ((), ()))
MESH = pl.DeviceIdType.MESH


def _dot(a, b, dims=NN):
    return lax.dot_general(a, b, dims, preferred_element_type=F32)


def _params(*semantics, vmem_limit_bytes=VMEM_LIMIT_BYTES):
    return pltpu.CompilerParams(dimension_semantics=semantics, vmem_limit_bytes=vmem_limit_bytes)


def _rowmean(a):
    return jnp.mean(a, axis=-1, keepdims=True)


def _colsum(a):
    return jnp.sum(a, axis=0, keepdims=True)


def _full(shape):
    zeros = (0,) * len(shape)
    return pl.BlockSpec(shape, lambda *_: zeros)


def _resident(shape):
    zeros = (0,) * len(shape)
    return pl.BlockSpec(shape, lambda *_: zeros, pipeline_mode=pl.Buffered(1))


def _kv_forward(mem, mem_norm, wkv):
    m, d = mem.shape

    def body(mem_ref, g_ref, w_ref, memn_ref, kv_ref):
        mv = mem_ref[...]
        r = lax.rsqrt(_rowmean(mv * mv) + EPS)
        memn = (mv * r * g_ref[...]).astype(BF16)
        memn_ref[...] = memn
        kv_ref[...] = _dot(memn, w_ref[...]).astype(BF16)

    return pl.pallas_call(
        body,
        name="kv_forward",
        grid=(1,),
        in_specs=[_full((m, d)), _full((1, d)), _full(wkv.shape)],
        out_specs=[_full((m, d)), _full((m, wkv.shape[1]))],
        out_shape=[jax.ShapeDtypeStruct((m, d), BF16), jax.ShapeDtypeStruct((m, wkv.shape[1]), BF16)],
        compiler_params=_params("arbitrary"),
    )(mem, mem_norm, wkv)


def _proj_gather(x_in, norm_pre, shards):
    s, d = x_in.shape
    t = min(TILE_PROJ, s)
    n_t = s // t
    n_arr = len(shards)

    def places(x, y, c):
        return (x, y, c), (x, y, 1 - c), (x ^ c, y ^ (1 - c)), (x ^ (1 - c), y ^ c), (1 - x, 1 - y)

    def index(chip, core):
        return 4 * chip[0] + 2 * chip[1] + core

    _, _, chip_a, chip_b, chip_d = places(*_position())
    c_out = lax.axis_index("c")
    me_out = index((lax.axis_index("x"), lax.axis_index("y")), c_out)
    order = jnp.stack(
        [
            me_out, me_out ^ 1, index(chip_a, c_out), index(chip_b, c_out), index(chip_b, 1 - c_out),
            index(chip_a, 1 - c_out), index(chip_d, c_out), index(chip_d, 1 - c_out),
        ]
    ).astype(jnp.int32)

    def body(order_ref, x_ref, g_ref, *refs):
        del order_ref
        src = refs[:n_arr]
        proj_ref, h_ref = refs[n_arr : n_arr + 2]
        out = refs[n_arr + 2 : 2 * n_arr + 2]
        wbuf, hs, send_sems, recv_sems, local_sems, load_sems = refs[2 * n_arr + 2 :]
        j = pl.program_id(0)
        i = pl.program_id(1)
        me, sibling, chip_a, chip_b, chip_d = places(*_position())
        c = me[2]

        def block(a, chip, core):
            return out[a].at[index(chip, core)]

        def copy(a, k, owner, to, from_input=False):
            return pltpu.make_async_remote_copy(
                src_ref=src[a] if from_input else block(a, *owner),
                dst_ref=block(a, *owner),
                send_sem=send_sems.at[a, k],
                recv_sem=recv_sems.at[a, k],
                device_id=to,
                device_id_type=MESH,
            )

        mine = (me[:2], c)

        def own(a):
            return pltpu.make_async_copy(src[a], block(a, *mine), local_sems.at[a])

        def first_sends(a):
            return [
                copy(a, 0, mine, sibling, from_input=True),
                copy(a, 1, mine, (*chip_a, c), from_input=True),
                copy(a, 2, mine, (*chip_b, c), from_input=True),
            ]

        def onward(a, k):
            owner = {3: chip_a, 4: chip_a, 5: chip_b, 6: chip_d}[k]
            return copy(a, k, (owner, c), (*chip_b, c) if k == 3 else sibling)

        def landed(a, k):
            owner = {0: mine[0], 1: chip_a, 2: chip_b, 3: chip_d, 4: chip_b, 5: chip_a, 6: chip_d}[k]
            core = c if k in (1, 2, 3) else 1 - c
            copy(a, k, (owner, core), me).wait_recv()
            return owner, core

        def load(ref, step):
            return pltpu.make_async_copy(ref, wbuf.at[step % 2], load_sems.at[step % 2])

        @pl.when(jnp.logical_and(j == 0, i == 0))
        def _():
            own(0).start()
            for cp in first_sends(0):
                cp.start()
            load(src[0], 0).start()
            load(src[0], 0).wait()

        steps = {1: (0, ()), 2: (1, (3, 4)), 3: (2, (5,)), 4: (4, ()), 5: (5, ()), 6: (3, (6,)), 7: (6, ())}
        for step, (k, then) in steps.items():

            @pl.when(jnp.logical_and(j == step, i == 0))
            def _():
                load(src[0], step).wait()

            @pl.when(jnp.logical_and(j == step - 1, i == n_t - 1))
            def _():
                owner = landed(0, k)
                for k2 in then:
                    onward(0, k2).start()
                if step == 2:
                    for a in range(1, n_arr):
                        own(a).start()
                        for cp in first_sends(a):
                            cp.start()
                if step == 6:
                    for a in range(1, n_arr):
                        for k1, then1 in ((1, (3, 4)), (2, (5,))):
                            landed(a, k1)
                            for k2 in then1:
                                onward(a, k2).start()
                load(block(0, *owner), step).start()

        @pl.when(j == 0)
        def _():
            xv = x_ref[...]
            h = (xv * lax.rsqrt(_rowmean(xv * xv) + EPS) * g_ref[...]).astype(BF16)
            hs[i] = h
            h_ref[...] = h

        proj_ref[...] = _dot(hs[i], wbuf[j % 2]).astype(BF16)

        @pl.when(jnp.logical_and(j == N_DEV - 1, i == n_t - 1))
        def _():
            for a in range(1, n_arr):
                landed(a, 3)
                onward(a, 6).start()
            for a in range(1, n_arr):
                for k in (0, 4, 5, 6):
                    landed(a, k)
            for a in range(n_arr):
                for cp in first_sends(a) + [onward(a, k) for k in (3, 4, 5, 6)]:
                    cp.wait_send()
                own(a).wait()

    res = pl.pallas_call(
        body,
        name="proj_gather",
        grid_spec=pltpu.PrefetchScalarGridSpec(
            num_scalar_prefetch=1,
            grid=(N_DEV, n_t),
            in_specs=[
                pl.BlockSpec((t, d), lambda j, i, order_ref: (jnp.where(j == 0, i, n_t - 1), 0)),
                pl.BlockSpec((1, d), lambda j, i, order_ref: (0, 0)),
            ]
            + [ANY] * n_arr,
            out_specs=[
                pl.BlockSpec((t, WIN_BLK), lambda j, i, order_ref: (i, order_ref[j])),
                pl.BlockSpec((t, d), lambda j, i, order_ref: (jnp.where(j == 0, i, n_t - 1), 0)),
            ]
            + [ANY] * n_arr,
            scratch_shapes=[
                pltpu.VMEM((2,) + shards[0].shape, BF16),
                pltpu.VMEM((n_t, t, d), BF16),
                pltpu.SemaphoreType.DMA((n_arr, 7)),
                pltpu.SemaphoreType.DMA((n_arr, 7)),
                pltpu.SemaphoreType.DMA((n_arr,)),
                pltpu.SemaphoreType.DMA((2,)),
            ],
        ),
        out_shape=[jax.ShapeDtypeStruct((s, IN_WIDTH), BF16), jax.ShapeDtypeStruct((s, d), BF16)]
        + [jax.ShapeDtypeStruct((N_DEV,) + a.shape, a.dtype) for a in shards],
        compiler_params=_params("arbitrary", "arbitrary"),
    )(order, x_in, norm_pre, *shards)
    return res[0], res[1], res[2:]


def _sigmoid(a):
    return jax.nn.sigmoid(a)


def _dsilu(a, sg):
    return sg * (1.0 + a * (1.0 - sg))


def _rms_fwd(u, gain):
    r = lax.rsqrt(_rowmean(u * u) + EPS)
    n = u * r
    return r, n, n * gain


def _rms_bwd(dy, gain, r, n):
    dn = dy * gain
    return _colsum(dy * n), r * (dn - n * _rowmean(dn * n))


def _mix(proj, x, target, kv, kv_t, wout, wout_t, pool_w, pool_w_t, pool_scale, ln_g, ln_b, sgu_bias, sgu_wm, sgu_wm_t, branch_norm, norm_post):
    s, d = x.shape
    t = min(TILE_MIX, s)
    n_tiles = s // t
    n_chunks = t // SGU_CHUNK
    halo_blocks_per_tile = t // HALO
    inv_d = 1.0 / d
    scale = 1.0 / (XATTN_HEAD_DIM**0.5)

    def body(
        proj_ref, halo_ref, x_ref, tgt_ref, kv_ref, kvt_ref, wout_ref, wout_t_ref, pw_ref, pwt_ref, pscale_ref, lng_ref,
        lnb_ref, bias_ref, wm_ref, wmt_ref, bnorm_ref, gpost_ref,
        y_ref, dout_ref, dxo_ref, dproj_ref, loss_ref, dgpost_ref, dbnorm_ref, dpscale_ref, dlng_ref, dlnb_ref,
        dpw_out, dwm_out, dbias_ref, dkv_out,
        carry_ref, dzsum_ref, dpw_ref, dwm_ref, dkv_ref,
    ):
        i = pl.program_id(0)
        tile = n_tiles - 1 - i

        @pl.when(i == 0)
        def _():
            carry_ref[...] = jnp.zeros_like(carry_ref)
            dzsum_ref[...] = jnp.zeros_like(dzsum_ref)
            for ref in (loss_ref, dgpost_ref, dbnorm_ref, dpscale_ref, dlng_ref, dlnb_ref, dpw_ref, dwm_ref, dkv_ref):
                ref[...] = jnp.zeros_like(ref)

        t_glob = tile * t + lax.broadcasted_iota(jnp.int32, (t, 1), 0)
        inv_cnt = [1.0 / jnp.minimum(t_glob + 1, w).astype(F32) for w in POOL_WINDOWS]

        xa = proj_ref[:, 0:BRANCH].astype(F32)
        ga = proj_ref[:, BRANCH : 2 * BRANCH].astype(F32)
        halo = jnp.where(tile == 0, 0.0, halo_ref[...].astype(F32))
        d_bf, pm_parts = [], []
        for g, w in enumerate(POOL_WINDOWS):
            cols = slice(g * POOL_GROUP_DIM, (g + 1) * POOL_GROUP_DIM)
            acc = jnp.concatenate([halo[:, cols], xa[:, cols]], axis=0)
            k = 1
            while k < w:
                acc = acc + pltpu.roll(acc, k, axis=0)
                k *= 2
            dg = (acc[HALO:, :] * inv_cnt[g] - xa[:, cols]).astype(BF16)
            d_bf.append(dg)
            pm_parts.append(_dot(dg, pw_ref[g]))
        pm = jnp.concatenate(pm_parts, axis=1)
        pscale = pscale_ref[...]
        pa = pm * pscale
        sga = _sigmoid(ga)
        sila = ga * sga
        ua = pa * sila
        g_a = bnorm_ref[:, 0:BRANCH]
        ra, na, ya = _rms_fwd(ua, g_a)

        u = proj_ref[:, 2 * BRANCH : 3 * BRANCH].astype(F32)
        v = proj_ref[:, 3 * BRANCH : 4 * BRANCH].astype(F32)
        gb = proj_ref[:, 4 * BRANCH : 5 * BRANCH].astype(F32)
        lng = lng_ref[...]
        vc = v - _rowmean(v)
        rstd = lax.rsqrt(_rowmean(vc * vc) + EPS)
        vhat = vc * rstd
        vn_bf = (vhat * lng + lnb_ref[...]).astype(BF16)
        z_rows = []
        for c in range(n_chunks):
            rows = slice(c * SGU_CHUNK, (c + 1) * SGU_CHUNK)
            z_rows.append(
                jnp.concatenate(
                    [
                        _dot(wm_ref[hd], vn_bf[rows, hd * SGU_HEAD_DIM : (hd + 1) * SGU_HEAD_DIM])
                        for hd in range(SGU_HEADS)
                    ],
                    axis=1,
                )
                + bias_ref[...]
            )
        z = z_rows[0] if n_chunks == 1 else jnp.concatenate(z_rows, axis=0)
        sb = u * z
        sgb = _sigmoid(gb)
        silb = gb * sgb
        ub = sb * silb
        g_b = bnorm_ref[:, BRANCH : 2 * BRANCH]
        rb, nb, yb = _rms_fwd(ub, g_b)

        q = proj_ref[:, 5 * BRANCH : 6 * BRANCH]
        gc = proj_ref[:, 6 * BRANCH : 7 * BRANCH].astype(F32)
        q_bf, p_bf, o_parts = [], [], []
        for hd in range(XATTN_HEADS):
            cols = slice(hd * XATTN_HEAD_DIM, (hd + 1) * XATTN_HEAD_DIM)
            qh = q[:, cols]
            sc = _dot(qh, kvt_ref[cols, :]) * scale
            e = jnp.exp(sc - jnp.max(sc, axis=-1, keepdims=True))
            p = e / jnp.sum(e, axis=-1, keepdims=True)
            q_bf.append(qh)
            p_bf.append(p.astype(BF16))
            o_parts.append(_dot(p_bf[hd], kv_ref[:, BRANCH + hd * XATTN_HEAD_DIM : BRANCH + (hd + 1) * XATTN_HEAD_DIM]))
        o = jnp.concatenate(o_parts, axis=1)
        sgc = _sigmoid(gc)
        silc = gc * sgc
        uc = o * silc
        g_c = bnorm_ref[:, 2 * BRANCH : 3 * BRANCH]
        rc, nc, yc = _rms_fwd(uc, g_c)

        out = None
        for b, y_branch in enumerate((ya, yb, yc)):
            rows = slice(b * BRANCH, (b + 1) * BRANCH)
            y_bf = y_branch.astype(BF16)
            y_ref[:, rows] = y_bf
            part = _dot(y_bf, wout_ref[rows, :])
            out = part if out is None else out + part
        gpost = gpost_ref[...]
        r_out = lax.rsqrt(_rowmean(out * out) + EPS)
        on = out * r_out
        err = x_ref[...] + on * gpost - tgt_ref[...]
        loss_ref[...] += 0.5 * jnp.sum(_rowmean(err * err), axis=0, keepdims=True)

        dxo = err * inv_d
        dxo_ref[...] = dxo
        dgp, dout = _rms_bwd(dxo, gpost, r_out, on)
        dgpost_ref[...] += dgp
        dout_bf = dout.astype(BF16)
        dout_ref[...] = dout_bf
        dy = [_dot(dout_bf, wout_t_ref[:, b * BRANCH : (b + 1) * BRANCH]) for b in range(3)]

        dg_a, dua = _rms_bwd(dy[0], g_a, ra, na)
        dg_b, dub = _rms_bwd(dy[1], g_b, rb, nb)
        dg_c, duc = _rms_bwd(dy[2], g_c, rc, nc)
        dbnorm_ref[...] += jnp.concatenate([dg_a, dg_b, dg_c], axis=1)

        dpa = dua * sila
        dga = dua * pa * _dsilu(ga, sga)
        dpscale_ref[...] += _colsum(dpa * pm)
        dpm = dpa * pscale
        dxa_parts, carry_parts = [], []
        for g, w in enumerate(POOL_WINDOWS):
            cols = slice(g * POOL_GROUP_DIM, (g + 1) * POOL_GROUP_DIM)
            dpm_g = dpm[:, cols].astype(BF16)
            dd = _dot(dpm_g, pwt_ref[g])
            dpw_ref[g] += _dot(d_bf[g], dpm_g, TN)
            cg = dd * inv_cnt[g]
            carry_parts.append(cg[0:HALO, :])
            acc = jnp.concatenate([cg, carry_ref[:, cols]], axis=0)
            k = 1
            while k < w:
                acc = acc + pltpu.roll(acc, t + HALO - k, axis=0)
                k *= 2
            dxa_parts.append(acc[0:t, :] - dd)
        carry_ref[...] = jnp.concatenate(carry_parts, axis=1)
        dxa = jnp.concatenate(dxa_parts, axis=1)

        dsb = dub * silb
        dgb = dub * sb * _dsilu(gb, sgb)
        du = dsb * z
        dz = dsb * u
        dz_bf = dz.astype(BF16)
        dvn_rows = []
        dz_sum = None
        for c in range(n_chunks):
            rows = slice(c * SGU_CHUNK, (c + 1) * SGU_CHUNK)
            dz_sum = dz[rows, :] if dz_sum is None else dz_sum + dz[rows, :]
            parts = []
            for hd in range(SGU_HEADS):
                cols = slice(hd * SGU_HEAD_DIM, (hd + 1) * SGU_HEAD_DIM)
                parts.append(_dot(wmt_ref[hd], dz_bf[rows, cols]))
                dwm_ref[hd] += _dot(dz_bf[rows, cols], vn_bf[rows, cols], NT)
            dvn_rows.append(jnp.concatenate(parts, axis=1))
        dzsum_ref[...] += dz_sum
        dvn = dvn_rows[0] if n_chunks == 1 else jnp.concatenate(dvn_rows, axis=0)
        dlng_ref[...] += _colsum(dvn * vhat)
        dlnb_ref[...] += _colsum(dvn)
        dvh = dvn * lng
        dv = rstd * (dvh - _rowmean(dvh) - vhat * _rowmean(dvh * vhat))

        do = duc * silc
        dgc = duc * o * _dsilu(gc, sgc)
        dq_parts = []
        for hd in range(XATTN_HEADS):
            cols = slice(hd * XATTN_HEAD_DIM, (hd + 1) * XATTN_HEAD_DIM)
            vcols = slice(BRANCH + hd * XATTN_HEAD_DIM, BRANCH + (hd + 1) * XATTN_HEAD_DIM)
            do_h = do[:, cols].astype(BF16)
            p = p_bf[hd].astype(F32)
            dp = _dot(do_h, kvt_ref[vcols, :])
            dkv_ref[:, vcols] += _dot(p_bf[hd], do_h, TN)
            ds_bf = (p * (dp - jnp.sum(dp * p, axis=-1, keepdims=True)) * scale).astype(BF16)
            dq_parts.append(_dot(ds_bf, kv_ref[:, cols]))
            dkv_ref[:, cols] += _dot(ds_bf, q_bf[hd], TN)
        dq = jnp.concatenate(dq_parts, axis=1)

        dproj_ref[...] = jnp.concatenate([dxa, dga, du, dv, dgb, dq, dgc], axis=1).astype(BF16)

        @pl.when(i == n_tiles - 1)
        def _():
            keep = lax.broadcasted_iota(jnp.int32, (SGU_CHUNK, SGU_CHUNK), 0) >= lax.broadcasted_iota(
                jnp.int32, (SGU_CHUNK, SGU_CHUNK), 1
            )
            for hd in range(SGU_HEADS):
                dwm_ref[hd] = jnp.where(keep, dwm_ref[hd], 0.0)
                per_pos = dzsum_ref[:, hd * SGU_HEAD_DIM : (hd + 1) * SGU_HEAD_DIM]
                dbias_ref[hd : hd + 1, :] = _colsum(per_pos.T)
            for acc, res in ((dpw_ref, dpw_out), (dwm_ref, dwm_out), (dkv_ref, dkv_out)):
                pltpu.sync_copy(acc, res)

    row_tile = lambda width: pl.BlockSpec((t, width), lambda i: (n_tiles - 1 - i, 0))
    halo_spec = pl.BlockSpec(
        (HALO, BRANCH), lambda i: (jnp.maximum((n_tiles - 1 - i) * halo_blocks_per_tile - 1, 0), 0)
    )
    acc_shapes = [
        (1, 128),
        (1, d),
        (1, MIX_WIDTH),
        (1, BRANCH),
        (1, BRANCH),
        (1, BRANCH),
        pool_w.shape,
        sgu_wm.shape,
        (SGU_HEADS, SGU_CHUNK),
        kv.shape,
    ]
    return pl.pallas_call(
        body,
        name="mix",
        grid=(n_tiles,),
        in_specs=[
            row_tile(IN_WIDTH), halo_spec, row_tile(d), row_tile(d), _resident(kv.shape), _resident(kv_t.shape),
            _resident(wout.shape), _resident(wout_t.shape), _resident(pool_w.shape), _resident(pool_w_t.shape),
            _full((1, BRANCH)), _full((1, BRANCH)), _full((1, BRANCH)), _resident((SGU_CHUNK, BRANCH)),
            _resident(sgu_wm.shape), _resident(sgu_wm_t.shape), _full((1, MIX_WIDTH)), _full((1, d)),
        ],
        out_specs=[row_tile(MIX_WIDTH), row_tile(d), row_tile(d), row_tile(IN_WIDTH)]
        + [ANY if len(a) == 3 or a == kv.shape else _full(a) for a in acc_shapes],
        out_shape=[
            jax.ShapeDtypeStruct((s, MIX_WIDTH), BF16),
            jax.ShapeDtypeStruct((s, d), BF16),
            jax.ShapeDtypeStruct((s, d), F32),
            jax.ShapeDtypeStruct((s, IN_WIDTH), BF16),
        ]
        + [jax.ShapeDtypeStruct(a, F32) for a in acc_shapes],
        scratch_shapes=[
            pltpu.VMEM((HALO, BRANCH), F32), pltpu.VMEM((SGU_CHUNK, BRANCH), F32), pltpu.VMEM(pool_w.shape, F32),
            pltpu.VMEM(sgu_wm.shape, F32), pltpu.VMEM(kv.shape, F32),
        ],
        compiler_params=_params("arbitrary", vmem_limit_bytes=VMEM_LIMIT_MIX_BYTES),
    )(
        proj, proj, x, target, kv, kv_t, wout, wout_t, pool_w, pool_w_t, pool_scale, ln_g, ln_b, sgu_bias, sgu_wm,
        sgu_wm_t, branch_norm, norm_post,
    )


def _position():
    return lax.axis_index("x"), lax.axis_index("y"), lax.axis_index("c")


N_CHIPS = 4


def _landing_shape(kind, a):
    return (N_CHIPS,) + a.shape[2:] if kind == "pair" else a.shape


def _carry_specs(groups):
    arrays = [(kind, a) for kind, arrs in groups for a in arrs]
    scratch = []
    for _, arrs in groups:
        n = len(arrs)
        scratch += [pltpu.SemaphoreType.DMA((n, N_DEV)), pltpu.SemaphoreType.DMA((n, N_DEV)), pltpu.SemaphoreType.DMA((n,))]
    return dict(
        n=len(arrays),
        operands=[a for _, a in arrays],
        in_specs=[ANY] * len(arrays),
        out_specs=[ANY] * len(arrays),
        out_shape=[jax.ShapeDtypeStruct(_landing_shape(kind, a), a.dtype) for kind, a in arrays],
        scratch_shapes=scratch,
    )


def _carry(groups, src, out, sems):
    x, y, c = _position()
    chip = 2 * x + y
    me = 2 * chip + c

    def remote(s, d, send_sems, recv_sems, a, m, to):
        return pltpu.make_async_remote_copy(
            src_ref=s, dst_ref=d, send_sem=send_sems.at[a, m], recv_sem=recv_sems.at[a, m], device_id=to,
            device_id_type=MESH,
        )

    def copies():
        far, near = [], []
        at = 0
        for g, (kind, arrs) in enumerate(groups):
            send_sems, recv_sems, local_sems = sems[3 * g : 3 * g + 3]
            for a in range(len(arrs)):
                s, d = src[at + a], out[at + a]
                if kind == "pair":
                    far.append(remote(s.at[:, 1 - c], d, send_sems, recv_sems, a, 1, (x, y, 1 - c)))
                elif kind == "chips":
                    for m in range(1, N_CHIPS):
                        px, py = x ^ (m >> 1), y ^ (m & 1)
                        far.append(remote(s.at[2 * px + py], d.at[chip], send_sems, recv_sems, a, m, (px, py, c)))
                    near.append(pltpu.make_async_copy(s.at[chip], d.at[chip], local_sems.at[a]))
                else:
                    for m in range(1, N_DEV):
                        px, py, pc = x ^ ((m >> 2) & 1), y ^ ((m >> 1) & 1), c ^ (m & 1)
                        far.append(
                            remote(s.at[4 * px + 2 * py + pc], d.at[me], send_sems, recv_sems, a, m, (px, py, pc))
                        )
                    near.append(pltpu.make_async_copy(s.at[me], d.at[me], local_sems.at[a]))
            at += len(arrs)
        return far, near

    def start():
        far, near = copies()
        for cp in near + far:
            cp.start()

    def finish():
        far, near = copies()
        for cp in far:
            cp.wait_recv()
        for cp in far:
            cp.wait_send()
        for cp in near:
            cp.wait()

    return start, finish


def _exchange(groups, name):
    carried = _carry_specs(groups)
    n_c = carried["n"]

    def body(*refs):
        start, finish = _carry(groups, refs[:n_c], refs[n_c : 2 * n_c], refs[2 * n_c :])
        start()
        finish()

    return pl.pallas_call(
        body,
        name=name,
        in_specs=carried["in_specs"],
        out_specs=carried["out_specs"],
        out_shape=carried["out_shape"],
        scratch_shapes=carried["scratch_shapes"],
    )(*carried["operands"])


def _pair_sum(mine, theirs, name, groups=()):
    n = len(mine)
    carried = _carry_specs(groups)
    n_c = carried["n"]
    core = lax.axis_index("c").astype(jnp.int32).reshape(1)

    def body(core_ref, *refs):
        del core_ref
        own = refs[:n]
        sib = refs[n : 2 * n]
        src = refs[2 * n : 2 * n + n_c]
        out = refs[2 * n + n_c : 3 * n + n_c]
        landed = refs[3 * n + n_c : 3 * n + 2 * n_c]
        start, finish = _carry(groups, src, landed, refs[3 * n + 2 * n_c :])
        b = pl.program_id(0)

        @pl.when(b == 0)
        def _():
            start()

        for a in range(n):
            out[a][...] = (own[a][...].astype(F32) + sib[a][...].astype(F32)).astype(out[a].dtype)

        @pl.when(b == N_CHIPS - 1)
        def _():
            finish()

    block = lambda a: pl.BlockSpec((None,) + a.shape[1:], lambda b, core_ref: (b, 0, 0))
    res = pl.pallas_call(
        body,
        name=name,
        grid_spec=pltpu.PrefetchScalarGridSpec(
            num_scalar_prefetch=1,
            grid=(N_CHIPS,),
            in_specs=[pl.BlockSpec((None, None) + a.shape[2:], lambda b, core_ref: (b, core_ref[0], 0, 0)) for a in mine]
            + [block(a) for a in theirs]
            + carried["in_specs"],
            out_specs=[block(a) for a in theirs] + carried["out_specs"],
            scratch_shapes=carried["scratch_shapes"],
        ),
        out_shape=[jax.ShapeDtypeStruct(a.shape, a.dtype) for a in theirs] + carried["out_shape"],
        compiler_params=_params("arbitrary"),
    )(core, *mine, *theirs, *carried["operands"])
    return res[:n], res[n:]


def _weight_grad(a, b, n_blk, blocked, name, groups):
    s = a.shape[0]
    t = min(TILE_GRAD, s)
    n_t = s // t
    n_pairs = n_blk // 2
    if blocked == "cols":
        k, c = a.shape[1], b.shape[1] // n_blk
        a_spec = pl.BlockSpec((t, k), lambda j, i: (i, 0))
        b_spec = pl.BlockSpec((t, 2 * c), lambda j, i: (i, j))
        acc_shape = (k, 2 * c)
    else:
        k, c = a.shape[1] // n_blk, b.shape[1]
        a_spec = pl.BlockSpec((t, 2 * k), lambda j, i: (i, j))
        b_spec = pl.BlockSpec((t, c), lambda j, i: (i, 0))
        acc_shape = (2 * k, c)
    carried = _carry_specs(groups)
    n_p = carried["n"]

    def body(a_ref, b_ref, *refs):
        src = refs[:n_p]
        o_ref = refs[n_p]
        landed = refs[n_p + 1 : 2 * n_p + 1]
        acc_ref = refs[2 * n_p + 1]
        start, finish = _carry(groups, src, landed, refs[2 * n_p + 2 :])
        j = pl.program_id(0)
        i = pl.program_id(1)

        @pl.when(jnp.logical_and(j == 0, i == 0))
        def _():
            start()

        @pl.when(i == 0)
        def _():
            acc_ref[...] = jnp.zeros_like(acc_ref)

        acc_ref[...] += _dot(a_ref[...], b_ref[...], TN)

        @pl.when(i == n_t - 1)
        def _():
            for half in range(2):
                if blocked == "cols":
                    o_ref[half] = acc_ref[:, half * c : (half + 1) * c].astype(BF16)
                else:
                    o_ref[half] = acc_ref[half * k : (half + 1) * k, :].astype(BF16)

        @pl.when(jnp.logical_and(j == n_pairs - 1, i == n_t - 1))
        def _():
            finish()

    res = pl.pallas_call(
        body,
        name=name,
        grid=(n_pairs, n_t),
        in_specs=[a_spec, b_spec] + carried["in_specs"],
        out_specs=[pl.BlockSpec((2, k, c), lambda j, i: (j, 0, 0))] + carried["out_specs"],
        out_shape=[jax.ShapeDtypeStruct((n_blk, k, c), BF16)] + carried["out_shape"],
        scratch_shapes=[pltpu.VMEM(acc_shape, F32)] + carried["scratch_shapes"],
        compiler_params=_params("arbitrary", "arbitrary"),
    )(a, b, *carried["operands"])
    return res[0], res[1:]


def _input_grad(dproj, win_t, x, dxo, norm_pre, groups):
    s, d = x.shape
    t = min(TILE_GRAD, s)
    n_t = s // t
    kb = BRANCH
    n_k = win_t.shape[0] // kb
    carried = _carry_specs(groups)
    n_p = carried["n"]

    def body(dp_ref, w_ref, x_ref, dxo_ref, g_ref, *refs):
        src = refs[:n_p]
        gx_ref, dg_ref = refs[n_p : n_p + 2]
        landed = refs[n_p + 2 : 2 * n_p + 2]
        acc_ref = refs[2 * n_p + 2]
        start, finish = _carry(groups, src, landed, refs[2 * n_p + 3 :])
        i = pl.program_id(0)
        j = pl.program_id(1)

        @pl.when(jnp.logical_and(i == 0, j == 0))
        def _():
            start()
            dg_ref[...] = jnp.zeros_like(dg_ref)

        @pl.when(j == 0)
        def _():
            acc_ref[...] = jnp.zeros_like(acc_ref)

        acc_ref[...] += _dot(dp_ref[...], w_ref[...])

        @pl.when(j == n_k - 1)
        def _():
            xv = x_ref[...]
            gain = g_ref[...]
            r = lax.rsqrt(_rowmean(xv * xv) + EPS)
            dgain, dx = _rms_bwd(acc_ref[...], gain, r, xv * r)
            dg_ref[...] += dgain
            gx_ref[...] = dxo_ref[...] + dx

        @pl.when(jnp.logical_and(i == n_t - 1, j == n_k - 1))
        def _():
            finish()

    res = pl.pallas_call(
        body,
        name="input_grad",
        grid=(n_t, n_k),
        in_specs=[
            pl.BlockSpec((t, kb), lambda i, j: (i, j)),
            pl.BlockSpec((kb, d), lambda i, j: (j, 0)),
            pl.BlockSpec((t, d), lambda i, j: (i, 0)),
            pl.BlockSpec((t, d), lambda i, j: (i, 0)),
            _full((1, d)),
        ]
        + carried["in_specs"],
        out_specs=[pl.BlockSpec((t, d), lambda i, j: (i, 0)), _full((1, d))] + carried["out_specs"],
        out_shape=[jax.ShapeDtypeStruct((s, d), F32), jax.ShapeDtypeStruct((1, d), F32)] + carried["out_shape"],
        scratch_shapes=[pltpu.VMEM((t, d), F32)] + carried["scratch_shapes"],
        compiler_params=_params("arbitrary", "arbitrary"),
    )(dproj, win_t, x, dxo, norm_pre, *carried["operands"])
    return res[0], res[1], res[2:]


def _kv_backward(dkv, memn, wkv, mem, mem_norm):
    m, d = mem.shape
    n = wkv.shape[1]

    def body(dkv_ref, memn_ref, w_ref, mem_ref, g_ref, gw_ref, dg_ref):
        dkv_bf = dkv_ref[...].astype(BF16)
        gw_ref[...] = _dot(memn_ref[...], dkv_bf, TN).astype(BF16).reshape(N_DEV, WKV_BLK, n)
        dmemn = _dot(dkv_bf, w_ref[...], NT)
        mv = mem_ref[...]
        r = lax.rsqrt(_rowmean(mv * mv) + EPS)
        dg_ref[...] = _colsum(dmemn * (mv * r))

    return pl.pallas_call(
        body,
        name="kv_backward",
        grid=(1,),
        in_specs=[_full((m, n)), _full((m, d)), _full(wkv.shape), _full((m, d)), _full((1, d))],
        out_specs=[_full((N_DEV, WKV_BLK, n)), _full((1, d))],
        out_shape=[jax.ShapeDtypeStruct((N_DEV, WKV_BLK, n), BF16), jax.ShapeDtypeStruct((1, d), F32)],
        compiler_params=_params("arbitrary"),
    )(dkv, memn, wkv, mem, mem_norm)


def _adamw_math(w, g, m, v):
    m = ADAM_B1 * m + (1.0 - ADAM_B1) * g
    v = ADAM_B2 * v + (1.0 - ADAM_B2) * (g * g)
    m_hat = m / (1.0 - ADAM_B1**ADAM_STEP)
    v_hat = v / (1.0 - ADAM_B2**ADAM_STEP)
    delta = -ADAM_LR * (m_hat / (jnp.sqrt(v_hat) + ADAM_EPS) + ADAM_WD * w)
    return delta, m, v


def _adamw(parts, w, m, v, name):
    r, c = w.shape
    slots = parts.shape[0]
    t = r
    while t * c * 4 > TILE_ADAM_BYTES and t % 16 == 0:
        t //= 2

    def body(p_ref, w_ref, m_ref, v_ref, g_ref, d_ref, nm_ref, nv_ref):
        g = p_ref[0].astype(F32)
        for k in range(1, slots):
            g = g + p_ref[k].astype(F32)
        delta, nm, nv = _adamw_math(w_ref[...], g, m_ref[...], v_ref[...])
        g_ref[...] = g
        d_ref[...] = delta
        nm_ref[...] = nm
        nv_ref[...] = nv

    tile = pl.BlockSpec((t, c), lambda i: (i, 0))
    return pl.pallas_call(
        body,
        name=name,
        grid=(r // t,),
        in_specs=[pl.BlockSpec((slots, t, c), lambda i: (0, i, 0)), tile, tile, tile],
        out_specs=[tile] * 4,
        out_shape=[jax.ShapeDtypeStruct((r, c), F32)] * 4,
        compiler_params=_params("parallel"),
    )(parts, w, m, v)


def _adamw_packed(parts, triples, name):
    slots = parts.shape[0]
    sizes = [w.shape[0] for w, _, _ in triples]

    def body(p_ref, *refs):
        ins = refs[: 3 * len(triples)]
        outs = refs[3 * len(triples) :]
        at = 0
        for n, rows in enumerate(sizes):
            g = p_ref[0, at : at + rows, :]
            for k in range(1, slots):
                g = g + p_ref[k, at : at + rows, :]
            w_ref, m_ref, v_ref = ins[3 * n : 3 * n + 3]
            delta, nm, nv = _adamw_math(w_ref[...], g, m_ref[...], v_ref[...])
            for ref, val in zip(outs[4 * n : 4 * n + 4], (g, delta, nm, nv)):
                ref[...] = val
            at += rows

    flat = [a for t in triples for a in t]
    res = pl.pallas_call(
        body,
        name=name,
        out_shape=[jax.ShapeDtypeStruct(w.shape, F32) for w, _, _ in triples for _ in range(4)],
        compiler_params=pltpu.CompilerParams(vmem_limit_bytes=VMEM_LIMIT_BYTES),
    )(parts, *flat)
    return [res[4 * n : 4 * n + 4] for n in range(len(triples))]


SMALL = ("norm_pre", "pool_scale", "sgu_ln_g", "sgu_ln_b", "sgu_w", "sgu_b", "mem_norm", "branch_norm", "norm_post")


def _local_view(name, w):
    if name == "sgu_w":
        return w.reshape(SGU_HEADS, SGU_CHUNK, SGU_CHUNK)
    if name == "sgu_b":
        return w.reshape(SGU_HEADS, SGU_CHUNK)
    return w.reshape(1, -1)


def _forward_backward(x, mem, target, shards, small):
    causal = jnp.tril(jnp.ones((SGU_CHUNK, SGU_CHUNK), dtype=bool))
    sgu_wm = jnp.where(causal[None], small["sgu_w"], 0.0).astype(BF16)
    sgu_bias = jnp.repeat(jnp.transpose(small["sgu_b"]), SGU_HEAD_DIM, axis=1)

    proj, h, (win, wkv, pool_all, wout) = _proj_gather(x, small["norm_pre"], shards)
    wout = wout.reshape(MIX_WIDTH, D_MODEL)
    wkv = wkv.reshape(D_MODEL, 2 * BRANCH)
    pool_full = (
        pool_all.reshape(N_DEV, len(POOL_WINDOWS), POOL_BLK, POOL_GROUP_DIM)
        .transpose(1, 0, 2, 3)
        .reshape(len(POOL_WINDOWS), POOL_GROUP_DIM, POOL_GROUP_DIM)
    )
    memn, kv = _kv_forward(mem, small["mem_norm"], wkv)
    (y, dout, dxo, dproj, loss, d_norm_post, d_branch_norm, d_pool_scale, d_ln_g, d_ln_b, d_pool_w, d_sgu_w, d_sgu_b,
     dkv) = _mix(
        proj, x, target, kv, kv.T, wout, wout.T, pool_full, jnp.swapaxes(pool_full, 1, 2), small["pool_scale"],
        small["sgu_ln_g"], small["sgu_ln_b"], sgu_bias, sgu_wm, jnp.swapaxes(sgu_wm, 1, 2), small["branch_norm"],
        small["norm_post"],
    )
    g_wkv, d_mem_norm = _kv_backward(dkv, memn, wkv, mem, small["mem_norm"])
    g_pool = (
        d_pool_w.reshape(len(POOL_WINDOWS), N_DEV, POOL_BLK, POOL_GROUP_DIM)
        .transpose(1, 0, 2, 3)
        .reshape(N_DEV, len(POOL_WINDOWS) * POOL_BLK, POOL_GROUP_DIM)
        .astype(BF16)
    )
    small_grads = dict(
        pool_scale=d_pool_scale, sgu_ln_g=d_ln_g, sgu_ln_b=d_ln_b, sgu_w=d_sgu_w, sgu_b=d_sgu_b,
        mem_norm=d_mem_norm, branch_norm=d_branch_norm, norm_post=d_norm_post,
    )
    packed = jnp.concatenate([small_grads[n].reshape(-1, 128) for n in SMALL if n != "norm_pre"], axis=0)
    packed = jnp.broadcast_to(packed[None, None], (N_CHIPS, 2) + packed.shape)

    by_chip = lambda g: g.reshape((N_CHIPS, 2) + g.shape[1:])
    small_mine = [by_chip(g_wkv), by_chip(g_pool), packed]
    g_win, small_theirs = _weight_grad(h, dproj, N_DEV, "cols", "grad_w_in", [("pair", small_mine)])
    small_sums, _ = _pair_sum(small_mine, small_theirs, "pair_sum_small")
    g_wout, (l_wkv, l_pool, l_packed, win_theirs) = _weight_grad(
        y, dout, N_DEV, "rows", "grad_w_out", [("chips", list(small_sums)), ("pair", [by_chip(g_win)])]
    )
    (win_sums,), (wout_theirs,) = _pair_sum(
        [by_chip(g_win)], [win_theirs], "pair_sum_w_in", [("pair", [by_chip(g_wout)])]
    )
    (wout_sums,), _ = _pair_sum([by_chip(g_wout)], [wout_theirs], "pair_sum_w_out")
    grad_x, d_norm_pre, (l_win, l_wout) = _input_grad(
        dproj, jnp.swapaxes(win, 1, 2).reshape(IN_WIDTH, D_MODEL), x, dxo, small["norm_pre"],
        [("chips", [win_sums, wout_sums])],
    )
    return loss, grad_x, dict(w_in=l_win, w_out=l_wout, w_kv=l_wkv, pool_w=l_pool), l_packed, d_norm_pre


def kernel(x, mem, norm_pre, w_in, pool_w, pool_scale, sgu_ln_g, sgu_ln_b, sgu_w, sgu_b, mem_norm, w_kv, branch_norm, w_out, norm_post, loss_target, m_norm_pre, m_w_in, m_pool_w, m_pool_scale, m_sgu_ln_g, m_sgu_ln_b, m_sgu_w, m_sgu_b, m_mem_norm, m_w_kv, m_branch_norm, m_w_out, m_norm_post, v_norm_pre, v_w_in, v_pool_w, v_pool_scale, v_sgu_ln_g, v_sgu_ln_b, v_sgu_w, v_sgu_b, v_mem_norm, v_w_kv, v_branch_norm, v_w_out, v_norm_post):
    weights = dict(norm_pre=norm_pre, w_in=w_in, pool_w=pool_w, pool_scale=pool_scale, sgu_ln_g=sgu_ln_g, sgu_ln_b=sgu_ln_b, sgu_w=sgu_w, sgu_b=sgu_b, mem_norm=mem_norm, w_kv=w_kv, branch_norm=branch_norm, w_out=w_out, norm_post=norm_post)
    first = dict(norm_pre=m_norm_pre, w_in=m_w_in, pool_w=m_pool_w, pool_scale=m_pool_scale, sgu_ln_g=m_sgu_ln_g, sgu_ln_b=m_sgu_ln_b, sgu_w=m_sgu_w, sgu_b=m_sgu_b, mem_norm=m_mem_norm, w_kv=m_w_kv, branch_norm=m_branch_norm, w_out=m_w_out, norm_post=m_norm_post)
    second = dict(norm_pre=v_norm_pre, w_in=v_w_in, pool_w=v_pool_w, pool_scale=v_pool_scale, sgu_ln_g=v_sgu_ln_g, sgu_ln_b=v_sgu_ln_b, sgu_w=v_sgu_w, sgu_b=v_sgu_b, mem_norm=v_mem_norm, w_kv=v_w_kv, branch_norm=v_branch_norm, w_out=v_w_out, norm_post=v_norm_post)
    order = ("norm_pre", "w_in", "pool_w", "pool_scale", "sgu_ln_g", "sgu_ln_b", "sgu_w", "sgu_b", "mem_norm", "w_kv", "branch_norm", "w_out", "norm_post")

    owned_shape = dict(
        w_in=(D_MODEL, WIN_BLK), w_out=(WOUT_BLK, D_MODEL), w_kv=(WKV_BLK, 2 * BRANCH),
        pool_w=(len(POOL_WINDOWS) * POOL_BLK, POOL_GROUP_DIM),
    )
    owned = {n: weights[n].reshape(owned_shape[n]) for n in owned_shape}
    small = {n: _local_view(n, weights[n]) for n in SMALL}
    loss, grad_x, landed, landed_packed, d_norm_pre = _forward_backward(
        x[0], mem[0], loss_target[0], [owned[n].astype(BF16) for n in ("w_in", "w_kv", "pool_w", "w_out")], small
    )
    d_norm_pre = d_norm_pre.reshape(-1, 128)
    (landed_norm_pre,) = _exchange(
        [("all", [jnp.broadcast_to(d_norm_pre[None], (N_DEV,) + d_norm_pre.shape)])], "exchange_norm_pre"
    )

    grads, deltas, new_m, new_v = {}, {}, {}, {}
    for n in owned_shape:
        shape = weights[n].shape
        res = _adamw(
            landed[n], owned[n], first[n].reshape(owned_shape[n]), second[n].reshape(owned_shape[n]), "adamw_" + n
        )
        grads[n], deltas[n], new_m[n], new_v[n] = (a.reshape(shape) for a in res)
    rows_of = lambda tree, n: tree[n].reshape(-1, 128)
    for names, parts, name in (
        ([n for n in SMALL if n != "norm_pre"], landed_packed, "adamw_replicated"),
        (["norm_pre"], landed_norm_pre, "adamw_norm_pre"),
    ):
        res = _adamw_packed(parts, [(rows_of(weights, n), rows_of(first, n), rows_of(second, n)) for n in names], name)
        for n, four in zip(names, res):
            for tree, a in zip((grads, deltas, new_m, new_v), four):
                tree[n] = a.reshape(weights[n].shape)

    total = lax.psum(loss[0, 0], ("x", "y", "c"))
    return (
        total,
        grad_x[None],
        *[grads[n] for n in order],
        *[deltas[n] for n in order],
        *[new_m[n] for n in order],
        *[new_v[n] for n in order],
    )
```

```python
import functools

import jax
import jax.numpy as jnp
from jax import lax
from jax.experimental import pallas as pl
from jax.experimental.pallas import tpu as pltpu

F32 = jnp.float32
BF16 = jnp.bfloat16
EPS = 1e-6

D_MODEL = 2048
POOL_WINDOWS = (2, 4, 8, 16)
POOL_GROUP_DIM = 256
BRANCH = 1024
SGU_CHUNK = 128
SGU_HEADS = 8
SGU_HEAD_DIM = 128
XATTN_HEADS = 4
XATTN_HEAD_DIM = 256
MIX_WIDTH = 3 * BRANCH
IN_WIDTH = 7 * BRANCH
N_DEV = 8
WIN_BLK = IN_WIDTH // N_DEV
WOUT_BLK = MIX_WIDTH // N_DEV
WKV_BLK = D_MODEL // N_DEV
POOL_BLK = POOL_GROUP_DIM // N_DEV
HALO = 16

ADAM_LR = 0.001
ADAM_B1 = 0.9
ADAM_B2 = 0.999
ADAM_EPS = 1e-08
ADAM_WD = 0.01
ADAM_STEP = 10

VMEM_LIMIT_BYTES = 56 * 1024 * 1024
VMEM_LIMIT_MIX_BYTES = 63 * 1024 * 1024

TILE_PROJ = 512
TILE_MIX = 128
TILE_GRAD = 512
TILE_ADAM_BYTES = 1 << 20

ANY = pl.BlockSpec(memory_space=pl.ANY)
NN = (((1,), (0,)), ((), ()))
NT = (((1,), (1,)), ((), ()))
TN = (((0,), (0,)), ((), ()))
MESH = pl.DeviceIdType.MESH


def _dot(a, b, dims=NN):
    return lax.dot_general(a, b, dims, preferred_element_type=F32)


def _params(*semantics, vmem_limit_bytes=VMEM_LIMIT_BYTES):
    return pltpu.CompilerParams(dimension_semantics=semantics, vmem_limit_bytes=vmem_limit_bytes)


def _rowmean(a):
    return jnp.mean(a, axis=-1, keepdims=True)


def _colsum(a):
    return jnp.sum(a, axis=0, keepdims=True)


def _full(shape):
    zeros = (0,) * len(shape)
    return pl.BlockSpec(shape, lambda *_: zeros)


def _resident(shape):
    zeros = (0,) * len(shape)
    return pl.BlockSpec(shape, lambda *_: zeros, pipeline_mode=pl.Buffered(1))


def _kv_forward(mem, mem_norm, wkv):
    m, d = mem.shape

    def body(mem_ref, g_ref, w_ref, memn_ref, kv_ref):
        mv = mem_ref[...]
        r = lax.rsqrt(_rowmean(mv * mv) + EPS)
        memn = (mv * r * g_ref[...]).astype(BF16)
        memn_ref[...] = memn
        kv_ref[...] = _dot(memn, w_ref[...]).astype(BF16)

    return pl.pallas_call(
        body,
        name="kv_forward",
        grid=(1,),
        in_specs=[_full((m, d)), _full((1, d)), _full(wkv.shape)],
        out_specs=[_full((m, d)), _full((m, wkv.shape[1]))],
        out_shape=[jax.ShapeDtypeStruct((m, d), BF16), jax.ShapeDtypeStruct((m, wkv.shape[1]), BF16)],
        compiler_params=_params("arbitrary"),
    )(mem, mem_norm, wkv)


def _proj_gather(x_in, norm_pre, shards):
    s, d = x_in.shape
    t = min(TILE_PROJ, s)
    n_t = s // t
    n_arr = len(shards)

    def places(x, y, c):
        return (x, y, c), (x, y, 1 - c), (x ^ c, y ^ (1 - c)), (x ^ (1 - c), y ^ c), (1 - x, 1 - y)

    def index(chip, core):
        return 4 * chip[0] + 2 * chip[1] + core

    _, _, chip_a, chip_b, chip_d = places(*_position())
    c_out = lax.axis_index("c")
    me_out = index((lax.axis_index("x"), lax.axis_index("y")), c_out)
    order = jnp.stack(
        [
            me_out, me_out ^ 1, index(chip_a, c_out), index(chip_b, c_out), index(chip_b, 1 - c_out),
            index(chip_a, 1 - c_out), index(chip_d, c_out), index(chip_d, 1 - c_out),
        ]
    ).astype(jnp.int32)

    def body(order_ref, x_ref, g_ref, *refs):
        del order_ref
        src = refs[:n_arr]
        proj_ref, h_ref = refs[n_arr : n_arr + 2]
        out = refs[n_arr + 2 : 2 * n_arr + 2]
        wbuf, hs, send_sems, recv_sems, local_sems, load_sems = refs[2 * n_arr + 2 :]
        j = pl.program_id(0)
        i = pl.program_id(1)
        me, sibling, chip_a, chip_b, chip_d = places(*_position())
        c = me[2]

        def block(a, chip, core):
            return out[a].at[index(chip, core)]

        def copy(a, k, owner, to, from_input=False):
            return pltpu.make_async_remote_copy(
                src_ref=src[a] if from_input else block(a, *owner),
                dst_ref=block(a, *owner),
                send_sem=send_sems.at[a, k],
                recv_sem=recv_sems.at[a, k],
                device_id=to,
                device_id_type=MESH,
            )

        mine = (me[:2], c)

        def own(a):
            return pltpu.make_async_copy(src[a], block(a, *mine), local_sems.at[a])

        def first_sends(a):
            return [
                copy(a, 0, mine, sibling, from_input=True),
                copy(a, 1, mine, (*chip_a, c), from_input=True),
                copy(a, 2, mine, (*chip_b, c), from_input=True),
            ]

        def onward(a, k):
            owner = {3: chip_a, 4: chip_a, 5: chip_b, 6: chip_d}[k]
            return copy(a, k, (owner, c), (*chip_b, c) if k == 3 else sibling)

        def landed(a, k):
            owner = {0: mine[0], 1: chip_a, 2: chip_b, 3: chip_d, 4: chip_b, 5: chip_a, 6: chip_d}[k]
            core = c if k in (1, 2, 3) else 1 - c
            copy(a, k, (owner, core), me).wait_recv()
            return owner, core

        def load(ref, step):
            return pltpu.make_async_copy(ref, wbuf.at[step % 2], load_sems.at[step % 2])

        @pl.when(jnp.logical_and(j == 0, i == 0))
        def _():
            own(0).start()
            for cp in first_sends(0):
                cp.start()
            load(src[0], 0).start()
            load(src[0], 0).wait()

        steps = {1: (0, ()), 2: (1, (3, 4)), 3: (2, (5,)), 4: (4, ()), 5: (5, ()), 6: (3, (6,)), 7: (6, ())}
        for step, (k, then) in steps.items():

            @pl.when(jnp.logical_and(j == step, i == 0))
            def _():
                load(src[0], step).wait()

            @pl.when(jnp.logical_and(j == step - 1, i == n_t - 1))
            def _():
                owner = landed(0, k)
                for k2 in then:
                    onward(0, k2).start()
                if step == 2:
                    for a in range(1, n_arr):
                        own(a).start()
                        for cp in first_sends(a):
                            cp.start()
                if step == 6:
                    for a in range(1, n_arr):
                        for k1, then1 in ((1, (3, 4)), (2, (5,))):
                            landed(a, k1)
                            for k2 in then1:
                                onward(a, k2).start()
                load(block(0, *owner), step).start()

        @pl.when(j == 0)
        def _():
            xv = x_ref[...]
            h = (xv * lax.rsqrt(_rowmean(xv * xv) + EPS) * g_ref[...]).astype(BF16)
            hs[i] = h
            h_ref[...] = h

        proj_ref[...] = _dot(hs[i], wbuf[j % 2]).astype(BF16)

        @pl.when(jnp.logical_and(j == N_DEV - 1, i == n_t - 1))
        def _():
            for a in range(1, n_arr):
                landed(a, 3)
                onward(a, 6).start()
            for a in range(1, n_arr):
                for k in (0, 4, 5, 6):
                    landed(a, k)
            for a in range(n_arr):
                for cp in first_sends(a) + [onward(a, k) for k in (3, 4, 5, 6)]:
                    cp.wait_send()
                own(a).wait()

    res = pl.pallas_call(
        body,
        name="proj_gather",
        grid_spec=pltpu.PrefetchScalarGridSpec(
            num_scalar_prefetch=1,
            grid=(N_DEV, n_t),
            in_specs=[
                pl.BlockSpec((t, d), lambda j, i, order_ref: (jnp.where(j == 0, i, n_t - 1), 0)),
                pl.BlockSpec((1, d), lambda j, i, order_ref: (0, 0)),
            ]
            + [ANY] * n_arr,
            out_specs=[
                pl.BlockSpec((t, WIN_BLK), lambda j, i, order_ref: (i, order_ref[j])),
                pl.BlockSpec((t, d), lambda j, i, order_ref: (jnp.where(j == 0, i, n_t - 1), 0)),
            ]
            + [ANY] * n_arr,
            scratch_shapes=[
                pltpu.VMEM((2,) + shards[0].shape, BF16),
                pltpu.VMEM((n_t, t, d), BF16),
                pltpu.SemaphoreType.DMA((n_arr, 7)),
                pltpu.SemaphoreType.DMA((n_arr, 7)),
                pltpu.SemaphoreType.DMA((n_arr,)),
                pltpu.SemaphoreType.DMA((2,)),
            ],
        ),
        out_shape=[jax.ShapeDtypeStruct((s, IN_WIDTH), BF16), jax.ShapeDtypeStruct((s, d), BF16)]
        + [jax.ShapeDtypeStruct((N_DEV,) + a.shape, a.dtype) for a in shards],
        compiler_params=_params("arbitrary", "arbitrary"),
    )(order, x_in, norm_pre, *shards)
    return res[0], res[1], res[2:]


def _sigmoid(a):
    return jax.nn.sigmoid(a)


def _dsilu(a, sg):
    return sg * (1.0 + a * (1.0 - sg))


def _rms_fwd(u, gain):
    r = lax.rsqrt(_rowmean(u * u) + EPS)
    n = u * r
    return r, n, n * gain


def _rms_bwd(dy, gain, r, n):
    dn = dy * gain
    return _colsum(dy * n), r * (dn - n * _rowmean(dn * n))


def _mix(proj, x, target, kv, kv_t, wout, wout_t, pool_w, pool_w_t, pool_scale, ln_g, ln_b, sgu_bias, sgu_wm, sgu_wm_t, branch_norm, norm_post):
    s, d = x.shape
    t = min(TILE_MIX, s)
    n_tiles = s // t
    n_chunks = t // SGU_CHUNK
    halo_blocks_per_tile = t // HALO
    inv_d = 1.0 / d
    scale = 1.0 / (XATTN_HEAD_DIM**0.5)

    def body(
        proj_ref, halo_ref, x_ref, tgt_ref, kv_ref, kvt_ref, wout_ref, wout_t_ref, pw_ref, pwt_ref, pscale_ref, lng_ref,
        lnb_ref, bias_ref, wm_ref, wmt_ref, bnorm_ref, gpost_ref,
        y_ref, dout_ref, dxo_ref, dproj_ref, loss_ref, dgpost_ref, dbnorm_ref, dpscale_ref, dlng_ref, dlnb_ref,
        dpw_out, dwm_out, dbias_ref, dkv_out,
        carry_ref, dzsum_ref, dpw_ref, dwm_ref, dkv_ref,
    ):
        i = pl.program_id(0)
        tile = n_tiles - 1 - i

        @pl.when(i == 0)
        def _():
            carry_ref[...] = jnp.zeros_like(carry_ref)
            dzsum_ref[...] = jnp.zeros_like(dzsum_ref)
            for ref in (loss_ref, dgpost_ref, dbnorm_ref, dpscale_ref, dlng_ref, dlnb_ref, dpw_ref, dwm_ref, dkv_ref):
                ref[...] = jnp.zeros_like(ref)

        t_glob = tile * t + lax.broadcasted_iota(jnp.int32, (t, 1), 0)
        inv_cnt = [1.0 / jnp.minimum(t_glob + 1, w).astype(F32) for w in POOL_WINDOWS]

        xa = proj_ref[:, 0:BRANCH].astype(F32)
        ga = proj_ref[:, BRANCH : 2 * BRANCH].astype(F32)
        halo = jnp.where(tile == 0, 0.0, halo_ref[...].astype(F32))
        d_bf, pm_parts = [], []
        for g, w in enumerate(POOL_WINDOWS):
            cols = slice(g * POOL_GROUP_DIM, (g + 1) * POOL_GROUP_DIM)
            acc = jnp.concatenate([halo[:, cols], xa[:, cols]], axis=0)
            k = 1
            while k < w:
                acc = acc + pltpu.roll(acc, k, axis=0)
                k *= 2
            dg = (acc[HALO:, :] * inv_cnt[g] - xa[:, cols]).astype(BF16)
            d_bf.append(dg)
            pm_parts.append(_dot(dg, pw_ref[g]))
        pm = jnp.concatenate(pm_parts, axis=1)
        pscale = pscale_ref[...]
        pa = pm * pscale
        sga = _sigmoid(ga)
        sila = ga * sga
        ua = pa * sila
        g_a = bnorm_ref[:, 0:BRANCH]
        ra, na, ya = _rms_fwd(ua, g_a)

        u = proj_ref[:, 2 * BRANCH : 3 * BRANCH].astype(F32)
        v = proj_ref[:, 3 * BRANCH : 4 * BRANCH].astype(F32)
        gb = proj_ref[:, 4 * BRANCH : 5 * BRANCH].astype(F32)
        lng = lng_ref[...]
        vc = v - _rowmean(v)
        rstd = lax.rsqrt(_rowmean(vc * vc) + EPS)
        vhat = vc * rstd
        vn_bf = (vhat * lng + lnb_ref[...]).astype(BF16)
        z_rows = []
        for c in range(n_chunks):
            rows = slice(c * SGU_CHUNK, (c + 1) * SGU_CHUNK)
            z_rows.append(
                jnp.concatenate(
                    [
                        _dot(wm_ref[hd], vn_bf[rows, hd * SGU_HEAD_DIM : (hd + 1) * SGU_HEAD_DIM])
                        for hd in range(SGU_HEADS)
                    ],
                    axis=1,
                )
                + bias_ref[...]
            )
        z = z_rows[0] if n_chunks == 1 else jnp.concatenate(z_rows, axis=0)
        sb = u * z
        sgb = _sigmoid(gb)
        silb = gb * sgb
        ub = sb * silb
        g_b = bnorm_ref[:, BRANCH : 2 * BRANCH]
        rb, nb, yb = _rms_fwd(ub, g_b)

        q = proj_ref[:, 5 * BRANCH : 6 * BRANCH]
        gc = proj_ref[:, 6 * BRANCH : 7 * BRANCH].astype(F32)
        q_bf, p_bf, o_parts = [], [], []
        for hd in range(XATTN_HEADS):
            cols = slice(hd * XATTN_HEAD_DIM, (hd + 1) * XATTN_HEAD_DIM)
            qh = q[:, cols]
            sc = _dot(qh, kvt_ref[cols, :]) * scale
            e = jnp.exp(sc - jnp.max(sc, axis=-1, keepdims=True))
            p = e / jnp.sum(e, axis=-1, keepdims=True)
            q_bf.append(qh)
            p_bf.append(p.astype(BF16))
            o_parts.append(_dot(p_bf[hd], kv_ref[:, BRANCH + hd * XATTN_HEAD_DIM : BRANCH + (hd + 1) * XATTN_HEAD_DIM]))
        o = jnp.concatenate(o_parts, axis=1)
        sgc = _sigmoid(gc)
        silc = gc * sgc
        uc = o * silc
        g_c = bnorm_ref[:, 2 * BRANCH : 3 * BRANCH]
        rc, nc, yc = _rms_fwd(uc, g_c)

        out = None
        for b, y_branch in enumerate((ya, yb, yc)):
            rows = slice(b * BRANCH, (b + 1) * BRANCH)
            y_bf = y_branch.astype(BF16)
            y_ref[:, rows] = y_bf
            part = _dot(y_bf, wout_ref[rows, :])
            out = part if out is None else out + part
        gpost = gpost_ref[...]
        r_out = lax.rsqrt(_rowmean(out * out) + EPS)
        on = out * r_out
        err = x_ref[...] + on * gpost - tgt_ref[...]
        loss_ref[...] += 0.5 * jnp.sum(_rowmean(err * err), axis=0, keepdims=True)

        dxo = err * inv_d
        dxo_ref[...] = dxo
        dgp, dout = _rms_bwd(dxo, gpost, r_out, on)
        dgpost_ref[...] += dgp
        dout_bf = dout.astype(BF16)
        dout_ref[...] = dout_bf
        dy = [_dot(dout_bf, wout_t_ref[:, b * BRANCH : (b + 1) * BRANCH]) for b in range(3)]

        dg_a, dua = _rms_bwd(dy[0], g_a, ra, na)
        dg_b, dub = _rms_bwd(dy[1], g_b, rb, nb)
        dg_c, duc = _rms_bwd(dy[2], g_c, rc, nc)
        dbnorm_ref[...] += jnp.concatenate([dg_a, dg_b, dg_c], axis=1)

        dpa = dua * sila
        dga = dua * pa * _dsilu(ga, sga)
        dpscale_ref[...] += _colsum(dpa * pm)
        dpm = dpa * pscale
        dxa_parts, carry_parts = [], []
        for g, w in enumerate(POOL_WINDOWS):
            cols = slice(g * POOL_GROUP_DIM, (g + 1) * POOL_GROUP_DIM)
            dpm_g = dpm[:, cols].astype(BF16)
            dd = _dot(dpm_g, pwt_ref[g])
            dpw_ref[g] += _dot(d_bf[g], dpm_g, TN)
            cg = dd * inv_cnt[g]
            carry_parts.append(cg[0:HALO, :])
            acc = jnp.concatenate([cg, carry_ref[:, cols]], axis=0)
            k = 1
            while k < w:
                acc = acc + pltpu.roll(acc, t + HALO - k, axis=0)
                k *= 2
            dxa_parts.append(acc[0:t, :] - dd)
        carry_ref[...] = jnp.concatenate(carry_parts, axis=1)
        dxa = jnp.concatenate(dxa_parts, axis=1)

        dsb = dub * silb
        dgb = dub * sb * _dsilu(gb, sgb)
        du = dsb * z
        dz = dsb * u
        dz_bf = dz.astype(BF16)
        dvn_rows = []
        dz_sum = None
        for c in range(n_chunks):
            rows = slice(c * SGU_CHUNK, (c + 1) * SGU_CHUNK)
            dz_sum = dz[rows, :] if dz_sum is None else dz_sum + dz[rows, :]
            parts = []
            for hd in range(SGU_HEADS):
                cols = slice(hd * SGU_HEAD_DIM, (hd + 1) * SGU_HEAD_DIM)
                parts.append(_dot(wmt_ref[hd], dz_bf[rows, cols]))
                dwm_ref[hd] += _dot(dz_bf[rows, cols], vn_bf[rows, cols], NT)
            dvn_rows.append(jnp.concatenate(parts, axis=1))
        dzsum_ref[...] += dz_sum
        dvn = dvn_rows[0] if n_chunks == 1 else jnp.concatenate(dvn_rows, axis=0)
        dlng_ref[...] += _colsum(dvn * vhat)
        dlnb_ref[...] += _colsum(dvn)
        dvh = dvn * lng
        dv = rstd * (dvh - _rowmean(dvh) - vhat * _rowmean(dvh * vhat))

        do = duc * silc
        dgc = duc * o * _dsilu(gc, sgc)
        dq_parts = []
        for hd in range(XATTN_HEADS):
            cols = slice(hd * XATTN_HEAD_DIM, (hd + 1) * XATTN_HEAD_DIM)
            vcols = slice(BRANCH + hd * XATTN_HEAD_DIM, BRANCH + (hd + 1) * XATTN_HEAD_DIM)
            do_h = do[:, cols].astype(BF16)
            p = p_bf[hd].astype(F32)
            dp = _dot(do_h, kvt_ref[vcols, :])
            dkv_ref[:, vcols] += _dot(p_bf[hd], do_h, TN)
            ds_bf = (p * (dp - jnp.sum(dp * p, axis=-1, keepdims=True)) * scale).astype(BF16)
            dq_parts.append(_dot(ds_bf, kv_ref[:, cols]))
            dkv_ref[:, cols] += _dot(ds_bf, q_bf[hd], TN)
        dq = jnp.concatenate(dq_parts, axis=1)

        dproj_ref[...] = jnp.concatenate([dxa, dga, du, dv, dgb, dq, dgc], axis=1).astype(BF16)

        @pl.when(i == n_tiles - 1)
        def _():
            keep = lax.broadcasted_iota(jnp.int32, (SGU_CHUNK, SGU_CHUNK), 0) >= lax.broadcasted_iota(
                jnp.int32, (SGU_CHUNK, SGU_CHUNK), 1
            )
            for hd in range(SGU_HEADS):
                dwm_ref[hd] = jnp.where(keep, dwm_ref[hd], 0.0)
                per_pos = dzsum_ref[:, hd * SGU_HEAD_DIM : (hd + 1) * SGU_HEAD_DIM]
                dbias_ref[hd : hd + 1, :] = _colsum(per_pos.T)
            for acc, res in ((dpw_ref, dpw_out), (dwm_ref, dwm_out), (dkv_ref, dkv_out)):
                pltpu.sync_copy(acc, res)

    row_tile = lambda width: pl.BlockSpec((t, width), lambda i: (n_tiles - 1 - i, 0))
    halo_spec = pl.BlockSpec(
        (HALO, BRANCH), lambda i: (jnp.maximum((n_tiles - 1 - i) * halo_blocks_per_tile - 1, 0), 0)
    )
    acc_shapes = [
        (1, 128),
        (1, d),
        (1, MIX_WIDTH),
        (1, BRANCH),
        (1, BRANCH),
        (1, BRANCH),
        pool_w.shape,
        sgu_wm.shape,
        (SGU_HEADS, SGU_CHUNK),
        kv.shape,
    ]
    return pl.pallas_call(
        body,
        name="mix",
        grid=(n_tiles,),
        in_specs=[
            row_tile(IN_WIDTH), halo_spec, row_tile(d), row_tile(d), _resident(kv.shape), _resident(kv_t.shape),
            _resident(wout.shape), _resident(wout_t.shape), _resident(pool_w.shape), _resident(pool_w_t.shape),
            _full((1, BRANCH)), _full((1, BRANCH)), _full((1, BRANCH)), _resident((SGU_CHUNK, BRANCH)),
            _resident(sgu_wm.shape), _resident(sgu_wm_t.shape), _full((1, MIX_WIDTH)), _full((1, d)),
        ],
        out_specs=[row_tile(MIX_WIDTH), row_tile(d), row_tile(d), row_tile(IN_WIDTH)]
        + [ANY if len(a) == 3 or a == kv.shape else _full(a) for a in acc_shapes],
        out_shape=[
            jax.ShapeDtypeStruct((s, MIX_WIDTH), BF16),
            jax.ShapeDtypeStruct((s, d), BF16),
            jax.ShapeDtypeStruct((s, d), F32),
            jax.ShapeDtypeStruct((s, IN_WIDTH), BF16),
        ]
        + [jax.ShapeDtypeStruct(a, F32) for a in acc_shapes],
        scratch_shapes=[
            pltpu.VMEM((HALO, BRANCH), F32), pltpu.VMEM((SGU_CHUNK, BRANCH), F32), pltpu.VMEM(pool_w.shape, F32),
            pltpu.VMEM(sgu_wm.shape, F32), pltpu.VMEM(kv.shape, F32),
        ],
        compiler_params=_params("arbitrary", vmem_limit_bytes=VMEM_LIMIT_MIX_BYTES),
    )(
        proj, proj, x, target, kv, kv_t, wout, wout_t, pool_w, pool_w_t, pool_scale, ln_g, ln_b, sgu_bias, sgu_wm,
        sgu_wm_t, branch_norm, norm_post,
    )


def _position():
    return lax.axis_index("x"), lax.axis_index("y"), lax.axis_index("c")


N_CHIPS = 4


def _landing_shape(kind, a):
    return (N_CHIPS,) + a.shape[2:] if kind == "pair" else a.shape


def _carry_specs(groups):
    arrays = [(kind, a) for kind, arrs in groups for a in arrs]
    scratch = []
    for _, arrs in groups:
        n = len(arrs)
        scratch += [pltpu.SemaphoreType.DMA((n, N_DEV)), pltpu.SemaphoreType.DMA((n, N_DEV)), pltpu.SemaphoreType.DMA((n,))]
    return dict(
        n=len(arrays),
        operands=[a for _, a in arrays],
        in_specs=[ANY] * len(arrays),
        out_specs=[ANY] * len(arrays),
        out_shape=[jax.ShapeDtypeStruct(_landing_shape(kind, a), a.dtype) for kind, a in arrays],
        scratch_shapes=scratch,
    )


def _carry(groups, src, out, sems):
    x, y, c = _position()
    chip = 2 * x + y
    me = 2 * chip + c

    def remote(s, d, send_sems, recv_sems, a, m, to):
        return pltpu.make_async_remote_copy(
            src_ref=s, dst_ref=d, send_sem=send_sems.at[a, m], recv_sem=recv_sems.at[a, m], device_id=to,
            device_id_type=MESH,
        )

    def copies():
        far, near = [], []
        at = 0
        for g, (kind, arrs) in enumerate(groups):
            send_sems, recv_sems, local_sems = sems[3 * g : 3 * g + 3]
            for a in range(len(arrs)):
                s, d = src[at + a], out[at + a]
                if kind == "pair":
                    far.append(remote(s.at[:, 1 - c], d, send_sems, recv_sems, a, 1, (x, y, 1 - c)))
                elif kind == "chips":
                    for m in range(1, N_CHIPS):
                        px, py = x ^ (m >> 1), y ^ (m & 1)
                        far.append(remote(s.at[2 * px + py], d.at[chip], send_sems, recv_sems, a, m, (px, py, c)))
                    near.append(pltpu.make_async_copy(s.at[chip], d.at[chip], local_sems.at[a]))
                else:
                    for m in range(1, N_DEV):
                        px, py, pc = x ^ ((m >> 2) & 1), y ^ ((m >> 1) & 1), c ^ (m & 1)
                        far.append(
                            remote(s.at[4 * px + 2 * py + pc], d.at[me], send_sems, recv_sems, a, m, (px, py, pc))
                        )
                    near.append(pltpu.make_async_copy(s.at[me], d.at[me], local_sems.at[a]))
            at += len(arrs)
        return far, near

    def start():
        far, near = copies()
        for cp in near + far:
            cp.start()

    def finish():
        far, near = copies()
        for cp in far:
            cp.wait_recv()
        for cp in far:
            cp.wait_send()
        for cp in near:
            cp.wait()

    return start, finish


def _exchange(groups, name):
    carried = _carry_specs(groups)
    n_c = carried["n"]

    def body(*refs):
        start, finish = _carry(groups, refs[:n_c], refs[n_c : 2 * n_c], refs[2 * n_c :])
        start()
        finish()

    return pl.pallas_call(
        body,
        name=name,
        in_specs=carried["in_specs"],
        out_specs=carried["out_specs"],
        out_shape=carried["out_shape"],
        scratch_shapes=carried["scratch_shapes"],
    )(*carried["operands"])


def _pair_sum(mine, theirs, name, groups=()):
    n = len(mine)
    carried = _carry_specs(groups)
    n_c = carried["n"]
    core = lax.axis_index("c").astype(jnp.int32).reshape(1)

    def body(core_ref, *refs):
        del core_ref
        own = refs[:n]
        sib = refs[n : 2 * n]
        src = refs[2 * n : 2 * n + n_c]
        out = refs[2 * n + n_c : 3 * n + n_c]
        landed = refs[3 * n + n_c : 3 * n + 2 * n_c]
        start, finish = _carry(groups, src, landed, refs[3 * n + 2 * n_c :])
        b = pl.program_id(0)

        @pl.when(b == 0)
        def _():
            start()

        for a in range(n):
            out[a][...] = (own[a][...].astype(F32) + sib[a][...].astype(F32)).astype(out[a].dtype)

        @pl.when(b == N_CHIPS - 1)
        def _():
            finish()

    block = lambda a: pl.BlockSpec((None,) + a.shape[1:], lambda b, core_ref: (b, 0, 0))
    res = pl.pallas_call(
        body,
        name=name,
        grid_spec=pltpu.PrefetchScalarGridSpec(
            num_scalar_prefetch=1,
            grid=(N_CHIPS,),
            in_specs=[pl.BlockSpec((None, None) + a.shape[2:], lambda b, core_ref: (b, core_ref[0], 0, 0)) for a in mine]
            + [block(a) for a in theirs]
            + carried["in_specs"],
            out_specs=[block(a) for a in theirs] + carried["out_specs"],
            scratch_shapes=carried["scratch_shapes"],
        ),
        out_shape=[jax.ShapeDtypeStruct(a.shape, a.dtype) for a in theirs] + carried["out_shape"],
        compiler_params=_params("arbitrary"),
    )(core, *mine, *theirs, *carried["operands"])
    return res[:n], res[n:]


def _weight_grad(a, b, n_blk, blocked, name, groups):
    s = a.shape[0]
    t = min(TILE_GRAD, s)
    n_t = s // t
    n_pairs = n_blk // 2
    if blocked == "cols":
        k, c = a.shape[1], b.shape[1] // n_blk
        a_spec = pl.BlockSpec((t, k), lambda j, i: (i, 0))
        b_spec = pl.BlockSpec((t, 2 * c), lambda j, i: (i, j))
        acc_shape = (k, 2 * c)
    else:
        k, c = a.shape[1] // n_blk, b.shape[1]
        a_spec = pl.BlockSpec((t, 2 * k), lambda j, i: (i, j))
        b_spec = pl.BlockSpec((t, c), lambda j, i: (i, 0))
        acc_shape = (2 * k, c)
    carried = _carry_specs(groups)
    n_p = carried["n"]

    def body(a_ref, b_ref, *refs):
        src = refs[:n_p]
        o_ref, theirs_ref = refs[n_p : n_p + 2]
        landed = refs[n_p + 2 : 2 * n_p + 2]
        acc_ref, sbuf, pair_send, pair_recv = refs[2 * n_p + 2 : 2 * n_p + 6]
        start, finish = _carry(groups, src, landed, refs[2 * n_p + 6 :])
        j = pl.program_id(0)
        i = pl.program_id(1)
        x, y, c_me = _position()

        def to_sibling(pair):
            return pltpu.make_async_remote_copy(
                src_ref=sbuf.at[1 - c_me], dst_ref=theirs_ref.at[pair], send_sem=pair_send.at[pair],
                recv_sem=pair_recv.at[pair], device_id=(x, y, 1 - c_me), device_id_type=MESH,
            )

        @pl.when(jnp.logical_and(j == 0, i == 0))
        def _():
            start()

        @pl.when(i == 0)
        def _():
            acc_ref[...] = jnp.zeros_like(acc_ref)

        acc_ref[...] += _dot(a_ref[...], b_ref[...], TN)

        @pl.when(i == n_t - 1)
        def _():
            for pair in range(1, n_pairs):

                @pl.when(j == pair)
                def _():
                    to_sibling(pair - 1).wait_send()

            for half in range(2):
                if blocked == "cols":
                    block = acc_ref[:, half * c : (half + 1) * c].astype(BF16)
                else:
                    block = acc_ref[half * k : (half + 1) * k, :].astype(BF16)
                o_ref[half] = block
                sbuf[half] = block
            for pair in range(n_pairs):

                @pl.when(j == pair)
                def _():
                    to_sibling(pair).start()

        @pl.when(jnp.logical_and(j == n_pairs - 1, i == n_t - 1))
        def _():
            to_sibling(n_pairs - 1).wait_send()
            for pair in range(n_pairs):
                to_sibling(pair).wait_recv()
            finish()

    res = pl.pallas_call(
        body,
        name=name,
        grid=(n_pairs, n_t),
        in_specs=[a_spec, b_spec] + carried["in_specs"],
        out_specs=[pl.BlockSpec((2, k, c), lambda j, i: (j, 0, 0)), ANY] + carried["out_specs"],
        out_shape=[jax.ShapeDtypeStruct((n_blk, k, c), BF16), jax.ShapeDtypeStruct((n_pairs, k, c), BF16)]
        + carried["out_shape"],
        scratch_shapes=[
            pltpu.VMEM(acc_shape, F32), pltpu.VMEM((2, k, c), BF16), pltpu.SemaphoreType.DMA((n_pairs,)),
            pltpu.SemaphoreType.DMA((n_pairs,)),
        ]
        + carried["scratch_shapes"],
        compiler_params=_params("arbitrary", "arbitrary"),
    )(a, b, *carried["operands"])
    return res[0], res[1], res[2:]


def _input_grad(dproj, win_t, x, dxo, norm_pre, groups):
    s, d = x.shape
    t = min(TILE_GRAD, s)
    n_t = s // t
    kb = BRANCH
    n_k = win_t.shape[0] // kb
    carried = _carry_specs(groups)
    n_p = carried["n"]

    def body(dp_ref, w_ref, x_ref, dxo_ref, g_ref, *refs):
        src = refs[:n_p]
        gx_ref, dg_ref = refs[n_p : n_p + 2]
        landed = refs[n_p + 2 : 2 * n_p + 2]
        acc_ref = refs[2 * n_p + 2]
        start, finish = _carry(groups, src, landed, refs[2 * n_p + 3 :])
        i = pl.program_id(0)
        j = pl.program_id(1)

        @pl.when(jnp.logical_and(i == 0, j == 0))
        def _():
            start()
            dg_ref[...] = jnp.zeros_like(dg_ref)

        @pl.when(j == 0)
        def _():
            acc_ref[...] = jnp.zeros_like(acc_ref)

        acc_ref[...] += _dot(dp_ref[...], w_ref[...])

        @pl.when(j == n_k - 1)
        def _():
            xv = x_ref[...]
            gain = g_ref[...]
            r = lax.rsqrt(_rowmean(xv * xv) + EPS)
            dgain, dx = _rms_bwd(acc_ref[...], gain, r, xv * r)
            dg_ref[...] += dgain
            gx_ref[...] = dxo_ref[...] + dx

        @pl.when(jnp.logical_and(i == n_t - 1, j == n_k - 1))
        def _():
            finish()

    res = pl.pallas_call(
        body,
        name="input_grad",
        grid=(n_t, n_k),
        in_specs=[
            pl.BlockSpec((t, kb), lambda i, j: (i, j)),
            pl.BlockSpec((kb, d), lambda i, j: (j, 0)),
            pl.BlockSpec((t, d), lambda i, j: (i, 0)),
            pl.BlockSpec((t, d), lambda i, j: (i, 0)),
            _full((1, d)),
        ]
        + carried["in_specs"],
        out_specs=[pl.BlockSpec((t, d), lambda i, j: (i, 0)), _full((1, d))] + carried["out_specs"],
        out_shape=[jax.ShapeDtypeStruct((s, d), F32), jax.ShapeDtypeStruct((1, d), F32)] + carried["out_shape"],
        scratch_shapes=[pltpu.VMEM((t, d), F32)] + carried["scratch_shapes"],
        compiler_params=_params("arbitrary", "arbitrary"),
    )(dproj, win_t, x, dxo, norm_pre, *carried["operands"])
    return res[0], res[1], res[2:]


def _kv_backward(dkv, memn, wkv, mem, mem_norm):
    m, d = mem.shape
    n = wkv.shape[1]

    def body(dkv_ref, memn_ref, w_ref, mem_ref, g_ref, gw_ref, dg_ref):
        dkv_bf = dkv_ref[...].astype(BF16)
        gw_ref[...] = _dot(memn_ref[...], dkv_bf, TN).astype(BF16).reshape(N_DEV, WKV_BLK, n)
        dmemn = _dot(dkv_bf, w_ref[...], NT)
        mv = mem_ref[...]
        r = lax.rsqrt(_rowmean(mv * mv) + EPS)
        dg_ref[...] = _colsum(dmemn * (mv * r))

    return pl.pallas_call(
        body,
        name="kv_backward",
        grid=(1,),
        in_specs=[_full((m, n)), _full((m, d)), _full(wkv.shape), _full((m, d)), _full((1, d))],
        out_specs=[_full((N_DEV, WKV_BLK, n)), _full((1, d))],
        out_shape=[jax.ShapeDtypeStruct((N_DEV, WKV_BLK, n), BF16), jax.ShapeDtypeStruct((1, d), F32)],
        compiler_params=_params("arbitrary"),
    )(dkv, memn, wkv, mem, mem_norm)


def _adamw_math(w, g, m, v):
    m = ADAM_B1 * m + (1.0 - ADAM_B1) * g
    v = ADAM_B2 * v + (1.0 - ADAM_B2) * (g * g)
    m_hat = m / (1.0 - ADAM_B1**ADAM_STEP)
    v_hat = v / (1.0 - ADAM_B2**ADAM_STEP)
    delta = -ADAM_LR * (m_hat / (jnp.sqrt(v_hat) + ADAM_EPS) + ADAM_WD * w)
    return delta, m, v


def _adamw(parts, w, m, v, name):
    r, c = w.shape
    slots = parts.shape[0]
    t = r
    while t * c * 4 > TILE_ADAM_BYTES and t % 16 == 0:
        t //= 2

    def body(p_ref, w_ref, m_ref, v_ref, g_ref, d_ref, nm_ref, nv_ref):
        g = p_ref[0].astype(F32)
        for k in range(1, slots):
            g = g + p_ref[k].astype(F32)
        delta, nm, nv = _adamw_math(w_ref[...], g, m_ref[...], v_ref[...])
        g_ref[...] = g
        d_ref[...] = delta
        nm_ref[...] = nm
        nv_ref[...] = nv

    tile = pl.BlockSpec((t, c), lambda i: (i, 0))
    return pl.pallas_call(
        body,
        name=name,
        grid=(r // t,),
        in_specs=[pl.BlockSpec((slots, t, c), lambda i: (0, i, 0)), tile, tile, tile],
        out_specs=[tile] * 4,
        out_shape=[jax.ShapeDtypeStruct((r, c), F32)] * 4,
        compiler_params=_params("parallel"),
    )(parts, w, m, v)


def _adamw_packed(parts, triples, name):
    slots = parts.shape[0]
    sizes = [w.shape[0] for w, _, _ in triples]

    def body(p_ref, *refs):
        ins = refs[: 3 * len(triples)]
        outs = refs[3 * len(triples) :]
        at = 0
        for n, rows in enumerate(sizes):
            g = p_ref[0, at : at + rows, :]
            for k in range(1, slots):
                g = g + p_ref[k, at : at + rows, :]
            w_ref, m_ref, v_ref = ins[3 * n : 3 * n + 3]
            delta, nm, nv = _adamw_math(w_ref[...], g, m_ref[...], v_ref[...])
            for ref, val in zip(outs[4 * n : 4 * n + 4], (g, delta, nm, nv)):
                ref[...] = val
            at += rows

    flat = [a for t in triples for a in t]
    res = pl.pallas_call(
        body,
        name=name,
        out_shape=[jax.ShapeDtypeStruct(w.shape, F32) for w, _, _ in triples for _ in range(4)],
        compiler_params=pltpu.CompilerParams(vmem_limit_bytes=VMEM_LIMIT_BYTES),
    )(parts, *flat)
    return [res[4 * n : 4 * n + 4] for n in range(len(triples))]


SMALL = ("norm_pre", "pool_scale", "sgu_ln_g", "sgu_ln_b", "sgu_w", "sgu_b", "mem_norm", "branch_norm", "norm_post")


def _local_view(name, w):
    if name == "sgu_w":
        return w.reshape(SGU_HEADS, SGU_CHUNK, SGU_CHUNK)
    if name == "sgu_b":
        return w.reshape(SGU_HEADS, SGU_CHUNK)
    return w.reshape(1, -1)


def _forward_backward(x, mem, target, shards, small):
    causal = jnp.tril(jnp.ones((SGU_CHUNK, SGU_CHUNK), dtype=bool))
    sgu_wm = jnp.where(causal[None], small["sgu_w"], 0.0).astype(BF16)
    sgu_bias = jnp.repeat(jnp.transpose(small["sgu_b"]), SGU_HEAD_DIM, axis=1)

    proj, h, (win, wkv, pool_all, wout) = _proj_gather(x, small["norm_pre"], shards)
    wout = wout.reshape(MIX_WIDTH, D_MODEL)
    wkv = wkv.reshape(D_MODEL, 2 * BRANCH)
    pool_full = (
        pool_all.reshape(N_DEV, len(POOL_WINDOWS), POOL_BLK, POOL_GROUP_DIM)
        .transpose(1, 0, 2, 3)
        .reshape(len(POOL_WINDOWS), POOL_GROUP_DIM, POOL_GROUP_DIM)
    )
    memn, kv = _kv_forward(mem, small["mem_norm"], wkv)
    (y, dout, dxo, dproj, loss, d_norm_post, d_branch_norm, d_pool_scale, d_ln_g, d_ln_b, d_pool_w, d_sgu_w, d_sgu_b,
     dkv) = _mix(
        proj, x, target, kv, kv.T, wout, wout.T, pool_full, jnp.swapaxes(pool_full, 1, 2), small["pool_scale"],
        small["sgu_ln_g"], small["sgu_ln_b"], sgu_bias, sgu_wm, jnp.swapaxes(sgu_wm, 1, 2), small["branch_norm"],
        small["norm_post"],
    )
    g_wkv, d_mem_norm = _kv_backward(dkv, memn, wkv, mem, small["mem_norm"])
    g_pool = (
        d_pool_w.reshape(len(POOL_WINDOWS), N_DEV, POOL_BLK, POOL_GROUP_DIM)
        .transpose(1, 0, 2, 3)
        .reshape(N_DEV, len(POOL_WINDOWS) * POOL_BLK, POOL_GROUP_DIM)
        .astype(BF16)
    )
    small_grads = dict(
        pool_scale=d_pool_scale, sgu_ln_g=d_ln_g, sgu_ln_b=d_ln_b, sgu_w=d_sgu_w, sgu_b=d_sgu_b,
        mem_norm=d_mem_norm, branch_norm=d_branch_norm, norm_post=d_norm_post,
    )
    packed = jnp.concatenate([small_grads[n].reshape(-1, 128) for n in SMALL if n != "norm_pre"], axis=0)
    packed = jnp.broadcast_to(packed[None, None], (N_CHIPS, 2) + packed.shape)

    by_chip = lambda g: g.reshape((N_CHIPS, 2) + g.shape[1:])
    small_mine = [by_chip(g_wkv), by_chip(g_pool), packed]
    g_wout, wout_theirs, small_theirs = _weight_grad(y, dout, N_DEV, "rows", "grad_w_out", [("pair", small_mine)])
    sums, _ = _pair_sum(small_mine + [by_chip(g_wout)], list(small_theirs) + [wout_theirs], "pair_sum_first")
    g_win, win_theirs, (l_wkv, l_pool, l_packed, l_wout) = _weight_grad(
        h, dproj, N_DEV, "cols", "grad_w_in", [("chips", list(sums))]
    )
    (win_sums,), _ = _pair_sum([by_chip(g_win)], [win_theirs], "pair_sum_w_in")
    grad_x, d_norm_pre, (l_win,) = _input_grad(
        dproj, jnp.swapaxes(win, 1, 2).reshape(IN_WIDTH, D_MODEL), x, dxo, small["norm_pre"],
        [("chips", [win_sums])],
    )
    return loss, grad_x, dict(w_in=l_win, w_out=l_wout, w_kv=l_wkv, pool_w=l_pool), l_packed, d_norm_pre


def kernel(x, mem, norm_pre, w_in, pool_w, pool_scale, sgu_ln_g, sgu_ln_b, sgu_w, sgu_b, mem_norm, w_kv, branch_norm, w_out, norm_post, loss_target, m_norm_pre, m_w_in, m_pool_w, m_pool_scale, m_sgu_ln_g, m_sgu_ln_b, m_sgu_w, m_sgu_b, m_mem_norm, m_w_kv, m_branch_norm, m_w_out, m_norm_post, v_norm_pre, v_w_in, v_pool_w, v_pool_scale, v_sgu_ln_g, v_sgu_ln_b, v_sgu_w, v_sgu_b, v_mem_norm, v_w_kv, v_branch_norm, v_w_out, v_norm_post):
    weights = dict(norm_pre=norm_pre, w_in=w_in, pool_w=pool_w, pool_scale=pool_scale, sgu_ln_g=sgu_ln_g, sgu_ln_b=sgu_ln_b, sgu_w=sgu_w, sgu_b=sgu_b, mem_norm=mem_norm, w_kv=w_kv, branch_norm=branch_norm, w_out=w_out, norm_post=norm_post)
    first = dict(norm_pre=m_norm_pre, w_in=m_w_in, pool_w=m_pool_w, pool_scale=m_pool_scale, sgu_ln_g=m_sgu_ln_g, sgu_ln_b=m_sgu_ln_b, sgu_w=m_sgu_w, sgu_b=m_sgu_b, mem_norm=m_mem_norm, w_kv=m_w_kv, branch_norm=m_branch_norm, w_out=m_w_out, norm_post=m_norm_post)
    second = dict(norm_pre=v_norm_pre, w_in=v_w_in, pool_w=v_pool_w, pool_scale=v_pool_scale, sgu_ln_g=v_sgu_ln_g, sgu_ln_b=v_sgu_ln_b, sgu_w=v_sgu_w, sgu_b=v_sgu_b, mem_norm=v_mem_norm, w_kv=v_w_kv, branch_norm=v_branch_norm, w_out=v_w_out, norm_post=v_norm_post)
    order = ("norm_pre", "w_in", "pool_w", "pool_scale", "sgu_ln_g", "sgu_ln_b", "sgu_w", "sgu_b", "mem_norm", "w_kv", "branch_norm", "w_out", "norm_post")

    owned_shape = dict(
        w_in=(D_MODEL, WIN_BLK), w_out=(WOUT_BLK, D_MODEL), w_kv=(WKV_BLK, 2 * BRANCH),
        pool_w=(len(POOL_WINDOWS) * POOL_BLK, POOL_GROUP_DIM),
    )
    owned = {n: weights[n].reshape(owned_shape[n]) for n in owned_shape}
    small = {n: _local_view(n, weights[n]) for n in SMALL}
    loss, grad_x, landed, landed_packed, d_norm_pre = _forward_backward(
        x[0], mem[0], loss_target[0], [owned[n].astype(BF16) for n in ("w_in", "w_kv", "pool_w", "w_out")], small
    )
    d_norm_pre = d_norm_pre.reshape(-1, 128)
    (landed_norm_pre,) = _exchange(
        [("all", [jnp.broadcast_to(d_norm_pre[None], (N_DEV,) + d_norm_pre.shape)])], "exchange_norm_pre"
    )

    grads, deltas, new_m, new_v = {}, {}, {}, {}
    for n in owned_shape:
        shape = weights[n].shape
        res = _adamw(
            landed[n], owned[n], first[n].reshape(owned_shape[n]), second[n].reshape(owned_shape[n]), "adamw_" + n
        )
        grads[n], deltas[n], new_m[n], new_v[n] = (a.reshape(shape) for a in res)
    rows_of = lambda tree, n: tree[n].reshape(-1, 128)
    for names, parts, name in (
        ([n for n in SMALL if n != "norm_pre"], landed_packed, "adamw_replicated"),
        (["norm_pre"], landed_norm_pre, "adamw_norm_pre"),
    ):
        res = _adamw_packed(parts, [(rows_of(weights, n), rows_of(first, n), rows_of(second, n)) for n in names], name)
        for n, four in zip(names, res):
            for tree, a in zip((grads, deltas, new_m, new_v), four):
                tree[n] = a.reshape(weights[n].shape)

    total = lax.psum(loss[0, 0], ("x", "y", "c"))
    return (
        total,
        grad_x[None],
        *[grads[n] for n in order],
        *[deltas[n] for n in order],
        *[new_m[n] for n in order],
        *[new_v[n] for n in order],
    )
```

```python
import functools

import jax
import jax.numpy as jnp
from jax import lax
from jax.experimental import pallas as pl
from jax.experimental.pallas import tpu as pltpu

F32 = jnp.float32
BF16 = jnp.bfloat16
EPS = 1e-6

D_MODEL = 2048
POOL_WINDOWS = (2, 4, 8, 16)
POOL_GROUP_DIM = 256
BRANCH = 1024
SGU_CHUNK = 128
SGU_HEADS = 8
SGU_HEAD_DIM = 128
XATTN_HEADS = 4
XATTN_HEAD_DIM = 256
MIX_WIDTH = 3 * BRANCH
IN_WIDTH = 7 * BRANCH
N_DEV = 8
WIN_BLK = IN_WIDTH // N_DEV
WOUT_BLK = MIX_WIDTH // N_DEV
WKV_BLK = D_MODEL // N_DEV
POOL_BLK = POOL_GROUP_DIM // N_DEV
HALO = 16

ADAM_LR = 0.001
ADAM_B1 = 0.9
ADAM_B2 = 0.999
ADAM_EPS = 1e-08
ADAM_WD = 0.01
ADAM_STEP = 10

VMEM_LIMIT_BYTES = 56 * 1024 * 1024
VMEM_LIMIT_MIX_BYTES = 63 * 1024 * 1024

TILE_PROJ = 512
TILE_MIX = 128
TILE_GRAD = 512
TILE_ADAM_BYTES = 1 << 20

ANY = pl.BlockSpec(memory_space=pl.ANY)
NN = (((1,), (0,)), ((), ()))
NT = (((1,), (1,)), ((), ()))
TN = (((0,), (0,)), ((), ()))
MESH = pl.DeviceIdType.MESH


def _dot(a, b, dims=NN):
    return lax.dot_general(a, b, dims, preferred_element_type=F32)


def _params(*semantics, vmem_limit_bytes=VMEM_LIMIT_BYTES):
    return pltpu.CompilerParams(dimension_semantics=semantics, vmem_limit_bytes=vmem_limit_bytes)


def _rowmean(a):
    return jnp.mean(a, axis=-1, keepdims=True)


def _colsum(a):
    return jnp.sum(a, axis=0, keepdims=True)


def _full(shape):
    zeros = (0,) * len(shape)
    return pl.BlockSpec(shape, lambda *_: zeros)


def _resident(shape):
    zeros = (0,) * len(shape)
    return pl.BlockSpec(shape, lambda *_: zeros, pipeline_mode=pl.Buffered(1))


def _kv_forward(mem, mem_norm, wkv):
    m, d = mem.shape
    n = wkv.shape[1]
    cols = 512

    def body(mem_ref, g_ref, w_ref, memn_ref, kv_ref):
        mv = mem_ref[...]
        r = lax.rsqrt(_rowmean(mv * mv) + EPS)
        memn = (mv * r * g_ref[...]).astype(BF16)
        memn_ref[...] = memn
        kv_ref[...] = _dot(memn, w_ref[...]).astype(BF16)

    return pl.pallas_call(
        body,
        name="kv_forward",
        grid=(n // cols,),
        in_specs=[_full((m, d)), _full((1, d)), pl.BlockSpec((d, cols), lambda j: (0, j))],
        out_specs=[_full((m, d)), pl.BlockSpec((m, cols), lambda j: (0, j))],
        out_shape=[jax.ShapeDtypeStruct((m, d), BF16), jax.ShapeDtypeStruct((m, n), BF16)],
        compiler_params=_params("arbitrary"),
    )(mem, mem_norm, wkv)


def _proj_gather(x_in, norm_pre, shards):
    s, d = x_in.shape
    t = min(TILE_PROJ, s)
    n_t = s // t
    n_arr = len(shards)

    def places(x, y, c):
        return (x, y, c), (x, y, 1 - c), (x ^ c, y ^ (1 - c)), (x ^ (1 - c), y ^ c), (1 - x, 1 - y)

    def index(chip, core):
        return 4 * chip[0] + 2 * chip[1] + core

    _, _, chip_a, chip_b, chip_d = places(*_position())
    c_out = lax.axis_index("c")
    me_out = index((lax.axis_index("x"), lax.axis_index("y")), c_out)
    order = jnp.stack(
        [
            me_out, me_out ^ 1, index(chip_a, c_out), index(chip_b, c_out), index(chip_b, 1 - c_out),
            index(chip_a, 1 - c_out), index(chip_d, c_out), index(chip_d, 1 - c_out),
        ]
    ).astype(jnp.int32)

    def body(order_ref, x_ref, g_ref, *refs):
        del order_ref
        src = refs[:n_arr]
        proj_ref, h_ref = refs[n_arr : n_arr + 2]
        out = refs[n_arr + 2 : 2 * n_arr + 2]
        wbuf, hs, send_sems, recv_sems, local_sems, load_sems = refs[2 * n_arr + 2 :]
        j = pl.program_id(0)
        i = pl.program_id(1)
        me, sibling, chip_a, chip_b, chip_d = places(*_position())
        c = me[2]

        def block(a, chip, core):
            return out[a].at[index(chip, core)]

        def copy(a, k, owner, to, from_input=False):
            return pltpu.make_async_remote_copy(
                src_ref=src[a] if from_input else block(a, *owner),
                dst_ref=block(a, *owner),
                send_sem=send_sems.at[a, k],
                recv_sem=recv_sems.at[a, k],
                device_id=to,
                device_id_type=MESH,
            )

        mine = (me[:2], c)

        def own(a):
            return pltpu.make_async_copy(src[a], block(a, *mine), local_sems.at[a])

        def first_sends(a):
            return [
                copy(a, 0, mine, sibling, from_input=True),
                copy(a, 1, mine, (*chip_a, c), from_input=True),
                copy(a, 2, mine, (*chip_b, c), from_input=True),
            ]

        def onward(a, k):
            owner = {3: chip_a, 4: chip_a, 5: chip_b, 6: chip_d}[k]
            return copy(a, k, (owner, c), (*chip_b, c) if k == 3 else sibling)

        def landed(a, k):
            owner = {0: mine[0], 1: chip_a, 2: chip_b, 3: chip_d, 4: chip_b, 5: chip_a, 6: chip_d}[k]
            core = c if k in (1, 2, 3) else 1 - c
            copy(a, k, (owner, core), me).wait_recv()
            return owner, core

        def load(ref, step):
            return pltpu.make_async_copy(ref, wbuf.at[step % 2], load_sems.at[step % 2])

        @pl.when(jnp.logical_and(j == 0, i == 0))
        def _():
            own(0).start()
            for cp in first_sends(0):
                cp.start()
            load(src[0], 0).start()
            load(src[0], 0).wait()

        steps = {1: (0, ()), 2: (1, (3, 4)), 3: (2, (5,)), 4: (4, ()), 5: (5, ()), 6: (3, (6,)), 7: (6, ())}
        for step, (k, then) in steps.items():

            @pl.when(jnp.logical_and(j == step, i == 0))
            def _():
                load(src[0], step).wait()

            @pl.when(jnp.logical_and(j == step - 1, i == n_t - 1))
            def _():
                owner = landed(0, k)
                for k2 in then:
                    onward(0, k2).start()
                if step == 2:
                    for a in range(1, n_arr):
                        own(a).start()
                        for cp in first_sends(a):
                            cp.start()
                if step == 6:
                    for a in range(1, n_arr):
                        for k1, then1 in ((1, (3, 4)), (2, (5,))):
                            landed(a, k1)
                            for k2 in then1:
                                onward(a, k2).start()
                load(block(0, *owner), step).start()

        @pl.when(j == 0)
        def _():
            xv = x_ref[...]
            h = (xv * lax.rsqrt(_rowmean(xv * xv) + EPS) * g_ref[...]).astype(BF16)
            hs[i] = h
            h_ref[...] = h

        proj_ref[...] = _dot(hs[i], wbuf[j % 2]).astype(BF16)

        @pl.when(jnp.logical_and(j == N_DEV - 1, i == n_t - 1))
        def _():
            for a in range(1, n_arr):
                landed(a, 3)
                onward(a, 6).start()
            for a in range(1, n_arr):
                for k in (0, 4, 5, 6):
                    landed(a, k)
            for a in range(n_arr):
                for cp in first_sends(a) + [onward(a, k) for k in (3, 4, 5, 6)]:
                    cp.wait_send()
                own(a).wait()

    res = pl.pallas_call(
        body,
        name="proj_gather",
        grid_spec=pltpu.PrefetchScalarGridSpec(
            num_scalar_prefetch=1,
            grid=(N_DEV, n_t),
            in_specs=[
                pl.BlockSpec((t, d), lambda j, i, order_ref: (jnp.where(j == 0, i, n_t - 1), 0)),
                pl.BlockSpec((1, d), lambda j, i, order_ref: (0, 0)),
            ]
            + [ANY] * n_arr,
            out_specs=[
                pl.BlockSpec((t, WIN_BLK), lambda j, i, order_ref: (i, order_ref[j])),
                pl.BlockSpec((t, d), lambda j, i, order_ref: (jnp.where(j == 0, i, n_t - 1), 0)),
            ]
            + [ANY] * n_arr,
            scratch_shapes=[
                pltpu.VMEM((2,) + shards[0].shape, BF16),
                pltpu.VMEM((n_t, t, d), BF16),
                pltpu.SemaphoreType.DMA((n_arr, 7)),
                pltpu.SemaphoreType.DMA((n_arr, 7)),
                pltpu.SemaphoreType.DMA((n_arr,)),
                pltpu.SemaphoreType.DMA((2,)),
            ],
        ),
        out_shape=[jax.ShapeDtypeStruct((s, IN_WIDTH), BF16), jax.ShapeDtypeStruct((s, d), BF16)]
        + [jax.ShapeDtypeStruct((N_DEV,) + a.shape, a.dtype) for a in shards],
        compiler_params=_params("arbitrary", "arbitrary"),
    )(order, x_in, norm_pre, *shards)
    return res[0], res[1], res[2:]


def _sigmoid(a):
    return jax.nn.sigmoid(a)


def _dsilu(a, sg):
    return sg * (1.0 + a * (1.0 - sg))


def _rms_fwd(u, gain):
    r = lax.rsqrt(_rowmean(u * u) + EPS)
    n = u * r
    return r, n, n * gain


def _rms_bwd(dy, gain, r, n):
    dn = dy * gain
    return _colsum(dy * n), r * (dn - n * _rowmean(dn * n))


def _mix(proj, x, target, kv, kv_t, wout, wout_t, pool_w, pool_w_t, pool_scale, ln_g, ln_b, sgu_bias, sgu_wm, sgu_wm_t, branch_norm, norm_post):
    s, d = x.shape
    t = min(TILE_MIX, s)
    n_tiles = s // t
    n_chunks = t // SGU_CHUNK
    halo_blocks_per_tile = t // HALO
    inv_d = 1.0 / d
    scale = 1.0 / (XATTN_HEAD_DIM**0.5)

    def body(
        proj_ref, halo_ref, x_ref, tgt_ref, kv_ref, kvt_ref, wout_ref, wout_t_ref, pw_ref, pwt_ref, pscale_ref, lng_ref,
        lnb_ref, bias_ref, wm_ref, wmt_ref, bnorm_ref, gpost_ref,
        y_ref, dout_ref, dxo_ref, dproj_ref, loss_ref, dgpost_ref, dbnorm_ref, dpscale_ref, dlng_ref, dlnb_ref,
        dpw_out, dwm_out, dbias_ref, dkv_out,
        carry_ref, dzsum_ref, dpw_ref, dwm_ref, dkv_ref,
    ):
        i = pl.program_id(0)
        tile = n_tiles - 1 - i

        @pl.when(i == 0)
        def _():
            carry_ref[...] = jnp.zeros_like(carry_ref)
            dzsum_ref[...] = jnp.zeros_like(dzsum_ref)
            for ref in (loss_ref, dgpost_ref, dbnorm_ref, dpscale_ref, dlng_ref, dlnb_ref, dpw_ref, dwm_ref, dkv_ref):
                ref[...] = jnp.zeros_like(ref)

        t_glob = tile * t + lax.broadcasted_iota(jnp.int32, (t, 1), 0)
        inv_cnt = [1.0 / jnp.minimum(t_glob + 1, w).astype(F32) for w in POOL_WINDOWS]

        xa = proj_ref[:, 0:BRANCH].astype(F32)
        ga = proj_ref[:, BRANCH : 2 * BRANCH].astype(F32)
        halo = jnp.where(tile == 0, 0.0, halo_ref[...].astype(F32))
        d_bf, pm_parts = [], []
        for g, w in enumerate(POOL_WINDOWS):
            cols = slice(g * POOL_GROUP_DIM, (g + 1) * POOL_GROUP_DIM)
            acc = jnp.concatenate([halo[:, cols], xa[:, cols]], axis=0)
            k = 1
            while k < w:
                acc = acc + pltpu.roll(acc, k, axis=0)
                k *= 2
            dg = (acc[HALO:, :] * inv_cnt[g] - xa[:, cols]).astype(BF16)
            d_bf.append(dg)
            pm_parts.append(_dot(dg, pw_ref[g]))
        pm = jnp.concatenate(pm_parts, axis=1)
        pscale = pscale_ref[...]
        pa = pm * pscale
        sga = _sigmoid(ga)
        sila = ga * sga
        ua = pa * sila
        g_a = bnorm_ref[:, 0:BRANCH]
        ra, na, ya = _rms_fwd(ua, g_a)

        u = proj_ref[:, 2 * BRANCH : 3 * BRANCH].astype(F32)
        v = proj_ref[:, 3 * BRANCH : 4 * BRANCH].astype(F32)
        gb = proj_ref[:, 4 * BRANCH : 5 * BRANCH].astype(F32)
        lng = lng_ref[...]
        vc = v - _rowmean(v)
        rstd = lax.rsqrt(_rowmean(vc * vc) + EPS)
        vhat = vc * rstd
        vn_bf = (vhat * lng + lnb_ref[...]).astype(BF16)
        z_rows = []
        for c in range(n_chunks):
            rows = slice(c * SGU_CHUNK, (c + 1) * SGU_CHUNK)
            z_rows.append(
                jnp.concatenate(
                    [
                        _dot(wm_ref[hd], vn_bf[rows, hd * SGU_HEAD_DIM : (hd + 1) * SGU_HEAD_DIM])
                        for hd in range(SGU_HEADS)
                    ],
                    axis=1,
                )
                + bias_ref[...]
            )
        z = z_rows[0] if n_chunks == 1 else jnp.concatenate(z_rows, axis=0)
        sb = u * z
        sgb = _sigmoid(gb)
        silb = gb * sgb
        ub = sb * silb
        g_b = bnorm_ref[:, BRANCH : 2 * BRANCH]
        rb, nb, yb = _rms_fwd(ub, g_b)

        q = proj_ref[:, 5 * BRANCH : 6 * BRANCH]
        gc = proj_ref[:, 6 * BRANCH : 7 * BRANCH].astype(F32)
        q_bf, p_bf, o_parts = [], [], []
        for hd in range(XATTN_HEADS):
            cols = slice(hd * XATTN_HEAD_DIM, (hd + 1) * XATTN_HEAD_DIM)
            qh = q[:, cols]
            sc = _dot(qh, kvt_ref[cols, :]) * scale
            e = jnp.exp(sc - jnp.max(sc, axis=-1, keepdims=True))
            p = e / jnp.sum(e, axis=-1, keepdims=True)
            q_bf.append(qh)
            p_bf.append(p.astype(BF16))
            o_parts.append(_dot(p_bf[hd], kv_ref[:, BRANCH + hd * XATTN_HEAD_DIM : BRANCH + (hd + 1) * XATTN_HEAD_DIM]))
        o = jnp.concatenate(o_parts, axis=1)
        sgc = _sigmoid(gc)
        silc = gc * sgc
        uc = o * silc
        g_c = bnorm_ref[:, 2 * BRANCH : 3 * BRANCH]
        rc, nc, yc = _rms_fwd(uc, g_c)

        out = None
        for b, y_branch in enumerate((ya, yb, yc)):
            rows = slice(b * BRANCH, (b + 1) * BRANCH)
            y_bf = y_branch.astype(BF16)
            y_ref[:, rows] = y_bf
            part = _dot(y_bf, wout_ref[rows, :])
            out = part if out is None else out + part
        gpost = gpost_ref[...]
        r_out = lax.rsqrt(_rowmean(out * out) + EPS)
        on = out * r_out
        err = x_ref[...] + on * gpost - tgt_ref[...]
        loss_ref[...] += 0.5 * jnp.sum(_rowmean(err * err), axis=0, keepdims=True)

        dxo = err * inv_d
        dxo_ref[...] = dxo
        dgp, dout = _rms_bwd(dxo, gpost, r_out, on)
        dgpost_ref[...] += dgp
        dout_bf = dout.astype(BF16)
        dout_ref[...] = dout_bf
        dy = [_dot(dout_bf, wout_t_ref[:, b * BRANCH : (b + 1) * BRANCH]) for b in range(3)]

        dg_a, dua = _rms_bwd(dy[0], g_a, ra, na)
        dg_b, dub = _rms_bwd(dy[1], g_b, rb, nb)
        dg_c, duc = _rms_bwd(dy[2], g_c, rc, nc)
        dbnorm_ref[...] += jnp.concatenate([dg_a, dg_b, dg_c], axis=1)

        dpa = dua * sila
        dga = dua * pa * _dsilu(ga, sga)
        dpscale_ref[...] += _colsum(dpa * pm)
        dpm = dpa * pscale
        dxa_parts, carry_parts = [], []
        for g, w in enumerate(POOL_WINDOWS):
            cols = slice(g * POOL_GROUP_DIM, (g + 1) * POOL_GROUP_DIM)
            dpm_g = dpm[:, cols].astype(BF16)
            dd = _dot(dpm_g, pwt_ref[g])
            dpw_ref[g] += _dot(d_bf[g], dpm_g, TN)
            cg = dd * inv_cnt[g]
            carry_parts.append(cg[0:HALO, :])
            acc = jnp.concatenate([cg, carry_ref[:, cols]], axis=0)
            k = 1
            while k < w:
                acc = acc + pltpu.roll(acc, t + HALO - k, axis=0)
                k *= 2
            dxa_parts.append(acc[0:t, :] - dd)
        carry_ref[...] = jnp.concatenate(carry_parts, axis=1)
        dxa = jnp.concatenate(dxa_parts, axis=1)

        dsb = dub * silb
        dgb = dub * sb * _dsilu(gb, sgb)
        du = dsb * z
        dz = dsb * u
        dz_bf = dz.astype(BF16)
        dvn_rows = []
        dz_sum = None
        for c in range(n_chunks):
            rows = slice(c * SGU_CHUNK, (c + 1) * SGU_CHUNK)
            dz_sum = dz[rows, :] if dz_sum is None else dz_sum + dz[rows, :]
            parts = []
            for hd in range(SGU_HEADS):
                cols = slice(hd * SGU_HEAD_DIM, (hd + 1) * SGU_HEAD_DIM)
                parts.append(_dot(wmt_ref[hd], dz_bf[rows, cols]))
                dwm_ref[hd] += _dot(dz_bf[rows, cols], vn_bf[rows, cols], NT)
            dvn_rows.append(jnp.concatenate(parts, axis=1))
        dzsum_ref[...] += dz_sum
        dvn = dvn_rows[0] if n_chunks == 1 else jnp.concatenate(dvn_rows, axis=0)
        dlng_ref[...] += _colsum(dvn * vhat)
        dlnb_ref[...] += _colsum(dvn)
        dvh = dvn * lng
        dv = rstd * (dvh - _rowmean(dvh) - vhat * _rowmean(dvh * vhat))

        do = duc * silc
        dgc = duc * o * _dsilu(gc, sgc)
        dq_parts = []
        for hd in range(XATTN_HEADS):
            cols = slice(hd * XATTN_HEAD_DIM, (hd + 1) * XATTN_HEAD_DIM)
            vcols = slice(BRANCH + hd * XATTN_HEAD_DIM, BRANCH + (hd + 1) * XATTN_HEAD_DIM)
            do_h = do[:, cols].astype(BF16)
            p = p_bf[hd].astype(F32)
            dp = _dot(do_h, kvt_ref[vcols, :])
            dkv_ref[:, vcols] += _dot(p_bf[hd], do_h, TN)
            ds_bf = (p * (dp - jnp.sum(dp * p, axis=-1, keepdims=True)) * scale).astype(BF16)
            dq_parts.append(_dot(ds_bf, kv_ref[:, cols]))
            dkv_ref[:, cols] += _dot(ds_bf, q_bf[hd], TN)
        dq = jnp.concatenate(dq_parts, axis=1)

        dproj_ref[...] = jnp.concatenate([dxa, dga, du, dv, dgb, dq, dgc], axis=1).astype(BF16)

        @pl.when(i == n_tiles - 1)
        def _():
            keep = lax.broadcasted_iota(jnp.int32, (SGU_CHUNK, SGU_CHUNK), 0) >= lax.broadcasted_iota(
                jnp.int32, (SGU_CHUNK, SGU_CHUNK), 1
            )
            for hd in range(SGU_HEADS):
                dwm_ref[hd] = jnp.where(keep, dwm_ref[hd], 0.0)
                per_pos = dzsum_ref[:, hd * SGU_HEAD_DIM : (hd + 1) * SGU_HEAD_DIM]
                dbias_ref[hd : hd + 1, :] = _colsum(per_pos.T)
            for acc, res in ((dpw_ref, dpw_out), (dwm_ref, dwm_out), (dkv_ref, dkv_out)):
                pltpu.sync_copy(acc, res)

    row_tile = lambda width: pl.BlockSpec((t, width), lambda i: (n_tiles - 1 - i, 0))
    halo_spec = pl.BlockSpec(
        (HALO, BRANCH), lambda i: (jnp.maximum((n_tiles - 1 - i) * halo_blocks_per_tile - 1, 0), 0)
    )
    acc_shapes = [
        (1, 128),
        (1, d),
        (1, MIX_WIDTH),
        (1, BRANCH),
        (1, BRANCH),
        (1, BRANCH),
        pool_w.shape,
        sgu_wm.shape,
        (SGU_HEADS, SGU_CHUNK),
        kv.shape,
    ]
    return pl.pallas_call(
        body,
        name="mix",
        grid=(n_tiles,),
        in_specs=[
            row_tile(IN_WIDTH), halo_spec, row_tile(d), row_tile(d), _resident(kv.shape), _resident(kv_t.shape),
            _resident(wout.shape), _resident(wout_t.shape), _resident(pool_w.shape), _resident(pool_w_t.shape),
            _full((1, BRANCH)), _full((1, BRANCH)), _full((1, BRANCH)), _resident((SGU_CHUNK, BRANCH)),
            _resident(sgu_wm.shape), _resident(sgu_wm_t.shape), _full((1, MIX_WIDTH)), _full((1, d)),
        ],
        out_specs=[row_tile(MIX_WIDTH), row_tile(d), row_tile(d), row_tile(IN_WIDTH)]
        + [ANY if len(a) == 3 or a == kv.shape else _full(a) for a in acc_shapes],
        out_shape=[
            jax.ShapeDtypeStruct((s, MIX_WIDTH), BF16),
            jax.ShapeDtypeStruct((s, d), BF16),
            jax.ShapeDtypeStruct((s, d), F32),
            jax.ShapeDtypeStruct((s, IN_WIDTH), BF16),
        ]
        + [jax.ShapeDtypeStruct(a, F32) for a in acc_shapes],
        scratch_shapes=[
            pltpu.VMEM((HALO, BRANCH), F32), pltpu.VMEM((SGU_CHUNK, BRANCH), F32), pltpu.VMEM(pool_w.shape, F32),
            pltpu.VMEM(sgu_wm.shape, F32), pltpu.VMEM(kv.shape, F32),
        ],
        compiler_params=_params("arbitrary", vmem_limit_bytes=VMEM_LIMIT_MIX_BYTES),
    )(
        proj, proj, x, target, kv, kv_t, wout, wout_t, pool_w, pool_w_t, pool_scale, ln_g, ln_b, sgu_bias, sgu_wm,
        sgu_wm_t, branch_norm, norm_post,
    )


def _position():
    return lax.axis_index("x"), lax.axis_index("y"), lax.axis_index("c")


N_CHIPS = 4


def _landing_shape(kind, a):
    return (N_CHIPS,) + a.shape[2:] if kind == "pair" else a.shape


def _carry_specs(groups):
    arrays = [(kind, a) for kind, arrs in groups for a in arrs]
    scratch = []
    for _, arrs in groups:
        n = len(arrs)
        scratch += [pltpu.SemaphoreType.DMA((n, N_DEV)), pltpu.SemaphoreType.DMA((n, N_DEV)), pltpu.SemaphoreType.DMA((n,))]
    return dict(
        n=len(arrays),
        operands=[a for _, a in arrays],
        in_specs=[ANY] * len(arrays),
        out_specs=[ANY] * len(arrays),
        out_shape=[jax.ShapeDtypeStruct(_landing_shape(kind, a), a.dtype) for kind, a in arrays],
        scratch_shapes=scratch,
    )


def _carry(groups, src, out, sems):
    x, y, c = _position()
    chip = 2 * x + y
    me = 2 * chip + c

    def remote(s, d, send_sems, recv_sems, a, m, to):
        return pltpu.make_async_remote_copy(
            src_ref=s, dst_ref=d, send_sem=send_sems.at[a, m], recv_sem=recv_sems.at[a, m], device_id=to,
            device_id_type=MESH,
        )

    def copies():
        far, near = [], []
        at = 0
        for g, (kind, arrs) in enumerate(groups):
            send_sems, recv_sems, local_sems = sems[3 * g : 3 * g + 3]
            for a in range(len(arrs)):
                s, d = src[at + a], out[at + a]
                if kind == "pair":
                    far.append(remote(s.at[:, 1 - c], d, send_sems, recv_sems, a, 1, (x, y, 1 - c)))
                elif kind == "chips":
                    for m in range(1, N_CHIPS):
                        px, py = x ^ (m >> 1), y ^ (m & 1)
                        far.append(remote(s.at[2 * px + py], d.at[chip], send_sems, recv_sems, a, m, (px, py, c)))
                    near.append(pltpu.make_async_copy(s.at[chip], d.at[chip], local_sems.at[a]))
                else:
                    for m in range(1, N_DEV):
                        px, py, pc = x ^ ((m >> 2) & 1), y ^ ((m >> 1) & 1), c ^ (m & 1)
                        far.append(
                            remote(s.at[4 * px + 2 * py + pc], d.at[me], send_sems, recv_sems, a, m, (px, py, pc))
                        )
                    near.append(pltpu.make_async_copy(s.at[me], d.at[me], local_sems.at[a]))
            at += len(arrs)
        return far, near

    def start():
        far, near = copies()
        for cp in near + far:
            cp.start()

    def finish():
        far, near = copies()
        for cp in far:
            cp.wait_recv()
        for cp in far:
            cp.wait_send()
        for cp in near:
            cp.wait()

    return start, finish


def _exchange(groups, name):
    carried = _carry_specs(groups)
    n_c = carried["n"]

    def body(*refs):
        start, finish = _carry(groups, refs[:n_c], refs[n_c : 2 * n_c], refs[2 * n_c :])
        start()
        finish()

    return pl.pallas_call(
        body,
        name=name,
        in_specs=carried["in_specs"],
        out_specs=carried["out_specs"],
        out_shape=carried["out_shape"],
        scratch_shapes=carried["scratch_shapes"],
    )(*carried["operands"])


def _pair_sum(mine, theirs, name, groups=()):
    n = len(mine)
    carried = _carry_specs(groups)
    n_c = carried["n"]
    core = lax.axis_index("c").astype(jnp.int32).reshape(1)

    def body(core_ref, *refs):
        del core_ref
        own = refs[:n]
        sib = refs[n : 2 * n]
        src = refs[2 * n : 2 * n + n_c]
        out = refs[2 * n + n_c : 3 * n + n_c]
        landed = refs[3 * n + n_c : 3 * n + 2 * n_c]
        start, finish = _carry(groups, src, landed, refs[3 * n + 2 * n_c :])
        b = pl.program_id(0)

        @pl.when(b == 0)
        def _():
            start()

        for a in range(n):
            out[a][...] = (own[a][...].astype(F32) + sib[a][...].astype(F32)).astype(out[a].dtype)

        @pl.when(b == N_CHIPS - 1)
        def _():
            finish()

    block = lambda a: pl.BlockSpec((None,) + a.shape[1:], lambda b, core_ref: (b, 0, 0))
    res = pl.pallas_call(
        body,
        name=name,
        grid_spec=pltpu.PrefetchScalarGridSpec(
            num_scalar_prefetch=1,
            grid=(N_CHIPS,),
            in_specs=[pl.BlockSpec((None, None) + a.shape[2:], lambda b, core_ref: (b, core_ref[0], 0, 0)) for a in mine]
            + [block(a) for a in theirs]
            + carried["in_specs"],
            out_specs=[block(a) for a in theirs] + carried["out_specs"],
            scratch_shapes=carried["scratch_shapes"],
        ),
        out_shape=[jax.ShapeDtypeStruct(a.shape, a.dtype) for a in theirs] + carried["out_shape"],
        compiler_params=_params("arbitrary"),
    )(core, *mine, *theirs, *carried["operands"])
    return res[:n], res[n:]


def _weight_grad(a, b, n_blk, blocked, name, groups):
    s = a.shape[0]
    t = min(TILE_GRAD, s)
    n_t = s // t
    n_pairs = n_blk // 2
    if blocked == "cols":
        k, c = a.shape[1], b.shape[1] // n_blk
        a_spec = pl.BlockSpec((t, k), lambda j, i: (i, 0))
        b_spec = pl.BlockSpec((t, 2 * c), lambda j, i: (i, j))
        acc_shape = (k, 2 * c)
    else:
        k, c = a.shape[1] // n_blk, b.shape[1]
        a_spec = pl.BlockSpec((t, 2 * k), lambda j, i: (i, j))
        b_spec = pl.BlockSpec((t, c), lambda j, i: (i, 0))
        acc_shape = (2 * k, c)
    carried = _carry_specs(groups)
    n_p = carried["n"]

    def body(a_ref, b_ref, *refs):
        src = refs[:n_p]
        o_ref, theirs_ref = refs[n_p : n_p + 2]
        landed = refs[n_p + 2 : 2 * n_p + 2]
        acc_ref, sbuf, pair_send, pair_recv = refs[2 * n_p + 2 : 2 * n_p + 6]
        start, finish = _carry(groups, src, landed, refs[2 * n_p + 6 :])
        j = pl.program_id(0)
        i = pl.program_id(1)
        x, y, c_me = _position()

        def to_sibling(pair):
            return pltpu.make_async_remote_copy(
                src_ref=sbuf.at[1 - c_me], dst_ref=theirs_ref.at[pair], send_sem=pair_send.at[pair],
                recv_sem=pair_recv.at[pair], device_id=(x, y, 1 - c_me), device_id_type=MESH,
            )

        @pl.when(jnp.logical_and(j == 0, i == 0))
        def _():
            start()

        @pl.when(i == 0)
        def _():
            acc_ref[...] = jnp.zeros_like(acc_ref)

        acc_ref[...] += _dot(a_ref[...], b_ref[...], TN)

        @pl.when(i == n_t - 1)
        def _():
            for pair in range(1, n_pairs):

                @pl.when(j == pair)
                def _():
                    to_sibling(pair - 1).wait_send()

            for half in range(2):
                if blocked == "cols":
                    block = acc_ref[:, half * c : (half + 1) * c].astype(BF16)
                else:
                    block = acc_ref[half * k : (half + 1) * k, :].astype(BF16)
                o_ref[half] = block
                sbuf[half] = block
            for pair in range(n_pairs):

                @pl.when(j == pair)
                def _():
                    to_sibling(pair).start()

        @pl.when(jnp.logical_and(j == n_pairs - 1, i == n_t - 1))
        def _():
            to_sibling(n_pairs - 1).wait_send()
            for pair in range(n_pairs):
                to_sibling(pair).wait_recv()
            finish()

    res = pl.pallas_call(
        body,
        name=name,
        grid=(n_pairs, n_t),
        in_specs=[a_spec, b_spec] + carried["in_specs"],
        out_specs=[pl.BlockSpec((2, k, c), lambda j, i: (j, 0, 0)), ANY] + carried["out_specs"],
        out_shape=[jax.ShapeDtypeStruct((n_blk, k, c), BF16), jax.ShapeDtypeStruct((n_pairs, k, c), BF16)]
        + carried["out_shape"],
        scratch_shapes=[
            pltpu.VMEM(acc_shape, F32), pltpu.VMEM((2, k, c), BF16), pltpu.SemaphoreType.DMA((n_pairs,)),
            pltpu.SemaphoreType.DMA((n_pairs,)),
        ]
        + carried["scratch_shapes"],
        compiler_params=_params("arbitrary", "arbitrary"),
    )(a, b, *carried["operands"])
    return res[0], res[1], res[2:]


def _input_grad(dproj, win_t, x, dxo, norm_pre, groups):
    s, d = x.shape
    t = min(TILE_GRAD, s)
    n_t = s // t
    kb = BRANCH
    n_k = win_t.shape[0] // kb
    carried = _carry_specs(groups)
    n_p = carried["n"]

    def body(dp_ref, w_ref, x_ref, dxo_ref, g_ref, *refs):
        src = refs[:n_p]
        gx_ref, dg_ref = refs[n_p : n_p + 2]
        landed = refs[n_p + 2 : 2 * n_p + 2]
        acc_ref = refs[2 * n_p + 2]
        start, finish = _carry(groups, src, landed, refs[2 * n_p + 3 :])
        i = pl.program_id(0)
        j = pl.program_id(1)

        @pl.when(jnp.logical_and(i == 0, j == 0))
        def _():
            start()
            dg_ref[...] = jnp.zeros_like(dg_ref)

        @pl.when(j == 0)
        def _():
            acc_ref[...] = jnp.zeros_like(acc_ref)

        acc_ref[...] += _dot(dp_ref[...], w_ref[...])

        @pl.when(j == n_k - 1)
        def _():
            xv = x_ref[...]
            gain = g_ref[...]
            r = lax.rsqrt(_rowmean(xv * xv) + EPS)
            dgain, dx = _rms_bwd(acc_ref[...], gain, r, xv * r)
            dg_ref[...] += dgain
            gx_ref[...] = dxo_ref[...] + dx

        @pl.when(jnp.logical_and(i == n_t - 1, j == n_k - 1))
        def _():
            finish()

    res = pl.pallas_call(
        body,
        name="input_grad",
        grid=(n_t, n_k),
        in_specs=[
            pl.BlockSpec((t, kb), lambda i, j: (i, j)),
            pl.BlockSpec((kb, d), lambda i, j: (j, 0)),
            pl.BlockSpec((t, d), lambda i, j: (i, 0)),
            pl.BlockSpec((t, d), lambda i, j: (i, 0)),
            _full((1, d)),
        ]
        + carried["in_specs"],
        out_specs=[pl.BlockSpec((t, d), lambda i, j: (i, 0)), _full((1, d))] + carried["out_specs"],
        out_shape=[jax.ShapeDtypeStruct((s, d), F32), jax.ShapeDtypeStruct((1, d), F32)] + carried["out_shape"],
        scratch_shapes=[pltpu.VMEM((t, d), F32)] + carried["scratch_shapes"],
        compiler_params=_params("arbitrary", "arbitrary"),
    )(dproj, win_t, x, dxo, norm_pre, *carried["operands"])
    return res[0], res[1], res[2:]


def _kv_backward(dkv, memn, wkv, mem):
    m, d = mem.shape
    n = wkv.shape[1]

    def body(dkv_ref, memn_ref, w_ref, mem_ref, mem_blk_ref, gw_ref, dg_ref):
        dkv_bf = dkv_ref[...].astype(BF16)
        gw_ref[...] = _dot(memn_ref[...], dkv_bf, TN).astype(BF16)
        dmemn = _dot(dkv_bf, w_ref[...], NT)
        mv = mem_ref[...]
        r = lax.rsqrt(_rowmean(mv * mv) + EPS)
        dg_ref[...] = _colsum(dmemn * (mem_blk_ref[...] * r))

    cols = pl.BlockSpec((m, WKV_BLK), lambda j: (0, j))
    return pl.pallas_call(
        body,
        name="kv_backward",
        grid=(N_DEV,),
        in_specs=[_full((m, n)), cols, pl.BlockSpec((WKV_BLK, n), lambda j: (j, 0)), _full((m, d)), cols],
        out_specs=[pl.BlockSpec((None, WKV_BLK, n), lambda j: (j, 0, 0)), pl.BlockSpec((1, WKV_BLK), lambda j: (0, j))],
        out_shape=[jax.ShapeDtypeStruct((N_DEV, WKV_BLK, n), BF16), jax.ShapeDtypeStruct((1, d), F32)],
        compiler_params=_params("arbitrary"),
    )(dkv, memn, wkv, mem, mem)


def _adamw_math(w, g, m, v):
    m = ADAM_B1 * m + (1.0 - ADAM_B1) * g
    v = ADAM_B2 * v + (1.0 - ADAM_B2) * (g * g)
    m_hat = m / (1.0 - ADAM_B1**ADAM_STEP)
    v_hat = v / (1.0 - ADAM_B2**ADAM_STEP)
    delta = -ADAM_LR * (m_hat / (jnp.sqrt(v_hat) + ADAM_EPS) + ADAM_WD * w)
    return delta, m, v


def _adamw(parts, w, m, v, name):
    r, c = w.shape
    slots = parts.shape[0]
    t = r
    while t * c * 4 > TILE_ADAM_BYTES and t % 16 == 0:
        t //= 2

    def body(p_ref, w_ref, m_ref, v_ref, g_ref, d_ref, nm_ref, nv_ref):
        g = p_ref[0].astype(F32)
        for k in range(1, slots):
            g = g + p_ref[k].astype(F32)
        delta, nm, nv = _adamw_math(w_ref[...], g, m_ref[...], v_ref[...])
        g_ref[...] = g
        d_ref[...] = delta
        nm_ref[...] = nm
        nv_ref[...] = nv

    tile = pl.BlockSpec((t, c), lambda i: (i, 0))
    return pl.pallas_call(
        body,
        name=name,
        grid=(r // t,),
        in_specs=[pl.BlockSpec((slots, t, c), lambda i: (0, i, 0)), tile, tile, tile],
        out_specs=[tile] * 4,
        out_shape=[jax.ShapeDtypeStruct((r, c), F32)] * 4,
        compiler_params=_params("parallel"),
    )(parts, w, m, v)


def _adamw_packed(parts, triples, name):
    slots = parts.shape[0]
    sizes = [w.shape[0] for w, _, _ in triples]
    rest = parts.shape[1] - sum(sizes)

    def total(p_ref, at, rows):
        g = p_ref[0, at : at + rows, :]
        for k in range(1, slots):
            g = g + p_ref[k, at : at + rows, :]
        return g

    def body(p_ref, *refs):
        ins = refs[: 3 * len(triples)]
        outs = refs[3 * len(triples) :]
        at = 0
        for n, rows in enumerate(sizes):
            g = total(p_ref, at, rows)
            w_ref, m_ref, v_ref = ins[3 * n : 3 * n + 3]
            delta, nm, nv = _adamw_math(w_ref[...], g, m_ref[...], v_ref[...])
            for ref, val in zip(outs[4 * n : 4 * n + 4], (g, delta, nm, nv)):
                ref[...] = val
            at += rows
        if rest:
            outs[-1][...] = total(p_ref, at, rest)

    flat = [a for t in triples for a in t]
    res = pl.pallas_call(
        body,
        name=name,
        out_shape=[jax.ShapeDtypeStruct(w.shape, F32) for w, _, _ in triples for _ in range(4)]
        + ([jax.ShapeDtypeStruct((rest, 128), F32)] if rest else []),
        compiler_params=pltpu.CompilerParams(vmem_limit_bytes=VMEM_LIMIT_BYTES),
    )(parts, *flat)
    return [res[4 * n : 4 * n + 4] for n in range(len(triples))], (res[-1] if rest else None)


SMALL = ("norm_pre", "pool_scale", "sgu_ln_g", "sgu_ln_b", "sgu_w", "sgu_b", "mem_norm", "branch_norm", "norm_post")


def _local_view(name, w):
    if name == "sgu_w":
        return w.reshape(SGU_HEADS, SGU_CHUNK, SGU_CHUNK)
    if name == "sgu_b":
        return w.reshape(SGU_HEADS, SGU_CHUNK)
    return w.reshape(1, -1)


def _forward_backward(x, mem, target, shards, small):
    causal = jnp.tril(jnp.ones((SGU_CHUNK, SGU_CHUNK), dtype=bool))
    sgu_wm = jnp.where(causal[None], small["sgu_w"], 0.0).astype(BF16)
    sgu_bias = jnp.repeat(jnp.transpose(small["sgu_b"]), SGU_HEAD_DIM, axis=1)

    proj, h, (win, wkv, pool_all, wout) = _proj_gather(x, small["norm_pre"], shards)
    wout = wout.reshape(MIX_WIDTH, D_MODEL)
    wkv = wkv.reshape(D_MODEL, 2 * BRANCH)
    pool_full = (
        pool_all.reshape(N_DEV, len(POOL_WINDOWS), POOL_BLK, POOL_GROUP_DIM)
        .transpose(1, 0, 2, 3)
        .reshape(len(POOL_WINDOWS), POOL_GROUP_DIM, POOL_GROUP_DIM)
    )
    memn, kv = _kv_forward(mem, small["mem_norm"], wkv)
    (y, dout, dxo, dproj, loss, d_norm_post, d_branch_norm, d_pool_scale, d_ln_g, d_ln_b, d_pool_w, d_sgu_w, d_sgu_b,
     dkv) = _mix(
        proj, x, target, kv, kv.T, wout, wout.T, pool_full, jnp.swapaxes(pool_full, 1, 2), small["pool_scale"],
        small["sgu_ln_g"], small["sgu_ln_b"], sgu_bias, sgu_wm, jnp.swapaxes(sgu_wm, 1, 2), small["branch_norm"],
        small["norm_post"],
    )
    g_wkv, d_mem_norm = _kv_backward(dkv, memn, wkv, mem)
    g_pool = (
        d_pool_w.reshape(len(POOL_WINDOWS), N_DEV, POOL_BLK, POOL_GROUP_DIM)
        .transpose(1, 0, 2, 3)
        .reshape(N_DEV, len(POOL_WINDOWS) * POOL_BLK, POOL_GROUP_DIM)
        .astype(BF16)
    )
    small_grads = dict(
        pool_scale=d_pool_scale, sgu_ln_g=d_ln_g, sgu_ln_b=d_ln_b, sgu_w=d_sgu_w, sgu_b=d_sgu_b,
        mem_norm=d_mem_norm, branch_norm=d_branch_norm, norm_post=d_norm_post,
    )
    packed = jnp.concatenate(
        [small_grads[n].reshape(-1, 128) for n in SMALL if n != "norm_pre"] + [jnp.broadcast_to(loss, (8, 128))], axis=0
    )
    packed = jnp.broadcast_to(packed[None, None], (N_CHIPS, 2) + packed.shape)

    by_chip = lambda g: g.reshape((N_CHIPS, 2) + g.shape[1:])
    small_mine = [by_chip(g_wkv), by_chip(g_pool), packed]
    g_wout, wout_theirs, small_theirs = _weight_grad(y, dout, N_DEV, "rows", "grad_w_out", [("pair", small_mine)])
    sums, _ = _pair_sum(small_mine + [by_chip(g_wout)], list(small_theirs) + [wout_theirs], "pair_sum_first")
    g_win, win_theirs, (l_wkv, l_pool, l_packed, l_wout) = _weight_grad(
        h, dproj, N_DEV, "cols", "grad_w_in", [("chips", list(sums))]
    )
    (win_sums,), _ = _pair_sum([by_chip(g_win)], [win_theirs], "pair_sum_w_in")
    grad_x, d_norm_pre, (l_win,) = _input_grad(
        dproj, jnp.swapaxes(win, 1, 2).reshape(IN_WIDTH, D_MODEL), x, dxo, small["norm_pre"],
        [("chips", [win_sums])],
    )
    return loss, grad_x, dict(w_in=l_win, w_out=l_wout, w_kv=l_wkv, pool_w=l_pool), l_packed, d_norm_pre


def kernel(x, mem, norm_pre, w_in, pool_w, pool_scale, sgu_ln_g, sgu_ln_b, sgu_w, sgu_b, mem_norm, w_kv, branch_norm, w_out, norm_post, loss_target, m_norm_pre, m_w_in, m_pool_w, m_pool_scale, m_sgu_ln_g, m_sgu_ln_b, m_sgu_w, m_sgu_b, m_mem_norm, m_w_kv, m_branch_norm, m_w_out, m_norm_post, v_norm_pre, v_w_in, v_pool_w, v_pool_scale, v_sgu_ln_g, v_sgu_ln_b, v_sgu_w, v_sgu_b, v_mem_norm, v_w_kv, v_branch_norm, v_w_out, v_norm_post):
    weights = dict(norm_pre=norm_pre, w_in=w_in, pool_w=pool_w, pool_scale=pool_scale, sgu_ln_g=sgu_ln_g, sgu_ln_b=sgu_ln_b, sgu_w=sgu_w, sgu_b=sgu_b, mem_norm=mem_norm, w_kv=w_kv, branch_norm=branch_norm, w_out=w_out, norm_post=norm_post)
    first = dict(norm_pre=m_norm_pre, w_in=m_w_in, pool_w=m_pool_w, pool_scale=m_pool_scale, sgu_ln_g=m_sgu_ln_g, sgu_ln_b=m_sgu_ln_b, sgu_w=m_sgu_w, sgu_b=m_sgu_b, mem_norm=m_mem_norm, w_kv=m_w_kv, branch_norm=m_branch_norm, w_out=m_w_out, norm_post=m_norm_post)
    second = dict(norm_pre=v_norm_pre, w_in=v_w_in, pool_w=v_pool_w, pool_scale=v_pool_scale, sgu_ln_g=v_sgu_ln_g, sgu_ln_b=v_sgu_ln_b, sgu_w=v_sgu_w, sgu_b=v_sgu_b, mem_norm=v_mem_norm, w_kv=v_w_kv, branch_norm=v_branch_norm, w_out=v_w_out, norm_post=v_norm_post)
    order = ("norm_pre", "w_in", "pool_w", "pool_scale", "sgu_ln_g", "sgu_ln_b", "sgu_w", "sgu_b", "mem_norm", "w_kv", "branch_norm", "w_out", "norm_post")

    owned_shape = dict(
        w_in=(D_MODEL, WIN_BLK), w_out=(WOUT_BLK, D_MODEL), w_kv=(WKV_BLK, 2 * BRANCH),
        pool_w=(len(POOL_WINDOWS) * POOL_BLK, POOL_GROUP_DIM),
    )
    owned = {n: weights[n].reshape(owned_shape[n]) for n in owned_shape}
    small = {n: _local_view(n, weights[n]) for n in SMALL}
    loss, grad_x, landed, landed_packed, d_norm_pre = _forward_backward(
        x[0], mem[0], loss_target[0], [owned[n].astype(BF16) for n in ("w_in", "w_kv", "pool_w", "w_out")], small
    )
    d_norm_pre = d_norm_pre.reshape(-1, 128)
    (landed_norm_pre,) = _exchange(
        [("all", [jnp.broadcast_to(d_norm_pre[None], (N_DEV,) + d_norm_pre.shape)])], "exchange_norm_pre"
    )

    grads, deltas, new_m, new_v = {}, {}, {}, {}
    for n in owned_shape:
        shape = weights[n].shape
        res = _adamw(
            landed[n], owned[n], first[n].reshape(owned_shape[n]), second[n].reshape(owned_shape[n]), "adamw_" + n
        )
        grads[n], deltas[n], new_m[n], new_v[n] = (a.reshape(shape) for a in res)
    rows_of = lambda tree, n: tree[n].reshape(-1, 128)
    for names, parts, name in (
        ([n for n in SMALL if n != "norm_pre"], landed_packed, "adamw_replicated"),
        (["norm_pre"], landed_norm_pre, "adamw_norm_pre"),
    ):
        res, rest = _adamw_packed(
            parts, [(rows_of(weights, n), rows_of(first, n), rows_of(second, n)) for n in names], name
        )
        if rest is not None:
            total = rest[0, 0]
        for n, four in zip(names, res):
            for tree, a in zip((grads, deltas, new_m, new_v), four):
                tree[n] = a.reshape(weights[n].shape)

    return (
        total,
        grad_x[None],
        *[grads[n] for n in order],
        *[deltas[n] for n in order],
        *[new_m[n] for n in order],
        *[new_v[n] for n in order],
    )
```

```python
import jax
import jax.numpy as jnp
from jax import lax
from jax.experimental import pallas as pl
from jax.experimental.pallas import tpu as pltpu

F32 = jnp.float32
BF16 = jnp.bfloat16
EPS = 1e-6

D_MODEL = 2048
POOL_WINDOWS = (2, 4, 8, 16)
POOL_GROUP_DIM = 256
BRANCH = 1024
SGU_CHUNK = 128
SGU_HEADS = 8
SGU_HEAD_DIM = 128
XATTN_HEADS = 4
XATTN_HEAD_DIM = 256
MIX_WIDTH = 3 * BRANCH
IN_WIDTH = 7 * BRANCH
N_DEV = 8
WIN_BLK = IN_WIDTH // N_DEV
WOUT_BLK = MIX_WIDTH // N_DEV
WKV_BLK = D_MODEL // N_DEV
POOL_BLK = POOL_GROUP_DIM // N_DEV
HALO = 16
LANES = 128
SUBLANES = 8

ADAM_LR = 0.001
ADAM_B1 = 0.9
ADAM_B2 = 0.999
ADAM_EPS = 1e-08
ADAM_WD = 0.01
ADAM_STEP = 10

VMEM_LIMIT_BYTES = 56 * 1024 * 1024
VMEM_LIMIT_MIX_BYTES = 63 * 1024 * 1024

TILE_PROJ = 512
TILE_MIX = 128
TILE_GRAD = 512
TILE_ADAM_BYTES = 1 << 20

ANY = pl.BlockSpec(memory_space=pl.ANY)
NN = (((1,), (0,)), ((), ()))
NT = (((1,), (1,)), ((), ()))
TN = (((0,), (0,)), ((), ()))
MESH = pl.DeviceIdType.MESH


def _dot(a, b, dims=NN):
    return lax.dot_general(a, b, dims, preferred_element_type=F32)


def _params(*semantics, vmem_limit_bytes=VMEM_LIMIT_BYTES):
    return pltpu.CompilerParams(dimension_semantics=semantics, vmem_limit_bytes=vmem_limit_bytes)


def _rowmean(a):
    return jnp.mean(a, axis=-1, keepdims=True)


def _colsum(a):
    return jnp.sum(a, axis=0, keepdims=True)


def _full(shape):
    zeros = (0,) * len(shape)
    return pl.BlockSpec(shape, lambda *_: zeros)


def _resident(shape):
    zeros = (0,) * len(shape)
    return pl.BlockSpec(shape, lambda *_: zeros, pipeline_mode=pl.Buffered(1))


def _kv_forward(mem, mem_norm, wkv):
    m, d = mem.shape
    n = wkv.shape[1]
    cols = 4 * LANES

    def body(mem_ref, g_ref, w_ref, memn_ref, kv_ref):
        mv = mem_ref[...]
        r = lax.rsqrt(_rowmean(mv * mv) + EPS)
        memn = (mv * r * g_ref[...]).astype(BF16)
        memn_ref[...] = memn
        kv_ref[...] = _dot(memn, w_ref[...]).astype(BF16)

    return pl.pallas_call(
        body,
        name="kv_forward",
        grid=(n // cols,),
        in_specs=[_full((m, d)), _full((1, d)), pl.BlockSpec((d, cols), lambda j: (0, j))],
        out_specs=[_full((m, d)), pl.BlockSpec((m, cols), lambda j: (0, j))],
        out_shape=[jax.ShapeDtypeStruct((m, d), BF16), jax.ShapeDtypeStruct((m, n), BF16)],
        compiler_params=_params("arbitrary"),
    )(mem, mem_norm, wkv)


def _proj_gather(x_in, norm_pre, shards):
    s, d = x_in.shape
    t = min(TILE_PROJ, s)
    n_t = s // t
    n_arr = len(shards)

    def places(x, y, c):
        return (x, y, c), (x, y, 1 - c), (x ^ c, y ^ (1 - c)), (x ^ (1 - c), y ^ c), (1 - x, 1 - y)

    def index(chip, core):
        return 4 * chip[0] + 2 * chip[1] + core

    _, _, chip_a, chip_b, chip_d = places(*_position())
    c_out = lax.axis_index("c")
    me_out = index((lax.axis_index("x"), lax.axis_index("y")), c_out)
    order = jnp.stack(
        [
            me_out, me_out ^ 1, index(chip_b, c_out), index(chip_a, 1 - c_out), index(chip_a, c_out),
            index(chip_b, 1 - c_out), index(chip_d, c_out), index(chip_d, 1 - c_out),
        ]
    ).astype(jnp.int32)

    def body(order_ref, x_ref, g_ref, *refs):
        del order_ref
        src = refs[:n_arr]
        proj_ref, h_ref = refs[n_arr : n_arr + 2]
        out = refs[n_arr + 2 : 2 * n_arr + 2]
        wbuf, hs, send_sems, recv_sems, local_sems, load_sems = refs[2 * n_arr + 2 :]
        j = pl.program_id(0)
        i = pl.program_id(1)
        me, sibling, chip_a, chip_b, chip_d = places(*_position())
        c = me[2]

        def block(a, chip, core):
            return out[a].at[index(chip, core)]

        def copy(a, k, owner, to, from_input=False):
            return pltpu.make_async_remote_copy(
                src_ref=src[a] if from_input else block(a, *owner),
                dst_ref=block(a, *owner),
                send_sem=send_sems.at[a, k],
                recv_sem=recv_sems.at[a, k],
                device_id=to,
                device_id_type=MESH,
            )

        mine = (me[:2], c)

        def own(a):
            return pltpu.make_async_copy(src[a], block(a, *mine), local_sems.at[a])

        def first_sends(a):
            return [
                copy(a, 0, mine, sibling, from_input=True),
                copy(a, 1, mine, (*chip_a, c), from_input=True),
                copy(a, 2, mine, (*chip_b, c), from_input=True),
            ]

        def onward(a, k):
            owner = {3: chip_a, 4: chip_a, 5: chip_b, 6: chip_d}[k]
            return copy(a, k, (owner, c), (*chip_b, c) if k == 3 else sibling)

        def landed(a, k):
            owner = {0: mine[0], 1: chip_a, 2: chip_b, 3: chip_d, 4: chip_b, 5: chip_a, 6: chip_d}[k]
            core = c if k in (1, 2, 3) else 1 - c
            copy(a, k, (owner, core), me).wait_recv()
            return owner, core

        def load(ref, step):
            return pltpu.make_async_copy(ref, wbuf.at[step % 2], load_sems.at[step % 2])

        @pl.when(jnp.logical_and(j == 0, i == 0))
        def _():
            own(0).start()
            first_sends(0)[0].start()
            first_sends(0)[2].start()
            load(src[0], 0).start()
            load(src[0], 0).wait()

        steps = {1: (0, ()), 2: (2, (5,)), 3: (5, ()), 4: (1, (3, 4)), 5: (4, ()), 6: (3, (6,)), 7: (6, ())}
        for step, (k, then) in steps.items():

            @pl.when(jnp.logical_and(j == step, i == 0))
            def _():
                load(src[0], step).wait()

            @pl.when(jnp.logical_and(j == step - 1, i == n_t - 1))
            def _():
                owner = landed(0, k)
                for k2 in then:
                    onward(0, k2).start()
                if step == 1:
                    first_sends(0)[1].start()
                if k == 1:
                    for a in range(1, n_arr):
                        own(a).start()
                        for cp in first_sends(a):
                            cp.start()
                if k == 3:
                    for a in range(1, n_arr):
                        for k1, then1 in ((1, (3, 4)), (2, (5,))):
                            landed(a, k1)
                            for k2 in then1:
                                onward(a, k2).start()
                load(block(0, *owner), step).start()

        @pl.when(j == 0)
        def _():
            xv = x_ref[...]
            h = (xv * lax.rsqrt(_rowmean(xv * xv) + EPS) * g_ref[...]).astype(BF16)
            hs[i] = h
            h_ref[...] = h

        proj_ref[...] = _dot(hs[i], wbuf[j % 2]).astype(BF16)

        @pl.when(jnp.logical_and(j == N_DEV - 1, i == n_t - 1))
        def _():
            for a in range(1, n_arr):
                landed(a, 3)
                onward(a, 6).start()
            for a in range(1, n_arr):
                for k in (0, 4, 5, 6):
                    landed(a, k)
            for a in range(n_arr):
                for cp in first_sends(a) + [onward(a, k) for k in (3, 4, 5, 6)]:
                    cp.wait_send()
                own(a).wait()

    res = pl.pallas_call(
        body,
        name="proj_gather",
        grid_spec=pltpu.PrefetchScalarGridSpec(
            num_scalar_prefetch=1,
            grid=(N_DEV, n_t),
            in_specs=[
                pl.BlockSpec((t, d), lambda j, i, order_ref: (jnp.where(j == 0, i, n_t - 1), 0)),
                pl.BlockSpec((1, d), lambda j, i, order_ref: (0, 0)),
            ]
            + [ANY] * n_arr,
            out_specs=[
                pl.BlockSpec((t, WIN_BLK), lambda j, i, order_ref: (i, order_ref[j])),
                pl.BlockSpec((t, d), lambda j, i, order_ref: (jnp.where(j == 0, i, n_t - 1), 0)),
            ]
            + [ANY] * n_arr,
            scratch_shapes=[
                pltpu.VMEM((2,) + shards[0].shape, BF16),
                pltpu.VMEM((n_t, t, d), BF16),
                pltpu.SemaphoreType.DMA((n_arr, 7)),
                pltpu.SemaphoreType.DMA((n_arr, 7)),
                pltpu.SemaphoreType.DMA((n_arr,)),
                pltpu.SemaphoreType.DMA((2,)),
            ],
        ),
        out_shape=[jax.ShapeDtypeStruct((s, IN_WIDTH), BF16), jax.ShapeDtypeStruct((s, d), BF16)]
        + [jax.ShapeDtypeStruct((N_DEV,) + a.shape, a.dtype) for a in shards],
        compiler_params=_params("arbitrary", "arbitrary"),
    )(order, x_in, norm_pre, *shards)
    return res[0], res[1], res[2:]


def _sigmoid(a):
    return jax.nn.sigmoid(a)


def _dsilu(a, sg):
    return sg * (1.0 + a * (1.0 - sg))


def _rms_fwd(u, gain):
    r = lax.rsqrt(_rowmean(u * u) + EPS)
    n = u * r
    return r, n, n * gain


def _rms_bwd(dy, gain, r, n):
    dn = dy * gain
    return _colsum(dy * n), r * (dn - n * _rowmean(dn * n))


def _mix(proj, x, target, kv, kv_t, wout, wout_t, pool_w, pool_w_t, pool_scale, ln_g, ln_b, sgu_bias, sgu_wm, sgu_wm_t, branch_norm, norm_post):
    s, d = x.shape
    t = min(TILE_MIX, s)
    n_tiles = s // t
    n_chunks = t // SGU_CHUNK
    halo_blocks_per_tile = t // HALO
    inv_d = 1.0 / d
    scale = 1.0 / (XATTN_HEAD_DIM**0.5)

    def body(
        proj_ref, halo_ref, x_ref, tgt_ref, kv_ref, kvt_ref, wout_ref, wout_t_ref, pw_ref, pwt_ref, pscale_ref, lng_ref,
        lnb_ref, bias_ref, wm_ref, wmt_ref, bnorm_ref, gpost_ref,
        y_ref, dout_ref, dxo_ref, dproj_ref, loss_ref, dgpost_ref, dbnorm_ref, dpscale_ref, dlng_ref, dlnb_ref,
        dpw_out, dwm_out, dbias_ref, dkv_out,
        carry_ref, dzsum_ref, dpw_ref, dwm_ref, dkv_ref,
    ):
        i = pl.program_id(0)
        tile = n_tiles - 1 - i

        @pl.when(i == 0)
        def _():
            carry_ref[...] = jnp.zeros_like(carry_ref)
            dzsum_ref[...] = jnp.zeros_like(dzsum_ref)
            for ref in (loss_ref, dgpost_ref, dbnorm_ref, dpscale_ref, dlng_ref, dlnb_ref, dpw_ref, dwm_ref, dkv_ref):
                ref[...] = jnp.zeros_like(ref)

        t_glob = tile * t + lax.broadcasted_iota(jnp.int32, (t, 1), 0)
        inv_cnt = [1.0 / jnp.minimum(t_glob + 1, w).astype(F32) for w in POOL_WINDOWS]

        xa = proj_ref[:, 0:BRANCH].astype(F32)
        ga = proj_ref[:, BRANCH : 2 * BRANCH].astype(F32)
        halo = jnp.where(tile == 0, 0.0, halo_ref[...].astype(F32))
        d_bf, pm_parts = [], []
        for g, w in enumerate(POOL_WINDOWS):
            cols = slice(g * POOL_GROUP_DIM, (g + 1) * POOL_GROUP_DIM)
            acc = jnp.concatenate([halo[:, cols], xa[:, cols]], axis=0)
            k = 1
            while k < w:
                acc = acc + pltpu.roll(acc, k, axis=0)
                k *= 2
            dg = (acc[HALO:, :] * inv_cnt[g] - xa[:, cols]).astype(BF16)
            d_bf.append(dg)
            pm_parts.append(_dot(dg, pw_ref[g]))
        pm = jnp.concatenate(pm_parts, axis=1)
        pscale = pscale_ref[...]
        pa = pm * pscale
        sga = _sigmoid(ga)
        sila = ga * sga
        ua = pa * sila
        g_a = bnorm_ref[:, 0:BRANCH]
        ra, na, ya = _rms_fwd(ua, g_a)

        u = proj_ref[:, 2 * BRANCH : 3 * BRANCH].astype(F32)
        v = proj_ref[:, 3 * BRANCH : 4 * BRANCH].astype(F32)
        gb = proj_ref[:, 4 * BRANCH : 5 * BRANCH].astype(F32)
        lng = lng_ref[...]
        vc = v - _rowmean(v)
        rstd = lax.rsqrt(_rowmean(vc * vc) + EPS)
        vhat = vc * rstd
        vn_bf = (vhat * lng + lnb_ref[...]).astype(BF16)
        z_rows = []
        for c in range(n_chunks):
            rows = slice(c * SGU_CHUNK, (c + 1) * SGU_CHUNK)
            z_rows.append(
                jnp.concatenate(
                    [
                        _dot(wm_ref[hd], vn_bf[rows, hd * SGU_HEAD_DIM : (hd + 1) * SGU_HEAD_DIM])
                        for hd in range(SGU_HEADS)
                    ],
                    axis=1,
                )
                + bias_ref[...]
            )
        z = z_rows[0] if n_chunks == 1 else jnp.concatenate(z_rows, axis=0)
        sb = u * z
        sgb = _sigmoid(gb)
        silb = gb * sgb
        ub = sb * silb
        g_b = bnorm_ref[:, BRANCH : 2 * BRANCH]
        rb, nb, yb = _rms_fwd(ub, g_b)

        q = proj_ref[:, 5 * BRANCH : 6 * BRANCH]
        gc = proj_ref[:, 6 * BRANCH : 7 * BRANCH].astype(F32)
        q_bf, p_bf, o_parts = [], [], []
        for hd in range(XATTN_HEADS):
            cols = slice(hd * XATTN_HEAD_DIM, (hd + 1) * XATTN_HEAD_DIM)
            qh = q[:, cols]
            sc = _dot(qh, kvt_ref[cols, :]) * scale
            e = jnp.exp(sc - jnp.max(sc, axis=-1, keepdims=True))
            p = e / jnp.sum(e, axis=-1, keepdims=True)
            q_bf.append(qh)
            p_bf.append(p.astype(BF16))
            o_parts.append(_dot(p_bf[hd], kv_ref[:, BRANCH + hd * XATTN_HEAD_DIM : BRANCH + (hd + 1) * XATTN_HEAD_DIM]))
        o = jnp.concatenate(o_parts, axis=1)
        sgc = _sigmoid(gc)
        silc = gc * sgc
        uc = o * silc
        g_c = bnorm_ref[:, 2 * BRANCH : 3 * BRANCH]
        rc, nc, yc = _rms_fwd(uc, g_c)

        out = None
        for b, y_branch in enumerate((ya, yb, yc)):
            rows = slice(b * BRANCH, (b + 1) * BRANCH)
            y_bf = y_branch.astype(BF16)
            y_ref[:, rows] = y_bf
            part = _dot(y_bf, wout_ref[rows, :])
            out = part if out is None else out + part
        gpost = gpost_ref[...]
        r_out = lax.rsqrt(_rowmean(out * out) + EPS)
        on = out * r_out
        err = x_ref[...] + on * gpost - tgt_ref[...]
        loss_ref[...] += 0.5 * jnp.sum(_rowmean(err * err), axis=0, keepdims=True)

        dxo = err * inv_d
        dxo_ref[...] = dxo
        dgp, dout = _rms_bwd(dxo, gpost, r_out, on)
        dgpost_ref[...] += dgp
        dout_bf = dout.astype(BF16)
        dout_ref[...] = dout_bf
        dy = [_dot(dout_bf, wout_t_ref[:, b * BRANCH : (b + 1) * BRANCH]) for b in range(3)]

        dg_a, dua = _rms_bwd(dy[0], g_a, ra, na)
        dg_b, dub = _rms_bwd(dy[1], g_b, rb, nb)
        dg_c, duc = _rms_bwd(dy[2], g_c, rc, nc)
        dbnorm_ref[...] += jnp.concatenate([dg_a, dg_b, dg_c], axis=1)

        dpa = dua * sila
        dga = dua * pa * _dsilu(ga, sga)
        dpscale_ref[...] += _colsum(dpa * pm)
        dpm = dpa * pscale
        dxa_parts, carry_parts = [], []
        for g, w in enumerate(POOL_WINDOWS):
            cols = slice(g * POOL_GROUP_DIM, (g + 1) * POOL_GROUP_DIM)
            dpm_g = dpm[:, cols].astype(BF16)
            dd = _dot(dpm_g, pwt_ref[g])
            dpw_ref[g] += _dot(d_bf[g], dpm_g, TN)
            cg = dd * inv_cnt[g]
            carry_parts.append(cg[0:HALO, :])
            acc = jnp.concatenate([cg, carry_ref[:, cols]], axis=0)
            k = 1
            while k < w:
                acc = acc + pltpu.roll(acc, t + HALO - k, axis=0)
                k *= 2
            dxa_parts.append(acc[0:t, :] - dd)
        carry_ref[...] = jnp.concatenate(carry_parts, axis=1)
        dxa = jnp.concatenate(dxa_parts, axis=1)

        dsb = dub * silb
        dgb = dub * sb * _dsilu(gb, sgb)
        du = dsb * z
        dz = dsb * u
        dz_bf = dz.astype(BF16)
        dvn_rows = []
        dz_sum = None
        for c in range(n_chunks):
            rows = slice(c * SGU_CHUNK, (c + 1) * SGU_CHUNK)
            dz_sum = dz[rows, :] if dz_sum is None else dz_sum + dz[rows, :]
            parts = []
            for hd in range(SGU_HEADS):
                cols = slice(hd * SGU_HEAD_DIM, (hd + 1) * SGU_HEAD_DIM)
                parts.append(_dot(wmt_ref[hd], dz_bf[rows, cols]))
                dwm_ref[hd] += _dot(dz_bf[rows, cols], vn_bf[rows, cols], NT)
            dvn_rows.append(jnp.concatenate(parts, axis=1))
        dzsum_ref[...] += dz_sum
        dvn = dvn_rows[0] if n_chunks == 1 else jnp.concatenate(dvn_rows, axis=0)
        dlng_ref[...] += _colsum(dvn * vhat)
        dlnb_ref[...] += _colsum(dvn)
        dvh = dvn * lng
        dv = rstd * (dvh - _rowmean(dvh) - vhat * _rowmean(dvh * vhat))

        do = duc * silc
        dgc = duc * o * _dsilu(gc, sgc)
        dq_parts = []
        for hd in range(XATTN_HEADS):
            cols = slice(hd * XATTN_HEAD_DIM, (hd + 1) * XATTN_HEAD_DIM)
            vcols = slice(BRANCH + hd * XATTN_HEAD_DIM, BRANCH + (hd + 1) * XATTN_HEAD_DIM)
            do_h = do[:, cols].astype(BF16)
            p = p_bf[hd].astype(F32)
            dp = _dot(do_h, kvt_ref[vcols, :])
            dkv_ref[:, vcols] += _dot(p_bf[hd], do_h, TN)
            ds_bf = (p * (dp - jnp.sum(dp * p, axis=-1, keepdims=True)) * scale).astype(BF16)
            dq_parts.append(_dot(ds_bf, kv_ref[:, cols]))
            dkv_ref[:, cols] += _dot(ds_bf, q_bf[hd], TN)
        dq = jnp.concatenate(dq_parts, axis=1)

        dproj_ref[...] = jnp.concatenate([dxa, dga, du, dv, dgb, dq, dgc], axis=1).astype(BF16)

        @pl.when(i == n_tiles - 1)
        def _():
            keep = lax.broadcasted_iota(jnp.int32, (SGU_CHUNK, SGU_CHUNK), 0) >= lax.broadcasted_iota(
                jnp.int32, (SGU_CHUNK, SGU_CHUNK), 1
            )
            for hd in range(SGU_HEADS):
                dwm_ref[hd] = jnp.where(keep, dwm_ref[hd], 0.0)
                per_pos = dzsum_ref[:, hd * SGU_HEAD_DIM : (hd + 1) * SGU_HEAD_DIM]
                dbias_ref[hd : hd + 1, :] = _colsum(per_pos.T)
            for acc, res in ((dpw_ref, dpw_out), (dwm_ref, dwm_out), (dkv_ref, dkv_out)):
                pltpu.sync_copy(acc, res)

    row_tile = lambda width: pl.BlockSpec((t, width), lambda i: (n_tiles - 1 - i, 0))
    halo_spec = pl.BlockSpec(
        (HALO, BRANCH), lambda i: (jnp.maximum((n_tiles - 1 - i) * halo_blocks_per_tile - 1, 0), 0)
    )
    acc_shapes = [
        (1, 128),
        (1, d),
        (1, MIX_WIDTH),
        (1, BRANCH),
        (1, BRANCH),
        (1, BRANCH),
        pool_w.shape,
        sgu_wm.shape,
        (SGU_HEADS, SGU_CHUNK),
        kv.shape,
    ]
    return pl.pallas_call(
        body,
        name="mix",
        grid=(n_tiles,),
        in_specs=[
            row_tile(IN_WIDTH), halo_spec, row_tile(d), row_tile(d), _resident(kv.shape), _resident(kv_t.shape),
            _resident(wout.shape), _resident(wout_t.shape), _resident(pool_w.shape), _resident(pool_w_t.shape),
            _full((1, BRANCH)), _full((1, BRANCH)), _full((1, BRANCH)), _resident((SGU_CHUNK, BRANCH)),
            _resident(sgu_wm.shape), _resident(sgu_wm_t.shape), _full((1, MIX_WIDTH)), _full((1, d)),
        ],
        out_specs=[row_tile(MIX_WIDTH), row_tile(d), row_tile(d), row_tile(IN_WIDTH)]
        + [ANY if len(a) == 3 or a == kv.shape else _full(a) for a in acc_shapes],
        out_shape=[
            jax.ShapeDtypeStruct((s, MIX_WIDTH), BF16),
            jax.ShapeDtypeStruct((s, d), BF16),
            jax.ShapeDtypeStruct((s, d), F32),
            jax.ShapeDtypeStruct((s, IN_WIDTH), BF16),
        ]
        + [jax.ShapeDtypeStruct(a, F32) for a in acc_shapes],
        scratch_shapes=[
            pltpu.VMEM((HALO, BRANCH), F32), pltpu.VMEM((SGU_CHUNK, BRANCH), F32), pltpu.VMEM(pool_w.shape, F32),
            pltpu.VMEM(sgu_wm.shape, F32), pltpu.VMEM(kv.shape, F32),
        ],
        compiler_params=_params("arbitrary", vmem_limit_bytes=VMEM_LIMIT_MIX_BYTES),
    )(
        proj, proj, x, target, kv, kv_t, wout, wout_t, pool_w, pool_w_t, pool_scale, ln_g, ln_b, sgu_bias, sgu_wm,
        sgu_wm_t, branch_norm, norm_post,
    )


def _position():
    return lax.axis_index("x"), lax.axis_index("y"), lax.axis_index("c")


N_CHIPS = 4


def _landing_shape(kind, a):
    return (N_CHIPS,) + a.shape[2:] if kind == "pair" else a.shape


def _carry_specs(groups):
    arrays = [(kind, a) for kind, arrs in groups for a in arrs]
    scratch = []
    for _, arrs in groups:
        n = len(arrs)
        scratch += [pltpu.SemaphoreType.DMA((n, N_DEV)), pltpu.SemaphoreType.DMA((n, N_DEV)), pltpu.SemaphoreType.DMA((n,))]
    return dict(
        n=len(arrays),
        operands=[a for _, a in arrays],
        in_specs=[ANY] * len(arrays),
        out_specs=[ANY] * len(arrays),
        out_shape=[jax.ShapeDtypeStruct(_landing_shape(kind, a), a.dtype) for kind, a in arrays],
        scratch_shapes=scratch,
    )


def _carry(groups, src, out, sems):
    x, y, c = _position()
    chip = 2 * x + y
    me = 2 * chip + c

    def remote(s, d, send_sems, recv_sems, a, m, to):
        return pltpu.make_async_remote_copy(
            src_ref=s, dst_ref=d, send_sem=send_sems.at[a, m], recv_sem=recv_sems.at[a, m], device_id=to,
            device_id_type=MESH,
        )

    def copies():
        far, near = [], []
        at = 0
        for g, (kind, arrs) in enumerate(groups):
            send_sems, recv_sems, local_sems = sems[3 * g : 3 * g + 3]
            for a in range(len(arrs)):
                s, d = src[at + a], out[at + a]
                if kind == "pair":
                    far.append(remote(s.at[:, 1 - c], d, send_sems, recv_sems, a, 1, (x, y, 1 - c)))
                elif kind == "chips":
                    for m in range(1, N_CHIPS):
                        px, py = x ^ (m >> 1), y ^ (m & 1)
                        far.append(remote(s.at[2 * px + py], d.at[chip], send_sems, recv_sems, a, m, (px, py, c)))
                    near.append(pltpu.make_async_copy(s.at[chip], d.at[chip], local_sems.at[a]))
                else:
                    for m in range(1, N_DEV):
                        px, py, pc = x ^ ((m >> 2) & 1), y ^ ((m >> 1) & 1), c ^ (m & 1)
                        far.append(
                            remote(s.at[4 * px + 2 * py + pc], d.at[me], send_sems, recv_sems, a, m, (px, py, pc))
                        )
                    near.append(pltpu.make_async_copy(s.at[me], d.at[me], local_sems.at[a]))
            at += len(arrs)
        return far, near

    def start():
        far, near = copies()
        for cp in near + far:
            cp.start()

    def finish():
        far, near = copies()
        for cp in far:
            cp.wait_recv()
        for cp in far:
            cp.wait_send()
        for cp in near:
            cp.wait()

    return start, finish


def _exchange(groups, name):
    carried = _carry_specs(groups)
    n_c = carried["n"]

    def body(*refs):
        start, finish = _carry(groups, refs[:n_c], refs[n_c : 2 * n_c], refs[2 * n_c :])
        start()
        finish()

    return pl.pallas_call(
        body,
        name=name,
        in_specs=carried["in_specs"],
        out_specs=carried["out_specs"],
        out_shape=carried["out_shape"],
        scratch_shapes=carried["scratch_shapes"],
    )(*carried["operands"])


def _pair_sum(mine, theirs, name, groups=()):
    n = len(mine)
    carried = _carry_specs(groups)
    n_c = carried["n"]
    core = lax.axis_index("c").astype(jnp.int32).reshape(1)

    def body(core_ref, *refs):
        del core_ref
        own = refs[:n]
        sib = refs[n : 2 * n]
        src = refs[2 * n : 2 * n + n_c]
        out = refs[2 * n + n_c : 3 * n + n_c]
        landed = refs[3 * n + n_c : 3 * n + 2 * n_c]
        start, finish = _carry(groups, src, landed, refs[3 * n + 2 * n_c :])
        b = pl.program_id(0)

        @pl.when(b == 0)
        def _():
            start()

        for a in range(n):
            out[a][...] = (own[a][...].astype(F32) + sib[a][...].astype(F32)).astype(out[a].dtype)

        @pl.when(b == N_CHIPS - 1)
        def _():
            finish()

    block = lambda a: pl.BlockSpec((None,) + a.shape[1:], lambda b, core_ref: (b, 0, 0))
    res = pl.pallas_call(
        body,
        name=name,
        grid_spec=pltpu.PrefetchScalarGridSpec(
            num_scalar_prefetch=1,
            grid=(N_CHIPS,),
            in_specs=[pl.BlockSpec((None, None) + a.shape[2:], lambda b, core_ref: (b, core_ref[0], 0, 0)) for a in mine]
            + [block(a) for a in theirs]
            + carried["in_specs"],
            out_specs=[block(a) for a in theirs] + carried["out_specs"],
            scratch_shapes=carried["scratch_shapes"],
        ),
        out_shape=[jax.ShapeDtypeStruct(a.shape, a.dtype) for a in theirs] + carried["out_shape"],
        compiler_params=_params("arbitrary"),
    )(core, *mine, *theirs, *carried["operands"])
    return res[:n], res[n:]


def _weight_grad(a, b, n_blk, blocked, name, groups):
    s = a.shape[0]
    t = min(TILE_GRAD, s)
    n_t = s // t
    n_pairs = n_blk // 2
    if blocked == "cols":
        k, c = a.shape[1], b.shape[1] // n_blk
        a_spec = pl.BlockSpec((t, k), lambda j, i: (i, 0))
        b_spec = pl.BlockSpec((t, 2 * c), lambda j, i: (i, j))
        acc_shape = (k, 2 * c)
    else:
        k, c = a.shape[1] // n_blk, b.shape[1]
        a_spec = pl.BlockSpec((t, 2 * k), lambda j, i: (i, j))
        b_spec = pl.BlockSpec((t, c), lambda j, i: (i, 0))
        acc_shape = (2 * k, c)
    carried = _carry_specs(groups)
    n_p = carried["n"]

    def body(a_ref, b_ref, *refs):
        src = refs[:n_p]
        o_ref, theirs_ref = refs[n_p : n_p + 2]
        landed = refs[n_p + 2 : 2 * n_p + 2]
        acc_ref, sbuf, pair_send, pair_recv = refs[2 * n_p + 2 : 2 * n_p + 6]
        start, finish = _carry(groups, src, landed, refs[2 * n_p + 6 :])
        j = pl.program_id(0)
        i = pl.program_id(1)
        x, y, c_me = _position()

        def to_sibling(pair):
            return pltpu.make_async_remote_copy(
                src_ref=sbuf.at[1 - c_me], dst_ref=theirs_ref.at[pair], send_sem=pair_send.at[pair],
                recv_sem=pair_recv.at[pair], device_id=(x, y, 1 - c_me), device_id_type=MESH,
            )

        @pl.when(jnp.logical_and(j == 0, i == 0))
        def _():
            start()

        @pl.when(i == 0)
        def _():
            acc_ref[...] = jnp.zeros_like(acc_ref)

        acc_ref[...] += _dot(a_ref[...], b_ref[...], TN)

        @pl.when(i == n_t - 1)
        def _():
            for pair in range(1, n_pairs):

                @pl.when(j == pair)
                def _():
                    to_sibling(pair - 1).wait_send()

            for half in range(2):
                if blocked == "cols":
                    block = acc_ref[:, half * c : (half + 1) * c].astype(BF16)
                else:
                    block = acc_ref[half * k : (half + 1) * k, :].astype(BF16)
                o_ref[half] = block
                sbuf[half] = block
            for pair in range(n_pairs):

                @pl.when(j == pair)
                def _():
                    to_sibling(pair).start()

        @pl.when(jnp.logical_and(j == n_pairs - 1, i == n_t - 1))
        def _():
            to_sibling(n_pairs - 1).wait_send()
            for pair in range(n_pairs):
                to_sibling(pair).wait_recv()
            finish()

    res = pl.pallas_call(
        body,
        name=name,
        grid=(n_pairs, n_t),
        in_specs=[a_spec, b_spec] + carried["in_specs"],
        out_specs=[pl.BlockSpec((2, k, c), lambda j, i: (j, 0, 0)), ANY] + carried["out_specs"],
        out_shape=[jax.ShapeDtypeStruct((n_blk, k, c), BF16), jax.ShapeDtypeStruct((n_pairs, k, c), BF16)]
        + carried["out_shape"],
        scratch_shapes=[
            pltpu.VMEM(acc_shape, F32), pltpu.VMEM((2, k, c), BF16), pltpu.SemaphoreType.DMA((n_pairs,)),
            pltpu.SemaphoreType.DMA((n_pairs,)),
        ]
        + carried["scratch_shapes"],
        compiler_params=_params("arbitrary", "arbitrary"),
    )(a, b, *carried["operands"])
    return res[0], res[1], res[2:]


def _input_grad(dproj, win_t, x, dxo, norm_pre, groups):
    s, d = x.shape
    t = min(TILE_GRAD, s)
    n_t = s // t
    kb = BRANCH
    n_k = win_t.shape[0] // kb
    carried = _carry_specs(groups)
    n_p = carried["n"]

    def body(dp_ref, w_ref, x_ref, dxo_ref, g_ref, *refs):
        src = refs[:n_p]
        gx_ref, dg_ref = refs[n_p : n_p + 2]
        landed = refs[n_p + 2 : 2 * n_p + 2]
        acc_ref = refs[2 * n_p + 2]
        start, finish = _carry(groups, src, landed, refs[2 * n_p + 3 :])
        i = pl.program_id(0)
        j = pl.program_id(1)

        @pl.when(jnp.logical_and(i == 0, j == 0))
        def _():
            start()
            dg_ref[...] = jnp.zeros_like(dg_ref)

        @pl.when(j == 0)
        def _():
            acc_ref[...] = jnp.zeros_like(acc_ref)

        acc_ref[...] += _dot(dp_ref[...], w_ref[...])

        @pl.when(j == n_k - 1)
        def _():
            xv = x_ref[...]
            gain = g_ref[...]
            r = lax.rsqrt(_rowmean(xv * xv) + EPS)
            dgain, dx = _rms_bwd(acc_ref[...], gain, r, xv * r)
            dg_ref[...] += dgain
            gx_ref[...] = dxo_ref[...] + dx

        @pl.when(jnp.logical_and(i == n_t - 1, j == n_k - 1))
        def _():
            finish()

    res = pl.pallas_call(
        body,
        name="input_grad",
        grid=(n_t, n_k),
        in_specs=[
            pl.BlockSpec((t, kb), lambda i, j: (i, j)),
            pl.BlockSpec((kb, d), lambda i, j: (j, 0)),
            pl.BlockSpec((t, d), lambda i, j: (i, 0)),
            pl.BlockSpec((t, d), lambda i, j: (i, 0)),
            _full((1, d)),
        ]
        + carried["in_specs"],
        out_specs=[pl.BlockSpec((t, d), lambda i, j: (i, 0)), _full((1, d))] + carried["out_specs"],
        out_shape=[jax.ShapeDtypeStruct((s, d), F32), jax.ShapeDtypeStruct((1, d), F32)] + carried["out_shape"],
        scratch_shapes=[pltpu.VMEM((t, d), F32)] + carried["scratch_shapes"],
        compiler_params=_params("arbitrary", "arbitrary"),
    )(dproj, win_t, x, dxo, norm_pre, *carried["operands"])
    return res[0], res[1], res[2:]


def _kv_backward(dkv, memn, wkv, mem):
    m, d = mem.shape
    n = wkv.shape[1]

    def body(dkv_ref, memn_ref, w_ref, mem_ref, mem_blk_ref, gw_ref, dg_ref):
        dkv_bf = dkv_ref[...].astype(BF16)
        gw_ref[...] = _dot(memn_ref[...], dkv_bf, TN).astype(BF16)
        dmemn = _dot(dkv_bf, w_ref[...], NT)
        mv = mem_ref[...]
        r = lax.rsqrt(_rowmean(mv * mv) + EPS)
        dg_ref[...] = _colsum(dmemn * (mem_blk_ref[...] * r))

    cols = pl.BlockSpec((m, WKV_BLK), lambda j: (0, j))
    return pl.pallas_call(
        body,
        name="kv_backward",
        grid=(N_DEV,),
        in_specs=[_full((m, n)), cols, pl.BlockSpec((WKV_BLK, n), lambda j: (j, 0)), _full((m, d)), cols],
        out_specs=[pl.BlockSpec((None, WKV_BLK, n), lambda j: (j, 0, 0)), pl.BlockSpec((1, WKV_BLK), lambda j: (0, j))],
        out_shape=[jax.ShapeDtypeStruct((N_DEV, WKV_BLK, n), BF16), jax.ShapeDtypeStruct((1, d), F32)],
        compiler_params=_params("arbitrary"),
    )(dkv, memn, wkv, mem, mem)


def _adamw_math(w, g, m, v):
    m = ADAM_B1 * m + (1.0 - ADAM_B1) * g
    v = ADAM_B2 * v + (1.0 - ADAM_B2) * (g * g)
    m_hat = m / (1.0 - ADAM_B1**ADAM_STEP)
    v_hat = v / (1.0 - ADAM_B2**ADAM_STEP)
    delta = -ADAM_LR * (m_hat / (jnp.sqrt(v_hat) + ADAM_EPS) + ADAM_WD * w)
    return delta, m, v


def _adamw(parts, w, m, v, name):
    r, c = w.shape
    slots = parts.shape[0]
    t = r
    while t * c * 4 > TILE_ADAM_BYTES and t % 16 == 0:
        t //= 2

    def body(p_ref, w_ref, m_ref, v_ref, g_ref, d_ref, nm_ref, nv_ref):
        g = p_ref[0].astype(F32)
        for k in range(1, slots):
            g = g + p_ref[k].astype(F32)
        delta, nm, nv = _adamw_math(w_ref[...], g, m_ref[...], v_ref[...])
        g_ref[...] = g
        d_ref[...] = delta
        nm_ref[...] = nm
        nv_ref[...] = nv

    tile = pl.BlockSpec((t, c), lambda i: (i, 0))
    return pl.pallas_call(
        body,
        name=name,
        grid=(r // t,),
        in_specs=[pl.BlockSpec((slots, t, c), lambda i: (0, i, 0)), tile, tile, tile],
        out_specs=[tile] * 4,
        out_shape=[jax.ShapeDtypeStruct((r, c), F32)] * 4,
        compiler_params=_params("parallel"),
    )(parts, w, m, v)


def _adamw_packed(parts, triples, name):
    slots = parts.shape[0]
    sizes = [w.shape[0] for w, _, _ in triples]
    rest = parts.shape[1] - sum(sizes)

    def total(p_ref, at, rows):
        g = p_ref[0, at : at + rows, :]
        for k in range(1, slots):
            g = g + p_ref[k, at : at + rows, :]
        return g

    def body(p_ref, *refs):
        ins = refs[: 3 * len(triples)]
        outs = refs[3 * len(triples) :]
        at = 0
        for n, rows in enumerate(sizes):
            g = total(p_ref, at, rows)
            w_ref, m_ref, v_ref = ins[3 * n : 3 * n + 3]
            delta, nm, nv = _adamw_math(w_ref[...], g, m_ref[...], v_ref[...])
            for ref, val in zip(outs[4 * n : 4 * n + 4], (g, delta, nm, nv)):
                ref[...] = val
            at += rows
        if rest:
            outs[-1][...] = total(p_ref, at, rest)

    flat = [a for t in triples for a in t]
    res = pl.pallas_call(
        body,
        name=name,
        out_shape=[jax.ShapeDtypeStruct(w.shape, F32) for w, _, _ in triples for _ in range(4)]
        + ([jax.ShapeDtypeStruct((rest, 128), F32)] if rest else []),
        compiler_params=pltpu.CompilerParams(vmem_limit_bytes=VMEM_LIMIT_BYTES),
    )(parts, *flat)
    return [res[4 * n : 4 * n + 4] for n in range(len(triples))], (res[-1] if rest else None)


SMALL = ("norm_pre", "pool_scale", "sgu_ln_g", "sgu_ln_b", "sgu_w", "sgu_b", "mem_norm", "branch_norm", "norm_post")


def _local_view(name, w):
    if name == "sgu_w":
        return w.reshape(SGU_HEADS, SGU_CHUNK, SGU_CHUNK)
    if name == "sgu_b":
        return w.reshape(SGU_HEADS, SGU_CHUNK)
    return w.reshape(1, -1)


def _forward_backward(x, mem, target, shards, small):
    causal = jnp.tril(jnp.ones((SGU_CHUNK, SGU_CHUNK), dtype=bool))
    sgu_wm = jnp.where(causal[None], small["sgu_w"], 0.0).astype(BF16)
    sgu_bias = jnp.repeat(jnp.transpose(small["sgu_b"]), SGU_HEAD_DIM, axis=1)

    proj, h, (win, wkv, pool_all, wout) = _proj_gather(x, small["norm_pre"], shards)
    wout = wout.reshape(MIX_WIDTH, D_MODEL)
    wkv = wkv.reshape(D_MODEL, 2 * BRANCH)
    pool_full = (
        pool_all.reshape(N_DEV, len(POOL_WINDOWS), POOL_BLK, POOL_GROUP_DIM)
        .transpose(1, 0, 2, 3)
        .reshape(len(POOL_WINDOWS), POOL_GROUP_DIM, POOL_GROUP_DIM)
    )
    memn, kv = _kv_forward(mem, small["mem_norm"], wkv)
    (y, dout, dxo, dproj, loss, d_norm_post, d_branch_norm, d_pool_scale, d_ln_g, d_ln_b, d_pool_w, d_sgu_w, d_sgu_b,
     dkv) = _mix(
        proj, x, target, kv, kv.T, wout, wout.T, pool_full, jnp.swapaxes(pool_full, 1, 2), small["pool_scale"],
        small["sgu_ln_g"], small["sgu_ln_b"], sgu_bias, sgu_wm, jnp.swapaxes(sgu_wm, 1, 2), small["branch_norm"],
        small["norm_post"],
    )
    g_wkv, d_mem_norm = _kv_backward(dkv, memn, wkv, mem)
    g_pool = (
        d_pool_w.reshape(len(POOL_WINDOWS), N_DEV, POOL_BLK, POOL_GROUP_DIM)
        .transpose(1, 0, 2, 3)
        .reshape(N_DEV, len(POOL_WINDOWS) * POOL_BLK, POOL_GROUP_DIM)
        .astype(BF16)
    )
    small_grads = dict(
        pool_scale=d_pool_scale, sgu_ln_g=d_ln_g, sgu_ln_b=d_ln_b, sgu_w=d_sgu_w, sgu_b=d_sgu_b,
        mem_norm=d_mem_norm, branch_norm=d_branch_norm, norm_post=d_norm_post,
    )
    packed = jnp.concatenate(
        [small_grads[n].reshape(-1, LANES) for n in SMALL if n != "norm_pre"]
        + [jnp.broadcast_to(loss, (SUBLANES, LANES))],
        axis=0,
    )
    packed = jnp.broadcast_to(packed[None, None], (N_CHIPS, 2) + packed.shape)

    by_chip = lambda g: g.reshape((N_CHIPS, 2) + g.shape[1:])
    small_mine = [by_chip(g_wkv), by_chip(g_pool), packed]
    small_theirs = _exchange([("pair", small_mine)], "pair_exchange_small")
    small_sums, _ = _pair_sum(small_mine, small_theirs, "pair_sum_small")
    g_wout, wout_theirs, (l_wkv, l_pool, l_packed) = _weight_grad(
        y, dout, N_DEV, "rows", "grad_w_out", [("chips", list(small_sums))]
    )
    (wout_sums,), _ = _pair_sum([by_chip(g_wout)], [wout_theirs], "pair_sum_w_out")
    g_win, win_theirs, (l_wout,) = _weight_grad(h, dproj, N_DEV, "cols", "grad_w_in", [("chips", [wout_sums])])
    (win_sums,), _ = _pair_sum([by_chip(g_win)], [win_theirs], "pair_sum_w_in")
    grad_x, d_norm_pre, (l_win,) = _input_grad(
        dproj, jnp.swapaxes(win, 1, 2).reshape(IN_WIDTH, D_MODEL), x, dxo, small["norm_pre"],
        [("chips", [win_sums])],
    )
    return grad_x, dict(w_in=l_win, w_out=l_wout, w_kv=l_wkv, pool_w=l_pool), l_packed, d_norm_pre


def kernel(x, mem, norm_pre, w_in, pool_w, pool_scale, sgu_ln_g, sgu_ln_b, sgu_w, sgu_b, mem_norm, w_kv, branch_norm, w_out, norm_post, loss_target, m_norm_pre, m_w_in, m_pool_w, m_pool_scale, m_sgu_ln_g, m_sgu_ln_b, m_sgu_w, m_sgu_b, m_mem_norm, m_w_kv, m_branch_norm, m_w_out, m_norm_post, v_norm_pre, v_w_in, v_pool_w, v_pool_scale, v_sgu_ln_g, v_sgu_ln_b, v_sgu_w, v_sgu_b, v_mem_norm, v_w_kv, v_branch_norm, v_w_out, v_norm_post):
    weights = dict(norm_pre=norm_pre, w_in=w_in, pool_w=pool_w, pool_scale=pool_scale, sgu_ln_g=sgu_ln_g, sgu_ln_b=sgu_ln_b, sgu_w=sgu_w, sgu_b=sgu_b, mem_norm=mem_norm, w_kv=w_kv, branch_norm=branch_norm, w_out=w_out, norm_post=norm_post)
    first = dict(norm_pre=m_norm_pre, w_in=m_w_in, pool_w=m_pool_w, pool_scale=m_pool_scale, sgu_ln_g=m_sgu_ln_g, sgu_ln_b=m_sgu_ln_b, sgu_w=m_sgu_w, sgu_b=m_sgu_b, mem_norm=m_mem_norm, w_kv=m_w_kv, branch_norm=m_branch_norm, w_out=m_w_out, norm_post=m_norm_post)
    second = dict(norm_pre=v_norm_pre, w_in=v_w_in, pool_w=v_pool_w, pool_scale=v_pool_scale, sgu_ln_g=v_sgu_ln_g, sgu_ln_b=v_sgu_ln_b, sgu_w=v_sgu_w, sgu_b=v_sgu_b, mem_norm=v_mem_norm, w_kv=v_w_kv, branch_norm=v_branch_norm, w_out=v_w_out, norm_post=v_norm_post)
    order = ("norm_pre", "w_in", "pool_w", "pool_scale", "sgu_ln_g", "sgu_ln_b", "sgu_w", "sgu_b", "mem_norm", "w_kv", "branch_norm", "w_out", "norm_post")

    owned_shape = dict(
        w_in=(D_MODEL, WIN_BLK), w_out=(WOUT_BLK, D_MODEL), w_kv=(WKV_BLK, 2 * BRANCH),
        pool_w=(len(POOL_WINDOWS) * POOL_BLK, POOL_GROUP_DIM),
    )
    owned = {n: weights[n].reshape(owned_shape[n]) for n in owned_shape}
    small = {n: _local_view(n, weights[n]) for n in SMALL}
    grad_x, landed, landed_packed, d_norm_pre = _forward_backward(
        x[0], mem[0], loss_target[0], [owned[n].astype(BF16) for n in ("w_in", "w_kv", "pool_w", "w_out")], small
    )
    d_norm_pre = d_norm_pre.reshape(-1, 128)
    (landed_norm_pre,) = _exchange(
        [("all", [jnp.broadcast_to(d_norm_pre[None], (N_DEV,) + d_norm_pre.shape)])], "exchange_norm_pre"
    )

    grads, deltas, new_m, new_v = {}, {}, {}, {}
    for n in owned_shape:
        shape = weights[n].shape
        res = _adamw(
            landed[n], owned[n], first[n].reshape(owned_shape[n]), second[n].reshape(owned_shape[n]), "adamw_" + n
        )
        grads[n], deltas[n], new_m[n], new_v[n] = (a.reshape(shape) for a in res)
    rows_of = lambda tree, n: tree[n].reshape(-1, 128)
    for names, parts, name in (
        ([n for n in SMALL if n != "norm_pre"], landed_packed, "adamw_replicated"),
        (["norm_pre"], landed_norm_pre, "adamw_norm_pre"),
    ):
        res, rest = _adamw_packed(
            parts, [(rows_of(weights, n), rows_of(first, n), rows_of(second, n)) for n in names], name
        )
        if rest is not None:
            total = rest[0, 0]
        for n, four in zip(names, res):
            for tree, a in zip((grads, deltas, new_m, new_v), four):
                tree[n] = a.reshape(weights[n].shape)

    return (
        total,
        grad_x[None],
        *[grads[n] for n in order],
        *[deltas[n] for n in order],
        *[new_m[n] for n in order],
        *[new_v[n] for n in order],
    )
```

```python
import jax
import jax.numpy as jnp
from jax import lax
from jax.experimental import pallas as pl
from jax.experimental.pallas import tpu as pltpu

F32 = jnp.float32
BF16 = jnp.bfloat16
EPS = 1e-6

D_MODEL = 2048
POOL_WINDOWS = (2, 4, 8, 16)
POOL_GROUP_DIM = 256
BRANCH = 1024
SGU_CHUNK = 128
SGU_HEADS = 8
SGU_HEAD_DIM = 128
XATTN_HEADS = 4
XATTN_HEAD_DIM = 256
MIX_WIDTH = 3 * BRANCH
IN_WIDTH = 7 * BRANCH
N_DEV = 8
WIN_BLK = IN_WIDTH // N_DEV
WOUT_BLK = MIX_WIDTH // N_DEV
WKV_BLK = D_MODEL // N_DEV
POOL_BLK = POOL_GROUP_DIM // N_DEV
HALO = 16
LANES = 128
SUBLANES = 8

ADAM_LR = 0.001
ADAM_B1 = 0.9
ADAM_B2 = 0.999
ADAM_EPS = 1e-08
ADAM_WD = 0.01
ADAM_STEP = 10

VMEM_LIMIT_BYTES = 56 * 1024 * 1024
VMEM_LIMIT_MIX_BYTES = 63 * 1024 * 1024

TILE_PROJ = 512
TILE_MIX = 128
TILE_GRAD = 512
TILE_ADAM_BYTES = 1 << 20

ANY = pl.BlockSpec(memory_space=pl.ANY)
NN = (((1,), (0,)), ((), ()))
NT = (((1,), (1,)), ((), ()))
TN = (((0,), (0,)), ((), ()))
MESH = pl.DeviceIdType.MESH


def _dot(a, b, dims=NN):
    return lax.dot_general(a, b, dims, preferred_element_type=F32)


def _params(*semantics, vmem_limit_bytes=VMEM_LIMIT_BYTES):
    return pltpu.CompilerParams(dimension_semantics=semantics, vmem_limit_bytes=vmem_limit_bytes)


def _rowmean(a):
    return jnp.mean(a, axis=-1, keepdims=True)


def _colsum(a):
    return jnp.sum(a, axis=0, keepdims=True)


def _full(shape):
    zeros = (0,) * len(shape)
    return pl.BlockSpec(shape, lambda *_: zeros)


def _resident(shape):
    zeros = (0,) * len(shape)
    return pl.BlockSpec(shape, lambda *_: zeros, pipeline_mode=pl.Buffered(1))


def _kv_forward(mem, mem_norm, wkv):
    m, d = mem.shape
    n = wkv.shape[1]
    cols = 4 * LANES

    def body(mem_ref, g_ref, w_ref, memn_ref, kv_ref):
        mv = mem_ref[...]
        r = lax.rsqrt(_rowmean(mv * mv) + EPS)
        memn = (mv * r * g_ref[...]).astype(BF16)
        memn_ref[...] = memn
        kv_ref[...] = _dot(memn, w_ref[...]).astype(BF16)

    return pl.pallas_call(
        body,
        name="kv_forward",
        grid=(n // cols,),
        in_specs=[_full((m, d)), _full((1, d)), pl.BlockSpec((d, cols), lambda j: (0, j))],
        out_specs=[_full((m, d)), pl.BlockSpec((m, cols), lambda j: (0, j))],
        out_shape=[jax.ShapeDtypeStruct((m, d), BF16), jax.ShapeDtypeStruct((m, n), BF16)],
        compiler_params=_params("arbitrary"),
    )(mem, mem_norm, wkv)


def _proj_gather(x_in, norm_pre, shards):
    s, d = x_in.shape
    t = min(TILE_PROJ, s)
    n_t = s // t
    n_arr = len(shards)

    def places(x, y, c):
        return (x, y, c), (x, y, 1 - c), (x ^ c, y ^ (1 - c)), (x ^ (1 - c), y ^ c), (1 - x, 1 - y)

    def index(chip, core):
        return 4 * chip[0] + 2 * chip[1] + core

    _, _, chip_a, chip_b, chip_d = places(*_position())
    c_out = lax.axis_index("c")
    me_out = index((lax.axis_index("x"), lax.axis_index("y")), c_out)
    order = jnp.stack(
        [
            me_out, me_out ^ 1, index(chip_a, c_out), index(chip_b, c_out), index(chip_b, 1 - c_out),
            index(chip_a, 1 - c_out), index(chip_d, c_out), index(chip_d, 1 - c_out),
        ]
    ).astype(jnp.int32)

    def body(order_ref, x_ref, g_ref, *refs):
        del order_ref
        src = refs[:n_arr]
        proj_ref, h_ref = refs[n_arr : n_arr + 2]
        out = refs[n_arr + 2 : 2 * n_arr + 2]
        wbuf, hs, send_sems, recv_sems, local_sems, load_sems = refs[2 * n_arr + 2 :]
        j = pl.program_id(0)
        i = pl.program_id(1)
        me, sibling, chip_a, chip_b, chip_d = places(*_position())
        c = me[2]

        def block(a, chip, core):
            return out[a].at[index(chip, core)]

        def copy(a, k, owner, to, from_input=False):
            return pltpu.make_async_remote_copy(
                src_ref=src[a] if from_input else block(a, *owner),
                dst_ref=block(a, *owner),
                send_sem=send_sems.at[a, k],
                recv_sem=recv_sems.at[a, k],
                device_id=to,
                device_id_type=MESH,
            )

        mine = (me[:2], c)

        def own(a):
            return pltpu.make_async_copy(src[a], block(a, *mine), local_sems.at[a])

        def first_sends(a):
            return [
                copy(a, 0, mine, sibling, from_input=True),
                copy(a, 1, mine, (*chip_a, c), from_input=True),
                copy(a, 2, mine, (*chip_b, c), from_input=True),
            ]

        def onward(a, k):
            owner = {3: chip_a, 4: chip_a, 5: chip_b, 6: chip_d}[k]
            return copy(a, k, (owner, c), (*chip_b, c) if k == 3 else sibling)

        def landed(a, k):
            owner = {0: mine[0], 1: chip_a, 2: chip_b, 3: chip_d, 4: chip_b, 5: chip_a, 6: chip_d}[k]
            core = c if k in (1, 2, 3) else 1 - c
            copy(a, k, (owner, core), me).wait_recv()
            return owner, core

        def load(ref, step):
            return pltpu.make_async_copy(ref, wbuf.at[step % 2], load_sems.at[step % 2])

        @pl.when(jnp.logical_and(j == 0, i == 0))
        def _():
            own(0).start()
            for cp in first_sends(0):
                cp.start()
            load(src[0], 0).start()
            load(src[0], 0).wait()

        steps = {1: (0, ()), 2: (1, (3, 4)), 3: (2, (5,)), 4: (4, ()), 5: (5, ()), 6: (3, (6,)), 7: (6, ())}
        for step, (k, then) in steps.items():

            @pl.when(jnp.logical_and(j == step, i == 0))
            def _():
                load(src[0], step).wait()

            @pl.when(jnp.logical_and(j == step - 1, i == n_t - 1))
            def _():
                owner = landed(0, k)
                for k2 in then:
                    onward(0, k2).start()
                if k == 1:
                    for a in range(1, n_arr):
                        own(a).start()
                        for cp in first_sends(a):
                            cp.start()
                if k == 3:
                    for a in range(1, n_arr):
                        for k1, then1 in ((1, (3, 4)), (2, (5,))):
                            landed(a, k1)
                            for k2 in then1:
                                onward(a, k2).start()
                load(block(0, *owner), step).start()

        @pl.when(j == 0)
        def _():
            xv = x_ref[...]
            h = (xv * lax.rsqrt(_rowmean(xv * xv) + EPS) * g_ref[...]).astype(BF16)
            hs[i] = h
            h_ref[...] = h

        proj_ref[...] = _dot(hs[i], wbuf[j % 2]).astype(BF16)

        @pl.when(jnp.logical_and(j == N_DEV - 1, i == n_t - 1))
        def _():
            for a in range(1, n_arr):
                landed(a, 3)
                onward(a, 6).start()
            for a in range(1, n_arr):
                for k in (0, 4, 5, 6):
                    landed(a, k)
            for a in range(n_arr):
                for cp in first_sends(a) + [onward(a, k) for k in (3, 4, 5, 6)]:
                    cp.wait_send()
                own(a).wait()

    res = pl.pallas_call(
        body,
        name="proj_gather",
        grid_spec=pltpu.PrefetchScalarGridSpec(
            num_scalar_prefetch=1,
            grid=(N_DEV, n_t),
            in_specs=[
                pl.BlockSpec((t, d), lambda j, i, order_ref: (jnp.where(j == 0, i, n_t - 1), 0)),
                pl.BlockSpec((1, d), lambda j, i, order_ref: (0, 0)),
            ]
            + [ANY] * n_arr,
            out_specs=[
                pl.BlockSpec((t, WIN_BLK), lambda j, i, order_ref: (i, order_ref[j])),
                pl.BlockSpec((t, d), lambda j, i, order_ref: (jnp.where(j == 0, i, n_t - 1), 0)),
            ]
            + [ANY] * n_arr,
            scratch_shapes=[
                pltpu.VMEM((2,) + shards[0].shape, BF16),
                pltpu.VMEM((n_t, t, d), BF16),
                pltpu.SemaphoreType.DMA((n_arr, 7)),
                pltpu.SemaphoreType.DMA((n_arr, 7)),
                pltpu.SemaphoreType.DMA((n_arr,)),
                pltpu.SemaphoreType.DMA((2,)),
            ],
        ),
        out_shape=[jax.ShapeDtypeStruct((s, IN_WIDTH), BF16), jax.ShapeDtypeStruct((s, d), BF16)]
        + [jax.ShapeDtypeStruct((N_DEV,) + a.shape, a.dtype) for a in shards],
        compiler_params=_params("arbitrary", "arbitrary"),
    )(order, x_in, norm_pre, *shards)
    return res[0], res[1], res[2:]


def _sigmoid(a):
    return jax.nn.sigmoid(a)


def _dsilu(a, sg):
    return sg * (1.0 + a * (1.0 - sg))


def _rms_fwd(u, gain):
    r = lax.rsqrt(_rowmean(u * u) + EPS)
    n = u * r
    return r, n, n * gain


def _rms_bwd(dy, gain, r, n):
    dn = dy * gain
    return _colsum(dy * n), r * (dn - n * _rowmean(dn * n))


def _mix(proj, x, target, kv, kv_t, wout, wout_t, pool_w, pool_w_t, pool_scale, ln_g, ln_b, sgu_bias, sgu_wm, sgu_wm_t, branch_norm, norm_post):
    s, d = x.shape
    t = min(TILE_MIX, s)
    n_tiles = s // t
    n_chunks = t // SGU_CHUNK
    halo_blocks_per_tile = t // HALO
    inv_d = 1.0 / d
    scale = 1.0 / (XATTN_HEAD_DIM**0.5)

    def body(
        proj_ref, halo_ref, x_ref, tgt_ref, kv_ref, kvt_ref, wout_ref, wout_t_ref, pw_ref, pwt_ref, pscale_ref, lng_ref,
        lnb_ref, bias_ref, wm_ref, wmt_ref, bnorm_ref, gpost_ref,
        y_ref, dout_ref, dxo_ref, dproj_ref, loss_ref, dgpost_ref, dbnorm_ref, dpscale_ref, dlng_ref, dlnb_ref,
        dpw_out, dwm_out, dbias_ref, dkv_out,
        carry_ref, dzsum_ref, dpw_ref, dwm_ref, dkv_ref,
    ):
        i = pl.program_id(0)
        tile = n_tiles - 1 - i

        @pl.when(i == 0)
        def _():
            carry_ref[...] = jnp.zeros_like(carry_ref)
            dzsum_ref[...] = jnp.zeros_like(dzsum_ref)
            for ref in (loss_ref, dgpost_ref, dbnorm_ref, dpscale_ref, dlng_ref, dlnb_ref, dpw_ref, dwm_ref, dkv_ref):
                ref[...] = jnp.zeros_like(ref)

        t_glob = tile * t + lax.broadcasted_iota(jnp.int32, (t, 1), 0)
        inv_cnt = [1.0 / jnp.minimum(t_glob + 1, w).astype(F32) for w in POOL_WINDOWS]

        xa = proj_ref[:, 0:BRANCH].astype(F32)
        ga = proj_ref[:, BRANCH : 2 * BRANCH].astype(F32)
        halo = jnp.where(tile == 0, 0.0, halo_ref[...].astype(F32))
        d_bf, pm_parts = [], []
        for g, w in enumerate(POOL_WINDOWS):
            cols = slice(g * POOL_GROUP_DIM, (g + 1) * POOL_GROUP_DIM)
            acc = jnp.concatenate([halo[:, cols], xa[:, cols]], axis=0)
            k = 1
            while k < w:
                acc = acc + pltpu.roll(acc, k, axis=0)
                k *= 2
            dg = (acc[HALO:, :] * inv_cnt[g] - xa[:, cols]).astype(BF16)
            d_bf.append(dg)
            pm_parts.append(_dot(dg, pw_ref[g]))
        pm = jnp.concatenate(pm_parts, axis=1)
        pscale = pscale_ref[...]
        pa = pm * pscale
        sga = _sigmoid(ga)
        sila = ga * sga
        ua = pa * sila
        g_a = bnorm_ref[:, 0:BRANCH]
        ra, na, ya = _rms_fwd(ua, g_a)

        u = proj_ref[:, 2 * BRANCH : 3 * BRANCH].astype(F32)
        v = proj_ref[:, 3 * BRANCH : 4 * BRANCH].astype(F32)
        gb = proj_ref[:, 4 * BRANCH : 5 * BRANCH].astype(F32)
        lng = lng_ref[...]
        vc = v - _rowmean(v)
        rstd = lax.rsqrt(_rowmean(vc * vc) + EPS)
        vhat = vc * rstd
        vn_bf = (vhat * lng + lnb_ref[...]).astype(BF16)
        z_rows = []
        for c in range(n_chunks):
            rows = slice(c * SGU_CHUNK, (c + 1) * SGU_CHUNK)
            z_rows.append(
                jnp.concatenate(
                    [
                        _dot(wm_ref[hd], vn_bf[rows, hd * SGU_HEAD_DIM : (hd + 1) * SGU_HEAD_DIM])
                        for hd in range(SGU_HEADS)
                    ],
                    axis=1,
                )
                + bias_ref[...]
            )
        z = z_rows[0] if n_chunks == 1 else jnp.concatenate(z_rows, axis=0)
        sb = u * z
        sgb = _sigmoid(gb)
        silb = gb * sgb
        ub = sb * silb
        g_b = bnorm_ref[:, BRANCH : 2 * BRANCH]
        rb, nb, yb = _rms_fwd(ub, g_b)

        q = proj_ref[:, 5 * BRANCH : 6 * BRANCH]
        gc = proj_ref[:, 6 * BRANCH : 7 * BRANCH].astype(F32)
        q_bf, p_bf, o_parts = [], [], []
        for hd in range(XATTN_HEADS):
            cols = slice(hd * XATTN_HEAD_DIM, (hd + 1) * XATTN_HEAD_DIM)
            qh = q[:, cols]
            sc = _dot(qh, kvt_ref[cols, :]) * scale
            e = jnp.exp(sc - jnp.max(sc, axis=-1, keepdims=True))
            p = e / jnp.sum(e, axis=-1, keepdims=True)
            q_bf.append(qh)
            p_bf.append(p.astype(BF16))
            o_parts.append(_dot(p_bf[hd], kv_ref[:, BRANCH + hd * XATTN_HEAD_DIM : BRANCH + (hd + 1) * XATTN_HEAD_DIM]))
        o = jnp.concatenate(o_parts, axis=1)
        sgc = _sigmoid(gc)
        silc = gc * sgc
        uc = o * silc
        g_c = bnorm_ref[:, 2 * BRANCH : 3 * BRANCH]
        rc, nc, yc = _rms_fwd(uc, g_c)

        out = None
        for b, y_branch in enumerate((ya, yb, yc)):
            rows = slice(b * BRANCH, (b + 1) * BRANCH)
            y_bf = y_branch.astype(BF16)
            y_ref[:, rows] = y_bf
            part = _dot(y_bf, wout_ref[rows, :])
            out = part if out is None else out + part
        gpost = gpost_ref[...]
        r_out = lax.rsqrt(_rowmean(out * out) + EPS)
        on = out * r_out
        err = x_ref[...] + on * gpost - tgt_ref[...]
        loss_ref[...] += 0.5 * jnp.sum(_rowmean(err * err), axis=0, keepdims=True)

        dxo = err * inv_d
        dxo_ref[...] = dxo
        dgp, dout = _rms_bwd(dxo, gpost, r_out, on)
        dgpost_ref[...] += dgp
        dout_bf = dout.astype(BF16)
        dout_ref[...] = dout_bf
        dy = [_dot(dout_bf, wout_t_ref[:, b * BRANCH : (b + 1) * BRANCH]) for b in range(3)]

        dg_a, dua = _rms_bwd(dy[0], g_a, ra, na)
        dg_b, dub = _rms_bwd(dy[1], g_b, rb, nb)
        dg_c, duc = _rms_bwd(dy[2], g_c, rc, nc)
        dbnorm_ref[...] += jnp.concatenate([dg_a, dg_b, dg_c], axis=1)

        dpa = dua * sila
        dga = dua * pa * _dsilu(ga, sga)
        dpscale_ref[...] += _colsum(dpa * pm)
        dpm = dpa * pscale
        dxa_parts, carry_parts = [], []
        for g, w in enumerate(POOL_WINDOWS):
            cols = slice(g * POOL_GROUP_DIM, (g + 1) * POOL_GROUP_DIM)
            dpm_g = dpm[:, cols].astype(BF16)
            dd = _dot(dpm_g, pwt_ref[g])
            dpw_ref[g] += _dot(d_bf[g], dpm_g, TN)
            cg = dd * inv_cnt[g]
            carry_parts.append(cg[0:HALO, :])
            acc = jnp.concatenate([cg, carry_ref[:, cols]], axis=0)
            k = 1
            while k < w:
                acc = acc + pltpu.roll(acc, t + HALO - k, axis=0)
                k *= 2
            dxa_parts.append(acc[0:t, :] - dd)
        carry_ref[...] = jnp.concatenate(carry_parts, axis=1)
        dxa = jnp.concatenate(dxa_parts, axis=1)

        dsb = dub * silb
        dgb = dub * sb * _dsilu(gb, sgb)
        du = dsb * z
        dz = dsb * u
        dz_bf = dz.astype(BF16)
        dvn_rows = []
        dz_sum = None
        for c in range(n_chunks):
            rows = slice(c * SGU_CHUNK, (c + 1) * SGU_CHUNK)
            dz_sum = dz[rows, :] if dz_sum is None else dz_sum + dz[rows, :]
            parts = []
            for hd in range(SGU_HEADS):
                cols = slice(hd * SGU_HEAD_DIM, (hd + 1) * SGU_HEAD_DIM)
                parts.append(_dot(wmt_ref[hd], dz_bf[rows, cols]))
                dwm_ref[hd] += _dot(dz_bf[rows, cols], vn_bf[rows, cols], NT)
            dvn_rows.append(jnp.concatenate(parts, axis=1))
        dzsum_ref[...] += dz_sum
        dvn = dvn_rows[0] if n_chunks == 1 else jnp.concatenate(dvn_rows, axis=0)
        dlng_ref[...] += _colsum(dvn * vhat)
        dlnb_ref[...] += _colsum(dvn)
        dvh = dvn * lng
        dv = rstd * (dvh - _rowmean(dvh) - vhat * _rowmean(dvh * vhat))

        do = duc * silc
        dgc = duc * o * _dsilu(gc, sgc)
        dq_parts = []
        for hd in range(XATTN_HEADS):
            cols = slice(hd * XATTN_HEAD_DIM, (hd + 1) * XATTN_HEAD_DIM)
            vcols = slice(BRANCH + hd * XATTN_HEAD_DIM, BRANCH + (hd + 1) * XATTN_HEAD_DIM)
            do_h = do[:, cols].astype(BF16)
            p = p_bf[hd].astype(F32)
            dp = _dot(do_h, kvt_ref[vcols, :])
            dkv_ref[:, vcols] += _dot(p_bf[hd], do_h, TN)
            ds_bf = (p * (dp - jnp.sum(dp * p, axis=-1, keepdims=True)) * scale).astype(BF16)
            dq_parts.append(_dot(ds_bf, kv_ref[:, cols]))
            dkv_ref[:, cols] += _dot(ds_bf, q_bf[hd], TN)
        dq = jnp.concatenate(dq_parts, axis=1)

        dproj_ref[...] = jnp.concatenate([dxa, dga, du, dv, dgb, dq, dgc], axis=1).astype(BF16)

        @pl.when(i == n_tiles - 1)
        def _():
            keep = lax.broadcasted_iota(jnp.int32, (SGU_CHUNK, SGU_CHUNK), 0) >= lax.broadcasted_iota(
                jnp.int32, (SGU_CHUNK, SGU_CHUNK), 1
            )
            for hd in range(SGU_HEADS):
                dwm_ref[hd] = jnp.where(keep, dwm_ref[hd], 0.0)
                per_pos = dzsum_ref[:, hd * SGU_HEAD_DIM : (hd + 1) * SGU_HEAD_DIM]
                dbias_ref[hd : hd + 1, :] = _colsum(per_pos.T)
            for acc, res in ((dpw_ref, dpw_out), (dwm_ref, dwm_out), (dkv_ref, dkv_out)):
                pltpu.sync_copy(acc, res)

    row_tile = lambda width: pl.BlockSpec((t, width), lambda i: (n_tiles - 1 - i, 0))
    halo_spec = pl.BlockSpec(
        (HALO, BRANCH), lambda i: (jnp.maximum((n_tiles - 1 - i) * halo_blocks_per_tile - 1, 0), 0)
    )
    acc_shapes = [
        (1, 128),
        (1, d),
        (1, MIX_WIDTH),
        (1, BRANCH),
        (1, BRANCH),
        (1, BRANCH),
        pool_w.shape,
        sgu_wm.shape,
        (SGU_HEADS, SGU_CHUNK),
        kv.shape,
    ]
    return pl.pallas_call(
        body,
        name="mix",
        grid=(n_tiles,),
        in_specs=[
            row_tile(IN_WIDTH), halo_spec, row_tile(d), row_tile(d), _resident(kv.shape), _resident(kv_t.shape),
            _resident(wout.shape), _resident(wout_t.shape), _resident(pool_w.shape), _resident(pool_w_t.shape),
            _full((1, BRANCH)), _full((1, BRANCH)), _full((1, BRANCH)), _resident((SGU_CHUNK, BRANCH)),
            _resident(sgu_wm.shape), _resident(sgu_wm_t.shape), _full((1, MIX_WIDTH)), _full((1, d)),
        ],
        out_specs=[row_tile(MIX_WIDTH), row_tile(d), row_tile(d), row_tile(IN_WIDTH)]
        + [ANY if len(a) == 3 or a == kv.shape else _full(a) for a in acc_shapes],
        out_shape=[
            jax.ShapeDtypeStruct((s, MIX_WIDTH), BF16),
            jax.ShapeDtypeStruct((s, d), BF16),
            jax.ShapeDtypeStruct((s, d), F32),
            jax.ShapeDtypeStruct((s, IN_WIDTH), BF16),
        ]
        + [jax.ShapeDtypeStruct(a, F32) for a in acc_shapes],
        scratch_shapes=[
            pltpu.VMEM((HALO, BRANCH), F32), pltpu.VMEM((SGU_CHUNK, BRANCH), F32), pltpu.VMEM(pool_w.shape, F32),
            pltpu.VMEM(sgu_wm.shape, F32), pltpu.VMEM(kv.shape, F32),
        ],
        compiler_params=_params("arbitrary", vmem_limit_bytes=VMEM_LIMIT_MIX_BYTES),
    )(
        proj, proj, x, target, kv, kv_t, wout, wout_t, pool_w, pool_w_t, pool_scale, ln_g, ln_b, sgu_bias, sgu_wm,
        sgu_wm_t, branch_norm, norm_post,
    )


def _position():
    return lax.axis_index("x"), lax.axis_index("y"), lax.axis_index("c")


N_CHIPS = 4


def _landing_shape(kind, a):
    return (N_CHIPS,) + a.shape[2:] if kind == "pair" else a.shape


def _carry_specs(groups):
    arrays = [(kind, a) for kind, arrs in groups for a in arrs]
    scratch = []
    for _, arrs in groups:
        n = len(arrs)
        scratch += [pltpu.SemaphoreType.DMA((n, N_DEV)), pltpu.SemaphoreType.DMA((n, N_DEV)), pltpu.SemaphoreType.DMA((n,))]
    return dict(
        n=len(arrays),
        operands=[a for _, a in arrays],
        in_specs=[ANY] * len(arrays),
        out_specs=[ANY] * len(arrays),
        out_shape=[jax.ShapeDtypeStruct(_landing_shape(kind, a), a.dtype) for kind, a in arrays],
        scratch_shapes=scratch,
    )


def _carry(groups, src, out, sems):
    x, y, c = _position()
    chip = 2 * x + y
    me = 2 * chip + c

    def remote(s, d, send_sems, recv_sems, a, m, to):
        return pltpu.make_async_remote_copy(
            src_ref=s, dst_ref=d, send_sem=send_sems.at[a, m], recv_sem=recv_sems.at[a, m], device_id=to,
            device_id_type=MESH,
        )

    def copies():
        far, near = [], []
        at = 0
        for g, (kind, arrs) in enumerate(groups):
            send_sems, recv_sems, local_sems = sems[3 * g : 3 * g + 3]
            for a in range(len(arrs)):
                s, d = src[at + a], out[at + a]
                if kind == "pair":
                    far.append(remote(s.at[:, 1 - c], d, send_sems, recv_sems, a, 1, (x, y, 1 - c)))
                elif kind == "chips":
                    for m in range(1, N_CHIPS):
                        px, py = x ^ (m >> 1), y ^ (m & 1)
                        far.append(remote(s.at[2 * px + py], d.at[chip], send_sems, recv_sems, a, m, (px, py, c)))
                    near.append(pltpu.make_async_copy(s.at[chip], d.at[chip], local_sems.at[a]))
                else:
                    for m in range(1, N_DEV):
                        px, py, pc = x ^ ((m >> 2) & 1), y ^ ((m >> 1) & 1), c ^ (m & 1)
                        far.append(
                            remote(s.at[4 * px + 2 * py + pc], d.at[me], send_sems, recv_sems, a, m, (px, py, pc))
                        )
                    near.append(pltpu.make_async_copy(s.at[me], d.at[me], local_sems.at[a]))
            at += len(arrs)
        return far, near

    def start():
        far, near = copies()
        for cp in near + far:
            cp.start()

    def finish():
        far, near = copies()
        for cp in far:
            cp.wait_recv()
        for cp in far:
            cp.wait_send()
        for cp in near:
            cp.wait()

    return start, finish


def _exchange(groups, name):
    carried = _carry_specs(groups)
    n_c = carried["n"]

    def body(*refs):
        start, finish = _carry(groups, refs[:n_c], refs[n_c : 2 * n_c], refs[2 * n_c :])
        start()
        finish()

    return pl.pallas_call(
        body,
        name=name,
        in_specs=carried["in_specs"],
        out_specs=carried["out_specs"],
        out_shape=carried["out_shape"],
        scratch_shapes=carried["scratch_shapes"],
    )(*carried["operands"])


def _pair_sum(mine, theirs, name, groups=()):
    n = len(mine)
    carried = _carry_specs(groups)
    n_c = carried["n"]
    core = lax.axis_index("c").astype(jnp.int32).reshape(1)

    def body(core_ref, *refs):
        del core_ref
        own = refs[:n]
        sib = refs[n : 2 * n]
        src = refs[2 * n : 2 * n + n_c]
        out = refs[2 * n + n_c : 3 * n + n_c]
        landed = refs[3 * n + n_c : 3 * n + 2 * n_c]
        start, finish = _carry(groups, src, landed, refs[3 * n + 2 * n_c :])
        b = pl.program_id(0)

        @pl.when(b == 0)
        def _():
            start()

        for a in range(n):
            out[a][...] = (own[a][...].astype(F32) + sib[a][...].astype(F32)).astype(out[a].dtype)

        @pl.when(b == N_CHIPS - 1)
        def _():
            finish()

    block = lambda a: pl.BlockSpec((None,) + a.shape[1:], lambda b, core_ref: (b, 0, 0))
    res = pl.pallas_call(
        body,
        name=name,
        grid_spec=pltpu.PrefetchScalarGridSpec(
            num_scalar_prefetch=1,
            grid=(N_CHIPS,),
            in_specs=[pl.BlockSpec((None, None) + a.shape[2:], lambda b, core_ref: (b, core_ref[0], 0, 0)) for a in mine]
            + [block(a) for a in theirs]
            + carried["in_specs"],
            out_specs=[block(a) for a in theirs] + carried["out_specs"],
            scratch_shapes=carried["scratch_shapes"],
        ),
        out_shape=[jax.ShapeDtypeStruct(a.shape, a.dtype) for a in theirs] + carried["out_shape"],
        compiler_params=_params("arbitrary"),
    )(core, *mine, *theirs, *carried["operands"])
    return res[:n], res[n:]


def _weight_grad(a, b, n_blk, blocked, name, groups):
    s = a.shape[0]
    t = min(TILE_GRAD, s)
    n_t = s // t
    n_pairs = n_blk // 2
    if blocked == "cols":
        k, c = a.shape[1], b.shape[1] // n_blk
        a_spec = pl.BlockSpec((t, k), lambda j, i: (i, 0))
        b_spec = pl.BlockSpec((t, 2 * c), lambda j, i: (i, j))
        acc_shape = (k, 2 * c)
    else:
        k, c = a.shape[1] // n_blk, b.shape[1]
        a_spec = pl.BlockSpec((t, 2 * k), lambda j, i: (i, j))
        b_spec = pl.BlockSpec((t, c), lambda j, i: (i, 0))
        acc_shape = (2 * k, c)
    carried = _carry_specs(groups)
    n_p = carried["n"]

    def body(a_ref, b_ref, *refs):
        src = refs[:n_p]
        o_ref, theirs_ref = refs[n_p : n_p + 2]
        landed = refs[n_p + 2 : 2 * n_p + 2]
        acc_ref, sbuf, pair_send, pair_recv = refs[2 * n_p + 2 : 2 * n_p + 6]
        start, finish = _carry(groups, src, landed, refs[2 * n_p + 6 :])
        j = pl.program_id(0)
        i = pl.program_id(1)
        x, y, c_me = _position()

        def to_sibling(pair):
            return pltpu.make_async_remote_copy(
                src_ref=sbuf.at[1 - c_me], dst_ref=theirs_ref.at[pair], send_sem=pair_send.at[pair],
                recv_sem=pair_recv.at[pair], device_id=(x, y, 1 - c_me), device_id_type=MESH,
            )

        @pl.when(jnp.logical_and(j == 0, i == 0))
        def _():
            start()

        @pl.when(i == 0)
        def _():
            acc_ref[...] = jnp.zeros_like(acc_ref)

        acc_ref[...] += _dot(a_ref[...], b_ref[...], TN)

        @pl.when(i == n_t - 1)
        def _():
            for pair in range(1, n_pairs):

                @pl.when(j == pair)
                def _():
                    to_sibling(pair - 1).wait_send()

            for half in range(2):
                if blocked == "cols":
                    block = acc_ref[:, half * c : (half + 1) * c].astype(BF16)
                else:
                    block = acc_ref[half * k : (half + 1) * k, :].astype(BF16)
                o_ref[half] = block
                sbuf[half] = block
            for pair in range(n_pairs):

                @pl.when(j == pair)
                def _():
                    to_sibling(pair).start()

        @pl.when(jnp.logical_and(j == n_pairs - 1, i == n_t - 1))
        def _():
            to_sibling(n_pairs - 1).wait_send()
            for pair in range(n_pairs):
                to_sibling(pair).wait_recv()
            finish()

    res = pl.pallas_call(
        body,
        name=name,
        grid=(n_pairs, n_t),
        in_specs=[a_spec, b_spec] + carried["in_specs"],
        out_specs=[pl.BlockSpec((2, k, c), lambda j, i: (j, 0, 0)), ANY] + carried["out_specs"],
        out_shape=[jax.ShapeDtypeStruct((n_blk, k, c), BF16), jax.ShapeDtypeStruct((n_pairs, k, c), BF16)]
        + carried["out_shape"],
        scratch_shapes=[
            pltpu.VMEM(acc_shape, F32), pltpu.VMEM((2, k, c), BF16), pltpu.SemaphoreType.DMA((n_pairs,)),
            pltpu.SemaphoreType.DMA((n_pairs,)),
        ]
        + carried["scratch_shapes"],
        compiler_params=_params("arbitrary", "arbitrary"),
    )(a, b, *carried["operands"])
    return res[0], res[1], res[2:]


def _input_grad(dproj, win_t, x, dxo, norm_pre, groups):
    s, d = x.shape
    t = min(TILE_GRAD, s)
    n_t = s // t
    kb = 2 * WIN_BLK
    n_k = win_t.shape[0] // kb
    carried = _carry_specs(groups)
    n_p = carried["n"]

    def body(dp_ref, w_ref, x_ref, dxo_ref, g_ref, *refs):
        src = refs[:n_p]
        gx_ref, dg_ref = refs[n_p : n_p + 2]
        landed = refs[n_p + 2 : 2 * n_p + 2]
        acc_ref = refs[2 * n_p + 2]
        start, finish = _carry(groups, src, landed, refs[2 * n_p + 3 :])
        i = pl.program_id(0)
        j = pl.program_id(1)

        @pl.when(jnp.logical_and(i == 0, j == 0))
        def _():
            start()
            dg_ref[...] = jnp.zeros_like(dg_ref)

        @pl.when(j == 0)
        def _():
            acc_ref[...] = jnp.zeros_like(acc_ref)

        acc_ref[...] += _dot(dp_ref[...], w_ref[...])

        @pl.when(j == n_k - 1)
        def _():
            xv = x_ref[...]
            gain = g_ref[...]
            r = lax.rsqrt(_rowmean(xv * xv) + EPS)
            dgain, dx = _rms_bwd(acc_ref[...], gain, r, xv * r)
            dg_ref[...] += dgain
            gx_ref[...] = dxo_ref[...] + dx

        @pl.when(jnp.logical_and(i == n_t - 1, j == n_k - 1))
        def _():
            finish()

    res = pl.pallas_call(
        body,
        name="input_grad",
        grid=(n_t, n_k),
        in_specs=[
            pl.BlockSpec((t, kb), lambda i, j: (i, j)),
            pl.BlockSpec((kb, d), lambda i, j: (j, 0)),
            pl.BlockSpec((t, d), lambda i, j: (i, 0)),
            pl.BlockSpec((t, d), lambda i, j: (i, 0)),
            _full((1, d)),
        ]
        + carried["in_specs"],
        out_specs=[pl.BlockSpec((t, d), lambda i, j: (i, 0)), _full((1, d))] + carried["out_specs"],
        out_shape=[jax.ShapeDtypeStruct((s, d), F32), jax.ShapeDtypeStruct((1, d), F32)] + carried["out_shape"],
        scratch_shapes=[pltpu.VMEM((t, d), F32)] + carried["scratch_shapes"],
        compiler_params=_params("arbitrary", "arbitrary", vmem_limit_bytes=VMEM_LIMIT_MIX_BYTES),
    )(dproj, win_t, x, dxo, norm_pre, *carried["operands"])
    return res[0], res[1], res[2:]


def _kv_backward(dkv, memn, wkv, mem):
    m, d = mem.shape
    n = wkv.shape[1]

    def body(dkv_ref, memn_ref, w_ref, mem_ref, mem_blk_ref, gw_ref, dg_ref):
        dkv_bf = dkv_ref[...].astype(BF16)
        gw_ref[...] = _dot(memn_ref[...], dkv_bf, TN).astype(BF16)
        dmemn = _dot(dkv_bf, w_ref[...], NT)
        mv = mem_ref[...]
        r = lax.rsqrt(_rowmean(mv * mv) + EPS)
        dg_ref[...] = _colsum(dmemn * (mem_blk_ref[...] * r))

    cols = pl.BlockSpec((m, WKV_BLK), lambda j: (0, j))
    return pl.pallas_call(
        body,
        name="kv_backward",
        grid=(N_DEV,),
        in_specs=[_full((m, n)), cols, pl.BlockSpec((WKV_BLK, n), lambda j: (j, 0)), _full((m, d)), cols],
        out_specs=[pl.BlockSpec((None, WKV_BLK, n), lambda j: (j, 0, 0)), pl.BlockSpec((1, WKV_BLK), lambda j: (0, j))],
        out_shape=[jax.ShapeDtypeStruct((N_DEV, WKV_BLK, n), BF16), jax.ShapeDtypeStruct((1, d), F32)],
        compiler_params=_params("arbitrary"),
    )(dkv, memn, wkv, mem, mem)


def _adamw_math(w, g, m, v):
    m = ADAM_B1 * m + (1.0 - ADAM_B1) * g
    v = ADAM_B2 * v + (1.0 - ADAM_B2) * (g * g)
    m_hat = m / (1.0 - ADAM_B1**ADAM_STEP)
    v_hat = v / (1.0 - ADAM_B2**ADAM_STEP)
    delta = -ADAM_LR * (m_hat / (jnp.sqrt(v_hat) + ADAM_EPS) + ADAM_WD * w)
    return delta, m, v


def _adamw(parts, w, m, v, name):
    r, c = w.shape
    slots = parts.shape[0]
    t = r
    while t * c * 4 > TILE_ADAM_BYTES and t % 16 == 0:
        t //= 2

    def body(p_ref, w_ref, m_ref, v_ref, g_ref, d_ref, nm_ref, nv_ref):
        g = p_ref[0].astype(F32)
        for k in range(1, slots):
            g = g + p_ref[k].astype(F32)
        delta, nm, nv = _adamw_math(w_ref[...], g, m_ref[...], v_ref[...])
        g_ref[...] = g
        d_ref[...] = delta
        nm_ref[...] = nm
        nv_ref[...] = nv

    tile = pl.BlockSpec((t, c), lambda i: (i, 0))
    return pl.pallas_call(
        body,
        name=name,
        grid=(r // t,),
        in_specs=[pl.BlockSpec((slots, t, c), lambda i: (0, i, 0)), tile, tile, tile],
        out_specs=[tile] * 4,
        out_shape=[jax.ShapeDtypeStruct((r, c), F32)] * 4,
        compiler_params=_params("parallel"),
    )(parts, w, m, v)


def _adamw_packed(parts, triples, name):
    slots = parts.shape[0]
    sizes = [w.shape[0] for w, _, _ in triples]
    rest = parts.shape[1] - sum(sizes)

    def total(p_ref, at, rows):
        g = p_ref[0, at : at + rows, :]
        for k in range(1, slots):
            g = g + p_ref[k, at : at + rows, :]
        return g

    def body(p_ref, *refs):
        ins = refs[: 3 * len(triples)]
        outs = refs[3 * len(triples) :]
        at = 0
        for n, rows in enumerate(sizes):
            g = total(p_ref, at, rows)
            w_ref, m_ref, v_ref = ins[3 * n : 3 * n + 3]
            delta, nm, nv = _adamw_math(w_ref[...], g, m_ref[...], v_ref[...])
            for ref, val in zip(outs[4 * n : 4 * n + 4], (g, delta, nm, nv)):
                ref[...] = val
            at += rows
        if rest:
            outs[-1][...] = total(p_ref, at, rest)

    flat = [a for t in triples for a in t]
    res = pl.pallas_call(
        body,
        name=name,
        out_shape=[jax.ShapeDtypeStruct(w.shape, F32) for w, _, _ in triples for _ in range(4)]
        + ([jax.ShapeDtypeStruct((rest, 128), F32)] if rest else []),
        compiler_params=pltpu.CompilerParams(vmem_limit_bytes=VMEM_LIMIT_BYTES),
    )(parts, *flat)
    return [res[4 * n : 4 * n + 4] for n in range(len(triples))], (res[-1] if rest else None)


SMALL = ("norm_pre", "pool_scale", "sgu_ln_g", "sgu_ln_b", "sgu_w", "sgu_b", "mem_norm", "branch_norm", "norm_post")


def _local_view(name, w):
    if name == "sgu_w":
        return w.reshape(SGU_HEADS, SGU_CHUNK, SGU_CHUNK)
    if name == "sgu_b":
        return w.reshape(SGU_HEADS, SGU_CHUNK)
    return w.reshape(1, -1)


def _forward_backward(x, mem, target, shards, small):
    causal = jnp.tril(jnp.ones((SGU_CHUNK, SGU_CHUNK), dtype=bool))
    sgu_wm = jnp.where(causal[None], small["sgu_w"], 0.0).astype(BF16)
    sgu_bias = jnp.repeat(jnp.transpose(small["sgu_b"]), SGU_HEAD_DIM, axis=1)

    proj, h, (win, wkv, pool_all, wout) = _proj_gather(x, small["norm_pre"], shards)
    wout = wout.reshape(MIX_WIDTH, D_MODEL)
    wkv = wkv.reshape(D_MODEL, 2 * BRANCH)
    pool_full = (
        pool_all.reshape(N_DEV, len(POOL_WINDOWS), POOL_BLK, POOL_GROUP_DIM)
        .transpose(1, 0, 2, 3)
        .reshape(len(POOL_WINDOWS), POOL_GROUP_DIM, POOL_GROUP_DIM)
    )
    memn, kv = _kv_forward(mem, small["mem_norm"], wkv)
    (y, dout, dxo, dproj, loss, d_norm_post, d_branch_norm, d_pool_scale, d_ln_g, d_ln_b, d_pool_w, d_sgu_w, d_sgu_b,
     dkv) = _mix(
        proj, x, target, kv, kv.T, wout, wout.T, pool_full, jnp.swapaxes(pool_full, 1, 2), small["pool_scale"],
        small["sgu_ln_g"], small["sgu_ln_b"], sgu_bias, sgu_wm, jnp.swapaxes(sgu_wm, 1, 2), small["branch_norm"],
        small["norm_post"],
    )
    g_wkv, d_mem_norm = _kv_backward(dkv, memn, wkv, mem)
    g_pool = (
        d_pool_w.reshape(len(POOL_WINDOWS), N_DEV, POOL_BLK, POOL_GROUP_DIM)
        .transpose(1, 0, 2, 3)
        .reshape(N_DEV, len(POOL_WINDOWS) * POOL_BLK, POOL_GROUP_DIM)
        .astype(BF16)
    )
    small_grads = dict(
        pool_scale=d_pool_scale, sgu_ln_g=d_ln_g, sgu_ln_b=d_ln_b, sgu_w=d_sgu_w, sgu_b=d_sgu_b,
        mem_norm=d_mem_norm, branch_norm=d_branch_norm, norm_post=d_norm_post,
    )
    packed = jnp.concatenate(
        [small_grads[n].reshape(-1, LANES) for n in SMALL if n != "norm_pre"]
        + [jnp.broadcast_to(loss, (SUBLANES, LANES))],
        axis=0,
    )
    packed = jnp.broadcast_to(packed[None, None], (N_CHIPS, 2) + packed.shape)

    by_chip = lambda g: g.reshape((N_CHIPS, 2) + g.shape[1:])
    small_mine = [by_chip(g_wkv), by_chip(g_pool), packed]
    g_wout, wout_theirs, small_theirs = _weight_grad(y, dout, N_DEV, "rows", "grad_w_out", [("pair", small_mine)])
    sums, _ = _pair_sum(small_mine + [by_chip(g_wout)], list(small_theirs) + [wout_theirs], "pair_sum_first")
    g_win, win_theirs, (l_wkv, l_pool, l_packed, l_wout) = _weight_grad(
        h, dproj, N_DEV, "cols", "grad_w_in", [("chips", list(sums))]
    )
    (win_sums,), _ = _pair_sum([by_chip(g_win)], [win_theirs], "pair_sum_w_in")
    win_late, _ = lax.optimization_barrier((win, dout))
    grad_x, d_norm_pre, (l_win,) = _input_grad(
        dproj, jnp.swapaxes(win_late, 1, 2).reshape(IN_WIDTH, D_MODEL), x, dxo, small["norm_pre"],
        [("chips", [win_sums])],
    )
    return grad_x, dict(w_in=l_win, w_out=l_wout, w_kv=l_wkv, pool_w=l_pool), l_packed, d_norm_pre


def kernel(x, mem, norm_pre, w_in, pool_w, pool_scale, sgu_ln_g, sgu_ln_b, sgu_w, sgu_b, mem_norm, w_kv, branch_norm, w_out, norm_post, loss_target, m_norm_pre, m_w_in, m_pool_w, m_pool_scale, m_sgu_ln_g, m_sgu_ln_b, m_sgu_w, m_sgu_b, m_mem_norm, m_w_kv, m_branch_norm, m_w_out, m_norm_post, v_norm_pre, v_w_in, v_pool_w, v_pool_scale, v_sgu_ln_g, v_sgu_ln_b, v_sgu_w, v_sgu_b, v_mem_norm, v_w_kv, v_branch_norm, v_w_out, v_norm_post):
    weights = dict(norm_pre=norm_pre, w_in=w_in, pool_w=pool_w, pool_scale=pool_scale, sgu_ln_g=sgu_ln_g, sgu_ln_b=sgu_ln_b, sgu_w=sgu_w, sgu_b=sgu_b, mem_norm=mem_norm, w_kv=w_kv, branch_norm=branch_norm, w_out=w_out, norm_post=norm_post)
    first = dict(norm_pre=m_norm_pre, w_in=m_w_in, pool_w=m_pool_w, pool_scale=m_pool_scale, sgu_ln_g=m_sgu_ln_g, sgu_ln_b=m_sgu_ln_b, sgu_w=m_sgu_w, sgu_b=m_sgu_b, mem_norm=m_mem_norm, w_kv=m_w_kv, branch_norm=m_branch_norm, w_out=m_w_out, norm_post=m_norm_post)
    second = dict(norm_pre=v_norm_pre, w_in=v_w_in, pool_w=v_pool_w, pool_scale=v_pool_scale, sgu_ln_g=v_sgu_ln_g, sgu_ln_b=v_sgu_ln_b, sgu_w=v_sgu_w, sgu_b=v_sgu_b, mem_norm=v_mem_norm, w_kv=v_w_kv, branch_norm=v_branch_norm, w_out=v_w_out, norm_post=v_norm_post)
    order = ("norm_pre", "w_in", "pool_w", "pool_scale", "sgu_ln_g", "sgu_ln_b", "sgu_w", "sgu_b", "mem_norm", "w_kv", "branch_norm", "w_out", "norm_post")

    owned_shape = dict(
        w_in=(D_MODEL, WIN_BLK), w_out=(WOUT_BLK, D_MODEL), w_kv=(WKV_BLK, 2 * BRANCH),
        pool_w=(len(POOL_WINDOWS) * POOL_BLK, POOL_GROUP_DIM),
    )
    owned = {n: weights[n].reshape(owned_shape[n]) for n in owned_shape}
    small = {n: _local_view(n, weights[n]) for n in SMALL}
    grad_x, landed, landed_packed, d_norm_pre = _forward_backward(
        x[0], mem[0], loss_target[0], [owned[n].astype(BF16) for n in ("w_in", "w_kv", "pool_w", "w_out")], small
    )
    d_norm_pre = d_norm_pre.reshape(-1, 128)
    (landed_norm_pre,) = _exchange(
        [("all", [jnp.broadcast_to(d_norm_pre[None], (N_DEV,) + d_norm_pre.shape)])], "exchange_norm_pre"
    )

    grads, deltas, new_m, new_v = {}, {}, {}, {}
    for n in owned_shape:
        shape = weights[n].shape
        res = _adamw(
            landed[n], owned[n], first[n].reshape(owned_shape[n]), second[n].reshape(owned_shape[n]), "adamw_" + n
        )
        grads[n], deltas[n], new_m[n], new_v[n] = (a.reshape(shape) for a in res)
    rows_of = lambda tree, n: tree[n].reshape(-1, 128)
    for names, parts, name in (
        ([n for n in SMALL if n != "norm_pre"], landed_packed, "adamw_replicated"),
        (["norm_pre"], landed_norm_pre, "adamw_norm_pre"),
    ):
        res, rest = _adamw_packed(
            parts, [(rows_of(weights, n), rows_of(first, n), rows_of(second, n)) for n in names], name
        )
        if rest is not None:
            total = rest[0, 0]
        for n, four in zip(names, res):
            for tree, a in zip((grads, deltas, new_m, new_v), four):
                tree[n] = a.reshape(weights[n].shape)

    return (
        total,
        grad_x[None],
        *[grads[n] for n in order],
        *[deltas[n] for n in order],
        *[new_m[n] for n in order],
        *[new_v[n] for n in order],
    )
```

```python
import jax
import jax.numpy as jnp
from jax import lax
from jax.experimental import pallas as pl
from jax.experimental.pallas import tpu as pltpu

F32 = jnp.float32
BF16 = jnp.bfloat16
EPS = 1e-6

D_MODEL = 2048
POOL_WINDOWS = (2, 4, 8, 16)
POOL_GROUP_DIM = 256
BRANCH = 1024
SGU_CHUNK = 128
SGU_HEADS = 8
SGU_HEAD_DIM = 128
XATTN_HEADS = 4
XATTN_HEAD_DIM = 256
MIX_WIDTH = 3 * BRANCH
IN_WIDTH = 7 * BRANCH
N_DEV = 8
WIN_BLK = IN_WIDTH // N_DEV
WOUT_BLK = MIX_WIDTH // N_DEV
WKV_BLK = D_MODEL // N_DEV
POOL_BLK = POOL_GROUP_DIM // N_DEV
HALO = 16
LANES = 128
SUBLANES = 8

ADAM_LR = 0.001
ADAM_B1 = 0.9
ADAM_B2 = 0.999
ADAM_EPS = 1e-08
ADAM_WD = 0.01
ADAM_STEP = 10

VMEM_LIMIT_BYTES = 56 * 1024 * 1024
VMEM_LIMIT_MIX_BYTES = 63 * 1024 * 1024

TILE_PROJ = 512
TILE_MIX = 128
TILE_GRAD = 512
TILE_WEIGHT_GRAD = 1024
TILE_ADAM_BYTES = 1 << 20

ANY = pl.BlockSpec(memory_space=pl.ANY)
NN = (((1,), (0,)), ((), ()))
NT = (((1,), (1,)), ((), ()))
TN = (((0,), (0,)), ((), ()))
MESH = pl.DeviceIdType.MESH


def _dot(a, b, dims=NN):
    return lax.dot_general(a, b, dims, preferred_element_type=F32)


def _params(*semantics, vmem_limit_bytes=VMEM_LIMIT_BYTES):
    return pltpu.CompilerParams(dimension_semantics=semantics, vmem_limit_bytes=vmem_limit_bytes)


def _rowmean(a):
    return jnp.mean(a, axis=-1, keepdims=True)


def _colsum(a):
    return jnp.sum(a, axis=0, keepdims=True)


def _full(shape):
    zeros = (0,) * len(shape)
    return pl.BlockSpec(shape, lambda *_: zeros)


def _resident(shape):
    zeros = (0,) * len(shape)
    return pl.BlockSpec(shape, lambda *_: zeros, pipeline_mode=pl.Buffered(1))


def _kv_forward(mem, mem_norm, wkv):
    m, d = mem.shape
    n = wkv.shape[1]
    cols = 4 * LANES

    def body(mem_ref, g_ref, w_ref, memn_ref, kv_ref):
        mv = mem_ref[...]
        r = lax.rsqrt(_rowmean(mv * mv) + EPS)
        memn = (mv * r * g_ref[...]).astype(BF16)
        memn_ref[...] = memn
        kv_ref[...] = _dot(memn, w_ref[...]).astype(BF16)

    return pl.pallas_call(
        body,
        name="kv_forward",
        grid=(n // cols,),
        in_specs=[_full((m, d)), _full((1, d)), pl.BlockSpec((d, cols), lambda j: (0, j))],
        out_specs=[_full((m, d)), pl.BlockSpec((m, cols), lambda j: (0, j))],
        out_shape=[jax.ShapeDtypeStruct((m, d), BF16), jax.ShapeDtypeStruct((m, n), BF16)],
        compiler_params=_params("arbitrary"),
    )(mem, mem_norm, wkv)


def _proj_gather(x_in, norm_pre, shards):
    s, d = x_in.shape
    t = min(TILE_PROJ, s)
    n_t = s // t
    n_arr = len(shards)

    def places(x, y, c):
        return (x, y, c), (x, y, 1 - c), (x ^ c, y ^ (1 - c)), (x ^ (1 - c), y ^ c), (1 - x, 1 - y)

    def index(chip, core):
        return 4 * chip[0] + 2 * chip[1] + core

    _, _, chip_a, chip_b, chip_d = places(*_position())
    c_out = lax.axis_index("c")
    me_out = index((lax.axis_index("x"), lax.axis_index("y")), c_out)
    order = jnp.stack(
        [
            me_out, me_out ^ 1, index(chip_a, c_out), index(chip_b, c_out), index(chip_b, 1 - c_out),
            index(chip_a, 1 - c_out), index(chip_d, c_out), index(chip_d, 1 - c_out),
        ]
    ).astype(jnp.int32)

    def body(order_ref, x_ref, g_ref, *refs):
        del order_ref
        src = refs[:n_arr]
        proj_ref, h_ref = refs[n_arr : n_arr + 2]
        out = refs[n_arr + 2 : 2 * n_arr + 2]
        wbuf, hs, send_sems, recv_sems, local_sems, load_sems = refs[2 * n_arr + 2 :]
        j = pl.program_id(0)
        i = pl.program_id(1)
        me, sibling, chip_a, chip_b, chip_d = places(*_position())
        c = me[2]

        def block(a, chip, core):
            return out[a].at[index(chip, core)]

        def copy(a, k, owner, to, from_input=False):
            return pltpu.make_async_remote_copy(
                src_ref=src[a] if from_input else block(a, *owner),
                dst_ref=block(a, *owner),
                send_sem=send_sems.at[a, k],
                recv_sem=recv_sems.at[a, k],
                device_id=to,
                device_id_type=MESH,
            )

        mine = (me[:2], c)

        def own(a):
            return pltpu.make_async_copy(src[a], block(a, *mine), local_sems.at[a])

        def first_sends(a):
            return [
                copy(a, 0, mine, sibling, from_input=True),
                copy(a, 1, mine, (*chip_a, c), from_input=True),
                copy(a, 2, mine, (*chip_b, c), from_input=True),
            ]

        def onward(a, k):
            owner = {3: chip_a, 4: chip_a, 5: chip_b, 6: chip_d}[k]
            return copy(a, k, (owner, c), (*chip_b, c) if k == 3 else sibling)

        def landed(a, k):
            owner = {0: mine[0], 1: chip_a, 2: chip_b, 3: chip_d, 4: chip_b, 5: chip_a, 6: chip_d}[k]
            core = c if k in (1, 2, 3) else 1 - c
            copy(a, k, (owner, core), me).wait_recv()
            return owner, core

        def load(ref, step):
            return pltpu.make_async_copy(ref, wbuf.at[step % 2], load_sems.at[step % 2])

        @pl.when(jnp.logical_and(j == 0, i == 0))
        def _():
            own(0).start()
            for cp in first_sends(0):
                cp.start()
            load(src[0], 0).start()
            load(src[0], 0).wait()

        steps = {1: (0, ()), 2: (1, (3, 4)), 3: (2, (5,)), 4: (4, ()), 5: (5, ()), 6: (3, (6,)), 7: (6, ())}
        for step, (k, then) in steps.items():

            @pl.when(jnp.logical_and(j == step, i == 0))
            def _():
                load(src[0], step).wait()

            @pl.when(jnp.logical_and(j == step - 1, i == n_t - 1))
            def _():
                owner = landed(0, k)
                for k2 in then:
                    onward(0, k2).start()
                if k == 1:
                    for a in range(1, n_arr):
                        own(a).start()
                        for cp in first_sends(a):
                            cp.start()
                if k == 3:
                    for a in range(1, n_arr):
                        for k1, then1 in ((1, (3, 4)), (2, (5,))):
                            landed(a, k1)
                            for k2 in then1:
                                onward(a, k2).start()
                load(block(0, *owner), step).start()

        @pl.when(j == 0)
        def _():
            xv = x_ref[...]
            h = (xv * lax.rsqrt(_rowmean(xv * xv) + EPS) * g_ref[...]).astype(BF16)
            hs[i] = h
            h_ref[...] = h

        proj_ref[...] = _dot(hs[i], wbuf[j % 2]).astype(BF16)

        @pl.when(jnp.logical_and(j == N_DEV - 1, i == n_t - 1))
        def _():
            for a in range(1, n_arr):
                landed(a, 3)
                onward(a, 6).start()
            for a in range(1, n_arr):
                for k in (0, 4, 5, 6):
                    landed(a, k)
            for a in range(n_arr):
                for cp in first_sends(a) + [onward(a, k) for k in (3, 4, 5, 6)]:
                    cp.wait_send()
                own(a).wait()

    res = pl.pallas_call(
        body,
        name="proj_gather",
        grid_spec=pltpu.PrefetchScalarGridSpec(
            num_scalar_prefetch=1,
            grid=(N_DEV, n_t),
            in_specs=[
                pl.BlockSpec((t, d), lambda j, i, order_ref: (jnp.where(j == 0, i, n_t - 1), 0)),
                pl.BlockSpec((1, d), lambda j, i, order_ref: (0, 0)),
            ]
            + [ANY] * n_arr,
            out_specs=[
                pl.BlockSpec((t, WIN_BLK), lambda j, i, order_ref: (i, order_ref[j])),
                pl.BlockSpec((t, d), lambda j, i, order_ref: (jnp.where(j == 0, i, n_t - 1), 0)),
            ]
            + [ANY] * n_arr,
            scratch_shapes=[
                pltpu.VMEM((2,) + shards[0].shape, BF16),
                pltpu.VMEM((n_t, t, d), BF16),
                pltpu.SemaphoreType.DMA((n_arr, 7)),
                pltpu.SemaphoreType.DMA((n_arr, 7)),
                pltpu.SemaphoreType.DMA((n_arr,)),
                pltpu.SemaphoreType.DMA((2,)),
            ],
        ),
        out_shape=[jax.ShapeDtypeStruct((s, IN_WIDTH), BF16), jax.ShapeDtypeStruct((s, d), BF16)]
        + [jax.ShapeDtypeStruct((N_DEV,) + a.shape, a.dtype) for a in shards],
        compiler_params=_params("arbitrary", "arbitrary"),
    )(order, x_in, norm_pre, *shards)
    return res[0], res[1], res[2:]


def _sigmoid(a):
    return jax.nn.sigmoid(a)


def _dsilu(a, sg):
    return sg * (1.0 + a * (1.0 - sg))


def _rms_fwd(u, gain):
    r = lax.rsqrt(_rowmean(u * u) + EPS)
    n = u * r
    return r, n, n * gain


def _rms_bwd(dy, gain, r, n):
    dn = dy * gain
    return _colsum(dy * n), r * (dn - n * _rowmean(dn * n))


def _mix(proj, x, target, kv, kv_t, wout, wout_t, pool_w, pool_w_t, pool_scale, ln_g, ln_b, sgu_bias, sgu_wm, sgu_wm_t, branch_norm, norm_post):
    s, d = x.shape
    t = min(TILE_MIX, s)
    n_tiles = s // t
    n_chunks = t // SGU_CHUNK
    halo_blocks_per_tile = t // HALO
    inv_d = 1.0 / d
    scale = 1.0 / (XATTN_HEAD_DIM**0.5)

    def body(
        proj_ref, halo_ref, x_ref, tgt_ref, kv_ref, kvt_ref, wout_ref, wout_t_ref, pw_ref, pwt_ref, pscale_ref, lng_ref,
        lnb_ref, bias_ref, wm_ref, wmt_ref, bnorm_ref, gpost_ref,
        y_ref, dout_ref, dxo_ref, dproj_ref, loss_ref, dgpost_ref, dbnorm_ref, dpscale_ref, dlng_ref, dlnb_ref,
        dpw_out, dwm_out, dbias_ref, dkv_out,
        carry_ref, dzsum_ref, dpw_ref, dwm_ref, dkv_ref,
    ):
        i = pl.program_id(0)
        tile = n_tiles - 1 - i

        @pl.when(i == 0)
        def _():
            carry_ref[...] = jnp.zeros_like(carry_ref)
            dzsum_ref[...] = jnp.zeros_like(dzsum_ref)
            for ref in (loss_ref, dgpost_ref, dbnorm_ref, dpscale_ref, dlng_ref, dlnb_ref, dpw_ref, dwm_ref, dkv_ref):
                ref[...] = jnp.zeros_like(ref)

        t_glob = tile * t + lax.broadcasted_iota(jnp.int32, (t, 1), 0)
        inv_cnt = [1.0 / jnp.minimum(t_glob + 1, w).astype(F32) for w in POOL_WINDOWS]

        xa = proj_ref[:, 0:BRANCH].astype(F32)
        ga = proj_ref[:, BRANCH : 2 * BRANCH].astype(F32)
        halo = jnp.where(tile == 0, 0.0, halo_ref[...].astype(F32))
        d_bf, pm_parts = [], []
        for g, w in enumerate(POOL_WINDOWS):
            cols = slice(g * POOL_GROUP_DIM, (g + 1) * POOL_GROUP_DIM)
            acc = jnp.concatenate([halo[:, cols], xa[:, cols]], axis=0)
            k = 1
            while k < w:
                acc = acc + pltpu.roll(acc, k, axis=0)
                k *= 2
            dg = (acc[HALO:, :] * inv_cnt[g] - xa[:, cols]).astype(BF16)
            d_bf.append(dg)
            pm_parts.append(_dot(dg, pw_ref[g]))
        pm = jnp.concatenate(pm_parts, axis=1)
        pscale = pscale_ref[...]
        pa = pm * pscale
        sga = _sigmoid(ga)
        sila = ga * sga
        ua = pa * sila
        g_a = bnorm_ref[:, 0:BRANCH]
        ra, na, ya = _rms_fwd(ua, g_a)

        u = proj_ref[:, 2 * BRANCH : 3 * BRANCH].astype(F32)
        v = proj_ref[:, 3 * BRANCH : 4 * BRANCH].astype(F32)
        gb = proj_ref[:, 4 * BRANCH : 5 * BRANCH].astype(F32)
        lng = lng_ref[...]
        vc = v - _rowmean(v)
        rstd = lax.rsqrt(_rowmean(vc * vc) + EPS)
        vhat = vc * rstd
        vn_bf = (vhat * lng + lnb_ref[...]).astype(BF16)
        z_rows = []
        for c in range(n_chunks):
            rows = slice(c * SGU_CHUNK, (c + 1) * SGU_CHUNK)
            z_rows.append(
                jnp.concatenate(
                    [
                        _dot(wm_ref[hd], vn_bf[rows, hd * SGU_HEAD_DIM : (hd + 1) * SGU_HEAD_DIM])
                        for hd in range(SGU_HEADS)
                    ],
                    axis=1,
                )
                + bias_ref[...]
            )
        z = z_rows[0] if n_chunks == 1 else jnp.concatenate(z_rows, axis=0)
        sb = u * z
        sgb = _sigmoid(gb)
        silb = gb * sgb
        ub = sb * silb
        g_b = bnorm_ref[:, BRANCH : 2 * BRANCH]
        rb, nb, yb = _rms_fwd(ub, g_b)

        q = proj_ref[:, 5 * BRANCH : 6 * BRANCH]
        gc = proj_ref[:, 6 * BRANCH : 7 * BRANCH].astype(F32)
        q_bf, p_bf, o_parts = [], [], []
        for hd in range(XATTN_HEADS):
            cols = slice(hd * XATTN_HEAD_DIM, (hd + 1) * XATTN_HEAD_DIM)
            qh = q[:, cols]
            sc = _dot(qh, kvt_ref[cols, :]) * scale
            e = jnp.exp(sc - jnp.max(sc, axis=-1, keepdims=True))
            p = e / jnp.sum(e, axis=-1, keepdims=True)
            q_bf.append(qh)
            p_bf.append(p.astype(BF16))
            o_parts.append(_dot(p_bf[hd], kv_ref[:, BRANCH + hd * XATTN_HEAD_DIM : BRANCH + (hd + 1) * XATTN_HEAD_DIM]))
        o = jnp.concatenate(o_parts, axis=1)
        sgc = _sigmoid(gc)
        silc = gc * sgc
        uc = o * silc
        g_c = bnorm_ref[:, 2 * BRANCH : 3 * BRANCH]
        rc, nc, yc = _rms_fwd(uc, g_c)

        out = None
        for b, y_branch in enumerate((ya, yb, yc)):
            rows = slice(b * BRANCH, (b + 1) * BRANCH)
            y_bf = y_branch.astype(BF16)
            y_ref[:, rows] = y_bf
            part = _dot(y_bf, wout_ref[rows, :])
            out = part if out is None else out + part
        gpost = gpost_ref[...]
        r_out = lax.rsqrt(_rowmean(out * out) + EPS)
        on = out * r_out
        err = x_ref[...] + on * gpost - tgt_ref[...]
        loss_ref[...] += 0.5 * jnp.sum(_rowmean(err * err), axis=0, keepdims=True)

        dxo = err * inv_d
        dxo_ref[...] = dxo
        dgp, dout = _rms_bwd(dxo, gpost, r_out, on)
        dgpost_ref[...] += dgp
        dout_bf = dout.astype(BF16)
        dout_ref[...] = dout_bf
        dy = [_dot(dout_bf, wout_t_ref[:, b * BRANCH : (b + 1) * BRANCH]) for b in range(3)]

        dg_a, dua = _rms_bwd(dy[0], g_a, ra, na)
        dg_b, dub = _rms_bwd(dy[1], g_b, rb, nb)
        dg_c, duc = _rms_bwd(dy[2], g_c, rc, nc)
        dbnorm_ref[...] += jnp.concatenate([dg_a, dg_b, dg_c], axis=1)

        dpa = dua * sila
        dga = dua * pa * _dsilu(ga, sga)
        dpscale_ref[...] += _colsum(dpa * pm)
        dpm = dpa * pscale
        dxa_parts, carry_parts = [], []
        for g, w in enumerate(POOL_WINDOWS):
            cols = slice(g * POOL_GROUP_DIM, (g + 1) * POOL_GROUP_DIM)
            dpm_g = dpm[:, cols].astype(BF16)
            dd = _dot(dpm_g, pwt_ref[g])
            dpw_ref[g] += _dot(d_bf[g], dpm_g, TN)
            cg = dd * inv_cnt[g]
            carry_parts.append(cg[0:HALO, :])
            acc = jnp.concatenate([cg, carry_ref[:, cols]], axis=0)
            k = 1
            while k < w:
                acc = acc + pltpu.roll(acc, t + HALO - k, axis=0)
                k *= 2
            dxa_parts.append(acc[0:t, :] - dd)
        carry_ref[...] = jnp.concatenate(carry_parts, axis=1)
        dxa = jnp.concatenate(dxa_parts, axis=1)

        dsb = dub * silb
        dgb = dub * sb * _dsilu(gb, sgb)
        du = dsb * z
        dz = dsb * u
        dz_bf = dz.astype(BF16)
        dvn_rows = []
        dz_sum = None
        for c in range(n_chunks):
            rows = slice(c * SGU_CHUNK, (c + 1) * SGU_CHUNK)
            dz_sum = dz[rows, :] if dz_sum is None else dz_sum + dz[rows, :]
            parts = []
            for hd in range(SGU_HEADS):
                cols = slice(hd * SGU_HEAD_DIM, (hd + 1) * SGU_HEAD_DIM)
                parts.append(_dot(wmt_ref[hd], dz_bf[rows, cols]))
                dwm_ref[hd] += _dot(dz_bf[rows, cols], vn_bf[rows, cols], NT)
            dvn_rows.append(jnp.concatenate(parts, axis=1))
        dzsum_ref[...] += dz_sum
        dvn = dvn_rows[0] if n_chunks == 1 else jnp.concatenate(dvn_rows, axis=0)
        dlng_ref[...] += _colsum(dvn * vhat)
        dlnb_ref[...] += _colsum(dvn)
        dvh = dvn * lng
        dv = rstd * (dvh - _rowmean(dvh) - vhat * _rowmean(dvh * vhat))

        do = duc * silc
        dgc = duc * o * _dsilu(gc, sgc)
        dq_parts = []
        for hd in range(XATTN_HEADS):
            cols = slice(hd * XATTN_HEAD_DIM, (hd + 1) * XATTN_HEAD_DIM)
            vcols = slice(BRANCH + hd * XATTN_HEAD_DIM, BRANCH + (hd + 1) * XATTN_HEAD_DIM)
            do_h = do[:, cols].astype(BF16)
            p = p_bf[hd].astype(F32)
            dp = _dot(do_h, kvt_ref[vcols, :])
            dkv_ref[:, vcols] += _dot(p_bf[hd], do_h, TN)
            ds_bf = (p * (dp - jnp.sum(dp * p, axis=-1, keepdims=True)) * scale).astype(BF16)
            dq_parts.append(_dot(ds_bf, kv_ref[:, cols]))
            dkv_ref[:, cols] += _dot(ds_bf, q_bf[hd], TN)
        dq = jnp.concatenate(dq_parts, axis=1)

        dproj_ref[...] = jnp.concatenate([dxa, dga, du, dv, dgb, dq, dgc], axis=1).astype(BF16)

        @pl.when(i == n_tiles - 1)
        def _():
            keep = lax.broadcasted_iota(jnp.int32, (SGU_CHUNK, SGU_CHUNK), 0) >= lax.broadcasted_iota(
                jnp.int32, (SGU_CHUNK, SGU_CHUNK), 1
            )
            for hd in range(SGU_HEADS):
                dwm_ref[hd] = jnp.where(keep, dwm_ref[hd], 0.0)
                per_pos = dzsum_ref[:, hd * SGU_HEAD_DIM : (hd + 1) * SGU_HEAD_DIM]
                dbias_ref[hd : hd + 1, :] = _colsum(per_pos.T)
            for acc, res in ((dpw_ref, dpw_out), (dwm_ref, dwm_out), (dkv_ref, dkv_out)):
                pltpu.sync_copy(acc, res)

    row_tile = lambda width: pl.BlockSpec((t, width), lambda i: (n_tiles - 1 - i, 0))
    halo_spec = pl.BlockSpec(
        (HALO, BRANCH), lambda i: (jnp.maximum((n_tiles - 1 - i) * halo_blocks_per_tile - 1, 0), 0)
    )
    acc_shapes = [
        (1, 128),
        (1, d),
        (1, MIX_WIDTH),
        (1, BRANCH),
        (1, BRANCH),
        (1, BRANCH),
        pool_w.shape,
        sgu_wm.shape,
        (SGU_HEADS, SGU_CHUNK),
        kv.shape,
    ]
    return pl.pallas_call(
        body,
        name="mix",
        grid=(n_tiles,),
        in_specs=[
            row_tile(IN_WIDTH), halo_spec, row_tile(d), row_tile(d), _resident(kv.shape), _resident(kv_t.shape),
            _resident(wout.shape), _resident(wout_t.shape), _resident(pool_w.shape), _resident(pool_w_t.shape),
            _full((1, BRANCH)), _full((1, BRANCH)), _full((1, BRANCH)), _resident((SGU_CHUNK, BRANCH)),
            _resident(sgu_wm.shape), _resident(sgu_wm_t.shape), _full((1, MIX_WIDTH)), _full((1, d)),
        ],
        out_specs=[row_tile(MIX_WIDTH), row_tile(d), row_tile(d), row_tile(IN_WIDTH)]
        + [ANY if len(a) == 3 or a == kv.shape else _full(a) for a in acc_shapes],
        out_shape=[
            jax.ShapeDtypeStruct((s, MIX_WIDTH), BF16),
            jax.ShapeDtypeStruct((s, d), BF16),
            jax.ShapeDtypeStruct((s, d), F32),
            jax.ShapeDtypeStruct((s, IN_WIDTH), BF16),
        ]
        + [jax.ShapeDtypeStruct(a, F32) for a in acc_shapes],
        scratch_shapes=[
            pltpu.VMEM((HALO, BRANCH), F32), pltpu.VMEM((SGU_CHUNK, BRANCH), F32), pltpu.VMEM(pool_w.shape, F32),
            pltpu.VMEM(sgu_wm.shape, F32), pltpu.VMEM(kv.shape, F32),
        ],
        compiler_params=_params("arbitrary", vmem_limit_bytes=VMEM_LIMIT_MIX_BYTES),
    )(
        proj, proj, x, target, kv, kv_t, wout, wout_t, pool_w, pool_w_t, pool_scale, ln_g, ln_b, sgu_bias, sgu_wm,
        sgu_wm_t, branch_norm, norm_post,
    )


def _position():
    return lax.axis_index("x"), lax.axis_index("y"), lax.axis_index("c")


N_CHIPS = 4


def _landing_shape(kind, a):
    return (N_CHIPS,) + a.shape[2:] if kind == "pair" else a.shape


def _carry_specs(groups):
    arrays = [(kind, a) for kind, arrs in groups for a in arrs]
    scratch = []
    for _, arrs in groups:
        n = len(arrs)
        scratch += [pltpu.SemaphoreType.DMA((n, N_DEV)), pltpu.SemaphoreType.DMA((n, N_DEV)), pltpu.SemaphoreType.DMA((n,))]
    return dict(
        n=len(arrays),
        operands=[a for _, a in arrays],
        in_specs=[ANY] * len(arrays),
        out_specs=[ANY] * len(arrays),
        out_shape=[jax.ShapeDtypeStruct(_landing_shape(kind, a), a.dtype) for kind, a in arrays],
        scratch_shapes=scratch,
    )


def _carry(groups, src, out, sems):
    x, y, c = _position()
    chip = 2 * x + y
    me = 2 * chip + c

    def remote(s, d, send_sems, recv_sems, a, m, to):
        return pltpu.make_async_remote_copy(
            src_ref=s, dst_ref=d, send_sem=send_sems.at[a, m], recv_sem=recv_sems.at[a, m], device_id=to,
            device_id_type=MESH,
        )

    def copies():
        far, near = [], []
        at = 0
        for g, (kind, arrs) in enumerate(groups):
            send_sems, recv_sems, local_sems = sems[3 * g : 3 * g + 3]
            for a in range(len(arrs)):
                s, d = src[at + a], out[at + a]
                if kind == "pair" and s.shape[0] == 1:
                    for b in range(N_CHIPS):
                        far.append(remote(s.at[0, 0], d.at[b], send_sems, recv_sems, a, 1 + b, (x, y, 1 - c)))
                elif kind == "pair":
                    far.append(remote(s.at[:, 1 - c], d, send_sems, recv_sems, a, 1, (x, y, 1 - c)))
                elif kind == "chips":
                    for m in range(1, N_CHIPS):
                        px, py = x ^ (m >> 1), y ^ (m & 1)
                        far.append(remote(s.at[2 * px + py], d.at[chip], send_sems, recv_sems, a, m, (px, py, c)))
                    near.append(pltpu.make_async_copy(s.at[chip], d.at[chip], local_sems.at[a]))
                else:
                    for m in range(1, N_DEV):
                        px, py, pc = x ^ ((m >> 2) & 1), y ^ ((m >> 1) & 1), c ^ (m & 1)
                        far.append(
                            remote(s.at[4 * px + 2 * py + pc], d.at[me], send_sems, recv_sems, a, m, (px, py, pc))
                        )
                    near.append(pltpu.make_async_copy(s.at[me], d.at[me], local_sems.at[a]))
            at += len(arrs)
        return far, near

    def start():
        far, near = copies()
        for cp in near + far:
            cp.start()

    def finish():
        far, near = copies()
        for cp in far:
            cp.wait_recv()
        for cp in far:
            cp.wait_send()
        for cp in near:
            cp.wait()

    return start, finish


def _exchange(groups, name):
    carried = _carry_specs(groups)
    n_c = carried["n"]

    def body(*refs):
        start, finish = _carry(groups, refs[:n_c], refs[n_c : 2 * n_c], refs[2 * n_c :])
        start()
        finish()

    return pl.pallas_call(
        body,
        name=name,
        in_specs=carried["in_specs"],
        out_specs=carried["out_specs"],
        out_shape=carried["out_shape"],
        scratch_shapes=carried["scratch_shapes"],
    )(*carried["operands"])


def _pair_sum(mine, theirs, name, groups=()):
    n = len(mine)
    carried = _carry_specs(groups)
    n_c = carried["n"]
    core = lax.axis_index("c").astype(jnp.int32).reshape(1)

    def body(core_ref, *refs):
        del core_ref
        own = refs[:n]
        sib = refs[n : 2 * n]
        src = refs[2 * n : 2 * n + n_c]
        out = refs[2 * n + n_c : 3 * n + n_c]
        landed = refs[3 * n + n_c : 3 * n + 2 * n_c]
        start, finish = _carry(groups, src, landed, refs[3 * n + 2 * n_c :])
        b = pl.program_id(0)

        @pl.when(b == 0)
        def _():
            start()

        for a in range(n):
            out[a][...] = (own[a][...].astype(F32) + sib[a][...].astype(F32)).astype(out[a].dtype)

        @pl.when(b == N_CHIPS - 1)
        def _():
            finish()

    block = lambda a: pl.BlockSpec((None,) + a.shape[1:], lambda b, core_ref: (b, 0, 0))
    res = pl.pallas_call(
        body,
        name=name,
        grid_spec=pltpu.PrefetchScalarGridSpec(
            num_scalar_prefetch=1,
            grid=(N_CHIPS,),
            in_specs=[
                pl.BlockSpec((None, None) + a.shape[2:], lambda b, core_ref: (0, 0, 0, 0))
                if a.shape[0] == 1
                else pl.BlockSpec((None, None) + a.shape[2:], lambda b, core_ref: (b, core_ref[0], 0, 0))
                for a in mine
            ]
            + [block(a) for a in theirs]
            + carried["in_specs"],
            out_specs=[block(a) for a in theirs] + carried["out_specs"],
            scratch_shapes=carried["scratch_shapes"],
        ),
        out_shape=[jax.ShapeDtypeStruct(a.shape, a.dtype) for a in theirs] + carried["out_shape"],
        compiler_params=_params("arbitrary"),
    )(core, *mine, *theirs, *carried["operands"])
    return res[:n], res[n:]


def _weight_grad(a, b, n_blk, blocked, name, groups):
    s = a.shape[0]
    t = min(TILE_WEIGHT_GRAD, s)
    n_t = s // t
    n_pairs = n_blk // 2
    if blocked == "cols":
        k, c = a.shape[1], b.shape[1] // n_blk
        a_spec = pl.BlockSpec((t, k), lambda j, i: (i, 0))
        b_spec = pl.BlockSpec((t, 2 * c), lambda j, i: (i, j))
        acc_shape = (k, 2 * c)
    else:
        k, c = a.shape[1] // n_blk, b.shape[1]
        a_spec = pl.BlockSpec((t, 2 * k), lambda j, i: (i, j))
        b_spec = pl.BlockSpec((t, c), lambda j, i: (i, 0))
        acc_shape = (2 * k, c)
    carried = _carry_specs(groups)
    n_p = carried["n"]

    def body(a_ref, b_ref, *refs):
        src = refs[:n_p]
        o_ref, theirs_ref = refs[n_p : n_p + 2]
        landed = refs[n_p + 2 : 2 * n_p + 2]
        acc_ref, sbuf, pair_send, pair_recv = refs[2 * n_p + 2 : 2 * n_p + 6]
        start, finish = _carry(groups, src, landed, refs[2 * n_p + 6 :])
        j = pl.program_id(0)
        i = pl.program_id(1)
        x, y, c_me = _position()

        def to_sibling(pair):
            return pltpu.make_async_remote_copy(
                src_ref=sbuf.at[1 - c_me], dst_ref=theirs_ref.at[pair], send_sem=pair_send.at[pair],
                recv_sem=pair_recv.at[pair], device_id=(x, y, 1 - c_me), device_id_type=MESH,
            )

        @pl.when(jnp.logical_and(j == 0, i == 0))
        def _():
            start()

        @pl.when(i == 0)
        def _():
            acc_ref[...] = jnp.zeros_like(acc_ref)

        acc_ref[...] += _dot(a_ref[...], b_ref[...], TN)

        @pl.when(i == n_t - 1)
        def _():
            for pair in range(1, n_pairs):

                @pl.when(j == pair)
                def _():
                    to_sibling(pair - 1).wait_send()

            for half in range(2):
                if blocked == "cols":
                    block = acc_ref[:, half * c : (half + 1) * c].astype(BF16)
                else:
                    block = acc_ref[half * k : (half + 1) * k, :].astype(BF16)
                o_ref[half] = block
                sbuf[half] = block
            for pair in range(n_pairs):

                @pl.when(j == pair)
                def _():
                    to_sibling(pair).start()

        @pl.when(jnp.logical_and(j == n_pairs - 1, i == n_t - 1))
        def _():
            to_sibling(n_pairs - 1).wait_send()
            for pair in range(n_pairs):
                to_sibling(pair).wait_recv()
            finish()

    res = pl.pallas_call(
        body,
        name=name,
        grid=(n_pairs, n_t),
        in_specs=[a_spec, b_spec] + carried["in_specs"],
        out_specs=[pl.BlockSpec((2, k, c), lambda j, i: (j, 0, 0)), ANY] + carried["out_specs"],
        out_shape=[jax.ShapeDtypeStruct((n_blk, k, c), BF16), jax.ShapeDtypeStruct((n_pairs, k, c), BF16)]
        + carried["out_shape"],
        scratch_shapes=[
            pltpu.VMEM(acc_shape, F32), pltpu.VMEM((2, k, c), BF16), pltpu.SemaphoreType.DMA((n_pairs,)),
            pltpu.SemaphoreType.DMA((n_pairs,)),
        ]
        + carried["scratch_shapes"],
        compiler_params=_params("arbitrary", "arbitrary", vmem_limit_bytes=VMEM_LIMIT_MIX_BYTES),
    )(a, b, *carried["operands"])
    return res[0], res[1], res[2:]


def _input_grad(dproj, win_t, x, dxo, norm_pre, groups):
    s, d = x.shape
    t = min(TILE_GRAD, s)
    n_t = s // t
    kb = 2 * WIN_BLK
    n_k = win_t.shape[0] // kb
    carried = _carry_specs(groups)
    n_p = carried["n"]

    def body(dp_ref, w_ref, x_ref, dxo_ref, g_ref, *refs):
        src = refs[:n_p]
        gx_ref, dg_ref = refs[n_p : n_p + 2]
        landed = refs[n_p + 2 : 2 * n_p + 2]
        acc_ref = refs[2 * n_p + 2]
        start, finish = _carry(groups, src, landed, refs[2 * n_p + 3 :])
        i = pl.program_id(0)
        j = pl.program_id(1)

        @pl.when(jnp.logical_and(i == 0, j == 0))
        def _():
            start()
            dg_ref[...] = jnp.zeros_like(dg_ref)

        @pl.when(j == 0)
        def _():
            acc_ref[...] = jnp.zeros_like(acc_ref)

        acc_ref[...] += _dot(dp_ref[...], w_ref[...])

        @pl.when(j == n_k - 1)
        def _():
            xv = x_ref[...]
            gain = g_ref[...]
            r = lax.rsqrt(_rowmean(xv * xv) + EPS)
            dgain, dx = _rms_bwd(acc_ref[...], gain, r, xv * r)
            dg_ref[...] += dgain
            gx_ref[...] = dxo_ref[...] + dx

        @pl.when(jnp.logical_and(i == n_t - 1, j == n_k - 1))
        def _():
            finish()

    res = pl.pallas_call(
        body,
        name="input_grad",
        grid=(n_t, n_k),
        in_specs=[
            pl.BlockSpec((t, kb), lambda i, j: (i, j)),
            pl.BlockSpec((kb, d), lambda i, j: (j, 0)),
            pl.BlockSpec((t, d), lambda i, j: (i, 0)),
            pl.BlockSpec((t, d), lambda i, j: (i, 0)),
            _full((1, d)),
        ]
        + carried["in_specs"],
        out_specs=[pl.BlockSpec((t, d), lambda i, j: (i, 0)), _full((1, d))] + carried["out_specs"],
        out_shape=[jax.ShapeDtypeStruct((s, d), F32), jax.ShapeDtypeStruct((1, d), F32)] + carried["out_shape"],
        scratch_shapes=[pltpu.VMEM((t, d), F32)] + carried["scratch_shapes"],
        compiler_params=_params("arbitrary", "arbitrary", vmem_limit_bytes=VMEM_LIMIT_MIX_BYTES),
    )(dproj, win_t, x, dxo, norm_pre, *carried["operands"])
    return res[0], res[1], res[2:]


def _kv_backward(dkv, memn, wkv, mem):
    m, d = mem.shape
    n = wkv.shape[1]

    def body(dkv_ref, memn_ref, w_ref, mem_ref, gw_ref, dg_ref):
        dkv_bf = dkv_ref[...].astype(BF16)
        gw_ref[...] = _dot(memn_ref[...], dkv_bf, TN).astype(BF16).reshape(N_DEV, WKV_BLK, n)
        dmemn = _dot(dkv_bf, w_ref[...], NT)
        mv = mem_ref[...]
        r = lax.rsqrt(_rowmean(mv * mv) + EPS)
        dg_ref[...] = _colsum(dmemn * (mv * r))

    return pl.pallas_call(
        body,
        name="kv_backward",
        grid=(1,),
        in_specs=[_full((m, n)), _full((m, d)), _full(wkv.shape), _full((m, d))],
        out_specs=[_full((N_DEV, WKV_BLK, n)), _full((1, d))],
        out_shape=[jax.ShapeDtypeStruct((N_DEV, WKV_BLK, n), BF16), jax.ShapeDtypeStruct((1, d), F32)],
        compiler_params=_params("arbitrary"),
    )(dkv, memn, wkv, mem)


def _adamw_math(w, g, m, v):
    m = ADAM_B1 * m + (1.0 - ADAM_B1) * g
    v = ADAM_B2 * v + (1.0 - ADAM_B2) * (g * g)
    m_hat = m / (1.0 - ADAM_B1**ADAM_STEP)
    v_hat = v / (1.0 - ADAM_B2**ADAM_STEP)
    delta = -ADAM_LR * (m_hat / (jnp.sqrt(v_hat) + ADAM_EPS) + ADAM_WD * w)
    return delta, m, v


def _adamw(parts, w, m, v, name):
    r, c = w.shape
    slots = parts.shape[0]
    t = r
    while t * c * 4 > TILE_ADAM_BYTES and t % 16 == 0:
        t //= 2

    def body(p_ref, w_ref, m_ref, v_ref, g_ref, d_ref, nm_ref, nv_ref):
        g = p_ref[0].astype(F32)
        for k in range(1, slots):
            g = g + p_ref[k].astype(F32)
        delta, nm, nv = _adamw_math(w_ref[...], g, m_ref[...], v_ref[...])
        g_ref[...] = g
        d_ref[...] = delta
        nm_ref[...] = nm
        nv_ref[...] = nv

    tile = pl.BlockSpec((t, c), lambda i: (i, 0))
    return pl.pallas_call(
        body,
        name=name,
        grid=(r // t,),
        in_specs=[pl.BlockSpec((slots, t, c), lambda i: (0, i, 0)), tile, tile, tile],
        out_specs=[tile] * 4,
        out_shape=[jax.ShapeDtypeStruct((r, c), F32)] * 4,
        compiler_params=_params("parallel"),
    )(parts, w, m, v)


def _adamw_packed(parts, triples, name):
    slots = parts.shape[0]
    sizes = [w.shape[0] for w, _, _ in triples]
    rest = parts.shape[1] - sum(sizes)

    def total(p_ref, at, rows):
        g = p_ref[0, at : at + rows, :]
        for k in range(1, slots):
            g = g + p_ref[k, at : at + rows, :]
        return g

    def body(p_ref, *refs):
        ins = refs[: 3 * len(triples)]
        outs = refs[3 * len(triples) :]
        at = 0
        for n, rows in enumerate(sizes):
            g = total(p_ref, at, rows)
            w_ref, m_ref, v_ref = ins[3 * n : 3 * n + 3]
            delta, nm, nv = _adamw_math(w_ref[...], g, m_ref[...], v_ref[...])
            for ref, val in zip(outs[4 * n : 4 * n + 4], (g, delta, nm, nv)):
                ref[...] = val
            at += rows
        if rest:
            outs[-1][...] = total(p_ref, at, rest)

    flat = [a for t in triples for a in t]
    res = pl.pallas_call(
        body,
        name=name,
        out_shape=[jax.ShapeDtypeStruct(w.shape, F32) for w, _, _ in triples for _ in range(4)]
        + ([jax.ShapeDtypeStruct((rest, 128), F32)] if rest else []),
        compiler_params=pltpu.CompilerParams(vmem_limit_bytes=VMEM_LIMIT_BYTES),
    )(parts, *flat)
    return [res[4 * n : 4 * n + 4] for n in range(len(triples))], (res[-1] if rest else None)


SMALL = ("norm_pre", "pool_scale", "sgu_ln_g", "sgu_ln_b", "sgu_w", "sgu_b", "mem_norm", "branch_norm", "norm_post")


def _local_view(name, w):
    if name == "sgu_w":
        return w.reshape(SGU_HEADS, SGU_CHUNK, SGU_CHUNK)
    if name == "sgu_b":
        return w.reshape(SGU_HEADS, SGU_CHUNK)
    return w.reshape(1, -1)


def _forward_backward(x, mem, target, shards, small):
    causal = jnp.tril(jnp.ones((SGU_CHUNK, SGU_CHUNK), dtype=bool))
    sgu_wm = jnp.where(causal[None], small["sgu_w"], 0.0).astype(BF16)
    sgu_bias = jnp.repeat(jnp.transpose(small["sgu_b"]), SGU_HEAD_DIM, axis=1)

    proj, h, (win, wkv, pool_all, wout) = _proj_gather(x, small["norm_pre"], shards)
    wout = wout.reshape(MIX_WIDTH, D_MODEL)
    wkv = wkv.reshape(D_MODEL, 2 * BRANCH)
    pool_full = (
        pool_all.reshape(N_DEV, len(POOL_WINDOWS), POOL_BLK, POOL_GROUP_DIM)
        .transpose(1, 0, 2, 3)
        .reshape(len(POOL_WINDOWS), POOL_GROUP_DIM, POOL_GROUP_DIM)
    )
    memn, kv = _kv_forward(mem, small["mem_norm"], wkv)
    (y, dout, dxo, dproj, loss, d_norm_post, d_branch_norm, d_pool_scale, d_ln_g, d_ln_b, d_pool_w, d_sgu_w, d_sgu_b,
     dkv) = _mix(
        proj, x, target, kv, kv.T, wout, wout.T, pool_full, jnp.swapaxes(pool_full, 1, 2), small["pool_scale"],
        small["sgu_ln_g"], small["sgu_ln_b"], sgu_bias, sgu_wm, jnp.swapaxes(sgu_wm, 1, 2), small["branch_norm"],
        small["norm_post"],
    )
    g_wkv, d_mem_norm = _kv_backward(dkv, memn, wkv, mem)
    g_pool = (
        d_pool_w.reshape(len(POOL_WINDOWS), N_DEV, POOL_BLK, POOL_GROUP_DIM)
        .transpose(1, 0, 2, 3)
        .reshape(N_DEV, len(POOL_WINDOWS) * POOL_BLK, POOL_GROUP_DIM)
        .astype(BF16)
    )
    small_grads = dict(
        pool_scale=d_pool_scale, sgu_ln_g=d_ln_g, sgu_ln_b=d_ln_b, sgu_w=d_sgu_w, sgu_b=d_sgu_b,
        mem_norm=d_mem_norm, branch_norm=d_branch_norm, norm_post=d_norm_post,
    )
    packed = jnp.concatenate(
        [small_grads[n].reshape(-1, LANES) for n in SMALL if n != "norm_pre"]
        + [jnp.broadcast_to(loss, (SUBLANES, LANES))],
        axis=0,
    )
    packed = packed[None, None]

    by_chip = lambda g: g.reshape((N_CHIPS, 2) + g.shape[1:])
    small_mine = [by_chip(g_wkv), by_chip(g_pool), packed]
    g_wout, wout_theirs, small_theirs = _weight_grad(y, dout, N_DEV, "rows", "grad_w_out", [("pair", small_mine)])
    sums, _ = _pair_sum(small_mine + [by_chip(g_wout)], list(small_theirs) + [wout_theirs], "pair_sum_first")
    g_win, win_theirs, (l_wkv, l_pool, l_packed, l_wout) = _weight_grad(
        h, dproj, N_DEV, "cols", "grad_w_in", [("chips", list(sums))]
    )
    (win_sums,), _ = _pair_sum([by_chip(g_win)], [win_theirs], "pair_sum_w_in")
    win_late, _ = lax.optimization_barrier((win, g_wkv))
    grad_x, d_norm_pre, (l_win,) = _input_grad(
        dproj, jnp.swapaxes(win_late, 1, 2).reshape(IN_WIDTH, D_MODEL), x, dxo, small["norm_pre"],
        [("chips", [win_sums])],
    )
    return grad_x, dict(w_in=l_win, w_out=l_wout, w_kv=l_wkv, pool_w=l_pool), l_packed, d_norm_pre


def kernel(x, mem, norm_pre, w_in, pool_w, pool_scale, sgu_ln_g, sgu_ln_b, sgu_w, sgu_b, mem_norm, w_kv, branch_norm, w_out, norm_post, loss_target, m_norm_pre, m_w_in, m_pool_w, m_pool_scale, m_sgu_ln_g, m_sgu_ln_b, m_sgu_w, m_sgu_b, m_mem_norm, m_w_kv, m_branch_norm, m_w_out, m_norm_post, v_norm_pre, v_w_in, v_pool_w, v_pool_scale, v_sgu_ln_g, v_sgu_ln_b, v_sgu_w, v_sgu_b, v_mem_norm, v_w_kv, v_branch_norm, v_w_out, v_norm_post):
    weights = dict(norm_pre=norm_pre, w_in=w_in, pool_w=pool_w, pool_scale=pool_scale, sgu_ln_g=sgu_ln_g, sgu_ln_b=sgu_ln_b, sgu_w=sgu_w, sgu_b=sgu_b, mem_norm=mem_norm, w_kv=w_kv, branch_norm=branch_norm, w_out=w_out, norm_post=norm_post)
    first = dict(norm_pre=m_norm_pre, w_in=m_w_in, pool_w=m_pool_w, pool_scale=m_pool_scale, sgu_ln_g=m_sgu_ln_g, sgu_ln_b=m_sgu_ln_b, sgu_w=m_sgu_w, sgu_b=m_sgu_b, mem_norm=m_mem_norm, w_kv=m_w_kv, branch_norm=m_branch_norm, w_out=m_w_out, norm_post=m_norm_post)
    second = dict(norm_pre=v_norm_pre, w_in=v_w_in, pool_w=v_pool_w, pool_scale=v_pool_scale, sgu_ln_g=v_sgu_ln_g, sgu_ln_b=v_sgu_ln_b, sgu_w=v_sgu_w, sgu_b=v_sgu_b, mem_norm=v_mem_norm, w_kv=v_w_kv, branch_norm=v_branch_norm, w_out=v_w_out, norm_post=v_norm_post)
    order = ("norm_pre", "w_in", "pool_w", "pool_scale", "sgu_ln_g", "sgu_ln_b", "sgu_w", "sgu_b", "mem_norm", "w_kv", "branch_norm", "w_out", "norm_post")

    owned_shape = dict(
        w_in=(D_MODEL, WIN_BLK), w_out=(WOUT_BLK, D_MODEL), w_kv=(WKV_BLK, 2 * BRANCH),
        pool_w=(len(POOL_WINDOWS) * POOL_BLK, POOL_GROUP_DIM),
    )
    owned = {n: weights[n].reshape(owned_shape[n]) for n in owned_shape}
    small = {n: _local_view(n, weights[n]) for n in SMALL}
    grad_x, landed, landed_packed, d_norm_pre = _forward_backward(
        x[0], mem[0], loss_target[0], [owned[n].astype(BF16) for n in ("w_in", "w_kv", "pool_w", "w_out")], small
    )
    d_norm_pre = d_norm_pre.reshape(-1, 128)
    (landed_norm_pre,) = _exchange(
        [("all", [jnp.broadcast_to(d_norm_pre[None], (N_DEV,) + d_norm_pre.shape)])], "exchange_norm_pre"
    )

    grads, deltas, new_m, new_v = {}, {}, {}, {}
    for n in owned_shape:
        shape = weights[n].shape
        res = _adamw(
            landed[n], owned[n], first[n].reshape(owned_shape[n]), second[n].reshape(owned_shape[n]), "adamw_" + n
        )
        grads[n], deltas[n], new_m[n], new_v[n] = (a.reshape(shape) for a in res)
    rows_of = lambda tree, n: tree[n].reshape(-1, 128)
    for names, parts, name in (
        ([n for n in SMALL if n != "norm_pre"], landed_packed, "adamw_replicated"),
        (["norm_pre"], landed_norm_pre, "adamw_norm_pre"),
    ):
        res, rest = _adamw_packed(
            parts, [(rows_of(weights, n), rows_of(first, n), rows_of(second, n)) for n in names], name
        )
        if rest is not None:
            total = rest[0, 0]
        for n, four in zip(names, res):
            for tree, a in zip((grads, deltas, new_m, new_v), four):
                tree[n] = a.reshape(weights[n].shape)

    return (
        total,
        grad_x[None],
        *[grads[n] for n in order],
        *[deltas[n] for n in order],
        *[new_m[n] for n in order],
        *[new_v[n] for n in order],
    )
```

```python
import jax
import jax.numpy as jnp
from jax import lax
from jax.experimental import pallas as pl
from jax.experimental.pallas import tpu as pltpu

F32 = jnp.float32
BF16 = jnp.bfloat16
EPS = 1e-6

D_MODEL = 2048
POOL_WINDOWS = (2, 4, 8, 16)
POOL_GROUP_DIM = 256
BRANCH = 1024
SGU_CHUNK = 128
SGU_HEADS = 8
SGU_HEAD_DIM = 128
XATTN_HEADS = 4
XATTN_HEAD_DIM = 256
MIX_WIDTH = 3 * BRANCH
IN_WIDTH = 7 * BRANCH
N_DEV = 8
WIN_BLK = IN_WIDTH // N_DEV
WOUT_BLK = MIX_WIDTH // N_DEV
WKV_BLK = D_MODEL // N_DEV
POOL_BLK = POOL_GROUP_DIM // N_DEV
HALO = 16
LANES = 128
SUBLANES = 8

ADAM_LR = 0.001
ADAM_B1 = 0.9
ADAM_B2 = 0.999
ADAM_EPS = 1e-08
ADAM_WD = 0.01
ADAM_STEP = 10

VMEM_LIMIT_BYTES = 56 * 1024 * 1024
VMEM_LIMIT_MIX_BYTES = 63 * 1024 * 1024

TILE_PROJ = 1024
TILE_MIX = 128
TILE_GRAD = 512
TILE_WEIGHT_GRAD = 1024
TILE_ADAM_BYTES = 1 << 20

ANY = pl.BlockSpec(memory_space=pl.ANY)
NN = (((1,), (0,)), ((), ()))
NT = (((1,), (1,)), ((), ()))
TN = (((0,), (0,)), ((), ()))
MESH = pl.DeviceIdType.MESH


def _dot(a, b, dims=NN):
    return lax.dot_general(a, b, dims, preferred_element_type=F32)


def _params(*semantics, vmem_limit_bytes=VMEM_LIMIT_BYTES):
    return pltpu.CompilerParams(dimension_semantics=semantics, vmem_limit_bytes=vmem_limit_bytes)


def _rowmean(a):
    return jnp.mean(a, axis=-1, keepdims=True)


def _colsum(a):
    return jnp.sum(a, axis=0, keepdims=True)


def _full(shape):
    zeros = (0,) * len(shape)
    return pl.BlockSpec(shape, lambda *_: zeros)


def _resident(shape):
    zeros = (0,) * len(shape)
    return pl.BlockSpec(shape, lambda *_: zeros, pipeline_mode=pl.Buffered(1))


def _kv_forward(mem, mem_norm, wkv):
    m, d = mem.shape
    n = wkv.shape[1]
    cols = 4 * LANES

    def body(mem_ref, g_ref, w_ref, memn_ref, kv_ref):
        mv = mem_ref[...]
        r = lax.rsqrt(_rowmean(mv * mv) + EPS)
        memn = (mv * r * g_ref[...]).astype(BF16)
        memn_ref[...] = memn
        kv_ref[...] = _dot(memn, w_ref[...]).astype(BF16)

    return pl.pallas_call(
        body,
        name="kv_forward",
        grid=(n // cols,),
        in_specs=[_full((m, d)), _full((1, d)), pl.BlockSpec((d, cols), lambda j: (0, j))],
        out_specs=[_full((m, d)), pl.BlockSpec((m, cols), lambda j: (0, j))],
        out_shape=[jax.ShapeDtypeStruct((m, d), BF16), jax.ShapeDtypeStruct((m, n), BF16)],
        compiler_params=_params("arbitrary"),
    )(mem, mem_norm, wkv)


def _proj_gather(x_in, norm_pre, shards):
    s, d = x_in.shape
    t = min(TILE_PROJ, s)
    n_t = s // t
    n_arr = len(shards)

    def places(x, y, c):
        return (x, y, c), (x, y, 1 - c), (x ^ c, y ^ (1 - c)), (x ^ (1 - c), y ^ c), (1 - x, 1 - y)

    def index(chip, core):
        return 4 * chip[0] + 2 * chip[1] + core

    _, _, chip_a, chip_b, chip_d = places(*_position())
    c_out = lax.axis_index("c")
    me_out = index((lax.axis_index("x"), lax.axis_index("y")), c_out)
    order = jnp.stack(
        [
            me_out, me_out ^ 1, index(chip_a, c_out), index(chip_b, c_out), index(chip_b, 1 - c_out),
            index(chip_a, 1 - c_out), index(chip_d, c_out), index(chip_d, 1 - c_out),
        ]
    ).astype(jnp.int32)

    def body(order_ref, x_ref, g_ref, *refs):
        del order_ref
        src = refs[:n_arr]
        proj_ref, h_ref = refs[n_arr : n_arr + 2]
        out = refs[n_arr + 2 : 2 * n_arr + 2]
        wbuf, hs, send_sems, recv_sems, local_sems, load_sems = refs[2 * n_arr + 2 :]
        j = pl.program_id(0)
        i = pl.program_id(1)
        me, sibling, chip_a, chip_b, chip_d = places(*_position())
        c = me[2]

        def block(a, chip, core):
            return out[a].at[index(chip, core)]

        def copy(a, k, owner, to, from_input=False):
            return pltpu.make_async_remote_copy(
                src_ref=src[a] if from_input else block(a, *owner),
                dst_ref=block(a, *owner),
                send_sem=send_sems.at[a, k],
                recv_sem=recv_sems.at[a, k],
                device_id=to,
                device_id_type=MESH,
            )

        mine = (me[:2], c)

        def own(a):
            return pltpu.make_async_copy(src[a], block(a, *mine), local_sems.at[a])

        def first_sends(a):
            return [
                copy(a, 0, mine, sibling, from_input=True),
                copy(a, 1, mine, (*chip_a, c), from_input=True),
                copy(a, 2, mine, (*chip_b, c), from_input=True),
            ]

        def onward(a, k):
            owner = {3: chip_a, 4: chip_a, 5: chip_b, 6: chip_d}[k]
            return copy(a, k, (owner, c), (*chip_b, c) if k == 3 else sibling)

        def landed(a, k):
            owner = {0: mine[0], 1: chip_a, 2: chip_b, 3: chip_d, 4: chip_b, 5: chip_a, 6: chip_d}[k]
            core = c if k in (1, 2, 3) else 1 - c
            copy(a, k, (owner, core), me).wait_recv()
            return owner, core

        def load(ref, step):
            return pltpu.make_async_copy(ref, wbuf.at[step % 2], load_sems.at[step % 2])

        @pl.when(jnp.logical_and(j == 0, i == 0))
        def _():
            own(0).start()
            for cp in first_sends(0):
                cp.start()
            load(src[0], 0).start()
            load(src[0], 0).wait()

        steps = {1: (0, ()), 2: (1, (3, 4)), 3: (2, (5,)), 4: (4, ()), 5: (5, ()), 6: (3, (6,)), 7: (6, ())}
        for step, (k, then) in steps.items():

            @pl.when(jnp.logical_and(j == step, i == 0))
            def _():
                load(src[0], step).wait()

            @pl.when(jnp.logical_and(j == step - 1, i == n_t - 1))
            def _():
                owner = landed(0, k)
                for k2 in then:
                    onward(0, k2).start()
                if k == 1:
                    for a in range(1, n_arr):
                        own(a).start()
                        for cp in first_sends(a):
                            cp.start()
                if k == 3:
                    for a in range(1, n_arr):
                        for k1, then1 in ((1, (3, 4)), (2, (5,))):
                            landed(a, k1)
                            for k2 in then1:
                                onward(a, k2).start()
                load(block(0, *owner), step).start()

        @pl.when(j == 0)
        def _():
            xv = x_ref[...]
            h = (xv * lax.rsqrt(_rowmean(xv * xv) + EPS) * g_ref[...]).astype(BF16)
            hs[i] = h
            h_ref[...] = h

        proj_ref[...] = _dot(hs[i], wbuf[j % 2]).astype(BF16)

        @pl.when(jnp.logical_and(j == N_DEV - 1, i == n_t - 1))
        def _():
            for a in range(1, n_arr):
                landed(a, 3)
                onward(a, 6).start()
            for a in range(1, n_arr):
                for k in (0, 4, 5, 6):
                    landed(a, k)
            for a in range(n_arr):
                for cp in first_sends(a) + [onward(a, k) for k in (3, 4, 5, 6)]:
                    cp.wait_send()
                own(a).wait()

    res = pl.pallas_call(
        body,
        name="proj_gather",
        grid_spec=pltpu.PrefetchScalarGridSpec(
            num_scalar_prefetch=1,
            grid=(N_DEV, n_t),
            in_specs=[
                pl.BlockSpec((t, d), lambda j, i, order_ref: (jnp.where(j == 0, i, n_t - 1), 0)),
                pl.BlockSpec((1, d), lambda j, i, order_ref: (0, 0)),
            ]
            + [ANY] * n_arr,
            out_specs=[
                pl.BlockSpec((t, WIN_BLK), lambda j, i, order_ref: (i, order_ref[j])),
                pl.BlockSpec((t, d), lambda j, i, order_ref: (jnp.where(j == 0, i, n_t - 1), 0)),
            ]
            + [ANY] * n_arr,
            scratch_shapes=[
                pltpu.VMEM((2,) + shards[0].shape, BF16),
                pltpu.VMEM((n_t, t, d), BF16),
                pltpu.SemaphoreType.DMA((n_arr, 7)),
                pltpu.SemaphoreType.DMA((n_arr, 7)),
                pltpu.SemaphoreType.DMA((n_arr,)),
                pltpu.SemaphoreType.DMA((2,)),
            ],
        ),
        out_shape=[jax.ShapeDtypeStruct((s, IN_WIDTH), BF16), jax.ShapeDtypeStruct((s, d), BF16)]
        + [jax.ShapeDtypeStruct((N_DEV,) + a.shape, a.dtype) for a in shards],
        compiler_params=_params("arbitrary", "arbitrary"),
    )(order, x_in, norm_pre, *shards)
    return res[0], res[1], res[2:]


def _sigmoid(a):
    return jax.nn.sigmoid(a)


def _dsilu(a, sg):
    return sg * (1.0 + a * (1.0 - sg))


def _rms_fwd(u, gain):
    r = lax.rsqrt(_rowmean(u * u) + EPS)
    n = u * r
    return r, n, n * gain


def _rms_bwd(dy, gain, r, n):
    dn = dy * gain
    return _colsum(dy * n), r * (dn - n * _rowmean(dn * n))


def _mix(proj, x, target, kv, kv_t, wout, wout_t, pool_w, pool_w_t, pool_scale, ln_g, ln_b, sgu_bias, sgu_wm, sgu_wm_t, branch_norm, norm_post):
    s, d = x.shape
    t = min(TILE_MIX, s)
    n_tiles = s // t
    n_chunks = t // SGU_CHUNK
    halo_blocks_per_tile = t // HALO
    inv_d = 1.0 / d
    scale = 1.0 / (XATTN_HEAD_DIM**0.5)

    def body(
        proj_ref, halo_ref, x_ref, tgt_ref, kv_ref, kvt_ref, wout_ref, wout_t_ref, pw_ref, pwt_ref, pscale_ref, lng_ref,
        lnb_ref, bias_ref, wm_ref, wmt_ref, bnorm_ref, gpost_ref,
        y_ref, dout_ref, dxo_ref, dproj_ref, loss_ref, dgpost_ref, dbnorm_ref, dpscale_ref, dlng_ref, dlnb_ref,
        dpw_out, dwm_out, dbias_ref, dkv_out,
        carry_ref, dzsum_ref, dpw_ref, dwm_ref, dkv_ref,
    ):
        i = pl.program_id(0)
        tile = n_tiles - 1 - i

        @pl.when(i == 0)
        def _():
            carry_ref[...] = jnp.zeros_like(carry_ref)
            dzsum_ref[...] = jnp.zeros_like(dzsum_ref)
            for ref in (loss_ref, dgpost_ref, dbnorm_ref, dpscale_ref, dlng_ref, dlnb_ref, dpw_ref, dwm_ref, dkv_ref):
                ref[...] = jnp.zeros_like(ref)

        t_glob = tile * t + lax.broadcasted_iota(jnp.int32, (t, 1), 0)
        inv_cnt = [1.0 / jnp.minimum(t_glob + 1, w).astype(F32) for w in POOL_WINDOWS]

        xa = proj_ref[:, 0:BRANCH].astype(F32)
        ga = proj_ref[:, BRANCH : 2 * BRANCH].astype(F32)
        halo = jnp.where(tile == 0, 0.0, halo_ref[...].astype(F32))
        d_bf, pm_parts = [], []
        for g, w in enumerate(POOL_WINDOWS):
            cols = slice(g * POOL_GROUP_DIM, (g + 1) * POOL_GROUP_DIM)
            acc = jnp.concatenate([halo[:, cols], xa[:, cols]], axis=0)
            k = 1
            while k < w:
                acc = acc + pltpu.roll(acc, k, axis=0)
                k *= 2
            dg = (acc[HALO:, :] * inv_cnt[g] - xa[:, cols]).astype(BF16)
            d_bf.append(dg)
            pm_parts.append(_dot(dg, pw_ref[g]))
        pm = jnp.concatenate(pm_parts, axis=1)
        pscale = pscale_ref[...]
        pa = pm * pscale
        sga = _sigmoid(ga)
        sila = ga * sga
        ua = pa * sila
        g_a = bnorm_ref[:, 0:BRANCH]
        ra, na, ya = _rms_fwd(ua, g_a)

        u = proj_ref[:, 2 * BRANCH : 3 * BRANCH].astype(F32)
        v = proj_ref[:, 3 * BRANCH : 4 * BRANCH].astype(F32)
        gb = proj_ref[:, 4 * BRANCH : 5 * BRANCH].astype(F32)
        lng = lng_ref[...]
        vc = v - _rowmean(v)
        rstd = lax.rsqrt(_rowmean(vc * vc) + EPS)
        vhat = vc * rstd
        vn_bf = (vhat * lng + lnb_ref[...]).astype(BF16)
        z_rows = []
        for c in range(n_chunks):
            rows = slice(c * SGU_CHUNK, (c + 1) * SGU_CHUNK)
            z_rows.append(
                jnp.concatenate(
                    [
                        _dot(wm_ref[hd], vn_bf[rows, hd * SGU_HEAD_DIM : (hd + 1) * SGU_HEAD_DIM])
                        for hd in range(SGU_HEADS)
                    ],
                    axis=1,
                )
                + bias_ref[...]
            )
        z = z_rows[0] if n_chunks == 1 else jnp.concatenate(z_rows, axis=0)
        sb = u * z
        sgb = _sigmoid(gb)
        silb = gb * sgb
        ub = sb * silb
        g_b = bnorm_ref[:, BRANCH : 2 * BRANCH]
        rb, nb, yb = _rms_fwd(ub, g_b)

        q = proj_ref[:, 5 * BRANCH : 6 * BRANCH]
        gc = proj_ref[:, 6 * BRANCH : 7 * BRANCH].astype(F32)
        q_bf, p_bf, o_parts = [], [], []
        for hd in range(XATTN_HEADS):
            cols = slice(hd * XATTN_HEAD_DIM, (hd + 1) * XATTN_HEAD_DIM)
            qh = q[:, cols]
            sc = _dot(qh, kvt_ref[cols, :]) * scale
            e = jnp.exp(sc - jnp.max(sc, axis=-1, keepdims=True))
            p = e / jnp.sum(e, axis=-1, keepdims=True)
            q_bf.append(qh)
            p_bf.append(p.astype(BF16))
            o_parts.append(_dot(p_bf[hd], kv_ref[:, BRANCH + hd * XATTN_HEAD_DIM : BRANCH + (hd + 1) * XATTN_HEAD_DIM]))
        o = jnp.concatenate(o_parts, axis=1)
        sgc = _sigmoid(gc)
        silc = gc * sgc
        uc = o * silc
        g_c = bnorm_ref[:, 2 * BRANCH : 3 * BRANCH]
        rc, nc, yc = _rms_fwd(uc, g_c)

        out = None
        for b, y_branch in enumerate((ya, yb, yc)):
            rows = slice(b * BRANCH, (b + 1) * BRANCH)
            y_bf = y_branch.astype(BF16)
            y_ref[:, rows] = y_bf
            part = _dot(y_bf, wout_ref[rows, :])
            out = part if out is None else out + part
        gpost = gpost_ref[...]
        r_out = lax.rsqrt(_rowmean(out * out) + EPS)
        on = out * r_out
        err = x_ref[...] + on * gpost - tgt_ref[...]
        loss_ref[...] += 0.5 * jnp.sum(_rowmean(err * err), axis=0, keepdims=True)

        dxo = err * inv_d
        dxo_ref[...] = dxo
        dgp, dout = _rms_bwd(dxo, gpost, r_out, on)
        dgpost_ref[...] += dgp
        dout_bf = dout.astype(BF16)
        dout_ref[...] = dout_bf
        dy = [_dot(dout_bf, wout_t_ref[:, b * BRANCH : (b + 1) * BRANCH]) for b in range(3)]

        dg_a, dua = _rms_bwd(dy[0], g_a, ra, na)
        dg_b, dub = _rms_bwd(dy[1], g_b, rb, nb)
        dg_c, duc = _rms_bwd(dy[2], g_c, rc, nc)
        dbnorm_ref[...] += jnp.concatenate([dg_a, dg_b, dg_c], axis=1)

        dpa = dua * sila
        dga = dua * pa * _dsilu(ga, sga)
        dpscale_ref[...] += _colsum(dpa * pm)
        dpm = dpa * pscale
        dxa_parts, carry_parts = [], []
        for g, w in enumerate(POOL_WINDOWS):
            cols = slice(g * POOL_GROUP_DIM, (g + 1) * POOL_GROUP_DIM)
            dpm_g = dpm[:, cols].astype(BF16)
            dd = _dot(dpm_g, pwt_ref[g])
            dpw_ref[g] += _dot(d_bf[g], dpm_g, TN)
            cg = dd * inv_cnt[g]
            carry_parts.append(cg[0:HALO, :])
            acc = jnp.concatenate([cg, carry_ref[:, cols]], axis=0)
            k = 1
            while k < w:
                acc = acc + pltpu.roll(acc, t + HALO - k, axis=0)
                k *= 2
            dxa_parts.append(acc[0:t, :] - dd)
        carry_ref[...] = jnp.concatenate(carry_parts, axis=1)
        dxa = jnp.concatenate(dxa_parts, axis=1)

        dsb = dub * silb
        dgb = dub * sb * _dsilu(gb, sgb)
        du = dsb * z
        dz = dsb * u
        dz_bf = dz.astype(BF16)
        dvn_rows = []
        dz_sum = None
        for c in range(n_chunks):
            rows = slice(c * SGU_CHUNK, (c + 1) * SGU_CHUNK)
            dz_sum = dz[rows, :] if dz_sum is None else dz_sum + dz[rows, :]
            parts = []
            for hd in range(SGU_HEADS):
                cols = slice(hd * SGU_HEAD_DIM, (hd + 1) * SGU_HEAD_DIM)
                parts.append(_dot(wmt_ref[hd], dz_bf[rows, cols]))
                dwm_ref[hd] += _dot(dz_bf[rows, cols], vn_bf[rows, cols], NT)
            dvn_rows.append(jnp.concatenate(parts, axis=1))
        dzsum_ref[...] += dz_sum
        dvn = dvn_rows[0] if n_chunks == 1 else jnp.concatenate(dvn_rows, axis=0)
        dlng_ref[...] += _colsum(dvn * vhat)
        dlnb_ref[...] += _colsum(dvn)
        dvh = dvn * lng
        dv = rstd * (dvh - _rowmean(dvh) - vhat * _rowmean(dvh * vhat))

        do = duc * silc
        dgc = duc * o * _dsilu(gc, sgc)
        dq_parts = []
        for hd in range(XATTN_HEADS):
            cols = slice(hd * XATTN_HEAD_DIM, (hd + 1) * XATTN_HEAD_DIM)
            vcols = slice(BRANCH + hd * XATTN_HEAD_DIM, BRANCH + (hd + 1) * XATTN_HEAD_DIM)
            do_h = do[:, cols].astype(BF16)
            p = p_bf[hd].astype(F32)
            dp = _dot(do_h, kvt_ref[vcols, :])
            dkv_ref[:, vcols] += _dot(p_bf[hd], do_h, TN)
            ds_bf = (p * (dp - jnp.sum(dp * p, axis=-1, keepdims=True)) * scale).astype(BF16)
            dq_parts.append(_dot(ds_bf, kv_ref[:, cols]))
            dkv_ref[:, cols] += _dot(ds_bf, q_bf[hd], TN)
        dq = jnp.concatenate(dq_parts, axis=1)

        dproj_ref[...] = jnp.concatenate([dxa, dga, du, dv, dgb, dq, dgc], axis=1).astype(BF16)

        @pl.when(i == n_tiles - 1)
        def _():
            keep = lax.broadcasted_iota(jnp.int32, (SGU_CHUNK, SGU_CHUNK), 0) >= lax.broadcasted_iota(
                jnp.int32, (SGU_CHUNK, SGU_CHUNK), 1
            )
            for hd in range(SGU_HEADS):
                dwm_ref[hd] = jnp.where(keep, dwm_ref[hd], 0.0)
                per_pos = dzsum_ref[:, hd * SGU_HEAD_DIM : (hd + 1) * SGU_HEAD_DIM]
                dbias_ref[hd : hd + 1, :] = _colsum(per_pos.T)
            for acc, res in ((dpw_ref, dpw_out), (dwm_ref, dwm_out), (dkv_ref, dkv_out)):
                pltpu.sync_copy(acc, res)

    row_tile = lambda width: pl.BlockSpec((t, width), lambda i: (n_tiles - 1 - i, 0))
    halo_spec = pl.BlockSpec(
        (HALO, BRANCH), lambda i: (jnp.maximum((n_tiles - 1 - i) * halo_blocks_per_tile - 1, 0), 0)
    )
    acc_shapes = [
        (1, 128),
        (1, d),
        (1, MIX_WIDTH),
        (1, BRANCH),
        (1, BRANCH),
        (1, BRANCH),
        pool_w.shape,
        sgu_wm.shape,
        (SGU_HEADS, SGU_CHUNK),
        kv.shape,
    ]
    return pl.pallas_call(
        body,
        name="mix",
        grid=(n_tiles,),
        in_specs=[
            row_tile(IN_WIDTH), halo_spec, row_tile(d), row_tile(d), _resident(kv.shape), _resident(kv_t.shape),
            _resident(wout.shape), _resident(wout_t.shape), _resident(pool_w.shape), _resident(pool_w_t.shape),
            _full((1, BRANCH)), _full((1, BRANCH)), _full((1, BRANCH)), _resident((SGU_CHUNK, BRANCH)),
            _resident(sgu_wm.shape), _resident(sgu_wm_t.shape), _full((1, MIX_WIDTH)), _full((1, d)),
        ],
        out_specs=[row_tile(MIX_WIDTH), row_tile(d), row_tile(d), row_tile(IN_WIDTH)]
        + [ANY if len(a) == 3 or a == kv.shape else _full(a) for a in acc_shapes],
        out_shape=[
            jax.ShapeDtypeStruct((s, MIX_WIDTH), BF16),
            jax.ShapeDtypeStruct((s, d), BF16),
            jax.ShapeDtypeStruct((s, d), F32),
            jax.ShapeDtypeStruct((s, IN_WIDTH), BF16),
        ]
        + [jax.ShapeDtypeStruct(a, F32) for a in acc_shapes],
        scratch_shapes=[
            pltpu.VMEM((HALO, BRANCH), F32), pltpu.VMEM((SGU_CHUNK, BRANCH), F32), pltpu.VMEM(pool_w.shape, F32),
            pltpu.VMEM(sgu_wm.shape, F32), pltpu.VMEM(kv.shape, F32),
        ],
        compiler_params=_params("arbitrary", vmem_limit_bytes=VMEM_LIMIT_MIX_BYTES),
    )(
        proj, proj, x, target, kv, kv_t, wout, wout_t, pool_w, pool_w_t, pool_scale, ln_g, ln_b, sgu_bias, sgu_wm,
        sgu_wm_t, branch_norm, norm_post,
    )


def _position():
    return lax.axis_index("x"), lax.axis_index("y"), lax.axis_index("c")


N_CHIPS = 4


def _landing_shape(kind, a):
    return (N_CHIPS,) + a.shape[2:] if kind == "pair" else a.shape


def _carry_specs(groups):
    arrays = [(kind, a) for kind, arrs in groups for a in arrs]
    scratch = []
    for _, arrs in groups:
        n = len(arrs)
        scratch += [pltpu.SemaphoreType.DMA((n, N_DEV)), pltpu.SemaphoreType.DMA((n, N_DEV)), pltpu.SemaphoreType.DMA((n,))]
    return dict(
        n=len(arrays),
        operands=[a for _, a in arrays],
        in_specs=[ANY] * len(arrays),
        out_specs=[ANY] * len(arrays),
        out_shape=[jax.ShapeDtypeStruct(_landing_shape(kind, a), a.dtype) for kind, a in arrays],
        scratch_shapes=scratch,
    )


def _carry(groups, src, out, sems):
    x, y, c = _position()
    chip = 2 * x + y
    me = 2 * chip + c

    def remote(s, d, send_sems, recv_sems, a, m, to):
        return pltpu.make_async_remote_copy(
            src_ref=s, dst_ref=d, send_sem=send_sems.at[a, m], recv_sem=recv_sems.at[a, m], device_id=to,
            device_id_type=MESH,
        )

    def copies():
        far, near = [], []
        at = 0
        for g, (kind, arrs) in enumerate(groups):
            send_sems, recv_sems, local_sems = sems[3 * g : 3 * g + 3]
            for a in range(len(arrs)):
                s, d = src[at + a], out[at + a]
                if kind == "pair" and s.shape[0] == 1:
                    for b in range(N_CHIPS):
                        far.append(remote(s.at[0, 0], d.at[b], send_sems, recv_sems, a, 1 + b, (x, y, 1 - c)))
                elif kind == "pair":
                    far.append(remote(s.at[:, 1 - c], d, send_sems, recv_sems, a, 1, (x, y, 1 - c)))
                elif kind == "chips":
                    for m in range(1, N_CHIPS):
                        px, py = x ^ (m >> 1), y ^ (m & 1)
                        far.append(remote(s.at[2 * px + py], d.at[chip], send_sems, recv_sems, a, m, (px, py, c)))
                    near.append(pltpu.make_async_copy(s.at[chip], d.at[chip], local_sems.at[a]))
                else:
                    for m in range(1, N_DEV):
                        px, py, pc = x ^ ((m >> 2) & 1), y ^ ((m >> 1) & 1), c ^ (m & 1)
                        far.append(
                            remote(s.at[4 * px + 2 * py + pc], d.at[me], send_sems, recv_sems, a, m, (px, py, pc))
                        )
                    near.append(pltpu.make_async_copy(s.at[me], d.at[me], local_sems.at[a]))
            at += len(arrs)
        return far, near

    def start():
        far, near = copies()
        for cp in near + far:
            cp.start()

    def finish():
        far, near = copies()
        for cp in far:
            cp.wait_recv()
        for cp in far:
            cp.wait_send()
        for cp in near:
            cp.wait()

    return start, finish


def _exchange(groups, name):
    carried = _carry_specs(groups)
    n_c = carried["n"]

    def body(*refs):
        start, finish = _carry(groups, refs[:n_c], refs[n_c : 2 * n_c], refs[2 * n_c :])
        start()
        finish()

    return pl.pallas_call(
        body,
        name=name,
        in_specs=carried["in_specs"],
        out_specs=carried["out_specs"],
        out_shape=carried["out_shape"],
        scratch_shapes=carried["scratch_shapes"],
    )(*carried["operands"])


def _pair_sum(mine, theirs, name, groups=()):
    n = len(mine)
    carried = _carry_specs(groups)
    n_c = carried["n"]
    core = lax.axis_index("c").astype(jnp.int32).reshape(1)

    def body(core_ref, *refs):
        del core_ref
        own = refs[:n]
        sib = refs[n : 2 * n]
        src = refs[2 * n : 2 * n + n_c]
        out = refs[2 * n + n_c : 3 * n + n_c]
        landed = refs[3 * n + n_c : 3 * n + 2 * n_c]
        start, finish = _carry(groups, src, landed, refs[3 * n + 2 * n_c :])
        b = pl.program_id(0)

        @pl.when(b == 0)
        def _():
            start()

        for a in range(n):
            out[a][...] = (own[a][...].astype(F32) + sib[a][...].astype(F32)).astype(out[a].dtype)

        @pl.when(b == N_CHIPS - 1)
        def _():
            finish()

    block = lambda a: pl.BlockSpec((None,) + a.shape[1:], lambda b, core_ref: (b, 0, 0))
    res = pl.pallas_call(
        body,
        name=name,
        grid_spec=pltpu.PrefetchScalarGridSpec(
            num_scalar_prefetch=1,
            grid=(N_CHIPS,),
            in_specs=[
                pl.BlockSpec((None, None) + a.shape[2:], lambda b, core_ref: (0, 0, 0, 0))
                if a.shape[0] == 1
                else pl.BlockSpec((None, None) + a.shape[2:], lambda b, core_ref: (b, core_ref[0], 0, 0))
                for a in mine
            ]
            + [block(a) for a in theirs]
            + carried["in_specs"],
            out_specs=[block(a) for a in theirs] + carried["out_specs"],
            scratch_shapes=carried["scratch_shapes"],
        ),
        out_shape=[jax.ShapeDtypeStruct(a.shape, a.dtype) for a in theirs] + carried["out_shape"],
        compiler_params=_params("arbitrary"),
    )(core, *mine, *theirs, *carried["operands"])
    return res[:n], res[n:]


def _weight_grad(a, b, n_blk, blocked, name, groups):
    s = a.shape[0]
    t = min(TILE_WEIGHT_GRAD, s)
    n_t = s // t
    n_pairs = n_blk // 2
    if blocked == "cols":
        k, c = a.shape[1], b.shape[1] // n_blk
        a_spec = pl.BlockSpec((t, k), lambda j, i: (i, 0))
        b_spec = pl.BlockSpec((t, 2 * c), lambda j, i: (i, j))
        acc_shape = (k, 2 * c)
    else:
        k, c = a.shape[1] // n_blk, b.shape[1]
        a_spec = pl.BlockSpec((t, 2 * k), lambda j, i: (i, j))
        b_spec = pl.BlockSpec((t, c), lambda j, i: (i, 0))
        acc_shape = (2 * k, c)
    carried = _carry_specs(groups)
    n_p = carried["n"]

    def body(a_ref, b_ref, *refs):
        src = refs[:n_p]
        o_ref, theirs_ref = refs[n_p : n_p + 2]
        landed = refs[n_p + 2 : 2 * n_p + 2]
        acc_ref, sbuf, pair_send, pair_recv = refs[2 * n_p + 2 : 2 * n_p + 6]
        start, finish = _carry(groups, src, landed, refs[2 * n_p + 6 :])
        j = pl.program_id(0)
        i = pl.program_id(1)
        x, y, c_me = _position()

        def to_sibling(pair):
            return pltpu.make_async_remote_copy(
                src_ref=sbuf.at[1 - c_me], dst_ref=theirs_ref.at[pair], send_sem=pair_send.at[pair],
                recv_sem=pair_recv.at[pair], device_id=(x, y, 1 - c_me), device_id_type=MESH,
            )

        @pl.when(jnp.logical_and(j == 0, i == 0))
        def _():
            start()

        @pl.when(i == 0)
        def _():
            acc_ref[...] = jnp.zeros_like(acc_ref)

        acc_ref[...] += _dot(a_ref[...], b_ref[...], TN)

        @pl.when(i == n_t - 1)
        def _():
            for pair in range(1, n_pairs):

                @pl.when(j == pair)
                def _():
                    to_sibling(pair - 1).wait_send()

            for half in range(2):
                if blocked == "cols":
                    block = acc_ref[:, half * c : (half + 1) * c].astype(BF16)
                else:
                    block = acc_ref[half * k : (half + 1) * k, :].astype(BF16)
                o_ref[half] = block
                sbuf[half] = block
            for pair in range(n_pairs):

                @pl.when(j == pair)
                def _():
                    to_sibling(pair).start()

        @pl.when(jnp.logical_and(j == n_pairs - 1, i == n_t - 1))
        def _():
            to_sibling(n_pairs - 1).wait_send()
            for pair in range(n_pairs):
                to_sibling(pair).wait_recv()
            finish()

    res = pl.pallas_call(
        body,
        name=name,
        grid=(n_pairs, n_t),
        in_specs=[a_spec, b_spec] + carried["in_specs"],
        out_specs=[pl.BlockSpec((2, k, c), lambda j, i: (j, 0, 0)), ANY] + carried["out_specs"],
        out_shape=[jax.ShapeDtypeStruct((n_blk, k, c), BF16), jax.ShapeDtypeStruct((n_pairs, k, c), BF16)]
        + carried["out_shape"],
        scratch_shapes=[
            pltpu.VMEM(acc_shape, F32), pltpu.VMEM((2, k, c), BF16), pltpu.SemaphoreType.DMA((n_pairs,)),
            pltpu.SemaphoreType.DMA((n_pairs,)),
        ]
        + carried["scratch_shapes"],
        compiler_params=_params("arbitrary", "arbitrary", vmem_limit_bytes=VMEM_LIMIT_MIX_BYTES),
    )(a, b, *carried["operands"])
    return res[0], res[1], res[2:]


def _input_grad(dproj, win_t, x, dxo, norm_pre, groups):
    s, d = x.shape
    t = min(TILE_GRAD, s)
    n_t = s // t
    kb = 2 * WIN_BLK
    n_k = win_t.shape[0] // kb
    carried = _carry_specs(groups)
    n_p = carried["n"]

    def body(dp_ref, w_ref, x_ref, dxo_ref, g_ref, *refs):
        src = refs[:n_p]
        gx_ref, dg_ref = refs[n_p : n_p + 2]
        landed = refs[n_p + 2 : 2 * n_p + 2]
        acc_ref = refs[2 * n_p + 2]
        start, finish = _carry(groups, src, landed, refs[2 * n_p + 3 :])
        i = pl.program_id(0)
        j = pl.program_id(1)

        @pl.when(jnp.logical_and(i == 0, j == 0))
        def _():
            start()
            dg_ref[...] = jnp.zeros_like(dg_ref)

        @pl.when(j == 0)
        def _():
            acc_ref[...] = jnp.zeros_like(acc_ref)

        acc_ref[...] += _dot(dp_ref[...], w_ref[...])

        @pl.when(j == n_k - 1)
        def _():
            xv = x_ref[...]
            gain = g_ref[...]
            r = lax.rsqrt(_rowmean(xv * xv) + EPS)
            dgain, dx = _rms_bwd(acc_ref[...], gain, r, xv * r)
            dg_ref[...] += dgain
            gx_ref[...] = dxo_ref[...] + dx

        @pl.when(jnp.logical_and(i == n_t - 1, j == n_k - 1))
        def _():
            finish()

    res = pl.pallas_call(
        body,
        name="input_grad",
        grid=(n_t, n_k),
        in_specs=[
            pl.BlockSpec((t, kb), lambda i, j: (i, j)),
            pl.BlockSpec((kb, d), lambda i, j: (j, 0)),
            pl.BlockSpec((t, d), lambda i, j: (i, 0)),
            pl.BlockSpec((t, d), lambda i, j: (i, 0)),
            _full((1, d)),
        ]
        + carried["in_specs"],
        out_specs=[pl.BlockSpec((t, d), lambda i, j: (i, 0)), _full((1, d))] + carried["out_specs"],
        out_shape=[jax.ShapeDtypeStruct((s, d), F32), jax.ShapeDtypeStruct((1, d), F32)] + carried["out_shape"],
        scratch_shapes=[pltpu.VMEM((t, d), F32)] + carried["scratch_shapes"],
        compiler_params=_params("arbitrary", "arbitrary", vmem_limit_bytes=VMEM_LIMIT_MIX_BYTES),
    )(dproj, win_t, x, dxo, norm_pre, *carried["operands"])
    return res[0], res[1], res[2:]


def _kv_backward(dkv, memn, wkv, mem):
    m, d = mem.shape
    n = wkv.shape[1]

    def body(dkv_ref, memn_ref, w_ref, mem_ref, gw_ref, dg_ref):
        dkv_bf = dkv_ref[...].astype(BF16)
        gw_ref[...] = _dot(memn_ref[...], dkv_bf, TN).astype(BF16).reshape(N_DEV, WKV_BLK, n)
        dmemn = _dot(dkv_bf, w_ref[...], NT)
        mv = mem_ref[...]
        r = lax.rsqrt(_rowmean(mv * mv) + EPS)
        dg_ref[...] = _colsum(dmemn * (mv * r))

    return pl.pallas_call(
        body,
        name="kv_backward",
        grid=(1,),
        in_specs=[_full((m, n)), _full((m, d)), _full(wkv.shape), _full((m, d))],
        out_specs=[_full((N_DEV, WKV_BLK, n)), _full((1, d))],
        out_shape=[jax.ShapeDtypeStruct((N_DEV, WKV_BLK, n), BF16), jax.ShapeDtypeStruct((1, d), F32)],
        compiler_params=_params("arbitrary"),
    )(dkv, memn, wkv, mem)


def _adamw_math(w, g, m, v):
    m = ADAM_B1 * m + (1.0 - ADAM_B1) * g
    v = ADAM_B2 * v + (1.0 - ADAM_B2) * (g * g)
    m_hat = m / (1.0 - ADAM_B1**ADAM_STEP)
    v_hat = v / (1.0 - ADAM_B2**ADAM_STEP)
    delta = -ADAM_LR * (m_hat / (jnp.sqrt(v_hat) + ADAM_EPS) + ADAM_WD * w)
    return delta, m, v


def _adamw(parts, w, m, v, name):
    r, c = w.shape
    slots = parts.shape[0]
    t = r
    while t * c * 4 > TILE_ADAM_BYTES and t % 16 == 0:
        t //= 2

    def body(p_ref, w_ref, m_ref, v_ref, g_ref, d_ref, nm_ref, nv_ref):
        g = p_ref[0].astype(F32)
        for k in range(1, slots):
            g = g + p_ref[k].astype(F32)
        delta, nm, nv = _adamw_math(w_ref[...], g, m_ref[...], v_ref[...])
        g_ref[...] = g
        d_ref[...] = delta
        nm_ref[...] = nm
        nv_ref[...] = nv

    tile = pl.BlockSpec((t, c), lambda i: (i, 0))
    return pl.pallas_call(
        body,
        name=name,
        grid=(r // t,),
        in_specs=[pl.BlockSpec((slots, t, c), lambda i: (0, i, 0)), tile, tile, tile],
        out_specs=[tile] * 4,
        out_shape=[jax.ShapeDtypeStruct((r, c), F32)] * 4,
        compiler_params=_params("parallel"),
    )(parts, w, m, v)


def _adamw_packed(parts, triples, name):
    slots = parts.shape[0]
    sizes = [w.shape[0] for w, _, _ in triples]
    rest = parts.shape[1] - sum(sizes)

    def total(p_ref, at, rows):
        g = p_ref[0, at : at + rows, :]
        for k in range(1, slots):
            g = g + p_ref[k, at : at + rows, :]
        return g

    def body(p_ref, *refs):
        ins = refs[: 3 * len(triples)]
        outs = refs[3 * len(triples) :]
        at = 0
        for n, rows in enumerate(sizes):
            g = total(p_ref, at, rows)
            w_ref, m_ref, v_ref = ins[3 * n : 3 * n + 3]
            delta, nm, nv = _adamw_math(w_ref[...], g, m_ref[...], v_ref[...])
            for ref, val in zip(outs[4 * n : 4 * n + 4], (g, delta, nm, nv)):
                ref[...] = val
            at += rows
        if rest:
            outs[-1][...] = total(p_ref, at, rest)

    flat = [a for t in triples for a in t]
    res = pl.pallas_call(
        body,
        name=name,
        out_shape=[jax.ShapeDtypeStruct(w.shape, F32) for w, _, _ in triples for _ in range(4)]
        + ([jax.ShapeDtypeStruct((rest, 128), F32)] if rest else []),
        compiler_params=pltpu.CompilerParams(vmem_limit_bytes=VMEM_LIMIT_BYTES),
    )(parts, *flat)
    return [res[4 * n : 4 * n + 4] for n in range(len(triples))], (res[-1] if rest else None)


SMALL = ("norm_pre", "pool_scale", "sgu_ln_g", "sgu_ln_b", "sgu_w", "sgu_b", "mem_norm", "branch_norm", "norm_post")


def _local_view(name, w):
    if name == "sgu_w":
        return w.reshape(SGU_HEADS, SGU_CHUNK, SGU_CHUNK)
    if name == "sgu_b":
        return w.reshape(SGU_HEADS, SGU_CHUNK)
    return w.reshape(1, -1)


def _forward_backward(x, mem, target, shards, small):
    causal = jnp.tril(jnp.ones((SGU_CHUNK, SGU_CHUNK), dtype=bool))
    sgu_wm = jnp.where(causal[None], small["sgu_w"], 0.0).astype(BF16)
    sgu_bias = jnp.repeat(jnp.transpose(small["sgu_b"]), SGU_HEAD_DIM, axis=1)

    proj, h, (win, wkv, pool_all, wout) = _proj_gather(x, small["norm_pre"], shards)
    wout = wout.reshape(MIX_WIDTH, D_MODEL)
    wkv = wkv.reshape(D_MODEL, 2 * BRANCH)
    pool_full = (
        pool_all.reshape(N_DEV, len(POOL_WINDOWS), POOL_BLK, POOL_GROUP_DIM)
        .transpose(1, 0, 2, 3)
        .reshape(len(POOL_WINDOWS), POOL_GROUP_DIM, POOL_GROUP_DIM)
    )
    memn, kv = _kv_forward(mem, small["mem_norm"], wkv)
    (y, dout, dxo, dproj, loss, d_norm_post, d_branch_norm, d_pool_scale, d_ln_g, d_ln_b, d_pool_w, d_sgu_w, d_sgu_b,
     dkv) = _mix(
        proj, x, target, kv, kv.T, wout, wout.T, pool_full, jnp.swapaxes(pool_full, 1, 2), small["pool_scale"],
        small["sgu_ln_g"], small["sgu_ln_b"], sgu_bias, sgu_wm, jnp.swapaxes(sgu_wm, 1, 2), small["branch_norm"],
        small["norm_post"],
    )
    g_wkv, d_mem_norm = _kv_backward(dkv, memn, wkv, mem)
    g_pool = (
        d_pool_w.reshape(len(POOL_WINDOWS), N_DEV, POOL_BLK, POOL_GROUP_DIM)
        .transpose(1, 0, 2, 3)
        .reshape(N_DEV, len(POOL_WINDOWS) * POOL_BLK, POOL_GROUP_DIM)
        .astype(BF16)
    )
    small_grads = dict(
        pool_scale=d_pool_scale, sgu_ln_g=d_ln_g, sgu_ln_b=d_ln_b, sgu_w=d_sgu_w, sgu_b=d_sgu_b,
        mem_norm=d_mem_norm, branch_norm=d_branch_norm, norm_post=d_norm_post,
    )
    packed = jnp.concatenate(
        [small_grads[n].reshape(-1, LANES) for n in SMALL if n != "norm_pre"]
        + [jnp.broadcast_to(loss, (SUBLANES, LANES))],
        axis=0,
    )
    packed = packed[None, None]

    by_chip = lambda g: g.reshape((N_CHIPS, 2) + g.shape[1:])
    small_mine = [by_chip(g_wkv), by_chip(g_pool), packed]
    g_wout, wout_theirs, small_theirs = _weight_grad(y, dout, N_DEV, "rows", "grad_w_out", [("pair", small_mine)])
    sums, _ = _pair_sum(small_mine + [by_chip(g_wout)], list(small_theirs) + [wout_theirs], "pair_sum_first")
    g_win, win_theirs, (l_wkv, l_pool, l_packed, l_wout) = _weight_grad(
        h, dproj, N_DEV, "cols", "grad_w_in", [("chips", list(sums))]
    )
    (win_sums,), _ = _pair_sum([by_chip(g_win)], [win_theirs], "pair_sum_w_in")
    win_late, _ = lax.optimization_barrier((win, g_wkv))
    grad_x, d_norm_pre, (l_win,) = _input_grad(
        dproj, jnp.swapaxes(win_late, 1, 2).reshape(IN_WIDTH, D_MODEL), x, dxo, small["norm_pre"],
        [("chips", [win_sums])],
    )
    return grad_x, dict(w_in=l_win, w_out=l_wout, w_kv=l_wkv, pool_w=l_pool), l_packed, d_norm_pre


def kernel(x, mem, norm_pre, w_in, pool_w, pool_scale, sgu_ln_g, sgu_ln_b, sgu_w, sgu_b, mem_norm, w_kv, branch_norm, w_out, norm_post, loss_target, m_norm_pre, m_w_in, m_pool_w, m_pool_scale, m_sgu_ln_g, m_sgu_ln_b, m_sgu_w, m_sgu_b, m_mem_norm, m_w_kv, m_branch_norm, m_w_out, m_norm_post, v_norm_pre, v_w_in, v_pool_w, v_pool_scale, v_sgu_ln_g, v_sgu_ln_b, v_sgu_w, v_sgu_b, v_mem_norm, v_w_kv, v_branch_norm, v_w_out, v_norm_post):
    weights = dict(norm_pre=norm_pre, w_in=w_in, pool_w=pool_w, pool_scale=pool_scale, sgu_ln_g=sgu_ln_g, sgu_ln_b=sgu_ln_b, sgu_w=sgu_w, sgu_b=sgu_b, mem_norm=mem_norm, w_kv=w_kv, branch_norm=branch_norm, w_out=w_out, norm_post=norm_post)
    first = dict(norm_pre=m_norm_pre, w_in=m_w_in, pool_w=m_pool_w, pool_scale=m_pool_scale, sgu_ln_g=m_sgu_ln_g, sgu_ln_b=m_sgu_ln_b, sgu_w=m_sgu_w, sgu_b=m_sgu_b, mem_norm=m_mem_norm, w_kv=m_w_kv, branch_norm=m_branch_norm, w_out=m_w_out, norm_post=m_norm_post)
    second = dict(norm_pre=v_norm_pre, w_in=v_w_in, pool_w=v_pool_w, pool_scale=v_pool_scale, sgu_ln_g=v_sgu_ln_g, sgu_ln_b=v_sgu_ln_b, sgu_w=v_sgu_w, sgu_b=v_sgu_b, mem_norm=v_mem_norm, w_kv=v_w_kv, branch_norm=v_branch_norm, w_out=v_w_out, norm_post=v_norm_post)
    order = ("norm_pre", "w_in", "pool_w", "pool_scale", "sgu_ln_g", "sgu_ln_b", "sgu_w", "sgu_b", "mem_norm", "w_kv", "branch_norm", "w_out", "norm_post")

    owned_shape = dict(
        w_in=(D_MODEL, WIN_BLK), w_out=(WOUT_BLK, D_MODEL), w_kv=(WKV_BLK, 2 * BRANCH),
        pool_w=(len(POOL_WINDOWS) * POOL_BLK, POOL_GROUP_DIM),
    )
    owned = {n: weights[n].reshape(owned_shape[n]) for n in owned_shape}
    small = {n: _local_view(n, weights[n]) for n in SMALL}
    grad_x, landed, landed_packed, d_norm_pre = _forward_backward(
        x[0], mem[0], loss_target[0], [owned[n].astype(BF16) for n in ("w_in", "w_kv", "pool_w", "w_out")], small
    )
    d_norm_pre = d_norm_pre.reshape(-1, 128)
    (landed_norm_pre,) = _exchange(
        [("all", [jnp.broadcast_to(d_norm_pre[None], (N_DEV,) + d_norm_pre.shape)])], "exchange_norm_pre"
    )

    grads, deltas, new_m, new_v = {}, {}, {}, {}
    for n in owned_shape:
        shape = weights[n].shape
        res = _adamw(
            landed[n], owned[n], first[n].reshape(owned_shape[n]), second[n].reshape(owned_shape[n]), "adamw_" + n
        )
        grads[n], deltas[n], new_m[n], new_v[n] = (a.reshape(shape) for a in res)
    rows_of = lambda tree, n: tree[n].reshape(-1, 128)
    for names, parts, name in (
        ([n for n in SMALL if n != "norm_pre"], landed_packed, "adamw_replicated"),
        (["norm_pre"], landed_norm_pre, "adamw_norm_pre"),
    ):
        res, rest = _adamw_packed(
            parts, [(rows_of(weights, n), rows_of(first, n), rows_of(second, n)) for n in names], name
        )
        if rest is not None:
            total = rest[0, 0]
        for n, four in zip(names, res):
            for tree, a in zip((grads, deltas, new_m, new_v), four):
                tree[n] = a.reshape(weights[n].shape)

    return (
        total,
        grad_x[None],
        *[grads[n] for n in order],
        *[deltas[n] for n in order],
        *[new_m[n] for n in order],
        *[new_v[n] for n in order],
    )
```

```python
import jax
import jax.numpy as jnp
from jax import lax
from jax.experimental import pallas as pl
from jax.experimental.pallas import tpu as pltpu

F32 = jnp.float32
BF16 = jnp.bfloat16
EPS = 1e-6

D_MODEL = 2048
POOL_WINDOWS = (2, 4, 8, 16)
POOL_GROUP_DIM = 256
BRANCH = 1024
SGU_CHUNK = 128
SGU_HEADS = 8
SGU_HEAD_DIM = 128
XATTN_HEADS = 4
XATTN_HEAD_DIM = 256
MIX_WIDTH = 3 * BRANCH
IN_WIDTH = 7 * BRANCH
N_DEV = 8
WIN_BLK = IN_WIDTH // N_DEV
WOUT_BLK = MIX_WIDTH // N_DEV
WKV_BLK = D_MODEL // N_DEV
POOL_BLK = POOL_GROUP_DIM // N_DEV
HALO = 16
LANES = 128
SUBLANES = 8

ADAM_LR = 0.001
ADAM_B1 = 0.9
ADAM_B2 = 0.999
ADAM_EPS = 1e-08
ADAM_WD = 0.01
ADAM_STEP = 10

VMEM_LIMIT_BYTES = 56 * 1024 * 1024
VMEM_LIMIT_MIX_BYTES = 63 * 1024 * 1024

TILE_PROJ = 512
TILE_MIX = 128
TILE_GRAD = 512
TILE_WEIGHT_GRAD = 1024
TILE_ADAM_BYTES = 1 << 20

ANY = pl.BlockSpec(memory_space=pl.ANY)
NN = (((1,), (0,)), ((), ()))
NT = (((1,), (1,)), ((), ()))
TN = (((0,), (0,)), ((), ()))
MESH = pl.DeviceIdType.MESH


def _dot(a, b, dims=NN):
    return lax.dot_general(a, b, dims, preferred_element_type=F32)


def _params(*semantics, vmem_limit_bytes=VMEM_LIMIT_BYTES):
    return pltpu.CompilerParams(dimension_semantics=semantics, vmem_limit_bytes=vmem_limit_bytes)


def _rowmean(a):
    return jnp.mean(a, axis=-1, keepdims=True)


def _colsum(a):
    return jnp.sum(a, axis=0, keepdims=True)


def _full(shape):
    zeros = (0,) * len(shape)
    return pl.BlockSpec(shape, lambda *_: zeros)


def _resident(shape):
    zeros = (0,) * len(shape)
    return pl.BlockSpec(shape, lambda *_: zeros, pipeline_mode=pl.Buffered(1))


def _kv_forward(mem, mem_norm, wkv):
    m, d = mem.shape
    n = wkv.shape[1]
    cols = 4 * LANES

    def body(mem_ref, g_ref, w_ref, memn_ref, kv_ref):
        mv = mem_ref[...]
        r = lax.rsqrt(_rowmean(mv * mv) + EPS)
        memn = (mv * r * g_ref[...]).astype(BF16)
        memn_ref[...] = memn
        kv_ref[...] = _dot(memn, w_ref[...]).astype(BF16)

    return pl.pallas_call(
        body,
        name="kv_forward",
        grid=(n // cols,),
        in_specs=[_full((m, d)), _full((1, d)), pl.BlockSpec((d, cols), lambda j: (0, j))],
        out_specs=[_full((m, d)), pl.BlockSpec((m, cols), lambda j: (0, j))],
        out_shape=[jax.ShapeDtypeStruct((m, d), BF16), jax.ShapeDtypeStruct((m, n), BF16)],
        compiler_params=_params("arbitrary"),
    )(mem, mem_norm, wkv)


def _proj_gather(x_in, norm_pre, shards):
    s, d = x_in.shape
    t = min(TILE_PROJ, s)
    n_t = s // t
    n_arr = len(shards)

    def places(x, y, c):
        return (x, y, c), (x, y, 1 - c), (x ^ c, y ^ (1 - c)), (x ^ (1 - c), y ^ c), (1 - x, 1 - y)

    def index(chip, core):
        return 4 * chip[0] + 2 * chip[1] + core

    _, _, chip_a, chip_b, chip_d = places(*_position())
    c_out = lax.axis_index("c")
    me_out = index((lax.axis_index("x"), lax.axis_index("y")), c_out)
    order = jnp.stack(
        [
            me_out, me_out ^ 1, index(chip_a, c_out), index(chip_b, c_out), index(chip_b, 1 - c_out),
            index(chip_a, 1 - c_out), index(chip_d, c_out), index(chip_d, 1 - c_out),
        ]
    ).astype(jnp.int32)

    def body(order_ref, x_ref, g_ref, *refs):
        del order_ref
        src = refs[:n_arr]
        proj_ref, h_ref = refs[n_arr : n_arr + 2]
        out = refs[n_arr + 2 : 2 * n_arr + 2]
        wbuf, hs, send_sems, recv_sems, local_sems, load_sems = refs[2 * n_arr + 2 :]
        j = pl.program_id(0)
        i = pl.program_id(1)
        me, sibling, chip_a, chip_b, chip_d = places(*_position())
        c = me[2]

        def block(a, chip, core):
            return out[a].at[index(chip, core)]

        def copy(a, k, owner, to, from_input=False):
            return pltpu.make_async_remote_copy(
                src_ref=src[a] if from_input else block(a, *owner),
                dst_ref=block(a, *owner),
                send_sem=send_sems.at[a, k],
                recv_sem=recv_sems.at[a, k],
                device_id=to,
                device_id_type=MESH,
            )

        mine = (me[:2], c)

        def own(a):
            return pltpu.make_async_copy(src[a], block(a, *mine), local_sems.at[a])

        def first_sends(a):
            return [
                copy(a, 0, mine, sibling, from_input=True),
                copy(a, 1, mine, (*chip_a, c), from_input=True),
                copy(a, 2, mine, (*chip_b, c), from_input=True),
            ]

        def onward(a, k):
            owner = {3: chip_a, 4: chip_a, 5: chip_b, 6: chip_d}[k]
            return copy(a, k, (owner, c), (*chip_b, c) if k == 3 else sibling)

        def landed(a, k):
            owner = {0: mine[0], 1: chip_a, 2: chip_b, 3: chip_d, 4: chip_b, 5: chip_a, 6: chip_d}[k]
            core = c if k in (1, 2, 3) else 1 - c
            copy(a, k, (owner, core), me).wait_recv()
            return owner, core

        def load(ref, step):
            return pltpu.make_async_copy(ref, wbuf.at[step % 2], load_sems.at[step % 2])

        @pl.when(jnp.logical_and(j == 0, i == 0))
        def _():
            own(0).start()
            for cp in first_sends(0):
                cp.start()
            load(src[0], 0).start()
            load(src[0], 0).wait()

        steps = {1: (0, ()), 2: (1, (3, 4)), 3: (2, (5,)), 4: (4, ()), 5: (5, ()), 6: (3, (6,)), 7: (6, ())}
        for step, (k, then) in steps.items():

            @pl.when(jnp.logical_and(j == step, i == 0))
            def _():
                load(src[0], step).wait()

            @pl.when(jnp.logical_and(j == step - 1, i == n_t - 1))
            def _():
                owner = landed(0, k)
                for k2 in then:
                    onward(0, k2).start()
                if k == 1:
                    for a in range(1, n_arr):
                        own(a).start()
                        for cp in first_sends(a):
                            cp.start()
                if k == 3:
                    for a in range(1, n_arr):
                        for k1, then1 in ((1, (3, 4)), (2, (5,))):
                            landed(a, k1)
                            for k2 in then1:
                                onward(a, k2).start()
                load(block(0, *owner), step).start()

        @pl.when(j == 0)
        def _():
            xv = x_ref[...]
            h = (xv * lax.rsqrt(_rowmean(xv * xv) + EPS) * g_ref[...]).astype(BF16)
            hs[i] = h
            h_ref[...] = h

        proj_ref[...] = _dot(hs[i], wbuf[j % 2]).astype(BF16)

        @pl.when(jnp.logical_and(j == N_DEV - 1, i == n_t - 1))
        def _():
            for a in range(1, n_arr):
                landed(a, 3)
                onward(a, 6).start()
            for a in range(1, n_arr):
                for k in (0, 4, 5, 6):
                    landed(a, k)
            for a in range(n_arr):
                for cp in first_sends(a) + [onward(a, k) for k in (3, 4, 5, 6)]:
                    cp.wait_send()
                own(a).wait()

    res = pl.pallas_call(
        body,
        name="proj_gather",
        grid_spec=pltpu.PrefetchScalarGridSpec(
            num_scalar_prefetch=1,
            grid=(N_DEV, n_t),
            in_specs=[
                pl.BlockSpec((t, d), lambda j, i, order_ref: (jnp.where(j == 0, i, n_t - 1), 0)),
                pl.BlockSpec((1, d), lambda j, i, order_ref: (0, 0)),
            ]
            + [ANY] * n_arr,
            out_specs=[
                pl.BlockSpec((t, WIN_BLK), lambda j, i, order_ref: (i, order_ref[j])),
                pl.BlockSpec((t, d), lambda j, i, order_ref: (jnp.where(j == 0, i, n_t - 1), 0)),
            ]
            + [ANY] * n_arr,
            scratch_shapes=[
                pltpu.VMEM((2,) + shards[0].shape, BF16),
                pltpu.VMEM((n_t, t, d), BF16),
                pltpu.SemaphoreType.DMA((n_arr, 7)),
                pltpu.SemaphoreType.DMA((n_arr, 7)),
                pltpu.SemaphoreType.DMA((n_arr,)),
                pltpu.SemaphoreType.DMA((2,)),
            ],
        ),
        out_shape=[jax.ShapeDtypeStruct((s, IN_WIDTH), BF16), jax.ShapeDtypeStruct((s, d), BF16)]
        + [jax.ShapeDtypeStruct((N_DEV,) + a.shape, a.dtype) for a in shards],
        compiler_params=_params("arbitrary", "arbitrary"),
    )(order, x_in, norm_pre, *shards)
    return res[0], res[1], res[2:]


def _sigmoid(a):
    return jax.nn.sigmoid(a)


def _dsilu(a, sg):
    return sg * (1.0 + a * (1.0 - sg))


def _rms_fwd(u, gain):
    r = lax.rsqrt(_rowmean(u * u) + EPS)
    n = u * r
    return r, n, n * gain


def _rms_bwd(dy, gain, r, n):
    dn = dy * gain
    return _colsum(dy * n), r * (dn - n * _rowmean(dn * n))


def _mix(proj, x, target, kv, kv_t, wout, wout_t, pool_w, pool_w_t, pool_scale, ln_g, ln_b, sgu_bias, sgu_wm, sgu_wm_t, branch_norm, norm_post):
    s, d = x.shape
    t = min(TILE_MIX, s)
    n_tiles = s // t
    n_chunks = t // SGU_CHUNK
    halo_blocks_per_tile = t // HALO
    inv_d = 1.0 / d
    scale = 1.0 / (XATTN_HEAD_DIM**0.5)

    def body(
        proj_ref, halo_ref, x_ref, tgt_ref, kv_ref, kvt_ref, wout_hbm, wout_t_hbm, pw_ref, pwt_ref, pscale_ref, lng_ref,
        lnb_ref, bias_ref, wm_ref, wmt_ref, bnorm_ref, gpost_ref,
        y_ref, dout_ref, dxo_ref, dproj_ref, loss_ref, dgpost_ref, dbnorm_ref, dpscale_ref, dlng_ref, dlnb_ref,
        dpw_out, dwm_out, dbias_ref, dkv_out,
        carry_ref, dzsum_ref, dpw_ref, dwm_ref, dkv_ref, wout_ref, wout_t_ref, wout_sems,
    ):
        i = pl.program_id(0)
        tile = n_tiles - 1 - i
        wout_load = pltpu.make_async_copy(wout_hbm, wout_ref, wout_sems.at[0])
        wout_t_load = pltpu.make_async_copy(wout_t_hbm, wout_t_ref, wout_sems.at[1])

        @pl.when(i == 0)
        def _():
            wout_load.start()
            wout_t_load.start()
            carry_ref[...] = jnp.zeros_like(carry_ref)
            dzsum_ref[...] = jnp.zeros_like(dzsum_ref)
            for ref in (loss_ref, dgpost_ref, dbnorm_ref, dpscale_ref, dlng_ref, dlnb_ref, dpw_ref, dwm_ref, dkv_ref):
                ref[...] = jnp.zeros_like(ref)

        t_glob = tile * t + lax.broadcasted_iota(jnp.int32, (t, 1), 0)
        inv_cnt = [1.0 / jnp.minimum(t_glob + 1, w).astype(F32) for w in POOL_WINDOWS]

        xa = proj_ref[:, 0:BRANCH].astype(F32)
        ga = proj_ref[:, BRANCH : 2 * BRANCH].astype(F32)
        halo = jnp.where(tile == 0, 0.0, halo_ref[...].astype(F32))
        d_bf, pm_parts = [], []
        for g, w in enumerate(POOL_WINDOWS):
            cols = slice(g * POOL_GROUP_DIM, (g + 1) * POOL_GROUP_DIM)
            acc = jnp.concatenate([halo[:, cols], xa[:, cols]], axis=0)
            k = 1
            while k < w:
                acc = acc + pltpu.roll(acc, k, axis=0)
                k *= 2
            dg = (acc[HALO:, :] * inv_cnt[g] - xa[:, cols]).astype(BF16)
            d_bf.append(dg)
            pm_parts.append(_dot(dg, pw_ref[g]))
        pm = jnp.concatenate(pm_parts, axis=1)
        pscale = pscale_ref[...]
        pa = pm * pscale
        sga = _sigmoid(ga)
        sila = ga * sga
        ua = pa * sila
        g_a = bnorm_ref[:, 0:BRANCH]
        ra, na, ya = _rms_fwd(ua, g_a)

        u = proj_ref[:, 2 * BRANCH : 3 * BRANCH].astype(F32)
        v = proj_ref[:, 3 * BRANCH : 4 * BRANCH].astype(F32)
        gb = proj_ref[:, 4 * BRANCH : 5 * BRANCH].astype(F32)
        lng = lng_ref[...]
        vc = v - _rowmean(v)
        rstd = lax.rsqrt(_rowmean(vc * vc) + EPS)
        vhat = vc * rstd
        vn_bf = (vhat * lng + lnb_ref[...]).astype(BF16)
        z_rows = []
        for c in range(n_chunks):
            rows = slice(c * SGU_CHUNK, (c + 1) * SGU_CHUNK)
            z_rows.append(
                jnp.concatenate(
                    [
                        _dot(wm_ref[hd], vn_bf[rows, hd * SGU_HEAD_DIM : (hd + 1) * SGU_HEAD_DIM])
                        for hd in range(SGU_HEADS)
                    ],
                    axis=1,
                )
                + bias_ref[...]
            )
        z = z_rows[0] if n_chunks == 1 else jnp.concatenate(z_rows, axis=0)
        sb = u * z
        sgb = _sigmoid(gb)
        silb = gb * sgb
        ub = sb * silb
        g_b = bnorm_ref[:, BRANCH : 2 * BRANCH]
        rb, nb, yb = _rms_fwd(ub, g_b)

        q = proj_ref[:, 5 * BRANCH : 6 * BRANCH]
        gc = proj_ref[:, 6 * BRANCH : 7 * BRANCH].astype(F32)
        q_bf, p_bf, o_parts = [], [], []
        for hd in range(XATTN_HEADS):
            cols = slice(hd * XATTN_HEAD_DIM, (hd + 1) * XATTN_HEAD_DIM)
            qh = q[:, cols]
            sc = _dot(qh, kvt_ref[cols, :]) * scale
            e = jnp.exp(sc - jnp.max(sc, axis=-1, keepdims=True))
            p = e / jnp.sum(e, axis=-1, keepdims=True)
            q_bf.append(qh)
            p_bf.append(p.astype(BF16))
            o_parts.append(_dot(p_bf[hd], kv_ref[:, BRANCH + hd * XATTN_HEAD_DIM : BRANCH + (hd + 1) * XATTN_HEAD_DIM]))
        o = jnp.concatenate(o_parts, axis=1)
        sgc = _sigmoid(gc)
        silc = gc * sgc
        uc = o * silc
        g_c = bnorm_ref[:, 2 * BRANCH : 3 * BRANCH]
        rc, nc, yc = _rms_fwd(uc, g_c)

        @pl.when(i == 0)
        def _():
            wout_load.wait()

        out = None
        for b, y_branch in enumerate((ya, yb, yc)):
            rows = slice(b * BRANCH, (b + 1) * BRANCH)
            y_bf = y_branch.astype(BF16)
            y_ref[:, rows] = y_bf
            part = _dot(y_bf, wout_ref[rows, :])
            out = part if out is None else out + part
        gpost = gpost_ref[...]
        r_out = lax.rsqrt(_rowmean(out * out) + EPS)
        on = out * r_out
        err = x_ref[...] + on * gpost - tgt_ref[...]
        loss_ref[...] += 0.5 * jnp.sum(_rowmean(err * err), axis=0, keepdims=True)

        dxo = err * inv_d
        dxo_ref[...] = dxo
        dgp, dout = _rms_bwd(dxo, gpost, r_out, on)
        dgpost_ref[...] += dgp
        dout_bf = dout.astype(BF16)
        dout_ref[...] = dout_bf

        @pl.when(i == 0)
        def _():
            wout_t_load.wait()

        dy = [_dot(dout_bf, wout_t_ref[:, b * BRANCH : (b + 1) * BRANCH]) for b in range(3)]

        dg_a, dua = _rms_bwd(dy[0], g_a, ra, na)
        dg_b, dub = _rms_bwd(dy[1], g_b, rb, nb)
        dg_c, duc = _rms_bwd(dy[2], g_c, rc, nc)
        dbnorm_ref[...] += jnp.concatenate([dg_a, dg_b, dg_c], axis=1)

        dpa = dua * sila
        dga = dua * pa * _dsilu(ga, sga)
        dpscale_ref[...] += _colsum(dpa * pm)
        dpm = dpa * pscale
        dxa_parts, carry_parts = [], []
        for g, w in enumerate(POOL_WINDOWS):
            cols = slice(g * POOL_GROUP_DIM, (g + 1) * POOL_GROUP_DIM)
            dpm_g = dpm[:, cols].astype(BF16)
            dd = _dot(dpm_g, pwt_ref[g])
            dpw_ref[g] += _dot(d_bf[g], dpm_g, TN)
            cg = dd * inv_cnt[g]
            carry_parts.append(cg[0:HALO, :])
            acc = jnp.concatenate([cg, carry_ref[:, cols]], axis=0)
            k = 1
            while k < w:
                acc = acc + pltpu.roll(acc, t + HALO - k, axis=0)
                k *= 2
            dxa_parts.append(acc[0:t, :] - dd)
        carry_ref[...] = jnp.concatenate(carry_parts, axis=1)
        dxa = jnp.concatenate(dxa_parts, axis=1)

        dsb = dub * silb
        dgb = dub * sb * _dsilu(gb, sgb)
        du = dsb * z
        dz = dsb * u
        dz_bf = dz.astype(BF16)
        dvn_rows = []
        dz_sum = None
        for c in range(n_chunks):
            rows = slice(c * SGU_CHUNK, (c + 1) * SGU_CHUNK)
            dz_sum = dz[rows, :] if dz_sum is None else dz_sum + dz[rows, :]
            parts = []
            for hd in range(SGU_HEADS):
                cols = slice(hd * SGU_HEAD_DIM, (hd + 1) * SGU_HEAD_DIM)
                parts.append(_dot(wmt_ref[hd], dz_bf[rows, cols]))
                dwm_ref[hd] += _dot(dz_bf[rows, cols], vn_bf[rows, cols], NT)
            dvn_rows.append(jnp.concatenate(parts, axis=1))
        dzsum_ref[...] += dz_sum
        dvn = dvn_rows[0] if n_chunks == 1 else jnp.concatenate(dvn_rows, axis=0)
        dlng_ref[...] += _colsum(dvn * vhat)
        dlnb_ref[...] += _colsum(dvn)
        dvh = dvn * lng
        dv = rstd * (dvh - _rowmean(dvh) - vhat * _rowmean(dvh * vhat))

        do = duc * silc
        dgc = duc * o * _dsilu(gc, sgc)
        dq_parts = []
        for hd in range(XATTN_HEADS):
            cols = slice(hd * XATTN_HEAD_DIM, (hd + 1) * XATTN_HEAD_DIM)
            vcols = slice(BRANCH + hd * XATTN_HEAD_DIM, BRANCH + (hd + 1) * XATTN_HEAD_DIM)
            do_h = do[:, cols].astype(BF16)
            p = p_bf[hd].astype(F32)
            dp = _dot(do_h, kvt_ref[vcols, :])
            dkv_ref[:, vcols] += _dot(p_bf[hd], do_h, TN)
            ds_bf = (p * (dp - jnp.sum(dp * p, axis=-1, keepdims=True)) * scale).astype(BF16)
            dq_parts.append(_dot(ds_bf, kv_ref[:, cols]))
            dkv_ref[:, cols] += _dot(ds_bf, q_bf[hd], TN)
        dq = jnp.concatenate(dq_parts, axis=1)

        dproj_ref[...] = jnp.concatenate([dxa, dga, du, dv, dgb, dq, dgc], axis=1).astype(BF16)

        @pl.when(i == n_tiles - 1)
        def _():
            keep = lax.broadcasted_iota(jnp.int32, (SGU_CHUNK, SGU_CHUNK), 0) >= lax.broadcasted_iota(
                jnp.int32, (SGU_CHUNK, SGU_CHUNK), 1
            )
            for hd in range(SGU_HEADS):
                dwm_ref[hd] = jnp.where(keep, dwm_ref[hd], 0.0)
                per_pos = dzsum_ref[:, hd * SGU_HEAD_DIM : (hd + 1) * SGU_HEAD_DIM]
                dbias_ref[hd : hd + 1, :] = _colsum(per_pos.T)
            for acc, res in ((dpw_ref, dpw_out), (dwm_ref, dwm_out), (dkv_ref, dkv_out)):
                pltpu.sync_copy(acc, res)

    row_tile = lambda width: pl.BlockSpec((t, width), lambda i: (n_tiles - 1 - i, 0))
    halo_spec = pl.BlockSpec(
        (HALO, BRANCH), lambda i: (jnp.maximum((n_tiles - 1 - i) * halo_blocks_per_tile - 1, 0), 0)
    )
    acc_shapes = [
        (1, 128),
        (1, d),
        (1, MIX_WIDTH),
        (1, BRANCH),
        (1, BRANCH),
        (1, BRANCH),
        pool_w.shape,
        sgu_wm.shape,
        (SGU_HEADS, SGU_CHUNK),
        kv.shape,
    ]
    return pl.pallas_call(
        body,
        name="mix",
        grid=(n_tiles,),
        in_specs=[
            row_tile(IN_WIDTH), halo_spec, row_tile(d), row_tile(d), _resident(kv.shape), _resident(kv_t.shape),
            ANY, ANY, _resident(pool_w.shape), _resident(pool_w_t.shape),
            _full((1, BRANCH)), _full((1, BRANCH)), _full((1, BRANCH)), _resident((SGU_CHUNK, BRANCH)),
            _resident(sgu_wm.shape), _resident(sgu_wm_t.shape), _full((1, MIX_WIDTH)), _full((1, d)),
        ],
        out_specs=[row_tile(MIX_WIDTH), row_tile(d), row_tile(d), row_tile(IN_WIDTH)]
        + [ANY if len(a) == 3 or a == kv.shape else _full(a) for a in acc_shapes],
        out_shape=[
            jax.ShapeDtypeStruct((s, MIX_WIDTH), BF16),
            jax.ShapeDtypeStruct((s, d), BF16),
            jax.ShapeDtypeStruct((s, d), F32),
            jax.ShapeDtypeStruct((s, IN_WIDTH), BF16),
        ]
        + [jax.ShapeDtypeStruct(a, F32) for a in acc_shapes],
        scratch_shapes=[
            pltpu.VMEM((HALO, BRANCH), F32), pltpu.VMEM((SGU_CHUNK, BRANCH), F32), pltpu.VMEM(pool_w.shape, F32),
            pltpu.VMEM(sgu_wm.shape, F32), pltpu.VMEM(kv.shape, F32), pltpu.VMEM(wout.shape, BF16),
            pltpu.VMEM(wout_t.shape, BF16), pltpu.SemaphoreType.DMA((2,)),
        ],
        compiler_params=_params("arbitrary", vmem_limit_bytes=VMEM_LIMIT_MIX_BYTES),
    )(
        proj, proj, x, target, kv, kv_t, wout, wout_t, pool_w, pool_w_t, pool_scale, ln_g, ln_b, sgu_bias, sgu_wm,
        sgu_wm_t, branch_norm, norm_post,
    )


def _position():
    return lax.axis_index("x"), lax.axis_index("y"), lax.axis_index("c")


N_CHIPS = 4


def _landing_shape(kind, a):
    return (N_CHIPS,) + a.shape[2:] if kind == "pair" else a.shape


def _carry_specs(groups):
    arrays = [(kind, a) for kind, arrs in groups for a in arrs]
    scratch = []
    for _, arrs in groups:
        n = len(arrs)
        scratch += [pltpu.SemaphoreType.DMA((n, N_DEV)), pltpu.SemaphoreType.DMA((n, N_DEV)), pltpu.SemaphoreType.DMA((n,))]
    return dict(
        n=len(arrays),
        operands=[a for _, a in arrays],
        in_specs=[ANY] * len(arrays),
        out_specs=[ANY] * len(arrays),
        out_shape=[jax.ShapeDtypeStruct(_landing_shape(kind, a), a.dtype) for kind, a in arrays],
        scratch_shapes=scratch,
    )


def _carry(groups, src, out, sems):
    x, y, c = _position()
    chip = 2 * x + y
    me = 2 * chip + c

    def remote(s, d, send_sems, recv_sems, a, m, to):
        return pltpu.make_async_remote_copy(
            src_ref=s, dst_ref=d, send_sem=send_sems.at[a, m], recv_sem=recv_sems.at[a, m], device_id=to,
            device_id_type=MESH,
        )

    def copies():
        far, near = [], []
        at = 0
        for g, (kind, arrs) in enumerate(groups):
            send_sems, recv_sems, local_sems = sems[3 * g : 3 * g + 3]
            for a in range(len(arrs)):
                s, d = src[at + a], out[at + a]
                if kind == "pair" and s.shape[0] == 1:
                    for b in range(N_CHIPS):
                        far.append(remote(s.at[0, 0], d.at[b], send_sems, recv_sems, a, 1 + b, (x, y, 1 - c)))
                elif kind == "pair":
                    far.append(remote(s.at[:, 1 - c], d, send_sems, recv_sems, a, 1, (x, y, 1 - c)))
                elif kind == "chips":
                    for m in range(1, N_CHIPS):
                        px, py = x ^ (m >> 1), y ^ (m & 1)
                        far.append(remote(s.at[2 * px + py], d.at[chip], send_sems, recv_sems, a, m, (px, py, c)))
                    near.append(pltpu.make_async_copy(s.at[chip], d.at[chip], local_sems.at[a]))
                else:
                    for m in range(1, N_DEV):
                        px, py, pc = x ^ ((m >> 2) & 1), y ^ ((m >> 1) & 1), c ^ (m & 1)
                        far.append(
                            remote(s.at[4 * px + 2 * py + pc], d.at[me], send_sems, recv_sems, a, m, (px, py, pc))
                        )
                    near.append(pltpu.make_async_copy(s.at[me], d.at[me], local_sems.at[a]))
            at += len(arrs)
        return far, near

    def start():
        far, near = copies()
        for cp in near + far:
            cp.start()

    def finish():
        far, near = copies()
        for cp in far:
            cp.wait_recv()
        for cp in far:
            cp.wait_send()
        for cp in near:
            cp.wait()

    return start, finish


def _exchange(groups, name):
    carried = _carry_specs(groups)
    n_c = carried["n"]

    def body(*refs):
        start, finish = _carry(groups, refs[:n_c], refs[n_c : 2 * n_c], refs[2 * n_c :])
        start()
        finish()

    return pl.pallas_call(
        body,
        name=name,
        in_specs=carried["in_specs"],
        out_specs=carried["out_specs"],
        out_shape=carried["out_shape"],
        scratch_shapes=carried["scratch_shapes"],
    )(*carried["operands"])


def _pair_sum(mine, theirs, name, groups=()):
    n = len(mine)
    carried = _carry_specs(groups)
    n_c = carried["n"]
    core = lax.axis_index("c").astype(jnp.int32).reshape(1)

    def body(core_ref, *refs):
        del core_ref
        own = refs[:n]
        sib = refs[n : 2 * n]
        src = refs[2 * n : 2 * n + n_c]
        out = refs[2 * n + n_c : 3 * n + n_c]
        landed = refs[3 * n + n_c : 3 * n + 2 * n_c]
        start, finish = _carry(groups, src, landed, refs[3 * n + 2 * n_c :])
        b = pl.program_id(0)

        @pl.when(b == 0)
        def _():
            start()

        for a in range(n):
            out[a][...] = (own[a][...].astype(F32) + sib[a][...].astype(F32)).astype(out[a].dtype)

        @pl.when(b == N_CHIPS - 1)
        def _():
            finish()

    block = lambda a: pl.BlockSpec((None,) + a.shape[1:], lambda b, core_ref: (b, 0, 0))
    res = pl.pallas_call(
        body,
        name=name,
        grid_spec=pltpu.PrefetchScalarGridSpec(
            num_scalar_prefetch=1,
            grid=(N_CHIPS,),
            in_specs=[
                pl.BlockSpec((None, None) + a.shape[2:], lambda b, core_ref: (0, 0, 0, 0))
                if a.shape[0] == 1
                else pl.BlockSpec((None, None) + a.shape[2:], lambda b, core_ref: (b, core_ref[0], 0, 0))
                for a in mine
            ]
            + [block(a) for a in theirs]
            + carried["in_specs"],
            out_specs=[block(a) for a in theirs] + carried["out_specs"],
            scratch_shapes=carried["scratch_shapes"],
        ),
        out_shape=[jax.ShapeDtypeStruct(a.shape, a.dtype) for a in theirs] + carried["out_shape"],
        compiler_params=_params("arbitrary"),
    )(core, *mine, *theirs, *carried["operands"])
    return res[:n], res[n:]


def _weight_grad(a, b, n_blk, blocked, name, groups):
    s = a.shape[0]
    t = min(TILE_WEIGHT_GRAD, s)
    n_t = s // t
    n_pairs = n_blk // 2
    if blocked == "cols":
        k, c = a.shape[1], b.shape[1] // n_blk
        a_spec = pl.BlockSpec((t, k), lambda j, i: (i, 0))
        b_spec = pl.BlockSpec((t, 2 * c), lambda j, i: (i, j))
        acc_shape = (k, 2 * c)
    else:
        k, c = a.shape[1] // n_blk, b.shape[1]
        a_spec = pl.BlockSpec((t, 2 * k), lambda j, i: (i, j))
        b_spec = pl.BlockSpec((t, c), lambda j, i: (i, 0))
        acc_shape = (2 * k, c)
    carried = _carry_specs(groups)
    n_p = carried["n"]

    def body(a_ref, b_ref, *refs):
        src = refs[:n_p]
        o_ref, theirs_ref = refs[n_p : n_p + 2]
        landed = refs[n_p + 2 : 2 * n_p + 2]
        acc_ref, sbuf, pair_send, pair_recv = refs[2 * n_p + 2 : 2 * n_p + 6]
        start, finish = _carry(groups, src, landed, refs[2 * n_p + 6 :])
        j = pl.program_id(0)
        i = pl.program_id(1)
        x, y, c_me = _position()

        def to_sibling(pair):
            return pltpu.make_async_remote_copy(
                src_ref=sbuf.at[1 - c_me], dst_ref=theirs_ref.at[pair], send_sem=pair_send.at[pair],
                recv_sem=pair_recv.at[pair], device_id=(x, y, 1 - c_me), device_id_type=MESH,
            )

        @pl.when(jnp.logical_and(j == 0, i == 0))
        def _():
            start()

        @pl.when(i == 0)
        def _():
            acc_ref[...] = jnp.zeros_like(acc_ref)

        acc_ref[...] += _dot(a_ref[...], b_ref[...], TN)

        @pl.when(i == n_t - 1)
        def _():
            for pair in range(1, n_pairs):

                @pl.when(j == pair)
                def _():
                    to_sibling(pair - 1).wait_send()

            for half in range(2):
                if blocked == "cols":
                    block = acc_ref[:, half * c : (half + 1) * c].astype(BF16)
                else:
                    block = acc_ref[half * k : (half + 1) * k, :].astype(BF16)
                o_ref[half] = block
                sbuf[half] = block
            for pair in range(n_pairs):

                @pl.when(j == pair)
                def _():
                    to_sibling(pair).start()

        @pl.when(jnp.logical_and(j == n_pairs - 1, i == n_t - 1))
        def _():
            to_sibling(n_pairs - 1).wait_send()
            for pair in range(n_pairs):
                to_sibling(pair).wait_recv()
            finish()

    res = pl.pallas_call(
        body,
        name=name,
        grid=(n_pairs, n_t),
        in_specs=[a_spec, b_spec] + carried["in_specs"],
        out_specs=[pl.BlockSpec((2, k, c), lambda j, i: (j, 0, 0)), ANY] + carried["out_specs"],
        out_shape=[jax.ShapeDtypeStruct((n_blk, k, c), BF16), jax.ShapeDtypeStruct((n_pairs, k, c), BF16)]
        + carried["out_shape"],
        scratch_shapes=[
            pltpu.VMEM(acc_shape, F32), pltpu.VMEM((2, k, c), BF16), pltpu.SemaphoreType.DMA((n_pairs,)),
            pltpu.SemaphoreType.DMA((n_pairs,)),
        ]
        + carried["scratch_shapes"],
        compiler_params=_params("arbitrary", "arbitrary", vmem_limit_bytes=VMEM_LIMIT_MIX_BYTES),
    )(a, b, *carried["operands"])
    return res[0], res[1], res[2:]


def _input_grad(dproj, win_t, x, dxo, norm_pre, groups):
    s, d = x.shape
    t = min(TILE_GRAD, s)
    n_t = s // t
    kb = 2 * WIN_BLK
    n_k = win_t.shape[0] // kb
    carried = _carry_specs(groups)
    n_p = carried["n"]

    def body(dp_ref, w_ref, x_ref, dxo_ref, g_ref, *refs):
        src = refs[:n_p]
        gx_ref, dg_ref = refs[n_p : n_p + 2]
        landed = refs[n_p + 2 : 2 * n_p + 2]
        acc_ref = refs[2 * n_p + 2]
        start, finish = _carry(groups, src, landed, refs[2 * n_p + 3 :])
        i = pl.program_id(0)
        j = pl.program_id(1)

        @pl.when(jnp.logical_and(i == 0, j == 0))
        def _():
            start()
            dg_ref[...] = jnp.zeros_like(dg_ref)

        @pl.when(j == 0)
        def _():
            acc_ref[...] = jnp.zeros_like(acc_ref)

        acc_ref[...] += _dot(dp_ref[...], w_ref[...])

        @pl.when(j == n_k - 1)
        def _():
            xv = x_ref[...]
            gain = g_ref[...]
            r = lax.rsqrt(_rowmean(xv * xv) + EPS)
            dgain, dx = _rms_bwd(acc_ref[...], gain, r, xv * r)
            dg_ref[...] += dgain
            gx_ref[...] = dxo_ref[...] + dx

        @pl.when(jnp.logical_and(i == n_t - 1, j == n_k - 1))
        def _():
            finish()

    res = pl.pallas_call(
        body,
        name="input_grad",
        grid=(n_t, n_k),
        in_specs=[
            pl.BlockSpec((t, kb), lambda i, j: (i, j)),
            pl.BlockSpec((kb, d), lambda i, j: (j, 0)),
            pl.BlockSpec((t, d), lambda i, j: (i, 0)),
            pl.BlockSpec((t, d), lambda i, j: (i, 0)),
            _full((1, d)),
        ]
        + carried["in_specs"],
        out_specs=[pl.BlockSpec((t, d), lambda i, j: (i, 0)), _full((1, d))] + carried["out_specs"],
        out_shape=[jax.ShapeDtypeStruct((s, d), F32), jax.ShapeDtypeStruct((1, d), F32)] + carried["out_shape"],
        scratch_shapes=[pltpu.VMEM((t, d), F32)] + carried["scratch_shapes"],
        compiler_params=_params("arbitrary", "arbitrary", vmem_limit_bytes=VMEM_LIMIT_MIX_BYTES),
    )(dproj, win_t, x, dxo, norm_pre, *carried["operands"])
    return res[0], res[1], res[2:]


def _kv_backward(dkv, memn, wkv, mem):
    m, d = mem.shape
    n = wkv.shape[1]

    def body(dkv_ref, memn_ref, w_ref, mem_ref, gw_ref, dg_ref):
        dkv_bf = dkv_ref[...].astype(BF16)
        gw_ref[...] = _dot(memn_ref[...], dkv_bf, TN).astype(BF16).reshape(N_DEV, WKV_BLK, n)
        dmemn = _dot(dkv_bf, w_ref[...], NT)
        mv = mem_ref[...]
        r = lax.rsqrt(_rowmean(mv * mv) + EPS)
        dg_ref[...] = _colsum(dmemn * (mv * r))

    return pl.pallas_call(
        body,
        name="kv_backward",
        grid=(1,),
        in_specs=[_full((m, n)), _full((m, d)), _full(wkv.shape), _full((m, d))],
        out_specs=[_full((N_DEV, WKV_BLK, n)), _full((1, d))],
        out_shape=[jax.ShapeDtypeStruct((N_DEV, WKV_BLK, n), BF16), jax.ShapeDtypeStruct((1, d), F32)],
        compiler_params=_params("arbitrary"),
    )(dkv, memn, wkv, mem)


def _adamw_math(w, g, m, v):
    m = ADAM_B1 * m + (1.0 - ADAM_B1) * g
    v = ADAM_B2 * v + (1.0 - ADAM_B2) * (g * g)
    m_hat = m / (1.0 - ADAM_B1**ADAM_STEP)
    v_hat = v / (1.0 - ADAM_B2**ADAM_STEP)
    delta = -ADAM_LR * (m_hat / (jnp.sqrt(v_hat) + ADAM_EPS) + ADAM_WD * w)
    return delta, m, v


def _adamw(parts, w, m, v, name):
    r, c = w.shape
    slots = parts.shape[0]
    t = r
    while t * c * 4 > TILE_ADAM_BYTES and t % 16 == 0:
        t //= 2

    def body(p_ref, w_ref, m_ref, v_ref, g_ref, d_ref, nm_ref, nv_ref):
        g = p_ref[0].astype(F32)
        for k in range(1, slots):
            g = g + p_ref[k].astype(F32)
        delta, nm, nv = _adamw_math(w_ref[...], g, m_ref[...], v_ref[...])
        g_ref[...] = g
        d_ref[...] = delta
        nm_ref[...] = nm
        nv_ref[...] = nv

    tile = pl.BlockSpec((t, c), lambda i: (i, 0))
    return pl.pallas_call(
        body,
        name=name,
        grid=(r // t,),
        in_specs=[pl.BlockSpec((slots, t, c), lambda i: (0, i, 0)), tile, tile, tile],
        out_specs=[tile] * 4,
        out_shape=[jax.ShapeDtypeStruct((r, c), F32)] * 4,
        compiler_params=_params("parallel"),
    )(parts, w, m, v)


def _adamw_packed(parts, triples, name):
    slots = parts.shape[0]
    sizes = [w.shape[0] for w, _, _ in triples]
    rest = parts.shape[1] - sum(sizes)

    def total(p_ref, at, rows):
        g = p_ref[0, at : at + rows, :]
        for k in range(1, slots):
            g = g + p_ref[k, at : at + rows, :]
        return g

    def body(p_ref, *refs):
        ins = refs[: 3 * len(triples)]
        outs = refs[3 * len(triples) :]
        at = 0
        for n, rows in enumerate(sizes):
            g = total(p_ref, at, rows)
            w_ref, m_ref, v_ref = ins[3 * n : 3 * n + 3]
            delta, nm, nv = _adamw_math(w_ref[...], g, m_ref[...], v_ref[...])
            for ref, val in zip(outs[4 * n : 4 * n + 4], (g, delta, nm, nv)):
                ref[...] = val
            at += rows
        if rest:
            outs[-1][...] = total(p_ref, at, rest)

    flat = [a for t in triples for a in t]
    res = pl.pallas_call(
        body,
        name=name,
        out_shape=[jax.ShapeDtypeStruct(w.shape, F32) for w, _, _ in triples for _ in range(4)]
        + ([jax.ShapeDtypeStruct((rest, 128), F32)] if rest else []),
        compiler_params=pltpu.CompilerParams(vmem_limit_bytes=VMEM_LIMIT_BYTES),
    )(parts, *flat)
    return [res[4 * n : 4 * n + 4] for n in range(len(triples))], (res[-1] if rest else None)


SMALL = ("norm_pre", "pool_scale", "sgu_ln_g", "sgu_ln_b", "sgu_w", "sgu_b", "mem_norm", "branch_norm", "norm_post")


def _local_view(name, w):
    if name == "sgu_w":
        return w.reshape(SGU_HEADS, SGU_CHUNK, SGU_CHUNK)
    if name == "sgu_b":
        return w.reshape(SGU_HEADS, SGU_CHUNK)
    return w.reshape(1, -1)


def _forward_backward(x, mem, target, shards, small):
    causal = jnp.tril(jnp.ones((SGU_CHUNK, SGU_CHUNK), dtype=bool))
    sgu_wm = jnp.where(causal[None], small["sgu_w"], 0.0).astype(BF16)
    sgu_bias = jnp.repeat(jnp.transpose(small["sgu_b"]), SGU_HEAD_DIM, axis=1)

    proj, h, (win, wkv, pool_all, wout) = _proj_gather(x, small["norm_pre"], shards)
    wout = wout.reshape(MIX_WIDTH, D_MODEL)
    wkv = wkv.reshape(D_MODEL, 2 * BRANCH)
    pool_full = (
        pool_all.reshape(N_DEV, len(POOL_WINDOWS), POOL_BLK, POOL_GROUP_DIM)
        .transpose(1, 0, 2, 3)
        .reshape(len(POOL_WINDOWS), POOL_GROUP_DIM, POOL_GROUP_DIM)
    )
    memn, kv = _kv_forward(mem, small["mem_norm"], wkv)
    (y, dout, dxo, dproj, loss, d_norm_post, d_branch_norm, d_pool_scale, d_ln_g, d_ln_b, d_pool_w, d_sgu_w, d_sgu_b,
     dkv) = _mix(
        proj, x, target, kv, kv.T, wout, wout.T, pool_full, jnp.swapaxes(pool_full, 1, 2), small["pool_scale"],
        small["sgu_ln_g"], small["sgu_ln_b"], sgu_bias, sgu_wm, jnp.swapaxes(sgu_wm, 1, 2), small["branch_norm"],
        small["norm_post"],
    )
    g_wkv, d_mem_norm = _kv_backward(dkv, memn, wkv, mem)
    g_pool = (
        d_pool_w.reshape(len(POOL_WINDOWS), N_DEV, POOL_BLK, POOL_GROUP_DIM)
        .transpose(1, 0, 2, 3)
        .reshape(N_DEV, len(POOL_WINDOWS) * POOL_BLK, POOL_GROUP_DIM)
        .astype(BF16)
    )
    small_grads = dict(
        pool_scale=d_pool_scale, sgu_ln_g=d_ln_g, sgu_ln_b=d_ln_b, sgu_w=d_sgu_w, sgu_b=d_sgu_b,
        mem_norm=d_mem_norm, branch_norm=d_branch_norm, norm_post=d_norm_post,
    )
    packed = jnp.concatenate(
        [small_grads[n].reshape(-1, LANES) for n in SMALL if n != "norm_pre"]
        + [jnp.broadcast_to(loss, (SUBLANES, LANES))],
        axis=0,
    )
    packed = packed[None, None]

    by_chip = lambda g: g.reshape((N_CHIPS, 2) + g.shape[1:])
    small_mine = [by_chip(g_wkv), by_chip(g_pool), packed]
    g_wout, wout_theirs, small_theirs = _weight_grad(y, dout, N_DEV, "rows", "grad_w_out", [("pair", small_mine)])
    sums, _ = _pair_sum(small_mine + [by_chip(g_wout)], list(small_theirs) + [wout_theirs], "pair_sum_first")
    g_win, win_theirs, (l_wkv, l_pool, l_packed, l_wout) = _weight_grad(
        h, dproj, N_DEV, "cols", "grad_w_in", [("chips", list(sums))]
    )
    (win_sums,), _ = _pair_sum([by_chip(g_win)], [win_theirs], "pair_sum_w_in")
    win_late, _ = lax.optimization_barrier((win, g_wkv))
    grad_x, d_norm_pre, (l_win,) = _input_grad(
        dproj, jnp.swapaxes(win_late, 1, 2).reshape(IN_WIDTH, D_MODEL), x, dxo, small["norm_pre"],
        [("chips", [win_sums])],
    )
    return grad_x, dict(w_in=l_win, w_out=l_wout, w_kv=l_wkv, pool_w=l_pool), l_packed, d_norm_pre


def kernel(x, mem, norm_pre, w_in, pool_w, pool_scale, sgu_ln_g, sgu_ln_b, sgu_w, sgu_b, mem_norm, w_kv, branch_norm, w_out, norm_post, loss_target, m_norm_pre, m_w_in, m_pool_w, m_pool_scale, m_sgu_ln_g, m_sgu_ln_b, m_sgu_w, m_sgu_b, m_mem_norm, m_w_kv, m_branch_norm, m_w_out, m_norm_post, v_norm_pre, v_w_in, v_pool_w, v_pool_scale, v_sgu_ln_g, v_sgu_ln_b, v_sgu_w, v_sgu_b, v_mem_norm, v_w_kv, v_branch_norm, v_w_out, v_norm_post):
    weights = dict(norm_pre=norm_pre, w_in=w_in, pool_w=pool_w, pool_scale=pool_scale, sgu_ln_g=sgu_ln_g, sgu_ln_b=sgu_ln_b, sgu_w=sgu_w, sgu_b=sgu_b, mem_norm=mem_norm, w_kv=w_kv, branch_norm=branch_norm, w_out=w_out, norm_post=norm_post)
    first = dict(norm_pre=m_norm_pre, w_in=m_w_in, pool_w=m_pool_w, pool_scale=m_pool_scale, sgu_ln_g=m_sgu_ln_g, sgu_ln_b=m_sgu_ln_b, sgu_w=m_sgu_w, sgu_b=m_sgu_b, mem_norm=m_mem_norm, w_kv=m_w_kv, branch_norm=m_branch_norm, w_out=m_w_out, norm_post=m_norm_post)
    second = dict(norm_pre=v_norm_pre, w_in=v_w_in, pool_w=v_pool_w, pool_scale=v_pool_scale, sgu_ln_g=v_sgu_ln_g, sgu_ln_b=v_sgu_ln_b, sgu_w=v_sgu_w, sgu_b=v_sgu_b, mem_norm=v_mem_norm, w_kv=v_w_kv, branch_norm=v_branch_norm, w_out=v_w_out, norm_post=v_norm_post)
    order = ("norm_pre", "w_in", "pool_w", "pool_scale", "sgu_ln_g", "sgu_ln_b", "sgu_w", "sgu_b", "mem_norm", "w_kv", "branch_norm", "w_out", "norm_post")

    owned_shape = dict(
        w_in=(D_MODEL, WIN_BLK), w_out=(WOUT_BLK, D_MODEL), w_kv=(WKV_BLK, 2 * BRANCH),
        pool_w=(len(POOL_WINDOWS) * POOL_BLK, POOL_GROUP_DIM),
    )
    owned = {n: weights[n].reshape(owned_shape[n]) for n in owned_shape}
    small = {n: _local_view(n, weights[n]) for n in SMALL}
    grad_x, landed, landed_packed, d_norm_pre = _forward_backward(
        x[0], mem[0], loss_target[0], [owned[n].astype(BF16) for n in ("w_in", "w_kv", "pool_w", "w_out")], small
    )
    d_norm_pre = d_norm_pre.reshape(-1, 128)
    (landed_norm_pre,) = _exchange(
        [("all", [jnp.broadcast_to(d_norm_pre[None], (N_DEV,) + d_norm_pre.shape)])], "exchange_norm_pre"
    )

    grads, deltas, new_m, new_v = {}, {}, {}, {}
    for n in owned_shape:
        shape = weights[n].shape
        res = _adamw(
            landed[n], owned[n], first[n].reshape(owned_shape[n]), second[n].reshape(owned_shape[n]), "adamw_" + n
        )
        grads[n], deltas[n], new_m[n], new_v[n] = (a.reshape(shape) for a in res)
    rows_of = lambda tree, n: tree[n].reshape(-1, 128)
    for names, parts, name in (
        ([n for n in SMALL if n != "norm_pre"], landed_packed, "adamw_replicated"),
        (["norm_pre"], landed_norm_pre, "adamw_norm_pre"),
    ):
        res, rest = _adamw_packed(
            parts, [(rows_of(weights, n), rows_of(first, n), rows_of(second, n)) for n in names], name
        )
        if rest is not None:
            total = rest[0, 0]
        for n, four in zip(names, res):
            for tree, a in zip((grads, deltas, new_m, new_v), four):
                tree[n] = a.reshape(weights[n].shape)

    return (
        total,
        grad_x[None],
        *[grads[n] for n in order],
        *[deltas[n] for n in order],
        *[new_m[n] for n in order],
        *[new_v[n] for n in order],
    )
```

```python
import jax
import jax.numpy as jnp
from jax import lax
from jax.experimental import pallas as pl
from jax.experimental.pallas import tpu as pltpu

F32 = jnp.float32
BF16 = jnp.bfloat16
EPS = 1e-6

D_MODEL = 2048
POOL_WINDOWS = (2, 4, 8, 16)
POOL_GROUP_DIM = 256
BRANCH = 1024
SGU_CHUNK = 128
SGU_HEADS = 8
SGU_HEAD_DIM = 128
XATTN_HEADS = 4
XATTN_HEAD_DIM = 256
MIX_WIDTH = 3 * BRANCH
IN_WIDTH = 7 * BRANCH
N_DEV = 8
WIN_BLK = IN_WIDTH // N_DEV
WOUT_BLK = MIX_WIDTH // N_DEV
WKV_BLK = D_MODEL // N_DEV
POOL_BLK = POOL_GROUP_DIM // N_DEV
HALO = 16
LANES = 128
SUBLANES = 8

ADAM_LR = 0.001
ADAM_B1 = 0.9
ADAM_B2 = 0.999
ADAM_EPS = 1e-08
ADAM_WD = 0.01
ADAM_STEP = 10

VMEM_LIMIT_BYTES = 56 * 1024 * 1024
VMEM_LIMIT_MIX_BYTES = 63 * 1024 * 1024

TILE_PROJ = 512
TILE_MIX = 128
TILE_GRAD = 512
TILE_WEIGHT_GRAD = 1024
TILE_ADAM_BYTES = 1 << 20

ANY = pl.BlockSpec(memory_space=pl.ANY)
NN = (((1,), (0,)), ((), ()))
NT = (((1,), (1,)), ((), ()))
TN = (((0,), (0,)), ((), ()))
MESH = pl.DeviceIdType.MESH


def _dot(a, b, dims=NN):
    return lax.dot_general(a, b, dims, preferred_element_type=F32)


def _params(*semantics, vmem_limit_bytes=VMEM_LIMIT_BYTES):
    return pltpu.CompilerParams(dimension_semantics=semantics, vmem_limit_bytes=vmem_limit_bytes)


def _rowmean(a):
    return jnp.mean(a, axis=-1, keepdims=True)


def _colsum(a):
    return jnp.sum(a, axis=0, keepdims=True)


def _full(shape):
    zeros = (0,) * len(shape)
    return pl.BlockSpec(shape, lambda *_: zeros)


def _resident(shape):
    zeros = (0,) * len(shape)
    return pl.BlockSpec(shape, lambda *_: zeros, pipeline_mode=pl.Buffered(1))


def _kv_forward(mem, mem_norm, wkv):
    m, d = mem.shape
    n = wkv.shape[1]
    cols = 4 * LANES

    def body(mem_ref, g_ref, w_ref, memn_ref, kv_ref, kvt_ref):
        mv = mem_ref[...]
        r = lax.rsqrt(_rowmean(mv * mv) + EPS)
        memn = (mv * r * g_ref[...]).astype(BF16)
        memn_ref[...] = memn
        kv = _dot(memn, w_ref[...])
        kv_ref[...] = kv.astype(BF16)
        kvt_ref[...] = kv.T.astype(BF16)

    return pl.pallas_call(
        body,
        name="kv_forward",
        grid=(n // cols,),
        in_specs=[_full((m, d)), _full((1, d)), pl.BlockSpec((d, cols), lambda j: (0, j))],
        out_specs=[_full((m, d)), pl.BlockSpec((m, cols), lambda j: (0, j)), pl.BlockSpec((cols, m), lambda j: (j, 0))],
        out_shape=[
            jax.ShapeDtypeStruct((m, d), BF16), jax.ShapeDtypeStruct((m, n), BF16), jax.ShapeDtypeStruct((n, m), BF16)
        ],
        compiler_params=_params("arbitrary"),
    )(mem, mem_norm, wkv)


def _proj_gather(x_in, norm_pre, shards):
    s, d = x_in.shape
    t = min(TILE_PROJ, s)
    n_t = s // t
    n_arr = len(shards)

    def places(x, y, c):
        return (x, y, c), (x, y, 1 - c), (x ^ c, y ^ (1 - c)), (x ^ (1 - c), y ^ c), (1 - x, 1 - y)

    def index(chip, core):
        return 4 * chip[0] + 2 * chip[1] + core

    _, _, chip_a, chip_b, chip_d = places(*_position())
    c_out = lax.axis_index("c")
    me_out = index((lax.axis_index("x"), lax.axis_index("y")), c_out)
    order = jnp.stack(
        [
            me_out, me_out ^ 1, index(chip_a, c_out), index(chip_b, c_out), index(chip_b, 1 - c_out),
            index(chip_a, 1 - c_out), index(chip_d, c_out), index(chip_d, 1 - c_out),
        ]
    ).astype(jnp.int32)

    def body(order_ref, x_ref, g_ref, *refs):
        del order_ref
        src = refs[:n_arr]
        proj_ref, h_ref = refs[n_arr : n_arr + 2]
        out = refs[n_arr + 2 : 2 * n_arr + 2]
        wbuf, hs, send_sems, recv_sems, local_sems, load_sems = refs[2 * n_arr + 2 :]
        j = pl.program_id(0)
        i = pl.program_id(1)
        me, sibling, chip_a, chip_b, chip_d = places(*_position())
        c = me[2]

        def block(a, chip, core):
            return out[a].at[index(chip, core)]

        def copy(a, k, owner, to, from_input=False):
            return pltpu.make_async_remote_copy(
                src_ref=src[a] if from_input else block(a, *owner),
                dst_ref=block(a, *owner),
                send_sem=send_sems.at[a, k],
                recv_sem=recv_sems.at[a, k],
                device_id=to,
                device_id_type=MESH,
            )

        mine = (me[:2], c)

        def own(a):
            return pltpu.make_async_copy(src[a], block(a, *mine), local_sems.at[a])

        def first_sends(a):
            return [
                copy(a, 0, mine, sibling, from_input=True),
                copy(a, 1, mine, (*chip_a, c), from_input=True),
                copy(a, 2, mine, (*chip_b, c), from_input=True),
            ]

        def onward(a, k):
            owner = {3: chip_a, 4: chip_a, 5: chip_b, 6: chip_d}[k]
            return copy(a, k, (owner, c), (*chip_b, c) if k == 3 else sibling)

        def landed(a, k):
            owner = {0: mine[0], 1: chip_a, 2: chip_b, 3: chip_d, 4: chip_b, 5: chip_a, 6: chip_d}[k]
            core = c if k in (1, 2, 3) else 1 - c
            copy(a, k, (owner, core), me).wait_recv()
            return owner, core

        def load(ref, step):
            return pltpu.make_async_copy(ref, wbuf.at[step % 2], load_sems.at[step % 2])

        @pl.when(jnp.logical_and(j == 0, i == 0))
        def _():
            own(0).start()
            for cp in first_sends(0):
                cp.start()
            load(src[0], 0).start()
            load(src[0], 0).wait()

        steps = {1: (0, ()), 2: (1, (3, 4)), 3: (2, (5,)), 4: (4, ()), 5: (5, ()), 6: (3, (6,)), 7: (6, ())}
        for step, (k, then) in steps.items():

            @pl.when(jnp.logical_and(j == step, i == 0))
            def _():
                load(src[0], step).wait()

            @pl.when(jnp.logical_and(j == step - 1, i == n_t - 1))
            def _():
                owner = landed(0, k)
                for k2 in then:
                    onward(0, k2).start()
                if k == 1:
                    for a in range(1, n_arr):
                        own(a).start()
                        for cp in first_sends(a):
                            cp.start()
                if k == 3:
                    for a in range(1, n_arr):
                        for k1, then1 in ((1, (3, 4)), (2, (5,))):
                            landed(a, k1)
                            for k2 in then1:
                                onward(a, k2).start()
                load(block(0, *owner), step).start()

        @pl.when(j == 0)
        def _():
            xv = x_ref[...]
            h = (xv * lax.rsqrt(_rowmean(xv * xv) + EPS) * g_ref[...]).astype(BF16)
            hs[i] = h
            h_ref[...] = h

        proj_ref[...] = _dot(hs[i], wbuf[j % 2]).astype(BF16)

        @pl.when(jnp.logical_and(j == N_DEV - 1, i == n_t - 1))
        def _():
            for a in range(1, n_arr):
                landed(a, 3)
                onward(a, 6).start()
            for a in range(1, n_arr):
                for k in (0, 4, 5, 6):
                    landed(a, k)
            for a in range(n_arr):
                for cp in first_sends(a) + [onward(a, k) for k in (3, 4, 5, 6)]:
                    cp.wait_send()
                own(a).wait()

    res = pl.pallas_call(
        body,
        name="proj_gather",
        grid_spec=pltpu.PrefetchScalarGridSpec(
            num_scalar_prefetch=1,
            grid=(N_DEV, n_t),
            in_specs=[
                pl.BlockSpec((t, d), lambda j, i, order_ref: (jnp.where(j == 0, i, n_t - 1), 0)),
                pl.BlockSpec((1, d), lambda j, i, order_ref: (0, 0)),
            ]
            + [ANY] * n_arr,
            out_specs=[
                pl.BlockSpec((t, WIN_BLK), lambda j, i, order_ref: (i, order_ref[j])),
                pl.BlockSpec((t, d), lambda j, i, order_ref: (jnp.where(j == 0, i, n_t - 1), 0)),
            ]
            + [ANY] * n_arr,
            scratch_shapes=[
                pltpu.VMEM((2,) + shards[0].shape, BF16),
                pltpu.VMEM((n_t, t, d), BF16),
                pltpu.SemaphoreType.DMA((n_arr, 7)),
                pltpu.SemaphoreType.DMA((n_arr, 7)),
                pltpu.SemaphoreType.DMA((n_arr,)),
                pltpu.SemaphoreType.DMA((2,)),
            ],
        ),
        out_shape=[jax.ShapeDtypeStruct((s, IN_WIDTH), BF16), jax.ShapeDtypeStruct((s, d), BF16)]
        + [jax.ShapeDtypeStruct((N_DEV,) + a.shape, a.dtype) for a in shards],
        compiler_params=_params("arbitrary", "arbitrary"),
    )(order, x_in, norm_pre, *shards)
    return res[0], res[1], res[2:]


def _sigmoid(a):
    return jax.nn.sigmoid(a)


def _dsilu(a, sg):
    return sg * (1.0 + a * (1.0 - sg))


def _rms_fwd(u, gain):
    r = lax.rsqrt(_rowmean(u * u) + EPS)
    n = u * r
    return r, n, n * gain


def _rms_bwd(dy, gain, r, n):
    dn = dy * gain
    return _colsum(dy * n), r * (dn - n * _rowmean(dn * n))


def _mix(proj, x, target, kv, kv_t, wout, wout_t, pool_w, pool_w_t, pool_scale, ln_g, ln_b, sgu_bias, sgu_wm, sgu_wm_t, branch_norm, norm_post):
    s, d = x.shape
    t = min(TILE_MIX, s)
    n_tiles = s // t
    n_chunks = t // SGU_CHUNK
    halo_blocks_per_tile = t // HALO
    inv_d = 1.0 / d
    scale = 1.0 / (XATTN_HEAD_DIM**0.5)

    def body(
        proj_ref, halo_ref, x_ref, tgt_ref, kv_ref, kvt_ref, wout_hbm, wout_t_hbm, pw_ref, pwt_ref, pscale_ref, lng_ref,
        lnb_ref, bias_ref, wm_ref, wmt_ref, bnorm_ref, gpost_ref,
        y_ref, dout_ref, dxo_ref, dproj_ref, loss_ref, dgpost_ref, dbnorm_ref, dpscale_ref, dlng_ref, dlnb_ref,
        dpw_out, dwm_out, dbias_ref, dkv_out,
        carry_ref, dzsum_ref, dpw_ref, dwm_ref, dkv_ref, wout_ref, wout_t_ref, wout_sems,
    ):
        i = pl.program_id(0)
        tile = n_tiles - 1 - i
        wout_load = pltpu.make_async_copy(wout_hbm, wout_ref, wout_sems.at[0])
        wout_t_load = pltpu.make_async_copy(wout_t_hbm, wout_t_ref, wout_sems.at[1])

        @pl.when(i == 0)
        def _():
            wout_load.start()
            wout_t_load.start()
            carry_ref[...] = jnp.zeros_like(carry_ref)
            dzsum_ref[...] = jnp.zeros_like(dzsum_ref)
            for ref in (loss_ref, dgpost_ref, dbnorm_ref, dpscale_ref, dlng_ref, dlnb_ref, dpw_ref, dwm_ref, dkv_ref):
                ref[...] = jnp.zeros_like(ref)

        t_glob = tile * t + lax.broadcasted_iota(jnp.int32, (t, 1), 0)
        inv_cnt = [1.0 / jnp.minimum(t_glob + 1, w).astype(F32) for w in POOL_WINDOWS]

        xa = proj_ref[:, 0:BRANCH].astype(F32)
        ga = proj_ref[:, BRANCH : 2 * BRANCH].astype(F32)
        halo = jnp.where(tile == 0, 0.0, halo_ref[...].astype(F32))
        d_bf, pm_parts = [], []
        for g, w in enumerate(POOL_WINDOWS):
            cols = slice(g * POOL_GROUP_DIM, (g + 1) * POOL_GROUP_DIM)
            acc = jnp.concatenate([halo[:, cols], xa[:, cols]], axis=0)
            k = 1
            while k < w:
                acc = acc + pltpu.roll(acc, k, axis=0)
                k *= 2
            dg = (acc[HALO:, :] * inv_cnt[g] - xa[:, cols]).astype(BF16)
            d_bf.append(dg)
            pm_parts.append(_dot(dg, pw_ref[g]))
        pm = jnp.concatenate(pm_parts, axis=1)
        pscale = pscale_ref[...]
        pa = pm * pscale
        sga = _sigmoid(ga)
        sila = ga * sga
        ua = pa * sila
        g_a = bnorm_ref[:, 0:BRANCH]
        ra, na, ya = _rms_fwd(ua, g_a)

        u = proj_ref[:, 2 * BRANCH : 3 * BRANCH].astype(F32)
        v = proj_ref[:, 3 * BRANCH : 4 * BRANCH].astype(F32)
        gb = proj_ref[:, 4 * BRANCH : 5 * BRANCH].astype(F32)
        lng = lng_ref[...]
        vc = v - _rowmean(v)
        rstd = lax.rsqrt(_rowmean(vc * vc) + EPS)
        vhat = vc * rstd
        vn_bf = (vhat * lng + lnb_ref[...]).astype(BF16)
        z_rows = []
        for c in range(n_chunks):
            rows = slice(c * SGU_CHUNK, (c + 1) * SGU_CHUNK)
            z_rows.append(
                jnp.concatenate(
                    [
                        _dot(wm_ref[hd], vn_bf[rows, hd * SGU_HEAD_DIM : (hd + 1) * SGU_HEAD_DIM])
                        for hd in range(SGU_HEADS)
                    ],
                    axis=1,
                )
                + bias_ref[...]
            )
        z = z_rows[0] if n_chunks == 1 else jnp.concatenate(z_rows, axis=0)
        sb = u * z
        sgb = _sigmoid(gb)
        silb = gb * sgb
        ub = sb * silb
        g_b = bnorm_ref[:, BRANCH : 2 * BRANCH]
        rb, nb, yb = _rms_fwd(ub, g_b)

        q = proj_ref[:, 5 * BRANCH : 6 * BRANCH]
        gc = proj_ref[:, 6 * BRANCH : 7 * BRANCH].astype(F32)
        q_bf, p_bf, o_parts = [], [], []
        for hd in range(XATTN_HEADS):
            cols = slice(hd * XATTN_HEAD_DIM, (hd + 1) * XATTN_HEAD_DIM)
            qh = q[:, cols]
            sc = _dot(qh, kvt_ref[cols, :]) * scale
            e = jnp.exp(sc - jnp.max(sc, axis=-1, keepdims=True))
            p = e / jnp.sum(e, axis=-1, keepdims=True)
            q_bf.append(qh)
            p_bf.append(p.astype(BF16))
            o_parts.append(_dot(p_bf[hd], kv_ref[:, BRANCH + hd * XATTN_HEAD_DIM : BRANCH + (hd + 1) * XATTN_HEAD_DIM]))
        o = jnp.concatenate(o_parts, axis=1)
        sgc = _sigmoid(gc)
        silc = gc * sgc
        uc = o * silc
        g_c = bnorm_ref[:, 2 * BRANCH : 3 * BRANCH]
        rc, nc, yc = _rms_fwd(uc, g_c)

        @pl.when(i == 0)
        def _():
            wout_load.wait()

        out = None
        for b, y_branch in enumerate((ya, yb, yc)):
            rows = slice(b * BRANCH, (b + 1) * BRANCH)
            y_bf = y_branch.astype(BF16)
            y_ref[:, rows] = y_bf
            part = _dot(y_bf, wout_ref[rows, :])
            out = part if out is None else out + part
        gpost = gpost_ref[...]
        r_out = lax.rsqrt(_rowmean(out * out) + EPS)
        on = out * r_out
        err = x_ref[...] + on * gpost - tgt_ref[...]
        loss_ref[...] += 0.5 * jnp.sum(_rowmean(err * err), axis=0, keepdims=True)

        dxo = err * inv_d
        dxo_ref[...] = dxo
        dgp, dout = _rms_bwd(dxo, gpost, r_out, on)
        dgpost_ref[...] += dgp
        dout_bf = dout.astype(BF16)
        dout_ref[...] = dout_bf

        @pl.when(i == 0)
        def _():
            wout_t_load.wait()

        dy = [_dot(dout_bf, wout_t_ref[:, b * BRANCH : (b + 1) * BRANCH]) for b in range(3)]

        dg_a, dua = _rms_bwd(dy[0], g_a, ra, na)
        dg_b, dub = _rms_bwd(dy[1], g_b, rb, nb)
        dg_c, duc = _rms_bwd(dy[2], g_c, rc, nc)
        dbnorm_ref[...] += jnp.concatenate([dg_a, dg_b, dg_c], axis=1)

        dpa = dua * sila
        dga = dua * pa * _dsilu(ga, sga)
        dpscale_ref[...] += _colsum(dpa * pm)
        dpm = dpa * pscale
        dxa_parts, carry_parts = [], []
        for g, w in enumerate(POOL_WINDOWS):
            cols = slice(g * POOL_GROUP_DIM, (g + 1) * POOL_GROUP_DIM)
            dpm_g = dpm[:, cols].astype(BF16)
            dd = _dot(dpm_g, pwt_ref[g])
            dpw_ref[g] += _dot(d_bf[g], dpm_g, TN)
            cg = dd * inv_cnt[g]
            carry_parts.append(cg[0:HALO, :])
            acc = jnp.concatenate([cg, carry_ref[:, cols]], axis=0)
            k = 1
            while k < w:
                acc = acc + pltpu.roll(acc, t + HALO - k, axis=0)
                k *= 2
            dxa_parts.append(acc[0:t, :] - dd)
        carry_ref[...] = jnp.concatenate(carry_parts, axis=1)
        dxa = jnp.concatenate(dxa_parts, axis=1)

        dsb = dub * silb
        dgb = dub * sb * _dsilu(gb, sgb)
        du = dsb * z
        dz = dsb * u
        dz_bf = dz.astype(BF16)
        dvn_rows = []
        dz_sum = None
        for c in range(n_chunks):
            rows = slice(c * SGU_CHUNK, (c + 1) * SGU_CHUNK)
            dz_sum = dz[rows, :] if dz_sum is None else dz_sum + dz[rows, :]
            parts = []
            for hd in range(SGU_HEADS):
                cols = slice(hd * SGU_HEAD_DIM, (hd + 1) * SGU_HEAD_DIM)
                parts.append(_dot(wmt_ref[hd], dz_bf[rows, cols]))
                dwm_ref[hd] += _dot(dz_bf[rows, cols], vn_bf[rows, cols], NT)
            dvn_rows.append(jnp.concatenate(parts, axis=1))
        dzsum_ref[...] += dz_sum
        dvn = dvn_rows[0] if n_chunks == 1 else jnp.concatenate(dvn_rows, axis=0)
        dlng_ref[...] += _colsum(dvn * vhat)
        dlnb_ref[...] += _colsum(dvn)
        dvh = dvn * lng
        dv = rstd * (dvh - _rowmean(dvh) - vhat * _rowmean(dvh * vhat))

        do = duc * silc
        dgc = duc * o * _dsilu(gc, sgc)
        dq_parts = []
        for hd in range(XATTN_HEADS):
            cols = slice(hd * XATTN_HEAD_DIM, (hd + 1) * XATTN_HEAD_DIM)
            vcols = slice(BRANCH + hd * XATTN_HEAD_DIM, BRANCH + (hd + 1) * XATTN_HEAD_DIM)
            do_h = do[:, cols].astype(BF16)
            p = p_bf[hd].astype(F32)
            dp = _dot(do_h, kvt_ref[vcols, :])
            dkv_ref[:, vcols] += _dot(p_bf[hd], do_h, TN)
            ds_bf = (p * (dp - jnp.sum(dp * p, axis=-1, keepdims=True)) * scale).astype(BF16)
            dq_parts.append(_dot(ds_bf, kv_ref[:, cols]))
            dkv_ref[:, cols] += _dot(ds_bf, q_bf[hd], TN)
        dq = jnp.concatenate(dq_parts, axis=1)

        dproj_ref[...] = jnp.concatenate([dxa, dga, du, dv, dgb, dq, dgc], axis=1).astype(BF16)

        @pl.when(i == n_tiles - 1)
        def _():
            keep = lax.broadcasted_iota(jnp.int32, (SGU_CHUNK, SGU_CHUNK), 0) >= lax.broadcasted_iota(
                jnp.int32, (SGU_CHUNK, SGU_CHUNK), 1
            )
            for hd in range(SGU_HEADS):
                dwm_ref[hd] = jnp.where(keep, dwm_ref[hd], 0.0)
                per_pos = dzsum_ref[:, hd * SGU_HEAD_DIM : (hd + 1) * SGU_HEAD_DIM]
                dbias_ref[hd : hd + 1, :] = _colsum(per_pos.T)
            for acc, res in ((dpw_ref, dpw_out), (dwm_ref, dwm_out), (dkv_ref, dkv_out)):
                pltpu.sync_copy(acc, res)

    row_tile = lambda width: pl.BlockSpec((t, width), lambda i: (n_tiles - 1 - i, 0))
    halo_spec = pl.BlockSpec(
        (HALO, BRANCH), lambda i: (jnp.maximum((n_tiles - 1 - i) * halo_blocks_per_tile - 1, 0), 0)
    )
    acc_shapes = [
        (1, 128),
        (1, d),
        (1, MIX_WIDTH),
        (1, BRANCH),
        (1, BRANCH),
        (1, BRANCH),
        pool_w.shape,
        sgu_wm.shape,
        (SGU_HEADS, SGU_CHUNK),
        kv.shape,
    ]
    return pl.pallas_call(
        body,
        name="mix",
        grid=(n_tiles,),
        in_specs=[
            row_tile(IN_WIDTH), halo_spec, row_tile(d), row_tile(d), _resident(kv.shape), _resident(kv_t.shape),
            ANY, ANY, _resident(pool_w.shape), _resident(pool_w_t.shape),
            _full((1, BRANCH)), _full((1, BRANCH)), _full((1, BRANCH)), _resident((SGU_CHUNK, BRANCH)),
            _resident(sgu_wm.shape), _resident(sgu_wm_t.shape), _full((1, MIX_WIDTH)), _full((1, d)),
        ],
        out_specs=[row_tile(MIX_WIDTH), row_tile(d), row_tile(d), row_tile(IN_WIDTH)]
        + [ANY if len(a) == 3 or a == kv.shape else _full(a) for a in acc_shapes],
        out_shape=[
            jax.ShapeDtypeStruct((s, MIX_WIDTH), BF16),
            jax.ShapeDtypeStruct((s, d), BF16),
            jax.ShapeDtypeStruct((s, d), F32),
            jax.ShapeDtypeStruct((s, IN_WIDTH), BF16),
        ]
        + [jax.ShapeDtypeStruct(a, F32) for a in acc_shapes],
        scratch_shapes=[
            pltpu.VMEM((HALO, BRANCH), F32), pltpu.VMEM((SGU_CHUNK, BRANCH), F32), pltpu.VMEM(pool_w.shape, F32),
            pltpu.VMEM(sgu_wm.shape, F32), pltpu.VMEM(kv.shape, F32), pltpu.VMEM(wout.shape, BF16),
            pltpu.VMEM(wout_t.shape, BF16), pltpu.SemaphoreType.DMA((2,)),
        ],
        compiler_params=_params("arbitrary", vmem_limit_bytes=VMEM_LIMIT_MIX_BYTES),
    )(
        proj, proj, x, target, kv, kv_t, wout, wout_t, pool_w, pool_w_t, pool_scale, ln_g, ln_b, sgu_bias, sgu_wm,
        sgu_wm_t, branch_norm, norm_post,
    )


def _position():
    return lax.axis_index("x"), lax.axis_index("y"), lax.axis_index("c")


N_CHIPS = 4


def _landing_shape(kind, a):
    return (N_CHIPS,) + a.shape[2:] if kind == "pair" else a.shape


def _carry_specs(groups):
    arrays = [(kind, a) for kind, arrs in groups for a in arrs]
    scratch = []
    for _, arrs in groups:
        n = len(arrs)
        scratch += [pltpu.SemaphoreType.DMA((n, N_DEV)), pltpu.SemaphoreType.DMA((n, N_DEV)), pltpu.SemaphoreType.DMA((n,))]
    return dict(
        n=len(arrays),
        operands=[a for _, a in arrays],
        in_specs=[ANY] * len(arrays),
        out_specs=[ANY] * len(arrays),
        out_shape=[jax.ShapeDtypeStruct(_landing_shape(kind, a), a.dtype) for kind, a in arrays],
        scratch_shapes=scratch,
    )


def _carry(groups, src, out, sems):
    x, y, c = _position()
    chip = 2 * x + y

    def remote(s, d, send_sems, recv_sems, a, m, to):
        return pltpu.make_async_remote_copy(
            src_ref=s, dst_ref=d, send_sem=send_sems.at[a, m], recv_sem=recv_sems.at[a, m], device_id=to,
            device_id_type=MESH,
        )

    def copies():
        far, near = [], []
        at = 0
        for g, (kind, arrs) in enumerate(groups):
            send_sems, recv_sems, local_sems = sems[3 * g : 3 * g + 3]
            for a in range(len(arrs)):
                s, d = src[at + a], out[at + a]
                if kind == "pair" and s.shape[0] == 1:
                    for b in range(N_CHIPS):
                        far.append(remote(s.at[0, 0], d.at[b], send_sems, recv_sems, a, 1 + b, (x, y, 1 - c)))
                elif kind == "pair":
                    far.append(remote(s.at[:, 1 - c], d, send_sems, recv_sems, a, 1, (x, y, 1 - c)))
                else:
                    assert kind == "chips", kind
                    for m in range(1, N_CHIPS):
                        px, py = x ^ (m >> 1), y ^ (m & 1)
                        far.append(remote(s.at[2 * px + py], d.at[chip], send_sems, recv_sems, a, m, (px, py, c)))
                    near.append(pltpu.make_async_copy(s.at[chip], d.at[chip], local_sems.at[a]))
            at += len(arrs)
        return far, near

    def start():
        far, near = copies()
        for cp in near + far:
            cp.start()

    def finish():
        far, near = copies()
        for cp in far:
            cp.wait_recv()
        for cp in far:
            cp.wait_send()
        for cp in near:
            cp.wait()

    return start, finish


def _pair_sum(mine, theirs, name, groups=()):
    n = len(mine)
    carried = _carry_specs(groups)
    n_c = carried["n"]
    core = lax.axis_index("c").astype(jnp.int32).reshape(1)

    def body(core_ref, *refs):
        del core_ref
        own = refs[:n]
        sib = refs[n : 2 * n]
        src = refs[2 * n : 2 * n + n_c]
        out = refs[2 * n + n_c : 3 * n + n_c]
        landed = refs[3 * n + n_c : 3 * n + 2 * n_c]
        start, finish = _carry(groups, src, landed, refs[3 * n + 2 * n_c :])
        b = pl.program_id(0)

        @pl.when(b == 0)
        def _():
            start()

        for a in range(n):
            out[a][...] = (own[a][...].astype(F32) + sib[a][...].astype(F32)).astype(out[a].dtype)

        @pl.when(b == N_CHIPS - 1)
        def _():
            finish()

    block = lambda a: pl.BlockSpec((None,) + a.shape[1:], lambda b, core_ref: (b, 0, 0))
    res = pl.pallas_call(
        body,
        name=name,
        grid_spec=pltpu.PrefetchScalarGridSpec(
            num_scalar_prefetch=1,
            grid=(N_CHIPS,),
            in_specs=[
                pl.BlockSpec((None, None) + a.shape[2:], lambda b, core_ref: (0, 0, 0, 0))
                if a.shape[0] == 1
                else pl.BlockSpec((None, None) + a.shape[2:], lambda b, core_ref: (b, core_ref[0], 0, 0))
                for a in mine
            ]
            + [block(a) for a in theirs]
            + carried["in_specs"],
            out_specs=[block(a) for a in theirs] + carried["out_specs"],
            scratch_shapes=carried["scratch_shapes"],
        ),
        out_shape=[jax.ShapeDtypeStruct(a.shape, a.dtype) for a in theirs] + carried["out_shape"],
        compiler_params=_params("arbitrary"),
    )(core, *mine, *theirs, *carried["operands"])
    return res[:n], res[n:]


def _weight_grad(a, b, n_blk, blocked, name, groups):
    s = a.shape[0]
    t = min(TILE_WEIGHT_GRAD, s)
    n_t = s // t
    n_pairs = n_blk // 2
    if blocked == "cols":
        k, c = a.shape[1], b.shape[1] // n_blk
        a_spec = pl.BlockSpec((t, k), lambda j, i: (i, 0))
        b_spec = pl.BlockSpec((t, 2 * c), lambda j, i: (i, j))
        acc_shape = (k, 2 * c)
    else:
        k, c = a.shape[1] // n_blk, b.shape[1]
        a_spec = pl.BlockSpec((t, 2 * k), lambda j, i: (i, j))
        b_spec = pl.BlockSpec((t, c), lambda j, i: (i, 0))
        acc_shape = (2 * k, c)
    carried = _carry_specs(groups)
    n_p = carried["n"]

    def body(a_ref, b_ref, *refs):
        src = refs[:n_p]
        o_ref, theirs_ref = refs[n_p : n_p + 2]
        landed = refs[n_p + 2 : 2 * n_p + 2]
        acc_ref, sbuf, pair_send, pair_recv = refs[2 * n_p + 2 : 2 * n_p + 6]
        start, finish = _carry(groups, src, landed, refs[2 * n_p + 6 :])
        j = pl.program_id(0)
        i = pl.program_id(1)
        x, y, c_me = _position()

        def to_sibling(pair):
            return pltpu.make_async_remote_copy(
                src_ref=sbuf.at[1 - c_me], dst_ref=theirs_ref.at[pair], send_sem=pair_send.at[pair],
                recv_sem=pair_recv.at[pair], device_id=(x, y, 1 - c_me), device_id_type=MESH,
            )

        @pl.when(jnp.logical_and(j == 0, i == 0))
        def _():
            start()

        @pl.when(i == 0)
        def _():
            acc_ref[...] = jnp.zeros_like(acc_ref)

        acc_ref[...] += _dot(a_ref[...], b_ref[...], TN)

        @pl.when(i == n_t - 1)
        def _():
            for pair in range(1, n_pairs):

                @pl.when(j == pair)
                def _():
                    to_sibling(pair - 1).wait_send()

            for half in range(2):
                if blocked == "cols":
                    block = acc_ref[:, half * c : (half + 1) * c].astype(BF16)
                else:
                    block = acc_ref[half * k : (half + 1) * k, :].astype(BF16)
                o_ref[half] = block
                sbuf[half] = block
            for pair in range(n_pairs):

                @pl.when(j == pair)
                def _():
                    to_sibling(pair).start()

        @pl.when(jnp.logical_and(j == n_pairs - 1, i == n_t - 1))
        def _():
            to_sibling(n_pairs - 1).wait_send()
            for pair in range(n_pairs):
                to_sibling(pair).wait_recv()
            finish()

    res = pl.pallas_call(
        body,
        name=name,
        grid=(n_pairs, n_t),
        in_specs=[a_spec, b_spec] + carried["in_specs"],
        out_specs=[pl.BlockSpec((2, k, c), lambda j, i: (j, 0, 0)), ANY] + carried["out_specs"],
        out_shape=[jax.ShapeDtypeStruct((n_blk, k, c), BF16), jax.ShapeDtypeStruct((n_pairs, k, c), BF16)]
        + carried["out_shape"],
        scratch_shapes=[
            pltpu.VMEM(acc_shape, F32), pltpu.VMEM((2, k, c), BF16), pltpu.SemaphoreType.DMA((n_pairs,)),
            pltpu.SemaphoreType.DMA((n_pairs,)),
        ]
        + carried["scratch_shapes"],
        compiler_params=_params("arbitrary", "arbitrary", vmem_limit_bytes=VMEM_LIMIT_MIX_BYTES),
    )(a, b, *carried["operands"])
    return res[0], res[1], res[2:]


def _input_grad(dproj, win_t, x, dxo, norm_pre, groups):
    s, d = x.shape
    t = min(TILE_GRAD, s)
    n_t = s // t
    kb = 2 * WIN_BLK
    n_k = win_t.shape[0] // kb
    carried = _carry_specs(groups)
    n_p = carried["n"]

    def body(dp_ref, w_ref, x_ref, dxo_ref, g_ref, *refs):
        src = refs[:n_p]
        gx_ref, dg_all = refs[n_p : n_p + 2]
        landed = refs[n_p + 2 : 2 * n_p + 2]
        acc_ref, dg_ref, dg_send, dg_recv, dg_local = refs[2 * n_p + 2 : 2 * n_p + 7]
        start, finish = _carry(groups, src, landed, refs[2 * n_p + 7 :])
        i = pl.program_id(0)
        j = pl.program_id(1)
        px, py, pc = _position()
        me = 4 * px + 2 * py + pc

        def dg_copies():
            far = [
                pltpu.make_async_remote_copy(
                    src_ref=dg_ref, dst_ref=dg_all.at[me], send_sem=dg_send.at[m], recv_sem=dg_recv.at[m],
                    device_id=(px ^ ((m >> 2) & 1), py ^ ((m >> 1) & 1), pc ^ (m & 1)), device_id_type=MESH,
                )
                for m in range(1, N_DEV)
            ]
            return far, pltpu.make_async_copy(dg_ref, dg_all.at[me], dg_local)

        @pl.when(jnp.logical_and(i == 0, j == 0))
        def _():
            start()
            dg_ref[...] = jnp.zeros_like(dg_ref)

        @pl.when(j == 0)
        def _():
            acc_ref[...] = jnp.zeros_like(acc_ref)

        acc_ref[...] += _dot(dp_ref[...], w_ref[...])

        @pl.when(j == n_k - 1)
        def _():
            xv = x_ref[...]
            gain = g_ref[...]
            r = lax.rsqrt(_rowmean(xv * xv) + EPS)
            dgain, dx = _rms_bwd(acc_ref[...], gain, r, xv * r)
            dg_ref[...] += dgain
            gx_ref[...] = dxo_ref[...] + dx

        @pl.when(jnp.logical_and(i == n_t - 1, j == n_k - 1))
        def _():
            far, near = dg_copies()
            for cp in [near] + far:
                cp.start()
            finish()
            for cp in far:
                cp.wait_recv()
            for cp in far:
                cp.wait_send()
            near.wait()

    res = pl.pallas_call(
        body,
        name="input_grad",
        grid=(n_t, n_k),
        in_specs=[
            pl.BlockSpec((t, kb), lambda i, j: (i, j)),
            pl.BlockSpec((kb, d), lambda i, j: (j, 0)),
            pl.BlockSpec((t, d), lambda i, j: (i, 0)),
            pl.BlockSpec((t, d), lambda i, j: (i, 0)),
            _full((1, d)),
        ]
        + carried["in_specs"],
        out_specs=[pl.BlockSpec((t, d), lambda i, j: (i, 0)), ANY] + carried["out_specs"],
        out_shape=[jax.ShapeDtypeStruct((s, d), F32), jax.ShapeDtypeStruct((N_DEV, 1, d), F32)] + carried["out_shape"],
        scratch_shapes=[
            pltpu.VMEM((t, d), F32), pltpu.VMEM((1, d), F32), pltpu.SemaphoreType.DMA((N_DEV,)),
            pltpu.SemaphoreType.DMA((N_DEV,)), pltpu.SemaphoreType.DMA,
        ]
        + carried["scratch_shapes"],
        compiler_params=_params("arbitrary", "arbitrary", vmem_limit_bytes=VMEM_LIMIT_MIX_BYTES),
    )(dproj, win_t, x, dxo, norm_pre, *carried["operands"])
    return res[0], res[1], res[2:]


def _kv_backward(dkv, memn, wkv, mem):
    m, d = mem.shape
    n = wkv.shape[1]

    def body(dkv_ref, memn_ref, w_ref, mem_ref, gw_ref, dg_ref):
        dkv_bf = dkv_ref[...].astype(BF16)
        gw_ref[...] = _dot(memn_ref[...], dkv_bf, TN).astype(BF16).reshape(N_DEV, WKV_BLK, n)
        dmemn = _dot(dkv_bf, w_ref[...], NT)
        mv = mem_ref[...]
        r = lax.rsqrt(_rowmean(mv * mv) + EPS)
        dg_ref[...] = _colsum(dmemn * (mv * r))

    return pl.pallas_call(
        body,
        name="kv_backward",
        grid=(1,),
        in_specs=[_full((m, n)), _full((m, d)), _full(wkv.shape), _full((m, d))],
        out_specs=[_full((N_DEV, WKV_BLK, n)), _full((1, d))],
        out_shape=[jax.ShapeDtypeStruct((N_DEV, WKV_BLK, n), BF16), jax.ShapeDtypeStruct((1, d), F32)],
        compiler_params=_params("arbitrary"),
    )(dkv, memn, wkv, mem)


def _adamw_math(w, g, m, v):
    m = ADAM_B1 * m + (1.0 - ADAM_B1) * g
    v = ADAM_B2 * v + (1.0 - ADAM_B2) * (g * g)
    m_hat = m / (1.0 - ADAM_B1**ADAM_STEP)
    v_hat = v / (1.0 - ADAM_B2**ADAM_STEP)
    delta = -ADAM_LR * (m_hat / (jnp.sqrt(v_hat) + ADAM_EPS) + ADAM_WD * w)
    return delta, m, v


def _adamw(parts, w, m, v, name):
    r, c = w.shape
    slots = parts.shape[0]
    t = r
    while t * c * 4 > TILE_ADAM_BYTES and t % 16 == 0:
        t //= 2

    def body(p_ref, w_ref, m_ref, v_ref, g_ref, d_ref, nm_ref, nv_ref):
        g = p_ref[0].astype(F32)
        for k in range(1, slots):
            g = g + p_ref[k].astype(F32)
        delta, nm, nv = _adamw_math(w_ref[...], g, m_ref[...], v_ref[...])
        g_ref[...] = g
        d_ref[...] = delta
        nm_ref[...] = nm
        nv_ref[...] = nv

    tile = pl.BlockSpec((t, c), lambda i: (i, 0))
    return pl.pallas_call(
        body,
        name=name,
        grid=(r // t,),
        in_specs=[pl.BlockSpec((slots, t, c), lambda i: (0, i, 0)), tile, tile, tile],
        out_specs=[tile] * 4,
        out_shape=[jax.ShapeDtypeStruct((r, c), F32)] * 4,
        compiler_params=_params("parallel"),
    )(parts, w, m, v)


def _adamw_packed(parts, triples, name):
    slots = parts.shape[0]
    sizes = [w.shape[0] for w, _, _ in triples]
    rest = parts.shape[1] - sum(sizes)

    def total(p_ref, at, rows):
        g = p_ref[0, at : at + rows, :]
        for k in range(1, slots):
            g = g + p_ref[k, at : at + rows, :]
        return g

    def body(p_ref, *refs):
        ins = refs[: 3 * len(triples)]
        outs = refs[3 * len(triples) :]
        at = 0
        for n, rows in enumerate(sizes):
            g = total(p_ref, at, rows)
            w_ref, m_ref, v_ref = ins[3 * n : 3 * n + 3]
            delta, nm, nv = _adamw_math(w_ref[...], g, m_ref[...], v_ref[...])
            for ref, val in zip(outs[4 * n : 4 * n + 4], (g, delta, nm, nv)):
                ref[...] = val
            at += rows
        if rest:
            outs[-1][...] = total(p_ref, at, rest)

    flat = [a for t in triples for a in t]
    res = pl.pallas_call(
        body,
        name=name,
        out_shape=[jax.ShapeDtypeStruct(w.shape, F32) for w, _, _ in triples for _ in range(4)]
        + ([jax.ShapeDtypeStruct((rest, 128), F32)] if rest else []),
        compiler_params=pltpu.CompilerParams(vmem_limit_bytes=VMEM_LIMIT_BYTES),
    )(parts, *flat)
    return [res[4 * n : 4 * n + 4] for n in range(len(triples))], (res[-1] if rest else None)


SMALL = ("norm_pre", "pool_scale", "sgu_ln_g", "sgu_ln_b", "sgu_w", "sgu_b", "mem_norm", "branch_norm", "norm_post")


def _local_view(name, w):
    if name == "sgu_w":
        return w.reshape(SGU_HEADS, SGU_CHUNK, SGU_CHUNK)
    if name == "sgu_b":
        return w.reshape(SGU_HEADS, SGU_CHUNK)
    return w.reshape(1, -1)


def _forward_backward(x, mem, target, shards, small):
    causal = jnp.tril(jnp.ones((SGU_CHUNK, SGU_CHUNK), dtype=bool))
    sgu_wm = jnp.where(causal[None], small["sgu_w"], 0.0).astype(BF16)
    sgu_bias = jnp.repeat(jnp.transpose(small["sgu_b"]), SGU_HEAD_DIM, axis=1)

    proj, h, (win, wkv, pool_all, wout) = _proj_gather(x, small["norm_pre"], shards)
    wout = wout.reshape(MIX_WIDTH, D_MODEL)
    wkv = wkv.reshape(D_MODEL, 2 * BRANCH)
    pool_full = (
        pool_all.reshape(N_DEV, len(POOL_WINDOWS), POOL_BLK, POOL_GROUP_DIM)
        .transpose(1, 0, 2, 3)
        .reshape(len(POOL_WINDOWS), POOL_GROUP_DIM, POOL_GROUP_DIM)
    )
    memn, kv, kv_t = _kv_forward(mem, small["mem_norm"], wkv)
    (y, dout, dxo, dproj, loss, d_norm_post, d_branch_norm, d_pool_scale, d_ln_g, d_ln_b, d_pool_w, d_sgu_w, d_sgu_b,
     dkv) = _mix(
        proj, x, target, kv, kv_t, wout, wout.T, pool_full, jnp.swapaxes(pool_full, 1, 2), small["pool_scale"],
        small["sgu_ln_g"], small["sgu_ln_b"], sgu_bias, sgu_wm, jnp.swapaxes(sgu_wm, 1, 2), small["branch_norm"],
        small["norm_post"],
    )
    g_wkv, d_mem_norm = _kv_backward(dkv, memn, wkv, mem)
    g_pool = (
        d_pool_w.reshape(len(POOL_WINDOWS), N_DEV, POOL_BLK, POOL_GROUP_DIM)
        .transpose(1, 0, 2, 3)
        .reshape(N_DEV, len(POOL_WINDOWS) * POOL_BLK, POOL_GROUP_DIM)
        .astype(BF16)
    )
    small_grads = dict(
        pool_scale=d_pool_scale, sgu_ln_g=d_ln_g, sgu_ln_b=d_ln_b, sgu_w=d_sgu_w, sgu_b=d_sgu_b,
        mem_norm=d_mem_norm, branch_norm=d_branch_norm, norm_post=d_norm_post,
    )
    packed = jnp.concatenate(
        [small_grads[n].reshape(-1, LANES) for n in SMALL if n != "norm_pre"]
        + [jnp.broadcast_to(loss, (SUBLANES, LANES))],
        axis=0,
    )
    packed = packed[None, None]

    by_chip = lambda g: g.reshape((N_CHIPS, 2) + g.shape[1:])
    small_mine = [by_chip(g_wkv), by_chip(g_pool), packed]
    g_wout, wout_theirs, small_theirs = _weight_grad(y, dout, N_DEV, "rows", "grad_w_out", [("pair", small_mine)])
    sums, _ = _pair_sum(small_mine + [by_chip(g_wout)], list(small_theirs) + [wout_theirs], "pair_sum_first")
    g_win, win_theirs, (l_wkv, l_pool, l_packed, l_wout) = _weight_grad(
        h, dproj, N_DEV, "cols", "grad_w_in", [("chips", list(sums))]
    )
    (win_sums,), _ = _pair_sum([by_chip(g_win)], [win_theirs], "pair_sum_w_in")
    win_late, _ = lax.optimization_barrier((win, g_wkv))
    grad_x, d_norm_pre, (l_win,) = _input_grad(
        dproj, jnp.swapaxes(win_late, 1, 2).reshape(IN_WIDTH, D_MODEL), x, dxo, small["norm_pre"],
        [("chips", [win_sums])],
    )
    return grad_x, dict(w_in=l_win, w_out=l_wout, w_kv=l_wkv, pool_w=l_pool), l_packed, d_norm_pre


def kernel(x, mem, norm_pre, w_in, pool_w, pool_scale, sgu_ln_g, sgu_ln_b, sgu_w, sgu_b, mem_norm, w_kv, branch_norm, w_out, norm_post, loss_target, m_norm_pre, m_w_in, m_pool_w, m_pool_scale, m_sgu_ln_g, m_sgu_ln_b, m_sgu_w, m_sgu_b, m_mem_norm, m_w_kv, m_branch_norm, m_w_out, m_norm_post, v_norm_pre, v_w_in, v_pool_w, v_pool_scale, v_sgu_ln_g, v_sgu_ln_b, v_sgu_w, v_sgu_b, v_mem_norm, v_w_kv, v_branch_norm, v_w_out, v_norm_post):
    weights = dict(norm_pre=norm_pre, w_in=w_in, pool_w=pool_w, pool_scale=pool_scale, sgu_ln_g=sgu_ln_g, sgu_ln_b=sgu_ln_b, sgu_w=sgu_w, sgu_b=sgu_b, mem_norm=mem_norm, w_kv=w_kv, branch_norm=branch_norm, w_out=w_out, norm_post=norm_post)
    first = dict(norm_pre=m_norm_pre, w_in=m_w_in, pool_w=m_pool_w, pool_scale=m_pool_scale, sgu_ln_g=m_sgu_ln_g, sgu_ln_b=m_sgu_ln_b, sgu_w=m_sgu_w, sgu_b=m_sgu_b, mem_norm=m_mem_norm, w_kv=m_w_kv, branch_norm=m_branch_norm, w_out=m_w_out, norm_post=m_norm_post)
    second = dict(norm_pre=v_norm_pre, w_in=v_w_in, pool_w=v_pool_w, pool_scale=v_pool_scale, sgu_ln_g=v_sgu_ln_g, sgu_ln_b=v_sgu_ln_b, sgu_w=v_sgu_w, sgu_b=v_sgu_b, mem_norm=v_mem_norm, w_kv=v_w_kv, branch_norm=v_branch_norm, w_out=v_w_out, norm_post=v_norm_post)
    order = ("norm_pre", "w_in", "pool_w", "pool_scale", "sgu_ln_g", "sgu_ln_b", "sgu_w", "sgu_b", "mem_norm", "w_kv", "branch_norm", "w_out", "norm_post")

    owned_shape = dict(
        w_in=(D_MODEL, WIN_BLK), w_out=(WOUT_BLK, D_MODEL), w_kv=(WKV_BLK, 2 * BRANCH),
        pool_w=(len(POOL_WINDOWS) * POOL_BLK, POOL_GROUP_DIM),
    )
    owned = {n: weights[n].reshape(owned_shape[n]) for n in owned_shape}
    small = {n: _local_view(n, weights[n]) for n in SMALL}
    grad_x, landed, landed_packed, d_norm_pre = _forward_backward(
        x[0], mem[0], loss_target[0], [owned[n].astype(BF16) for n in ("w_in", "w_kv", "pool_w", "w_out")], small
    )
    landed_norm_pre = d_norm_pre.reshape(N_DEV, -1, LANES)

    grads, deltas, new_m, new_v = {}, {}, {}, {}
    for n in owned_shape:
        shape = weights[n].shape
        res = _adamw(
            landed[n], owned[n], first[n].reshape(owned_shape[n]), second[n].reshape(owned_shape[n]), "adamw_" + n
        )
        grads[n], deltas[n], new_m[n], new_v[n] = (a.reshape(shape) for a in res)
    rows_of = lambda tree, n: tree[n].reshape(-1, 128)
    for names, parts, name in (
        ([n for n in SMALL if n != "norm_pre"], landed_packed, "adamw_replicated"),
        (["norm_pre"], landed_norm_pre, "adamw_norm_pre"),
    ):
        res, rest = _adamw_packed(
            parts, [(rows_of(weights, n), rows_of(first, n), rows_of(second, n)) for n in names], name
        )
        if rest is not None:
            total = rest[0, 0]
        for n, four in zip(names, res):
            for tree, a in zip((grads, deltas, new_m, new_v), four):
                tree[n] = a.reshape(weights[n].shape)

    return (
        total,
        grad_x[None],
        *[grads[n] for n in order],
        *[deltas[n] for n in order],
        *[new_m[n] for n in order],
        *[new_v[n] for n in order],
    )
```

```python
import jax
import jax.numpy as jnp
from jax import lax
from jax.experimental import pallas as pl
from jax.experimental.pallas import tpu as pltpu

F32 = jnp.float32
BF16 = jnp.bfloat16
EPS = 1e-6

D_MODEL = 2048
POOL_WINDOWS = (2, 4, 8, 16)
POOL_GROUP_DIM = 256
BRANCH = 1024
SGU_CHUNK = 128
SGU_HEADS = 8
SGU_HEAD_DIM = 128
XATTN_HEADS = 4
XATTN_HEAD_DIM = 256
MIX_WIDTH = 3 * BRANCH
IN_WIDTH = 7 * BRANCH
N_DEV = 8
WIN_BLK = IN_WIDTH // N_DEV
WOUT_BLK = MIX_WIDTH // N_DEV
WKV_BLK = D_MODEL // N_DEV
POOL_BLK = POOL_GROUP_DIM // N_DEV
HALO = 16
LANES = 128
SUBLANES = 8

ADAM_LR = 0.001
ADAM_B1 = 0.9
ADAM_B2 = 0.999
ADAM_EPS = 1e-08
ADAM_WD = 0.01
ADAM_STEP = 10

VMEM_LIMIT_BYTES = 56 * 1024 * 1024
VMEM_LIMIT_MIX_BYTES = 63 * 1024 * 1024

TILE_PROJ = 512
TILE_MIX = 128
TILE_GRAD = 512
TILE_WEIGHT_GRAD = 1024
TILE_ADAM_BYTES = 1 << 20

ANY = pl.BlockSpec(memory_space=pl.ANY)
NN = (((1,), (0,)), ((), ()))
NT = (((1,), (1,)), ((), ()))
TN = (((0,), (0,)), ((), ()))
MESH = pl.DeviceIdType.MESH


def _dot(a, b, dims=NN):
    return lax.dot_general(a, b, dims, preferred_element_type=F32)


def _params(*semantics, vmem_limit_bytes=VMEM_LIMIT_BYTES):
    return pltpu.CompilerParams(dimension_semantics=semantics, vmem_limit_bytes=vmem_limit_bytes)


def _rowmean(a):
    return jnp.mean(a, axis=-1, keepdims=True)


def _colsum(a):
    return jnp.sum(a, axis=0, keepdims=True)


def _full(shape):
    zeros = (0,) * len(shape)
    return pl.BlockSpec(shape, lambda *_: zeros)


def _resident(shape):
    zeros = (0,) * len(shape)
    return pl.BlockSpec(shape, lambda *_: zeros, pipeline_mode=pl.Buffered(1))


def _kv_forward(mem, mem_norm, wkv):
    m, d = mem.shape
    n = wkv.shape[1]
    cols = 4 * LANES

    def body(mem_ref, g_ref, w_ref, memn_ref, kv_ref, kvt_ref):
        mv = mem_ref[...]
        r = lax.rsqrt(_rowmean(mv * mv) + EPS)
        memn = (mv * r * g_ref[...]).astype(BF16)
        memn_ref[...] = memn
        kv = _dot(memn, w_ref[...])
        kv_ref[...] = kv.astype(BF16)
        kvt_ref[...] = kv.T.astype(BF16)

    return pl.pallas_call(
        body,
        name="kv_forward",
        grid=(n // cols,),
        in_specs=[_full((m, d)), _full((1, d)), pl.BlockSpec((d, cols), lambda j: (0, j))],
        out_specs=[_full((m, d)), pl.BlockSpec((m, cols), lambda j: (0, j)), pl.BlockSpec((cols, m), lambda j: (j, 0))],
        out_shape=[
            jax.ShapeDtypeStruct((m, d), BF16), jax.ShapeDtypeStruct((m, n), BF16), jax.ShapeDtypeStruct((n, m), BF16)
        ],
        compiler_params=_params("arbitrary"),
    )(mem, mem_norm, wkv)


def _proj_gather(x_in, norm_pre, shards):
    s, d = x_in.shape
    t = min(TILE_PROJ, s)
    n_t = s // t
    n_arr = len(shards)

    def places(x, y, c):
        return (x, y, c), (x, y, 1 - c), (x ^ c, y ^ (1 - c)), (x ^ (1 - c), y ^ c), (1 - x, 1 - y)

    def index(chip, core):
        return 4 * chip[0] + 2 * chip[1] + core

    _, _, chip_a, chip_b, chip_d = places(*_position())
    c_out = lax.axis_index("c")
    me_out = index((lax.axis_index("x"), lax.axis_index("y")), c_out)
    order = jnp.stack(
        [
            me_out, me_out ^ 1, index(chip_a, c_out), index(chip_b, c_out), index(chip_b, 1 - c_out),
            index(chip_a, 1 - c_out), index(chip_d, c_out), index(chip_d, 1 - c_out),
        ]
    ).astype(jnp.int32)

    def body(order_ref, x_ref, g_ref, *refs):
        del order_ref
        src = refs[:n_arr]
        proj_ref, h_ref = refs[n_arr : n_arr + 2]
        out = refs[n_arr + 2 : 2 * n_arr + 2]
        wbuf, hs, send_sems, recv_sems, local_sems, load_sems = refs[2 * n_arr + 2 :]
        j = pl.program_id(0)
        i = pl.program_id(1)
        me, sibling, chip_a, chip_b, chip_d = places(*_position())
        c = me[2]

        def block(a, chip, core):
            return out[a].at[index(chip, core)]

        def copy(a, k, owner, to, from_input=False):
            return pltpu.make_async_remote_copy(
                src_ref=src[a] if from_input else block(a, *owner),
                dst_ref=block(a, *owner),
                send_sem=send_sems.at[a, k],
                recv_sem=recv_sems.at[a, k],
                device_id=to,
                device_id_type=MESH,
            )

        mine = (me[:2], c)

        def own(a):
            return pltpu.make_async_copy(src[a], block(a, *mine), local_sems.at[a])

        def first_sends(a):
            return [
                copy(a, 0, mine, sibling, from_input=True),
                copy(a, 1, mine, (*chip_a, c), from_input=True),
                copy(a, 2, mine, (*chip_b, c), from_input=True),
            ]

        def onward(a, k):
            owner = {3: chip_a, 4: chip_a, 5: chip_b, 6: chip_d}[k]
            return copy(a, k, (owner, c), (*chip_b, c) if k == 3 else sibling)

        def landed(a, k):
            owner = {0: mine[0], 1: chip_a, 2: chip_b, 3: chip_d, 4: chip_b, 5: chip_a, 6: chip_d}[k]
            core = c if k in (1, 2, 3) else 1 - c
            copy(a, k, (owner, core), me).wait_recv()
            return owner, core

        def load(ref, step):
            return pltpu.make_async_copy(ref, wbuf.at[step % 2], load_sems.at[step % 2])

        @pl.when(jnp.logical_and(j == 0, i == 0))
        def _():
            own(0).start()
            for cp in first_sends(0):
                cp.start()
            load(src[0], 0).start()
            load(src[0], 0).wait()

        steps = {1: (0, ()), 2: (1, (3, 4)), 3: (2, (5,)), 4: (4, ()), 5: (5, ()), 6: (3, (6,)), 7: (6, ())}
        for step, (k, then) in steps.items():

            @pl.when(jnp.logical_and(j == step, i == 0))
            def _():
                load(src[0], step).wait()

            @pl.when(jnp.logical_and(j == step - 1, i == n_t - 1))
            def _():
                owner = landed(0, k)
                for k2 in then:
                    onward(0, k2).start()
                if k == 1:
                    for a in range(1, n_arr):
                        own(a).start()
                        for cp in first_sends(a):
                            cp.start()
                if k == 3:
                    for a in range(1, n_arr):
                        for k1, then1 in ((1, (3, 4)), (2, (5,))):
                            landed(a, k1)
                            for k2 in then1:
                                onward(a, k2).start()
                load(block(0, *owner), step).start()

        @pl.when(j == 0)
        def _():
            xv = x_ref[...]
            h = (xv * lax.rsqrt(_rowmean(xv * xv) + EPS) * g_ref[...]).astype(BF16)
            hs[i] = h
            h_ref[...] = h

        proj_ref[...] = _dot(hs[i], wbuf[j % 2]).astype(BF16)

        @pl.when(jnp.logical_and(j == N_DEV - 1, i == n_t - 1))
        def _():
            for a in range(1, n_arr):
                landed(a, 3)
                onward(a, 6).start()
            for a in range(1, n_arr):
                for k in (0, 4, 5, 6):
                    landed(a, k)
            for a in range(n_arr):
                for cp in first_sends(a) + [onward(a, k) for k in (3, 4, 5, 6)]:
                    cp.wait_send()
                own(a).wait()

    res = pl.pallas_call(
        body,
        name="proj_gather",
        grid_spec=pltpu.PrefetchScalarGridSpec(
            num_scalar_prefetch=1,
            grid=(N_DEV, n_t),
            in_specs=[
                pl.BlockSpec((t, d), lambda j, i, order_ref: (jnp.where(j == 0, i, n_t - 1), 0)),
                pl.BlockSpec((1, d), lambda j, i, order_ref: (0, 0)),
            ]
            + [ANY] * n_arr,
            out_specs=[
                pl.BlockSpec((t, WIN_BLK), lambda j, i, order_ref: (i, order_ref[j])),
                pl.BlockSpec((t, d), lambda j, i, order_ref: (jnp.where(j == 0, i, n_t - 1), 0)),
            ]
            + [ANY] * n_arr,
            scratch_shapes=[
                pltpu.VMEM((2,) + shards[0].shape, BF16),
                pltpu.VMEM((n_t, t, d), BF16),
                pltpu.SemaphoreType.DMA((n_arr, 7)),
                pltpu.SemaphoreType.DMA((n_arr, 7)),
                pltpu.SemaphoreType.DMA((n_arr,)),
                pltpu.SemaphoreType.DMA((2,)),
            ],
        ),
        out_shape=[jax.ShapeDtypeStruct((s, IN_WIDTH), BF16), jax.ShapeDtypeStruct((s, d), BF16)]
        + [jax.ShapeDtypeStruct((N_DEV,) + a.shape, a.dtype) for a in shards],
        compiler_params=_params("arbitrary", "arbitrary"),
    )(order, x_in, norm_pre, *shards)
    return res[0], res[1], res[2:]


def _sigmoid(a):
    return jax.nn.sigmoid(a)


def _dsilu(a, sg):
    return sg * (1.0 + a * (1.0 - sg))


def _rms_fwd(u, gain):
    r = lax.rsqrt(_rowmean(u * u) + EPS)
    n = u * r
    return r, n, n * gain


def _rms_bwd(dy, gain, r, n):
    dn = dy * gain
    return _colsum(dy * n), r * (dn - n * _rowmean(dn * n))


def _mix(proj, x, target, kv, kv_t, wout, wout_t, pool_w, pool_w_t, pool_scale, ln_g, ln_b, sgu_bias, sgu_wm, sgu_wm_t, branch_norm, norm_post):
    s, d = x.shape
    t = min(TILE_MIX, s)
    n_tiles = s // t
    n_chunks = t // SGU_CHUNK
    halo_blocks_per_tile = t // HALO
    inv_d = 1.0 / d
    scale = 1.0 / (XATTN_HEAD_DIM**0.5)

    def body(
        proj_ref, halo_ref, x_ref, tgt_ref, kv_ref, kvt_ref, wout_hbm, wout_t_hbm, pw_ref, pwt_ref, pscale_ref, lng_ref,
        lnb_ref, bias_ref, wm_ref, wmt_ref, bnorm_ref, gpost_ref,
        y_ref, dout_ref, dxo_ref, dproj_ref, loss_ref, dgpost_ref, dbnorm_ref, dpscale_ref, dlng_ref, dlnb_ref,
        dpw_out, dwm_out, dbias_ref, dkv_out,
        carry_ref, dzsum_ref, dpw_ref, dwm_ref, dkv_ref, wout_ref, wout_t_ref, wout_sems,
    ):
        i = pl.program_id(0)
        tile = n_tiles - 1 - i
        wout_load = pltpu.make_async_copy(wout_hbm, wout_ref, wout_sems.at[0])
        wout_t_load = pltpu.make_async_copy(wout_t_hbm, wout_t_ref, wout_sems.at[1])

        @pl.when(i == 0)
        def _():
            wout_load.start()
            wout_t_load.start()
            carry_ref[...] = jnp.zeros_like(carry_ref)
            dzsum_ref[...] = jnp.zeros_like(dzsum_ref)
            for ref in (loss_ref, dgpost_ref, dbnorm_ref, dpscale_ref, dlng_ref, dlnb_ref, dpw_ref, dwm_ref, dkv_ref):
                ref[...] = jnp.zeros_like(ref)

        t_glob = tile * t + lax.broadcasted_iota(jnp.int32, (t, 1), 0)
        inv_cnt = [1.0 / jnp.minimum(t_glob + 1, w).astype(F32) for w in POOL_WINDOWS]

        xa = proj_ref[:, 0:BRANCH].astype(F32)
        ga = proj_ref[:, BRANCH : 2 * BRANCH].astype(F32)
        halo = jnp.where(tile == 0, 0.0, halo_ref[...].astype(F32))
        d_bf, pm_parts = [], []
        for g, w in enumerate(POOL_WINDOWS):
            cols = slice(g * POOL_GROUP_DIM, (g + 1) * POOL_GROUP_DIM)
            acc = jnp.concatenate([halo[:, cols], xa[:, cols]], axis=0)
            k = 1
            while k < w:
                acc = acc + pltpu.roll(acc, k, axis=0)
                k *= 2
            dg = (acc[HALO:, :] * inv_cnt[g] - xa[:, cols]).astype(BF16)
            d_bf.append(dg)
            pm_parts.append(_dot(dg, pw_ref[g]))
        pm = jnp.concatenate(pm_parts, axis=1)
        pscale = pscale_ref[...]
        pa = pm * pscale
        sga = _sigmoid(ga)
        sila = ga * sga
        ua = pa * sila
        g_a = bnorm_ref[:, 0:BRANCH]
        ra, na, ya = _rms_fwd(ua, g_a)

        u = proj_ref[:, 2 * BRANCH : 3 * BRANCH].astype(F32)
        v = proj_ref[:, 3 * BRANCH : 4 * BRANCH].astype(F32)
        gb = proj_ref[:, 4 * BRANCH : 5 * BRANCH].astype(F32)
        lng = lng_ref[...]
        vc = v - _rowmean(v)
        rstd = lax.rsqrt(_rowmean(vc * vc) + EPS)
        vhat = vc * rstd
        vn_bf = (vhat * lng + lnb_ref[...]).astype(BF16)
        z_rows = []
        for c in range(n_chunks):
            rows = slice(c * SGU_CHUNK, (c + 1) * SGU_CHUNK)
            z_rows.append(
                jnp.concatenate(
                    [
                        _dot(wm_ref[hd], vn_bf[rows, hd * SGU_HEAD_DIM : (hd + 1) * SGU_HEAD_DIM])
                        for hd in range(SGU_HEADS)
                    ],
                    axis=1,
                )
                + bias_ref[...]
            )
        z = z_rows[0] if n_chunks == 1 else jnp.concatenate(z_rows, axis=0)
        sb = u * z
        sgb = _sigmoid(gb)
        silb = gb * sgb
        ub = sb * silb
        g_b = bnorm_ref[:, BRANCH : 2 * BRANCH]
        rb, nb, yb = _rms_fwd(ub, g_b)

        q = proj_ref[:, 5 * BRANCH : 6 * BRANCH]
        gc = proj_ref[:, 6 * BRANCH : 7 * BRANCH].astype(F32)
        q_bf, p_bf, o_parts = [], [], []
        for hd in range(XATTN_HEADS):
            cols = slice(hd * XATTN_HEAD_DIM, (hd + 1) * XATTN_HEAD_DIM)
            qh = q[:, cols]
            sc = _dot(qh, kvt_ref[cols, :]) * scale
            e = jnp.exp(sc - jnp.max(sc, axis=-1, keepdims=True))
            p = e / jnp.sum(e, axis=-1, keepdims=True)
            q_bf.append(qh)
            p_bf.append(p.astype(BF16))
            o_parts.append(_dot(p_bf[hd], kv_ref[:, BRANCH + hd * XATTN_HEAD_DIM : BRANCH + (hd + 1) * XATTN_HEAD_DIM]))
        o = jnp.concatenate(o_parts, axis=1)
        sgc = _sigmoid(gc)
        silc = gc * sgc
        uc = o * silc
        g_c = bnorm_ref[:, 2 * BRANCH : 3 * BRANCH]
        rc, nc, yc = _rms_fwd(uc, g_c)

        @pl.when(i == 0)
        def _():
            wout_load.wait()

        out = None
        for b, y_branch in enumerate((ya, yb, yc)):
            rows = slice(b * BRANCH, (b + 1) * BRANCH)
            y_bf = y_branch.astype(BF16)
            y_ref[:, rows] = y_bf
            part = _dot(y_bf, wout_ref[rows, :])
            out = part if out is None else out + part
        gpost = gpost_ref[...]
        r_out = lax.rsqrt(_rowmean(out * out) + EPS)
        on = out * r_out
        err = x_ref[...] + on * gpost - tgt_ref[...]
        loss_ref[...] += 0.5 * jnp.sum(_rowmean(err * err), axis=0, keepdims=True)

        dxo = err * inv_d
        dxo_ref[...] = dxo
        dgp, dout = _rms_bwd(dxo, gpost, r_out, on)
        dgpost_ref[...] += dgp
        dout_bf = dout.astype(BF16)
        dout_ref[...] = dout_bf

        @pl.when(i == 0)
        def _():
            wout_t_load.wait()

        dy = [_dot(dout_bf, wout_t_ref[:, b * BRANCH : (b + 1) * BRANCH]) for b in range(3)]

        dg_a, dua = _rms_bwd(dy[0], g_a, ra, na)
        dg_b, dub = _rms_bwd(dy[1], g_b, rb, nb)
        dg_c, duc = _rms_bwd(dy[2], g_c, rc, nc)
        dbnorm_ref[...] += jnp.concatenate([dg_a, dg_b, dg_c], axis=1)

        dpa = dua * sila
        dga = dua * pa * _dsilu(ga, sga)
        dpscale_ref[...] += _colsum(dpa * pm)
        dpm = dpa * pscale
        dxa_parts, carry_parts = [], []
        for g, w in enumerate(POOL_WINDOWS):
            cols = slice(g * POOL_GROUP_DIM, (g + 1) * POOL_GROUP_DIM)
            dpm_g = dpm[:, cols].astype(BF16)
            dd = _dot(dpm_g, pwt_ref[g])
            dpw_ref[g] += _dot(d_bf[g], dpm_g, TN)
            cg = dd * inv_cnt[g]
            carry_parts.append(cg[0:HALO, :])
            acc = jnp.concatenate([cg, carry_ref[:, cols]], axis=0)
            k = 1
            while k < w:
                acc = acc + pltpu.roll(acc, t + HALO - k, axis=0)
                k *= 2
            dxa_parts.append(acc[0:t, :] - dd)
        carry_ref[...] = jnp.concatenate(carry_parts, axis=1)
        dxa = jnp.concatenate(dxa_parts, axis=1)

        dsb = dub * silb
        dgb = dub * sb * _dsilu(gb, sgb)
        du = dsb * z
        dz = dsb * u
        dz_bf = dz.astype(BF16)
        dvn_rows = []
        dz_sum = None
        for c in range(n_chunks):
            rows = slice(c * SGU_CHUNK, (c + 1) * SGU_CHUNK)
            dz_sum = dz[rows, :] if dz_sum is None else dz_sum + dz[rows, :]
            parts = []
            for hd in range(SGU_HEADS):
                cols = slice(hd * SGU_HEAD_DIM, (hd + 1) * SGU_HEAD_DIM)
                parts.append(_dot(wmt_ref[hd], dz_bf[rows, cols]))
                dwm_ref[hd] += _dot(dz_bf[rows, cols], vn_bf[rows, cols], NT)
            dvn_rows.append(jnp.concatenate(parts, axis=1))
        dzsum_ref[...] += dz_sum
        dvn = dvn_rows[0] if n_chunks == 1 else jnp.concatenate(dvn_rows, axis=0)
        dlng_ref[...] += _colsum(dvn * vhat)
        dlnb_ref[...] += _colsum(dvn)
        dvh = dvn * lng
        dv = rstd * (dvh - _rowmean(dvh) - vhat * _rowmean(dvh * vhat))

        do = duc * silc
        dgc = duc * o * _dsilu(gc, sgc)
        dq_parts = []
        for hd in range(XATTN_HEADS):
            cols = slice(hd * XATTN_HEAD_DIM, (hd + 1) * XATTN_HEAD_DIM)
            vcols = slice(BRANCH + hd * XATTN_HEAD_DIM, BRANCH + (hd + 1) * XATTN_HEAD_DIM)
            do_h = do[:, cols].astype(BF16)
            p = p_bf[hd].astype(F32)
            dp = _dot(do_h, kvt_ref[vcols, :])
            dkv_ref[:, vcols] += _dot(p_bf[hd], do_h, TN)
            ds_bf = (p * (dp - jnp.sum(dp * p, axis=-1, keepdims=True)) * scale).astype(BF16)
            dq_parts.append(_dot(ds_bf, kv_ref[:, cols]))
            dkv_ref[:, cols] += _dot(ds_bf, q_bf[hd], TN)
        dq = jnp.concatenate(dq_parts, axis=1)

        dproj_ref[...] = jnp.concatenate([dxa, dga, du, dv, dgb, dq, dgc], axis=1).astype(BF16)

        @pl.when(i == n_tiles - 1)
        def _():
            keep = lax.broadcasted_iota(jnp.int32, (SGU_CHUNK, SGU_CHUNK), 0) >= lax.broadcasted_iota(
                jnp.int32, (SGU_CHUNK, SGU_CHUNK), 1
            )
            for hd in range(SGU_HEADS):
                dwm_ref[hd] = jnp.where(keep, dwm_ref[hd], 0.0)
                per_pos = dzsum_ref[:, hd * SGU_HEAD_DIM : (hd + 1) * SGU_HEAD_DIM]
                dbias_ref[hd : hd + 1, :] = _colsum(per_pos.T)
            for acc, res in ((dpw_ref, dpw_out), (dwm_ref, dwm_out), (dkv_ref, dkv_out)):
                pltpu.sync_copy(acc, res)

    row_tile = lambda width: pl.BlockSpec((t, width), lambda i: (n_tiles - 1 - i, 0))
    halo_spec = pl.BlockSpec(
        (HALO, BRANCH), lambda i: (jnp.maximum((n_tiles - 1 - i) * halo_blocks_per_tile - 1, 0), 0)
    )
    acc_shapes = [
        (1, 128),
        (1, d),
        (1, MIX_WIDTH),
        (1, BRANCH),
        (1, BRANCH),
        (1, BRANCH),
        pool_w.shape,
        sgu_wm.shape,
        (SGU_HEADS, SGU_CHUNK),
        kv.shape,
    ]
    return pl.pallas_call(
        body,
        name="mix",
        grid=(n_tiles,),
        in_specs=[
            row_tile(IN_WIDTH), halo_spec, row_tile(d), row_tile(d), _resident(kv.shape), _resident(kv_t.shape),
            ANY, ANY, _resident(pool_w.shape), _resident(pool_w_t.shape),
            _full((1, BRANCH)), _full((1, BRANCH)), _full((1, BRANCH)), _resident((SGU_CHUNK, BRANCH)),
            _resident(sgu_wm.shape), _resident(sgu_wm_t.shape), _full((1, MIX_WIDTH)), _full((1, d)),
        ],
        out_specs=[row_tile(MIX_WIDTH), row_tile(d), row_tile(d), row_tile(IN_WIDTH)]
        + [ANY if len(a) == 3 or a == kv.shape else _full(a) for a in acc_shapes],
        out_shape=[
            jax.ShapeDtypeStruct((s, MIX_WIDTH), BF16),
            jax.ShapeDtypeStruct((s, d), BF16),
            jax.ShapeDtypeStruct((s, d), F32),
            jax.ShapeDtypeStruct((s, IN_WIDTH), BF16),
        ]
        + [jax.ShapeDtypeStruct(a, F32) for a in acc_shapes],
        scratch_shapes=[
            pltpu.VMEM((HALO, BRANCH), F32), pltpu.VMEM((SGU_CHUNK, BRANCH), F32), pltpu.VMEM(pool_w.shape, F32),
            pltpu.VMEM(sgu_wm.shape, F32), pltpu.VMEM(kv.shape, F32), pltpu.VMEM(wout.shape, BF16),
            pltpu.VMEM(wout_t.shape, BF16), pltpu.SemaphoreType.DMA((2,)),
        ],
        compiler_params=_params("arbitrary", vmem_limit_bytes=VMEM_LIMIT_MIX_BYTES),
    )(
        proj, proj, x, target, kv, kv_t, wout, wout_t, pool_w, pool_w_t, pool_scale, ln_g, ln_b, sgu_bias, sgu_wm,
        sgu_wm_t, branch_norm, norm_post,
    )


def _position():
    return lax.axis_index("x"), lax.axis_index("y"), lax.axis_index("c")


N_CHIPS = 4


def _landing_shape(kind, a):
    return (N_CHIPS,) + a.shape[2:] if kind == "pair" else a.shape


def _carry_specs(groups):
    arrays = [(kind, a) for kind, arrs in groups for a in arrs]
    scratch = []
    for _, arrs in groups:
        n = len(arrs)
        scratch += [pltpu.SemaphoreType.DMA((n, N_DEV)), pltpu.SemaphoreType.DMA((n, N_DEV)), pltpu.SemaphoreType.DMA((n,))]
    return dict(
        n=len(arrays),
        operands=[a for _, a in arrays],
        in_specs=[ANY] * len(arrays),
        out_specs=[ANY] * len(arrays),
        out_shape=[jax.ShapeDtypeStruct(_landing_shape(kind, a), a.dtype) for kind, a in arrays],
        scratch_shapes=scratch,
    )


def _carry(groups, src, out, sems):
    x, y, c = _position()
    chip = 2 * x + y

    def remote(s, d, send_sems, recv_sems, a, m, to):
        return pltpu.make_async_remote_copy(
            src_ref=s, dst_ref=d, send_sem=send_sems.at[a, m], recv_sem=recv_sems.at[a, m], device_id=to,
            device_id_type=MESH,
        )

    def copies():
        far, near = [], []
        at = 0
        for g, (kind, arrs) in enumerate(groups):
            send_sems, recv_sems, local_sems = sems[3 * g : 3 * g + 3]
            for a in range(len(arrs)):
                s, d = src[at + a], out[at + a]
                if kind == "pair" and s.shape[0] == 1:
                    for b in range(N_CHIPS):
                        far.append(remote(s.at[0, 0], d.at[b], send_sems, recv_sems, a, 1 + b, (x, y, 1 - c)))
                elif kind == "pair":
                    far.append(remote(s.at[:, 1 - c], d, send_sems, recv_sems, a, 1, (x, y, 1 - c)))
                else:
                    assert kind == "chips", kind
                    for m in range(1, N_CHIPS):
                        px, py = x ^ (m >> 1), y ^ (m & 1)
                        far.append(remote(s.at[2 * px + py], d.at[chip], send_sems, recv_sems, a, m, (px, py, c)))
                    near.append(pltpu.make_async_copy(s.at[chip], d.at[chip], local_sems.at[a]))
            at += len(arrs)
        return far, near

    def start():
        far, near = copies()
        for cp in near + far:
            cp.start()

    def finish():
        far, near = copies()
        for cp in far:
            cp.wait_recv()
        for cp in far:
            cp.wait_send()
        for cp in near:
            cp.wait()

    return start, finish


def _pair_sum(mine, theirs, name, groups=()):
    n = len(mine)
    carried = _carry_specs(groups)
    n_c = carried["n"]
    core = lax.axis_index("c").astype(jnp.int32).reshape(1)

    def body(core_ref, *refs):
        del core_ref
        own = refs[:n]
        sib = refs[n : 2 * n]
        src = refs[2 * n : 2 * n + n_c]
        out = refs[2 * n + n_c : 3 * n + n_c]
        landed = refs[3 * n + n_c : 3 * n + 2 * n_c]
        start, finish = _carry(groups, src, landed, refs[3 * n + 2 * n_c :])
        b = pl.program_id(0)

        @pl.when(b == 0)
        def _():
            start()

        for a in range(n):
            out[a][...] = (own[a][...].astype(F32) + sib[a][...].astype(F32)).astype(out[a].dtype)

        @pl.when(b == N_CHIPS - 1)
        def _():
            finish()

    block = lambda a: pl.BlockSpec((None,) + a.shape[1:], lambda b, core_ref: (b, 0, 0))
    res = pl.pallas_call(
        body,
        name=name,
        grid_spec=pltpu.PrefetchScalarGridSpec(
            num_scalar_prefetch=1,
            grid=(N_CHIPS,),
            in_specs=[
                pl.BlockSpec((None, None) + a.shape[2:], lambda b, core_ref: (0, 0, 0, 0))
                if a.shape[0] == 1
                else pl.BlockSpec((None, None) + a.shape[2:], lambda b, core_ref: (b, core_ref[0], 0, 0))
                for a in mine
            ]
            + [block(a) for a in theirs]
            + carried["in_specs"],
            out_specs=[block(a) for a in theirs] + carried["out_specs"],
            scratch_shapes=carried["scratch_shapes"],
        ),
        out_shape=[jax.ShapeDtypeStruct(a.shape, a.dtype) for a in theirs] + carried["out_shape"],
        compiler_params=_params("arbitrary"),
    )(core, *mine, *theirs, *carried["operands"])
    return res[:n], res[n:]


def _weight_grad(a, b, n_blk, blocked, name, groups):
    s = a.shape[0]
    t = min(TILE_WEIGHT_GRAD, s)
    n_t = s // t
    n_pairs = n_blk // 2
    if blocked == "cols":
        k, c = a.shape[1], b.shape[1] // n_blk
        a_spec = pl.BlockSpec((t, k), lambda j, i: (i, 0))
        b_spec = pl.BlockSpec((t, 2 * c), lambda j, i: (i, j))
        acc_shape = (k, 2 * c)
    else:
        k, c = a.shape[1] // n_blk, b.shape[1]
        a_spec = pl.BlockSpec((t, 2 * k), lambda j, i: (i, j))
        b_spec = _resident(b.shape)
        acc_shape = (2 * k, c)
    carried = _carry_specs(groups)
    n_p = carried["n"]

    def body(a_ref, b_ref, *refs):
        src = refs[:n_p]
        o_ref, theirs_ref = refs[n_p : n_p + 2]
        landed = refs[n_p + 2 : 2 * n_p + 2]
        acc_ref, sbuf, pair_send, pair_recv = refs[2 * n_p + 2 : 2 * n_p + 6]
        start, finish = _carry(groups, src, landed, refs[2 * n_p + 6 :])
        j = pl.program_id(0)
        i = pl.program_id(1)
        x, y, c_me = _position()

        def to_sibling(pair):
            return pltpu.make_async_remote_copy(
                src_ref=sbuf.at[1 - c_me], dst_ref=theirs_ref.at[pair], send_sem=pair_send.at[pair],
                recv_sem=pair_recv.at[pair], device_id=(x, y, 1 - c_me), device_id_type=MESH,
            )

        @pl.when(jnp.logical_and(j == 0, i == 0))
        def _():
            start()

        @pl.when(i == 0)
        def _():
            acc_ref[...] = jnp.zeros_like(acc_ref)

        b_tile = b_ref[...] if blocked == "cols" else b_ref[pl.ds(pl.multiple_of(i * t, t), t), :]
        acc_ref[...] += _dot(a_ref[...], b_tile, TN)

        @pl.when(i == n_t - 1)
        def _():
            for pair in range(1, n_pairs):

                @pl.when(j == pair)
                def _():
                    to_sibling(pair - 1).wait_send()

            for half in range(2):
                if blocked == "cols":
                    block = acc_ref[:, half * c : (half + 1) * c].astype(BF16)
                else:
                    block = acc_ref[half * k : (half + 1) * k, :].astype(BF16)
                o_ref[half] = block
                sbuf[half] = block
            for pair in range(n_pairs):

                @pl.when(j == pair)
                def _():
                    to_sibling(pair).start()

        @pl.when(jnp.logical_and(j == n_pairs - 1, i == n_t - 1))
        def _():
            to_sibling(n_pairs - 1).wait_send()
            for pair in range(n_pairs):
                to_sibling(pair).wait_recv()
            finish()

    res = pl.pallas_call(
        body,
        name=name,
        grid=(n_pairs, n_t),
        in_specs=[a_spec, b_spec] + carried["in_specs"],
        out_specs=[pl.BlockSpec((2, k, c), lambda j, i: (j, 0, 0)), ANY] + carried["out_specs"],
        out_shape=[jax.ShapeDtypeStruct((n_blk, k, c), BF16), jax.ShapeDtypeStruct((n_pairs, k, c), BF16)]
        + carried["out_shape"],
        scratch_shapes=[
            pltpu.VMEM(acc_shape, F32), pltpu.VMEM((2, k, c), BF16), pltpu.SemaphoreType.DMA((n_pairs,)),
            pltpu.SemaphoreType.DMA((n_pairs,)),
        ]
        + carried["scratch_shapes"],
        compiler_params=_params("arbitrary", "arbitrary", vmem_limit_bytes=VMEM_LIMIT_MIX_BYTES),
    )(a, b, *carried["operands"])
    return res[0], res[1], res[2:]


def _input_grad(dproj, win_t, x, dxo, norm_pre, groups):
    s, d = x.shape
    t = min(TILE_GRAD, s)
    n_t = s // t
    kb = 2 * WIN_BLK
    n_k = win_t.shape[0] // kb
    carried = _carry_specs(groups)
    n_p = carried["n"]

    def body(dp_ref, w_ref, x_ref, dxo_ref, g_ref, *refs):
        src = refs[:n_p]
        gx_ref, dg_all = refs[n_p : n_p + 2]
        landed = refs[n_p + 2 : 2 * n_p + 2]
        acc_ref, dg_ref, dg_send, dg_recv, dg_local = refs[2 * n_p + 2 : 2 * n_p + 7]
        start, finish = _carry(groups, src, landed, refs[2 * n_p + 7 :])
        i = pl.program_id(0)
        j = pl.program_id(1)
        px, py, pc = _position()
        me = 4 * px + 2 * py + pc

        def dg_copies():
            far = [
                pltpu.make_async_remote_copy(
                    src_ref=dg_ref, dst_ref=dg_all.at[me], send_sem=dg_send.at[m], recv_sem=dg_recv.at[m],
                    device_id=(px ^ ((m >> 2) & 1), py ^ ((m >> 1) & 1), pc ^ (m & 1)), device_id_type=MESH,
                )
                for m in range(1, N_DEV)
            ]
            return far, pltpu.make_async_copy(dg_ref, dg_all.at[me], dg_local)

        @pl.when(jnp.logical_and(i == 0, j == 0))
        def _():
            start()
            dg_ref[...] = jnp.zeros_like(dg_ref)

        @pl.when(j == 0)
        def _():
            acc_ref[...] = jnp.zeros_like(acc_ref)

        acc_ref[...] += _dot(dp_ref[...], w_ref[...])

        @pl.when(j == n_k - 1)
        def _():
            xv = x_ref[...]
            gain = g_ref[...]
            r = lax.rsqrt(_rowmean(xv * xv) + EPS)
            dgain, dx = _rms_bwd(acc_ref[...], gain, r, xv * r)
            dg_ref[...] += dgain
            gx_ref[...] = dxo_ref[...] + dx

        @pl.when(jnp.logical_and(i == n_t - 1, j == n_k - 1))
        def _():
            far, near = dg_copies()
            for cp in [near] + far:
                cp.start()
            finish()
            for cp in far:
                cp.wait_recv()
            for cp in far:
                cp.wait_send()
            near.wait()

    res = pl.pallas_call(
        body,
        name="input_grad",
        grid=(n_t, n_k),
        in_specs=[
            pl.BlockSpec((t, kb), lambda i, j: (i, j)),
            pl.BlockSpec((kb, d), lambda i, j: (j, 0)),
            pl.BlockSpec((t, d), lambda i, j: (i, 0)),
            pl.BlockSpec((t, d), lambda i, j: (i, 0)),
            _full((1, d)),
        ]
        + carried["in_specs"],
        out_specs=[pl.BlockSpec((t, d), lambda i, j: (i, 0)), ANY] + carried["out_specs"],
        out_shape=[jax.ShapeDtypeStruct((s, d), F32), jax.ShapeDtypeStruct((N_DEV, 1, d), F32)] + carried["out_shape"],
        scratch_shapes=[
            pltpu.VMEM((t, d), F32), pltpu.VMEM((1, d), F32), pltpu.SemaphoreType.DMA((N_DEV,)),
            pltpu.SemaphoreType.DMA((N_DEV,)), pltpu.SemaphoreType.DMA,
        ]
        + carried["scratch_shapes"],
        compiler_params=_params("arbitrary", "arbitrary", vmem_limit_bytes=VMEM_LIMIT_MIX_BYTES),
    )(dproj, win_t, x, dxo, norm_pre, *carried["operands"])
    return res[0], res[1], res[2:]


def _kv_backward(dkv, memn, wkv, mem):
    m, d = mem.shape
    n = wkv.shape[1]

    def body(dkv_ref, memn_ref, w_ref, mem_ref, gw_ref, dg_ref):
        dkv_bf = dkv_ref[...].astype(BF16)
        gw_ref[...] = _dot(memn_ref[...], dkv_bf, TN).astype(BF16).reshape(N_DEV, WKV_BLK, n)
        dmemn = _dot(dkv_bf, w_ref[...], NT)
        mv = mem_ref[...]
        r = lax.rsqrt(_rowmean(mv * mv) + EPS)
        dg_ref[...] = _colsum(dmemn * (mv * r))

    return pl.pallas_call(
        body,
        name="kv_backward",
        grid=(1,),
        in_specs=[_full((m, n)), _full((m, d)), _full(wkv.shape), _full((m, d))],
        out_specs=[_full((N_DEV, WKV_BLK, n)), _full((1, d))],
        out_shape=[jax.ShapeDtypeStruct((N_DEV, WKV_BLK, n), BF16), jax.ShapeDtypeStruct((1, d), F32)],
        compiler_params=_params("arbitrary"),
    )(dkv, memn, wkv, mem)


def _adamw_math(w, g, m, v):
    m = ADAM_B1 * m + (1.0 - ADAM_B1) * g
    v = ADAM_B2 * v + (1.0 - ADAM_B2) * (g * g)
    m_hat = m / (1.0 - ADAM_B1**ADAM_STEP)
    v_hat = v / (1.0 - ADAM_B2**ADAM_STEP)
    delta = -ADAM_LR * (m_hat / (jnp.sqrt(v_hat) + ADAM_EPS) + ADAM_WD * w)
    return delta, m, v


def _adamw(parts, w, m, v, name):
    r, c = w.shape
    slots = parts.shape[0]
    t = r
    while t * c * 4 > TILE_ADAM_BYTES and t % 16 == 0:
        t //= 2

    def body(p_ref, w_ref, m_ref, v_ref, g_ref, d_ref, nm_ref, nv_ref):
        g = p_ref[0].astype(F32)
        for k in range(1, slots):
            g = g + p_ref[k].astype(F32)
        delta, nm, nv = _adamw_math(w_ref[...], g, m_ref[...], v_ref[...])
        g_ref[...] = g
        d_ref[...] = delta
        nm_ref[...] = nm
        nv_ref[...] = nv

    tile = pl.BlockSpec((t, c), lambda i: (i, 0))
    return pl.pallas_call(
        body,
        name=name,
        grid=(r // t,),
        in_specs=[pl.BlockSpec((slots, t, c), lambda i: (0, i, 0)), tile, tile, tile],
        out_specs=[tile] * 4,
        out_shape=[jax.ShapeDtypeStruct((r, c), F32)] * 4,
        compiler_params=_params("parallel"),
    )(parts, w, m, v)


def _adamw_packed(parts, triples, name):
    slots = parts.shape[0]
    sizes = [w.shape[0] for w, _, _ in triples]
    rest = parts.shape[1] - sum(sizes)

    def total(p_ref, at, rows):
        g = p_ref[0, at : at + rows, :]
        for k in range(1, slots):
            g = g + p_ref[k, at : at + rows, :]
        return g

    def body(p_ref, *refs):
        ins = refs[: 3 * len(triples)]
        outs = refs[3 * len(triples) :]
        at = 0
        for n, rows in enumerate(sizes):
            g = total(p_ref, at, rows)
            w_ref, m_ref, v_ref = ins[3 * n : 3 * n + 3]
            delta, nm, nv = _adamw_math(w_ref[...], g, m_ref[...], v_ref[...])
            for ref, val in zip(outs[4 * n : 4 * n + 4], (g, delta, nm, nv)):
                ref[...] = val
            at += rows
        if rest:
            outs[-1][...] = total(p_ref, at, rest)

    flat = [a for t in triples for a in t]
    res = pl.pallas_call(
        body,
        name=name,
        out_shape=[jax.ShapeDtypeStruct(w.shape, F32) for w, _, _ in triples for _ in range(4)]
        + ([jax.ShapeDtypeStruct((rest, 128), F32)] if rest else []),
        compiler_params=pltpu.CompilerParams(vmem_limit_bytes=VMEM_LIMIT_BYTES),
    )(parts, *flat)
    return [res[4 * n : 4 * n + 4] for n in range(len(triples))], (res[-1] if rest else None)


SMALL = ("norm_pre", "pool_scale", "sgu_ln_g", "sgu_ln_b", "sgu_w", "sgu_b", "mem_norm", "branch_norm", "norm_post")


def _local_view(name, w):
    if name == "sgu_w":
        return w.reshape(SGU_HEADS, SGU_CHUNK, SGU_CHUNK)
    if name == "sgu_b":
        return w.reshape(SGU_HEADS, SGU_CHUNK)
    return w.reshape(1, -1)


def _forward_backward(x, mem, target, shards, small):
    causal = jnp.tril(jnp.ones((SGU_CHUNK, SGU_CHUNK), dtype=bool))
    sgu_wm = jnp.where(causal[None], small["sgu_w"], 0.0).astype(BF16)
    sgu_bias = jnp.repeat(jnp.transpose(small["sgu_b"]), SGU_HEAD_DIM, axis=1)

    proj, h, (win, wkv, pool_all, wout) = _proj_gather(x, small["norm_pre"], shards)
    wout = wout.reshape(MIX_WIDTH, D_MODEL)
    wkv = wkv.reshape(D_MODEL, 2 * BRANCH)
    pool_full = (
        pool_all.reshape(N_DEV, len(POOL_WINDOWS), POOL_BLK, POOL_GROUP_DIM)
        .transpose(1, 0, 2, 3)
        .reshape(len(POOL_WINDOWS), POOL_GROUP_DIM, POOL_GROUP_DIM)
    )
    memn, kv, kv_t = _kv_forward(mem, small["mem_norm"], wkv)
    (y, dout, dxo, dproj, loss, d_norm_post, d_branch_norm, d_pool_scale, d_ln_g, d_ln_b, d_pool_w, d_sgu_w, d_sgu_b,
     dkv) = _mix(
        proj, x, target, kv, kv_t, wout, wout.T, pool_full, jnp.swapaxes(pool_full, 1, 2), small["pool_scale"],
        small["sgu_ln_g"], small["sgu_ln_b"], sgu_bias, sgu_wm, jnp.swapaxes(sgu_wm, 1, 2), small["branch_norm"],
        small["norm_post"],
    )
    g_wkv, d_mem_norm = _kv_backward(dkv, memn, wkv, mem)
    g_pool = (
        d_pool_w.reshape(len(POOL_WINDOWS), N_DEV, POOL_BLK, POOL_GROUP_DIM)
        .transpose(1, 0, 2, 3)
        .reshape(N_DEV, len(POOL_WINDOWS) * POOL_BLK, POOL_GROUP_DIM)
        .astype(BF16)
    )
    small_grads = dict(
        pool_scale=d_pool_scale, sgu_ln_g=d_ln_g, sgu_ln_b=d_ln_b, sgu_w=d_sgu_w, sgu_b=d_sgu_b,
        mem_norm=d_mem_norm, branch_norm=d_branch_norm, norm_post=d_norm_post,
    )
    packed = jnp.concatenate(
        [small_grads[n].reshape(-1, LANES) for n in SMALL if n != "norm_pre"]
        + [jnp.broadcast_to(loss, (SUBLANES, LANES))],
        axis=0,
    )
    packed = packed[None, None]

    by_chip = lambda g: g.reshape((N_CHIPS, 2) + g.shape[1:])
    small_mine = [by_chip(g_wkv), by_chip(g_pool), packed]
    g_wout, wout_theirs, small_theirs = _weight_grad(y, dout, N_DEV, "rows", "grad_w_out", [("pair", small_mine)])
    sums, _ = _pair_sum(small_mine + [by_chip(g_wout)], list(small_theirs) + [wout_theirs], "pair_sum_first")
    g_win, win_theirs, (l_wkv, l_pool, l_packed, l_wout) = _weight_grad(
        h, dproj, N_DEV, "cols", "grad_w_in", [("chips", list(sums))]
    )
    (win_sums,), _ = _pair_sum([by_chip(g_win)], [win_theirs], "pair_sum_w_in")
    win_late, _ = lax.optimization_barrier((win, g_wkv))
    grad_x, d_norm_pre, (l_win,) = _input_grad(
        dproj, jnp.swapaxes(win_late, 1, 2).reshape(IN_WIDTH, D_MODEL), x, dxo, small["norm_pre"],
        [("chips", [win_sums])],
    )
    return grad_x, dict(w_in=l_win, w_out=l_wout, w_kv=l_wkv, pool_w=l_pool), l_packed, d_norm_pre


def kernel(x, mem, norm_pre, w_in, pool_w, pool_scale, sgu_ln_g, sgu_ln_b, sgu_w, sgu_b, mem_norm, w_kv, branch_norm, w_out, norm_post, loss_target, m_norm_pre, m_w_in, m_pool_w, m_pool_scale, m_sgu_ln_g, m_sgu_ln_b, m_sgu_w, m_sgu_b, m_mem_norm, m_w_kv, m_branch_norm, m_w_out, m_norm_post, v_norm_pre, v_w_in, v_pool_w, v_pool_scale, v_sgu_ln_g, v_sgu_ln_b, v_sgu_w, v_sgu_b, v_mem_norm, v_w_kv, v_branch_norm, v_w_out, v_norm_post):
    weights = dict(norm_pre=norm_pre, w_in=w_in, pool_w=pool_w, pool_scale=pool_scale, sgu_ln_g=sgu_ln_g, sgu_ln_b=sgu_ln_b, sgu_w=sgu_w, sgu_b=sgu_b, mem_norm=mem_norm, w_kv=w_kv, branch_norm=branch_norm, w_out=w_out, norm_post=norm_post)
    first = dict(norm_pre=m_norm_pre, w_in=m_w_in, pool_w=m_pool_w, pool_scale=m_pool_scale, sgu_ln_g=m_sgu_ln_g, sgu_ln_b=m_sgu_ln_b, sgu_w=m_sgu_w, sgu_b=m_sgu_b, mem_norm=m_mem_norm, w_kv=m_w_kv, branch_norm=m_branch_norm, w_out=m_w_out, norm_post=m_norm_post)
    second = dict(norm_pre=v_norm_pre, w_in=v_w_in, pool_w=v_pool_w, pool_scale=v_pool_scale, sgu_ln_g=v_sgu_ln_g, sgu_ln_b=v_sgu_ln_b, sgu_w=v_sgu_w, sgu_b=v_sgu_b, mem_norm=v_mem_norm, w_kv=v_w_kv, branch_norm=v_branch_norm, w_out=v_w_out, norm_post=v_norm_post)
    order = ("norm_pre", "w_in", "pool_w", "pool_scale", "sgu_ln_g", "sgu_ln_b", "sgu_w", "sgu_b", "mem_norm", "w_kv", "branch_norm", "w_out", "norm_post")

    owned_shape = dict(
        w_in=(D_MODEL, WIN_BLK), w_out=(WOUT_BLK, D_MODEL), w_kv=(WKV_BLK, 2 * BRANCH),
        pool_w=(len(POOL_WINDOWS) * POOL_BLK, POOL_GROUP_DIM),
    )
    owned = {n: weights[n].reshape(owned_shape[n]) for n in owned_shape}
    small = {n: _local_view(n, weights[n]) for n in SMALL}
    grad_x, landed, landed_packed, d_norm_pre = _forward_backward(
        x[0], mem[0], loss_target[0], [owned[n].astype(BF16) for n in ("w_in", "w_kv", "pool_w", "w_out")], small
    )
    landed_norm_pre = d_norm_pre.reshape(N_DEV, -1, LANES)

    grads, deltas, new_m, new_v = {}, {}, {}, {}
    for n in owned_shape:
        shape = weights[n].shape
        res = _adamw(
            landed[n], owned[n], first[n].reshape(owned_shape[n]), second[n].reshape(owned_shape[n]), "adamw_" + n
        )
        grads[n], deltas[n], new_m[n], new_v[n] = (a.reshape(shape) for a in res)
    rows_of = lambda tree, n: tree[n].reshape(-1, 128)
    for names, parts, name in (
        ([n for n in SMALL if n != "norm_pre"], landed_packed, "adamw_replicated"),
        (["norm_pre"], landed_norm_pre, "adamw_norm_pre"),
    ):
        res, rest = _adamw_packed(
            parts, [(rows_of(weights, n), rows_of(first, n), rows_of(second, n)) for n in names], name
        )
        if rest is not None:
            total = rest[0, 0]
        for n, four in zip(names, res):
            for tree, a in zip((grads, deltas, new_m, new_v), four):
                tree[n] = a.reshape(weights[n].shape)

    return (
        total,
        grad_x[None],
        *[grads[n] for n in order],
        *[deltas[n] for n in order],
        *[new_m[n] for n in order],
        *[new_v[n] for n in order],
    )
```

```python
import jax
import jax.numpy as jnp
from jax import lax
from jax.experimental import pallas as pl
from jax.experimental.pallas import tpu as pltpu

F32 = jnp.float32
BF16 = jnp.bfloat16
EPS = 1e-6

D_MODEL = 2048
POOL_WINDOWS = (2, 4, 8, 16)
POOL_GROUP_DIM = 256
BRANCH = 1024
SGU_CHUNK = 128
SGU_HEADS = 8
SGU_HEAD_DIM = 128
XATTN_HEADS = 4
XATTN_HEAD_DIM = 256
MIX_WIDTH = 3 * BRANCH
IN_WIDTH = 7 * BRANCH
N_DEV = 8
WIN_BLK = IN_WIDTH // N_DEV
WOUT_BLK = MIX_WIDTH // N_DEV
WKV_BLK = D_MODEL // N_DEV
POOL_BLK = POOL_GROUP_DIM // N_DEV
HALO = 16
LANES = 128
SUBLANES = 8

ADAM_LR = 0.001
ADAM_B1 = 0.9
ADAM_B2 = 0.999
ADAM_EPS = 1e-08
ADAM_WD = 0.01
ADAM_STEP = 10

VMEM_LIMIT_BYTES = 56 * 1024 * 1024
VMEM_LIMIT_MIX_BYTES = 63 * 1024 * 1024

TILE_PROJ = 512
TILE_MIX = 128
TILE_GRAD = 512
TILE_WEIGHT_GRAD = 1024
TILE_ADAM_BYTES = 1 << 20

ANY = pl.BlockSpec(memory_space=pl.ANY)
NN = (((1,), (0,)), ((), ()))
NT = (((1,), (1,)), ((), ()))
TN = (((0,), (0,)), ((), ()))
MESH = pl.DeviceIdType.MESH


def _dot(a, b, dims=NN):
    return lax.dot_general(a, b, dims, preferred_element_type=F32)


def _params(*semantics, vmem_limit_bytes=VMEM_LIMIT_BYTES):
    return pltpu.CompilerParams(dimension_semantics=semantics, vmem_limit_bytes=vmem_limit_bytes)


def _rowmean(a):
    return jnp.mean(a, axis=-1, keepdims=True)


def _colsum(a):
    return jnp.sum(a, axis=0, keepdims=True)


def _full(shape):
    zeros = (0,) * len(shape)
    return pl.BlockSpec(shape, lambda *_: zeros)


def _resident(shape):
    zeros = (0,) * len(shape)
    return pl.BlockSpec(shape, lambda *_: zeros, pipeline_mode=pl.Buffered(1))


def _kv_forward(mem, mem_norm, wkv):
    m, d = mem.shape
    n = wkv.shape[1]
    cols = 4 * LANES

    def body(mem_ref, g_ref, w_ref, memn_ref, kv_ref, kvt_ref):
        mv = mem_ref[...]
        r = lax.rsqrt(_rowmean(mv * mv) + EPS)
        memn = (mv * r * g_ref[...]).astype(BF16)
        memn_ref[...] = memn
        kv = _dot(memn, w_ref[...])
        kv_ref[...] = kv.astype(BF16)
        kvt_ref[...] = kv.T.astype(BF16)

    return pl.pallas_call(
        body,
        name="kv_forward",
        grid=(n // cols,),
        in_specs=[_full((m, d)), _full((1, d)), pl.BlockSpec((d, cols), lambda j: (0, j))],
        out_specs=[_full((m, d)), pl.BlockSpec((m, cols), lambda j: (0, j)), pl.BlockSpec((cols, m), lambda j: (j, 0))],
        out_shape=[
            jax.ShapeDtypeStruct((m, d), BF16), jax.ShapeDtypeStruct((m, n), BF16), jax.ShapeDtypeStruct((n, m), BF16)
        ],
        compiler_params=_params("arbitrary"),
    )(mem, mem_norm, wkv)


def _proj_gather(x_in, norm_pre, shards):
    s, d = x_in.shape
    t = min(TILE_PROJ, s)
    n_t = s // t
    n_arr = len(shards)

    def places(x, y, c):
        return (x, y, c), (x, y, 1 - c), (x ^ c, y ^ (1 - c)), (x ^ (1 - c), y ^ c), (1 - x, 1 - y)

    def index(chip, core):
        return 4 * chip[0] + 2 * chip[1] + core

    _, _, chip_a, chip_b, chip_d = places(*_position())
    c_out = lax.axis_index("c")
    me_out = index((lax.axis_index("x"), lax.axis_index("y")), c_out)
    order = jnp.stack(
        [
            me_out, me_out ^ 1, index(chip_a, c_out), index(chip_b, c_out), index(chip_b, 1 - c_out),
            index(chip_a, 1 - c_out), index(chip_d, c_out), index(chip_d, 1 - c_out),
        ]
    ).astype(jnp.int32)

    def body(order_ref, x_ref, g_ref, *refs):
        del order_ref
        raw = refs[:n_arr]
        proj_ref, h_ref = refs[n_arr : n_arr + 2]
        out = refs[n_arr + 2 : 2 * n_arr + 2]
        staged = refs[2 * n_arr + 2 : 3 * n_arr + 1]
        src = (raw[0],) + tuple(staged)
        wbuf, hs, send_sems, recv_sems, local_sems, load_sems = refs[3 * n_arr + 1 : 3 * n_arr + 7]
        wide = refs[3 * n_arr + 7 : 4 * n_arr + 6]
        narrow = refs[4 * n_arr + 6 :]
        j = pl.program_id(0)
        i = pl.program_id(1)
        me, sibling, chip_a, chip_b, chip_d = places(*_position())
        c = me[2]

        def block(a, chip, core):
            return out[a].at[index(chip, core)]

        def copy(a, k, owner, to, from_input=False):
            return pltpu.make_async_remote_copy(
                src_ref=src[a] if from_input else block(a, *owner),
                dst_ref=block(a, *owner),
                send_sem=send_sems.at[a, k],
                recv_sem=recv_sems.at[a, k],
                device_id=to,
                device_id_type=MESH,
            )

        mine = (me[:2], c)

        def own(a):
            return pltpu.make_async_copy(src[a], block(a, *mine), local_sems.at[a])

        def first_sends(a):
            return [
                copy(a, 0, mine, sibling, from_input=True),
                copy(a, 1, mine, (*chip_a, c), from_input=True),
                copy(a, 2, mine, (*chip_b, c), from_input=True),
            ]

        def onward(a, k):
            owner = {3: chip_a, 4: chip_a, 5: chip_b, 6: chip_d}[k]
            return copy(a, k, (owner, c), (*chip_b, c) if k == 3 else sibling)

        def landed(a, k):
            owner = {0: mine[0], 1: chip_a, 2: chip_b, 3: chip_d, 4: chip_b, 5: chip_a, 6: chip_d}[k]
            core = c if k in (1, 2, 3) else 1 - c
            copy(a, k, (owner, core), me).wait_recv()
            return owner, core

        def load(ref, step):
            return pltpu.make_async_copy(ref, wbuf.at[step % 2], load_sems.at[step % 2])

        @pl.when(jnp.logical_and(j == 0, i == 0))
        def _():
            own(0).start()
            for cp in first_sends(0):
                cp.start()
            for a in range(1, n_arr):
                pltpu.sync_copy(raw[a], wide[a - 1])
                narrow[a - 1][...] = wide[a - 1][...].astype(BF16)
                pltpu.sync_copy(narrow[a - 1], staged[a - 1])
            load(src[0], 0).start()
            load(src[0], 0).wait()

        steps = {1: (0, ()), 2: (1, (3, 4)), 3: (2, (5,)), 4: (4, ()), 5: (5, ()), 6: (3, (6,)), 7: (6, ())}
        for step, (k, then) in steps.items():

            @pl.when(jnp.logical_and(j == step, i == 0))
            def _():
                load(src[0], step).wait()

            @pl.when(jnp.logical_and(j == step - 1, i == n_t - 1))
            def _():
                owner = landed(0, k)
                for k2 in then:
                    onward(0, k2).start()
                if k == 1:
                    for a in range(1, n_arr):
                        own(a).start()
                        for cp in first_sends(a):
                            cp.start()
                if k == 3:
                    for a in range(1, n_arr):
                        for k1, then1 in ((1, (3, 4)), (2, (5,))):
                            landed(a, k1)
                            for k2 in then1:
                                onward(a, k2).start()
                load(block(0, *owner), step).start()

        @pl.when(j == 0)
        def _():
            xv = x_ref[...]
            h = (xv * lax.rsqrt(_rowmean(xv * xv) + EPS) * g_ref[...]).astype(BF16)
            hs[i] = h
            h_ref[...] = h

        proj_ref[...] = _dot(hs[i], wbuf[j % 2]).astype(BF16)

        @pl.when(jnp.logical_and(j == N_DEV - 1, i == n_t - 1))
        def _():
            for a in range(1, n_arr):
                landed(a, 3)
                onward(a, 6).start()
            for a in range(1, n_arr):
                for k in (0, 4, 5, 6):
                    landed(a, k)
            for a in range(n_arr):
                for cp in first_sends(a) + [onward(a, k) for k in (3, 4, 5, 6)]:
                    cp.wait_send()
                own(a).wait()

    res = pl.pallas_call(
        body,
        name="proj_gather",
        grid_spec=pltpu.PrefetchScalarGridSpec(
            num_scalar_prefetch=1,
            grid=(N_DEV, n_t),
            in_specs=[
                pl.BlockSpec((t, d), lambda j, i, order_ref: (jnp.where(j == 0, i, n_t - 1), 0)),
                pl.BlockSpec((1, d), lambda j, i, order_ref: (0, 0)),
            ]
            + [ANY] * n_arr,
            out_specs=[
                pl.BlockSpec((t, WIN_BLK), lambda j, i, order_ref: (i, order_ref[j])),
                pl.BlockSpec((t, d), lambda j, i, order_ref: (jnp.where(j == 0, i, n_t - 1), 0)),
            ]
            + [ANY] * (2 * n_arr - 1),
            scratch_shapes=[
                pltpu.VMEM((2,) + shards[0].shape, BF16),
                pltpu.VMEM((n_t, t, d), BF16),
                pltpu.SemaphoreType.DMA((n_arr, 7)),
                pltpu.SemaphoreType.DMA((n_arr, 7)),
                pltpu.SemaphoreType.DMA((n_arr,)),
                pltpu.SemaphoreType.DMA((2,)),
            ]
            + [pltpu.VMEM(a.shape, F32) for a in shards[1:]]
            + [pltpu.VMEM(a.shape, BF16) for a in shards[1:]],
        ),
        out_shape=[jax.ShapeDtypeStruct((s, IN_WIDTH), BF16), jax.ShapeDtypeStruct((s, d), BF16)]
        + [jax.ShapeDtypeStruct((N_DEV,) + a.shape, BF16) for a in shards]
        + [jax.ShapeDtypeStruct(a.shape, BF16) for a in shards[1:]],
        compiler_params=_params("arbitrary", "arbitrary"),
    )(order, x_in, norm_pre, *shards)
    return res[0], res[1], res[2 : 2 + n_arr]


def _sigmoid(a):
    return jax.nn.sigmoid(a)


def _dsilu(a, sg):
    return sg * (1.0 + a * (1.0 - sg))


def _rms_fwd(u, gain):
    r = lax.rsqrt(_rowmean(u * u) + EPS)
    n = u * r
    return r, n, n * gain


def _rms_bwd(dy, gain, r, n):
    dn = dy * gain
    return _colsum(dy * n), r * (dn - n * _rowmean(dn * n))


def _mix(proj, x, target, kv, kv_t, wout, wout_t, pool_w, pool_w_t, pool_scale, ln_g, ln_b, sgu_bias, sgu_wm, sgu_wm_t, branch_norm, norm_post):
    s, d = x.shape
    t = min(TILE_MIX, s)
    n_tiles = s // t
    n_chunks = t // SGU_CHUNK
    halo_blocks_per_tile = t // HALO
    inv_d = 1.0 / d
    scale = 1.0 / (XATTN_HEAD_DIM**0.5)

    def body(
        proj_ref, halo_ref, x_ref, tgt_ref, kv_ref, kvt_ref, wout_hbm, wout_t_hbm, pw_ref, pwt_ref, pscale_ref, lng_ref,
        lnb_ref, bias_ref, wm_ref, wmt_ref, bnorm_ref, gpost_ref,
        y_ref, dout_ref, dxo_ref, dproj_ref, loss_ref, dgpost_ref, dbnorm_ref, dpscale_ref, dlng_ref, dlnb_ref,
        dpw_out, dwm_out, dbias_ref, dkv_out,
        carry_ref, dzsum_ref, dpw_ref, dwm_ref, dkv_ref, wout_ref, wout_t_ref, wout_sems,
    ):
        i = pl.program_id(0)
        tile = n_tiles - 1 - i
        wout_load = pltpu.make_async_copy(wout_hbm, wout_ref, wout_sems.at[0])
        wout_t_load = pltpu.make_async_copy(wout_t_hbm, wout_t_ref, wout_sems.at[1])

        @pl.when(i == 0)
        def _():
            wout_load.start()
            wout_t_load.start()
            carry_ref[...] = jnp.zeros_like(carry_ref)
            dzsum_ref[...] = jnp.zeros_like(dzsum_ref)
            for ref in (loss_ref, dgpost_ref, dbnorm_ref, dpscale_ref, dlng_ref, dlnb_ref, dpw_ref, dwm_ref, dkv_ref):
                ref[...] = jnp.zeros_like(ref)

        t_glob = tile * t + lax.broadcasted_iota(jnp.int32, (t, 1), 0)
        inv_cnt = [1.0 / jnp.minimum(t_glob + 1, w).astype(F32) for w in POOL_WINDOWS]

        xa = proj_ref[:, 0:BRANCH].astype(F32)
        ga = proj_ref[:, BRANCH : 2 * BRANCH].astype(F32)
        halo = jnp.where(tile == 0, 0.0, halo_ref[...].astype(F32))
        d_bf, pm_parts = [], []
        for g, w in enumerate(POOL_WINDOWS):
            cols = slice(g * POOL_GROUP_DIM, (g + 1) * POOL_GROUP_DIM)
            acc = jnp.concatenate([halo[:, cols], xa[:, cols]], axis=0)
            k = 1
            while k < w:
                acc = acc + pltpu.roll(acc, k, axis=0)
                k *= 2
            dg = (acc[HALO:, :] * inv_cnt[g] - xa[:, cols]).astype(BF16)
            d_bf.append(dg)
            pm_parts.append(_dot(dg, pw_ref[g]))
        pm = jnp.concatenate(pm_parts, axis=1)
        pscale = pscale_ref[...]
        pa = pm * pscale
        sga = _sigmoid(ga)
        sila = ga * sga
        ua = pa * sila
        g_a = bnorm_ref[:, 0:BRANCH]
        ra, na, ya = _rms_fwd(ua, g_a)

        u = proj_ref[:, 2 * BRANCH : 3 * BRANCH].astype(F32)
        v = proj_ref[:, 3 * BRANCH : 4 * BRANCH].astype(F32)
        gb = proj_ref[:, 4 * BRANCH : 5 * BRANCH].astype(F32)
        lng = lng_ref[...]
        vc = v - _rowmean(v)
        rstd = lax.rsqrt(_rowmean(vc * vc) + EPS)
        vhat = vc * rstd
        vn_bf = (vhat * lng + lnb_ref[...]).astype(BF16)
        z_rows = []
        for c in range(n_chunks):
            rows = slice(c * SGU_CHUNK, (c + 1) * SGU_CHUNK)
            z_rows.append(
                jnp.concatenate(
                    [
                        _dot(wm_ref[hd], vn_bf[rows, hd * SGU_HEAD_DIM : (hd + 1) * SGU_HEAD_DIM])
                        for hd in range(SGU_HEADS)
                    ],
                    axis=1,
                )
                + bias_ref[...]
            )
        z = z_rows[0] if n_chunks == 1 else jnp.concatenate(z_rows, axis=0)
        sb = u * z
        sgb = _sigmoid(gb)
        silb = gb * sgb
        ub = sb * silb
        g_b = bnorm_ref[:, BRANCH : 2 * BRANCH]
        rb, nb, yb = _rms_fwd(ub, g_b)

        q = proj_ref[:, 5 * BRANCH : 6 * BRANCH]
        gc = proj_ref[:, 6 * BRANCH : 7 * BRANCH].astype(F32)
        q_bf, p_bf, o_parts = [], [], []
        for hd in range(XATTN_HEADS):
            cols = slice(hd * XATTN_HEAD_DIM, (hd + 1) * XATTN_HEAD_DIM)
            qh = q[:, cols]
            sc = _dot(qh, kvt_ref[cols, :]) * scale
            e = jnp.exp(sc - jnp.max(sc, axis=-1, keepdims=True))
            p = e / jnp.sum(e, axis=-1, keepdims=True)
            q_bf.append(qh)
            p_bf.append(p.astype(BF16))
            o_parts.append(_dot(p_bf[hd], kv_ref[:, BRANCH + hd * XATTN_HEAD_DIM : BRANCH + (hd + 1) * XATTN_HEAD_DIM]))
        o = jnp.concatenate(o_parts, axis=1)
        sgc = _sigmoid(gc)
        silc = gc * sgc
        uc = o * silc
        g_c = bnorm_ref[:, 2 * BRANCH : 3 * BRANCH]
        rc, nc, yc = _rms_fwd(uc, g_c)

        @pl.when(i == 0)
        def _():
            wout_load.wait()

        out = None
        for b, y_branch in enumerate((ya, yb, yc)):
            rows = slice(b * BRANCH, (b + 1) * BRANCH)
            y_bf = y_branch.astype(BF16)
            y_ref[:, rows] = y_bf
            part = _dot(y_bf, wout_ref[rows, :])
            out = part if out is None else out + part
        gpost = gpost_ref[...]
        r_out = lax.rsqrt(_rowmean(out * out) + EPS)
        on = out * r_out
        err = x_ref[...] + on * gpost - tgt_ref[...]
        loss_ref[...] += 0.5 * jnp.sum(_rowmean(err * err), axis=0, keepdims=True)

        dxo = err * inv_d
        dxo_ref[...] = dxo
        dgp, dout = _rms_bwd(dxo, gpost, r_out, on)
        dgpost_ref[...] += dgp
        dout_bf = dout.astype(BF16)
        dout_ref[...] = dout_bf

        @pl.when(i == 0)
        def _():
            wout_t_load.wait()

        dy = [_dot(dout_bf, wout_t_ref[:, b * BRANCH : (b + 1) * BRANCH]) for b in range(3)]

        dg_a, dua = _rms_bwd(dy[0], g_a, ra, na)
        dg_b, dub = _rms_bwd(dy[1], g_b, rb, nb)
        dg_c, duc = _rms_bwd(dy[2], g_c, rc, nc)
        dbnorm_ref[...] += jnp.concatenate([dg_a, dg_b, dg_c], axis=1)

        dpa = dua * sila
        dga = dua * pa * _dsilu(ga, sga)
        dpscale_ref[...] += _colsum(dpa * pm)
        dpm = dpa * pscale
        dxa_parts, carry_parts = [], []
        for g, w in enumerate(POOL_WINDOWS):
            cols = slice(g * POOL_GROUP_DIM, (g + 1) * POOL_GROUP_DIM)
            dpm_g = dpm[:, cols].astype(BF16)
            dd = _dot(dpm_g, pwt_ref[g])
            dpw_ref[g] += _dot(d_bf[g], dpm_g, TN)
            cg = dd * inv_cnt[g]
            carry_parts.append(cg[0:HALO, :])
            acc = jnp.concatenate([cg, carry_ref[:, cols]], axis=0)
            k = 1
            while k < w:
                acc = acc + pltpu.roll(acc, t + HALO - k, axis=0)
                k *= 2
            dxa_parts.append(acc[0:t, :] - dd)
        carry_ref[...] = jnp.concatenate(carry_parts, axis=1)
        dxa = jnp.concatenate(dxa_parts, axis=1)

        dsb = dub * silb
        dgb = dub * sb * _dsilu(gb, sgb)
        du = dsb * z
        dz = dsb * u
        dz_bf = dz.astype(BF16)
        dvn_rows = []
        dz_sum = None
        for c in range(n_chunks):
            rows = slice(c * SGU_CHUNK, (c + 1) * SGU_CHUNK)
            dz_sum = dz[rows, :] if dz_sum is None else dz_sum + dz[rows, :]
            parts = []
            for hd in range(SGU_HEADS):
                cols = slice(hd * SGU_HEAD_DIM, (hd + 1) * SGU_HEAD_DIM)
                parts.append(_dot(wmt_ref[hd], dz_bf[rows, cols]))
                dwm_ref[hd] += _dot(dz_bf[rows, cols], vn_bf[rows, cols], NT)
            dvn_rows.append(jnp.concatenate(parts, axis=1))
        dzsum_ref[...] += dz_sum
        dvn = dvn_rows[0] if n_chunks == 1 else jnp.concatenate(dvn_rows, axis=0)
        dlng_ref[...] += _colsum(dvn * vhat)
        dlnb_ref[...] += _colsum(dvn)
        dvh = dvn * lng
        dv = rstd * (dvh - _rowmean(dvh) - vhat * _rowmean(dvh * vhat))

        do = duc * silc
        dgc = duc * o * _dsilu(gc, sgc)
        dq_parts = []
        for hd in range(XATTN_HEADS):
            cols = slice(hd * XATTN_HEAD_DIM, (hd + 1) * XATTN_HEAD_DIM)
            vcols = slice(BRANCH + hd * XATTN_HEAD_DIM, BRANCH + (hd + 1) * XATTN_HEAD_DIM)
            do_h = do[:, cols].astype(BF16)
            p = p_bf[hd].astype(F32)
            dp = _dot(do_h, kvt_ref[vcols, :])
            dkv_ref[:, vcols] += _dot(p_bf[hd], do_h, TN)
            ds_bf = (p * (dp - jnp.sum(dp * p, axis=-1, keepdims=True)) * scale).astype(BF16)
            dq_parts.append(_dot(ds_bf, kv_ref[:, cols]))
            dkv_ref[:, cols] += _dot(ds_bf, q_bf[hd], TN)
        dq = jnp.concatenate(dq_parts, axis=1)

        dproj_ref[...] = jnp.concatenate([dxa, dga, du, dv, dgb, dq, dgc], axis=1).astype(BF16)

        @pl.when(i == n_tiles - 1)
        def _():
            keep = lax.broadcasted_iota(jnp.int32, (SGU_CHUNK, SGU_CHUNK), 0) >= lax.broadcasted_iota(
                jnp.int32, (SGU_CHUNK, SGU_CHUNK), 1
            )
            for hd in range(SGU_HEADS):
                dwm_ref[hd] = jnp.where(keep, dwm_ref[hd], 0.0)
                per_pos = dzsum_ref[:, hd * SGU_HEAD_DIM : (hd + 1) * SGU_HEAD_DIM]
                dbias_ref[hd : hd + 1, :] = _colsum(per_pos.T)
            for acc, res in ((dpw_ref, dpw_out), (dwm_ref, dwm_out), (dkv_ref, dkv_out)):
                pltpu.sync_copy(acc, res)

    row_tile = lambda width: pl.BlockSpec((t, width), lambda i: (n_tiles - 1 - i, 0))
    halo_spec = pl.BlockSpec(
        (HALO, BRANCH), lambda i: (jnp.maximum((n_tiles - 1 - i) * halo_blocks_per_tile - 1, 0), 0)
    )
    acc_shapes = [
        (1, 128),
        (1, d),
        (1, MIX_WIDTH),
        (1, BRANCH),
        (1, BRANCH),
        (1, BRANCH),
        pool_w.shape,
        sgu_wm.shape,
        (SGU_HEADS, SGU_CHUNK),
        kv.shape,
    ]
    return pl.pallas_call(
        body,
        name="mix",
        grid=(n_tiles,),
        in_specs=[
            row_tile(IN_WIDTH), halo_spec, row_tile(d), row_tile(d), _resident(kv.shape), _resident(kv_t.shape),
            ANY, ANY, _resident(pool_w.shape), _resident(pool_w_t.shape),
            _full((1, BRANCH)), _full((1, BRANCH)), _full((1, BRANCH)), _resident((SGU_CHUNK, BRANCH)),
            _resident(sgu_wm.shape), _resident(sgu_wm_t.shape), _full((1, MIX_WIDTH)), _full((1, d)),
        ],
        out_specs=[row_tile(MIX_WIDTH), row_tile(d), row_tile(d), row_tile(IN_WIDTH)]
        + [ANY if len(a) == 3 or a == kv.shape else _full(a) for a in acc_shapes],
        out_shape=[
            jax.ShapeDtypeStruct((s, MIX_WIDTH), BF16),
            jax.ShapeDtypeStruct((s, d), BF16),
            jax.ShapeDtypeStruct((s, d), F32),
            jax.ShapeDtypeStruct((s, IN_WIDTH), BF16),
        ]
        + [jax.ShapeDtypeStruct(a, F32) for a in acc_shapes],
        scratch_shapes=[
            pltpu.VMEM((HALO, BRANCH), F32), pltpu.VMEM((SGU_CHUNK, BRANCH), F32), pltpu.VMEM(pool_w.shape, F32),
            pltpu.VMEM(sgu_wm.shape, F32), pltpu.VMEM(kv.shape, F32), pltpu.VMEM(wout.shape, BF16),
            pltpu.VMEM(wout_t.shape, BF16), pltpu.SemaphoreType.DMA((2,)),
        ],
        compiler_params=_params("arbitrary", vmem_limit_bytes=VMEM_LIMIT_MIX_BYTES),
    )(
        proj, proj, x, target, kv, kv_t, wout, wout_t, pool_w, pool_w_t, pool_scale, ln_g, ln_b, sgu_bias, sgu_wm,
        sgu_wm_t, branch_norm, norm_post,
    )


def _position():
    return lax.axis_index("x"), lax.axis_index("y"), lax.axis_index("c")


N_CHIPS = 4


def _landing_shape(kind, a):
    return (N_CHIPS,) + a.shape[2:] if kind == "pair" else a.shape


def _carry_specs(groups):
    arrays = [(kind, a) for kind, arrs in groups for a in arrs]
    scratch = []
    for _, arrs in groups:
        n = len(arrs)
        scratch += [pltpu.SemaphoreType.DMA((n, N_DEV)), pltpu.SemaphoreType.DMA((n, N_DEV)), pltpu.SemaphoreType.DMA((n,))]
    return dict(
        n=len(arrays),
        operands=[a for _, a in arrays],
        in_specs=[ANY] * len(arrays),
        out_specs=[ANY] * len(arrays),
        out_shape=[jax.ShapeDtypeStruct(_landing_shape(kind, a), a.dtype) for kind, a in arrays],
        scratch_shapes=scratch,
    )


def _carry(groups, src, out, sems):
    x, y, c = _position()
    chip = 2 * x + y

    def remote(s, d, send_sems, recv_sems, a, m, to):
        return pltpu.make_async_remote_copy(
            src_ref=s, dst_ref=d, send_sem=send_sems.at[a, m], recv_sem=recv_sems.at[a, m], device_id=to,
            device_id_type=MESH,
        )

    def copies():
        far, near = [], []
        at = 0
        for g, (kind, arrs) in enumerate(groups):
            send_sems, recv_sems, local_sems = sems[3 * g : 3 * g + 3]
            for a in range(len(arrs)):
                s, d = src[at + a], out[at + a]
                if kind == "pair" and s.shape[0] == 1:
                    for b in range(N_CHIPS):
                        far.append(remote(s.at[0, 0], d.at[b], send_sems, recv_sems, a, 1 + b, (x, y, 1 - c)))
                elif kind == "pair":
                    far.append(remote(s.at[:, 1 - c], d, send_sems, recv_sems, a, 1, (x, y, 1 - c)))
                else:
                    assert kind == "chips", kind
                    for m in range(1, N_CHIPS):
                        px, py = x ^ (m >> 1), y ^ (m & 1)
                        far.append(remote(s.at[2 * px + py], d.at[chip], send_sems, recv_sems, a, m, (px, py, c)))
                    near.append(pltpu.make_async_copy(s.at[chip], d.at[chip], local_sems.at[a]))
            at += len(arrs)
        return far, near

    def start():
        far, near = copies()
        for cp in near + far:
            cp.start()

    def finish():
        far, near = copies()
        for cp in far:
            cp.wait_recv()
        for cp in far:
            cp.wait_send()
        for cp in near:
            cp.wait()

    return start, finish


def _pair_sum(mine, theirs, name, groups=()):
    n = len(mine)
    carried = _carry_specs(groups)
    n_c = carried["n"]
    core = lax.axis_index("c").astype(jnp.int32).reshape(1)

    def body(core_ref, *refs):
        del core_ref
        own = refs[:n]
        sib = refs[n : 2 * n]
        src = refs[2 * n : 2 * n + n_c]
        out = refs[2 * n + n_c : 3 * n + n_c]
        landed = refs[3 * n + n_c : 3 * n + 2 * n_c]
        start, finish = _carry(groups, src, landed, refs[3 * n + 2 * n_c :])
        b = pl.program_id(0)

        @pl.when(b == 0)
        def _():
            start()

        for a in range(n):
            out[a][...] = (own[a][...].astype(F32) + sib[a][...].astype(F32)).astype(out[a].dtype)

        @pl.when(b == N_CHIPS - 1)
        def _():
            finish()

    block = lambda a: pl.BlockSpec((None,) + a.shape[1:], lambda b, core_ref: (b, 0, 0))
    res = pl.pallas_call(
        body,
        name=name,
        grid_spec=pltpu.PrefetchScalarGridSpec(
            num_scalar_prefetch=1,
            grid=(N_CHIPS,),
            in_specs=[
                pl.BlockSpec((None, None) + a.shape[2:], lambda b, core_ref: (0, 0, 0, 0))
                if a.shape[0] == 1
                else pl.BlockSpec((None, None) + a.shape[2:], lambda b, core_ref: (b, core_ref[0], 0, 0))
                for a in mine
            ]
            + [block(a) for a in theirs]
            + carried["in_specs"],
            out_specs=[block(a) for a in theirs] + carried["out_specs"],
            scratch_shapes=carried["scratch_shapes"],
        ),
        out_shape=[jax.ShapeDtypeStruct(a.shape, a.dtype) for a in theirs] + carried["out_shape"],
        compiler_params=_params("arbitrary"),
    )(core, *mine, *theirs, *carried["operands"])
    return res[:n], res[n:]


def _weight_grad(a, b, n_blk, blocked, name, groups):
    s = a.shape[0]
    t = min(TILE_WEIGHT_GRAD, s)
    n_t = s // t
    n_pairs = n_blk // 2
    if blocked == "cols":
        k, c = a.shape[1], b.shape[1] // n_blk
        a_spec = pl.BlockSpec((t, k), lambda j, i: (i, 0))
        b_spec = pl.BlockSpec((t, 2 * c), lambda j, i: (i, j))
        acc_shape = (k, 2 * c)
    else:
        k, c = a.shape[1] // n_blk, b.shape[1]
        a_spec = pl.BlockSpec((t, 2 * k), lambda j, i: (i, j))
        b_spec = pl.BlockSpec((t, c), lambda j, i: (i, 0))
        acc_shape = (2 * k, c)
    carried = _carry_specs(groups)
    n_p = carried["n"]

    def body(a_ref, b_ref, *refs):
        src = refs[:n_p]
        o_ref, theirs_ref = refs[n_p : n_p + 2]
        landed = refs[n_p + 2 : 2 * n_p + 2]
        acc_ref, sbuf, pair_send, pair_recv = refs[2 * n_p + 2 : 2 * n_p + 6]
        start, finish = _carry(groups, src, landed, refs[2 * n_p + 6 :])
        j = pl.program_id(0)
        i = pl.program_id(1)
        x, y, c_me = _position()

        def to_sibling(pair):
            return pltpu.make_async_remote_copy(
                src_ref=sbuf.at[1 - c_me], dst_ref=theirs_ref.at[pair], send_sem=pair_send.at[pair],
                recv_sem=pair_recv.at[pair], device_id=(x, y, 1 - c_me), device_id_type=MESH,
            )

        @pl.when(jnp.logical_and(j == 0, i == 0))
        def _():
            start()

        @pl.when(i == 0)
        def _():
            acc_ref[...] = jnp.zeros_like(acc_ref)

        acc_ref[...] += _dot(a_ref[...], b_ref[...], TN)

        @pl.when(i == n_t - 1)
        def _():
            for pair in range(1, n_pairs):

                @pl.when(j == pair)
                def _():
                    to_sibling(pair - 1).wait_send()

            for half in range(2):
                if blocked == "cols":
                    block = acc_ref[:, half * c : (half + 1) * c].astype(BF16)
                else:
                    block = acc_ref[half * k : (half + 1) * k, :].astype(BF16)
                o_ref[half] = block
                sbuf[half] = block
            for pair in range(n_pairs):

                @pl.when(j == pair)
                def _():
                    to_sibling(pair).start()

        @pl.when(jnp.logical_and(j == n_pairs - 1, i == n_t - 1))
        def _():
            to_sibling(n_pairs - 1).wait_send()
            for pair in range(n_pairs):
                to_sibling(pair).wait_recv()
            finish()

    res = pl.pallas_call(
        body,
        name=name,
        grid=(n_pairs, n_t),
        in_specs=[a_spec, b_spec] + carried["in_specs"],
        out_specs=[pl.BlockSpec((2, k, c), lambda j, i: (j, 0, 0)), ANY] + carried["out_specs"],
        out_shape=[jax.ShapeDtypeStruct((n_blk, k, c), BF16), jax.ShapeDtypeStruct((n_pairs, k, c), BF16)]
        + carried["out_shape"],
        scratch_shapes=[
            pltpu.VMEM(acc_shape, F32), pltpu.VMEM((2, k, c), BF16), pltpu.SemaphoreType.DMA((n_pairs,)),
            pltpu.SemaphoreType.DMA((n_pairs,)),
        ]
        + carried["scratch_shapes"],
        compiler_params=_params("arbitrary", "arbitrary", vmem_limit_bytes=VMEM_LIMIT_MIX_BYTES),
    )(a, b, *carried["operands"])
    return res[0], res[1], res[2:]


def _input_grad(dproj, win_t, x, dxo, norm_pre, groups):
    s, d = x.shape
    t = min(TILE_GRAD, s)
    n_t = s // t
    kb = 2 * WIN_BLK
    n_k = win_t.shape[0] // kb
    carried = _carry_specs(groups)
    n_p = carried["n"]

    def body(dp_ref, w_ref, x_ref, dxo_ref, g_ref, *refs):
        src = refs[:n_p]
        gx_ref, dg_all = refs[n_p : n_p + 2]
        landed = refs[n_p + 2 : 2 * n_p + 2]
        acc_ref, dg_ref, dg_send, dg_recv, dg_local = refs[2 * n_p + 2 : 2 * n_p + 7]
        start, finish = _carry(groups, src, landed, refs[2 * n_p + 7 :])
        i = pl.program_id(0)
        j = pl.program_id(1)
        px, py, pc = _position()
        me = 4 * px + 2 * py + pc

        def dg_copies():
            far = [
                pltpu.make_async_remote_copy(
                    src_ref=dg_ref, dst_ref=dg_all.at[me], send_sem=dg_send.at[m], recv_sem=dg_recv.at[m],
                    device_id=(px ^ ((m >> 2) & 1), py ^ ((m >> 1) & 1), pc ^ (m & 1)), device_id_type=MESH,
                )
                for m in range(1, N_DEV)
            ]
            return far, pltpu.make_async_copy(dg_ref, dg_all.at[me], dg_local)

        @pl.when(jnp.logical_and(i == 0, j == 0))
        def _():
            start()
            dg_ref[...] = jnp.zeros_like(dg_ref)

        @pl.when(j == 0)
        def _():
            acc_ref[...] = jnp.zeros_like(acc_ref)

        acc_ref[...] += _dot(dp_ref[...], w_ref[...])

        @pl.when(j == n_k - 1)
        def _():
            xv = x_ref[...]
            gain = g_ref[...]
            r = lax.rsqrt(_rowmean(xv * xv) + EPS)
            dgain, dx = _rms_bwd(acc_ref[...], gain, r, xv * r)
            dg_ref[...] += dgain
            gx_ref[...] = dxo_ref[...] + dx

        @pl.when(jnp.logical_and(i == n_t - 1, j == n_k - 1))
        def _():
            far, near = dg_copies()
            for cp in [near] + far:
                cp.start()
            finish()
            for cp in far:
                cp.wait_recv()
            for cp in far:
                cp.wait_send()
            near.wait()

    res = pl.pallas_call(
        body,
        name="input_grad",
        grid=(n_t, n_k),
        in_specs=[
            pl.BlockSpec((t, kb), lambda i, j: (i, j)),
            pl.BlockSpec((kb, d), lambda i, j: (j, 0)),
            pl.BlockSpec((t, d), lambda i, j: (i, 0)),
            pl.BlockSpec((t, d), lambda i, j: (i, 0)),
            _full((1, d)),
        ]
        + carried["in_specs"],
        out_specs=[pl.BlockSpec((t, d), lambda i, j: (i, 0)), ANY] + carried["out_specs"],
        out_shape=[jax.ShapeDtypeStruct((s, d), F32), jax.ShapeDtypeStruct((N_DEV, 1, d), F32)] + carried["out_shape"],
        scratch_shapes=[
            pltpu.VMEM((t, d), F32), pltpu.VMEM((1, d), F32), pltpu.SemaphoreType.DMA((N_DEV,)),
            pltpu.SemaphoreType.DMA((N_DEV,)), pltpu.SemaphoreType.DMA,
        ]
        + carried["scratch_shapes"],
        compiler_params=_params("arbitrary", "arbitrary", vmem_limit_bytes=VMEM_LIMIT_MIX_BYTES),
    )(dproj, win_t, x, dxo, norm_pre, *carried["operands"])
    return res[0], res[1], res[2:]


def _kv_backward(dkv, memn, wkv, mem):
    m, d = mem.shape
    n = wkv.shape[1]

    def body(dkv_ref, memn_ref, w_ref, mem_ref, gw_ref, dg_ref):
        dkv_bf = dkv_ref[...].astype(BF16)
        gw_ref[...] = _dot(memn_ref[...], dkv_bf, TN).astype(BF16).reshape(N_DEV, WKV_BLK, n)
        dmemn = _dot(dkv_bf, w_ref[...], NT)
        mv = mem_ref[...]
        r = lax.rsqrt(_rowmean(mv * mv) + EPS)
        dg_ref[...] = _colsum(dmemn * (mv * r))

    return pl.pallas_call(
        body,
        name="kv_backward",
        grid=(1,),
        in_specs=[_full((m, n)), _full((m, d)), _full(wkv.shape), _full((m, d))],
        out_specs=[_full((N_DEV, WKV_BLK, n)), _full((1, d))],
        out_shape=[jax.ShapeDtypeStruct((N_DEV, WKV_BLK, n), BF16), jax.ShapeDtypeStruct((1, d), F32)],
        compiler_params=_params("arbitrary"),
    )(dkv, memn, wkv, mem)


def _adamw_math(w, g, m, v):
    m = ADAM_B1 * m + (1.0 - ADAM_B1) * g
    v = ADAM_B2 * v + (1.0 - ADAM_B2) * (g * g)
    m_hat = m / (1.0 - ADAM_B1**ADAM_STEP)
    v_hat = v / (1.0 - ADAM_B2**ADAM_STEP)
    delta = -ADAM_LR * (m_hat / (jnp.sqrt(v_hat) + ADAM_EPS) + ADAM_WD * w)
    return delta, m, v


def _adamw(parts, w, m, v, name):
    r, c = w.shape
    slots = parts.shape[0]
    t = r
    while t * c * 4 > TILE_ADAM_BYTES and t % 16 == 0:
        t //= 2

    def body(p_ref, w_ref, m_ref, v_ref, g_ref, d_ref, nm_ref, nv_ref):
        g = p_ref[0].astype(F32)
        for k in range(1, slots):
            g = g + p_ref[k].astype(F32)
        delta, nm, nv = _adamw_math(w_ref[...], g, m_ref[...], v_ref[...])
        g_ref[...] = g
        d_ref[...] = delta
        nm_ref[...] = nm
        nv_ref[...] = nv

    tile = pl.BlockSpec((t, c), lambda i: (i, 0))
    return pl.pallas_call(
        body,
        name=name,
        grid=(r // t,),
        in_specs=[pl.BlockSpec((slots, t, c), lambda i: (0, i, 0)), tile, tile, tile],
        out_specs=[tile] * 4,
        out_shape=[jax.ShapeDtypeStruct((r, c), F32)] * 4,
        compiler_params=_params("parallel"),
    )(parts, w, m, v)


def _adamw_packed(parts, triples, name):
    slots = parts.shape[0]
    sizes = [w.shape[0] for w, _, _ in triples]
    rest = parts.shape[1] - sum(sizes)

    def total(p_ref, at, rows):
        g = p_ref[0, at : at + rows, :]
        for k in range(1, slots):
            g = g + p_ref[k, at : at + rows, :]
        return g

    def body(p_ref, *refs):
        ins = refs[: 3 * len(triples)]
        outs = refs[3 * len(triples) :]
        at = 0
        for n, rows in enumerate(sizes):
            g = total(p_ref, at, rows)
            w_ref, m_ref, v_ref = ins[3 * n : 3 * n + 3]
            delta, nm, nv = _adamw_math(w_ref[...], g, m_ref[...], v_ref[...])
            for ref, val in zip(outs[4 * n : 4 * n + 4], (g, delta, nm, nv)):
                ref[...] = val
            at += rows
        if rest:
            outs[-1][...] = total(p_ref, at, rest)

    flat = [a for t in triples for a in t]
    res = pl.pallas_call(
        body,
        name=name,
        out_shape=[jax.ShapeDtypeStruct(w.shape, F32) for w, _, _ in triples for _ in range(4)]
        + ([jax.ShapeDtypeStruct((rest, 128), F32)] if rest else []),
        compiler_params=pltpu.CompilerParams(vmem_limit_bytes=VMEM_LIMIT_BYTES),
    )(parts, *flat)
    return [res[4 * n : 4 * n + 4] for n in range(len(triples))], (res[-1] if rest else None)


SMALL = ("norm_pre", "pool_scale", "sgu_ln_g", "sgu_ln_b", "sgu_w", "sgu_b", "mem_norm", "branch_norm", "norm_post")


def _local_view(name, w):
    if name == "sgu_w":
        return w.reshape(SGU_HEADS, SGU_CHUNK, SGU_CHUNK)
    if name == "sgu_b":
        return w.reshape(SGU_HEADS, SGU_CHUNK)
    return w.reshape(1, -1)


def _forward_backward(x, mem, target, shards, small):
    causal = jnp.tril(jnp.ones((SGU_CHUNK, SGU_CHUNK), dtype=bool))
    sgu_wm = jnp.where(causal[None], small["sgu_w"], 0.0).astype(BF16)
    sgu_bias = jnp.repeat(jnp.transpose(small["sgu_b"]), SGU_HEAD_DIM, axis=1)

    proj, h, (win, wkv, pool_all, wout) = _proj_gather(x, small["norm_pre"], shards)
    wout = wout.reshape(MIX_WIDTH, D_MODEL)
    wkv = wkv.reshape(D_MODEL, 2 * BRANCH)
    pool_full = (
        pool_all.reshape(N_DEV, len(POOL_WINDOWS), POOL_BLK, POOL_GROUP_DIM)
        .transpose(1, 0, 2, 3)
        .reshape(len(POOL_WINDOWS), POOL_GROUP_DIM, POOL_GROUP_DIM)
    )
    memn, kv, kv_t = _kv_forward(mem, small["mem_norm"], wkv)
    (y, dout, dxo, dproj, loss, d_norm_post, d_branch_norm, d_pool_scale, d_ln_g, d_ln_b, d_pool_w, d_sgu_w, d_sgu_b,
     dkv) = _mix(
        proj, x, target, kv, kv_t, wout, wout.T, pool_full, jnp.swapaxes(pool_full, 1, 2), small["pool_scale"],
        small["sgu_ln_g"], small["sgu_ln_b"], sgu_bias, sgu_wm, jnp.swapaxes(sgu_wm, 1, 2), small["branch_norm"],
        small["norm_post"],
    )
    g_wkv, d_mem_norm = _kv_backward(dkv, memn, wkv, mem)
    g_pool = (
        d_pool_w.reshape(len(POOL_WINDOWS), N_DEV, POOL_BLK, POOL_GROUP_DIM)
        .transpose(1, 0, 2, 3)
        .reshape(N_DEV, len(POOL_WINDOWS) * POOL_BLK, POOL_GROUP_DIM)
        .astype(BF16)
    )
    small_grads = dict(
        pool_scale=d_pool_scale, sgu_ln_g=d_ln_g, sgu_ln_b=d_ln_b, sgu_w=d_sgu_w, sgu_b=d_sgu_b,
        mem_norm=d_mem_norm, branch_norm=d_branch_norm, norm_post=d_norm_post,
    )
    packed = jnp.concatenate(
        [small_grads[n].reshape(-1, LANES) for n in SMALL if n != "norm_pre"]
        + [jnp.broadcast_to(loss, (SUBLANES, LANES))],
        axis=0,
    )
    packed = packed[None, None]

    by_chip = lambda g: g.reshape((N_CHIPS, 2) + g.shape[1:])
    small_mine = [by_chip(g_wkv), by_chip(g_pool), packed]
    g_wout, wout_theirs, small_theirs = _weight_grad(y, dout, N_DEV, "rows", "grad_w_out", [("pair", small_mine)])
    sums, _ = _pair_sum(small_mine + [by_chip(g_wout)], list(small_theirs) + [wout_theirs], "pair_sum_first")
    g_win, win_theirs, (l_wkv, l_pool, l_packed, l_wout) = _weight_grad(
        h, dproj, N_DEV, "cols", "grad_w_in", [("chips", list(sums))]
    )
    (win_sums,), _ = _pair_sum([by_chip(g_win)], [win_theirs], "pair_sum_w_in")
    win_late, _ = lax.optimization_barrier((win, g_wkv))
    grad_x, d_norm_pre, (l_win,) = _input_grad(
        dproj, jnp.swapaxes(win_late, 1, 2).reshape(IN_WIDTH, D_MODEL), x, dxo, small["norm_pre"],
        [("chips", [win_sums])],
    )
    return grad_x, dict(w_in=l_win, w_out=l_wout, w_kv=l_wkv, pool_w=l_pool), l_packed, d_norm_pre


def kernel(x, mem, norm_pre, w_in, pool_w, pool_scale, sgu_ln_g, sgu_ln_b, sgu_w, sgu_b, mem_norm, w_kv, branch_norm, w_out, norm_post, loss_target, m_norm_pre, m_w_in, m_pool_w, m_pool_scale, m_sgu_ln_g, m_sgu_ln_b, m_sgu_w, m_sgu_b, m_mem_norm, m_w_kv, m_branch_norm, m_w_out, m_norm_post, v_norm_pre, v_w_in, v_pool_w, v_pool_scale, v_sgu_ln_g, v_sgu_ln_b, v_sgu_w, v_sgu_b, v_mem_norm, v_w_kv, v_branch_norm, v_w_out, v_norm_post):
    weights = dict(norm_pre=norm_pre, w_in=w_in, pool_w=pool_w, pool_scale=pool_scale, sgu_ln_g=sgu_ln_g, sgu_ln_b=sgu_ln_b, sgu_w=sgu_w, sgu_b=sgu_b, mem_norm=mem_norm, w_kv=w_kv, branch_norm=branch_norm, w_out=w_out, norm_post=norm_post)
    first = dict(norm_pre=m_norm_pre, w_in=m_w_in, pool_w=m_pool_w, pool_scale=m_pool_scale, sgu_ln_g=m_sgu_ln_g, sgu_ln_b=m_sgu_ln_b, sgu_w=m_sgu_w, sgu_b=m_sgu_b, mem_norm=m_mem_norm, w_kv=m_w_kv, branch_norm=m_branch_norm, w_out=m_w_out, norm_post=m_norm_post)
    second = dict(norm_pre=v_norm_pre, w_in=v_w_in, pool_w=v_pool_w, pool_scale=v_pool_scale, sgu_ln_g=v_sgu_ln_g, sgu_ln_b=v_sgu_ln_b, sgu_w=v_sgu_w, sgu_b=v_sgu_b, mem_norm=v_mem_norm, w_kv=v_w_kv, branch_norm=v_branch_norm, w_out=v_w_out, norm_post=v_norm_post)
    order = ("norm_pre", "w_in", "pool_w", "pool_scale", "sgu_ln_g", "sgu_ln_b", "sgu_w", "sgu_b", "mem_norm", "w_kv", "branch_norm", "w_out", "norm_post")

    owned_shape = dict(
        w_in=(D_MODEL, WIN_BLK), w_out=(WOUT_BLK, D_MODEL), w_kv=(WKV_BLK, 2 * BRANCH),
        pool_w=(len(POOL_WINDOWS) * POOL_BLK, POOL_GROUP_DIM),
    )
    owned = {n: weights[n].reshape(owned_shape[n]) for n in owned_shape}
    small = {n: _local_view(n, weights[n]) for n in SMALL}
    grad_x, landed, landed_packed, d_norm_pre = _forward_backward(
        x[0], mem[0], loss_target[0],
        [owned["w_in"].astype(BF16)] + [owned[n] for n in ("w_kv", "pool_w", "w_out")], small,
    )
    landed_norm_pre = d_norm_pre.reshape(N_DEV, -1, LANES)

    grads, deltas, new_m, new_v = {}, {}, {}, {}
    for n in owned_shape:
        shape = weights[n].shape
        res = _adamw(
            landed[n], owned[n], first[n].reshape(owned_shape[n]), second[n].reshape(owned_shape[n]), "adamw_" + n
        )
        grads[n], deltas[n], new_m[n], new_v[n] = (a.reshape(shape) for a in res)
    rows_of = lambda tree, n: tree[n].reshape(-1, 128)
    for names, parts, name in (
        ([n for n in SMALL if n != "norm_pre"], landed_packed, "adamw_replicated"),
        (["norm_pre"], landed_norm_pre, "adamw_norm_pre"),
    ):
        res, rest = _adamw_packed(
            parts, [(rows_of(weights, n), rows_of(first, n), rows_of(second, n)) for n in names], name
        )
        if rest is not None:
            total = rest[0, 0]
        for n, four in zip(names, res):
            for tree, a in zip((grads, deltas, new_m, new_v), four):
                tree[n] = a.reshape(weights[n].shape)

    return (
        total,
        grad_x[None],
        *[grads[n] for n in order],
        *[deltas[n] for n in order],
        *[new_m[n] for n in order],
        *[new_v[n] for n in order],
    )
```

```python
import jax
import jax.numpy as jnp
from jax import lax
from jax.experimental import pallas as pl
from jax.experimental.pallas import tpu as pltpu

F32 = jnp.float32
BF16 = jnp.bfloat16
EPS = 1e-6

D_MODEL = 2048
POOL_WINDOWS = (2, 4, 8, 16)
POOL_GROUP_DIM = 256
BRANCH = 1024
SGU_CHUNK = 128
SGU_HEADS = 8
SGU_HEAD_DIM = 128
XATTN_HEADS = 4
XATTN_HEAD_DIM = 256
MIX_WIDTH = 3 * BRANCH
IN_WIDTH = 7 * BRANCH
N_DEV = 8
WIN_BLK = IN_WIDTH // N_DEV
WOUT_BLK = MIX_WIDTH // N_DEV
WKV_BLK = D_MODEL // N_DEV
POOL_BLK = POOL_GROUP_DIM // N_DEV
HALO = 16
LANES = 128
SUBLANES = 8

ADAM_LR = 0.001
ADAM_B1 = 0.9
ADAM_B2 = 0.999
ADAM_EPS = 1e-08
ADAM_WD = 0.01
ADAM_STEP = 10

VMEM_LIMIT_BYTES = 56 * 1024 * 1024
VMEM_LIMIT_MIX_BYTES = 63 * 1024 * 1024

TILE_PROJ = 512
TILE_MIX = 128
TILE_GRAD = 512
TILE_WEIGHT_GRAD = 1024
TILE_ADAM_BYTES = 1 << 20

ANY = pl.BlockSpec(memory_space=pl.ANY)
NN = (((1,), (0,)), ((), ()))
NT = (((1,), (1,)), ((), ()))
TN = (((0,), (0,)), ((), ()))
MESH = pl.DeviceIdType.MESH


def _dot(a, b, dims=NN):
    return lax.dot_general(a, b, dims, preferred_element_type=F32)


def _params(*semantics, vmem_limit_bytes=VMEM_LIMIT_BYTES):
    return pltpu.CompilerParams(dimension_semantics=semantics, vmem_limit_bytes=vmem_limit_bytes)


def _rowmean(a):
    return jnp.mean(a, axis=-1, keepdims=True)


def _colsum(a):
    return jnp.sum(a, axis=0, keepdims=True)


def _full(shape):
    zeros = (0,) * len(shape)
    return pl.BlockSpec(shape, lambda *_: zeros)


def _resident(shape):
    zeros = (0,) * len(shape)
    return pl.BlockSpec(shape, lambda *_: zeros, pipeline_mode=pl.Buffered(1))


def _proj_gather(x_in, norm_pre, shards, mem, mem_norm):
    s, d = x_in.shape
    t = min(TILE_PROJ, s)
    n_t = s // t
    n_arr = len(shards)

    def places(x, y, c):
        return (x, y, c), (x, y, 1 - c), (x ^ c, y ^ (1 - c)), (x ^ (1 - c), y ^ c), (1 - x, 1 - y)

    def index(chip, core):
        return 4 * chip[0] + 2 * chip[1] + core

    _, _, chip_a, chip_b, chip_d = places(*_position())
    c_out = lax.axis_index("c")
    me_out = index((lax.axis_index("x"), lax.axis_index("y")), c_out)
    order = jnp.stack(
        [
            me_out, me_out ^ 1, index(chip_a, c_out), index(chip_b, c_out), index(chip_b, 1 - c_out),
            index(chip_a, 1 - c_out), index(chip_d, c_out), index(chip_d, 1 - c_out),
        ]
    ).astype(jnp.int32)

    def body(order_ref, x_ref, g_ref, *refs):
        del order_ref
        refs = list(refs)
        take = lambda count: [refs.pop(0) for _ in range(count)]
        raw = take(n_arr)
        mem_hbm, mem_gain_ref, proj_ref, h_ref = take(4)
        out = take(n_arr)
        staged = take(n_arr - 1)
        memn_out, kv_out, kvt_out = take(3)
        src = (raw[0],) + tuple(staged)
        wbuf, hs, send_sems, recv_sems, local_sems, load_sems = take(6)
        wide = take(n_arr - 1)
        narrow = take(n_arr - 1)
        wkv_buf, mem_buf, memn_buf, kv_buf, kvt_buf = take(5)
        j = pl.program_id(0)
        i = pl.program_id(1)
        me, sibling, chip_a, chip_b, chip_d = places(*_position())
        c = me[2]

        def block(a, chip, core):
            return out[a].at[index(chip, core)]

        def copy(a, k, owner, to, from_input=False):
            return pltpu.make_async_remote_copy(
                src_ref=src[a] if from_input else block(a, *owner),
                dst_ref=block(a, *owner),
                send_sem=send_sems.at[a, k],
                recv_sem=recv_sems.at[a, k],
                device_id=to,
                device_id_type=MESH,
            )

        mine = (me[:2], c)

        def own(a):
            return pltpu.make_async_copy(src[a], block(a, *mine), local_sems.at[a])

        def first_sends(a):
            return [
                copy(a, 0, mine, sibling, from_input=True),
                copy(a, 1, mine, (*chip_a, c), from_input=True),
                copy(a, 2, mine, (*chip_b, c), from_input=True),
            ]

        def onward(a, k):
            owner = {3: chip_a, 4: chip_a, 5: chip_b, 6: chip_d}[k]
            return copy(a, k, (owner, c), (*chip_b, c) if k == 3 else sibling)

        def landed(a, k):
            owner = {0: mine[0], 1: chip_a, 2: chip_b, 3: chip_d, 4: chip_b, 5: chip_a, 6: chip_d}[k]
            core = c if k in (1, 2, 3) else 1 - c
            copy(a, k, (owner, core), me).wait_recv()
            return owner, core

        def load(ref, step):
            return pltpu.make_async_copy(ref, wbuf.at[step % 2], load_sems.at[step % 2])

        @pl.when(jnp.logical_and(j == 0, i == 0))
        def _():
            own(0).start()
            for cp in first_sends(0):
                cp.start()
            for a in range(1, n_arr):
                pltpu.sync_copy(raw[a], wide[a - 1])
                narrow[a - 1][...] = wide[a - 1][...].astype(BF16)
                pltpu.sync_copy(narrow[a - 1], staged[a - 1])
            load(src[0], 0).start()
            load(src[0], 0).wait()

        steps = {1: (0, ()), 2: (1, (3, 4)), 3: (2, (5,)), 4: (4, ()), 5: (5, ()), 6: (3, (6,)), 7: (6, ())}
        for step, (k, then) in steps.items():

            @pl.when(jnp.logical_and(j == step, i == 0))
            def _():
                load(src[0], step).wait()

            @pl.when(jnp.logical_and(j == step - 1, i == n_t - 1))
            def _():
                owner = landed(0, k)
                for k2 in then:
                    onward(0, k2).start()
                if k == 1:
                    for a in range(1, n_arr):
                        own(a).start()
                        for cp in first_sends(a):
                            cp.start()
                if k == 3:
                    for a in range(1, n_arr):
                        for k1, then1 in ((1, (3, 4)), (2, (5,))):
                            landed(a, k1)
                            for k2 in then1:
                                onward(a, k2).start()
                load(block(0, *owner), step).start()

        @pl.when(j == 0)
        def _():
            xv = x_ref[...]
            h = (xv * lax.rsqrt(_rowmean(xv * xv) + EPS) * g_ref[...]).astype(BF16)
            hs[i] = h
            h_ref[...] = h

        proj_ref[...] = _dot(hs[i], wbuf[j % 2]).astype(BF16)

        @pl.when(jnp.logical_and(j == N_DEV - 1, i == n_t - 1))
        def _():
            for a in range(1, n_arr):
                landed(a, 3)
                onward(a, 6).start()
            for a in range(1, n_arr):
                for k in (0, 4, 5, 6):
                    landed(a, k)
            for a in range(n_arr):
                for cp in first_sends(a) + [onward(a, k) for k in (3, 4, 5, 6)]:
                    cp.wait_send()
                own(a).wait()
            pltpu.sync_copy(out[1], wkv_buf)
            pltpu.sync_copy(mem_hbm, mem_buf)
            mv = mem_buf[...]
            memn = (mv * lax.rsqrt(_rowmean(mv * mv) + EPS) * mem_gain_ref[...]).astype(BF16)
            kv = _dot(memn, wkv_buf[...].reshape(wkv_buf.shape[0] * wkv_buf.shape[1], wkv_buf.shape[2]))
            memn_buf[...] = memn
            kv_buf[...] = kv.astype(BF16)
            kvt_buf[...] = kv.T.astype(BF16)
            for buf, res_ref in ((memn_buf, memn_out), (kv_buf, kv_out), (kvt_buf, kvt_out)):
                pltpu.sync_copy(buf, res_ref)

    m_len = mem.shape[0]
    kv_width = shards[1].shape[1]
    res = pl.pallas_call(
        body,
        name="proj_gather",
        grid_spec=pltpu.PrefetchScalarGridSpec(
            num_scalar_prefetch=1,
            grid=(N_DEV, n_t),
            in_specs=[
                pl.BlockSpec((t, d), lambda j, i, order_ref: (jnp.where(j == 0, i, n_t - 1), 0)),
                pl.BlockSpec((1, d), lambda j, i, order_ref: (0, 0)),
            ]
            + [ANY] * (n_arr + 1)
            + [pl.BlockSpec((1, d), lambda j, i, order_ref: (0, 0))],
            out_specs=[
                pl.BlockSpec((t, WIN_BLK), lambda j, i, order_ref: (i, order_ref[j])),
                pl.BlockSpec((t, d), lambda j, i, order_ref: (jnp.where(j == 0, i, n_t - 1), 0)),
            ]
            + [ANY] * (2 * n_arr + 2),
            scratch_shapes=[
                pltpu.VMEM((2,) + shards[0].shape, BF16),
                pltpu.VMEM((n_t, t, d), BF16),
                pltpu.SemaphoreType.DMA((n_arr, 7)),
                pltpu.SemaphoreType.DMA((n_arr, 7)),
                pltpu.SemaphoreType.DMA((n_arr,)),
                pltpu.SemaphoreType.DMA((2,)),
            ]
            + [pltpu.VMEM(a.shape, F32) for a in shards[1:]]
            + [pltpu.VMEM(a.shape, BF16) for a in shards[1:]]
            + [
                pltpu.VMEM((N_DEV,) + shards[1].shape, BF16), pltpu.VMEM((m_len, d), F32),
                pltpu.VMEM((m_len, d), BF16), pltpu.VMEM((m_len, kv_width), BF16), pltpu.VMEM((kv_width, m_len), BF16),
            ],
        ),
        out_shape=[jax.ShapeDtypeStruct((s, IN_WIDTH), BF16), jax.ShapeDtypeStruct((s, d), BF16)]
        + [jax.ShapeDtypeStruct((N_DEV,) + a.shape, BF16) for a in shards]
        + [jax.ShapeDtypeStruct(a.shape, BF16) for a in shards[1:]]
        + [
            jax.ShapeDtypeStruct((m_len, d), BF16), jax.ShapeDtypeStruct((m_len, kv_width), BF16),
            jax.ShapeDtypeStruct((kv_width, m_len), BF16),
        ],
        compiler_params=_params("arbitrary", "arbitrary", vmem_limit_bytes=VMEM_LIMIT_MIX_BYTES),
    )(order, x_in, norm_pre, *shards, mem, mem_norm)
    return res[0], res[1], res[2 : 2 + n_arr], res[-3:]


def _sigmoid(a):
    return jax.nn.sigmoid(a)


def _dsilu(a, sg):
    return sg * (1.0 + a * (1.0 - sg))


def _rms_fwd(u, gain):
    r = lax.rsqrt(_rowmean(u * u) + EPS)
    n = u * r
    return r, n, n * gain


def _rms_bwd(dy, gain, r, n):
    dn = dy * gain
    return _colsum(dy * n), r * (dn - n * _rowmean(dn * n))


def _mix(proj, x, target, kv, kv_t, wout, wout_t, pool_w, pool_w_t, pool_scale, ln_g, ln_b, sgu_bias, sgu_wm, sgu_wm_t, branch_norm, norm_post):
    s, d = x.shape
    t = min(TILE_MIX, s)
    n_tiles = s // t
    n_chunks = t // SGU_CHUNK
    halo_blocks_per_tile = t // HALO
    inv_d = 1.0 / d
    scale = 1.0 / (XATTN_HEAD_DIM**0.5)

    def body(
        proj_ref, halo_ref, x_ref, tgt_ref, kv_ref, kvt_ref, wout_hbm, wout_t_hbm, pw_ref, pwt_ref, pscale_ref, lng_ref,
        lnb_ref, bias_ref, wm_ref, wmt_ref, bnorm_ref, gpost_ref,
        y_ref, dout_ref, dxo_ref, dproj_ref, loss_ref, dgpost_ref, dbnorm_ref, dpscale_ref, dlng_ref, dlnb_ref,
        dpw_out, dwm_out, dbias_ref, dkv_out,
        carry_ref, dzsum_ref, dpw_ref, dwm_ref, dkv_ref, wout_ref, wout_t_ref, wout_sems,
    ):
        i = pl.program_id(0)
        tile = n_tiles - 1 - i
        wout_load = pltpu.make_async_copy(wout_hbm, wout_ref, wout_sems.at[0])
        wout_t_load = pltpu.make_async_copy(wout_t_hbm, wout_t_ref, wout_sems.at[1])

        @pl.when(i == 0)
        def _():
            wout_load.start()
            wout_t_load.start()
            carry_ref[...] = jnp.zeros_like(carry_ref)
            dzsum_ref[...] = jnp.zeros_like(dzsum_ref)
            for ref in (loss_ref, dgpost_ref, dbnorm_ref, dpscale_ref, dlng_ref, dlnb_ref, dpw_ref, dwm_ref, dkv_ref):
                ref[...] = jnp.zeros_like(ref)

        t_glob = tile * t + lax.broadcasted_iota(jnp.int32, (t, 1), 0)
        inv_cnt = [1.0 / jnp.minimum(t_glob + 1, w).astype(F32) for w in POOL_WINDOWS]

        xa = proj_ref[:, 0:BRANCH].astype(F32)
        ga = proj_ref[:, BRANCH : 2 * BRANCH].astype(F32)
        halo = jnp.where(tile == 0, 0.0, halo_ref[...].astype(F32))
        d_bf, pm_parts = [], []
        for g, w in enumerate(POOL_WINDOWS):
            cols = slice(g * POOL_GROUP_DIM, (g + 1) * POOL_GROUP_DIM)
            acc = jnp.concatenate([halo[:, cols], xa[:, cols]], axis=0)
            k = 1
            while k < w:
                acc = acc + pltpu.roll(acc, k, axis=0)
                k *= 2
            dg = (acc[HALO:, :] * inv_cnt[g] - xa[:, cols]).astype(BF16)
            d_bf.append(dg)
            pm_parts.append(_dot(dg, pw_ref[g]))
        pm = jnp.concatenate(pm_parts, axis=1)
        pscale = pscale_ref[...]
        pa = pm * pscale
        sga = _sigmoid(ga)
        sila = ga * sga
        ua = pa * sila
        g_a = bnorm_ref[:, 0:BRANCH]
        ra, na, ya = _rms_fwd(ua, g_a)

        u = proj_ref[:, 2 * BRANCH : 3 * BRANCH].astype(F32)
        v = proj_ref[:, 3 * BRANCH : 4 * BRANCH].astype(F32)
        gb = proj_ref[:, 4 * BRANCH : 5 * BRANCH].astype(F32)
        lng = lng_ref[...]
        vc = v - _rowmean(v)
        rstd = lax.rsqrt(_rowmean(vc * vc) + EPS)
        vhat = vc * rstd
        vn_bf = (vhat * lng + lnb_ref[...]).astype(BF16)
        z_rows = []
        for c in range(n_chunks):
            rows = slice(c * SGU_CHUNK, (c + 1) * SGU_CHUNK)
            z_rows.append(
                jnp.concatenate(
                    [
                        _dot(wm_ref[hd], vn_bf[rows, hd * SGU_HEAD_DIM : (hd + 1) * SGU_HEAD_DIM])
                        for hd in range(SGU_HEADS)
                    ],
                    axis=1,
                )
                + bias_ref[...]
            )
        z = z_rows[0] if n_chunks == 1 else jnp.concatenate(z_rows, axis=0)
        sb = u * z
        sgb = _sigmoid(gb)
        silb = gb * sgb
        ub = sb * silb
        g_b = bnorm_ref[:, BRANCH : 2 * BRANCH]
        rb, nb, yb = _rms_fwd(ub, g_b)

        q = proj_ref[:, 5 * BRANCH : 6 * BRANCH]
        gc = proj_ref[:, 6 * BRANCH : 7 * BRANCH].astype(F32)
        q_bf, p_bf, o_parts = [], [], []
        for hd in range(XATTN_HEADS):
            cols = slice(hd * XATTN_HEAD_DIM, (hd + 1) * XATTN_HEAD_DIM)
            qh = q[:, cols]
            sc = _dot(qh, kvt_ref[cols, :]) * scale
            e = jnp.exp(sc - jnp.max(sc, axis=-1, keepdims=True))
            p = e / jnp.sum(e, axis=-1, keepdims=True)
            q_bf.append(qh)
            p_bf.append(p.astype(BF16))
            o_parts.append(_dot(p_bf[hd], kv_ref[:, BRANCH + hd * XATTN_HEAD_DIM : BRANCH + (hd + 1) * XATTN_HEAD_DIM]))
        o = jnp.concatenate(o_parts, axis=1)
        sgc = _sigmoid(gc)
        silc = gc * sgc
        uc = o * silc
        g_c = bnorm_ref[:, 2 * BRANCH : 3 * BRANCH]
        rc, nc, yc = _rms_fwd(uc, g_c)

        @pl.when(i == 0)
        def _():
            wout_load.wait()

        out = None
        for b, y_branch in enumerate((ya, yb, yc)):
            rows = slice(b * BRANCH, (b + 1) * BRANCH)
            y_bf = y_branch.astype(BF16)
            y_ref[:, rows] = y_bf
            part = _dot(y_bf, wout_ref[rows, :])
            out = part if out is None else out + part
        gpost = gpost_ref[...]
        r_out = lax.rsqrt(_rowmean(out * out) + EPS)
        on = out * r_out
        err = x_ref[...] + on * gpost - tgt_ref[...]
        loss_ref[...] += 0.5 * jnp.sum(_rowmean(err * err), axis=0, keepdims=True)

        dxo = err * inv_d
        dxo_ref[...] = dxo
        dgp, dout = _rms_bwd(dxo, gpost, r_out, on)
        dgpost_ref[...] += dgp
        dout_bf = dout.astype(BF16)
        dout_ref[...] = dout_bf

        @pl.when(i == 0)
        def _():
            wout_t_load.wait()

        dy = [_dot(dout_bf, wout_t_ref[:, b * BRANCH : (b + 1) * BRANCH]) for b in range(3)]

        dg_a, dua = _rms_bwd(dy[0], g_a, ra, na)
        dg_b, dub = _rms_bwd(dy[1], g_b, rb, nb)
        dg_c, duc = _rms_bwd(dy[2], g_c, rc, nc)
        dbnorm_ref[...] += jnp.concatenate([dg_a, dg_b, dg_c], axis=1)

        dpa = dua * sila
        dga = dua * pa * _dsilu(ga, sga)
        dpscale_ref[...] += _colsum(dpa * pm)
        dpm = dpa * pscale
        dxa_parts, carry_parts = [], []
        for g, w in enumerate(POOL_WINDOWS):
            cols = slice(g * POOL_GROUP_DIM, (g + 1) * POOL_GROUP_DIM)
            dpm_g = dpm[:, cols].astype(BF16)
            dd = _dot(dpm_g, pwt_ref[g])
            dpw_ref[g] += _dot(d_bf[g], dpm_g, TN)
            cg = dd * inv_cnt[g]
            carry_parts.append(cg[0:HALO, :])
            acc = jnp.concatenate([cg, carry_ref[:, cols]], axis=0)
            k = 1
            while k < w:
                acc = acc + pltpu.roll(acc, t + HALO - k, axis=0)
                k *= 2
            dxa_parts.append(acc[0:t, :] - dd)
        carry_ref[...] = jnp.concatenate(carry_parts, axis=1)
        dxa = jnp.concatenate(dxa_parts, axis=1)

        dsb = dub * silb
        dgb = dub * sb * _dsilu(gb, sgb)
        du = dsb * z
        dz = dsb * u
        dz_bf = dz.astype(BF16)
        dvn_rows = []
        dz_sum = None
        for c in range(n_chunks):
            rows = slice(c * SGU_CHUNK, (c + 1) * SGU_CHUNK)
            dz_sum = dz[rows, :] if dz_sum is None else dz_sum + dz[rows, :]
            parts = []
            for hd in range(SGU_HEADS):
                cols = slice(hd * SGU_HEAD_DIM, (hd + 1) * SGU_HEAD_DIM)
                parts.append(_dot(wmt_ref[hd], dz_bf[rows, cols]))
                dwm_ref[hd] += _dot(dz_bf[rows, cols], vn_bf[rows, cols], NT)
            dvn_rows.append(jnp.concatenate(parts, axis=1))
        dzsum_ref[...] += dz_sum
        dvn = dvn_rows[0] if n_chunks == 1 else jnp.concatenate(dvn_rows, axis=0)
        dlng_ref[...] += _colsum(dvn * vhat)
        dlnb_ref[...] += _colsum(dvn)
        dvh = dvn * lng
        dv = rstd * (dvh - _rowmean(dvh) - vhat * _rowmean(dvh * vhat))

        do = duc * silc
        dgc = duc * o * _dsilu(gc, sgc)
        dq_parts = []
        for hd in range(XATTN_HEADS):
            cols = slice(hd * XATTN_HEAD_DIM, (hd + 1) * XATTN_HEAD_DIM)
            vcols = slice(BRANCH + hd * XATTN_HEAD_DIM, BRANCH + (hd + 1) * XATTN_HEAD_DIM)
            do_h = do[:, cols].astype(BF16)
            p = p_bf[hd].astype(F32)
            dp = _dot(do_h, kvt_ref[vcols, :])
            dkv_ref[:, vcols] += _dot(p_bf[hd], do_h, TN)
            ds_bf = (p * (dp - jnp.sum(dp * p, axis=-1, keepdims=True)) * scale).astype(BF16)
            dq_parts.append(_dot(ds_bf, kv_ref[:, cols]))
            dkv_ref[:, cols] += _dot(ds_bf, q_bf[hd], TN)
        dq = jnp.concatenate(dq_parts, axis=1)

        dproj_ref[...] = jnp.concatenate([dxa, dga, du, dv, dgb, dq, dgc], axis=1).astype(BF16)

        @pl.when(i == n_tiles - 1)
        def _():
            keep = lax.broadcasted_iota(jnp.int32, (SGU_CHUNK, SGU_CHUNK), 0) >= lax.broadcasted_iota(
                jnp.int32, (SGU_CHUNK, SGU_CHUNK), 1
            )
            for hd in range(SGU_HEADS):
                dwm_ref[hd] = jnp.where(keep, dwm_ref[hd], 0.0)
                per_pos = dzsum_ref[:, hd * SGU_HEAD_DIM : (hd + 1) * SGU_HEAD_DIM]
                dbias_ref[hd : hd + 1, :] = _colsum(per_pos.T)
            for acc, res in ((dpw_ref, dpw_out), (dwm_ref, dwm_out), (dkv_ref, dkv_out)):
                pltpu.sync_copy(acc, res)

    row_tile = lambda width: pl.BlockSpec((t, width), lambda i: (n_tiles - 1 - i, 0))
    halo_spec = pl.BlockSpec(
        (HALO, BRANCH), lambda i: (jnp.maximum((n_tiles - 1 - i) * halo_blocks_per_tile - 1, 0), 0)
    )
    acc_shapes = [
        (1, 128),
        (1, d),
        (1, MIX_WIDTH),
        (1, BRANCH),
        (1, BRANCH),
        (1, BRANCH),
        pool_w.shape,
        sgu_wm.shape,
        (SGU_HEADS, SGU_CHUNK),
        kv.shape,
    ]
    return pl.pallas_call(
        body,
        name="mix",
        grid=(n_tiles,),
        in_specs=[
            row_tile(IN_WIDTH), halo_spec, row_tile(d), row_tile(d), _resident(kv.shape), _resident(kv_t.shape),
            ANY, ANY, _resident(pool_w.shape), _resident(pool_w_t.shape),
            _full((1, BRANCH)), _full((1, BRANCH)), _full((1, BRANCH)), _resident((SGU_CHUNK, BRANCH)),
            _resident(sgu_wm.shape), _resident(sgu_wm_t.shape), _full((1, MIX_WIDTH)), _full((1, d)),
        ],
        out_specs=[row_tile(MIX_WIDTH), row_tile(d), row_tile(d), row_tile(IN_WIDTH)]
        + [ANY if len(a) == 3 or a == kv.shape else _full(a) for a in acc_shapes],
        out_shape=[
            jax.ShapeDtypeStruct((s, MIX_WIDTH), BF16),
            jax.ShapeDtypeStruct((s, d), BF16),
            jax.ShapeDtypeStruct((s, d), F32),
            jax.ShapeDtypeStruct((s, IN_WIDTH), BF16),
        ]
        + [jax.ShapeDtypeStruct(a, F32) for a in acc_shapes],
        scratch_shapes=[
            pltpu.VMEM((HALO, BRANCH), F32), pltpu.VMEM((SGU_CHUNK, BRANCH), F32), pltpu.VMEM(pool_w.shape, F32),
            pltpu.VMEM(sgu_wm.shape, F32), pltpu.VMEM(kv.shape, F32), pltpu.VMEM(wout.shape, BF16),
            pltpu.VMEM(wout_t.shape, BF16), pltpu.SemaphoreType.DMA((2,)),
        ],
        compiler_params=_params("arbitrary", vmem_limit_bytes=VMEM_LIMIT_MIX_BYTES),
    )(
        proj, proj, x, target, kv, kv_t, wout, wout_t, pool_w, pool_w_t, pool_scale, ln_g, ln_b, sgu_bias, sgu_wm,
        sgu_wm_t, branch_norm, norm_post,
    )


def _position():
    return lax.axis_index("x"), lax.axis_index("y"), lax.axis_index("c")


N_CHIPS = 4


def _landing_shape(kind, a):
    return (N_CHIPS,) + a.shape[2:] if kind == "pair" else a.shape


def _carry_specs(groups):
    arrays = [(kind, a) for kind, arrs in groups for a in arrs]
    scratch = []
    for _, arrs in groups:
        n = len(arrs)
        scratch += [pltpu.SemaphoreType.DMA((n, N_DEV)), pltpu.SemaphoreType.DMA((n, N_DEV)), pltpu.SemaphoreType.DMA((n,))]
    return dict(
        n=len(arrays),
        operands=[a for _, a in arrays],
        in_specs=[ANY] * len(arrays),
        out_specs=[ANY] * len(arrays),
        out_shape=[jax.ShapeDtypeStruct(_landing_shape(kind, a), a.dtype) for kind, a in arrays],
        scratch_shapes=scratch,
    )


def _carry(groups, src, out, sems):
    x, y, c = _position()
    chip = 2 * x + y

    def remote(s, d, send_sems, recv_sems, a, m, to):
        return pltpu.make_async_remote_copy(
            src_ref=s, dst_ref=d, send_sem=send_sems.at[a, m], recv_sem=recv_sems.at[a, m], device_id=to,
            device_id_type=MESH,
        )

    def copies():
        far, near = [], []
        at = 0
        for g, (kind, arrs) in enumerate(groups):
            send_sems, recv_sems, local_sems = sems[3 * g : 3 * g + 3]
            for a in range(len(arrs)):
                s, d = src[at + a], out[at + a]
                if kind == "pair" and s.shape[0] == 1:
                    for b in range(N_CHIPS):
                        far.append(remote(s.at[0, 0], d.at[b], send_sems, recv_sems, a, 1 + b, (x, y, 1 - c)))
                elif kind == "pair":
                    far.append(remote(s.at[:, 1 - c], d, send_sems, recv_sems, a, 1, (x, y, 1 - c)))
                else:
                    assert kind == "chips", kind
                    for m in range(1, N_CHIPS):
                        px, py = x ^ (m >> 1), y ^ (m & 1)
                        far.append(remote(s.at[2 * px + py], d.at[chip], send_sems, recv_sems, a, m, (px, py, c)))
                    near.append(pltpu.make_async_copy(s.at[chip], d.at[chip], local_sems.at[a]))
            at += len(arrs)
        return far, near

    def start():
        far, near = copies()
        for cp in near + far:
            cp.start()

    def finish():
        far, near = copies()
        for cp in far:
            cp.wait_recv()
        for cp in far:
            cp.wait_send()
        for cp in near:
            cp.wait()

    return start, finish


def _pair_sum(mine, theirs, name, groups=()):
    n = len(mine)
    carried = _carry_specs(groups)
    n_c = carried["n"]
    core = lax.axis_index("c").astype(jnp.int32).reshape(1)

    def body(core_ref, *refs):
        del core_ref
        own = refs[:n]
        sib = refs[n : 2 * n]
        src = refs[2 * n : 2 * n + n_c]
        out = refs[2 * n + n_c : 3 * n + n_c]
        landed = refs[3 * n + n_c : 3 * n + 2 * n_c]
        start, finish = _carry(groups, src, landed, refs[3 * n + 2 * n_c :])
        b = pl.program_id(0)

        @pl.when(b == 0)
        def _():
            start()

        for a in range(n):
            out[a][...] = (own[a][...].astype(F32) + sib[a][...].astype(F32)).astype(out[a].dtype)

        @pl.when(b == N_CHIPS - 1)
        def _():
            finish()

    block = lambda a: pl.BlockSpec((None,) + a.shape[1:], lambda b, core_ref: (b, 0, 0))
    res = pl.pallas_call(
        body,
        name=name,
        grid_spec=pltpu.PrefetchScalarGridSpec(
            num_scalar_prefetch=1,
            grid=(N_CHIPS,),
            in_specs=[
                pl.BlockSpec((None, None) + a.shape[2:], lambda b, core_ref: (0, 0, 0, 0))
                if a.shape[0] == 1
                else pl.BlockSpec((None, None) + a.shape[2:], lambda b, core_ref: (b, core_ref[0], 0, 0))
                for a in mine
            ]
            + [block(a) for a in theirs]
            + carried["in_specs"],
            out_specs=[block(a) for a in theirs] + carried["out_specs"],
            scratch_shapes=carried["scratch_shapes"],
        ),
        out_shape=[jax.ShapeDtypeStruct(a.shape, a.dtype) for a in theirs] + carried["out_shape"],
        compiler_params=_params("arbitrary"),
    )(core, *mine, *theirs, *carried["operands"])
    return res[:n], res[n:]


def _weight_grad(a, b, n_blk, blocked, name, groups):
    s = a.shape[0]
    t = min(TILE_WEIGHT_GRAD, s)
    n_t = s // t
    n_pairs = n_blk // 2
    if blocked == "cols":
        k, c = a.shape[1], b.shape[1] // n_blk
        a_spec = pl.BlockSpec((t, k), lambda j, i: (i, 0))
        b_spec = pl.BlockSpec((t, 2 * c), lambda j, i: (i, j))
        acc_shape = (k, 2 * c)
    else:
        k, c = a.shape[1] // n_blk, b.shape[1]
        a_spec = pl.BlockSpec((t, 2 * k), lambda j, i: (i, j))
        b_spec = pl.BlockSpec((t, c), lambda j, i: (i, 0))
        acc_shape = (2 * k, c)
    carried = _carry_specs(groups)
    n_p = carried["n"]

    def body(a_ref, b_ref, *refs):
        src = refs[:n_p]
        o_ref, theirs_ref = refs[n_p : n_p + 2]
        landed = refs[n_p + 2 : 2 * n_p + 2]
        acc_ref, sbuf, pair_send, pair_recv = refs[2 * n_p + 2 : 2 * n_p + 6]
        start, finish = _carry(groups, src, landed, refs[2 * n_p + 6 :])
        j = pl.program_id(0)
        i = pl.program_id(1)
        x, y, c_me = _position()

        def to_sibling(pair):
            return pltpu.make_async_remote_copy(
                src_ref=sbuf.at[1 - c_me], dst_ref=theirs_ref.at[pair], send_sem=pair_send.at[pair],
                recv_sem=pair_recv.at[pair], device_id=(x, y, 1 - c_me), device_id_type=MESH,
            )

        @pl.when(jnp.logical_and(j == 0, i == 0))
        def _():
            start()

        @pl.when(i == 0)
        def _():
            acc_ref[...] = jnp.zeros_like(acc_ref)

        acc_ref[...] += _dot(a_ref[...], b_ref[...], TN)

        @pl.when(i == n_t - 1)
        def _():
            for pair in range(1, n_pairs):

                @pl.when(j == pair)
                def _():
                    to_sibling(pair - 1).wait_send()

            for half in range(2):
                if blocked == "cols":
                    block = acc_ref[:, half * c : (half + 1) * c].astype(BF16)
                else:
                    block = acc_ref[half * k : (half + 1) * k, :].astype(BF16)
                o_ref[half] = block
                sbuf[half] = block
            for pair in range(n_pairs):

                @pl.when(j == pair)
                def _():
                    to_sibling(pair).start()

        @pl.when(jnp.logical_and(j == n_pairs - 1, i == n_t - 1))
        def _():
            to_sibling(n_pairs - 1).wait_send()
            for pair in range(n_pairs):
                to_sibling(pair).wait_recv()
            finish()

    res = pl.pallas_call(
        body,
        name=name,
        grid=(n_pairs, n_t),
        in_specs=[a_spec, b_spec] + carried["in_specs"],
        out_specs=[pl.BlockSpec((2, k, c), lambda j, i: (j, 0, 0)), ANY] + carried["out_specs"],
        out_shape=[jax.ShapeDtypeStruct((n_blk, k, c), BF16), jax.ShapeDtypeStruct((n_pairs, k, c), BF16)]
        + carried["out_shape"],
        scratch_shapes=[
            pltpu.VMEM(acc_shape, F32), pltpu.VMEM((2, k, c), BF16), pltpu.SemaphoreType.DMA((n_pairs,)),
            pltpu.SemaphoreType.DMA((n_pairs,)),
        ]
        + carried["scratch_shapes"],
        compiler_params=_params("arbitrary", "arbitrary", vmem_limit_bytes=VMEM_LIMIT_MIX_BYTES),
    )(a, b, *carried["operands"])
    return res[0], res[1], res[2:]


def _input_grad(dproj, win_t, x, dxo, norm_pre, groups):
    s, d = x.shape
    t = min(TILE_GRAD, s)
    n_t = s // t
    kb = 2 * WIN_BLK
    n_k = win_t.shape[0] // kb
    carried = _carry_specs(groups)
    n_p = carried["n"]

    def body(dp_ref, w_ref, x_ref, dxo_ref, g_ref, *refs):
        src = refs[:n_p]
        gx_ref, dg_all = refs[n_p : n_p + 2]
        landed = refs[n_p + 2 : 2 * n_p + 2]
        acc_ref, dg_ref, dg_send, dg_recv, dg_local = refs[2 * n_p + 2 : 2 * n_p + 7]
        start, finish = _carry(groups, src, landed, refs[2 * n_p + 7 :])
        i = pl.program_id(0)
        j = pl.program_id(1)
        px, py, pc = _position()
        me = 4 * px + 2 * py + pc

        def dg_copies():
            far = [
                pltpu.make_async_remote_copy(
                    src_ref=dg_ref, dst_ref=dg_all.at[me], send_sem=dg_send.at[m], recv_sem=dg_recv.at[m],
                    device_id=(px ^ ((m >> 2) & 1), py ^ ((m >> 1) & 1), pc ^ (m & 1)), device_id_type=MESH,
                )
                for m in range(1, N_DEV)
            ]
            return far, pltpu.make_async_copy(dg_ref, dg_all.at[me], dg_local)

        @pl.when(jnp.logical_and(i == 0, j == 0))
        def _():
            start()
            dg_ref[...] = jnp.zeros_like(dg_ref)

        @pl.when(j == 0)
        def _():
            acc_ref[...] = jnp.zeros_like(acc_ref)

        acc_ref[...] += _dot(dp_ref[...], w_ref[...])

        @pl.when(j == n_k - 1)
        def _():
            xv = x_ref[...]
            gain = g_ref[...]
            r = lax.rsqrt(_rowmean(xv * xv) + EPS)
            dgain, dx = _rms_bwd(acc_ref[...], gain, r, xv * r)
            dg_ref[...] += dgain
            gx_ref[...] = dxo_ref[...] + dx

        @pl.when(jnp.logical_and(i == n_t - 1, j == n_k - 1))
        def _():
            far, near = dg_copies()
            for cp in [near] + far:
                cp.start()
            finish()
            for cp in far:
                cp.wait_recv()
            for cp in far:
                cp.wait_send()
            near.wait()

    res = pl.pallas_call(
        body,
        name="input_grad",
        grid=(n_t, n_k),
        in_specs=[
            pl.BlockSpec((t, kb), lambda i, j: (i, j)),
            pl.BlockSpec((kb, d), lambda i, j: (j, 0)),
            pl.BlockSpec((t, d), lambda i, j: (i, 0)),
            pl.BlockSpec((t, d), lambda i, j: (i, 0)),
            _full((1, d)),
        ]
        + carried["in_specs"],
        out_specs=[pl.BlockSpec((t, d), lambda i, j: (i, 0)), ANY] + carried["out_specs"],
        out_shape=[jax.ShapeDtypeStruct((s, d), F32), jax.ShapeDtypeStruct((N_DEV, 1, d), F32)] + carried["out_shape"],
        scratch_shapes=[
            pltpu.VMEM((t, d), F32), pltpu.VMEM((1, d), F32), pltpu.SemaphoreType.DMA((N_DEV,)),
            pltpu.SemaphoreType.DMA((N_DEV,)), pltpu.SemaphoreType.DMA,
        ]
        + carried["scratch_shapes"],
        compiler_params=_params("arbitrary", "arbitrary", vmem_limit_bytes=VMEM_LIMIT_MIX_BYTES),
    )(dproj, win_t, x, dxo, norm_pre, *carried["operands"])
    return res[0], res[1], res[2:]


def _kv_backward(dkv, memn, wkv, mem):
    m, d = mem.shape
    n = wkv.shape[1]

    def body(dkv_ref, memn_ref, w_ref, mem_ref, gw_ref, dg_ref):
        dkv_bf = dkv_ref[...].astype(BF16)
        gw_ref[...] = _dot(memn_ref[...], dkv_bf, TN).astype(BF16).reshape(N_DEV, WKV_BLK, n)
        dmemn = _dot(dkv_bf, w_ref[...], NT)
        mv = mem_ref[...]
        r = lax.rsqrt(_rowmean(mv * mv) + EPS)
        dg_ref[...] = _colsum(dmemn * (mv * r))

    return pl.pallas_call(
        body,
        name="kv_backward",
        grid=(1,),
        in_specs=[_full((m, n)), _full((m, d)), _full(wkv.shape), _full((m, d))],
        out_specs=[_full((N_DEV, WKV_BLK, n)), _full((1, d))],
        out_shape=[jax.ShapeDtypeStruct((N_DEV, WKV_BLK, n), BF16), jax.ShapeDtypeStruct((1, d), F32)],
        compiler_params=_params("arbitrary"),
    )(dkv, memn, wkv, mem)


def _adamw_math(w, g, m, v):
    m = ADAM_B1 * m + (1.0 - ADAM_B1) * g
    v = ADAM_B2 * v + (1.0 - ADAM_B2) * (g * g)
    m_hat = m / (1.0 - ADAM_B1**ADAM_STEP)
    v_hat = v / (1.0 - ADAM_B2**ADAM_STEP)
    delta = -ADAM_LR * (m_hat / (jnp.sqrt(v_hat) + ADAM_EPS) + ADAM_WD * w)
    return delta, m, v


def _adamw(parts, w, m, v, name):
    r, c = w.shape
    slots = parts.shape[0]
    t = r
    while t * c * 4 > TILE_ADAM_BYTES and t % 16 == 0:
        t //= 2

    def body(p_ref, w_ref, m_ref, v_ref, g_ref, d_ref, nm_ref, nv_ref):
        g = p_ref[0].astype(F32)
        for k in range(1, slots):
            g = g + p_ref[k].astype(F32)
        delta, nm, nv = _adamw_math(w_ref[...], g, m_ref[...], v_ref[...])
        g_ref[...] = g
        d_ref[...] = delta
        nm_ref[...] = nm
        nv_ref[...] = nv

    tile = pl.BlockSpec((t, c), lambda i: (i, 0))
    return pl.pallas_call(
        body,
        name=name,
        grid=(r // t,),
        in_specs=[pl.BlockSpec((slots, t, c), lambda i: (0, i, 0)), tile, tile, tile],
        out_specs=[tile] * 4,
        out_shape=[jax.ShapeDtypeStruct((r, c), F32)] * 4,
        compiler_params=_params("parallel"),
    )(parts, w, m, v)


def _adamw_packed(parts, triples, name):
    slots = parts.shape[0]
    sizes = [w.shape[0] for w, _, _ in triples]
    rest = parts.shape[1] - sum(sizes)

    def total(p_ref, at, rows):
        g = p_ref[0, at : at + rows, :]
        for k in range(1, slots):
            g = g + p_ref[k, at : at + rows, :]
        return g

    def body(p_ref, *refs):
        ins = refs[: 3 * len(triples)]
        outs = refs[3 * len(triples) :]
        at = 0
        for n, rows in enumerate(sizes):
            g = total(p_ref, at, rows)
            w_ref, m_ref, v_ref = ins[3 * n : 3 * n + 3]
            delta, nm, nv = _adamw_math(w_ref[...], g, m_ref[...], v_ref[...])
            for ref, val in zip(outs[4 * n : 4 * n + 4], (g, delta, nm, nv)):
                ref[...] = val
            at += rows
        if rest:
            outs[-1][...] = total(p_ref, at, rest)

    flat = [a for t in triples for a in t]
    res = pl.pallas_call(
        body,
        name=name,
        out_shape=[jax.ShapeDtypeStruct(w.shape, F32) for w, _, _ in triples for _ in range(4)]
        + ([jax.ShapeDtypeStruct((rest, 128), F32)] if rest else []),
        compiler_params=pltpu.CompilerParams(vmem_limit_bytes=VMEM_LIMIT_BYTES),
    )(parts, *flat)
    return [res[4 * n : 4 * n + 4] for n in range(len(triples))], (res[-1] if rest else None)


SMALL = ("norm_pre", "pool_scale", "sgu_ln_g", "sgu_ln_b", "sgu_w", "sgu_b", "mem_norm", "branch_norm", "norm_post")


def _local_view(name, w):
    if name == "sgu_w":
        return w.reshape(SGU_HEADS, SGU_CHUNK, SGU_CHUNK)
    if name == "sgu_b":
        return w.reshape(SGU_HEADS, SGU_CHUNK)
    return w.reshape(1, -1)


def _forward_backward(x, mem, target, shards, small):
    causal = jnp.tril(jnp.ones((SGU_CHUNK, SGU_CHUNK), dtype=bool))
    sgu_wm = jnp.where(causal[None], small["sgu_w"], 0.0).astype(BF16)
    sgu_bias = jnp.repeat(jnp.transpose(small["sgu_b"]), SGU_HEAD_DIM, axis=1)

    proj, h, (win, wkv, pool_all, wout), (memn, kv, kv_t) = _proj_gather(
        x, small["norm_pre"], shards, mem, small["mem_norm"]
    )
    wout = wout.reshape(MIX_WIDTH, D_MODEL)
    wkv = wkv.reshape(D_MODEL, 2 * BRANCH)
    pool_full = (
        pool_all.reshape(N_DEV, len(POOL_WINDOWS), POOL_BLK, POOL_GROUP_DIM)
        .transpose(1, 0, 2, 3)
        .reshape(len(POOL_WINDOWS), POOL_GROUP_DIM, POOL_GROUP_DIM)
    )
    (y, dout, dxo, dproj, loss, d_norm_post, d_branch_norm, d_pool_scale, d_ln_g, d_ln_b, d_pool_w, d_sgu_w, d_sgu_b,
     dkv) = _mix(
        proj, x, target, kv, kv_t, wout, wout.T, pool_full, jnp.swapaxes(pool_full, 1, 2), small["pool_scale"],
        small["sgu_ln_g"], small["sgu_ln_b"], sgu_bias, sgu_wm, jnp.swapaxes(sgu_wm, 1, 2), small["branch_norm"],
        small["norm_post"],
    )
    g_wkv, d_mem_norm = _kv_backward(dkv, memn, wkv, mem)
    g_pool = (
        d_pool_w.reshape(len(POOL_WINDOWS), N_DEV, POOL_BLK, POOL_GROUP_DIM)
        .transpose(1, 0, 2, 3)
        .reshape(N_DEV, len(POOL_WINDOWS) * POOL_BLK, POOL_GROUP_DIM)
        .astype(BF16)
    )
    small_grads = dict(
        pool_scale=d_pool_scale, sgu_ln_g=d_ln_g, sgu_ln_b=d_ln_b, sgu_w=d_sgu_w, sgu_b=d_sgu_b,
        mem_norm=d_mem_norm, branch_norm=d_branch_norm, norm_post=d_norm_post,
    )
    packed = jnp.concatenate(
        [small_grads[n].reshape(-1, LANES) for n in SMALL if n != "norm_pre"]
        + [jnp.broadcast_to(loss, (SUBLANES, LANES))],
        axis=0,
    )
    packed = packed[None, None]

    by_chip = lambda g: g.reshape((N_CHIPS, 2) + g.shape[1:])
    small_mine = [by_chip(g_wkv), by_chip(g_pool), packed]
    g_wout, wout_theirs, small_theirs = _weight_grad(y, dout, N_DEV, "rows", "grad_w_out", [("pair", small_mine)])
    sums, _ = _pair_sum(small_mine + [by_chip(g_wout)], list(small_theirs) + [wout_theirs], "pair_sum_first")
    g_win, win_theirs, (l_wkv, l_pool, l_packed, l_wout) = _weight_grad(
        h, dproj, N_DEV, "cols", "grad_w_in", [("chips", list(sums))]
    )
    (win_sums,), _ = _pair_sum([by_chip(g_win)], [win_theirs], "pair_sum_w_in")
    win_late, _ = lax.optimization_barrier((win, g_wkv))
    grad_x, d_norm_pre, (l_win,) = _input_grad(
        dproj, jnp.swapaxes(win_late, 1, 2).reshape(IN_WIDTH, D_MODEL), x, dxo, small["norm_pre"],
        [("chips", [win_sums])],
    )
    return grad_x, dict(w_in=l_win, w_out=l_wout, w_kv=l_wkv, pool_w=l_pool), l_packed, d_norm_pre


def kernel(x, mem, norm_pre, w_in, pool_w, pool_scale, sgu_ln_g, sgu_ln_b, sgu_w, sgu_b, mem_norm, w_kv, branch_norm, w_out, norm_post, loss_target, m_norm_pre, m_w_in, m_pool_w, m_pool_scale, m_sgu_ln_g, m_sgu_ln_b, m_sgu_w, m_sgu_b, m_mem_norm, m_w_kv, m_branch_norm, m_w_out, m_norm_post, v_norm_pre, v_w_in, v_pool_w, v_pool_scale, v_sgu_ln_g, v_sgu_ln_b, v_sgu_w, v_sgu_b, v_mem_norm, v_w_kv, v_branch_norm, v_w_out, v_norm_post):
    weights = dict(norm_pre=norm_pre, w_in=w_in, pool_w=pool_w, pool_scale=pool_scale, sgu_ln_g=sgu_ln_g, sgu_ln_b=sgu_ln_b, sgu_w=sgu_w, sgu_b=sgu_b, mem_norm=mem_norm, w_kv=w_kv, branch_norm=branch_norm, w_out=w_out, norm_post=norm_post)
    first = dict(norm_pre=m_norm_pre, w_in=m_w_in, pool_w=m_pool_w, pool_scale=m_pool_scale, sgu_ln_g=m_sgu_ln_g, sgu_ln_b=m_sgu_ln_b, sgu_w=m_sgu_w, sgu_b=m_sgu_b, mem_norm=m_mem_norm, w_kv=m_w_kv, branch_norm=m_branch_norm, w_out=m_w_out, norm_post=m_norm_post)
    second = dict(norm_pre=v_norm_pre, w_in=v_w_in, pool_w=v_pool_w, pool_scale=v_pool_scale, sgu_ln_g=v_sgu_ln_g, sgu_ln_b=v_sgu_ln_b, sgu_w=v_sgu_w, sgu_b=v_sgu_b, mem_norm=v_mem_norm, w_kv=v_w_kv, branch_norm=v_branch_norm, w_out=v_w_out, norm_post=v_norm_post)
    order = ("norm_pre", "w_in", "pool_w", "pool_scale", "sgu_ln_g", "sgu_ln_b", "sgu_w", "sgu_b", "mem_norm", "w_kv", "branch_norm", "w_out", "norm_post")

    owned_shape = dict(
        w_in=(D_MODEL, WIN_BLK), w_out=(WOUT_BLK, D_MODEL), w_kv=(WKV_BLK, 2 * BRANCH),
        pool_w=(len(POOL_WINDOWS) * POOL_BLK, POOL_GROUP_DIM),
    )
    owned = {n: weights[n].reshape(owned_shape[n]) for n in owned_shape}
    small = {n: _local_view(n, weights[n]) for n in SMALL}
    grad_x, landed, landed_packed, d_norm_pre = _forward_backward(
        x[0], mem[0], loss_target[0],
        [owned["w_in"].astype(BF16)] + [owned[n] for n in ("w_kv", "pool_w", "w_out")], small,
    )
    landed_norm_pre = d_norm_pre.reshape(N_DEV, -1, LANES)

    grads, deltas, new_m, new_v = {}, {}, {}, {}
    for n in owned_shape:
        shape = weights[n].shape
        res = _adamw(
            landed[n], owned[n], first[n].reshape(owned_shape[n]), second[n].reshape(owned_shape[n]), "adamw_" + n
        )
        grads[n], deltas[n], new_m[n], new_v[n] = (a.reshape(shape) for a in res)
    rows_of = lambda tree, n: tree[n].reshape(-1, 128)
    for names, parts, name in (
        ([n for n in SMALL if n != "norm_pre"], landed_packed, "adamw_replicated"),
        (["norm_pre"], landed_norm_pre, "adamw_norm_pre"),
    ):
        res, rest = _adamw_packed(
            parts, [(rows_of(weights, n), rows_of(first, n), rows_of(second, n)) for n in names], name
        )
        if rest is not None:
            total = rest[0, 0]
        for n, four in zip(names, res):
            for tree, a in zip((grads, deltas, new_m, new_v), four):
                tree[n] = a.reshape(weights[n].shape)

    return (
        total,
        grad_x[None],
        *[grads[n] for n in order],
        *[deltas[n] for n in order],
        *[new_m[n] for n in order],
        *[new_v[n] for n in order],
    )
```

```python
import jax
import jax.numpy as jnp
from jax import lax
from jax.experimental import pallas as pl
from jax.experimental.pallas import tpu as pltpu

F32 = jnp.float32
BF16 = jnp.bfloat16
EPS = 1e-6

D_MODEL = 2048
POOL_WINDOWS = (2, 4, 8, 16)
POOL_GROUP_DIM = 256
BRANCH = 1024
SGU_CHUNK = 128
SGU_HEADS = 8
SGU_HEAD_DIM = 128
XATTN_HEADS = 4
XATTN_HEAD_DIM = 256
MIX_WIDTH = 3 * BRANCH
IN_WIDTH = 7 * BRANCH
N_DEV = 8
WIN_BLK = IN_WIDTH // N_DEV
WOUT_BLK = MIX_WIDTH // N_DEV
WKV_BLK = D_MODEL // N_DEV
POOL_BLK = POOL_GROUP_DIM // N_DEV
HALO = 16
LANES = 128
SUBLANES = 8

ADAM_LR = 0.001
ADAM_B1 = 0.9
ADAM_B2 = 0.999
ADAM_EPS = 1e-08
ADAM_WD = 0.01
ADAM_STEP = 10

VMEM_LIMIT_BYTES = 56 * 1024 * 1024
VMEM_LIMIT_MIX_BYTES = 63 * 1024 * 1024

TILE_PROJ = 512
TILE_MIX = 128
TILE_GRAD = 512
TILE_WEIGHT_GRAD = 1024
TILE_ADAM_BYTES = 1 << 20

ANY = pl.BlockSpec(memory_space=pl.ANY)
NN = (((1,), (0,)), ((), ()))
NT = (((1,), (1,)), ((), ()))
TN = (((0,), (0,)), ((), ()))
MESH = pl.DeviceIdType.MESH


def _dot(a, b, dims=NN):
    return lax.dot_general(a, b, dims, preferred_element_type=F32)


def _params(*semantics, vmem_limit_bytes=VMEM_LIMIT_BYTES):
    return pltpu.CompilerParams(dimension_semantics=semantics, vmem_limit_bytes=vmem_limit_bytes)


def _rowmean(a):
    return jnp.mean(a, axis=-1, keepdims=True)


def _colsum(a):
    return jnp.sum(a, axis=0, keepdims=True)


def _full(shape):
    zeros = (0,) * len(shape)
    return pl.BlockSpec(shape, lambda *_: zeros)


def _resident(shape):
    zeros = (0,) * len(shape)
    return pl.BlockSpec(shape, lambda *_: zeros, pipeline_mode=pl.Buffered(1))


def _kv_forward(mem, mem_norm, wkv):
    m, d = mem.shape
    n = wkv.shape[1]
    cols = 4 * LANES

    def body(mem_ref, g_ref, w_ref, memn_ref, kv_ref, kvt_ref):
        mv = mem_ref[...]
        r = lax.rsqrt(_rowmean(mv * mv) + EPS)
        memn = (mv * r * g_ref[...]).astype(BF16)
        memn_ref[...] = memn
        kv = _dot(memn, w_ref[...])
        kv_ref[...] = kv.astype(BF16)
        kvt_ref[...] = kv.T.astype(BF16)

    return pl.pallas_call(
        body,
        name="kv_forward",
        grid=(n // cols,),
        in_specs=[_full((m, d)), _full((1, d)), pl.BlockSpec((d, cols), lambda j: (0, j))],
        out_specs=[_full((m, d)), pl.BlockSpec((m, cols), lambda j: (0, j)), pl.BlockSpec((cols, m), lambda j: (j, 0))],
        out_shape=[
            jax.ShapeDtypeStruct((m, d), BF16), jax.ShapeDtypeStruct((m, n), BF16), jax.ShapeDtypeStruct((n, m), BF16)
        ],
        compiler_params=_params("arbitrary"),
    )(mem, mem_norm, wkv)


def _proj_gather(x_in, norm_pre, shards):
    s, d = x_in.shape
    t = min(TILE_PROJ, s)
    n_t = s // t
    n_arr = len(shards)

    def places(x, y, c):
        return (x, y, c), (x, y, 1 - c), (x ^ c, y ^ (1 - c)), (x ^ (1 - c), y ^ c), (1 - x, 1 - y)

    def index(chip, core):
        return 4 * chip[0] + 2 * chip[1] + core

    _, _, chip_a, chip_b, chip_d = places(*_position())
    c_out = lax.axis_index("c")
    me_out = index((lax.axis_index("x"), lax.axis_index("y")), c_out)
    order = jnp.stack(
        [
            me_out, me_out ^ 1, index(chip_a, c_out), index(chip_b, c_out), index(chip_b, 1 - c_out),
            index(chip_a, 1 - c_out), index(chip_d, c_out), index(chip_d, 1 - c_out),
        ]
    ).astype(jnp.int32)

    def body(order_ref, x_ref, g_ref, *refs):
        del order_ref
        raw = refs[:n_arr]
        proj_ref, h_ref = refs[n_arr : n_arr + 2]
        out = refs[n_arr + 2 : 2 * n_arr + 2]
        staged = refs[2 * n_arr + 2 : 3 * n_arr + 1]
        src = (raw[0],) + tuple(staged)
        wbuf, hs, send_sems, recv_sems, local_sems, load_sems = refs[3 * n_arr + 1 : 3 * n_arr + 7]
        wide = refs[3 * n_arr + 7 : 4 * n_arr + 6]
        narrow = refs[4 * n_arr + 6 :]
        j = pl.program_id(0)
        i = pl.program_id(1)
        me, sibling, chip_a, chip_b, chip_d = places(*_position())
        c = me[2]

        def block(a, chip, core):
            return out[a].at[index(chip, core)]

        def copy(a, k, owner, to, from_input=False):
            return pltpu.make_async_remote_copy(
                src_ref=src[a] if from_input else block(a, *owner),
                dst_ref=block(a, *owner),
                send_sem=send_sems.at[a, k],
                recv_sem=recv_sems.at[a, k],
                device_id=to,
                device_id_type=MESH,
            )

        mine = (me[:2], c)

        def own(a):
            return pltpu.make_async_copy(src[a], block(a, *mine), local_sems.at[a])

        def first_sends(a):
            return [
                copy(a, 0, mine, sibling, from_input=True),
                copy(a, 1, mine, (*chip_a, c), from_input=True),
                copy(a, 2, mine, (*chip_b, c), from_input=True),
            ]

        def onward(a, k):
            owner = {3: chip_a, 4: chip_a, 5: chip_b, 6: chip_d}[k]
            return copy(a, k, (owner, c), (*chip_b, c) if k == 3 else sibling)

        def landed(a, k):
            owner = {0: mine[0], 1: chip_a, 2: chip_b, 3: chip_d, 4: chip_b, 5: chip_a, 6: chip_d}[k]
            core = c if k in (1, 2, 3) else 1 - c
            copy(a, k, (owner, core), me).wait_recv()
            return owner, core

        def load(ref, step):
            return pltpu.make_async_copy(ref, wbuf.at[step % 2], load_sems.at[step % 2])

        @pl.when(jnp.logical_and(j == 0, i == 0))
        def _():
            own(0).start()
            for cp in first_sends(0):
                cp.start()
            for a in range(1, n_arr):
                pltpu.sync_copy(raw[a], wide[a - 1])
                narrow[a - 1][...] = wide[a - 1][...].astype(BF16)
                pltpu.sync_copy(narrow[a - 1], staged[a - 1])
            load(src[0], 0).start()
            load(src[0], 0).wait()

        steps = {1: (0, ()), 2: (1, (3, 4)), 3: (2, (5,)), 4: (4, ()), 5: (5, ()), 6: (3, (6,)), 7: (6, ())}
        for step, (k, then) in steps.items():

            @pl.when(jnp.logical_and(j == step, i == 0))
            def _():
                load(src[0], step).wait()

            @pl.when(jnp.logical_and(j == step - 1, i == n_t - 1))
            def _():
                owner = landed(0, k)
                for k2 in then:
                    onward(0, k2).start()
                if k == 1:
                    for a in range(1, n_arr):
                        own(a).start()
                        for cp in first_sends(a):
                            cp.start()
                if k == 3:
                    for a in range(1, n_arr):
                        for k1, then1 in ((1, (3, 4)), (2, (5,))):
                            landed(a, k1)
                            for k2 in then1:
                                onward(a, k2).start()
                load(block(0, *owner), step).start()

        @pl.when(j == 0)
        def _():
            xv = x_ref[...]
            h = (xv * lax.rsqrt(_rowmean(xv * xv) + EPS) * g_ref[...]).astype(BF16)
            hs[i] = h
            h_ref[...] = h

        proj_ref[...] = _dot(hs[i], wbuf[j % 2]).astype(BF16)

        @pl.when(jnp.logical_and(j == N_DEV - 1, i == n_t - 1))
        def _():
            for a in range(1, n_arr):
                landed(a, 3)
                onward(a, 6).start()
            for a in range(1, n_arr):
                for k in (0, 4, 5, 6):
                    landed(a, k)
            for a in range(n_arr):
                for cp in first_sends(a) + [onward(a, k) for k in (3, 4, 5, 6)]:
                    cp.wait_send()
                own(a).wait()

    res = pl.pallas_call(
        body,
        name="proj_gather",
        grid_spec=pltpu.PrefetchScalarGridSpec(
            num_scalar_prefetch=1,
            grid=(N_DEV, n_t),
            in_specs=[
                pl.BlockSpec((t, d), lambda j, i, order_ref: (jnp.where(j == 0, i, n_t - 1), 0)),
                pl.BlockSpec((1, d), lambda j, i, order_ref: (0, 0)),
            ]
            + [ANY] * n_arr,
            out_specs=[
                pl.BlockSpec((t, WIN_BLK), lambda j, i, order_ref: (i, order_ref[j])),
                pl.BlockSpec((t, d), lambda j, i, order_ref: (jnp.where(j == 0, i, n_t - 1), 0)),
            ]
            + [ANY] * (2 * n_arr - 1),
            scratch_shapes=[
                pltpu.VMEM((2,) + shards[0].shape, BF16),
                pltpu.VMEM((n_t, t, d), BF16),
                pltpu.SemaphoreType.DMA((n_arr, 7)),
                pltpu.SemaphoreType.DMA((n_arr, 7)),
                pltpu.SemaphoreType.DMA((n_arr,)),
                pltpu.SemaphoreType.DMA((2,)),
            ]
            + [pltpu.VMEM(a.shape, F32) for a in shards[1:]]
            + [pltpu.VMEM(a.shape, BF16) for a in shards[1:]],
        ),
        out_shape=[jax.ShapeDtypeStruct((s, IN_WIDTH), BF16), jax.ShapeDtypeStruct((s, d), BF16)]
        + [jax.ShapeDtypeStruct((N_DEV,) + a.shape, BF16) for a in shards]
        + [jax.ShapeDtypeStruct(a.shape, BF16) for a in shards[1:]],
        compiler_params=_params("arbitrary", "arbitrary"),
    )(order, x_in, norm_pre, *shards)
    return res[0], res[1], res[2 : 2 + n_arr]


def _sigmoid(a):
    return jax.nn.sigmoid(a)


def _dsilu(a, sg):
    return sg * (1.0 + a * (1.0 - sg))


def _rms_fwd(u, gain):
    r = lax.rsqrt(_rowmean(u * u) + EPS)
    n = u * r
    return r, n, n * gain


def _rms_bwd(dy, gain, r, n):
    dn = dy * gain
    return _colsum(dy * n), r * (dn - n * _rowmean(dn * n))


def _mix(proj, x, target, kv, kv_t, wout, wout_t, pool_w, pool_w_t, pool_scale, ln_g, ln_b, sgu_bias, sgu_wm, sgu_wm_t, branch_norm, norm_post):
    s, d = x.shape
    t = min(TILE_MIX, s)
    n_tiles = s // t
    n_chunks = t // SGU_CHUNK
    halo_blocks_per_tile = t // HALO
    inv_d = 1.0 / d
    scale = 1.0 / (XATTN_HEAD_DIM**0.5)

    def body(
        proj_ref, halo_ref, x_ref, tgt_ref, kv_ref, kvt_ref, wout_hbm, wout_t_hbm, pw_ref, pwt_ref, pscale_ref, lng_ref,
        lnb_ref, bias_ref, wm_ref, wmt_ref, bnorm_ref, gpost_ref,
        y_ref, dout_ref, dxo_ref, dproj_ref, loss_ref, dgpost_ref, dbnorm_ref, dpscale_ref, dlng_ref, dlnb_ref,
        dpw_out, dwm_out, dbias_ref, dkv_out,
        carry_ref, dzsum_ref, dpw_ref, dwm_ref, dkv_ref, wout_ref, wout_t_ref, wout_sems,
    ):
        i = pl.program_id(0)
        tile = n_tiles - 1 - i
        wout_load = pltpu.make_async_copy(wout_hbm, wout_ref, wout_sems.at[0])
        wout_t_load = pltpu.make_async_copy(wout_t_hbm, wout_t_ref, wout_sems.at[1])

        @pl.when(i == 0)
        def _():
            wout_load.start()
            wout_t_load.start()
            carry_ref[...] = jnp.zeros_like(carry_ref)
            dzsum_ref[...] = jnp.zeros_like(dzsum_ref)
            for ref in (loss_ref, dgpost_ref, dbnorm_ref, dpscale_ref, dlng_ref, dlnb_ref, dpw_ref, dwm_ref, dkv_ref):
                ref[...] = jnp.zeros_like(ref)

        t_glob = tile * t + lax.broadcasted_iota(jnp.int32, (t, 1), 0)
        inv_cnt = [1.0 / jnp.minimum(t_glob + 1, w).astype(F32) for w in POOL_WINDOWS]

        xa = proj_ref[:, 0:BRANCH].astype(F32)
        ga = proj_ref[:, BRANCH : 2 * BRANCH].astype(F32)
        halo = jnp.where(tile == 0, 0.0, halo_ref[...].astype(F32))
        d_bf, pm_parts = [], []
        for g, w in enumerate(POOL_WINDOWS):
            cols = slice(g * POOL_GROUP_DIM, (g + 1) * POOL_GROUP_DIM)
            acc = jnp.concatenate([halo[:, cols], xa[:, cols]], axis=0)
            k = 1
            while k < w:
                acc = acc + pltpu.roll(acc, k, axis=0)
                k *= 2
            dg = (acc[HALO:, :] * inv_cnt[g] - xa[:, cols]).astype(BF16)
            d_bf.append(dg)
            pm_parts.append(_dot(dg, pw_ref[g]))
        pm = jnp.concatenate(pm_parts, axis=1)
        pscale = pscale_ref[...]
        pa = pm * pscale
        sga = _sigmoid(ga)
        sila = ga * sga
        ua = pa * sila
        g_a = bnorm_ref[:, 0:BRANCH]
        ra, na, ya = _rms_fwd(ua, g_a)

        u = proj_ref[:, 2 * BRANCH : 3 * BRANCH].astype(F32)
        v = proj_ref[:, 3 * BRANCH : 4 * BRANCH].astype(F32)
        gb = proj_ref[:, 4 * BRANCH : 5 * BRANCH].astype(F32)
        lng = lng_ref[...]
        vc = v - _rowmean(v)
        rstd = lax.rsqrt(_rowmean(vc * vc) + EPS)
        vhat = vc * rstd
        vn_bf = (vhat * lng + lnb_ref[...]).astype(BF16)
        z_rows = []
        for c in range(n_chunks):
            rows = slice(c * SGU_CHUNK, (c + 1) * SGU_CHUNK)
            z_rows.append(
                jnp.concatenate(
                    [
                        _dot(wm_ref[hd], vn_bf[rows, hd * SGU_HEAD_DIM : (hd + 1) * SGU_HEAD_DIM])
                        for hd in range(SGU_HEADS)
                    ],
                    axis=1,
                )
                + bias_ref[...]
            )
        z = z_rows[0] if n_chunks == 1 else jnp.concatenate(z_rows, axis=0)
        sb = u * z
        sgb = _sigmoid(gb)
        silb = gb * sgb
        ub = sb * silb
        g_b = bnorm_ref[:, BRANCH : 2 * BRANCH]
        rb, nb, yb = _rms_fwd(ub, g_b)

        q = proj_ref[:, 5 * BRANCH : 6 * BRANCH]
        gc = proj_ref[:, 6 * BRANCH : 7 * BRANCH].astype(F32)
        q_bf, p_bf, o_parts = [], [], []
        for hd in range(XATTN_HEADS):
            cols = slice(hd * XATTN_HEAD_DIM, (hd + 1) * XATTN_HEAD_DIM)
            qh = q[:, cols]
            sc = _dot(qh, kvt_ref[cols, :]) * scale
            e = jnp.exp(sc - jnp.max(sc, axis=-1, keepdims=True))
            p = e / jnp.sum(e, axis=-1, keepdims=True)
            q_bf.append(qh)
            p_bf.append(p.astype(BF16))
            o_parts.append(_dot(p_bf[hd], kv_ref[:, BRANCH + hd * XATTN_HEAD_DIM : BRANCH + (hd + 1) * XATTN_HEAD_DIM]))
        o = jnp.concatenate(o_parts, axis=1)
        sgc = _sigmoid(gc)
        silc = gc * sgc
        uc = o * silc
        g_c = bnorm_ref[:, 2 * BRANCH : 3 * BRANCH]
        rc, nc, yc = _rms_fwd(uc, g_c)

        @pl.when(i == 0)
        def _():
            wout_load.wait()

        out = None
        for b, y_branch in enumerate((ya, yb, yc)):
            rows = slice(b * BRANCH, (b + 1) * BRANCH)
            y_bf = y_branch.astype(BF16)
            y_ref[:, rows] = y_bf
            part = _dot(y_bf, wout_ref[rows, :])
            out = part if out is None else out + part
        gpost = gpost_ref[...]
        r_out = lax.rsqrt(_rowmean(out * out) + EPS)
        on = out * r_out
        err = x_ref[...] + on * gpost - tgt_ref[...]
        loss_ref[...] += 0.5 * jnp.sum(_rowmean(err * err), axis=0, keepdims=True)

        dxo = err * inv_d
        dxo_ref[...] = dxo
        dgp, dout = _rms_bwd(dxo, gpost, r_out, on)
        dgpost_ref[...] += dgp
        dout_bf = dout.astype(BF16)
        dout_ref[...] = dout_bf

        @pl.when(i == 0)
        def _():
            wout_t_load.wait()

        dy = [_dot(dout_bf, wout_t_ref[:, b * BRANCH : (b + 1) * BRANCH]) for b in range(3)]

        dg_a, dua = _rms_bwd(dy[0], g_a, ra, na)
        dg_b, dub = _rms_bwd(dy[1], g_b, rb, nb)
        dg_c, duc = _rms_bwd(dy[2], g_c, rc, nc)
        dbnorm_ref[...] += jnp.concatenate([dg_a, dg_b, dg_c], axis=1)

        dpa = dua * sila
        dga = dua * pa * _dsilu(ga, sga)
        dpscale_ref[...] += _colsum(dpa * pm)
        dpm = dpa * pscale
        dxa_parts, carry_parts = [], []
        for g, w in enumerate(POOL_WINDOWS):
            cols = slice(g * POOL_GROUP_DIM, (g + 1) * POOL_GROUP_DIM)
            dpm_g = dpm[:, cols].astype(BF16)
            dd = _dot(dpm_g, pwt_ref[g])
            dpw_ref[g] += _dot(d_bf[g], dpm_g, TN)
            cg = dd * inv_cnt[g]
            carry_parts.append(cg[0:HALO, :])
            acc = jnp.concatenate([cg, carry_ref[:, cols]], axis=0)
            k = 1
            while k < w:
                acc = acc + pltpu.roll(acc, t + HALO - k, axis=0)
                k *= 2
            dxa_parts.append(acc[0:t, :] - dd)
        carry_ref[...] = jnp.concatenate(carry_parts, axis=1)
        dxa = jnp.concatenate(dxa_parts, axis=1)

        dsb = dub * silb
        dgb = dub * sb * _dsilu(gb, sgb)
        du = dsb * z
        dz = dsb * u
        dz_bf = dz.astype(BF16)
        dvn_rows = []
        dz_sum = None
        for c in range(n_chunks):
            rows = slice(c * SGU_CHUNK, (c + 1) * SGU_CHUNK)
            dz_sum = dz[rows, :] if dz_sum is None else dz_sum + dz[rows, :]
            parts = []
            for hd in range(SGU_HEADS):
                cols = slice(hd * SGU_HEAD_DIM, (hd + 1) * SGU_HEAD_DIM)
                parts.append(_dot(wmt_ref[hd], dz_bf[rows, cols]))
                dwm_ref[hd] += _dot(dz_bf[rows, cols], vn_bf[rows, cols], NT)
            dvn_rows.append(jnp.concatenate(parts, axis=1))
        dzsum_ref[...] += dz_sum
        dvn = dvn_rows[0] if n_chunks == 1 else jnp.concatenate(dvn_rows, axis=0)
        dlng_ref[...] += _colsum(dvn * vhat)
        dlnb_ref[...] += _colsum(dvn)
        dvh = dvn * lng
        dv = rstd * (dvh - _rowmean(dvh) - vhat * _rowmean(dvh * vhat))

        do = duc * silc
        dgc = duc * o * _dsilu(gc, sgc)
        dq_parts = []
        for hd in range(XATTN_HEADS):
            cols = slice(hd * XATTN_HEAD_DIM, (hd + 1) * XATTN_HEAD_DIM)
            vcols = slice(BRANCH + hd * XATTN_HEAD_DIM, BRANCH + (hd + 1) * XATTN_HEAD_DIM)
            do_h = do[:, cols].astype(BF16)
            p = p_bf[hd].astype(F32)
            dp = _dot(do_h, kvt_ref[vcols, :])
            dkv_ref[:, vcols] += _dot(p_bf[hd], do_h, TN)
            ds_bf = (p * (dp - jnp.sum(dp * p, axis=-1, keepdims=True)) * scale).astype(BF16)
            dq_parts.append(_dot(ds_bf, kv_ref[:, cols]))
            dkv_ref[:, cols] += _dot(ds_bf, q_bf[hd], TN)
        dq = jnp.concatenate(dq_parts, axis=1)

        dproj_ref[...] = jnp.concatenate([dxa, dga, du, dv, dgb, dq, dgc], axis=1).astype(BF16)

        @pl.when(i == n_tiles - 1)
        def _():
            keep = lax.broadcasted_iota(jnp.int32, (SGU_CHUNK, SGU_CHUNK), 0) >= lax.broadcasted_iota(
                jnp.int32, (SGU_CHUNK, SGU_CHUNK), 1
            )
            for hd in range(SGU_HEADS):
                dwm_ref[hd] = jnp.where(keep, dwm_ref[hd], 0.0)
                per_pos = dzsum_ref[:, hd * SGU_HEAD_DIM : (hd + 1) * SGU_HEAD_DIM]
                dbias_ref[hd : hd + 1, :] = _colsum(per_pos.T)
            for acc, res in ((dpw_ref, dpw_out), (dwm_ref, dwm_out), (dkv_ref, dkv_out)):
                pltpu.sync_copy(acc, res)

    row_tile = lambda width: pl.BlockSpec((t, width), lambda i: (n_tiles - 1 - i, 0))
    halo_spec = pl.BlockSpec(
        (HALO, BRANCH), lambda i: (jnp.maximum((n_tiles - 1 - i) * halo_blocks_per_tile - 1, 0), 0)
    )
    acc_shapes = [
        (1, 128),
        (1, d),
        (1, MIX_WIDTH),
        (1, BRANCH),
        (1, BRANCH),
        (1, BRANCH),
        pool_w.shape,
        sgu_wm.shape,
        (SGU_HEADS, SGU_CHUNK),
        kv.shape,
    ]
    return pl.pallas_call(
        body,
        name="mix",
        grid=(n_tiles,),
        in_specs=[
            row_tile(IN_WIDTH), halo_spec, row_tile(d), row_tile(d), _resident(kv.shape), _resident(kv_t.shape),
            ANY, ANY, _resident(pool_w.shape), _resident(pool_w_t.shape),
            _full((1, BRANCH)), _full((1, BRANCH)), _full((1, BRANCH)), _resident((SGU_CHUNK, BRANCH)),
            _resident(sgu_wm.shape), _resident(sgu_wm_t.shape), _full((1, MIX_WIDTH)), _full((1, d)),
        ],
        out_specs=[row_tile(MIX_WIDTH), row_tile(d), row_tile(d), row_tile(IN_WIDTH)]
        + [ANY if len(a) == 3 or a == kv.shape else _full(a) for a in acc_shapes],
        out_shape=[
            jax.ShapeDtypeStruct((s, MIX_WIDTH), BF16),
            jax.ShapeDtypeStruct((s, d), BF16),
            jax.ShapeDtypeStruct((s, d), F32),
            jax.ShapeDtypeStruct((s, IN_WIDTH), BF16),
        ]
        + [jax.ShapeDtypeStruct(a, F32) for a in acc_shapes],
        scratch_shapes=[
            pltpu.VMEM((HALO, BRANCH), F32), pltpu.VMEM((SGU_CHUNK, BRANCH), F32), pltpu.VMEM(pool_w.shape, F32),
            pltpu.VMEM(sgu_wm.shape, F32), pltpu.VMEM(kv.shape, F32), pltpu.VMEM(wout.shape, BF16),
            pltpu.VMEM(wout_t.shape, BF16), pltpu.SemaphoreType.DMA((2,)),
        ],
        compiler_params=_params("arbitrary", vmem_limit_bytes=VMEM_LIMIT_MIX_BYTES),
    )(
        proj, proj, x, target, kv, kv_t, wout, wout_t, pool_w, pool_w_t, pool_scale, ln_g, ln_b, sgu_bias, sgu_wm,
        sgu_wm_t, branch_norm, norm_post,
    )


def _position():
    return lax.axis_index("x"), lax.axis_index("y"), lax.axis_index("c")


N_CHIPS = 4


def _landing_shape(kind, a):
    return (N_CHIPS,) + a.shape[2:] if kind == "pair" else a.shape


def _carry_specs(groups):
    arrays = [(kind, a) for kind, arrs in groups for a in arrs]
    scratch = []
    for _, arrs in groups:
        n = len(arrs)
        scratch += [pltpu.SemaphoreType.DMA((n, N_DEV)), pltpu.SemaphoreType.DMA((n, N_DEV)), pltpu.SemaphoreType.DMA((n,))]
    return dict(
        n=len(arrays),
        operands=[a for _, a in arrays],
        in_specs=[ANY] * len(arrays),
        out_specs=[ANY] * len(arrays),
        out_shape=[jax.ShapeDtypeStruct(_landing_shape(kind, a), a.dtype) for kind, a in arrays],
        scratch_shapes=scratch,
    )


def _carry(groups, src, out, sems):
    x, y, c = _position()
    chip = 2 * x + y

    def remote(s, d, send_sems, recv_sems, a, m, to):
        return pltpu.make_async_remote_copy(
            src_ref=s, dst_ref=d, send_sem=send_sems.at[a, m], recv_sem=recv_sems.at[a, m], device_id=to,
            device_id_type=MESH,
        )

    def copies():
        far, near = [], []
        at = 0
        for g, (kind, arrs) in enumerate(groups):
            send_sems, recv_sems, local_sems = sems[3 * g : 3 * g + 3]
            for a in range(len(arrs)):
                s, d = src[at + a], out[at + a]
                if kind == "pair" and s.shape[0] == 1:
                    for b in range(N_CHIPS):
                        far.append(remote(s.at[0, 0], d.at[b], send_sems, recv_sems, a, 1 + b, (x, y, 1 - c)))
                elif kind == "pair":
                    far.append(remote(s.at[:, 1 - c], d, send_sems, recv_sems, a, 1, (x, y, 1 - c)))
                else:
                    assert kind == "chips", kind
                    for m in range(1, N_CHIPS):
                        px, py = x ^ (m >> 1), y ^ (m & 1)
                        far.append(remote(s.at[2 * px + py], d.at[chip], send_sems, recv_sems, a, m, (px, py, c)))
                    near.append(pltpu.make_async_copy(s.at[chip], d.at[chip], local_sems.at[a]))
            at += len(arrs)
        return far, near

    def start():
        far, near = copies()
        for cp in near + far:
            cp.start()

    def finish():
        far, near = copies()
        for cp in far:
            cp.wait_recv()
        for cp in far:
            cp.wait_send()
        for cp in near:
            cp.wait()

    return start, finish


def _pair_sum(mine, theirs, name, groups=()):
    n = len(mine)
    carried = _carry_specs(groups)
    n_c = carried["n"]
    core = lax.axis_index("c").astype(jnp.int32).reshape(1)

    def body(core_ref, *refs):
        del core_ref
        own = refs[:n]
        sib = refs[n : 2 * n]
        src = refs[2 * n : 2 * n + n_c]
        out = refs[2 * n + n_c : 3 * n + n_c]
        landed = refs[3 * n + n_c : 3 * n + 2 * n_c]
        start, finish = _carry(groups, src, landed, refs[3 * n + 2 * n_c :])
        b = pl.program_id(0)

        @pl.when(b == 0)
        def _():
            start()

        for a in range(n):
            out[a][...] = (own[a][...].astype(F32) + sib[a][...].astype(F32)).astype(out[a].dtype)

        @pl.when(b == N_CHIPS - 1)
        def _():
            finish()

    block = lambda a: pl.BlockSpec((None,) + a.shape[1:], lambda b, core_ref: (b, 0, 0))
    res = pl.pallas_call(
        body,
        name=name,
        grid_spec=pltpu.PrefetchScalarGridSpec(
            num_scalar_prefetch=1,
            grid=(N_CHIPS,),
            in_specs=[
                pl.BlockSpec((None, None) + a.shape[2:], lambda b, core_ref: (0, 0, 0, 0))
                if a.shape[0] == 1
                else pl.BlockSpec((None, None) + a.shape[2:], lambda b, core_ref: (b, core_ref[0], 0, 0))
                for a in mine
            ]
            + [block(a) for a in theirs]
            + carried["in_specs"],
            out_specs=[block(a) for a in theirs] + carried["out_specs"],
            scratch_shapes=carried["scratch_shapes"],
        ),
        out_shape=[jax.ShapeDtypeStruct(a.shape, a.dtype) for a in theirs] + carried["out_shape"],
        compiler_params=_params("arbitrary"),
    )(core, *mine, *theirs, *carried["operands"])
    return res[:n], res[n:]


def _weight_grad(a, b, n_blk, blocked, name, groups):
    s = a.shape[0]
    t = min(TILE_WEIGHT_GRAD, s)
    n_t = s // t
    n_pairs = n_blk // 2
    if blocked == "cols":
        k, c = a.shape[1], b.shape[1] // n_blk
        a_spec = pl.BlockSpec((t, k), lambda j, i: (i, 0))
        b_spec = pl.BlockSpec((t, 2 * c), lambda j, i: (i, j))
        acc_shape = (k, 2 * c)
    else:
        k, c = a.shape[1] // n_blk, b.shape[1]
        a_spec = pl.BlockSpec((t, 2 * k), lambda j, i: (i, j))
        b_spec = pl.BlockSpec((t, c), lambda j, i: (i, 0))
        acc_shape = (2 * k, c)
    carried = _carry_specs(groups)
    n_p = carried["n"]

    def body(a_ref, b_ref, *refs):
        src = refs[:n_p]
        o_ref, theirs_ref = refs[n_p : n_p + 2]
        landed = refs[n_p + 2 : 2 * n_p + 2]
        acc_ref, sbuf, pair_send, pair_recv = refs[2 * n_p + 2 : 2 * n_p + 6]
        start, finish = _carry(groups, src, landed, refs[2 * n_p + 6 :])
        j = pl.program_id(0)
        i = pl.program_id(1)
        x, y, c_me = _position()

        def to_sibling(pair):
            return pltpu.make_async_remote_copy(
                src_ref=sbuf.at[1 - c_me], dst_ref=theirs_ref.at[pair], send_sem=pair_send.at[pair],
                recv_sem=pair_recv.at[pair], device_id=(x, y, 1 - c_me), device_id_type=MESH,
            )

        @pl.when(jnp.logical_and(j == 0, i == 0))
        def _():
            start()

        @pl.when(i == 0)
        def _():
            acc_ref[...] = jnp.zeros_like(acc_ref)

        acc_ref[...] += _dot(a_ref[...], b_ref[...], TN)

        @pl.when(i == n_t - 1)
        def _():
            for pair in range(1, n_pairs):

                @pl.when(j == pair)
                def _():
                    to_sibling(pair - 1).wait_send()

            for half in range(2):
                if blocked == "cols":
                    block = acc_ref[:, half * c : (half + 1) * c].astype(BF16)
                else:
                    block = acc_ref[half * k : (half + 1) * k, :].astype(BF16)
                o_ref[half] = block
                sbuf[half] = block
            for pair in range(n_pairs):

                @pl.when(j == pair)
                def _():
                    to_sibling(pair).start()

        @pl.when(jnp.logical_and(j == n_pairs - 1, i == n_t - 1))
        def _():
            to_sibling(n_pairs - 1).wait_send()
            for pair in range(n_pairs):
                to_sibling(pair).wait_recv()
            finish()

    res = pl.pallas_call(
        body,
        name=name,
        grid=(n_pairs, n_t),
        in_specs=[a_spec, b_spec] + carried["in_specs"],
        out_specs=[pl.BlockSpec((2, k, c), lambda j, i: (j, 0, 0)), ANY] + carried["out_specs"],
        out_shape=[jax.ShapeDtypeStruct((n_blk, k, c), BF16), jax.ShapeDtypeStruct((n_pairs, k, c), BF16)]
        + carried["out_shape"],
        scratch_shapes=[
            pltpu.VMEM(acc_shape, F32), pltpu.VMEM((2, k, c), BF16), pltpu.SemaphoreType.DMA((n_pairs,)),
            pltpu.SemaphoreType.DMA((n_pairs,)),
        ]
        + carried["scratch_shapes"],
        compiler_params=_params("arbitrary", "arbitrary", vmem_limit_bytes=VMEM_LIMIT_MIX_BYTES),
    )(a, b, *carried["operands"])
    return res[0], res[1], res[2:]


HBM = pl.BlockSpec(memory_space=pltpu.HBM)
SEMAPHORES = pl.BlockSpec(memory_space=pltpu.SEMAPHORE)
DATAFLOW = pltpu.SideEffectType.DATAFLOW_SIDE_EFFECTING


def _chip_copies(v_ref, land_ref, send_sems, recv_sems):
    x, y, c = _position()
    chip = 2 * x + y
    return chip, [
        pltpu.make_async_remote_copy(
            src_ref=v_ref.at[2 * (x ^ (m >> 1)) + (y ^ (m & 1))], dst_ref=land_ref.at[chip],
            send_sem=send_sems.at[m - 1], recv_sem=recv_sems.at[m - 1],
            device_id=(x ^ (m >> 1), y ^ (m & 1), c), device_id_type=MESH,
        )
        for m in range(1, N_CHIPS)
    ]


def _chips_start(sums):
    def body(v_ref, land_ref, send_sems, recv_sems, v_thru, land_thru, token, own_sem):
        del v_thru, land_thru
        chip, copies = _chip_copies(v_ref, land_ref, send_sems, recv_sems)
        for cp in copies:
            cp.start()
        own = pltpu.make_async_copy(v_ref.at[chip], land_ref.at[chip], own_sem)
        own.start()
        own.wait()
        token[...] = jnp.zeros_like(token)

    sems = pltpu.SemaphoreType.DMA((N_CHIPS - 1,))
    return pl.pallas_call(
        body,
        name="chips_start",
        out_shape=(
            sems, sems, pltpu.HBM(sums.shape, sums.dtype), pltpu.HBM(sums.shape, sums.dtype),
            jax.ShapeDtypeStruct((SUBLANES, LANES), F32),
        ),
        in_specs=(HBM, HBM),
        out_specs=(SEMAPHORES, SEMAPHORES, HBM, HBM, pl.BlockSpec(memory_space=pltpu.VMEM)),
        input_output_aliases={0: 2, 1: 3},
        scratch_shapes=[pltpu.SemaphoreType.DMA],
        compiler_params=pltpu.CompilerParams(has_side_effects=DATAFLOW),
    )(
        pltpu.with_memory_space_constraint(sums, pltpu.HBM),
        pltpu.with_memory_space_constraint(lax.empty(sums.shape, sums.dtype), pltpu.HBM),
    )


def _chips_wait(send_sems, recv_sems, v_thru, land_thru, after):
    def body(v_ref, land_ref, send_sems, recv_sems, after_ref, v_dead, got_ref):
        del after_ref, v_dead, got_ref
        _, copies = _chip_copies(v_ref, land_ref, send_sems, recv_sems)
        for cp in copies:
            cp.wait_send()
        for cp in copies:
            cp.wait_recv()

    return pl.pallas_call(
        body,
        name="chips_wait",
        out_shape=(pltpu.HBM(v_thru.shape, v_thru.dtype), pltpu.HBM(v_thru.shape, v_thru.dtype)),
        in_specs=(HBM, HBM, SEMAPHORES, SEMAPHORES, ANY),
        out_specs=(HBM, HBM),
        input_output_aliases={0: 0, 1: 1},
        compiler_params=pltpu.CompilerParams(has_side_effects=DATAFLOW),
    )(v_thru, land_thru, send_sems, recv_sems, after)[1]


def _exchange_all(part, name):
    def body(p_ref, o_ref, send_sems, recv_sems, own_sem):
        x, y, c = _position()
        me = 4 * x + 2 * y + c
        far = [
            pltpu.make_async_remote_copy(
                src_ref=p_ref, dst_ref=o_ref.at[me], send_sem=send_sems.at[m], recv_sem=recv_sems.at[m],
                device_id=(x ^ ((m >> 2) & 1), y ^ ((m >> 1) & 1), c ^ (m & 1)), device_id_type=MESH,
            )
            for m in range(1, N_DEV)
        ]
        own = pltpu.make_async_copy(p_ref, o_ref.at[me], own_sem)
        for cp in [own] + far:
            cp.start()
        for cp in far:
            cp.wait_recv()
        for cp in far:
            cp.wait_send()
        own.wait()

    return pl.pallas_call(
        body,
        name=name,
        in_specs=[ANY],
        out_specs=ANY,
        out_shape=jax.ShapeDtypeStruct((N_DEV,) + part.shape, part.dtype),
        scratch_shapes=[pltpu.SemaphoreType.DMA((N_DEV,)), pltpu.SemaphoreType.DMA((N_DEV,)), pltpu.SemaphoreType.DMA],
    )(part)


def _input_grad(dproj, win_t, x, dxo, norm_pre):
    s, d = x.shape
    t = min(TILE_GRAD, s)
    n_t = s // t
    kb = 2 * WIN_BLK
    n_k = win_t.shape[0] // kb

    def body(dp_ref, w_ref, x_ref, dxo_ref, g_ref, gx_ref, dg_ref, acc_ref):
        i = pl.program_id(0)
        j = pl.program_id(1)

        @pl.when(jnp.logical_and(i == 0, j == 0))
        def _():
            dg_ref[...] = jnp.zeros_like(dg_ref)

        @pl.when(j == 0)
        def _():
            acc_ref[...] = jnp.zeros_like(acc_ref)

        acc_ref[...] += _dot(dp_ref[...], w_ref[...])

        @pl.when(j == n_k - 1)
        def _():
            xv = x_ref[...]
            gain = g_ref[...]
            r = lax.rsqrt(_rowmean(xv * xv) + EPS)
            dgain, dx = _rms_bwd(acc_ref[...], gain, r, xv * r)
            dg_ref[...] += dgain
            gx_ref[...] = dxo_ref[...] + dx

    return pl.pallas_call(
        body,
        name="input_grad",
        grid=(n_t, n_k),
        in_specs=[
            pl.BlockSpec((t, kb), lambda i, j: (i, j)),
            pl.BlockSpec((kb, d), lambda i, j: (j, 0)),
            pl.BlockSpec((t, d), lambda i, j: (i, 0)),
            pl.BlockSpec((t, d), lambda i, j: (i, 0)),
            _full((1, d)),
        ],
        out_specs=[pl.BlockSpec((t, d), lambda i, j: (i, 0)), _full((1, d))],
        out_shape=[jax.ShapeDtypeStruct((s, d), F32), jax.ShapeDtypeStruct((1, d), F32)],
        scratch_shapes=[pltpu.VMEM((t, d), F32)],
        compiler_params=_params("arbitrary", "arbitrary", vmem_limit_bytes=VMEM_LIMIT_MIX_BYTES),
    )(dproj, win_t, x, dxo, norm_pre)


def _kv_backward(dkv, memn, wkv, mem):
    m, d = mem.shape
    n = wkv.shape[1]

    def body(dkv_ref, memn_ref, w_ref, mem_ref, gw_ref, dg_ref):
        dkv_bf = dkv_ref[...].astype(BF16)
        gw_ref[...] = _dot(memn_ref[...], dkv_bf, TN).astype(BF16).reshape(N_DEV, WKV_BLK, n)
        dmemn = _dot(dkv_bf, w_ref[...], NT)
        mv = mem_ref[...]
        r = lax.rsqrt(_rowmean(mv * mv) + EPS)
        dg_ref[...] = _colsum(dmemn * (mv * r))

    return pl.pallas_call(
        body,
        name="kv_backward",
        grid=(1,),
        in_specs=[_full((m, n)), _full((m, d)), _full(wkv.shape), _full((m, d))],
        out_specs=[_full((N_DEV, WKV_BLK, n)), _full((1, d))],
        out_shape=[jax.ShapeDtypeStruct((N_DEV, WKV_BLK, n), BF16), jax.ShapeDtypeStruct((1, d), F32)],
        compiler_params=_params("arbitrary"),
    )(dkv, memn, wkv, mem)


def _adamw_math(w, g, m, v):
    m = ADAM_B1 * m + (1.0 - ADAM_B1) * g
    v = ADAM_B2 * v + (1.0 - ADAM_B2) * (g * g)
    m_hat = m / (1.0 - ADAM_B1**ADAM_STEP)
    v_hat = v / (1.0 - ADAM_B2**ADAM_STEP)
    delta = -ADAM_LR * (m_hat / (jnp.sqrt(v_hat) + ADAM_EPS) + ADAM_WD * w)
    return delta, m, v


def _adamw(parts, w, m, v, name):
    r, c = w.shape
    slots = parts.shape[0]
    t = r
    while t * c * 4 > TILE_ADAM_BYTES and t % 16 == 0:
        t //= 2

    def body(p_ref, w_ref, m_ref, v_ref, g_ref, d_ref, nm_ref, nv_ref):
        g = p_ref[0].astype(F32)
        for k in range(1, slots):
            g = g + p_ref[k].astype(F32)
        delta, nm, nv = _adamw_math(w_ref[...], g, m_ref[...], v_ref[...])
        g_ref[...] = g
        d_ref[...] = delta
        nm_ref[...] = nm
        nv_ref[...] = nv

    tile = pl.BlockSpec((t, c), lambda i: (i, 0))
    return pl.pallas_call(
        body,
        name=name,
        grid=(r // t,),
        in_specs=[pl.BlockSpec((slots, t, c), lambda i: (0, i, 0)), tile, tile, tile],
        out_specs=[tile] * 4,
        out_shape=[jax.ShapeDtypeStruct((r, c), F32)] * 4,
        compiler_params=_params("parallel"),
    )(parts, w, m, v)


def _adamw_packed(parts, triples, name):
    slots = parts.shape[0]
    sizes = [w.shape[0] for w, _, _ in triples]
    rest = parts.shape[1] - sum(sizes)

    def total(p_ref, at, rows):
        g = p_ref[0, at : at + rows, :]
        for k in range(1, slots):
            g = g + p_ref[k, at : at + rows, :]
        return g

    def body(p_ref, *refs):
        ins = refs[: 3 * len(triples)]
        outs = refs[3 * len(triples) :]
        at = 0
        for n, rows in enumerate(sizes):
            g = total(p_ref, at, rows)
            w_ref, m_ref, v_ref = ins[3 * n : 3 * n + 3]
            delta, nm, nv = _adamw_math(w_ref[...], g, m_ref[...], v_ref[...])
            for ref, val in zip(outs[4 * n : 4 * n + 4], (g, delta, nm, nv)):
                ref[...] = val
            at += rows
        if rest:
            outs[-1][...] = total(p_ref, at, rest)

    flat = [a for t in triples for a in t]
    res = pl.pallas_call(
        body,
        name=name,
        out_shape=[jax.ShapeDtypeStruct(w.shape, F32) for w, _, _ in triples for _ in range(4)]
        + ([jax.ShapeDtypeStruct((rest, 128), F32)] if rest else []),
        compiler_params=pltpu.CompilerParams(vmem_limit_bytes=VMEM_LIMIT_BYTES),
    )(parts, *flat)
    return [res[4 * n : 4 * n + 4] for n in range(len(triples))], (res[-1] if rest else None)


SMALL = ("norm_pre", "pool_scale", "sgu_ln_g", "sgu_ln_b", "sgu_w", "sgu_b", "mem_norm", "branch_norm", "norm_post")


def _local_view(name, w):
    if name == "sgu_w":
        return w.reshape(SGU_HEADS, SGU_CHUNK, SGU_CHUNK)
    if name == "sgu_b":
        return w.reshape(SGU_HEADS, SGU_CHUNK)
    return w.reshape(1, -1)


def _forward_backward(x, mem, target, shards, small):
    causal = jnp.tril(jnp.ones((SGU_CHUNK, SGU_CHUNK), dtype=bool))
    sgu_wm = jnp.where(causal[None], small["sgu_w"], 0.0).astype(BF16)
    sgu_bias = jnp.repeat(jnp.transpose(small["sgu_b"]), SGU_HEAD_DIM, axis=1)

    proj, h, (win, wkv, pool_all, wout) = _proj_gather(x, small["norm_pre"], shards)
    wout = wout.reshape(MIX_WIDTH, D_MODEL)
    wkv = wkv.reshape(D_MODEL, 2 * BRANCH)
    pool_full = (
        pool_all.reshape(N_DEV, len(POOL_WINDOWS), POOL_BLK, POOL_GROUP_DIM)
        .transpose(1, 0, 2, 3)
        .reshape(len(POOL_WINDOWS), POOL_GROUP_DIM, POOL_GROUP_DIM)
    )
    memn, kv, kv_t = _kv_forward(mem, small["mem_norm"], wkv)
    (y, dout, dxo, dproj, loss, d_norm_post, d_branch_norm, d_pool_scale, d_ln_g, d_ln_b, d_pool_w, d_sgu_w, d_sgu_b,
     dkv) = _mix(
        proj, x, target, kv, kv_t, wout, wout.T, pool_full, jnp.swapaxes(pool_full, 1, 2), small["pool_scale"],
        small["sgu_ln_g"], small["sgu_ln_b"], sgu_bias, sgu_wm, jnp.swapaxes(sgu_wm, 1, 2), small["branch_norm"],
        small["norm_post"],
    )
    g_wkv, d_mem_norm = _kv_backward(dkv, memn, wkv, mem)
    g_pool = (
        d_pool_w.reshape(len(POOL_WINDOWS), N_DEV, POOL_BLK, POOL_GROUP_DIM)
        .transpose(1, 0, 2, 3)
        .reshape(N_DEV, len(POOL_WINDOWS) * POOL_BLK, POOL_GROUP_DIM)
        .astype(BF16)
    )
    small_grads = dict(
        pool_scale=d_pool_scale, sgu_ln_g=d_ln_g, sgu_ln_b=d_ln_b, sgu_w=d_sgu_w, sgu_b=d_sgu_b,
        mem_norm=d_mem_norm, branch_norm=d_branch_norm, norm_post=d_norm_post,
    )
    packed = jnp.concatenate(
        [small_grads[n].reshape(-1, LANES) for n in SMALL if n != "norm_pre"]
        + [jnp.broadcast_to(loss, (SUBLANES, LANES))],
        axis=0,
    )
    packed = packed[None, None]

    by_chip = lambda g: g.reshape((N_CHIPS, 2) + g.shape[1:])
    small_mine = [by_chip(g_wkv), by_chip(g_pool), packed]
    g_wout, wout_theirs, small_theirs = _weight_grad(y, dout, N_DEV, "rows", "grad_w_out", [("pair", small_mine)])
    sums, _ = _pair_sum(small_mine + [by_chip(g_wout)], list(small_theirs) + [wout_theirs], "pair_sum_first")
    g_win, win_theirs, (l_wkv, l_pool, l_packed, l_wout) = _weight_grad(
        h, dproj, N_DEV, "cols", "grad_w_in", [("chips", list(sums))]
    )
    (win_sums,), _ = _pair_sum([by_chip(g_win)], [win_theirs], "pair_sum_w_in")
    win_late, _ = lax.optimization_barrier((win, g_wkv))
    send_sems, recv_sems, sums_thru, land_thru, token = _chips_start(win_sums)
    grad_x, d_norm_pre = _input_grad(
        dproj, jnp.swapaxes(win_late, 1, 2).reshape(IN_WIDTH, D_MODEL), x, dxo, small["norm_pre"] + token[0:1, 0:1]
    )
    l_win = _chips_wait(send_sems, recv_sems, sums_thru, land_thru, grad_x)
    d_norm_pre = _exchange_all(d_norm_pre, "exchange_norm_pre")
    return grad_x, dict(w_in=l_win, w_out=l_wout, w_kv=l_wkv, pool_w=l_pool), l_packed, d_norm_pre


def kernel(x, mem, norm_pre, w_in, pool_w, pool_scale, sgu_ln_g, sgu_ln_b, sgu_w, sgu_b, mem_norm, w_kv, branch_norm, w_out, norm_post, loss_target, m_norm_pre, m_w_in, m_pool_w, m_pool_scale, m_sgu_ln_g, m_sgu_ln_b, m_sgu_w, m_sgu_b, m_mem_norm, m_w_kv, m_branch_norm, m_w_out, m_norm_post, v_norm_pre, v_w_in, v_pool_w, v_pool_scale, v_sgu_ln_g, v_sgu_ln_b, v_sgu_w, v_sgu_b, v_mem_norm, v_w_kv, v_branch_norm, v_w_out, v_norm_post):
    weights = dict(norm_pre=norm_pre, w_in=w_in, pool_w=pool_w, pool_scale=pool_scale, sgu_ln_g=sgu_ln_g, sgu_ln_b=sgu_ln_b, sgu_w=sgu_w, sgu_b=sgu_b, mem_norm=mem_norm, w_kv=w_kv, branch_norm=branch_norm, w_out=w_out, norm_post=norm_post)
    first = dict(norm_pre=m_norm_pre, w_in=m_w_in, pool_w=m_pool_w, pool_scale=m_pool_scale, sgu_ln_g=m_sgu_ln_g, sgu_ln_b=m_sgu_ln_b, sgu_w=m_sgu_w, sgu_b=m_sgu_b, mem_norm=m_mem_norm, w_kv=m_w_kv, branch_norm=m_branch_norm, w_out=m_w_out, norm_post=m_norm_post)
    second = dict(norm_pre=v_norm_pre, w_in=v_w_in, pool_w=v_pool_w, pool_scale=v_pool_scale, sgu_ln_g=v_sgu_ln_g, sgu_ln_b=v_sgu_ln_b, sgu_w=v_sgu_w, sgu_b=v_sgu_b, mem_norm=v_mem_norm, w_kv=v_w_kv, branch_norm=v_branch_norm, w_out=v_w_out, norm_post=v_norm_post)
    order = ("norm_pre", "w_in", "pool_w", "pool_scale", "sgu_ln_g", "sgu_ln_b", "sgu_w", "sgu_b", "mem_norm", "w_kv", "branch_norm", "w_out", "norm_post")

    owned_shape = dict(
        w_in=(D_MODEL, WIN_BLK), w_out=(WOUT_BLK, D_MODEL), w_kv=(WKV_BLK, 2 * BRANCH),
        pool_w=(len(POOL_WINDOWS) * POOL_BLK, POOL_GROUP_DIM),
    )
    owned = {n: weights[n].reshape(owned_shape[n]) for n in owned_shape}
    small = {n: _local_view(n, weights[n]) for n in SMALL}
    grad_x, landed, landed_packed, d_norm_pre = _forward_backward(
        x[0], mem[0], loss_target[0],
        [owned["w_in"].astype(BF16)] + [owned[n] for n in ("w_kv", "pool_w", "w_out")], small,
    )
    landed_norm_pre = d_norm_pre.reshape(N_DEV, -1, LANES)

    grads, deltas, new_m, new_v = {}, {}, {}, {}
    for n in owned_shape:
        shape = weights[n].shape
        res = _adamw(
            landed[n], owned[n], first[n].reshape(owned_shape[n]), second[n].reshape(owned_shape[n]), "adamw_" + n
        )
        grads[n], deltas[n], new_m[n], new_v[n] = (a.reshape(shape) for a in res)
    rows_of = lambda tree, n: tree[n].reshape(-1, 128)
    for names, parts, name in (
        ([n for n in SMALL if n != "norm_pre"], landed_packed, "adamw_replicated"),
        (["norm_pre"], landed_norm_pre, "adamw_norm_pre"),
    ):
        res, rest = _adamw_packed(
            parts, [(rows_of(weights, n), rows_of(first, n), rows_of(second, n)) for n in names], name
        )
        if rest is not None:
            total = rest[0, 0]
        for n, four in zip(names, res):
            for tree, a in zip((grads, deltas, new_m, new_v), four):
                tree[n] = a.reshape(weights[n].shape)

    return (
        total,
        grad_x[None],
        *[grads[n] for n in order],
        *[deltas[n] for n in order],
        *[new_m[n] for n in order],
        *[new_v[n] for n in order],
    )
```

```python
import jax
import jax.numpy as jnp
from jax import lax
from jax.experimental import pallas as pl
from jax.experimental.pallas import tpu as pltpu

F32 = jnp.float32
BF16 = jnp.bfloat16
EPS = 1e-6

D_MODEL = 2048
POOL_WINDOWS = (2, 4, 8, 16)
POOL_GROUP_DIM = 256
BRANCH = 1024
SGU_CHUNK = 128
SGU_HEADS = 8
SGU_HEAD_DIM = 128
XATTN_HEADS = 4
XATTN_HEAD_DIM = 256
MIX_WIDTH = 3 * BRANCH
IN_WIDTH = 7 * BRANCH
N_DEV = 8
WIN_BLK = IN_WIDTH // N_DEV
WOUT_BLK = MIX_WIDTH // N_DEV
WKV_BLK = D_MODEL // N_DEV
POOL_BLK = POOL_GROUP_DIM // N_DEV
HALO = 16
LANES = 128
SUBLANES = 8

ADAM_LR = 0.001
ADAM_B1 = 0.9
ADAM_B2 = 0.999
ADAM_EPS = 1e-08
ADAM_WD = 0.01
ADAM_STEP = 10

VMEM_LIMIT_BYTES = 56 * 1024 * 1024
VMEM_LIMIT_MIX_BYTES = 63 * 1024 * 1024

TILE_PROJ = 512
TILE_MIX = 128
TILE_GRAD = 512
TILE_WEIGHT_GRAD = 1024
TILE_ADAM_BYTES = 1 << 20

ANY = pl.BlockSpec(memory_space=pl.ANY)
NN = (((1,), (0,)), ((), ()))
NT = (((1,), (1,)), ((), ()))
TN = (((0,), (0,)), ((), ()))
MESH = pl.DeviceIdType.MESH


def _dot(a, b, dims=NN):
    return lax.dot_general(a, b, dims, preferred_element_type=F32)


def _params(*semantics, vmem_limit_bytes=VMEM_LIMIT_BYTES):
    return pltpu.CompilerParams(dimension_semantics=semantics, vmem_limit_bytes=vmem_limit_bytes)


def _rowmean(a):
    return jnp.mean(a, axis=-1, keepdims=True)


def _colsum(a):
    return jnp.sum(a, axis=0, keepdims=True)


def _full(shape):
    zeros = (0,) * len(shape)
    return pl.BlockSpec(shape, lambda *_: zeros)


def _resident(shape):
    zeros = (0,) * len(shape)
    return pl.BlockSpec(shape, lambda *_: zeros, pipeline_mode=pl.Buffered(1))


def _kv_forward(mem, mem_norm, wkv):
    m, d = mem.shape
    n = wkv.shape[1]
    cols = 4 * LANES

    def body(mem_ref, g_ref, w_ref, memn_ref, kv_ref, kvt_ref):
        mv = mem_ref[...]
        r = lax.rsqrt(_rowmean(mv * mv) + EPS)
        memn = (mv * r * g_ref[...]).astype(BF16)
        memn_ref[...] = memn
        kv = _dot(memn, w_ref[...])
        kv_ref[...] = kv.astype(BF16)
        kvt_ref[...] = kv.T.astype(BF16)

    return pl.pallas_call(
        body,
        name="kv_forward",
        grid=(n // cols,),
        in_specs=[_full((m, d)), _full((1, d)), pl.BlockSpec((d, cols), lambda j: (0, j))],
        out_specs=[_full((m, d)), pl.BlockSpec((m, cols), lambda j: (0, j)), pl.BlockSpec((cols, m), lambda j: (j, 0))],
        out_shape=[
            jax.ShapeDtypeStruct((m, d), BF16), jax.ShapeDtypeStruct((m, n), BF16), jax.ShapeDtypeStruct((n, m), BF16)
        ],
        compiler_params=_params("arbitrary"),
    )(mem, mem_norm, wkv)


def _proj_gather(x_in, norm_pre, shards):
    s, d = x_in.shape
    t = min(TILE_PROJ, s)
    n_t = s // t
    n_arr = len(shards)

    def places(x, y, c):
        return (x, y, c), (x, y, 1 - c), (x ^ c, y ^ (1 - c)), (x ^ (1 - c), y ^ c), (1 - x, 1 - y)

    def index(chip, core):
        return 4 * chip[0] + 2 * chip[1] + core

    _, _, chip_a, chip_b, chip_d = places(*_position())
    c_out = lax.axis_index("c")
    me_out = index((lax.axis_index("x"), lax.axis_index("y")), c_out)
    order = jnp.stack(
        [
            me_out, me_out ^ 1, index(chip_a, c_out), index(chip_b, c_out), index(chip_b, 1 - c_out),
            index(chip_a, 1 - c_out), index(chip_d, c_out), index(chip_d, 1 - c_out),
        ]
    ).astype(jnp.int32)

    def body(order_ref, x_ref, g_ref, *refs):
        del order_ref
        raw = refs[:n_arr]
        proj_ref, h_ref = refs[n_arr : n_arr + 2]
        out = refs[n_arr + 2 : 2 * n_arr + 2]
        staged = refs[2 * n_arr + 2 : 3 * n_arr + 1]
        src = (raw[0],) + tuple(staged)
        wbuf, hs, send_sems, recv_sems, local_sems, load_sems = refs[3 * n_arr + 1 : 3 * n_arr + 7]
        wide = refs[3 * n_arr + 7 : 4 * n_arr + 6]
        narrow = refs[4 * n_arr + 6 :]
        j = pl.program_id(0)
        i = pl.program_id(1)
        me, sibling, chip_a, chip_b, chip_d = places(*_position())
        c = me[2]

        def block(a, chip, core):
            return out[a].at[index(chip, core)]

        def copy(a, k, owner, to, from_input=False):
            return pltpu.make_async_remote_copy(
                src_ref=src[a] if from_input else block(a, *owner),
                dst_ref=block(a, *owner),
                send_sem=send_sems.at[a, k],
                recv_sem=recv_sems.at[a, k],
                device_id=to,
                device_id_type=MESH,
            )

        mine = (me[:2], c)

        def own(a):
            return pltpu.make_async_copy(src[a], block(a, *mine), local_sems.at[a])

        def first_sends(a):
            return [
                copy(a, 0, mine, sibling, from_input=True),
                copy(a, 1, mine, (*chip_a, c), from_input=True),
                copy(a, 2, mine, (*chip_b, c), from_input=True),
            ]

        def onward(a, k):
            owner = {3: chip_a, 4: chip_a, 5: chip_b, 6: chip_d}[k]
            return copy(a, k, (owner, c), (*chip_b, c) if k == 3 else sibling)

        def landed(a, k):
            owner = {0: mine[0], 1: chip_a, 2: chip_b, 3: chip_d, 4: chip_b, 5: chip_a, 6: chip_d}[k]
            core = c if k in (1, 2, 3) else 1 - c
            copy(a, k, (owner, core), me).wait_recv()
            return owner, core

        def load(ref, step):
            return pltpu.make_async_copy(ref, wbuf.at[step % 2], load_sems.at[step % 2])

        @pl.when(jnp.logical_and(j == 0, i == 0))
        def _():
            own(0).start()
            for cp in first_sends(0):
                cp.start()
            for a in range(1, n_arr):
                pltpu.sync_copy(raw[a], wide[a - 1])
                narrow[a - 1][...] = wide[a - 1][...].astype(BF16)
                pltpu.sync_copy(narrow[a - 1], staged[a - 1])
            load(src[0], 0).start()
            load(src[0], 0).wait()

        steps = {1: (0, ()), 2: (1, (3, 4)), 3: (2, (5,)), 4: (4, ()), 5: (5, ()), 6: (3, (6,)), 7: (6, ())}
        for step, (k, then) in steps.items():

            @pl.when(jnp.logical_and(j == step, i == 0))
            def _():
                load(src[0], step).wait()

            @pl.when(jnp.logical_and(j == step - 1, i == n_t - 1))
            def _():
                owner = landed(0, k)
                for k2 in then:
                    onward(0, k2).start()
                if k == 1:
                    for a in range(1, n_arr):
                        own(a).start()
                        for cp in first_sends(a):
                            cp.start()
                if k == 3:
                    for a in range(1, n_arr):
                        for k1, then1 in ((1, (3, 4)), (2, (5,))):
                            landed(a, k1)
                            for k2 in then1:
                                onward(a, k2).start()
                load(block(0, *owner), step).start()

        @pl.when(j == 0)
        def _():
            xv = x_ref[...]
            h = (xv * lax.rsqrt(_rowmean(xv * xv) + EPS) * g_ref[...]).astype(BF16)
            hs[i] = h
            h_ref[...] = h

        proj_ref[...] = _dot(hs[i], wbuf[j % 2]).astype(BF16)

        @pl.when(jnp.logical_and(j == N_DEV - 1, i == n_t - 1))
        def _():
            for a in range(1, n_arr):
                landed(a, 3)
                onward(a, 6).start()
            for a in range(1, n_arr):
                for k in (0, 4, 5, 6):
                    landed(a, k)
            for a in range(n_arr):
                for cp in first_sends(a) + [onward(a, k) for k in (3, 4, 5, 6)]:
                    cp.wait_send()
                own(a).wait()

    res = pl.pallas_call(
        body,
        name="proj_gather",
        grid_spec=pltpu.PrefetchScalarGridSpec(
            num_scalar_prefetch=1,
            grid=(N_DEV, n_t),
            in_specs=[
                pl.BlockSpec((t, d), lambda j, i, order_ref: (jnp.where(j == 0, i, n_t - 1), 0)),
                pl.BlockSpec((1, d), lambda j, i, order_ref: (0, 0)),
            ]
            + [ANY] * n_arr,
            out_specs=[
                pl.BlockSpec((t, WIN_BLK), lambda j, i, order_ref: (i, order_ref[j])),
                pl.BlockSpec((t, d), lambda j, i, order_ref: (jnp.where(j == 0, i, n_t - 1), 0)),
            ]
            + [ANY] * (2 * n_arr - 1),
            scratch_shapes=[
                pltpu.VMEM((2,) + shards[0].shape, BF16),
                pltpu.VMEM((n_t, t, d), BF16),
                pltpu.SemaphoreType.DMA((n_arr, 7)),
                pltpu.SemaphoreType.DMA((n_arr, 7)),
                pltpu.SemaphoreType.DMA((n_arr,)),
                pltpu.SemaphoreType.DMA((2,)),
            ]
            + [pltpu.VMEM(a.shape, F32) for a in shards[1:]]
            + [pltpu.VMEM(a.shape, BF16) for a in shards[1:]],
        ),
        out_shape=[jax.ShapeDtypeStruct((s, IN_WIDTH), BF16), jax.ShapeDtypeStruct((s, d), BF16)]
        + [jax.ShapeDtypeStruct((N_DEV,) + a.shape, BF16) for a in shards]
        + [jax.ShapeDtypeStruct(a.shape, BF16) for a in shards[1:]],
        compiler_params=_params("arbitrary", "arbitrary"),
    )(order, x_in, norm_pre, *shards)
    return res[0], res[1], res[2 : 2 + n_arr]


def _sigmoid(a):
    return jax.nn.sigmoid(a)


def _dsilu(a, sg):
    return sg * (1.0 + a * (1.0 - sg))


def _rms_fwd(u, gain):
    r = lax.rsqrt(_rowmean(u * u) + EPS)
    n = u * r
    return r, n, n * gain


def _rms_bwd(dy, gain, r, n):
    dn = dy * gain
    return _colsum(dy * n), r * (dn - n * _rowmean(dn * n))


def _mix(proj, x, target, kv, kv_t, wout, wout_t, pool_w, pool_w_t, pool_scale, ln_g, ln_b, sgu_bias, sgu_wm, sgu_wm_t, branch_norm, norm_post):
    s, d = x.shape
    t = min(TILE_MIX, s)
    n_tiles = s // t
    n_chunks = t // SGU_CHUNK
    halo_blocks_per_tile = t // HALO
    inv_d = 1.0 / d
    scale = 1.0 / (XATTN_HEAD_DIM**0.5)

    def body(
        proj_ref, halo_ref, x_ref, tgt_ref, kv_ref, kvt_ref, wout_hbm, wout_t_hbm, pw_ref, pwt_ref, pscale_ref, lng_ref,
        lnb_ref, bias_ref, wm_ref, wmt_ref, bnorm_ref, gpost_ref,
        y_ref, dout_ref, dxo_ref, dproj_ref, loss_ref, dgpost_ref, dbnorm_ref, dpscale_ref, dlng_ref, dlnb_ref,
        dpw_out, dwm_out, dbias_ref, dkv_out,
        carry_ref, dzsum_ref, dpw_ref, dwm_ref, dkv_ref, wout_ref, wout_t_ref, wout_sems,
    ):
        i = pl.program_id(0)
        tile = n_tiles - 1 - i
        wout_load = pltpu.make_async_copy(wout_hbm, wout_ref, wout_sems.at[0])
        wout_t_load = pltpu.make_async_copy(wout_t_hbm, wout_t_ref, wout_sems.at[1])

        @pl.when(i == 0)
        def _():
            wout_load.start()
            wout_t_load.start()
            carry_ref[...] = jnp.zeros_like(carry_ref)
            dzsum_ref[...] = jnp.zeros_like(dzsum_ref)
            for ref in (loss_ref, dgpost_ref, dbnorm_ref, dpscale_ref, dlng_ref, dlnb_ref, dpw_ref, dwm_ref, dkv_ref):
                ref[...] = jnp.zeros_like(ref)

        t_glob = tile * t + lax.broadcasted_iota(jnp.int32, (t, 1), 0)
        inv_cnt = [1.0 / jnp.minimum(t_glob + 1, w).astype(F32) for w in POOL_WINDOWS]

        xa = proj_ref[:, 0:BRANCH].astype(F32)
        ga = proj_ref[:, BRANCH : 2 * BRANCH].astype(F32)
        halo = jnp.where(tile == 0, 0.0, halo_ref[...].astype(F32))
        d_bf, pm_parts = [], []
        for g, w in enumerate(POOL_WINDOWS):
            cols = slice(g * POOL_GROUP_DIM, (g + 1) * POOL_GROUP_DIM)
            acc = jnp.concatenate([halo[:, cols], xa[:, cols]], axis=0)
            k = 1
            while k < w:
                acc = acc + pltpu.roll(acc, k, axis=0)
                k *= 2
            dg = (acc[HALO:, :] * inv_cnt[g] - xa[:, cols]).astype(BF16)
            d_bf.append(dg)
            pm_parts.append(_dot(dg, pw_ref[g]))
        pm = jnp.concatenate(pm_parts, axis=1)
        pscale = pscale_ref[...]
        pa = pm * pscale
        sga = _sigmoid(ga)
        sila = ga * sga
        ua = pa * sila
        g_a = bnorm_ref[:, 0:BRANCH]
        ra, na, ya = _rms_fwd(ua, g_a)

        u = proj_ref[:, 2 * BRANCH : 3 * BRANCH].astype(F32)
        v = proj_ref[:, 3 * BRANCH : 4 * BRANCH].astype(F32)
        gb = proj_ref[:, 4 * BRANCH : 5 * BRANCH].astype(F32)
        lng = lng_ref[...]
        vc = v - _rowmean(v)
        rstd = lax.rsqrt(_rowmean(vc * vc) + EPS)
        vhat = vc * rstd
        vn_bf = (vhat * lng + lnb_ref[...]).astype(BF16)
        z_rows = []
        for c in range(n_chunks):
            rows = slice(c * SGU_CHUNK, (c + 1) * SGU_CHUNK)
            z_rows.append(
                jnp.concatenate(
                    [
                        _dot(wm_ref[hd], vn_bf[rows, hd * SGU_HEAD_DIM : (hd + 1) * SGU_HEAD_DIM])
                        for hd in range(SGU_HEADS)
                    ],
                    axis=1,
                )
                + bias_ref[...]
            )
        z = z_rows[0] if n_chunks == 1 else jnp.concatenate(z_rows, axis=0)
        sb = u * z
        sgb = _sigmoid(gb)
        silb = gb * sgb
        ub = sb * silb
        g_b = bnorm_ref[:, BRANCH : 2 * BRANCH]
        rb, nb, yb = _rms_fwd(ub, g_b)

        q = proj_ref[:, 5 * BRANCH : 6 * BRANCH]
        gc = proj_ref[:, 6 * BRANCH : 7 * BRANCH].astype(F32)
        q_bf, p_bf, o_parts = [], [], []
        for hd in range(XATTN_HEADS):
            cols = slice(hd * XATTN_HEAD_DIM, (hd + 1) * XATTN_HEAD_DIM)
            qh = q[:, cols]
            sc = _dot(qh, kvt_ref[cols, :]) * scale
            e = jnp.exp(sc - jnp.max(sc, axis=-1, keepdims=True))
            p = e / jnp.sum(e, axis=-1, keepdims=True)
            q_bf.append(qh)
            p_bf.append(p.astype(BF16))
            o_parts.append(_dot(p_bf[hd], kv_ref[:, BRANCH + hd * XATTN_HEAD_DIM : BRANCH + (hd + 1) * XATTN_HEAD_DIM]))
        o = jnp.concatenate(o_parts, axis=1)
        sgc = _sigmoid(gc)
        silc = gc * sgc
        uc = o * silc
        g_c = bnorm_ref[:, 2 * BRANCH : 3 * BRANCH]
        rc, nc, yc = _rms_fwd(uc, g_c)

        @pl.when(i == 0)
        def _():
            wout_load.wait()

        out = None
        for b, y_branch in enumerate((ya, yb, yc)):
            rows = slice(b * BRANCH, (b + 1) * BRANCH)
            y_bf = y_branch.astype(BF16)
            y_ref[:, rows] = y_bf
            part = _dot(y_bf, wout_ref[rows, :])
            out = part if out is None else out + part
        gpost = gpost_ref[...]
        r_out = lax.rsqrt(_rowmean(out * out) + EPS)
        on = out * r_out
        err = x_ref[...] + on * gpost - tgt_ref[...]
        loss_ref[...] += 0.5 * jnp.sum(_rowmean(err * err), axis=0, keepdims=True)

        dxo = err * inv_d
        dxo_ref[...] = dxo
        dgp, dout = _rms_bwd(dxo, gpost, r_out, on)
        dgpost_ref[...] += dgp
        dout_bf = dout.astype(BF16)
        dout_ref[...] = dout_bf

        @pl.when(i == 0)
        def _():
            wout_t_load.wait()

        dy = [_dot(dout_bf, wout_t_ref[:, b * BRANCH : (b + 1) * BRANCH]) for b in range(3)]

        dg_a, dua = _rms_bwd(dy[0], g_a, ra, na)
        dg_b, dub = _rms_bwd(dy[1], g_b, rb, nb)
        dg_c, duc = _rms_bwd(dy[2], g_c, rc, nc)
        dbnorm_ref[...] += jnp.concatenate([dg_a, dg_b, dg_c], axis=1)

        dpa = dua * sila
        dga = dua * pa * _dsilu(ga, sga)
        dpscale_ref[...] += _colsum(dpa * pm)
        dpm = dpa * pscale
        dxa_parts, carry_parts = [], []
        for g, w in enumerate(POOL_WINDOWS):
            cols = slice(g * POOL_GROUP_DIM, (g + 1) * POOL_GROUP_DIM)
            dpm_g = dpm[:, cols].astype(BF16)
            dd = _dot(dpm_g, pwt_ref[g])
            dpw_ref[g] += _dot(d_bf[g], dpm_g, TN)
            cg = dd * inv_cnt[g]
            carry_parts.append(cg[0:HALO, :])
            acc = jnp.concatenate([cg, carry_ref[:, cols]], axis=0)
            k = 1
            while k < w:
                acc = acc + pltpu.roll(acc, t + HALO - k, axis=0)
                k *= 2
            dxa_parts.append(acc[0:t, :] - dd)
        carry_ref[...] = jnp.concatenate(carry_parts, axis=1)
        dxa = jnp.concatenate(dxa_parts, axis=1)

        dsb = dub * silb
        dgb = dub * sb * _dsilu(gb, sgb)
        du = dsb * z
        dz = dsb * u
        dz_bf = dz.astype(BF16)
        dvn_rows = []
        dz_sum = None
        for c in range(n_chunks):
            rows = slice(c * SGU_CHUNK, (c + 1) * SGU_CHUNK)
            dz_sum = dz[rows, :] if dz_sum is None else dz_sum + dz[rows, :]
            parts = []
            for hd in range(SGU_HEADS):
                cols = slice(hd * SGU_HEAD_DIM, (hd + 1) * SGU_HEAD_DIM)
                parts.append(_dot(wmt_ref[hd], dz_bf[rows, cols]))
                dwm_ref[hd] += _dot(dz_bf[rows, cols], vn_bf[rows, cols], NT)
            dvn_rows.append(jnp.concatenate(parts, axis=1))
        dzsum_ref[...] += dz_sum
        dvn = dvn_rows[0] if n_chunks == 1 else jnp.concatenate(dvn_rows, axis=0)
        dlng_ref[...] += _colsum(dvn * vhat)
        dlnb_ref[...] += _colsum(dvn)
        dvh = dvn * lng
        dv = rstd * (dvh - _rowmean(dvh) - vhat * _rowmean(dvh * vhat))

        do = duc * silc
        dgc = duc * o * _dsilu(gc, sgc)
        dq_parts = []
        for hd in range(XATTN_HEADS):
            cols = slice(hd * XATTN_HEAD_DIM, (hd + 1) * XATTN_HEAD_DIM)
            vcols = slice(BRANCH + hd * XATTN_HEAD_DIM, BRANCH + (hd + 1) * XATTN_HEAD_DIM)
            do_h = do[:, cols].astype(BF16)
            p = p_bf[hd].astype(F32)
            dp = _dot(do_h, kvt_ref[vcols, :])
            dkv_ref[:, vcols] += _dot(p_bf[hd], do_h, TN)
            ds_bf = (p * (dp - jnp.sum(dp * p, axis=-1, keepdims=True)) * scale).astype(BF16)
            dq_parts.append(_dot(ds_bf, kv_ref[:, cols]))
            dkv_ref[:, cols] += _dot(ds_bf, q_bf[hd], TN)
        dq = jnp.concatenate(dq_parts, axis=1)

        dproj_ref[...] = jnp.concatenate([dxa, dga, du, dv, dgb, dq, dgc], axis=1).astype(BF16)

        @pl.when(i == n_tiles - 1)
        def _():
            keep = lax.broadcasted_iota(jnp.int32, (SGU_CHUNK, SGU_CHUNK), 0) >= lax.broadcasted_iota(
                jnp.int32, (SGU_CHUNK, SGU_CHUNK), 1
            )
            for hd in range(SGU_HEADS):
                dwm_ref[hd] = jnp.where(keep, dwm_ref[hd], 0.0)
                per_pos = dzsum_ref[:, hd * SGU_HEAD_DIM : (hd + 1) * SGU_HEAD_DIM]
                dbias_ref[hd : hd + 1, :] = _colsum(per_pos.T)
            for acc, res in ((dpw_ref, dpw_out), (dwm_ref, dwm_out), (dkv_ref, dkv_out)):
                pltpu.sync_copy(acc, res)

    row_tile = lambda width: pl.BlockSpec((t, width), lambda i: (n_tiles - 1 - i, 0))
    halo_spec = pl.BlockSpec(
        (HALO, BRANCH), lambda i: (jnp.maximum((n_tiles - 1 - i) * halo_blocks_per_tile - 1, 0), 0)
    )
    acc_shapes = [
        (1, 128),
        (1, d),
        (1, MIX_WIDTH),
        (1, BRANCH),
        (1, BRANCH),
        (1, BRANCH),
        pool_w.shape,
        sgu_wm.shape,
        (SGU_HEADS, SGU_CHUNK),
        kv.shape,
    ]
    return pl.pallas_call(
        body,
        name="mix",
        grid=(n_tiles,),
        in_specs=[
            row_tile(IN_WIDTH), halo_spec, row_tile(d), row_tile(d), _resident(kv.shape), _resident(kv_t.shape),
            ANY, ANY, _resident(pool_w.shape), _resident(pool_w_t.shape),
            _full((1, BRANCH)), _full((1, BRANCH)), _full((1, BRANCH)), _resident((SGU_CHUNK, BRANCH)),
            _resident(sgu_wm.shape), _resident(sgu_wm_t.shape), _full((1, MIX_WIDTH)), _full((1, d)),
        ],
        out_specs=[row_tile(MIX_WIDTH), row_tile(d), row_tile(d), row_tile(IN_WIDTH)]
        + [ANY if len(a) == 3 or a == kv.shape else _full(a) for a in acc_shapes],
        out_shape=[
            jax.ShapeDtypeStruct((s, MIX_WIDTH), BF16),
            jax.ShapeDtypeStruct((s, d), BF16),
            jax.ShapeDtypeStruct((s, d), F32),
            jax.ShapeDtypeStruct((s, IN_WIDTH), BF16),
        ]
        + [jax.ShapeDtypeStruct(a, F32) for a in acc_shapes],
        scratch_shapes=[
            pltpu.VMEM((HALO, BRANCH), F32), pltpu.VMEM((SGU_CHUNK, BRANCH), F32), pltpu.VMEM(pool_w.shape, F32),
            pltpu.VMEM(sgu_wm.shape, F32), pltpu.VMEM(kv.shape, F32), pltpu.VMEM(wout.shape, BF16),
            pltpu.VMEM(wout_t.shape, BF16), pltpu.SemaphoreType.DMA((2,)),
        ],
        compiler_params=_params("arbitrary", vmem_limit_bytes=VMEM_LIMIT_MIX_BYTES),
    )(
        proj, proj, x, target, kv, kv_t, wout, wout_t, pool_w, pool_w_t, pool_scale, ln_g, ln_b, sgu_bias, sgu_wm,
        sgu_wm_t, branch_norm, norm_post,
    )


def _position():
    return lax.axis_index("x"), lax.axis_index("y"), lax.axis_index("c")


N_CHIPS = 4


def _landing_shape(kind, a):
    return (N_CHIPS,) + a.shape[2:] if kind == "pair" else a.shape


def _carry_specs(groups):
    arrays = [(kind, a) for kind, arrs in groups for a in arrs]
    scratch = []
    for _, arrs in groups:
        n = len(arrs)
        scratch += [pltpu.SemaphoreType.DMA((n, N_DEV)), pltpu.SemaphoreType.DMA((n, N_DEV)), pltpu.SemaphoreType.DMA((n,))]
    return dict(
        n=len(arrays),
        operands=[a for _, a in arrays],
        in_specs=[ANY] * len(arrays),
        out_specs=[ANY] * len(arrays),
        out_shape=[jax.ShapeDtypeStruct(_landing_shape(kind, a), a.dtype) for kind, a in arrays],
        scratch_shapes=scratch,
    )


def _carry(groups, src, out, sems):
    x, y, c = _position()
    chip = 2 * x + y

    def remote(s, d, send_sems, recv_sems, a, m, to):
        return pltpu.make_async_remote_copy(
            src_ref=s, dst_ref=d, send_sem=send_sems.at[a, m], recv_sem=recv_sems.at[a, m], device_id=to,
            device_id_type=MESH,
        )

    def copies():
        far, near = [], []
        at = 0
        for g, (kind, arrs) in enumerate(groups):
            send_sems, recv_sems, local_sems = sems[3 * g : 3 * g + 3]
            for a in range(len(arrs)):
                s, d = src[at + a], out[at + a]
                if kind == "pair" and s.shape[0] == 1:
                    for b in range(N_CHIPS):
                        far.append(remote(s.at[0, 0], d.at[b], send_sems, recv_sems, a, 1 + b, (x, y, 1 - c)))
                elif kind == "pair":
                    far.append(remote(s.at[:, 1 - c], d, send_sems, recv_sems, a, 1, (x, y, 1 - c)))
                else:
                    assert kind == "chips", kind
                    for m in range(1, N_CHIPS):
                        px, py = x ^ (m >> 1), y ^ (m & 1)
                        far.append(remote(s.at[2 * px + py], d.at[chip], send_sems, recv_sems, a, m, (px, py, c)))
                    near.append(pltpu.make_async_copy(s.at[chip], d.at[chip], local_sems.at[a]))
            at += len(arrs)
        return far, near

    def start():
        far, near = copies()
        for cp in near + far:
            cp.start()

    def finish():
        far, near = copies()
        for cp in far:
            cp.wait_recv()
        for cp in far:
            cp.wait_send()
        for cp in near:
            cp.wait()

    return start, finish


def _pair_sum(mine, theirs, name, groups=()):
    n = len(mine)
    carried = _carry_specs(groups)
    n_c = carried["n"]
    core = lax.axis_index("c").astype(jnp.int32).reshape(1)

    def body(core_ref, *refs):
        del core_ref
        own = refs[:n]
        sib = refs[n : 2 * n]
        src = refs[2 * n : 2 * n + n_c]
        out = refs[2 * n + n_c : 3 * n + n_c]
        landed = refs[3 * n + n_c : 3 * n + 2 * n_c]
        start, finish = _carry(groups, src, landed, refs[3 * n + 2 * n_c :])
        b = pl.program_id(0)

        @pl.when(b == 0)
        def _():
            start()

        for a in range(n):
            out[a][...] = (own[a][...].astype(F32) + sib[a][...].astype(F32)).astype(out[a].dtype)

        @pl.when(b == N_CHIPS - 1)
        def _():
            finish()

    block = lambda a: pl.BlockSpec((None,) + a.shape[1:], lambda b, core_ref: (b, 0, 0))
    res = pl.pallas_call(
        body,
        name=name,
        grid_spec=pltpu.PrefetchScalarGridSpec(
            num_scalar_prefetch=1,
            grid=(N_CHIPS,),
            in_specs=[
                pl.BlockSpec((None, None) + a.shape[2:], lambda b, core_ref: (0, 0, 0, 0))
                if a.shape[0] == 1
                else pl.BlockSpec((None, None) + a.shape[2:], lambda b, core_ref: (b, core_ref[0], 0, 0))
                for a in mine
            ]
            + [block(a) for a in theirs]
            + carried["in_specs"],
            out_specs=[block(a) for a in theirs] + carried["out_specs"],
            scratch_shapes=carried["scratch_shapes"],
        ),
        out_shape=[jax.ShapeDtypeStruct(a.shape, a.dtype) for a in theirs] + carried["out_shape"],
        compiler_params=_params("arbitrary"),
    )(core, *mine, *theirs, *carried["operands"])
    return res[:n], res[n:]


def _weight_grad(a, b, n_blk, blocked, name, groups):
    s = a.shape[0]
    t = min(TILE_WEIGHT_GRAD, s)
    n_t = s // t
    n_pairs = n_blk // 2
    if blocked == "cols":
        k, c = a.shape[1], b.shape[1] // n_blk
        a_spec = pl.BlockSpec((t, k), lambda j, i: (i, 0))
        b_spec = pl.BlockSpec((t, 2 * c), lambda j, i: (i, j))
        acc_shape = (k, 2 * c)
    else:
        k, c = a.shape[1] // n_blk, b.shape[1]
        a_spec = pl.BlockSpec((t, 2 * k), lambda j, i: (i, j))
        b_spec = pl.BlockSpec((t, c), lambda j, i: (i, 0))
        acc_shape = (2 * k, c)
    carried = _carry_specs(groups)
    n_p = carried["n"]

    def body(a_ref, b_ref, *refs):
        src = refs[:n_p]
        o_ref, theirs_ref = refs[n_p : n_p + 2]
        landed = refs[n_p + 2 : 2 * n_p + 2]
        acc_ref, sbuf, pair_send, pair_recv = refs[2 * n_p + 2 : 2 * n_p + 6]
        start, finish = _carry(groups, src, landed, refs[2 * n_p + 6 :])
        j = pl.program_id(0)
        i = pl.program_id(1)
        x, y, c_me = _position()

        def to_sibling(pair):
            return pltpu.make_async_remote_copy(
                src_ref=sbuf.at[1 - c_me], dst_ref=theirs_ref.at[pair], send_sem=pair_send.at[pair],
                recv_sem=pair_recv.at[pair], device_id=(x, y, 1 - c_me), device_id_type=MESH,
            )

        @pl.when(jnp.logical_and(j == 0, i == 0))
        def _():
            start()

        @pl.when(i == 0)
        def _():
            acc_ref[...] = jnp.zeros_like(acc_ref)

        acc_ref[...] += _dot(a_ref[...], b_ref[...], TN)

        @pl.when(i == n_t - 1)
        def _():
            for pair in range(1, n_pairs):

                @pl.when(j == pair)
                def _():
                    to_sibling(pair - 1).wait_send()

            for half in range(2):
                if blocked == "cols":
                    block = acc_ref[:, half * c : (half + 1) * c].astype(BF16)
                else:
                    block = acc_ref[half * k : (half + 1) * k, :].astype(BF16)
                o_ref[half] = block
                sbuf[half] = block
            for pair in range(n_pairs):

                @pl.when(j == pair)
                def _():
                    to_sibling(pair).start()

        @pl.when(jnp.logical_and(j == n_pairs - 1, i == n_t - 1))
        def _():
            to_sibling(n_pairs - 1).wait_send()
            for pair in range(n_pairs):
                to_sibling(pair).wait_recv()
            finish()

    res = pl.pallas_call(
        body,
        name=name,
        grid=(n_pairs, n_t),
        in_specs=[a_spec, b_spec] + carried["in_specs"],
        out_specs=[pl.BlockSpec((2, k, c), lambda j, i: (j, 0, 0)), ANY] + carried["out_specs"],
        out_shape=[jax.ShapeDtypeStruct((n_blk, k, c), BF16), jax.ShapeDtypeStruct((n_pairs, k, c), BF16)]
        + carried["out_shape"],
        scratch_shapes=[
            pltpu.VMEM(acc_shape, F32), pltpu.VMEM((2, k, c), BF16), pltpu.SemaphoreType.DMA((n_pairs,)),
            pltpu.SemaphoreType.DMA((n_pairs,)),
        ]
        + carried["scratch_shapes"],
        compiler_params=_params("arbitrary", "arbitrary", vmem_limit_bytes=VMEM_LIMIT_MIX_BYTES),
    )(a, b, *carried["operands"])
    return res[0], res[1], res[2:]


HBM = pl.BlockSpec(memory_space=pltpu.HBM)
SEMAPHORES = pl.BlockSpec(memory_space=pltpu.SEMAPHORE)
DATAFLOW = pltpu.SideEffectType.DATAFLOW_SIDE_EFFECTING


def _chip_copies(v_ref, land_ref, send_sems, recv_sems):
    x, y, c = _position()
    chip = 2 * x + y
    return chip, [
        pltpu.make_async_remote_copy(
            src_ref=v_ref.at[2 * (x ^ (m >> 1)) + (y ^ (m & 1))], dst_ref=land_ref.at[chip],
            send_sem=send_sems.at[m - 1], recv_sem=recv_sems.at[m - 1],
            device_id=(x ^ (m >> 1), y ^ (m & 1), c), device_id_type=MESH,
        )
        for m in range(1, N_CHIPS)
    ]


def _chips_start(sums):
    def body(v_ref, land_ref, send_sems, recv_sems, v_thru, land_thru, token, own_sem):
        del v_thru, land_thru
        chip, copies = _chip_copies(v_ref, land_ref, send_sems, recv_sems)
        own = pltpu.make_async_copy(v_ref.at[chip], land_ref.at[chip], own_sem)
        own.start()
        own.wait()
        for cp in copies:
            cp.start()
        token[...] = jnp.zeros_like(token)

    sems = pltpu.SemaphoreType.DMA((N_CHIPS - 1,))
    return pl.pallas_call(
        body,
        name="chips_start",
        out_shape=(
            sems, sems, pltpu.HBM(sums.shape, sums.dtype), pltpu.HBM(sums.shape, sums.dtype),
            jax.ShapeDtypeStruct((SUBLANES, LANES), F32),
        ),
        in_specs=(HBM, HBM),
        out_specs=(SEMAPHORES, SEMAPHORES, HBM, HBM, pl.BlockSpec(memory_space=pltpu.VMEM)),
        input_output_aliases={0: 2, 1: 3},
        scratch_shapes=[pltpu.SemaphoreType.DMA],
        compiler_params=pltpu.CompilerParams(has_side_effects=DATAFLOW),
    )(
        pltpu.with_memory_space_constraint(sums, pltpu.HBM),
        pltpu.with_memory_space_constraint(lax.empty(sums.shape, sums.dtype), pltpu.HBM),
    )


def _chips_wait(send_sems, recv_sems, v_thru, land_thru, after):
    def body(v_ref, land_ref, send_sems, recv_sems, after_ref, v_dead, got_ref):
        del after_ref, v_dead, got_ref
        _, copies = _chip_copies(v_ref, land_ref, send_sems, recv_sems)
        for cp in copies:
            cp.wait_send()
        for cp in copies:
            cp.wait_recv()

    return pl.pallas_call(
        body,
        name="chips_wait",
        out_shape=(pltpu.HBM(v_thru.shape, v_thru.dtype), pltpu.HBM(v_thru.shape, v_thru.dtype)),
        in_specs=(HBM, HBM, SEMAPHORES, SEMAPHORES, ANY),
        out_specs=(HBM, HBM),
        input_output_aliases={0: 0, 1: 1},
        compiler_params=pltpu.CompilerParams(has_side_effects=DATAFLOW),
    )(v_thru, land_thru, send_sems, recv_sems, after)[1]


def _exchange_all(part, name):
    def body(p_ref, o_ref, send_sems, recv_sems, own_sem):
        x, y, c = _position()
        me = 4 * x + 2 * y + c
        far = [
            pltpu.make_async_remote_copy(
                src_ref=p_ref, dst_ref=o_ref.at[me], send_sem=send_sems.at[m], recv_sem=recv_sems.at[m],
                device_id=(x ^ ((m >> 2) & 1), y ^ ((m >> 1) & 1), c ^ (m & 1)), device_id_type=MESH,
            )
            for m in range(1, N_DEV)
        ]
        own = pltpu.make_async_copy(p_ref, o_ref.at[me], own_sem)
        for cp in [own] + far:
            cp.start()
        for cp in far:
            cp.wait_recv()
        for cp in far:
            cp.wait_send()
        own.wait()

    return pl.pallas_call(
        body,
        name=name,
        in_specs=[ANY],
        out_specs=ANY,
        out_shape=jax.ShapeDtypeStruct((N_DEV,) + part.shape, part.dtype),
        scratch_shapes=[pltpu.SemaphoreType.DMA((N_DEV,)), pltpu.SemaphoreType.DMA((N_DEV,)), pltpu.SemaphoreType.DMA],
    )(part)


def _input_grad(dproj, win_t, x, dxo, norm_pre):
    s, d = x.shape
    t = min(TILE_GRAD, s)
    n_t = s // t
    kb = 2 * WIN_BLK
    n_k = win_t.shape[0] // kb

    def body(dp_ref, w_ref, x_ref, dxo_ref, g_ref, gx_ref, dg_ref, acc_ref):
        i = pl.program_id(0)
        j = pl.program_id(1)

        @pl.when(jnp.logical_and(i == 0, j == 0))
        def _():
            dg_ref[...] = jnp.zeros_like(dg_ref)

        @pl.when(j == 0)
        def _():
            acc_ref[...] = jnp.zeros_like(acc_ref)

        acc_ref[...] += _dot(dp_ref[...], w_ref[...])

        @pl.when(j == n_k - 1)
        def _():
            xv = x_ref[...]
            gain = g_ref[...]
            r = lax.rsqrt(_rowmean(xv * xv) + EPS)
            dgain, dx = _rms_bwd(acc_ref[...], gain, r, xv * r)
            dg_ref[...] += dgain
            gx_ref[...] = dxo_ref[...] + dx

    return pl.pallas_call(
        body,
        name="input_grad",
        grid=(n_t, n_k),
        in_specs=[
            pl.BlockSpec((t, kb), lambda i, j: (i, j)),
            pl.BlockSpec((kb, d), lambda i, j: (j, 0)),
            pl.BlockSpec((t, d), lambda i, j: (i, 0)),
            pl.BlockSpec((t, d), lambda i, j: (i, 0)),
            _full((1, d)),
        ],
        out_specs=[pl.BlockSpec((t, d), lambda i, j: (i, 0)), _full((1, d))],
        out_shape=[jax.ShapeDtypeStruct((s, d), F32), jax.ShapeDtypeStruct((1, d), F32)],
        scratch_shapes=[pltpu.VMEM((t, d), F32)],
        compiler_params=_params("arbitrary", "arbitrary", vmem_limit_bytes=VMEM_LIMIT_MIX_BYTES),
    )(dproj, win_t, x, dxo, norm_pre)


def _kv_backward(dkv, memn, wkv, mem):
    m, d = mem.shape
    n = wkv.shape[1]

    def body(dkv_ref, memn_ref, w_ref, mem_ref, gw_ref, dg_ref):
        dkv_bf = dkv_ref[...].astype(BF16)
        gw_ref[...] = _dot(memn_ref[...], dkv_bf, TN).astype(BF16).reshape(N_DEV, WKV_BLK, n)
        dmemn = _dot(dkv_bf, w_ref[...], NT)
        mv = mem_ref[...]
        r = lax.rsqrt(_rowmean(mv * mv) + EPS)
        dg_ref[...] = _colsum(dmemn * (mv * r))

    return pl.pallas_call(
        body,
        name="kv_backward",
        grid=(1,),
        in_specs=[_full((m, n)), _full((m, d)), _full(wkv.shape), _full((m, d))],
        out_specs=[_full((N_DEV, WKV_BLK, n)), _full((1, d))],
        out_shape=[jax.ShapeDtypeStruct((N_DEV, WKV_BLK, n), BF16), jax.ShapeDtypeStruct((1, d), F32)],
        compiler_params=_params("arbitrary"),
    )(dkv, memn, wkv, mem)


def _adamw_math(w, g, m, v):
    m = ADAM_B1 * m + (1.0 - ADAM_B1) * g
    v = ADAM_B2 * v + (1.0 - ADAM_B2) * (g * g)
    m_hat = m / (1.0 - ADAM_B1**ADAM_STEP)
    v_hat = v / (1.0 - ADAM_B2**ADAM_STEP)
    delta = -ADAM_LR * (m_hat / (jnp.sqrt(v_hat) + ADAM_EPS) + ADAM_WD * w)
    return delta, m, v


def _adamw(parts, w, m, v, name):
    r, c = w.shape
    slots = parts.shape[0]
    t = r
    while t * c * 4 > TILE_ADAM_BYTES and t % 16 == 0:
        t //= 2

    def body(p_ref, w_ref, m_ref, v_ref, g_ref, d_ref, nm_ref, nv_ref):
        g = p_ref[0].astype(F32)
        for k in range(1, slots):
            g = g + p_ref[k].astype(F32)
        delta, nm, nv = _adamw_math(w_ref[...], g, m_ref[...], v_ref[...])
        g_ref[...] = g
        d_ref[...] = delta
        nm_ref[...] = nm
        nv_ref[...] = nv

    tile = pl.BlockSpec((t, c), lambda i: (i, 0))
    return pl.pallas_call(
        body,
        name=name,
        grid=(r // t,),
        in_specs=[pl.BlockSpec((slots, t, c), lambda i: (0, i, 0)), tile, tile, tile],
        out_specs=[tile] * 4,
        out_shape=[jax.ShapeDtypeStruct((r, c), F32)] * 4,
        compiler_params=_params("parallel"),
    )(parts, w, m, v)


def _adamw_packed(parts, triples, name):
    slots = parts.shape[0]
    sizes = [w.shape[0] for w, _, _ in triples]
    rest = parts.shape[1] - sum(sizes)

    def total(p_ref, at, rows):
        g = p_ref[0, at : at + rows, :]
        for k in range(1, slots):
            g = g + p_ref[k, at : at + rows, :]
        return g

    def body(p_ref, *refs):
        ins = refs[: 3 * len(triples)]
        outs = refs[3 * len(triples) :]
        at = 0
        for n, rows in enumerate(sizes):
            g = total(p_ref, at, rows)
            w_ref, m_ref, v_ref = ins[3 * n : 3 * n + 3]
            delta, nm, nv = _adamw_math(w_ref[...], g, m_ref[...], v_ref[...])
            for ref, val in zip(outs[4 * n : 4 * n + 4], (g, delta, nm, nv)):
                ref[...] = val
            at += rows
        if rest:
            outs[-1][...] = total(p_ref, at, rest)

    flat = [a for t in triples for a in t]
    res = pl.pallas_call(
        body,
        name=name,
        out_shape=[jax.ShapeDtypeStruct(w.shape, F32) for w, _, _ in triples for _ in range(4)]
        + ([jax.ShapeDtypeStruct((rest, 128), F32)] if rest else []),
        compiler_params=pltpu.CompilerParams(vmem_limit_bytes=VMEM_LIMIT_BYTES),
    )(parts, *flat)
    return [res[4 * n : 4 * n + 4] for n in range(len(triples))], (res[-1] if rest else None)


SMALL = ("norm_pre", "pool_scale", "sgu_ln_g", "sgu_ln_b", "sgu_w", "sgu_b", "mem_norm", "branch_norm", "norm_post")


def _local_view(name, w):
    if name == "sgu_w":
        return w.reshape(SGU_HEADS, SGU_CHUNK, SGU_CHUNK)
    if name == "sgu_b":
        return w.reshape(SGU_HEADS, SGU_CHUNK)
    return w.reshape(1, -1)


def _forward_backward(x, mem, target, shards, small):
    causal = jnp.tril(jnp.ones((SGU_CHUNK, SGU_CHUNK), dtype=bool))
    sgu_wm = jnp.where(causal[None], small["sgu_w"], 0.0).astype(BF16)
    sgu_bias = jnp.repeat(jnp.transpose(small["sgu_b"]), SGU_HEAD_DIM, axis=1)

    proj, h, (win, wkv, pool_all, wout) = _proj_gather(x, small["norm_pre"], shards)
    wout = wout.reshape(MIX_WIDTH, D_MODEL)
    wkv = wkv.reshape(D_MODEL, 2 * BRANCH)
    pool_full = (
        pool_all.reshape(N_DEV, len(POOL_WINDOWS), POOL_BLK, POOL_GROUP_DIM)
        .transpose(1, 0, 2, 3)
        .reshape(len(POOL_WINDOWS), POOL_GROUP_DIM, POOL_GROUP_DIM)
    )
    memn, kv, kv_t = _kv_forward(mem, small["mem_norm"], wkv)
    (y, dout, dxo, dproj, loss, d_norm_post, d_branch_norm, d_pool_scale, d_ln_g, d_ln_b, d_pool_w, d_sgu_w, d_sgu_b,
     dkv) = _mix(
        proj, x, target, kv, kv_t, wout, wout.T, pool_full, jnp.swapaxes(pool_full, 1, 2), small["pool_scale"],
        small["sgu_ln_g"], small["sgu_ln_b"], sgu_bias, sgu_wm, jnp.swapaxes(sgu_wm, 1, 2), small["branch_norm"],
        small["norm_post"],
    )
    g_wkv, d_mem_norm = _kv_backward(dkv, memn, wkv, mem)
    g_pool = (
        d_pool_w.reshape(len(POOL_WINDOWS), N_DEV, POOL_BLK, POOL_GROUP_DIM)
        .transpose(1, 0, 2, 3)
        .reshape(N_DEV, len(POOL_WINDOWS) * POOL_BLK, POOL_GROUP_DIM)
        .astype(BF16)
    )
    small_grads = dict(
        pool_scale=d_pool_scale, sgu_ln_g=d_ln_g, sgu_ln_b=d_ln_b, sgu_w=d_sgu_w, sgu_b=d_sgu_b,
        mem_norm=d_mem_norm, branch_norm=d_branch_norm, norm_post=d_norm_post,
    )
    packed = jnp.concatenate(
        [small_grads[n].reshape(-1, LANES) for n in SMALL if n != "norm_pre"]
        + [jnp.broadcast_to(loss, (SUBLANES, LANES))],
        axis=0,
    )
    packed = packed[None, None]

    by_chip = lambda g: g.reshape((N_CHIPS, 2) + g.shape[1:])
    small_mine = [by_chip(g_wkv), by_chip(g_pool), packed]
    g_wout, wout_theirs, small_theirs = _weight_grad(y, dout, N_DEV, "rows", "grad_w_out", [("pair", small_mine)])
    sums, _ = _pair_sum(small_mine + [by_chip(g_wout)], list(small_theirs) + [wout_theirs], "pair_sum_first")
    g_win, win_theirs, (l_wkv, l_pool, l_packed, l_wout) = _weight_grad(
        h, dproj, N_DEV, "cols", "grad_w_in", [("chips", list(sums))]
    )
    (win_sums,), _ = _pair_sum([by_chip(g_win)], [win_theirs], "pair_sum_w_in")
    win_late, _ = lax.optimization_barrier((win, g_wkv))
    send_sems, recv_sems, sums_thru, land_thru, token = _chips_start(win_sums)
    grad_x, d_norm_pre = _input_grad(
        dproj, jnp.swapaxes(win_late, 1, 2).reshape(IN_WIDTH, D_MODEL), x, dxo, small["norm_pre"] + token[0:1, 0:1]
    )
    l_win = _chips_wait(send_sems, recv_sems, sums_thru, land_thru, grad_x)
    d_norm_pre = _exchange_all(d_norm_pre, "exchange_norm_pre")
    return grad_x, dict(w_in=l_win, w_out=l_wout, w_kv=l_wkv, pool_w=l_pool), l_packed, d_norm_pre


def kernel(x, mem, norm_pre, w_in, pool_w, pool_scale, sgu_ln_g, sgu_ln_b, sgu_w, sgu_b, mem_norm, w_kv, branch_norm, w_out, norm_post, loss_target, m_norm_pre, m_w_in, m_pool_w, m_pool_scale, m_sgu_ln_g, m_sgu_ln_b, m_sgu_w, m_sgu_b, m_mem_norm, m_w_kv, m_branch_norm, m_w_out, m_norm_post, v_norm_pre, v_w_in, v_pool_w, v_pool_scale, v_sgu_ln_g, v_sgu_ln_b, v_sgu_w, v_sgu_b, v_mem_norm, v_w_kv, v_branch_norm, v_w_out, v_norm_post):
    weights = dict(norm_pre=norm_pre, w_in=w_in, pool_w=pool_w, pool_scale=pool_scale, sgu_ln_g=sgu_ln_g, sgu_ln_b=sgu_ln_b, sgu_w=sgu_w, sgu_b=sgu_b, mem_norm=mem_norm, w_kv=w_kv, branch_norm=branch_norm, w_out=w_out, norm_post=norm_post)
    first = dict(norm_pre=m_norm_pre, w_in=m_w_in, pool_w=m_pool_w, pool_scale=m_pool_scale, sgu_ln_g=m_sgu_ln_g, sgu_ln_b=m_sgu_ln_b, sgu_w=m_sgu_w, sgu_b=m_sgu_b, mem_norm=m_mem_norm, w_kv=m_w_kv, branch_norm=m_branch_norm, w_out=m_w_out, norm_post=m_norm_post)
    second = dict(norm_pre=v_norm_pre, w_in=v_w_in, pool_w=v_pool_w, pool_scale=v_pool_scale, sgu_ln_g=v_sgu_ln_g, sgu_ln_b=v_sgu_ln_b, sgu_w=v_sgu_w, sgu_b=v_sgu_b, mem_norm=v_mem_norm, w_kv=v_w_kv, branch_norm=v_branch_norm, w_out=v_w_out, norm_post=v_norm_post)
    order = ("norm_pre", "w_in", "pool_w", "pool_scale", "sgu_ln_g", "sgu_ln_b", "sgu_w", "sgu_b", "mem_norm", "w_kv", "branch_norm", "w_out", "norm_post")

    owned_shape = dict(
        w_in=(D_MODEL, WIN_BLK), w_out=(WOUT_BLK, D_MODEL), w_kv=(WKV_BLK, 2 * BRANCH),
        pool_w=(len(POOL_WINDOWS) * POOL_BLK, POOL_GROUP_DIM),
    )
    owned = {n: weights[n].reshape(owned_shape[n]) for n in owned_shape}
    small = {n: _local_view(n, weights[n]) for n in SMALL}
    grad_x, landed, landed_packed, d_norm_pre = _forward_backward(
        x[0], mem[0], loss_target[0],
        [owned["w_in"].astype(BF16)] + [owned[n] for n in ("w_kv", "pool_w", "w_out")], small,
    )
    landed_norm_pre = d_norm_pre.reshape(N_DEV, -1, LANES)

    grads, deltas, new_m, new_v = {}, {}, {}, {}
    for n in owned_shape:
        shape = weights[n].shape
        res = _adamw(
            landed[n], owned[n], first[n].reshape(owned_shape[n]), second[n].reshape(owned_shape[n]), "adamw_" + n
        )
        grads[n], deltas[n], new_m[n], new_v[n] = (a.reshape(shape) for a in res)
    rows_of = lambda tree, n: tree[n].reshape(-1, 128)
    for names, parts, name in (
        ([n for n in SMALL if n != "norm_pre"], landed_packed, "adamw_replicated"),
        (["norm_pre"], landed_norm_pre, "adamw_norm_pre"),
    ):
        res, rest = _adamw_packed(
            parts, [(rows_of(weights, n), rows_of(first, n), rows_of(second, n)) for n in names], name
        )
        if rest is not None:
            total = rest[0, 0]
        for n, four in zip(names, res):
            for tree, a in zip((grads, deltas, new_m, new_v), four):
                tree[n] = a.reshape(weights[n].shape)

    return (
        total,
        grad_x[None],
        *[grads[n] for n in order],
        *[deltas[n] for n in order],
        *[new_m[n] for n in order],
        *[new_v[n] for n in order],
    )
```

```python
import jax
import jax.numpy as jnp
from jax import lax
from jax.experimental import pallas as pl
from jax.experimental.pallas import tpu as pltpu

F32 = jnp.float32
BF16 = jnp.bfloat16
EPS = 1e-6

D_MODEL = 2048
POOL_WINDOWS = (2, 4, 8, 16)
POOL_GROUP_DIM = 256
BRANCH = 1024
SGU_CHUNK = 128
SGU_HEADS = 8
SGU_HEAD_DIM = 128
XATTN_HEADS = 4
XATTN_HEAD_DIM = 256
MIX_WIDTH = 3 * BRANCH
IN_WIDTH = 7 * BRANCH
N_DEV = 8
WIN_BLK = IN_WIDTH // N_DEV
WOUT_BLK = MIX_WIDTH // N_DEV
WKV_BLK = D_MODEL // N_DEV
POOL_BLK = POOL_GROUP_DIM // N_DEV
HALO = 16
LANES = 128
SUBLANES = 8

ADAM_LR = 0.001
ADAM_B1 = 0.9
ADAM_B2 = 0.999
ADAM_EPS = 1e-08
ADAM_WD = 0.01
ADAM_STEP = 10

VMEM_LIMIT_BYTES = 56 * 1024 * 1024
VMEM_LIMIT_MIX_BYTES = 63 * 1024 * 1024

TILE_PROJ = 512
TILE_MIX = 128
TILE_GRAD = 512
TILE_WEIGHT_GRAD = 1024
TILE_ADAM_BYTES = 1 << 20

ANY = pl.BlockSpec(memory_space=pl.ANY)
NN = (((1,), (0,)), ((), ()))
NT = (((1,), (1,)), ((), ()))
TN = (((0,), (0,)), ((), ()))
MESH = pl.DeviceIdType.MESH


def _dot(a, b, dims=NN):
    return lax.dot_general(a, b, dims, preferred_element_type=F32)


def _params(*semantics, vmem_limit_bytes=VMEM_LIMIT_BYTES):
    return pltpu.CompilerParams(dimension_semantics=semantics, vmem_limit_bytes=vmem_limit_bytes)


def _rowmean(a):
    return jnp.mean(a, axis=-1, keepdims=True)


def _colsum(a):
    return jnp.sum(a, axis=0, keepdims=True)


def _full(shape):
    zeros = (0,) * len(shape)
    return pl.BlockSpec(shape, lambda *_: zeros)


def _resident(shape):
    zeros = (0,) * len(shape)
    return pl.BlockSpec(shape, lambda *_: zeros, pipeline_mode=pl.Buffered(1))


def _kv_forward(mem, mem_norm, wkv):
    m, d = mem.shape
    n = wkv.shape[1]
    cols = 4 * LANES

    def body(mem_ref, g_ref, w_ref, memn_ref, kv_ref, kvt_ref):
        mv = mem_ref[...]
        r = lax.rsqrt(_rowmean(mv * mv) + EPS)
        memn = (mv * r * g_ref[...]).astype(BF16)
        memn_ref[...] = memn
        kv = _dot(memn, w_ref[...])
        kv_ref[...] = kv.astype(BF16)
        kvt_ref[...] = kv.T.astype(BF16)

    return pl.pallas_call(
        body,
        name="kv_forward",
        grid=(n // cols,),
        in_specs=[_full((m, d)), _full((1, d)), pl.BlockSpec((d, cols), lambda j: (0, j))],
        out_specs=[_full((m, d)), pl.BlockSpec((m, cols), lambda j: (0, j)), pl.BlockSpec((cols, m), lambda j: (j, 0))],
        out_shape=[
            jax.ShapeDtypeStruct((m, d), BF16), jax.ShapeDtypeStruct((m, n), BF16), jax.ShapeDtypeStruct((n, m), BF16)
        ],
        compiler_params=_params("arbitrary"),
    )(mem, mem_norm, wkv)


def _proj_gather(x_in, norm_pre, shards):
    s, d = x_in.shape
    t = min(TILE_PROJ, s)
    n_t = s // t
    n_arr = len(shards)

    def places(x, y, c):
        return (x, y, c), (x, y, 1 - c), (x ^ c, y ^ (1 - c)), (x ^ (1 - c), y ^ c), (1 - x, 1 - y)

    def index(chip, core):
        return 4 * chip[0] + 2 * chip[1] + core

    _, _, chip_a, chip_b, chip_d = places(*_position())
    c_out = lax.axis_index("c")
    me_out = index((lax.axis_index("x"), lax.axis_index("y")), c_out)
    order = jnp.stack(
        [
            me_out, me_out ^ 1, index(chip_a, c_out), index(chip_b, c_out), index(chip_b, 1 - c_out),
            index(chip_a, 1 - c_out), index(chip_d, c_out), index(chip_d, 1 - c_out),
        ]
    ).astype(jnp.int32)

    def body(order_ref, x_ref, g_ref, *refs):
        del order_ref
        raw = refs[:n_arr]
        proj_ref, h_ref = refs[n_arr : n_arr + 2]
        out = refs[n_arr + 2 : 2 * n_arr + 2]
        staged = refs[2 * n_arr + 2 : 3 * n_arr + 1]
        src = (raw[0],) + tuple(staged)
        wbuf, hs, send_sems, recv_sems, local_sems, load_sems = refs[3 * n_arr + 1 : 3 * n_arr + 7]
        wide = refs[3 * n_arr + 7 : 4 * n_arr + 6]
        narrow = refs[4 * n_arr + 6 :]
        j = pl.program_id(0)
        i = pl.program_id(1)
        me, sibling, chip_a, chip_b, chip_d = places(*_position())
        c = me[2]

        def block(a, chip, core):
            return out[a].at[index(chip, core)]

        def copy(a, k, owner, to, from_input=False):
            return pltpu.make_async_remote_copy(
                src_ref=src[a] if from_input else block(a, *owner),
                dst_ref=block(a, *owner),
                send_sem=send_sems.at[a, k],
                recv_sem=recv_sems.at[a, k],
                device_id=to,
                device_id_type=MESH,
            )

        mine = (me[:2], c)

        def own(a):
            return pltpu.make_async_copy(src[a], block(a, *mine), local_sems.at[a])

        def first_sends(a):
            return [
                copy(a, 0, mine, sibling, from_input=True),
                copy(a, 1, mine, (*chip_a, c), from_input=True),
                copy(a, 2, mine, (*chip_b, c), from_input=True),
            ]

        def onward(a, k):
            owner = {3: chip_a, 4: chip_a, 5: chip_b, 6: chip_d}[k]
            return copy(a, k, (owner, c), (*chip_b, c) if k == 3 else sibling)

        def landed(a, k):
            owner = {0: mine[0], 1: chip_a, 2: chip_b, 3: chip_d, 4: chip_b, 5: chip_a, 6: chip_d}[k]
            core = c if k in (1, 2, 3) else 1 - c
            copy(a, k, (owner, core), me).wait_recv()
            return owner, core

        def load(ref, step):
            return pltpu.make_async_copy(ref, wbuf.at[step % 2], load_sems.at[step % 2])

        @pl.when(jnp.logical_and(j == 0, i == 0))
        def _():
            own(0).start()
            for cp in first_sends(0):
                cp.start()
            for a in range(1, n_arr):
                pltpu.sync_copy(raw[a], wide[a - 1])
                narrow[a - 1][...] = wide[a - 1][...].astype(BF16)
                pltpu.sync_copy(narrow[a - 1], staged[a - 1])
            load(src[0], 0).start()
            load(src[0], 0).wait()

        steps = {1: (0, ()), 2: (1, (3, 4)), 3: (2, (5,)), 4: (4, ()), 5: (5, ()), 6: (3, (6,)), 7: (6, ())}
        for step, (k, then) in steps.items():

            @pl.when(jnp.logical_and(j == step, i == 0))
            def _():
                load(src[0], step).wait()

            @pl.when(jnp.logical_and(j == step - 1, i == n_t - 1))
            def _():
                owner = landed(0, k)
                for k2 in then:
                    onward(0, k2).start()
                if k == 1:
                    for a in range(1, n_arr):
                        own(a).start()
                        for cp in first_sends(a):
                            cp.start()
                if k == 3:
                    for a in range(1, n_arr):
                        for k1, then1 in ((1, (3, 4)), (2, (5,))):
                            landed(a, k1)
                            for k2 in then1:
                                onward(a, k2).start()
                load(block(0, *owner), step).start()

        @pl.when(j == 0)
        def _():
            xv = x_ref[...]
            h = (xv * lax.rsqrt(_rowmean(xv * xv) + EPS) * g_ref[...]).astype(BF16)
            hs[i] = h
            h_ref[...] = h

        proj_ref[...] = _dot(hs[i], wbuf[j % 2]).astype(BF16)

        @pl.when(jnp.logical_and(j == N_DEV - 1, i == n_t - 1))
        def _():
            for a in range(1, n_arr):
                landed(a, 3)
                onward(a, 6).start()
            for a in range(1, n_arr):
                for k in (0, 4, 5, 6):
                    landed(a, k)
            for a in range(n_arr):
                for cp in first_sends(a) + [onward(a, k) for k in (3, 4, 5, 6)]:
                    cp.wait_send()
                own(a).wait()

    res = pl.pallas_call(
        body,
        name="proj_gather",
        grid_spec=pltpu.PrefetchScalarGridSpec(
            num_scalar_prefetch=1,
            grid=(N_DEV, n_t),
            in_specs=[
                pl.BlockSpec((t, d), lambda j, i, order_ref: (jnp.where(j == 0, i, n_t - 1), 0)),
                pl.BlockSpec((1, d), lambda j, i, order_ref: (0, 0)),
            ]
            + [ANY] * n_arr,
            out_specs=[
                pl.BlockSpec((t, WIN_BLK), lambda j, i, order_ref: (i, order_ref[j])),
                pl.BlockSpec((t, d), lambda j, i, order_ref: (jnp.where(j == 0, i, n_t - 1), 0)),
            ]
            + [ANY] * (2 * n_arr - 1),
            scratch_shapes=[
                pltpu.VMEM((2,) + shards[0].shape, BF16),
                pltpu.VMEM((n_t, t, d), BF16),
                pltpu.SemaphoreType.DMA((n_arr, 7)),
                pltpu.SemaphoreType.DMA((n_arr, 7)),
                pltpu.SemaphoreType.DMA((n_arr,)),
                pltpu.SemaphoreType.DMA((2,)),
            ]
            + [pltpu.VMEM(a.shape, F32) for a in shards[1:]]
            + [pltpu.VMEM(a.shape, BF16) for a in shards[1:]],
        ),
        out_shape=[jax.ShapeDtypeStruct((s, IN_WIDTH), BF16), jax.ShapeDtypeStruct((s, d), BF16)]
        + [jax.ShapeDtypeStruct((N_DEV,) + a.shape, BF16) for a in shards]
        + [jax.ShapeDtypeStruct(a.shape, BF16) for a in shards[1:]],
        compiler_params=_params("arbitrary", "arbitrary"),
    )(order, x_in, norm_pre, *shards)
    return res[0], res[1], res[2 : 2 + n_arr]


def _sigmoid(a):
    return jax.nn.sigmoid(a)


def _dsilu(a, sg):
    return sg * (1.0 + a * (1.0 - sg))


def _rms_fwd(u, gain):
    r = lax.rsqrt(_rowmean(u * u) + EPS)
    n = u * r
    return r, n, n * gain


def _rms_bwd(dy, gain, r, n):
    dn = dy * gain
    return _colsum(dy * n), r * (dn - n * _rowmean(dn * n))


def _mix(proj, x, target, kv, kv_t, wout, wout_t, pool_w, pool_w_t, pool_scale, ln_g, ln_b, sgu_bias, sgu_wm, sgu_wm_t, branch_norm, norm_post):
    s, d = x.shape
    t = min(TILE_MIX, s)
    n_tiles = s // t
    n_chunks = t // SGU_CHUNK
    halo_blocks_per_tile = t // HALO
    inv_d = 1.0 / d
    scale = 1.0 / (XATTN_HEAD_DIM**0.5)

    def body(
        proj_ref, halo_ref, x_ref, tgt_ref, kv_ref, kvt_ref, wout_hbm, wout_t_hbm, pw_ref, pwt_ref, pscale_ref, lng_ref,
        lnb_ref, bias_ref, wm_ref, wmt_ref, bnorm_ref, gpost_ref,
        y_ref, dout_ref, dxo_ref, dproj_ref, loss_ref, dgpost_ref, dbnorm_ref, dpscale_ref, dlng_ref, dlnb_ref,
        dpw_out, dwm_out, dbias_ref, dkv_out,
        carry_ref, dzsum_ref, dpw_ref, dwm_ref, dkv_ref, wout_ref, wout_t_ref, wout_sems,
    ):
        i = pl.program_id(0)
        tile = n_tiles - 1 - i
        wout_load = pltpu.make_async_copy(wout_hbm, wout_ref, wout_sems.at[0])
        wout_t_load = pltpu.make_async_copy(wout_t_hbm, wout_t_ref, wout_sems.at[1])

        @pl.when(i == 0)
        def _():
            wout_load.start()
            wout_t_load.start()
            carry_ref[...] = jnp.zeros_like(carry_ref)
            dzsum_ref[...] = jnp.zeros_like(dzsum_ref)
            for ref in (loss_ref, dgpost_ref, dbnorm_ref, dpscale_ref, dlng_ref, dlnb_ref, dpw_ref, dwm_ref, dkv_ref):
                ref[...] = jnp.zeros_like(ref)

        t_glob = tile * t + lax.broadcasted_iota(jnp.int32, (t, 1), 0)
        inv_cnt = [1.0 / jnp.minimum(t_glob + 1, w).astype(F32) for w in POOL_WINDOWS]

        xa = proj_ref[:, 0:BRANCH].astype(F32)
        ga = proj_ref[:, BRANCH : 2 * BRANCH].astype(F32)
        halo = jnp.where(tile == 0, 0.0, halo_ref[...].astype(F32))
        d_bf, pm_parts = [], []
        for g, w in enumerate(POOL_WINDOWS):
            cols = slice(g * POOL_GROUP_DIM, (g + 1) * POOL_GROUP_DIM)
            acc = jnp.concatenate([halo[:, cols], xa[:, cols]], axis=0)
            k = 1
            while k < w:
                acc = acc + pltpu.roll(acc, k, axis=0)
                k *= 2
            dg = (acc[HALO:, :] * inv_cnt[g] - xa[:, cols]).astype(BF16)
            d_bf.append(dg)
            pm_parts.append(_dot(dg, pw_ref[g]))
        pm = jnp.concatenate(pm_parts, axis=1)
        pscale = pscale_ref[...]
        pa = pm * pscale
        sga = _sigmoid(ga)
        sila = ga * sga
        ua = pa * sila
        g_a = bnorm_ref[:, 0:BRANCH]
        ra, na, ya = _rms_fwd(ua, g_a)

        u = proj_ref[:, 2 * BRANCH : 3 * BRANCH].astype(F32)
        v = proj_ref[:, 3 * BRANCH : 4 * BRANCH].astype(F32)
        gb = proj_ref[:, 4 * BRANCH : 5 * BRANCH].astype(F32)
        lng = lng_ref[...]
        vc = v - _rowmean(v)
        rstd = lax.rsqrt(_rowmean(vc * vc) + EPS)
        vhat = vc * rstd
        vn_bf = (vhat * lng + lnb_ref[...]).astype(BF16)
        z_rows = []
        for c in range(n_chunks):
            rows = slice(c * SGU_CHUNK, (c + 1) * SGU_CHUNK)
            z_rows.append(
                jnp.concatenate(
                    [
                        _dot(wm_ref[hd], vn_bf[rows, hd * SGU_HEAD_DIM : (hd + 1) * SGU_HEAD_DIM])
                        for hd in range(SGU_HEADS)
                    ],
                    axis=1,
                )
                + bias_ref[...]
            )
        z = z_rows[0] if n_chunks == 1 else jnp.concatenate(z_rows, axis=0)
        sb = u * z
        sgb = _sigmoid(gb)
        silb = gb * sgb
        ub = sb * silb
        g_b = bnorm_ref[:, BRANCH : 2 * BRANCH]
        rb, nb, yb = _rms_fwd(ub, g_b)

        q = proj_ref[:, 5 * BRANCH : 6 * BRANCH]
        gc = proj_ref[:, 6 * BRANCH : 7 * BRANCH].astype(F32)
        q_bf, p_bf, o_parts = [], [], []
        for hd in range(XATTN_HEADS):
            cols = slice(hd * XATTN_HEAD_DIM, (hd + 1) * XATTN_HEAD_DIM)
            qh = q[:, cols]
            sc = _dot(qh, kvt_ref[cols, :]) * scale
            e = jnp.exp(sc - jnp.max(sc, axis=-1, keepdims=True))
            p = e / jnp.sum(e, axis=-1, keepdims=True)
            q_bf.append(qh)
            p_bf.append(p.astype(BF16))
            o_parts.append(_dot(p_bf[hd], kv_ref[:, BRANCH + hd * XATTN_HEAD_DIM : BRANCH + (hd + 1) * XATTN_HEAD_DIM]))
        o = jnp.concatenate(o_parts, axis=1)
        sgc = _sigmoid(gc)
        silc = gc * sgc
        uc = o * silc
        g_c = bnorm_ref[:, 2 * BRANCH : 3 * BRANCH]
        rc, nc, yc = _rms_fwd(uc, g_c)

        @pl.when(i == 0)
        def _():
            wout_load.wait()

        out = None
        for b, y_branch in enumerate((ya, yb, yc)):
            rows = slice(b * BRANCH, (b + 1) * BRANCH)
            y_bf = y_branch.astype(BF16)
            y_ref[:, rows] = y_bf
            part = _dot(y_bf, wout_ref[rows, :])
            out = part if out is None else out + part
        gpost = gpost_ref[...]
        r_out = lax.rsqrt(_rowmean(out * out) + EPS)
        on = out * r_out
        err = x_ref[...] + on * gpost - tgt_ref[...]
        loss_ref[...] += 0.5 * jnp.sum(_rowmean(err * err), axis=0, keepdims=True)

        dxo = err * inv_d
        dxo_ref[...] = dxo
        dgp, dout = _rms_bwd(dxo, gpost, r_out, on)
        dgpost_ref[...] += dgp
        dout_bf = dout.astype(BF16)
        dout_ref[...] = dout_bf

        @pl.when(i == 0)
        def _():
            wout_t_load.wait()

        dy = [_dot(dout_bf, wout_t_ref[:, b * BRANCH : (b + 1) * BRANCH]) for b in range(3)]

        dg_a, dua = _rms_bwd(dy[0], g_a, ra, na)
        dg_b, dub = _rms_bwd(dy[1], g_b, rb, nb)
        dg_c, duc = _rms_bwd(dy[2], g_c, rc, nc)
        dbnorm_ref[...] += jnp.concatenate([dg_a, dg_b, dg_c], axis=1)

        dpa = dua * sila
        dga = dua * pa * _dsilu(ga, sga)
        dpscale_ref[...] += _colsum(dpa * pm)
        dpm = dpa * pscale
        dxa_parts, carry_parts = [], []
        for g, w in enumerate(POOL_WINDOWS):
            cols = slice(g * POOL_GROUP_DIM, (g + 1) * POOL_GROUP_DIM)
            dpm_g = dpm[:, cols].astype(BF16)
            dd = _dot(dpm_g, pwt_ref[g])
            dpw_ref[g] += _dot(d_bf[g], dpm_g, TN)
            cg = dd * inv_cnt[g]
            carry_parts.append(cg[0:HALO, :])
            acc = jnp.concatenate([cg, carry_ref[:, cols]], axis=0)
            k = 1
            while k < w:
                acc = acc + pltpu.roll(acc, t + HALO - k, axis=0)
                k *= 2
            dxa_parts.append(acc[0:t, :] - dd)
        carry_ref[...] = jnp.concatenate(carry_parts, axis=1)
        dxa = jnp.concatenate(dxa_parts, axis=1)

        dsb = dub * silb
        dgb = dub * sb * _dsilu(gb, sgb)
        du = dsb * z
        dz = dsb * u
        dz_bf = dz.astype(BF16)
        dvn_rows = []
        dz_sum = None
        for c in range(n_chunks):
            rows = slice(c * SGU_CHUNK, (c + 1) * SGU_CHUNK)
            dz_sum = dz[rows, :] if dz_sum is None else dz_sum + dz[rows, :]
            parts = []
            for hd in range(SGU_HEADS):
                cols = slice(hd * SGU_HEAD_DIM, (hd + 1) * SGU_HEAD_DIM)
                parts.append(_dot(wmt_ref[hd], dz_bf[rows, cols]))
                dwm_ref[hd] += _dot(dz_bf[rows, cols], vn_bf[rows, cols], NT)
            dvn_rows.append(jnp.concatenate(parts, axis=1))
        dzsum_ref[...] += dz_sum
        dvn = dvn_rows[0] if n_chunks == 1 else jnp.concatenate(dvn_rows, axis=0)
        dlng_ref[...] += _colsum(dvn * vhat)
        dlnb_ref[...] += _colsum(dvn)
        dvh = dvn * lng
        dv = rstd * (dvh - _rowmean(dvh) - vhat * _rowmean(dvh * vhat))

        do = duc * silc
        dgc = duc * o * _dsilu(gc, sgc)
        dq_parts = []
        for hd in range(XATTN_HEADS):
            cols = slice(hd * XATTN_HEAD_DIM, (hd + 1) * XATTN_HEAD_DIM)
            vcols = slice(BRANCH + hd * XATTN_HEAD_DIM, BRANCH + (hd + 1) * XATTN_HEAD_DIM)
            do_h = do[:, cols].astype(BF16)
            p = p_bf[hd].astype(F32)
            dp = _dot(do_h, kvt_ref[vcols, :])
            dkv_ref[:, vcols] += _dot(p_bf[hd], do_h, TN)
            ds_bf = (p * (dp - jnp.sum(dp * p, axis=-1, keepdims=True)) * scale).astype(BF16)
            dq_parts.append(_dot(ds_bf, kv_ref[:, cols]))
            dkv_ref[:, cols] += _dot(ds_bf, q_bf[hd], TN)
        dq = jnp.concatenate(dq_parts, axis=1)

        dproj_ref[...] = jnp.concatenate([dxa, dga, du, dv, dgb, dq, dgc], axis=1).astype(BF16)

        @pl.when(i == n_tiles - 1)
        def _():
            keep = lax.broadcasted_iota(jnp.int32, (SGU_CHUNK, SGU_CHUNK), 0) >= lax.broadcasted_iota(
                jnp.int32, (SGU_CHUNK, SGU_CHUNK), 1
            )
            for hd in range(SGU_HEADS):
                dwm_ref[hd] = jnp.where(keep, dwm_ref[hd], 0.0)
                per_pos = dzsum_ref[:, hd * SGU_HEAD_DIM : (hd + 1) * SGU_HEAD_DIM]
                dbias_ref[hd : hd + 1, :] = _colsum(per_pos.T)
            for acc, res in ((dpw_ref, dpw_out), (dwm_ref, dwm_out), (dkv_ref, dkv_out)):
                pltpu.sync_copy(acc, res)

    row_tile = lambda width: pl.BlockSpec((t, width), lambda i: (n_tiles - 1 - i, 0))
    halo_spec = pl.BlockSpec(
        (HALO, BRANCH), lambda i: (jnp.maximum((n_tiles - 1 - i) * halo_blocks_per_tile - 1, 0), 0)
    )
    acc_shapes = [
        (1, 128),
        (1, d),
        (1, MIX_WIDTH),
        (1, BRANCH),
        (1, BRANCH),
        (1, BRANCH),
        pool_w.shape,
        sgu_wm.shape,
        (SGU_HEADS, SGU_CHUNK),
        kv.shape,
    ]
    return pl.pallas_call(
        body,
        name="mix",
        grid=(n_tiles,),
        in_specs=[
            row_tile(IN_WIDTH), halo_spec, row_tile(d), row_tile(d), _resident(kv.shape), _resident(kv_t.shape),
            ANY, ANY, _resident(pool_w.shape), _resident(pool_w_t.shape),
            _full((1, BRANCH)), _full((1, BRANCH)), _full((1, BRANCH)), _resident((SGU_CHUNK, BRANCH)),
            _resident(sgu_wm.shape), _resident(sgu_wm_t.shape), _full((1, MIX_WIDTH)), _full((1, d)),
        ],
        out_specs=[row_tile(MIX_WIDTH), row_tile(d), row_tile(d), row_tile(IN_WIDTH)]
        + [ANY if len(a) == 3 or a == kv.shape else _full(a) for a in acc_shapes],
        out_shape=[
            jax.ShapeDtypeStruct((s, MIX_WIDTH), BF16),
            jax.ShapeDtypeStruct((s, d), BF16),
            jax.ShapeDtypeStruct((s, d), F32),
            jax.ShapeDtypeStruct((s, IN_WIDTH), BF16),
        ]
        + [jax.ShapeDtypeStruct(a, F32) for a in acc_shapes],
        scratch_shapes=[
            pltpu.VMEM((HALO, BRANCH), F32), pltpu.VMEM((SGU_CHUNK, BRANCH), F32), pltpu.VMEM(pool_w.shape, F32),
            pltpu.VMEM(sgu_wm.shape, F32), pltpu.VMEM(kv.shape, F32), pltpu.VMEM(wout.shape, BF16),
            pltpu.VMEM(wout_t.shape, BF16), pltpu.SemaphoreType.DMA((2,)),
        ],
        compiler_params=_params("arbitrary", vmem_limit_bytes=VMEM_LIMIT_MIX_BYTES),
    )(
        proj, proj, x, target, kv, kv_t, wout, wout_t, pool_w, pool_w_t, pool_scale, ln_g, ln_b, sgu_bias, sgu_wm,
        sgu_wm_t, branch_norm, norm_post,
    )


def _position():
    return lax.axis_index("x"), lax.axis_index("y"), lax.axis_index("c")


N_CHIPS = 4


def _landing_shape(kind, a):
    return (N_CHIPS,) + a.shape[2:] if kind == "pair" else a.shape


def _carry_specs(groups):
    arrays = [(kind, a) for kind, arrs in groups for a in arrs]
    scratch = []
    for _, arrs in groups:
        n = len(arrs)
        scratch += [pltpu.SemaphoreType.DMA((n, N_DEV)), pltpu.SemaphoreType.DMA((n, N_DEV)), pltpu.SemaphoreType.DMA((n,))]
    return dict(
        n=len(arrays),
        operands=[a for _, a in arrays],
        in_specs=[ANY] * len(arrays),
        out_specs=[ANY] * len(arrays),
        out_shape=[jax.ShapeDtypeStruct(_landing_shape(kind, a), a.dtype) for kind, a in arrays],
        scratch_shapes=scratch,
    )


def _carry(groups, src, out, sems):
    x, y, c = _position()
    chip = 2 * x + y

    def remote(s, d, send_sems, recv_sems, a, m, to):
        return pltpu.make_async_remote_copy(
            src_ref=s, dst_ref=d, send_sem=send_sems.at[a, m], recv_sem=recv_sems.at[a, m], device_id=to,
            device_id_type=MESH,
        )

    def copies():
        far, near = [], []
        at = 0
        for g, (kind, arrs) in enumerate(groups):
            send_sems, recv_sems, local_sems = sems[3 * g : 3 * g + 3]
            for a in range(len(arrs)):
                s, d = src[at + a], out[at + a]
                if kind == "pair" and s.shape[0] == 1:
                    for b in range(N_CHIPS):
                        far.append(remote(s.at[0, 0], d.at[b], send_sems, recv_sems, a, 1 + b, (x, y, 1 - c)))
                elif kind == "pair":
                    far.append(remote(s.at[:, 1 - c], d, send_sems, recv_sems, a, 1, (x, y, 1 - c)))
                else:
                    assert kind == "chips", kind
                    for m in range(1, N_CHIPS):
                        px, py = x ^ (m >> 1), y ^ (m & 1)
                        far.append(remote(s.at[2 * px + py], d.at[chip], send_sems, recv_sems, a, m, (px, py, c)))
                    near.append(pltpu.make_async_copy(s.at[chip], d.at[chip], local_sems.at[a]))
            at += len(arrs)
        return far, near

    def start():
        far, near = copies()
        for cp in near + far:
            cp.start()

    def finish():
        far, near = copies()
        for cp in far:
            cp.wait_recv()
        for cp in far:
            cp.wait_send()
        for cp in near:
            cp.wait()

    return start, finish


def _pair_sum(mine, theirs, name, groups=()):
    n = len(mine)
    carried = _carry_specs(groups)
    n_c = carried["n"]
    core = lax.axis_index("c").astype(jnp.int32).reshape(1)

    def body(core_ref, *refs):
        del core_ref
        own = refs[:n]
        sib = refs[n : 2 * n]
        src = refs[2 * n : 2 * n + n_c]
        out = refs[2 * n + n_c : 3 * n + n_c]
        landed = refs[3 * n + n_c : 3 * n + 2 * n_c]
        start, finish = _carry(groups, src, landed, refs[3 * n + 2 * n_c :])
        b = pl.program_id(0)

        @pl.when(b == 0)
        def _():
            start()

        for a in range(n):
            out[a][...] = (own[a][...].astype(F32) + sib[a][...].astype(F32)).astype(out[a].dtype)

        @pl.when(b == N_CHIPS - 1)
        def _():
            finish()

    block = lambda a: pl.BlockSpec((None,) + a.shape[1:], lambda b, core_ref: (b, 0, 0))
    res = pl.pallas_call(
        body,
        name=name,
        grid_spec=pltpu.PrefetchScalarGridSpec(
            num_scalar_prefetch=1,
            grid=(N_CHIPS,),
            in_specs=[
                pl.BlockSpec((None, None) + a.shape[2:], lambda b, core_ref: (0, 0, 0, 0))
                if a.shape[0] == 1
                else pl.BlockSpec((None, None) + a.shape[2:], lambda b, core_ref: (b, core_ref[0], 0, 0))
                for a in mine
            ]
            + [block(a) for a in theirs]
            + carried["in_specs"],
            out_specs=[block(a) for a in theirs] + carried["out_specs"],
            scratch_shapes=carried["scratch_shapes"],
        ),
        out_shape=[jax.ShapeDtypeStruct(a.shape, a.dtype) for a in theirs] + carried["out_shape"],
        compiler_params=_params("arbitrary"),
    )(core, *mine, *theirs, *carried["operands"])
    return res[:n], res[n:]


def _weight_grad(a, b, n_blk, blocked, name, groups):
    s = a.shape[0]
    t = min(TILE_WEIGHT_GRAD, s)
    n_t = s // t
    n_pairs = n_blk // 2
    if blocked == "cols":
        k, c = a.shape[1], b.shape[1] // n_blk
        a_spec = pl.BlockSpec((t, k), lambda j, i: (i, 0))
        b_spec = pl.BlockSpec((t, 2 * c), lambda j, i: (i, j))
        acc_shape = (k, 2 * c)
    else:
        k, c = a.shape[1] // n_blk, b.shape[1]
        a_spec = pl.BlockSpec((t, 2 * k), lambda j, i: (i, j))
        b_spec = pl.BlockSpec((t, c), lambda j, i: (i, 0))
        acc_shape = (2 * k, c)
    carried = _carry_specs(groups)
    n_p = carried["n"]

    def body(a_ref, b_ref, *refs):
        src = refs[:n_p]
        o_ref, theirs_ref = refs[n_p : n_p + 2]
        landed = refs[n_p + 2 : 2 * n_p + 2]
        acc_ref, sbuf, pair_send, pair_recv = refs[2 * n_p + 2 : 2 * n_p + 6]
        start, finish = _carry(groups, src, landed, refs[2 * n_p + 6 :])
        j = pl.program_id(0)
        i = pl.program_id(1)
        x, y, c_me = _position()

        def to_sibling(pair):
            return pltpu.make_async_remote_copy(
                src_ref=sbuf.at[1 - c_me], dst_ref=theirs_ref.at[pair], send_sem=pair_send.at[pair],
                recv_sem=pair_recv.at[pair], device_id=(x, y, 1 - c_me), device_id_type=MESH,
            )

        @pl.when(jnp.logical_and(j == 0, i == 0))
        def _():
            start()

        @pl.when(i == 0)
        def _():
            acc_ref[...] = jnp.zeros_like(acc_ref)

        acc_ref[...] += _dot(a_ref[...], b_ref[...], TN)

        @pl.when(i == n_t - 1)
        def _():
            for pair in range(1, n_pairs):

                @pl.when(j == pair)
                def _():
                    to_sibling(pair - 1).wait_send()

            for half in range(2):
                if blocked == "cols":
                    block = acc_ref[:, half * c : (half + 1) * c].astype(BF16)
                else:
                    block = acc_ref[half * k : (half + 1) * k, :].astype(BF16)
                o_ref[half] = block
                sbuf[half] = block
            for pair in range(n_pairs):

                @pl.when(j == pair)
                def _():
                    to_sibling(pair).start()

        @pl.when(jnp.logical_and(j == n_pairs - 1, i == n_t - 1))
        def _():
            to_sibling(n_pairs - 1).wait_send()
            for pair in range(n_pairs):
                to_sibling(pair).wait_recv()
            finish()

    res = pl.pallas_call(
        body,
        name=name,
        grid=(n_pairs, n_t),
        in_specs=[a_spec, b_spec] + carried["in_specs"],
        out_specs=[pl.BlockSpec((2, k, c), lambda j, i: (j, 0, 0)), ANY] + carried["out_specs"],
        out_shape=[jax.ShapeDtypeStruct((n_blk, k, c), BF16), jax.ShapeDtypeStruct((n_pairs, k, c), BF16)]
        + carried["out_shape"],
        scratch_shapes=[
            pltpu.VMEM(acc_shape, F32), pltpu.VMEM((2, k, c), BF16), pltpu.SemaphoreType.DMA((n_pairs,)),
            pltpu.SemaphoreType.DMA((n_pairs,)),
        ]
        + carried["scratch_shapes"],
        compiler_params=_params("arbitrary", "arbitrary", vmem_limit_bytes=VMEM_LIMIT_MIX_BYTES),
    )(a, b, *carried["operands"])
    return res[0], res[1], res[2:]


def _input_grad(dproj, win_t, x, dxo, norm_pre, groups):
    s, d = x.shape
    t = min(TILE_GRAD, s)
    n_t = s // t
    kb = 2 * WIN_BLK
    n_k = win_t.shape[0] // kb
    carried = _carry_specs(groups)
    n_p = carried["n"]

    def body(dp_ref, w_ref, x_ref, dxo_ref, g_ref, *refs):
        src = refs[:n_p]
        gx_ref, dg_all = refs[n_p : n_p + 2]
        landed = refs[n_p + 2 : 2 * n_p + 2]
        acc_ref, dg_ref, dg_send, dg_recv, dg_local = refs[2 * n_p + 2 : 2 * n_p + 7]
        start, finish = _carry(groups, src, landed, refs[2 * n_p + 7 :])
        i = pl.program_id(0)
        j = pl.program_id(1)
        px, py, pc = _position()
        me = 4 * px + 2 * py + pc

        def dg_copies():
            far = [
                pltpu.make_async_remote_copy(
                    src_ref=dg_ref, dst_ref=dg_all.at[me], send_sem=dg_send.at[m], recv_sem=dg_recv.at[m],
                    device_id=(px ^ ((m >> 2) & 1), py ^ ((m >> 1) & 1), pc ^ (m & 1)), device_id_type=MESH,
                )
                for m in range(1, N_DEV)
            ]
            return far, pltpu.make_async_copy(dg_ref, dg_all.at[me], dg_local)

        @pl.when(jnp.logical_and(i == 0, j == 0))
        def _():
            start()
            dg_ref[...] = jnp.zeros_like(dg_ref)

        @pl.when(j == 0)
        def _():
            acc_ref[...] = jnp.zeros_like(acc_ref)

        acc_ref[...] += _dot(dp_ref[...], w_ref[...])

        @pl.when(j == n_k - 1)
        def _():
            xv = x_ref[...]
            gain = g_ref[...]
            r = lax.rsqrt(_rowmean(xv * xv) + EPS)
            dgain, dx = _rms_bwd(acc_ref[...], gain, r, xv * r)
            dg_ref[...] += dgain
            gx_ref[...] = dxo_ref[...] + dx

        @pl.when(jnp.logical_and(i == n_t - 1, j == n_k - 1))
        def _():
            far, near = dg_copies()
            for cp in [near] + far:
                cp.start()
            finish()
            for cp in far:
                cp.wait_recv()
            for cp in far:
                cp.wait_send()
            near.wait()

    res = pl.pallas_call(
        body,
        name="input_grad",
        grid=(n_t, n_k),
        in_specs=[
            pl.BlockSpec((t, kb), lambda i, j: (i, j)),
            pl.BlockSpec((kb, d), lambda i, j: (j, 0)),
            pl.BlockSpec((t, d), lambda i, j: (i, 0)),
            pl.BlockSpec((t, d), lambda i, j: (i, 0)),
            _full((1, d)),
        ]
        + carried["in_specs"],
        out_specs=[pl.BlockSpec((t, d), lambda i, j: (i, 0)), ANY] + carried["out_specs"],
        out_shape=[jax.ShapeDtypeStruct((s, d), F32), jax.ShapeDtypeStruct((N_DEV, 1, d), F32)] + carried["out_shape"],
        scratch_shapes=[
            pltpu.VMEM((t, d), F32), pltpu.VMEM((1, d), F32), pltpu.SemaphoreType.DMA((N_DEV,)),
            pltpu.SemaphoreType.DMA((N_DEV,)), pltpu.SemaphoreType.DMA,
        ]
        + carried["scratch_shapes"],
        input_output_aliases={3: 0},
        compiler_params=_params("arbitrary", "arbitrary", vmem_limit_bytes=VMEM_LIMIT_MIX_BYTES),
    )(dproj, win_t, x, dxo, norm_pre, *carried["operands"])
    return res[0], res[1], res[2:]


def _kv_backward(dkv, memn, wkv, mem):
    m, d = mem.shape
    n = wkv.shape[1]

    def body(dkv_ref, memn_ref, w_ref, mem_ref, gw_ref, dg_ref):
        dkv_bf = dkv_ref[...].astype(BF16)
        gw_ref[...] = _dot(memn_ref[...], dkv_bf, TN).astype(BF16).reshape(N_DEV, WKV_BLK, n)
        dmemn = _dot(dkv_bf, w_ref[...], NT)
        mv = mem_ref[...]
        r = lax.rsqrt(_rowmean(mv * mv) + EPS)
        dg_ref[...] = _colsum(dmemn * (mv * r))

    return pl.pallas_call(
        body,
        name="kv_backward",
        grid=(1,),
        in_specs=[_full((m, n)), _full((m, d)), _full(wkv.shape), _full((m, d))],
        out_specs=[_full((N_DEV, WKV_BLK, n)), _full((1, d))],
        out_shape=[jax.ShapeDtypeStruct((N_DEV, WKV_BLK, n), BF16), jax.ShapeDtypeStruct((1, d), F32)],
        compiler_params=_params("arbitrary"),
    )(dkv, memn, wkv, mem)


def _adamw_math(w, g, m, v):
    m = ADAM_B1 * m + (1.0 - ADAM_B1) * g
    v = ADAM_B2 * v + (1.0 - ADAM_B2) * (g * g)
    m_hat = m / (1.0 - ADAM_B1**ADAM_STEP)
    v_hat = v / (1.0 - ADAM_B2**ADAM_STEP)
    delta = -ADAM_LR * (m_hat / (jnp.sqrt(v_hat) + ADAM_EPS) + ADAM_WD * w)
    return delta, m, v


def _adamw(parts, w, m, v, name):
    r, c = w.shape
    slots = parts.shape[0]
    t = r
    while t * c * 4 > TILE_ADAM_BYTES and t % 16 == 0:
        t //= 2

    def body(p_ref, w_ref, m_ref, v_ref, g_ref, d_ref, nm_ref, nv_ref):
        g = p_ref[0].astype(F32)
        for k in range(1, slots):
            g = g + p_ref[k].astype(F32)
        delta, nm, nv = _adamw_math(w_ref[...], g, m_ref[...], v_ref[...])
        g_ref[...] = g
        d_ref[...] = delta
        nm_ref[...] = nm
        nv_ref[...] = nv

    tile = pl.BlockSpec((t, c), lambda i: (i, 0))
    return pl.pallas_call(
        body,
        name=name,
        grid=(r // t,),
        in_specs=[pl.BlockSpec((slots, t, c), lambda i: (0, i, 0)), tile, tile, tile],
        out_specs=[tile] * 4,
        out_shape=[jax.ShapeDtypeStruct((r, c), F32)] * 4,
        compiler_params=_params("parallel"),
    )(parts, w, m, v)


def _adamw_packed(parts, triples, name):
    slots = parts.shape[0]
    sizes = [w.shape[0] for w, _, _ in triples]
    rest = parts.shape[1] - sum(sizes)

    def total(p_ref, at, rows):
        g = p_ref[0, at : at + rows, :]
        for k in range(1, slots):
            g = g + p_ref[k, at : at + rows, :]
        return g

    def body(p_ref, *refs):
        ins = refs[: 3 * len(triples)]
        outs = refs[3 * len(triples) :]
        at = 0
        for n, rows in enumerate(sizes):
            g = total(p_ref, at, rows)
            w_ref, m_ref, v_ref = ins[3 * n : 3 * n + 3]
            delta, nm, nv = _adamw_math(w_ref[...], g, m_ref[...], v_ref[...])
            for ref, val in zip(outs[4 * n : 4 * n + 4], (g, delta, nm, nv)):
                ref[...] = val
            at += rows
        if rest:
            outs[-1][...] = total(p_ref, at, rest)

    flat = [a for t in triples for a in t]
    res = pl.pallas_call(
        body,
        name=name,
        out_shape=[jax.ShapeDtypeStruct(w.shape, F32) for w, _, _ in triples for _ in range(4)]
        + ([jax.ShapeDtypeStruct((rest, 128), F32)] if rest else []),
        compiler_params=pltpu.CompilerParams(vmem_limit_bytes=VMEM_LIMIT_BYTES),
    )(parts, *flat)
    return [res[4 * n : 4 * n + 4] for n in range(len(triples))], (res[-1] if rest else None)


SMALL = ("norm_pre", "pool_scale", "sgu_ln_g", "sgu_ln_b", "sgu_w", "sgu_b", "mem_norm", "branch_norm", "norm_post")


def _local_view(name, w):
    if name == "sgu_w":
        return w.reshape(SGU_HEADS, SGU_CHUNK, SGU_CHUNK)
    if name == "sgu_b":
        return w.reshape(SGU_HEADS, SGU_CHUNK)
    return w.reshape(1, -1)


def _forward_backward(x, mem, target, shards, small):
    causal = jnp.tril(jnp.ones((SGU_CHUNK, SGU_CHUNK), dtype=bool))
    sgu_wm = jnp.where(causal[None], small["sgu_w"], 0.0).astype(BF16)
    sgu_bias = jnp.repeat(jnp.transpose(small["sgu_b"]), SGU_HEAD_DIM, axis=1)

    proj, h, (win, wkv, pool_all, wout) = _proj_gather(x, small["norm_pre"], shards)
    wout = wout.reshape(MIX_WIDTH, D_MODEL)
    wkv = wkv.reshape(D_MODEL, 2 * BRANCH)
    pool_full = (
        pool_all.reshape(N_DEV, len(POOL_WINDOWS), POOL_BLK, POOL_GROUP_DIM)
        .transpose(1, 0, 2, 3)
        .reshape(len(POOL_WINDOWS), POOL_GROUP_DIM, POOL_GROUP_DIM)
    )
    memn, kv, kv_t = _kv_forward(mem, small["mem_norm"], wkv)
    (y, dout, dxo, dproj, loss, d_norm_post, d_branch_norm, d_pool_scale, d_ln_g, d_ln_b, d_pool_w, d_sgu_w, d_sgu_b,
     dkv) = _mix(
        proj, x, target, kv, kv_t, wout, wout.T, pool_full, jnp.swapaxes(pool_full, 1, 2), small["pool_scale"],
        small["sgu_ln_g"], small["sgu_ln_b"], sgu_bias, sgu_wm, jnp.swapaxes(sgu_wm, 1, 2), small["branch_norm"],
        small["norm_post"],
    )
    g_wkv, d_mem_norm = _kv_backward(dkv, memn, wkv, mem)
    g_pool = (
        d_pool_w.reshape(len(POOL_WINDOWS), N_DEV, POOL_BLK, POOL_GROUP_DIM)
        .transpose(1, 0, 2, 3)
        .reshape(N_DEV, len(POOL_WINDOWS) * POOL_BLK, POOL_GROUP_DIM)
        .astype(BF16)
    )
    small_grads = dict(
        pool_scale=d_pool_scale, sgu_ln_g=d_ln_g, sgu_ln_b=d_ln_b, sgu_w=d_sgu_w, sgu_b=d_sgu_b,
        mem_norm=d_mem_norm, branch_norm=d_branch_norm, norm_post=d_norm_post,
    )
    packed = jnp.concatenate(
        [small_grads[n].reshape(-1, LANES) for n in SMALL if n != "norm_pre"]
        + [jnp.broadcast_to(loss, (SUBLANES, LANES))],
        axis=0,
    )
    packed = packed[None, None]

    by_chip = lambda g: g.reshape((N_CHIPS, 2) + g.shape[1:])
    small_mine = [by_chip(g_wkv), by_chip(g_pool), packed]
    g_wout, wout_theirs, small_theirs = _weight_grad(y, dout, N_DEV, "rows", "grad_w_out", [("pair", small_mine)])
    sums, _ = _pair_sum(small_mine + [by_chip(g_wout)], list(small_theirs) + [wout_theirs], "pair_sum_first")
    g_win, win_theirs, (l_wkv, l_pool, l_packed, l_wout) = _weight_grad(
        h, dproj, N_DEV, "cols", "grad_w_in", [("chips", list(sums))]
    )
    (win_sums,), _ = _pair_sum([by_chip(g_win)], [win_theirs], "pair_sum_w_in")
    win_late, _ = lax.optimization_barrier((win, g_wkv))
    grad_x, d_norm_pre, (l_win,) = _input_grad(
        dproj, jnp.swapaxes(win_late, 1, 2).reshape(IN_WIDTH, D_MODEL), x, dxo, small["norm_pre"],
        [("chips", [win_sums])],
    )
    return grad_x, dict(w_in=l_win, w_out=l_wout, w_kv=l_wkv, pool_w=l_pool), l_packed, d_norm_pre


def kernel(x, mem, norm_pre, w_in, pool_w, pool_scale, sgu_ln_g, sgu_ln_b, sgu_w, sgu_b, mem_norm, w_kv, branch_norm, w_out, norm_post, loss_target, m_norm_pre, m_w_in, m_pool_w, m_pool_scale, m_sgu_ln_g, m_sgu_ln_b, m_sgu_w, m_sgu_b, m_mem_norm, m_w_kv, m_branch_norm, m_w_out, m_norm_post, v_norm_pre, v_w_in, v_pool_w, v_pool_scale, v_sgu_ln_g, v_sgu_ln_b, v_sgu_w, v_sgu_b, v_mem_norm, v_w_kv, v_branch_norm, v_w_out, v_norm_post):
    weights = dict(norm_pre=norm_pre, w_in=w_in, pool_w=pool_w, pool_scale=pool_scale, sgu_ln_g=sgu_ln_g, sgu_ln_b=sgu_ln_b, sgu_w=sgu_w, sgu_b=sgu_b, mem_norm=mem_norm, w_kv=w_kv, branch_norm=branch_norm, w_out=w_out, norm_post=norm_post)
    first = dict(norm_pre=m_norm_pre, w_in=m_w_in, pool_w=m_pool_w, pool_scale=m_pool_scale, sgu_ln_g=m_sgu_ln_g, sgu_ln_b=m_sgu_ln_b, sgu_w=m_sgu_w, sgu_b=m_sgu_b, mem_norm=m_mem_norm, w_kv=m_w_kv, branch_norm=m_branch_norm, w_out=m_w_out, norm_post=m_norm_post)
    second = dict(norm_pre=v_norm_pre, w_in=v_w_in, pool_w=v_pool_w, pool_scale=v_pool_scale, sgu_ln_g=v_sgu_ln_g, sgu_ln_b=v_sgu_ln_b, sgu_w=v_sgu_w, sgu_b=v_sgu_b, mem_norm=v_mem_norm, w_kv=v_w_kv, branch_norm=v_branch_norm, w_out=v_w_out, norm_post=v_norm_post)
    order = ("norm_pre", "w_in", "pool_w", "pool_scale", "sgu_ln_g", "sgu_ln_b", "sgu_w", "sgu_b", "mem_norm", "w_kv", "branch_norm", "w_out", "norm_post")

    owned_shape = dict(
        w_in=(D_MODEL, WIN_BLK), w_out=(WOUT_BLK, D_MODEL), w_kv=(WKV_BLK, 2 * BRANCH),
        pool_w=(len(POOL_WINDOWS) * POOL_BLK, POOL_GROUP_DIM),
    )
    owned = {n: weights[n].reshape(owned_shape[n]) for n in owned_shape}
    small = {n: _local_view(n, weights[n]) for n in SMALL}
    grad_x, landed, landed_packed, d_norm_pre = _forward_backward(
        x[0], mem[0], loss_target[0],
        [owned["w_in"].astype(BF16)] + [owned[n] for n in ("w_kv", "pool_w", "w_out")], small,
    )
    landed_norm_pre = d_norm_pre.reshape(N_DEV, -1, LANES)

    grads, deltas, new_m, new_v = {}, {}, {}, {}
    for n in owned_shape:
        shape = weights[n].shape
        res = _adamw(
            landed[n], owned[n], first[n].reshape(owned_shape[n]), second[n].reshape(owned_shape[n]), "adamw_" + n
        )
        grads[n], deltas[n], new_m[n], new_v[n] = (a.reshape(shape) for a in res)
    rows_of = lambda tree, n: tree[n].reshape(-1, 128)
    for names, parts, name in (
        ([n for n in SMALL if n != "norm_pre"], landed_packed, "adamw_replicated"),
        (["norm_pre"], landed_norm_pre, "adamw_norm_pre"),
    ):
        res, rest = _adamw_packed(
            parts, [(rows_of(weights, n), rows_of(first, n), rows_of(second, n)) for n in names], name
        )
        if rest is not None:
            total = rest[0, 0]
        for n, four in zip(names, res):
            for tree, a in zip((grads, deltas, new_m, new_v), four):
                tree[n] = a.reshape(weights[n].shape)

    return (
        total,
        grad_x[None],
        *[grads[n] for n in order],
        *[deltas[n] for n in order],
        *[new_m[n] for n in order],
        *[new_v[n] for n in order],
    )
```

```python
import jax
import jax.numpy as jnp
from jax import lax
from jax.experimental import pallas as pl
from jax.experimental.pallas import tpu as pltpu

F32 = jnp.float32
BF16 = jnp.bfloat16
EPS = 1e-6

D_MODEL = 2048
POOL_WINDOWS = (2, 4, 8, 16)
POOL_GROUP_DIM = 256
BRANCH = 1024
SGU_CHUNK = 128
SGU_HEADS = 8
SGU_HEAD_DIM = 128
XATTN_HEADS = 4
XATTN_HEAD_DIM = 256
MIX_WIDTH = 3 * BRANCH
IN_WIDTH = 7 * BRANCH
N_DEV = 8
WIN_BLK = IN_WIDTH // N_DEV
WOUT_BLK = MIX_WIDTH // N_DEV
WKV_BLK = D_MODEL // N_DEV
POOL_BLK = POOL_GROUP_DIM // N_DEV
HALO = 16
LANES = 128
SUBLANES = 8

ADAM_LR = 0.001
ADAM_B1 = 0.9
ADAM_B2 = 0.999
ADAM_EPS = 1e-08
ADAM_WD = 0.01
ADAM_STEP = 10

VMEM_LIMIT_BYTES = 56 * 1024 * 1024
VMEM_LIMIT_MIX_BYTES = 63 * 1024 * 1024

TILE_PROJ = 512
TILE_MIX = 128
TILE_GRAD = 512
TILE_WEIGHT_GRAD = 1024
TILE_ADAM_BYTES = 1 << 20

ANY = pl.BlockSpec(memory_space=pl.ANY)
NN = (((1,), (0,)), ((), ()))
NT = (((1,), (1,)), ((), ()))
TN = (((0,), (0,)), ((), ()))
MESH = pl.DeviceIdType.MESH


def _dot(a, b, dims=NN):
    return lax.dot_general(a, b, dims, preferred_element_type=F32)


def _params(*semantics, vmem_limit_bytes=VMEM_LIMIT_BYTES):
    return pltpu.CompilerParams(dimension_semantics=semantics, vmem_limit_bytes=vmem_limit_bytes)


def _rowmean(a):
    return jnp.mean(a, axis=-1, keepdims=True)


def _colsum(a):
    return jnp.sum(a, axis=0, keepdims=True)


def _full(shape):
    zeros = (0,) * len(shape)
    return pl.BlockSpec(shape, lambda *_: zeros)


def _resident(shape):
    zeros = (0,) * len(shape)
    return pl.BlockSpec(shape, lambda *_: zeros, pipeline_mode=pl.Buffered(1))


def _kv_forward(mem, mem_norm, wkv):
    m, d = mem.shape
    n = wkv.shape[1]
    cols = 4 * LANES

    def body(mem_ref, g_ref, w_ref, memn_ref, kv_ref, kvt_ref):
        mv = mem_ref[...]
        r = lax.rsqrt(_rowmean(mv * mv) + EPS)
        memn = (mv * r * g_ref[...]).astype(BF16)
        memn_ref[...] = memn
        kv = _dot(memn, w_ref[...])
        kv_ref[...] = kv.astype(BF16)
        kvt_ref[...] = kv.T.astype(BF16)

    return pl.pallas_call(
        body,
        name="kv_forward",
        grid=(n // cols,),
        in_specs=[_full((m, d)), _full((1, d)), pl.BlockSpec((d, cols), lambda j: (0, j))],
        out_specs=[_full((m, d)), pl.BlockSpec((m, cols), lambda j: (0, j)), pl.BlockSpec((cols, m), lambda j: (j, 0))],
        out_shape=[
            jax.ShapeDtypeStruct((m, d), BF16), jax.ShapeDtypeStruct((m, n), BF16), jax.ShapeDtypeStruct((n, m), BF16)
        ],
        compiler_params=_params("arbitrary"),
    )(mem, mem_norm, wkv)


def _proj_gather(x_in, norm_pre, shards):
    s, d = x_in.shape
    t = min(TILE_PROJ, s)
    n_t = s // t
    n_arr = len(shards)

    def places(x, y, c):
        return (x, y, c), (x, y, 1 - c), (x ^ c, y ^ (1 - c)), (x ^ (1 - c), y ^ c), (1 - x, 1 - y)

    def index(chip, core):
        return 4 * chip[0] + 2 * chip[1] + core

    _, _, chip_a, chip_b, chip_d = places(*_position())
    c_out = lax.axis_index("c")
    me_out = index((lax.axis_index("x"), lax.axis_index("y")), c_out)
    order = jnp.stack(
        [
            me_out, me_out ^ 1, index(chip_a, c_out), index(chip_b, c_out), index(chip_b, 1 - c_out),
            index(chip_a, 1 - c_out), index(chip_d, c_out), index(chip_d, 1 - c_out),
        ]
    ).astype(jnp.int32)

    def body(order_ref, x_ref, g_ref, *refs):
        del order_ref
        raw = refs[:n_arr]
        proj_ref, h_ref = refs[n_arr : n_arr + 2]
        out = refs[n_arr + 2 : 2 * n_arr + 2]
        staged = refs[2 * n_arr + 2 : 3 * n_arr + 1]
        src = (raw[0],) + tuple(staged)
        wbuf, hs, send_sems, recv_sems, local_sems, load_sems = refs[3 * n_arr + 1 : 3 * n_arr + 7]
        wide = refs[3 * n_arr + 7 : 4 * n_arr + 6]
        narrow = refs[4 * n_arr + 6 :]
        j = pl.program_id(0)
        i = pl.program_id(1)
        me, sibling, chip_a, chip_b, chip_d = places(*_position())
        c = me[2]

        def block(a, chip, core):
            return out[a].at[index(chip, core)]

        def copy(a, k, owner, to, from_input=False):
            return pltpu.make_async_remote_copy(
                src_ref=src[a] if from_input else block(a, *owner),
                dst_ref=block(a, *owner),
                send_sem=send_sems.at[a, k],
                recv_sem=recv_sems.at[a, k],
                device_id=to,
                device_id_type=MESH,
            )

        mine = (me[:2], c)

        def own(a):
            return pltpu.make_async_copy(src[a], block(a, *mine), local_sems.at[a])

        def first_sends(a):
            return [
                copy(a, 0, mine, sibling, from_input=True),
                copy(a, 1, mine, (*chip_a, c), from_input=True),
                copy(a, 2, mine, (*chip_b, c), from_input=True),
            ]

        def onward(a, k):
            owner = {3: chip_a, 4: chip_a, 5: chip_b, 6: chip_d}[k]
            return copy(a, k, (owner, c), (*chip_b, c) if k == 3 else sibling)

        def landed(a, k):
            owner = {0: mine[0], 1: chip_a, 2: chip_b, 3: chip_d, 4: chip_b, 5: chip_a, 6: chip_d}[k]
            core = c if k in (1, 2, 3) else 1 - c
            copy(a, k, (owner, core), me).wait_recv()
            return owner, core

        def load(ref, step):
            return pltpu.make_async_copy(ref, wbuf.at[step % 2], load_sems.at[step % 2])

        @pl.when(jnp.logical_and(j == 0, i == 0))
        def _():
            own(0).start()
            for cp in first_sends(0):
                cp.start()
            for a in range(1, n_arr):
                pltpu.sync_copy(raw[a], wide[a - 1])
                narrow[a - 1][...] = wide[a - 1][...].astype(BF16)
                pltpu.sync_copy(narrow[a - 1], staged[a - 1])
            load(src[0], 0).start()
            load(src[0], 0).wait()

        steps = {1: (0, ()), 2: (1, (3, 4)), 3: (2, (5,)), 4: (4, ()), 5: (5, ()), 6: (3, (6,)), 7: (6, ())}
        for step, (k, then) in steps.items():

            @pl.when(jnp.logical_and(j == step, i == 0))
            def _():
                load(src[0], step).wait()

            @pl.when(jnp.logical_and(j == step - 1, i == n_t - 1))
            def _():
                owner = landed(0, k)
                for k2 in then:
                    onward(0, k2).start()
                if k == 1:
                    for a in range(1, n_arr):
                        own(a).start()
                        for cp in first_sends(a):
                            cp.start()
                if k == 3:
                    for a in range(1, n_arr):
                        for k1, then1 in ((1, (3, 4)), (2, (5,))):
                            landed(a, k1)
                            for k2 in then1:
                                onward(a, k2).start()
                load(block(0, *owner), step).start()

        @pl.when(j == 0)
        def _():
            xv = x_ref[...]
            h = (xv * lax.rsqrt(_rowmean(xv * xv) + EPS) * g_ref[...]).astype(BF16)
            hs[i] = h
            h_ref[...] = h

        proj_ref[...] = _dot(hs[i], wbuf[j % 2]).astype(BF16)

        @pl.when(jnp.logical_and(j == N_DEV - 1, i == n_t - 1))
        def _():
            for a in range(1, n_arr):
                landed(a, 3)
                onward(a, 6).start()
            for a in range(1, n_arr):
                for k in (0, 4, 5, 6):
                    landed(a, k)
            for a in range(n_arr):
                for cp in first_sends(a) + [onward(a, k) for k in (3, 4, 5, 6)]:
                    cp.wait_send()
                own(a).wait()

    res = pl.pallas_call(
        body,
        name="proj_gather",
        grid_spec=pltpu.PrefetchScalarGridSpec(
            num_scalar_prefetch=1,
            grid=(N_DEV, n_t),
            in_specs=[
                pl.BlockSpec((t, d), lambda j, i, order_ref: (jnp.where(j == 0, i, n_t - 1), 0)),
                pl.BlockSpec((1, d), lambda j, i, order_ref: (0, 0)),
            ]
            + [ANY] * n_arr,
            out_specs=[
                pl.BlockSpec((t, WIN_BLK), lambda j, i, order_ref: (i, order_ref[j])),
                pl.BlockSpec((t, d), lambda j, i, order_ref: (jnp.where(j == 0, i, n_t - 1), 0)),
            ]
            + [ANY] * (2 * n_arr - 1),
            scratch_shapes=[
                pltpu.VMEM((2,) + shards[0].shape, BF16),
                pltpu.VMEM((n_t, t, d), BF16),
                pltpu.SemaphoreType.DMA((n_arr, 7)),
                pltpu.SemaphoreType.DMA((n_arr, 7)),
                pltpu.SemaphoreType.DMA((n_arr,)),
                pltpu.SemaphoreType.DMA((2,)),
            ]
            + [pltpu.VMEM(a.shape, F32) for a in shards[1:]]
            + [pltpu.VMEM(a.shape, BF16) for a in shards[1:]],
        ),
        out_shape=[jax.ShapeDtypeStruct((s, IN_WIDTH), BF16), jax.ShapeDtypeStruct((s, d), BF16)]
        + [jax.ShapeDtypeStruct((N_DEV,) + a.shape, BF16) for a in shards]
        + [jax.ShapeDtypeStruct(a.shape, BF16) for a in shards[1:]],
        compiler_params=_params("arbitrary", "arbitrary"),
    )(order, x_in, norm_pre, *shards)
    return res[0], res[1], res[2 : 2 + n_arr]


def _sigmoid(a):
    return jax.nn.sigmoid(a)


def _dsilu(a, sg):
    return sg * (1.0 + a * (1.0 - sg))


def _rms_fwd(u, gain):
    r = lax.rsqrt(_rowmean(u * u) + EPS)
    n = u * r
    return r, n, n * gain


def _rms_bwd(dy, gain, r, n):
    dn = dy * gain
    return _colsum(dy * n), r * (dn - n * _rowmean(dn * n))


def _mix(proj, x, target, kv, kv_t, wout, wout_t, pool_w, pool_w_t, pool_scale, ln_g, ln_b, sgu_bias, sgu_wm, sgu_wm_t, branch_norm, norm_post):
    s, d = x.shape
    t = min(TILE_MIX, s)
    n_tiles = s // t
    n_chunks = t // SGU_CHUNK
    halo_blocks_per_tile = t // HALO
    inv_d = 1.0 / d
    scale = 1.0 / (XATTN_HEAD_DIM**0.5)

    def body(
        proj_ref, halo_ref, x_ref, tgt_ref, kv_ref, kvt_ref, wout_hbm, wout_t_hbm, pw_ref, pwt_ref, pscale_ref, lng_ref,
        lnb_ref, bias_ref, wm_ref, wmt_ref, bnorm_ref, gpost_ref,
        y_ref, dout_ref, dxo_ref, dproj_ref, loss_ref, dgpost_ref, dbnorm_ref, dpscale_ref, dlng_ref, dlnb_ref,
        dpw_out, dwm_out, dbias_ref, dkv_out,
        carry_ref, dzsum_ref, dpw_ref, dwm_ref, dkv_ref, wout_ref, wout_t_ref, wout_sems,
    ):
        i = pl.program_id(0)
        tile = n_tiles - 1 - i
        wout_load = pltpu.make_async_copy(wout_hbm, wout_ref, wout_sems.at[0])
        wout_t_load = pltpu.make_async_copy(wout_t_hbm, wout_t_ref, wout_sems.at[1])

        @pl.when(i == 0)
        def _():
            wout_load.start()
            wout_t_load.start()
            carry_ref[...] = jnp.zeros_like(carry_ref)
            dzsum_ref[...] = jnp.zeros_like(dzsum_ref)
            for ref in (loss_ref, dgpost_ref, dbnorm_ref, dpscale_ref, dlng_ref, dlnb_ref, dpw_ref, dwm_ref, dkv_ref):
                ref[...] = jnp.zeros_like(ref)

        t_glob = tile * t + lax.broadcasted_iota(jnp.int32, (t, 1), 0)
        inv_cnt = [1.0 / jnp.minimum(t_glob + 1, w).astype(F32) for w in POOL_WINDOWS]

        xa = proj_ref[:, 0:BRANCH].astype(F32)
        ga = proj_ref[:, BRANCH : 2 * BRANCH].astype(F32)
        halo = jnp.where(tile == 0, 0.0, halo_ref[...].astype(F32))
        d_bf, pm_parts = [], []
        for g, w in enumerate(POOL_WINDOWS):
            cols = slice(g * POOL_GROUP_DIM, (g + 1) * POOL_GROUP_DIM)
            acc = jnp.concatenate([halo[:, cols], xa[:, cols]], axis=0)
            k = 1
            while k < w:
                acc = acc + pltpu.roll(acc, k, axis=0)
                k *= 2
            dg = (acc[HALO:, :] * inv_cnt[g] - xa[:, cols]).astype(BF16)
            d_bf.append(dg)
            pm_parts.append(_dot(dg, pw_ref[g]))
        pm = jnp.concatenate(pm_parts, axis=1)
        pscale = pscale_ref[...]
        pa = pm * pscale
        sga = _sigmoid(ga)
        sila = ga * sga
        ua = pa * sila
        g_a = bnorm_ref[:, 0:BRANCH]
        ra, na, ya = _rms_fwd(ua, g_a)

        u = proj_ref[:, 2 * BRANCH : 3 * BRANCH].astype(F32)
        v = proj_ref[:, 3 * BRANCH : 4 * BRANCH].astype(F32)
        gb = proj_ref[:, 4 * BRANCH : 5 * BRANCH].astype(F32)
        lng = lng_ref[...]
        vc = v - _rowmean(v)
        rstd = lax.rsqrt(_rowmean(vc * vc) + EPS)
        vhat = vc * rstd
        vn_bf = (vhat * lng + lnb_ref[...]).astype(BF16)
        z_rows = []
        for c in range(n_chunks):
            rows = slice(c * SGU_CHUNK, (c + 1) * SGU_CHUNK)
            z_rows.append(
                jnp.concatenate(
                    [
                        _dot(wm_ref[hd], vn_bf[rows, hd * SGU_HEAD_DIM : (hd + 1) * SGU_HEAD_DIM])
                        for hd in range(SGU_HEADS)
                    ],
                    axis=1,
                )
                + bias_ref[...]
            )
        z = z_rows[0] if n_chunks == 1 else jnp.concatenate(z_rows, axis=0)
        sb = u * z
        sgb = _sigmoid(gb)
        silb = gb * sgb
        ub = sb * silb
        g_b = bnorm_ref[:, BRANCH : 2 * BRANCH]
        rb, nb, yb = _rms_fwd(ub, g_b)

        q = proj_ref[:, 5 * BRANCH : 6 * BRANCH]
        gc = proj_ref[:, 6 * BRANCH : 7 * BRANCH].astype(F32)
        q_bf, p_bf, o_parts = [], [], []
        for hd in range(XATTN_HEADS):
            cols = slice(hd * XATTN_HEAD_DIM, (hd + 1) * XATTN_HEAD_DIM)
            qh = q[:, cols]
            sc = _dot(qh, kvt_ref[cols, :]) * scale
            e = jnp.exp(sc - jnp.max(sc, axis=-1, keepdims=True))
            p = e / jnp.sum(e, axis=-1, keepdims=True)
            q_bf.append(qh)
            p_bf.append(p.astype(BF16))
            o_parts.append(_dot(p_bf[hd], kv_ref[:, BRANCH + hd * XATTN_HEAD_DIM : BRANCH + (hd + 1) * XATTN_HEAD_DIM]))
        o = jnp.concatenate(o_parts, axis=1)
        sgc = _sigmoid(gc)
        silc = gc * sgc
        uc = o * silc
        g_c = bnorm_ref[:, 2 * BRANCH : 3 * BRANCH]
        rc, nc, yc = _rms_fwd(uc, g_c)

        @pl.when(i == 0)
        def _():
            wout_load.wait()

        out = None
        for b, y_branch in enumerate((ya, yb, yc)):
            rows = slice(b * BRANCH, (b + 1) * BRANCH)
            y_bf = y_branch.astype(BF16)
            y_ref[:, rows] = y_bf
            part = _dot(y_bf, wout_ref[rows, :])
            out = part if out is None else out + part
        gpost = gpost_ref[...]
        r_out = lax.rsqrt(_rowmean(out * out) + EPS)
        on = out * r_out
        err = x_ref[...] + on * gpost - tgt_ref[...]
        loss_ref[...] += 0.5 * jnp.sum(_rowmean(err * err), axis=0, keepdims=True)

        dxo = err * inv_d
        dxo_ref[...] = dxo
        dgp, dout = _rms_bwd(dxo, gpost, r_out, on)
        dgpost_ref[...] += dgp
        dout_bf = dout.astype(BF16)
        dout_ref[...] = dout_bf

        @pl.when(i == 0)
        def _():
            wout_t_load.wait()

        dy = [_dot(dout_bf, wout_t_ref[:, b * BRANCH : (b + 1) * BRANCH]) for b in range(3)]

        dg_a, dua = _rms_bwd(dy[0], g_a, ra, na)
        dg_b, dub = _rms_bwd(dy[1], g_b, rb, nb)
        dg_c, duc = _rms_bwd(dy[2], g_c, rc, nc)
        dbnorm_ref[...] += jnp.concatenate([dg_a, dg_b, dg_c], axis=1)

        dpa = dua * sila
        dga = dua * pa * _dsilu(ga, sga)
        dpscale_ref[...] += _colsum(dpa * pm)
        dpm = dpa * pscale
        dxa_parts, carry_parts = [], []
        for g, w in enumerate(POOL_WINDOWS):
            cols = slice(g * POOL_GROUP_DIM, (g + 1) * POOL_GROUP_DIM)
            dpm_g = dpm[:, cols].astype(BF16)
            dd = _dot(dpm_g, pwt_ref[g])
            dpw_ref[g] += _dot(d_bf[g], dpm_g, TN)
            cg = dd * inv_cnt[g]
            carry_parts.append(cg[0:HALO, :])
            acc = jnp.concatenate([cg, carry_ref[:, cols]], axis=0)
            k = 1
            while k < w:
                acc = acc + pltpu.roll(acc, t + HALO - k, axis=0)
                k *= 2
            dxa_parts.append(acc[0:t, :] - dd)
        carry_ref[...] = jnp.concatenate(carry_parts, axis=1)
        dxa = jnp.concatenate(dxa_parts, axis=1)

        dsb = dub * silb
        dgb = dub * sb * _dsilu(gb, sgb)
        du = dsb * z
        dz = dsb * u
        dz_bf = dz.astype(BF16)
        dvn_rows = []
        dz_sum = None
        for c in range(n_chunks):
            rows = slice(c * SGU_CHUNK, (c + 1) * SGU_CHUNK)
            dz_sum = dz[rows, :] if dz_sum is None else dz_sum + dz[rows, :]
            parts = []
            for hd in range(SGU_HEADS):
                cols = slice(hd * SGU_HEAD_DIM, (hd + 1) * SGU_HEAD_DIM)
                parts.append(_dot(wmt_ref[hd], dz_bf[rows, cols]))
                dwm_ref[hd] += _dot(dz_bf[rows, cols], vn_bf[rows, cols], NT)
            dvn_rows.append(jnp.concatenate(parts, axis=1))
        dzsum_ref[...] += dz_sum
        dvn = dvn_rows[0] if n_chunks == 1 else jnp.concatenate(dvn_rows, axis=0)
        dlng_ref[...] += _colsum(dvn * vhat)
        dlnb_ref[...] += _colsum(dvn)
        dvh = dvn * lng
        dv = rstd * (dvh - _rowmean(dvh) - vhat * _rowmean(dvh * vhat))

        do = duc * silc
        dgc = duc * o * _dsilu(gc, sgc)
        dq_parts = []
        for hd in range(XATTN_HEADS):
            cols = slice(hd * XATTN_HEAD_DIM, (hd + 1) * XATTN_HEAD_DIM)
            vcols = slice(BRANCH + hd * XATTN_HEAD_DIM, BRANCH + (hd + 1) * XATTN_HEAD_DIM)
            do_h = do[:, cols].astype(BF16)
            p = p_bf[hd].astype(F32)
            dp = _dot(do_h, kvt_ref[vcols, :])
            dkv_ref[:, vcols] += _dot(p_bf[hd], do_h, TN)
            ds_bf = (p * (dp - jnp.sum(dp * p, axis=-1, keepdims=True)) * scale).astype(BF16)
            dq_parts.append(_dot(ds_bf, kv_ref[:, cols]))
            dkv_ref[:, cols] += _dot(ds_bf, q_bf[hd], TN)
        dq = jnp.concatenate(dq_parts, axis=1)

        dproj_ref[...] = jnp.concatenate([dxa, dga, du, dv, dgb, dq, dgc], axis=1).astype(BF16)

        @pl.when(i == n_tiles - 1)
        def _():
            keep = lax.broadcasted_iota(jnp.int32, (SGU_CHUNK, SGU_CHUNK), 0) >= lax.broadcasted_iota(
                jnp.int32, (SGU_CHUNK, SGU_CHUNK), 1
            )
            for hd in range(SGU_HEADS):
                dwm_ref[hd] = jnp.where(keep, dwm_ref[hd], 0.0)
                per_pos = dzsum_ref[:, hd * SGU_HEAD_DIM : (hd + 1) * SGU_HEAD_DIM]
                dbias_ref[hd : hd + 1, :] = _colsum(per_pos.T)
            for acc, res in ((dpw_ref, dpw_out), (dwm_ref, dwm_out), (dkv_ref, dkv_out)):
                pltpu.sync_copy(acc, res)

    row_tile = lambda width: pl.BlockSpec((t, width), lambda i: (n_tiles - 1 - i, 0))
    halo_spec = pl.BlockSpec(
        (HALO, BRANCH), lambda i: (jnp.maximum((n_tiles - 1 - i) * halo_blocks_per_tile - 1, 0), 0)
    )
    acc_shapes = [
        (1, 128),
        (1, d),
        (1, MIX_WIDTH),
        (1, BRANCH),
        (1, BRANCH),
        (1, BRANCH),
        pool_w.shape,
        sgu_wm.shape,
        (SGU_HEADS, SGU_CHUNK),
        kv.shape,
    ]
    return pl.pallas_call(
        body,
        name="mix",
        grid=(n_tiles,),
        in_specs=[
            row_tile(IN_WIDTH), halo_spec, row_tile(d), row_tile(d), _resident(kv.shape), _resident(kv_t.shape),
            ANY, ANY, _resident(pool_w.shape), _resident(pool_w_t.shape),
            _full((1, BRANCH)), _full((1, BRANCH)), _full((1, BRANCH)), _resident((SGU_CHUNK, BRANCH)),
            _resident(sgu_wm.shape), _resident(sgu_wm_t.shape), _full((1, MIX_WIDTH)), _full((1, d)),
        ],
        out_specs=[row_tile(MIX_WIDTH), row_tile(d), row_tile(d), row_tile(IN_WIDTH)]
        + [ANY if len(a) == 3 or a == kv.shape else _full(a) for a in acc_shapes],
        out_shape=[
            jax.ShapeDtypeStruct((s, MIX_WIDTH), BF16),
            jax.ShapeDtypeStruct((s, d), BF16),
            jax.ShapeDtypeStruct((s, d), F32),
            jax.ShapeDtypeStruct((s, IN_WIDTH), BF16),
        ]
        + [jax.ShapeDtypeStruct(a, F32) for a in acc_shapes],
        scratch_shapes=[
            pltpu.VMEM((HALO, BRANCH), F32), pltpu.VMEM((SGU_CHUNK, BRANCH), F32), pltpu.VMEM(pool_w.shape, F32),
            pltpu.VMEM(sgu_wm.shape, F32), pltpu.VMEM(kv.shape, F32), pltpu.VMEM(wout.shape, BF16),
            pltpu.VMEM(wout_t.shape, BF16), pltpu.SemaphoreType.DMA((2,)),
        ],
        compiler_params=_params("arbitrary", vmem_limit_bytes=VMEM_LIMIT_MIX_BYTES),
    )(
        proj, proj, x, target, kv, kv_t, wout, wout_t, pool_w, pool_w_t, pool_scale, ln_g, ln_b, sgu_bias, sgu_wm,
        sgu_wm_t, branch_norm, norm_post,
    )


def _position():
    return lax.axis_index("x"), lax.axis_index("y"), lax.axis_index("c")


N_CHIPS = 4


def _landing_shape(kind, a):
    return (N_CHIPS,) + a.shape[2:] if kind == "pair" else a.shape


def _carry_specs(groups):
    arrays = [(kind, a) for kind, arrs in groups for a in arrs]
    scratch = []
    for _, arrs in groups:
        n = len(arrs)
        scratch += [pltpu.SemaphoreType.DMA((n, N_DEV)), pltpu.SemaphoreType.DMA((n, N_DEV)), pltpu.SemaphoreType.DMA((n,))]
    return dict(
        n=len(arrays),
        operands=[a for _, a in arrays],
        in_specs=[ANY] * len(arrays),
        out_specs=[ANY] * len(arrays),
        out_shape=[jax.ShapeDtypeStruct(_landing_shape(kind, a), a.dtype) for kind, a in arrays],
        scratch_shapes=scratch,
    )


def _carry(groups, src, out, sems):
    x, y, c = _position()
    chip = 2 * x + y

    def remote(s, d, send_sems, recv_sems, a, m, to):
        return pltpu.make_async_remote_copy(
            src_ref=s, dst_ref=d, send_sem=send_sems.at[a, m], recv_sem=recv_sems.at[a, m], device_id=to,
            device_id_type=MESH,
        )

    def copies():
        far, near = [], []
        at = 0
        for g, (kind, arrs) in enumerate(groups):
            send_sems, recv_sems, local_sems = sems[3 * g : 3 * g + 3]
            for a in range(len(arrs)):
                s, d = src[at + a], out[at + a]
                if kind == "pair" and s.shape[0] == 1:
                    for b in range(N_CHIPS):
                        far.append(remote(s.at[0, 0], d.at[b], send_sems, recv_sems, a, 1 + b, (x, y, 1 - c)))
                elif kind == "pair":
                    far.append(remote(s.at[:, 1 - c], d, send_sems, recv_sems, a, 1, (x, y, 1 - c)))
                else:
                    assert kind == "chips", kind
                    for m in range(1, N_CHIPS):
                        px, py = x ^ (m >> 1), y ^ (m & 1)
                        far.append(remote(s.at[2 * px + py], d.at[chip], send_sems, recv_sems, a, m, (px, py, c)))
                    near.append(pltpu.make_async_copy(s.at[chip], d.at[chip], local_sems.at[a]))
            at += len(arrs)
        return far, near

    def start():
        far, near = copies()
        for cp in near + far:
            cp.start()

    def finish():
        far, near = copies()
        for cp in far:
            cp.wait_recv()
        for cp in far:
            cp.wait_send()
        for cp in near:
            cp.wait()

    return start, finish


def _pair_sum(mine, theirs, name, groups=()):
    n = len(mine)
    carried = _carry_specs(groups)
    n_c = carried["n"]
    core = lax.axis_index("c").astype(jnp.int32).reshape(1)

    def body(core_ref, *refs):
        del core_ref
        own = refs[:n]
        sib = refs[n : 2 * n]
        src = refs[2 * n : 2 * n + n_c]
        out = refs[2 * n + n_c : 3 * n + n_c]
        landed = refs[3 * n + n_c : 3 * n + 2 * n_c]
        start, finish = _carry(groups, src, landed, refs[3 * n + 2 * n_c :])
        b = pl.program_id(0)

        @pl.when(b == 0)
        def _():
            start()

        for a in range(n):
            out[a][...] = (own[a][...].astype(F32) + sib[a][...].astype(F32)).astype(out[a].dtype)

        @pl.when(b == N_CHIPS - 1)
        def _():
            finish()

    block = lambda a: pl.BlockSpec((None,) + a.shape[1:], lambda b, core_ref: (b, 0, 0))
    res = pl.pallas_call(
        body,
        name=name,
        grid_spec=pltpu.PrefetchScalarGridSpec(
            num_scalar_prefetch=1,
            grid=(N_CHIPS,),
            in_specs=[
                pl.BlockSpec((None, None) + a.shape[2:], lambda b, core_ref: (0, 0, 0, 0))
                if a.shape[0] == 1
                else pl.BlockSpec((None, None) + a.shape[2:], lambda b, core_ref: (b, core_ref[0], 0, 0))
                for a in mine
            ]
            + [block(a) for a in theirs]
            + carried["in_specs"],
            out_specs=[block(a) for a in theirs] + carried["out_specs"],
            scratch_shapes=carried["scratch_shapes"],
        ),
        out_shape=[jax.ShapeDtypeStruct(a.shape, a.dtype) for a in theirs] + carried["out_shape"],
        compiler_params=_params("arbitrary"),
    )(core, *mine, *theirs, *carried["operands"])
    return res[:n], res[n:]


def _weight_grad(a, b, n_blk, blocked, name, groups):
    s = b.shape[0]
    t = min(TILE_WEIGHT_GRAD, s)
    n_t = s // t
    n_pairs = n_blk // 2
    dims = TN
    if blocked == "cols":
        k, c = a.shape[1], b.shape[1] // n_blk
        a_spec = pl.BlockSpec((t, k), lambda j, i: (i, 0))
        b_spec = pl.BlockSpec((t, 2 * c), lambda j, i: (i, j))
        acc_shape = (k, 2 * c)
    elif blocked == "cols_t":
        k, c = a.shape[0], b.shape[1] // n_blk
        a_spec = pl.BlockSpec((k, t), lambda j, i: (0, i))
        b_spec = pl.BlockSpec((t, 2 * c), lambda j, i: (i, j))
        acc_shape = (k, 2 * c)
        dims = NN
    else:
        k, c = a.shape[1] // n_blk, b.shape[1]
        a_spec = pl.BlockSpec((t, 2 * k), lambda j, i: (i, j))
        b_spec = pl.BlockSpec((t, c), lambda j, i: (i, 0))
        acc_shape = (2 * k, c)
    carried = _carry_specs(groups)
    n_p = carried["n"]

    def body(a_ref, b_ref, *refs):
        src = refs[:n_p]
        o_ref, theirs_ref = refs[n_p : n_p + 2]
        landed = refs[n_p + 2 : 2 * n_p + 2]
        acc_ref, sbuf, pair_send, pair_recv = refs[2 * n_p + 2 : 2 * n_p + 6]
        start, finish = _carry(groups, src, landed, refs[2 * n_p + 6 :])
        j = pl.program_id(0)
        i = pl.program_id(1)
        x, y, c_me = _position()

        def to_sibling(pair):
            return pltpu.make_async_remote_copy(
                src_ref=sbuf.at[1 - c_me], dst_ref=theirs_ref.at[pair], send_sem=pair_send.at[pair],
                recv_sem=pair_recv.at[pair], device_id=(x, y, 1 - c_me), device_id_type=MESH,
            )

        @pl.when(jnp.logical_and(j == 0, i == 0))
        def _():
            start()

        @pl.when(i == 0)
        def _():
            acc_ref[...] = jnp.zeros_like(acc_ref)

        acc_ref[...] += _dot(a_ref[...], b_ref[...], dims)

        @pl.when(i == n_t - 1)
        def _():
            for pair in range(1, n_pairs):

                @pl.when(j == pair)
                def _():
                    to_sibling(pair - 1).wait_send()

            for half in range(2):
                if blocked != "rows":
                    block = acc_ref[:, half * c : (half + 1) * c].astype(BF16)
                else:
                    block = acc_ref[half * k : (half + 1) * k, :].astype(BF16)
                o_ref[half] = block
                sbuf[half] = block
            for pair in range(n_pairs):

                @pl.when(j == pair)
                def _():
                    to_sibling(pair).start()

        @pl.when(jnp.logical_and(j == n_pairs - 1, i == n_t - 1))
        def _():
            to_sibling(n_pairs - 1).wait_send()
            for pair in range(n_pairs):
                to_sibling(pair).wait_recv()
            finish()

    res = pl.pallas_call(
        body,
        name=name,
        grid=(n_pairs, n_t),
        in_specs=[a_spec, b_spec] + carried["in_specs"],
        out_specs=[pl.BlockSpec((2, k, c), lambda j, i: (j, 0, 0)), ANY] + carried["out_specs"],
        out_shape=[jax.ShapeDtypeStruct((n_blk, k, c), BF16), jax.ShapeDtypeStruct((n_pairs, k, c), BF16)]
        + carried["out_shape"],
        scratch_shapes=[
            pltpu.VMEM(acc_shape, F32), pltpu.VMEM((2, k, c), BF16), pltpu.SemaphoreType.DMA((n_pairs,)),
            pltpu.SemaphoreType.DMA((n_pairs,)),
        ]
        + carried["scratch_shapes"],
        compiler_params=_params("arbitrary", "arbitrary", vmem_limit_bytes=VMEM_LIMIT_MIX_BYTES),
    )(a, b, *carried["operands"])
    return res[0], res[1], res[2:]


def _input_grad(dproj, win_t, x, dxo, norm_pre, groups):
    s, d = x.shape
    t = min(TILE_GRAD, s)
    n_t = s // t
    kb = 2 * WIN_BLK
    n_k = win_t.shape[0] // kb
    carried = _carry_specs(groups)
    n_p = carried["n"]

    def body(dp_ref, w_ref, x_ref, dxo_ref, g_ref, *refs):
        src = refs[:n_p]
        gx_ref, dg_all = refs[n_p : n_p + 2]
        landed = refs[n_p + 2 : 2 * n_p + 2]
        acc_ref, dg_ref, dg_send, dg_recv, dg_local = refs[2 * n_p + 2 : 2 * n_p + 7]
        start, finish = _carry(groups, src, landed, refs[2 * n_p + 7 :])
        i = pl.program_id(0)
        j = pl.program_id(1)
        px, py, pc = _position()
        me = 4 * px + 2 * py + pc

        def dg_copies():
            far = [
                pltpu.make_async_remote_copy(
                    src_ref=dg_ref, dst_ref=dg_all.at[me], send_sem=dg_send.at[m], recv_sem=dg_recv.at[m],
                    device_id=(px ^ ((m >> 2) & 1), py ^ ((m >> 1) & 1), pc ^ (m & 1)), device_id_type=MESH,
                )
                for m in range(1, N_DEV)
            ]
            return far, pltpu.make_async_copy(dg_ref, dg_all.at[me], dg_local)

        @pl.when(jnp.logical_and(i == 0, j == 0))
        def _():
            start()
            dg_ref[...] = jnp.zeros_like(dg_ref)

        @pl.when(j == 0)
        def _():
            acc_ref[...] = jnp.zeros_like(acc_ref)

        acc_ref[...] += _dot(dp_ref[...], w_ref[...])

        @pl.when(j == n_k - 1)
        def _():
            xv = x_ref[...]
            gain = g_ref[...]
            r = lax.rsqrt(_rowmean(xv * xv) + EPS)
            dgain, dx = _rms_bwd(acc_ref[...], gain, r, xv * r)
            dg_ref[...] += dgain
            gx_ref[...] = dxo_ref[...] + dx

        @pl.when(jnp.logical_and(i == n_t - 1, j == n_k - 1))
        def _():
            far, near = dg_copies()
            for cp in [near] + far:
                cp.start()
            finish()
            for cp in far:
                cp.wait_recv()
            for cp in far:
                cp.wait_send()
            near.wait()

    res = pl.pallas_call(
        body,
        name="input_grad",
        grid=(n_t, n_k),
        in_specs=[
            pl.BlockSpec((t, kb), lambda i, j: (i, j)),
            pl.BlockSpec((kb, d), lambda i, j: (j, 0)),
            pl.BlockSpec((t, d), lambda i, j: (i, 0)),
            pl.BlockSpec((t, d), lambda i, j: (i, 0)),
            _full((1, d)),
        ]
        + carried["in_specs"],
        out_specs=[pl.BlockSpec((t, d), lambda i, j: (i, 0)), ANY] + carried["out_specs"],
        out_shape=[jax.ShapeDtypeStruct((s, d), F32), jax.ShapeDtypeStruct((N_DEV, 1, d), F32)] + carried["out_shape"],
        scratch_shapes=[
            pltpu.VMEM((t, d), F32), pltpu.VMEM((1, d), F32), pltpu.SemaphoreType.DMA((N_DEV,)),
            pltpu.SemaphoreType.DMA((N_DEV,)), pltpu.SemaphoreType.DMA,
        ]
        + carried["scratch_shapes"],
        compiler_params=_params("arbitrary", "arbitrary", vmem_limit_bytes=VMEM_LIMIT_MIX_BYTES),
    )(dproj, win_t, x, dxo, norm_pre, *carried["operands"])
    return res[0], res[1], res[2:]


def _kv_backward(dkv, memn, wkv, mem):
    m, d = mem.shape
    n = wkv.shape[1]

    def body(dkv_ref, memn_ref, w_ref, mem_ref, gw_ref, dg_ref):
        dkv_bf = dkv_ref[...].astype(BF16)
        gw_ref[...] = _dot(memn_ref[...], dkv_bf, TN).astype(BF16).reshape(N_DEV, WKV_BLK, n)
        dmemn = _dot(dkv_bf, w_ref[...], NT)
        mv = mem_ref[...]
        r = lax.rsqrt(_rowmean(mv * mv) + EPS)
        dg_ref[...] = _colsum(dmemn * (mv * r))

    return pl.pallas_call(
        body,
        name="kv_backward",
        grid=(1,),
        in_specs=[_full((m, n)), _full((m, d)), _full(wkv.shape), _full((m, d))],
        out_specs=[_full((N_DEV, WKV_BLK, n)), _full((1, d))],
        out_shape=[jax.ShapeDtypeStruct((N_DEV, WKV_BLK, n), BF16), jax.ShapeDtypeStruct((1, d), F32)],
        compiler_params=_params("arbitrary"),
    )(dkv, memn, wkv, mem)


def _adamw_math(w, g, m, v):
    m = ADAM_B1 * m + (1.0 - ADAM_B1) * g
    v = ADAM_B2 * v + (1.0 - ADAM_B2) * (g * g)
    m_hat = m / (1.0 - ADAM_B1**ADAM_STEP)
    v_hat = v / (1.0 - ADAM_B2**ADAM_STEP)
    delta = -ADAM_LR * (m_hat / (jnp.sqrt(v_hat) + ADAM_EPS) + ADAM_WD * w)
    return delta, m, v


def _adamw(parts, w, m, v, name):
    r, c = w.shape
    slots = parts.shape[0]
    t = r
    while t * c * 4 > TILE_ADAM_BYTES and t % 16 == 0:
        t //= 2

    def body(p_ref, w_ref, m_ref, v_ref, g_ref, d_ref, nm_ref, nv_ref):
        g = p_ref[0].astype(F32)
        for k in range(1, slots):
            g = g + p_ref[k].astype(F32)
        delta, nm, nv = _adamw_math(w_ref[...], g, m_ref[...], v_ref[...])
        g_ref[...] = g
        d_ref[...] = delta
        nm_ref[...] = nm
        nv_ref[...] = nv

    tile = pl.BlockSpec((t, c), lambda i: (i, 0))
    return pl.pallas_call(
        body,
        name=name,
        grid=(r // t,),
        in_specs=[pl.BlockSpec((slots, t, c), lambda i: (0, i, 0)), tile, tile, tile],
        out_specs=[tile] * 4,
        out_shape=[jax.ShapeDtypeStruct((r, c), F32)] * 4,
        compiler_params=_params("parallel"),
    )(parts, w, m, v)


def _adamw_packed(parts, triples, name):
    slots = parts.shape[0]
    sizes = [w.shape[0] for w, _, _ in triples]
    rest = parts.shape[1] - sum(sizes)

    def total(p_ref, at, rows):
        g = p_ref[0, at : at + rows, :]
        for k in range(1, slots):
            g = g + p_ref[k, at : at + rows, :]
        return g

    def body(p_ref, *refs):
        ins = refs[: 3 * len(triples)]
        outs = refs[3 * len(triples) :]
        at = 0
        for n, rows in enumerate(sizes):
            g = total(p_ref, at, rows)
            w_ref, m_ref, v_ref = ins[3 * n : 3 * n + 3]
            delta, nm, nv = _adamw_math(w_ref[...], g, m_ref[...], v_ref[...])
            for ref, val in zip(outs[4 * n : 4 * n + 4], (g, delta, nm, nv)):
                ref[...] = val
            at += rows
        if rest:
            outs[-1][...] = total(p_ref, at, rest)

    flat = [a for t in triples for a in t]
    res = pl.pallas_call(
        body,
        name=name,
        out_shape=[jax.ShapeDtypeStruct(w.shape, F32) for w, _, _ in triples for _ in range(4)]
        + ([jax.ShapeDtypeStruct((rest, 128), F32)] if rest else []),
        compiler_params=pltpu.CompilerParams(vmem_limit_bytes=VMEM_LIMIT_BYTES),
    )(parts, *flat)
    return [res[4 * n : 4 * n + 4] for n in range(len(triples))], (res[-1] if rest else None)


SMALL = ("norm_pre", "pool_scale", "sgu_ln_g", "sgu_ln_b", "sgu_w", "sgu_b", "mem_norm", "branch_norm", "norm_post")


def _local_view(name, w):
    if name == "sgu_w":
        return w.reshape(SGU_HEADS, SGU_CHUNK, SGU_CHUNK)
    if name == "sgu_b":
        return w.reshape(SGU_HEADS, SGU_CHUNK)
    return w.reshape(1, -1)


def _forward_backward(x, mem, target, shards, small):
    causal = jnp.tril(jnp.ones((SGU_CHUNK, SGU_CHUNK), dtype=bool))
    sgu_wm = jnp.where(causal[None], small["sgu_w"], 0.0).astype(BF16)
    sgu_bias = jnp.repeat(jnp.transpose(small["sgu_b"]), SGU_HEAD_DIM, axis=1)

    proj, h, (win, wkv, pool_all, wout) = _proj_gather(x, small["norm_pre"], shards)
    wout = wout.reshape(MIX_WIDTH, D_MODEL)
    wkv = wkv.reshape(D_MODEL, 2 * BRANCH)
    pool_full = (
        pool_all.reshape(N_DEV, len(POOL_WINDOWS), POOL_BLK, POOL_GROUP_DIM)
        .transpose(1, 0, 2, 3)
        .reshape(len(POOL_WINDOWS), POOL_GROUP_DIM, POOL_GROUP_DIM)
    )
    memn, kv, kv_t = _kv_forward(mem, small["mem_norm"], wkv)
    (y, dout, dxo, dproj, loss, d_norm_post, d_branch_norm, d_pool_scale, d_ln_g, d_ln_b, d_pool_w, d_sgu_w, d_sgu_b,
     dkv) = _mix(
        proj, x, target, kv, kv_t, wout, wout.T, pool_full, jnp.swapaxes(pool_full, 1, 2), small["pool_scale"],
        small["sgu_ln_g"], small["sgu_ln_b"], sgu_bias, sgu_wm, jnp.swapaxes(sgu_wm, 1, 2), small["branch_norm"],
        small["norm_post"],
    )
    g_wkv, d_mem_norm = _kv_backward(dkv, memn, wkv, mem)
    g_pool = (
        d_pool_w.reshape(len(POOL_WINDOWS), N_DEV, POOL_BLK, POOL_GROUP_DIM)
        .transpose(1, 0, 2, 3)
        .reshape(N_DEV, len(POOL_WINDOWS) * POOL_BLK, POOL_GROUP_DIM)
        .astype(BF16)
    )
    small_grads = dict(
        pool_scale=d_pool_scale, sgu_ln_g=d_ln_g, sgu_ln_b=d_ln_b, sgu_w=d_sgu_w, sgu_b=d_sgu_b,
        mem_norm=d_mem_norm, branch_norm=d_branch_norm, norm_post=d_norm_post,
    )
    packed = jnp.concatenate(
        [small_grads[n].reshape(-1, LANES) for n in SMALL if n != "norm_pre"]
        + [jnp.broadcast_to(loss, (SUBLANES, LANES))],
        axis=0,
    )
    packed = packed[None, None]

    by_chip = lambda g: g.reshape((N_CHIPS, 2) + g.shape[1:])
    small_mine = [by_chip(g_wkv), by_chip(g_pool), packed]
    g_wout, wout_theirs, small_theirs = _weight_grad(y, dout, N_DEV, "rows", "grad_w_out", [("pair", small_mine)])
    sums, _ = _pair_sum(small_mine + [by_chip(g_wout)], list(small_theirs) + [wout_theirs], "pair_sum_first")
    h_late, _ = lax.optimization_barrier((h, dout))
    g_win, win_theirs, (l_wkv, l_pool, l_packed, l_wout) = _weight_grad(
        jnp.swapaxes(h_late, 0, 1), dproj, N_DEV, "cols_t", "grad_w_in", [("chips", list(sums))]
    )
    (win_sums,), _ = _pair_sum([by_chip(g_win)], [win_theirs], "pair_sum_w_in")
    win_late, _ = lax.optimization_barrier((win, g_wkv))
    grad_x, d_norm_pre, (l_win,) = _input_grad(
        dproj, jnp.swapaxes(win_late, 1, 2).reshape(IN_WIDTH, D_MODEL), x, dxo, small["norm_pre"],
        [("chips", [win_sums])],
    )
    return grad_x, dict(w_in=l_win, w_out=l_wout, w_kv=l_wkv, pool_w=l_pool), l_packed, d_norm_pre


def kernel(x, mem, norm_pre, w_in, pool_w, pool_scale, sgu_ln_g, sgu_ln_b, sgu_w, sgu_b, mem_norm, w_kv, branch_norm, w_out, norm_post, loss_target, m_norm_pre, m_w_in, m_pool_w, m_pool_scale, m_sgu_ln_g, m_sgu_ln_b, m_sgu_w, m_sgu_b, m_mem_norm, m_w_kv, m_branch_norm, m_w_out, m_norm_post, v_norm_pre, v_w_in, v_pool_w, v_pool_scale, v_sgu_ln_g, v_sgu_ln_b, v_sgu_w, v_sgu_b, v_mem_norm, v_w_kv, v_branch_norm, v_w_out, v_norm_post):
    weights = dict(norm_pre=norm_pre, w_in=w_in, pool_w=pool_w, pool_scale=pool_scale, sgu_ln_g=sgu_ln_g, sgu_ln_b=sgu_ln_b, sgu_w=sgu_w, sgu_b=sgu_b, mem_norm=mem_norm, w_kv=w_kv, branch_norm=branch_norm, w_out=w_out, norm_post=norm_post)
    first = dict(norm_pre=m_norm_pre, w_in=m_w_in, pool_w=m_pool_w, pool_scale=m_pool_scale, sgu_ln_g=m_sgu_ln_g, sgu_ln_b=m_sgu_ln_b, sgu_w=m_sgu_w, sgu_b=m_sgu_b, mem_norm=m_mem_norm, w_kv=m_w_kv, branch_norm=m_branch_norm, w_out=m_w_out, norm_post=m_norm_post)
    second = dict(norm_pre=v_norm_pre, w_in=v_w_in, pool_w=v_pool_w, pool_scale=v_pool_scale, sgu_ln_g=v_sgu_ln_g, sgu_ln_b=v_sgu_ln_b, sgu_w=v_sgu_w, sgu_b=v_sgu_b, mem_norm=v_mem_norm, w_kv=v_w_kv, branch_norm=v_branch_norm, w_out=v_w_out, norm_post=v_norm_post)
    order = ("norm_pre", "w_in", "pool_w", "pool_scale", "sgu_ln_g", "sgu_ln_b", "sgu_w", "sgu_b", "mem_norm", "w_kv", "branch_norm", "w_out", "norm_post")

    owned_shape = dict(
        w_in=(D_MODEL, WIN_BLK), w_out=(WOUT_BLK, D_MODEL), w_kv=(WKV_BLK, 2 * BRANCH),
        pool_w=(len(POOL_WINDOWS) * POOL_BLK, POOL_GROUP_DIM),
    )
    owned = {n: weights[n].reshape(owned_shape[n]) for n in owned_shape}
    small = {n: _local_view(n, weights[n]) for n in SMALL}
    grad_x, landed, landed_packed, d_norm_pre = _forward_backward(
        x[0], mem[0], loss_target[0],
        [owned["w_in"].astype(BF16)] + [owned[n] for n in ("w_kv", "pool_w", "w_out")], small,
    )
    landed_norm_pre = d_norm_pre.reshape(N_DEV, -1, LANES)

    grads, deltas, new_m, new_v = {}, {}, {}, {}
    for n in owned_shape:
        shape = weights[n].shape
        res = _adamw(
            landed[n], owned[n], first[n].reshape(owned_shape[n]), second[n].reshape(owned_shape[n]), "adamw_" + n
        )
        grads[n], deltas[n], new_m[n], new_v[n] = (a.reshape(shape) for a in res)
    rows_of = lambda tree, n: tree[n].reshape(-1, 128)
    for names, parts, name in (
        ([n for n in SMALL if n != "norm_pre"], landed_packed, "adamw_replicated"),
        (["norm_pre"], landed_norm_pre, "adamw_norm_pre"),
    ):
        res, rest = _adamw_packed(
            parts, [(rows_of(weights, n), rows_of(first, n), rows_of(second, n)) for n in names], name
        )
        if rest is not None:
            total = rest[0, 0]
        for n, four in zip(names, res):
            for tree, a in zip((grads, deltas, new_m, new_v), four):
                tree[n] = a.reshape(weights[n].shape)

    return (
        total,
        grad_x[None],
        *[grads[n] for n in order],
        *[deltas[n] for n in order],
        *[new_m[n] for n in order],
        *[new_v[n] for n in order],
    )
```

```python
import jax
import jax.numpy as jnp
from jax import lax
from jax.experimental import pallas as pl
from jax.experimental.pallas import tpu as pltpu

F32 = jnp.float32
BF16 = jnp.bfloat16
EPS = 1e-6

D_MODEL = 2048
POOL_WINDOWS = (2, 4, 8, 16)
POOL_GROUP_DIM = 256
BRANCH = 1024
SGU_CHUNK = 128
SGU_HEADS = 8
SGU_HEAD_DIM = 128
XATTN_HEADS = 4
XATTN_HEAD_DIM = 256
MIX_WIDTH = 3 * BRANCH
IN_WIDTH = 7 * BRANCH
N_DEV = 8
WIN_BLK = IN_WIDTH // N_DEV
WOUT_BLK = MIX_WIDTH // N_DEV
WKV_BLK = D_MODEL // N_DEV
POOL_BLK = POOL_GROUP_DIM // N_DEV
HALO = 16
LANES = 128
SUBLANES = 8

ADAM_LR = 0.001
ADAM_B1 = 0.9
ADAM_B2 = 0.999
ADAM_EPS = 1e-08
ADAM_WD = 0.01
ADAM_STEP = 10

VMEM_LIMIT_BYTES = 56 * 1024 * 1024
VMEM_LIMIT_MIX_BYTES = 63 * 1024 * 1024

TILE_PROJ = 512
TILE_MIX = 128
TILE_GRAD = 512
TILE_WEIGHT_GRAD = 1024
TILE_ADAM_BYTES = 1 << 20

ANY = pl.BlockSpec(memory_space=pl.ANY)
NN = (((1,), (0,)), ((), ()))
NT = (((1,), (1,)), ((), ()))
TN = (((0,), (0,)), ((), ()))
MESH = pl.DeviceIdType.MESH


def _dot(a, b, dims=NN):
    return lax.dot_general(a, b, dims, preferred_element_type=F32)


def _params(*semantics, vmem_limit_bytes=VMEM_LIMIT_BYTES):
    return pltpu.CompilerParams(dimension_semantics=semantics, vmem_limit_bytes=vmem_limit_bytes)


def _rowmean(a):
    return jnp.mean(a, axis=-1, keepdims=True)


def _colsum(a):
    return jnp.sum(a, axis=0, keepdims=True)


def _full(shape):
    zeros = (0,) * len(shape)
    return pl.BlockSpec(shape, lambda *_: zeros)


def _resident(shape):
    zeros = (0,) * len(shape)
    return pl.BlockSpec(shape, lambda *_: zeros, pipeline_mode=pl.Buffered(1))


def _kv_forward(mem, mem_norm, wkv):
    m, d = mem.shape
    n = wkv.shape[1]
    cols = 4 * LANES

    def body(mem_ref, g_ref, w_ref, memn_ref, kv_ref, kvt_ref):
        mv = mem_ref[...]
        r = lax.rsqrt(_rowmean(mv * mv) + EPS)
        memn = (mv * r * g_ref[...]).astype(BF16)
        memn_ref[...] = memn
        kv = _dot(memn, w_ref[...])
        kv_ref[...] = kv.astype(BF16)
        kvt_ref[...] = kv.T.astype(BF16)

    return pl.pallas_call(
        body,
        name="kv_forward",
        grid=(n // cols,),
        in_specs=[_full((m, d)), _full((1, d)), pl.BlockSpec((d, cols), lambda j: (0, j))],
        out_specs=[_full((m, d)), pl.BlockSpec((m, cols), lambda j: (0, j)), pl.BlockSpec((cols, m), lambda j: (j, 0))],
        out_shape=[
            jax.ShapeDtypeStruct((m, d), BF16), jax.ShapeDtypeStruct((m, n), BF16), jax.ShapeDtypeStruct((n, m), BF16)
        ],
        compiler_params=_params("arbitrary"),
    )(mem, mem_norm, wkv)


def _proj_gather(x_in, norm_pre, shards):
    s, d = x_in.shape
    t = min(TILE_PROJ, s)
    n_t = s // t
    n_arr = len(shards)

    def places(x, y, c):
        return (x, y, c), (x, y, 1 - c), (x ^ c, y ^ (1 - c)), (x ^ (1 - c), y ^ c), (1 - x, 1 - y)

    def index(chip, core):
        return 4 * chip[0] + 2 * chip[1] + core

    _, _, chip_a, chip_b, chip_d = places(*_position())
    c_out = lax.axis_index("c")
    me_out = index((lax.axis_index("x"), lax.axis_index("y")), c_out)
    order = jnp.stack(
        [
            me_out, me_out ^ 1, index(chip_a, c_out), index(chip_b, c_out), index(chip_b, 1 - c_out),
            index(chip_a, 1 - c_out), index(chip_d, c_out), index(chip_d, 1 - c_out),
        ]
    ).astype(jnp.int32)

    def body(order_ref, x_ref, g_ref, *refs):
        del order_ref
        raw = refs[:n_arr]
        proj_ref, h_ref = refs[n_arr : n_arr + 2]
        out = refs[n_arr + 2 : 2 * n_arr + 2]
        staged = refs[2 * n_arr + 2 : 3 * n_arr + 1]
        src = (raw[0],) + tuple(staged)
        wbuf, hs, send_sems, recv_sems, local_sems, load_sems = refs[3 * n_arr + 1 : 3 * n_arr + 7]
        wide = refs[3 * n_arr + 7 : 4 * n_arr + 6]
        narrow = refs[4 * n_arr + 6 :]
        j = pl.program_id(0)
        i = pl.program_id(1)
        me, sibling, chip_a, chip_b, chip_d = places(*_position())
        c = me[2]

        def block(a, chip, core):
            return out[a].at[index(chip, core)]

        def copy(a, k, owner, to, from_input=False):
            return pltpu.make_async_remote_copy(
                src_ref=src[a] if from_input else block(a, *owner),
                dst_ref=block(a, *owner),
                send_sem=send_sems.at[a, k],
                recv_sem=recv_sems.at[a, k],
                device_id=to,
                device_id_type=MESH,
            )

        mine = (me[:2], c)

        def own(a):
            return pltpu.make_async_copy(src[a], block(a, *mine), local_sems.at[a])

        def first_sends(a):
            return [
                copy(a, 0, mine, sibling, from_input=True),
                copy(a, 1, mine, (*chip_a, c), from_input=True),
                copy(a, 2, mine, (*chip_b, c), from_input=True),
            ]

        def onward(a, k):
            owner = {3: chip_a, 4: chip_a, 5: chip_b, 6: chip_d}[k]
            return copy(a, k, (owner, c), (*chip_b, c) if k == 3 else sibling)

        def landed(a, k):
            owner = {0: mine[0], 1: chip_a, 2: chip_b, 3: chip_d, 4: chip_b, 5: chip_a, 6: chip_d}[k]
            core = c if k in (1, 2, 3) else 1 - c
            copy(a, k, (owner, core), me).wait_recv()
            return owner, core

        def load(ref, step):
            return pltpu.make_async_copy(ref, wbuf.at[step % 2], load_sems.at[step % 2])

        @pl.when(jnp.logical_and(j == 0, i == 0))
        def _():
            own(0).start()
            for cp in first_sends(0):
                cp.start()
            for a in range(1, n_arr):
                pltpu.sync_copy(raw[a], wide[a - 1])
                narrow[a - 1][...] = wide[a - 1][...].astype(BF16)
                pltpu.sync_copy(narrow[a - 1], staged[a - 1])
            load(src[0], 0).start()
            load(src[0], 0).wait()

        steps = {1: (0, ()), 2: (1, (3, 4)), 3: (2, (5,)), 4: (4, ()), 5: (5, ()), 6: (3, (6,)), 7: (6, ())}
        for step, (k, then) in steps.items():

            @pl.when(jnp.logical_and(j == step, i == 0))
            def _():
                load(src[0], step).wait()

            @pl.when(jnp.logical_and(j == step - 1, i == n_t - 1))
            def _():
                owner = landed(0, k)
                for k2 in then:
                    onward(0, k2).start()
                if k == 1:
                    for a in range(1, n_arr):
                        own(a).start()
                        for cp in first_sends(a):
                            cp.start()
                if k == 3:
                    for a in range(1, n_arr):
                        for k1, then1 in ((1, (3, 4)), (2, (5,))):
                            landed(a, k1)
                            for k2 in then1:
                                onward(a, k2).start()
                load(block(0, *owner), step).start()

        @pl.when(j == 0)
        def _():
            xv = x_ref[...]
            h = (xv * lax.rsqrt(_rowmean(xv * xv) + EPS) * g_ref[...]).astype(BF16)
            hs[i] = h
            h_ref[...] = h

        proj_ref[...] = _dot(hs[i], wbuf[j % 2]).astype(BF16)

        @pl.when(jnp.logical_and(j == N_DEV - 1, i == n_t - 1))
        def _():
            for a in range(1, n_arr):
                landed(a, 3)
                onward(a, 6).start()
            for a in range(1, n_arr):
                for k in (0, 4, 5, 6):
                    landed(a, k)
            for a in range(n_arr):
                for cp in first_sends(a) + [onward(a, k) for k in (3, 4, 5, 6)]:
                    cp.wait_send()
                own(a).wait()

    res = pl.pallas_call(
        body,
        name="proj_gather",
        grid_spec=pltpu.PrefetchScalarGridSpec(
            num_scalar_prefetch=1,
            grid=(N_DEV, n_t),
            in_specs=[
                pl.BlockSpec((t, d), lambda j, i, order_ref: (jnp.where(j == 0, i, n_t - 1), 0)),
                pl.BlockSpec((1, d), lambda j, i, order_ref: (0, 0)),
            ]
            + [ANY] * n_arr,
            out_specs=[
                pl.BlockSpec((t, WIN_BLK), lambda j, i, order_ref: (i, order_ref[j])),
                pl.BlockSpec((t, d), lambda j, i, order_ref: (jnp.where(j == 0, i, n_t - 1), 0)),
            ]
            + [ANY] * (2 * n_arr - 1),
            scratch_shapes=[
                pltpu.VMEM((2,) + shards[0].shape, BF16),
                pltpu.VMEM((n_t, t, d), BF16),
                pltpu.SemaphoreType.DMA((n_arr, 7)),
                pltpu.SemaphoreType.DMA((n_arr, 7)),
                pltpu.SemaphoreType.DMA((n_arr,)),
                pltpu.SemaphoreType.DMA((2,)),
            ]
            + [pltpu.VMEM(a.shape, F32) for a in shards[1:]]
            + [pltpu.VMEM(a.shape, BF16) for a in shards[1:]],
        ),
        out_shape=[jax.ShapeDtypeStruct((s, IN_WIDTH), BF16), jax.ShapeDtypeStruct((s, d), BF16)]
        + [jax.ShapeDtypeStruct((N_DEV,) + a.shape, BF16) for a in shards]
        + [jax.ShapeDtypeStruct(a.shape, BF16) for a in shards[1:]],
        compiler_params=_params("arbitrary", "arbitrary"),
    )(order, x_in, norm_pre, *shards)
    return res[0], res[1], res[2 : 2 + n_arr]


def _sigmoid(a):
    return jax.nn.sigmoid(a)


def _dsilu(a, sg):
    return sg * (1.0 + a * (1.0 - sg))


def _rms_fwd(u, gain):
    r = lax.rsqrt(_rowmean(u * u) + EPS)
    n = u * r
    return r, n, n * gain


def _rms_bwd(dy, gain, r, n):
    dn = dy * gain
    return _colsum(dy * n), r * (dn - n * _rowmean(dn * n))


def _mix(proj, x, target, kv, kv_t, wout, wout_t, pool_w, pool_w_t, pool_scale, ln_g, ln_b, sgu_bias, sgu_wm, sgu_wm_t, branch_norm, norm_post):
    s, d = x.shape
    t = min(TILE_MIX, s)
    n_tiles = s // t
    n_chunks = t // SGU_CHUNK
    halo_blocks_per_tile = t // HALO
    inv_d = 1.0 / d
    scale = 1.0 / (XATTN_HEAD_DIM**0.5)

    def body(
        proj_ref, halo_ref, x_ref, tgt_ref, kv_ref, kvt_ref, wout_hbm, wout_t_hbm, pw_ref, pwt_ref, pscale_ref, lng_ref,
        lnb_ref, bias_ref, wm_ref, wmt_ref, bnorm_ref, gpost_ref,
        y_ref, dout_ref, dxo_ref, dproj_ref, loss_ref, dgpost_ref, dbnorm_ref, dpscale_ref, dlng_ref, dlnb_ref,
        dpw_out, dwm_out, dbias_ref, dkv_out,
        carry_ref, dzsum_ref, dpw_ref, dwm_ref, dkv_ref, wout_ref, wout_t_ref, wout_sems,
    ):
        i = pl.program_id(0)
        tile = n_tiles - 1 - i
        wout_load = pltpu.make_async_copy(wout_hbm, wout_ref, wout_sems.at[0])
        wout_t_load = pltpu.make_async_copy(wout_t_hbm, wout_t_ref, wout_sems.at[1])

        @pl.when(i == 0)
        def _():
            wout_load.start()
            wout_t_load.start()
            carry_ref[...] = jnp.zeros_like(carry_ref)
            dzsum_ref[...] = jnp.zeros_like(dzsum_ref)
            for ref in (loss_ref, dgpost_ref, dbnorm_ref, dpscale_ref, dlng_ref, dlnb_ref, dpw_ref, dwm_ref, dkv_ref):
                ref[...] = jnp.zeros_like(ref)

        t_glob = tile * t + lax.broadcasted_iota(jnp.int32, (t, 1), 0)
        inv_cnt = [1.0 / jnp.minimum(t_glob + 1, w).astype(F32) for w in POOL_WINDOWS]

        xa = proj_ref[:, 0:BRANCH].astype(F32)
        ga = proj_ref[:, BRANCH : 2 * BRANCH].astype(F32)
        halo = jnp.where(tile == 0, 0.0, halo_ref[...].astype(F32))
        d_bf, pm_parts = [], []
        for g, w in enumerate(POOL_WINDOWS):
            cols = slice(g * POOL_GROUP_DIM, (g + 1) * POOL_GROUP_DIM)
            acc = jnp.concatenate([halo[:, cols], xa[:, cols]], axis=0)
            k = 1
            while k < w:
                acc = acc + pltpu.roll(acc, k, axis=0)
                k *= 2
            dg = (acc[HALO:, :] * inv_cnt[g] - xa[:, cols]).astype(BF16)
            d_bf.append(dg)
            pm_parts.append(_dot(dg, pw_ref[g]))
        pm = jnp.concatenate(pm_parts, axis=1)
        pscale = pscale_ref[...]
        pa = pm * pscale
        sga = _sigmoid(ga)
        sila = ga * sga
        ua = pa * sila
        g_a = bnorm_ref[:, 0:BRANCH]
        ra, na, ya = _rms_fwd(ua, g_a)

        u = proj_ref[:, 2 * BRANCH : 3 * BRANCH].astype(F32)
        v = proj_ref[:, 3 * BRANCH : 4 * BRANCH].astype(F32)
        gb = proj_ref[:, 4 * BRANCH : 5 * BRANCH].astype(F32)
        lng = lng_ref[...]
        vc = v - _rowmean(v)
        rstd = lax.rsqrt(_rowmean(vc * vc) + EPS)
        vhat = vc * rstd
        vn_bf = (vhat * lng + lnb_ref[...]).astype(BF16)
        z_rows = []
        for c in range(n_chunks):
            rows = slice(c * SGU_CHUNK, (c + 1) * SGU_CHUNK)
            z_rows.append(
                jnp.concatenate(
                    [
                        _dot(wm_ref[hd], vn_bf[rows, hd * SGU_HEAD_DIM : (hd + 1) * SGU_HEAD_DIM])
                        for hd in range(SGU_HEADS)
                    ],
                    axis=1,
                )
                + bias_ref[...]
            )
        z = z_rows[0] if n_chunks == 1 else jnp.concatenate(z_rows, axis=0)
        sb = u * z
        sgb = _sigmoid(gb)
        silb = gb * sgb
        ub = sb * silb
        g_b = bnorm_ref[:, BRANCH : 2 * BRANCH]
        rb, nb, yb = _rms_fwd(ub, g_b)

        q = proj_ref[:, 5 * BRANCH : 6 * BRANCH]
        gc = proj_ref[:, 6 * BRANCH : 7 * BRANCH].astype(F32)
        q_bf, p_bf, o_parts = [], [], []
        for hd in range(XATTN_HEADS):
            cols = slice(hd * XATTN_HEAD_DIM, (hd + 1) * XATTN_HEAD_DIM)
            qh = q[:, cols]
            sc = _dot(qh, kvt_ref[cols, :]) * scale
            e = jnp.exp(sc - jnp.max(sc, axis=-1, keepdims=True))
            p = e / jnp.sum(e, axis=-1, keepdims=True)
            q_bf.append(qh)
            p_bf.append(p.astype(BF16))
            o_parts.append(_dot(p_bf[hd], kv_ref[:, BRANCH + hd * XATTN_HEAD_DIM : BRANCH + (hd + 1) * XATTN_HEAD_DIM]))
        o = jnp.concatenate(o_parts, axis=1)
        sgc = _sigmoid(gc)
        silc = gc * sgc
        uc = o * silc
        g_c = bnorm_ref[:, 2 * BRANCH : 3 * BRANCH]
        rc, nc, yc = _rms_fwd(uc, g_c)

        @pl.when(i == 0)
        def _():
            wout_load.wait()

        out = None
        for b, y_branch in enumerate((ya, yb, yc)):
            rows = slice(b * BRANCH, (b + 1) * BRANCH)
            y_bf = y_branch.astype(BF16)
            y_ref[:, rows] = y_bf
            part = _dot(y_bf, wout_ref[rows, :])
            out = part if out is None else out + part
        gpost = gpost_ref[...]
        r_out = lax.rsqrt(_rowmean(out * out) + EPS)
        on = out * r_out
        err = x_ref[...] + on * gpost - tgt_ref[...]
        loss_ref[...] += 0.5 * jnp.sum(_rowmean(err * err), axis=0, keepdims=True)

        dxo = err * inv_d
        dxo_ref[...] = dxo
        dgp, dout = _rms_bwd(dxo, gpost, r_out, on)
        dgpost_ref[...] += dgp
        dout_bf = dout.astype(BF16)
        dout_ref[...] = dout_bf

        @pl.when(i == 0)
        def _():
            wout_t_load.wait()

        dy = [_dot(dout_bf, wout_t_ref[:, b * BRANCH : (b + 1) * BRANCH]) for b in range(3)]

        dg_a, dua = _rms_bwd(dy[0], g_a, ra, na)
        dg_b, dub = _rms_bwd(dy[1], g_b, rb, nb)
        dg_c, duc = _rms_bwd(dy[2], g_c, rc, nc)
        dbnorm_ref[...] += jnp.concatenate([dg_a, dg_b, dg_c], axis=1)

        dpa = dua * sila
        dga = dua * pa * _dsilu(ga, sga)
        dpscale_ref[...] += _colsum(dpa * pm)
        dpm = dpa * pscale
        dxa_parts, carry_parts = [], []
        for g, w in enumerate(POOL_WINDOWS):
            cols = slice(g * POOL_GROUP_DIM, (g + 1) * POOL_GROUP_DIM)
            dpm_g = dpm[:, cols].astype(BF16)
            dd = _dot(dpm_g, pwt_ref[g])
            dpw_ref[g] += _dot(d_bf[g], dpm_g, TN)
            cg = dd * inv_cnt[g]
            carry_parts.append(cg[0:HALO, :])
            acc = jnp.concatenate([cg, carry_ref[:, cols]], axis=0)
            k = 1
            while k < w:
                acc = acc + pltpu.roll(acc, t + HALO - k, axis=0)
                k *= 2
            dxa_parts.append(acc[0:t, :] - dd)
        carry_ref[...] = jnp.concatenate(carry_parts, axis=1)
        dxa = jnp.concatenate(dxa_parts, axis=1)

        dsb = dub * silb
        dgb = dub * sb * _dsilu(gb, sgb)
        du = dsb * z
        dz = dsb * u
        dz_bf = dz.astype(BF16)
        dvn_rows = []
        dz_sum = None
        for c in range(n_chunks):
            rows = slice(c * SGU_CHUNK, (c + 1) * SGU_CHUNK)
            dz_sum = dz[rows, :] if dz_sum is None else dz_sum + dz[rows, :]
            parts = []
            for hd in range(SGU_HEADS):
                cols = slice(hd * SGU_HEAD_DIM, (hd + 1) * SGU_HEAD_DIM)
                parts.append(_dot(wmt_ref[hd], dz_bf[rows, cols]))
                dwm_ref[hd] += _dot(dz_bf[rows, cols], vn_bf[rows, cols], NT)
            dvn_rows.append(jnp.concatenate(parts, axis=1))
        dzsum_ref[...] += dz_sum
        dvn = dvn_rows[0] if n_chunks == 1 else jnp.concatenate(dvn_rows, axis=0)
        dlng_ref[...] += _colsum(dvn * vhat)
        dlnb_ref[...] += _colsum(dvn)
        dvh = dvn * lng
        dv = rstd * (dvh - _rowmean(dvh) - vhat * _rowmean(dvh * vhat))

        do = duc * silc
        dgc = duc * o * _dsilu(gc, sgc)
        dq_parts = []
        for hd in range(XATTN_HEADS):
            cols = slice(hd * XATTN_HEAD_DIM, (hd + 1) * XATTN_HEAD_DIM)
            vcols = slice(BRANCH + hd * XATTN_HEAD_DIM, BRANCH + (hd + 1) * XATTN_HEAD_DIM)
            do_h = do[:, cols].astype(BF16)
            p = p_bf[hd].astype(F32)
            dp = _dot(do_h, kvt_ref[vcols, :])
            dkv_ref[:, vcols] += _dot(p_bf[hd], do_h, TN)
            ds_bf = (p * (dp - jnp.sum(dp * p, axis=-1, keepdims=True)) * scale).astype(BF16)
            dq_parts.append(_dot(ds_bf, kv_ref[:, cols]))
            dkv_ref[:, cols] += _dot(ds_bf, q_bf[hd], TN)
        dq = jnp.concatenate(dq_parts, axis=1)

        dproj_ref[...] = jnp.concatenate([dxa, dga, du, dv, dgb, dq, dgc], axis=1).astype(BF16)

        @pl.when(i == n_tiles - 1)
        def _():
            keep = lax.broadcasted_iota(jnp.int32, (SGU_CHUNK, SGU_CHUNK), 0) >= lax.broadcasted_iota(
                jnp.int32, (SGU_CHUNK, SGU_CHUNK), 1
            )
            for hd in range(SGU_HEADS):
                dwm_ref[hd] = jnp.where(keep, dwm_ref[hd], 0.0)
                per_pos = dzsum_ref[:, hd * SGU_HEAD_DIM : (hd + 1) * SGU_HEAD_DIM]
                dbias_ref[hd : hd + 1, :] = _colsum(per_pos.T)
            for acc, res in ((dpw_ref, dpw_out), (dwm_ref, dwm_out), (dkv_ref, dkv_out)):
                pltpu.sync_copy(acc, res)

    row_tile = lambda width: pl.BlockSpec((t, width), lambda i: (n_tiles - 1 - i, 0))
    halo_spec = pl.BlockSpec(
        (HALO, BRANCH), lambda i: (jnp.maximum((n_tiles - 1 - i) * halo_blocks_per_tile - 1, 0), 0)
    )
    acc_shapes = [
        (1, 128),
        (1, d),
        (1, MIX_WIDTH),
        (1, BRANCH),
        (1, BRANCH),
        (1, BRANCH),
        pool_w.shape,
        sgu_wm.shape,
        (SGU_HEADS, SGU_CHUNK),
        kv.shape,
    ]
    return pl.pallas_call(
        body,
        name="mix",
        grid=(n_tiles,),
        in_specs=[
            row_tile(IN_WIDTH), halo_spec, row_tile(d), row_tile(d), _resident(kv.shape), _resident(kv_t.shape),
            ANY, ANY, _resident(pool_w.shape), _resident(pool_w_t.shape),
            _full((1, BRANCH)), _full((1, BRANCH)), _full((1, BRANCH)), _resident((SGU_CHUNK, BRANCH)),
            _resident(sgu_wm.shape), _resident(sgu_wm_t.shape), _full((1, MIX_WIDTH)), _full((1, d)),
        ],
        out_specs=[row_tile(MIX_WIDTH), row_tile(d), row_tile(d), row_tile(IN_WIDTH)]
        + [ANY if len(a) == 3 or a == kv.shape else _full(a) for a in acc_shapes],
        out_shape=[
            jax.ShapeDtypeStruct((s, MIX_WIDTH), BF16),
            jax.ShapeDtypeStruct((s, d), BF16),
            jax.ShapeDtypeStruct((s, d), F32),
            jax.ShapeDtypeStruct((s, IN_WIDTH), BF16),
        ]
        + [jax.ShapeDtypeStruct(a, F32) for a in acc_shapes],
        scratch_shapes=[
            pltpu.VMEM((HALO, BRANCH), F32), pltpu.VMEM((SGU_CHUNK, BRANCH), F32), pltpu.VMEM(pool_w.shape, F32),
            pltpu.VMEM(sgu_wm.shape, F32), pltpu.VMEM(kv.shape, F32), pltpu.VMEM(wout.shape, BF16),
            pltpu.VMEM(wout_t.shape, BF16), pltpu.SemaphoreType.DMA((2,)),
        ],
        compiler_params=_params("arbitrary", vmem_limit_bytes=VMEM_LIMIT_MIX_BYTES),
    )(
        proj, proj, x, target, kv, kv_t, wout, wout_t, pool_w, pool_w_t, pool_scale, ln_g, ln_b, sgu_bias, sgu_wm,
        sgu_wm_t, branch_norm, norm_post,
    )


def _position():
    return lax.axis_index("x"), lax.axis_index("y"), lax.axis_index("c")


N_CHIPS = 4


def _landing_shape(kind, a):
    return (N_CHIPS,) + a.shape[2:] if kind == "pair" else a.shape


def _carry_specs(groups):
    arrays = [(kind, a) for kind, arrs in groups for a in arrs]
    scratch = []
    for _, arrs in groups:
        n = len(arrs)
        scratch += [pltpu.SemaphoreType.DMA((n, N_DEV)), pltpu.SemaphoreType.DMA((n, N_DEV)), pltpu.SemaphoreType.DMA((n,))]
    return dict(
        n=len(arrays),
        operands=[a for _, a in arrays],
        in_specs=[ANY] * len(arrays),
        out_specs=[ANY] * len(arrays),
        out_shape=[jax.ShapeDtypeStruct(_landing_shape(kind, a), a.dtype) for kind, a in arrays],
        scratch_shapes=scratch,
    )


def _carry(groups, src, out, sems):
    x, y, c = _position()
    chip = 2 * x + y

    def remote(s, d, send_sems, recv_sems, a, m, to):
        return pltpu.make_async_remote_copy(
            src_ref=s, dst_ref=d, send_sem=send_sems.at[a, m], recv_sem=recv_sems.at[a, m], device_id=to,
            device_id_type=MESH,
        )

    def copies():
        far, near = [], []
        at = 0
        for g, (kind, arrs) in enumerate(groups):
            send_sems, recv_sems, local_sems = sems[3 * g : 3 * g + 3]
            for a in range(len(arrs)):
                s, d = src[at + a], out[at + a]
                if kind == "pair" and s.shape[0] == 1:
                    for b in range(N_CHIPS):
                        far.append(remote(s.at[0, 0], d.at[b], send_sems, recv_sems, a, 1 + b, (x, y, 1 - c)))
                elif kind == "pair":
                    far.append(remote(s.at[:, 1 - c], d, send_sems, recv_sems, a, 1, (x, y, 1 - c)))
                else:
                    assert kind == "chips", kind
                    for m in range(1, N_CHIPS):
                        px, py = x ^ (m >> 1), y ^ (m & 1)
                        far.append(remote(s.at[2 * px + py], d.at[chip], send_sems, recv_sems, a, m, (px, py, c)))
                    near.append(pltpu.make_async_copy(s.at[chip], d.at[chip], local_sems.at[a]))
            at += len(arrs)
        return far, near

    def start():
        far, near = copies()
        for cp in near + far:
            cp.start()

    def finish():
        far, near = copies()
        for cp in far:
            cp.wait_recv()
        for cp in far:
            cp.wait_send()
        for cp in near:
            cp.wait()

    return start, finish


def _pair_sum(mine, theirs, name, groups=()):
    n = len(mine)
    carried = _carry_specs(groups)
    n_c = carried["n"]
    core = lax.axis_index("c").astype(jnp.int32).reshape(1)

    def body(core_ref, *refs):
        del core_ref
        own = refs[:n]
        sib = refs[n : 2 * n]
        src = refs[2 * n : 2 * n + n_c]
        out = refs[2 * n + n_c : 3 * n + n_c]
        landed = refs[3 * n + n_c : 3 * n + 2 * n_c]
        start, finish = _carry(groups, src, landed, refs[3 * n + 2 * n_c :])
        b = pl.program_id(0)

        @pl.when(b == 0)
        def _():
            start()

        for a in range(n):
            out[a][...] = (own[a][...].astype(F32) + sib[a][...].astype(F32)).astype(out[a].dtype)

        @pl.when(b == N_CHIPS - 1)
        def _():
            finish()

    block = lambda a: pl.BlockSpec((None,) + a.shape[1:], lambda b, core_ref: (b, 0, 0))
    res = pl.pallas_call(
        body,
        name=name,
        grid_spec=pltpu.PrefetchScalarGridSpec(
            num_scalar_prefetch=1,
            grid=(N_CHIPS,),
            in_specs=[
                pl.BlockSpec((None, None) + a.shape[2:], lambda b, core_ref: (0, 0, 0, 0))
                if a.shape[0] == 1
                else pl.BlockSpec((None, None) + a.shape[2:], lambda b, core_ref: (b, core_ref[0], 0, 0))
                for a in mine
            ]
            + [block(a) for a in theirs]
            + carried["in_specs"],
            out_specs=[block(a) for a in theirs] + carried["out_specs"],
            scratch_shapes=carried["scratch_shapes"],
        ),
        out_shape=[jax.ShapeDtypeStruct(a.shape, a.dtype) for a in theirs] + carried["out_shape"],
        compiler_params=_params("arbitrary"),
    )(core, *mine, *theirs, *carried["operands"])
    return res[:n], res[n:]


def _weight_grad(a, b, n_blk, blocked, name, groups):
    s = a.shape[0]
    t = min(TILE_WEIGHT_GRAD, s)
    n_t = s // t
    n_pairs = n_blk // 2
    if blocked == "cols":
        k, c = a.shape[1], b.shape[1] // n_blk
        a_spec = pl.BlockSpec((t, k), lambda j, i: (i, 0))
        b_spec = pl.BlockSpec((t, 2 * c), lambda j, i: (i, j))
        acc_shape = (k, 2 * c)
    else:
        k, c = a.shape[1] // n_blk, b.shape[1]
        a_spec = pl.BlockSpec((t, 2 * k), lambda j, i: (i, j))
        b_spec = pl.BlockSpec((t, c), lambda j, i: (i, 0))
        acc_shape = (2 * k, c)
    carried = _carry_specs(groups)
    n_p = carried["n"]

    def body(a_ref, b_ref, *refs):
        src = refs[:n_p]
        o_ref, theirs_ref = refs[n_p : n_p + 2]
        landed = refs[n_p + 2 : 2 * n_p + 2]
        acc_ref, sbuf, pair_send, pair_recv = refs[2 * n_p + 2 : 2 * n_p + 6]
        start, finish = _carry(groups, src, landed, refs[2 * n_p + 6 :])
        j = pl.program_id(0)
        i = pl.program_id(1)
        x, y, c_me = _position()

        def to_sibling(pair):
            return pltpu.make_async_remote_copy(
                src_ref=sbuf.at[1 - c_me], dst_ref=theirs_ref.at[pair], send_sem=pair_send.at[pair],
                recv_sem=pair_recv.at[pair], device_id=(x, y, 1 - c_me), device_id_type=MESH,
            )

        @pl.when(jnp.logical_and(j == 0, i == 0))
        def _():
            start()

        @pl.when(i == 0)
        def _():
            acc_ref[...] = jnp.zeros_like(acc_ref)

        acc_ref[...] += _dot(a_ref[...], b_ref[...], TN)

        @pl.when(i == n_t - 1)
        def _():
            for pair in range(1, n_pairs):

                @pl.when(j == pair)
                def _():
                    to_sibling(pair - 1).wait_send()

            for half in range(2):
                if blocked == "cols":
                    block = acc_ref[:, half * c : (half + 1) * c].astype(BF16)
                else:
                    block = acc_ref[half * k : (half + 1) * k, :].astype(BF16)
                o_ref[half] = block
                sbuf[half] = block
            for pair in range(n_pairs):

                @pl.when(j == pair)
                def _():
                    to_sibling(pair).start()

        @pl.when(jnp.logical_and(j == n_pairs - 1, i == n_t - 1))
        def _():
            to_sibling(n_pairs - 1).wait_send()
            for pair in range(n_pairs):
                to_sibling(pair).wait_recv()
            finish()

    res = pl.pallas_call(
        body,
        name=name,
        grid=(n_pairs, n_t),
        in_specs=[a_spec, b_spec] + carried["in_specs"],
        out_specs=[pl.BlockSpec((2, k, c), lambda j, i: (j, 0, 0)), ANY] + carried["out_specs"],
        out_shape=[jax.ShapeDtypeStruct((n_blk, k, c), BF16), jax.ShapeDtypeStruct((n_pairs, k, c), BF16)]
        + carried["out_shape"],
        scratch_shapes=[
            pltpu.VMEM(acc_shape, F32), pltpu.VMEM((2, k, c), BF16), pltpu.SemaphoreType.DMA((n_pairs,)),
            pltpu.SemaphoreType.DMA((n_pairs,)),
        ]
        + carried["scratch_shapes"],
        compiler_params=_params("arbitrary", "arbitrary", vmem_limit_bytes=VMEM_LIMIT_MIX_BYTES),
    )(a, b, *carried["operands"])
    return res[0], res[1], res[2:]


def _input_grad(dproj, win_t, x, dxo, norm_pre, groups):
    s, d = x.shape
    t = min(TILE_GRAD, s)
    n_t = s // t
    kb = 2 * WIN_BLK
    n_k = win_t.shape[0] // kb
    carried = _carry_specs(groups)
    n_p = carried["n"]

    def body(dp_ref, w_ref, x_ref, dxo_ref, g_ref, *refs):
        src = refs[:n_p]
        gx_ref, dg_all = refs[n_p : n_p + 2]
        landed = refs[n_p + 2 : 2 * n_p + 2]
        acc_ref, dg_ref, dg_send, dg_recv, dg_local = refs[2 * n_p + 2 : 2 * n_p + 7]
        start, finish = _carry(groups, src, landed, refs[2 * n_p + 7 :])
        i = pl.program_id(0)
        j = pl.program_id(1)
        px, py, pc = _position()
        me = 4 * px + 2 * py + pc

        def dg_copies():
            far = [
                pltpu.make_async_remote_copy(
                    src_ref=dg_ref, dst_ref=dg_all.at[me], send_sem=dg_send.at[m], recv_sem=dg_recv.at[m],
                    device_id=(px ^ ((m >> 2) & 1), py ^ ((m >> 1) & 1), pc ^ (m & 1)), device_id_type=MESH,
                )
                for m in range(1, N_DEV)
            ]
            return far, pltpu.make_async_copy(dg_ref, dg_all.at[me], dg_local)

        @pl.when(jnp.logical_and(i == 0, j == 0))
        def _():
            start()
            dg_ref[...] = jnp.zeros_like(dg_ref)

        @pl.when(j == 0)
        def _():
            acc_ref[...] = jnp.zeros_like(acc_ref)

        acc_ref[...] += _dot(dp_ref[...], w_ref[...])

        @pl.when(j == n_k - 1)
        def _():
            xv = x_ref[...]
            gain = g_ref[...]
            r = lax.rsqrt(_rowmean(xv * xv) + EPS)
            dgain, dx = _rms_bwd(acc_ref[...], gain, r, xv * r)
            dg_ref[...] += dgain
            gx_ref[...] = dxo_ref[...] + dx

        @pl.when(jnp.logical_and(i == n_t - 1, j == n_k - 1))
        def _():
            far, near = dg_copies()
            for cp in [near] + far:
                cp.start()
            finish()
            for cp in far:
                cp.wait_recv()
            for cp in far:
                cp.wait_send()
            near.wait()

    res = pl.pallas_call(
        body,
        name="input_grad",
        grid=(n_t, n_k),
        in_specs=[
            pl.BlockSpec((t, kb), lambda i, j: (i, j)),
            pl.BlockSpec((kb, d), lambda i, j: (j, 0)),
            pl.BlockSpec((t, d), lambda i, j: (i, 0)),
            pl.BlockSpec((t, d), lambda i, j: (i, 0)),
            _full((1, d)),
        ]
        + carried["in_specs"],
        out_specs=[pl.BlockSpec((t, d), lambda i, j: (i, 0)), ANY] + carried["out_specs"],
        out_shape=[jax.ShapeDtypeStruct((s, d), F32), jax.ShapeDtypeStruct((N_DEV, 1, d), F32)] + carried["out_shape"],
        scratch_shapes=[
            pltpu.VMEM((t, d), F32), pltpu.VMEM((1, d), F32), pltpu.SemaphoreType.DMA((N_DEV,)),
            pltpu.SemaphoreType.DMA((N_DEV,)), pltpu.SemaphoreType.DMA,
        ]
        + carried["scratch_shapes"],
        compiler_params=_params("arbitrary", "arbitrary", vmem_limit_bytes=VMEM_LIMIT_MIX_BYTES),
    )(dproj, win_t, x, dxo, norm_pre, *carried["operands"])
    return res[0], res[1], res[2:]


def _kv_backward(dkv, memn, wkv, mem):
    m, d = mem.shape
    n = wkv.shape[1]

    def body(dkv_ref, memn_ref, w_ref, mem_ref, gw_ref, dg_ref):
        dkv_bf = dkv_ref[...].astype(BF16)
        gw_ref[...] = _dot(memn_ref[...], dkv_bf, TN).astype(BF16).reshape(N_DEV, WKV_BLK, n)
        dmemn = _dot(dkv_bf, w_ref[...], NT)
        mv = mem_ref[...]
        r = lax.rsqrt(_rowmean(mv * mv) + EPS)
        dg_ref[...] = _colsum(dmemn * (mv * r))

    return pl.pallas_call(
        body,
        name="kv_backward",
        grid=(1,),
        in_specs=[_full((m, n)), _full((m, d)), _full(wkv.shape), _full((m, d))],
        out_specs=[_full((N_DEV, WKV_BLK, n)), _full((1, d))],
        out_shape=[jax.ShapeDtypeStruct((N_DEV, WKV_BLK, n), BF16), jax.ShapeDtypeStruct((1, d), F32)],
        compiler_params=_params("arbitrary"),
    )(dkv, memn, wkv, mem)


def _adamw_math(w, g, m, v):
    m = ADAM_B1 * m + (1.0 - ADAM_B1) * g
    v = ADAM_B2 * v + (1.0 - ADAM_B2) * (g * g)
    m_hat = m / (1.0 - ADAM_B1**ADAM_STEP)
    v_hat = v / (1.0 - ADAM_B2**ADAM_STEP)
    delta = -ADAM_LR * (m_hat / (jnp.sqrt(v_hat) + ADAM_EPS) + ADAM_WD * w)
    return delta, m, v


def _adamw(parts, w, m, v, name):
    r, c = w.shape
    slots = parts.shape[0]
    t = r
    while t * c * 4 > TILE_ADAM_BYTES and t % 16 == 0:
        t //= 2

    def body(p_ref, w_ref, m_ref, v_ref, g_ref, d_ref, nm_ref, nv_ref):
        g = p_ref[0].astype(F32)
        for k in range(1, slots):
            g = g + p_ref[k].astype(F32)
        delta, nm, nv = _adamw_math(w_ref[...], g, m_ref[...], v_ref[...])
        g_ref[...] = g
        d_ref[...] = delta
        nm_ref[...] = nm
        nv_ref[...] = nv

    tile = pl.BlockSpec((t, c), lambda i: (i, 0))
    return pl.pallas_call(
        body,
        name=name,
        grid=(r // t,),
        in_specs=[pl.BlockSpec((slots, t, c), lambda i: (0, i, 0)), tile, tile, tile],
        out_specs=[tile] * 4,
        out_shape=[jax.ShapeDtypeStruct((r, c), F32)] * 4,
        compiler_params=_params("parallel"),
    )(parts, w, m, v)


def _adamw_packed(parts, triples, name):
    slots = parts.shape[0]
    sizes = [w.shape[0] for w, _, _ in triples]
    rest = parts.shape[1] - sum(sizes)

    def total(p_ref, at, rows):
        g = p_ref[0, at : at + rows, :]
        for k in range(1, slots):
            g = g + p_ref[k, at : at + rows, :]
        return g

    def body(p_ref, *refs):
        ins = refs[: 3 * len(triples)]
        outs = refs[3 * len(triples) :]
        at = 0
        for n, rows in enumerate(sizes):
            g = total(p_ref, at, rows)
            w_ref, m_ref, v_ref = ins[3 * n : 3 * n + 3]
            delta, nm, nv = _adamw_math(w_ref[...], g, m_ref[...], v_ref[...])
            for ref, val in zip(outs[4 * n : 4 * n + 4], (g, delta, nm, nv)):
                ref[...] = val
            at += rows
        if rest:
            outs[-1][...] = total(p_ref, at, rest)

    flat = [a for t in triples for a in t]
    res = pl.pallas_call(
        body,
        name=name,
        out_shape=[jax.ShapeDtypeStruct(w.shape, F32) for w, _, _ in triples for _ in range(4)]
        + ([jax.ShapeDtypeStruct((rest, 128), F32)] if rest else []),
        compiler_params=pltpu.CompilerParams(vmem_limit_bytes=VMEM_LIMIT_BYTES),
    )(parts, *flat)
    return [res[4 * n : 4 * n + 4] for n in range(len(triples))], (res[-1] if rest else None)


SMALL = ("norm_pre", "pool_scale", "sgu_ln_g", "sgu_ln_b", "sgu_w", "sgu_b", "mem_norm", "branch_norm", "norm_post")


def _local_view(name, w):
    if name == "sgu_w":
        return w.reshape(SGU_HEADS, SGU_CHUNK, SGU_CHUNK)
    if name == "sgu_b":
        return w.reshape(SGU_HEADS, SGU_CHUNK)
    return w.reshape(1, -1)


def _forward_backward(x, mem, target, shards, small):
    causal = jnp.tril(jnp.ones((SGU_CHUNK, SGU_CHUNK), dtype=bool))
    sgu_wm = jnp.where(causal[None], small["sgu_w"], 0.0).astype(BF16)
    sgu_bias = jnp.repeat(jnp.transpose(small["sgu_b"]), SGU_HEAD_DIM, axis=1)

    proj, h, (win, wkv, pool_all, wout) = _proj_gather(x, small["norm_pre"], shards)
    wout = wout.reshape(MIX_WIDTH, D_MODEL)
    wkv = wkv.reshape(D_MODEL, 2 * BRANCH)
    pool_full = (
        pool_all.reshape(N_DEV, len(POOL_WINDOWS), POOL_BLK, POOL_GROUP_DIM)
        .transpose(1, 0, 2, 3)
        .reshape(len(POOL_WINDOWS), POOL_GROUP_DIM, POOL_GROUP_DIM)
    )
    memn, kv, kv_t = _kv_forward(mem, small["mem_norm"], wkv)
    (y, dout, dxo, dproj, loss, d_norm_post, d_branch_norm, d_pool_scale, d_ln_g, d_ln_b, d_pool_w, d_sgu_w, d_sgu_b,
     dkv) = _mix(
        proj, x, target, kv, kv_t, wout, wout.T, pool_full, jnp.swapaxes(pool_full, 1, 2), small["pool_scale"],
        small["sgu_ln_g"], small["sgu_ln_b"], sgu_bias, sgu_wm, jnp.swapaxes(sgu_wm, 1, 2), small["branch_norm"],
        small["norm_post"],
    )
    g_wkv, d_mem_norm = _kv_backward(dkv, memn, wkv, mem)
    g_pool = (
        d_pool_w.reshape(len(POOL_WINDOWS), N_DEV, POOL_BLK, POOL_GROUP_DIM)
        .transpose(1, 0, 2, 3)
        .reshape(N_DEV, len(POOL_WINDOWS) * POOL_BLK, POOL_GROUP_DIM)
        .astype(BF16)
    )
    small_grads = dict(
        pool_scale=d_pool_scale, sgu_ln_g=d_ln_g, sgu_ln_b=d_ln_b, sgu_w=d_sgu_w, sgu_b=d_sgu_b,
        mem_norm=d_mem_norm, branch_norm=d_branch_norm, norm_post=d_norm_post,
    )
    packed = jnp.concatenate(
        [small_grads[n].reshape(-1, LANES) for n in SMALL if n != "norm_pre"]
        + [jnp.broadcast_to(loss, (SUBLANES, LANES))],
        axis=0,
    )
    packed = packed[None, None]

    by_chip = lambda g: g.reshape((N_CHIPS, 2) + g.shape[1:])
    small_mine = [by_chip(g_wkv), by_chip(g_pool), packed]
    g_wout, wout_theirs, small_theirs = _weight_grad(y, dout, N_DEV, "rows", "grad_w_out", [("pair", small_mine)])
    sums, _ = _pair_sum(small_mine + [by_chip(g_wout)], list(small_theirs) + [wout_theirs], "pair_sum_first")
    g_win, win_theirs, (l_wkv, l_pool, l_packed, l_wout) = _weight_grad(
        h, dproj, N_DEV, "cols", "grad_w_in", [("chips", list(sums))]
    )
    (win_sums,), _ = _pair_sum([by_chip(g_win)], [win_theirs], "pair_sum_w_in")
    win_late, _ = lax.optimization_barrier((win, sums[0]))
    grad_x, d_norm_pre, (l_win,) = _input_grad(
        dproj, jnp.swapaxes(win_late, 1, 2).reshape(IN_WIDTH, D_MODEL), x, dxo, small["norm_pre"],
        [("chips", [win_sums])],
    )
    return grad_x, dict(w_in=l_win, w_out=l_wout, w_kv=l_wkv, pool_w=l_pool), l_packed, d_norm_pre


def kernel(x, mem, norm_pre, w_in, pool_w, pool_scale, sgu_ln_g, sgu_ln_b, sgu_w, sgu_b, mem_norm, w_kv, branch_norm, w_out, norm_post, loss_target, m_norm_pre, m_w_in, m_pool_w, m_pool_scale, m_sgu_ln_g, m_sgu_ln_b, m_sgu_w, m_sgu_b, m_mem_norm, m_w_kv, m_branch_norm, m_w_out, m_norm_post, v_norm_pre, v_w_in, v_pool_w, v_pool_scale, v_sgu_ln_g, v_sgu_ln_b, v_sgu_w, v_sgu_b, v_mem_norm, v_w_kv, v_branch_norm, v_w_out, v_norm_post):
    weights = dict(norm_pre=norm_pre, w_in=w_in, pool_w=pool_w, pool_scale=pool_scale, sgu_ln_g=sgu_ln_g, sgu_ln_b=sgu_ln_b, sgu_w=sgu_w, sgu_b=sgu_b, mem_norm=mem_norm, w_kv=w_kv, branch_norm=branch_norm, w_out=w_out, norm_post=norm_post)
    first = dict(norm_pre=m_norm_pre, w_in=m_w_in, pool_w=m_pool_w, pool_scale=m_pool_scale, sgu_ln_g=m_sgu_ln_g, sgu_ln_b=m_sgu_ln_b, sgu_w=m_sgu_w, sgu_b=m_sgu_b, mem_norm=m_mem_norm, w_kv=m_w_kv, branch_norm=m_branch_norm, w_out=m_w_out, norm_post=m_norm_post)
    second = dict(norm_pre=v_norm_pre, w_in=v_w_in, pool_w=v_pool_w, pool_scale=v_pool_scale, sgu_ln_g=v_sgu_ln_g, sgu_ln_b=v_sgu_ln_b, sgu_w=v_sgu_w, sgu_b=v_sgu_b, mem_norm=v_mem_norm, w_kv=v_w_kv, branch_norm=v_branch_norm, w_out=v_w_out, norm_post=v_norm_post)
    order = ("norm_pre", "w_in", "pool_w", "pool_scale", "sgu_ln_g", "sgu_ln_b", "sgu_w", "sgu_b", "mem_norm", "w_kv", "branch_norm", "w_out", "norm_post")

    owned_shape = dict(
        w_in=(D_MODEL, WIN_BLK), w_out=(WOUT_BLK, D_MODEL), w_kv=(WKV_BLK, 2 * BRANCH),
        pool_w=(len(POOL_WINDOWS) * POOL_BLK, POOL_GROUP_DIM),
    )
    owned = {n: weights[n].reshape(owned_shape[n]) for n in owned_shape}
    small = {n: _local_view(n, weights[n]) for n in SMALL}
    grad_x, landed, landed_packed, d_norm_pre = _forward_backward(
        x[0], mem[0], loss_target[0],
        [owned["w_in"].astype(BF16)] + [owned[n] for n in ("w_kv", "pool_w", "w_out")], small,
    )
    landed_norm_pre = d_norm_pre.reshape(N_DEV, -1, LANES)

    grads, deltas, new_m, new_v = {}, {}, {}, {}
    for n in owned_shape:
        shape = weights[n].shape
        res = _adamw(
            landed[n], owned[n], first[n].reshape(owned_shape[n]), second[n].reshape(owned_shape[n]), "adamw_" + n
        )
        grads[n], deltas[n], new_m[n], new_v[n] = (a.reshape(shape) for a in res)
    rows_of = lambda tree, n: tree[n].reshape(-1, 128)
    for names, parts, name in (
        ([n for n in SMALL if n != "norm_pre"], landed_packed, "adamw_replicated"),
        (["norm_pre"], landed_norm_pre, "adamw_norm_pre"),
    ):
        res, rest = _adamw_packed(
            parts, [(rows_of(weights, n), rows_of(first, n), rows_of(second, n)) for n in names], name
        )
        if rest is not None:
            total = rest[0, 0]
        for n, four in zip(names, res):
            for tree, a in zip((grads, deltas, new_m, new_v), four):
                tree[n] = a.reshape(weights[n].shape)

    return (
        total,
        grad_x[None],
        *[grads[n] for n in order],
        *[deltas[n] for n in order],
        *[new_m[n] for n in order],
        *[new_v[n] for n in order],
    )
```
